```python
import math
import jax
import jax.numpy as jnp
from jax import lax
import numpy as np

D_MODEL = 2048
BATCH = 8
SEQ = 8192
DEPTH = 1

CTX_LEN = 256
GRID_W = 64
SSM_EXPAND = 2
D_INNER = SSM_EXPAND * D_MODEL
SSM_HEAD_DIM = 64
SSM_HEADS = D_INNER // SSM_HEAD_DIM
SSM_GROUPS = 8
HEADS_PER_GROUP = SSM_HEADS // SSM_GROUPS
SSM_STATE = 128
SSM_CONV = 5
CHUNK = 128
CONV_DIM = D_INNER + 2 * SSM_GROUPS * SSM_STATE
D_CF = D_MODEL
CF_KERNEL = 31
D_FF = -(-8 * D_MODEL // (3 * 256)) * 256
EPS = 1e-6
OFF_Z = D_INNER
OFF_XBC = OFF_Z + CONV_DIM
OFF_DT = OFF_XBC + 2 * SSM_HEADS
OFF_GLU = OFF_DT + 2 * D_CF
D_IN_TOTAL = OFF_GLU + 2 * D_MODEL

kernel_name = "hybrid_ssd_conformer_dit_block"


def rmsnorm(x, w):
    xf = x.astype(jnp.float32)
    y = xf * lax.rsqrt(jnp.mean(xf * xf, axis=-1, keepdims=True) + EPS)
    return (y * w.astype(jnp.float32)).astype(x.dtype)


def layernorm(x, g, b):
    xf = x.astype(jnp.float32)
    mu = jnp.mean(xf, axis=-1, keepdims=True)
    xc = xf - mu
    var = jnp.mean(xc * xc, axis=-1, keepdims=True)
    y = xc * lax.rsqrt(var + EPS) * g.astype(jnp.float32) + b.astype(jnp.float32)
    return y.astype(x.dtype)


def ada_params(cvec, w_mod, b_mod):
    m = jax.nn.silu(cvec) @ w_mod + b_mod
    return jnp.split(m, 6, axis=-1)


def modulate(h, shift, scale):
    return h * (1 + scale) + shift


def dwconv_seq(u, w, bias):
    k = w.shape[0]
    out = lax.conv_general_dilated(
        u, w[:, None, :].astype(u.dtype), window_strides=(1,), padding=[(k // 2, k // 2)],
        dimension_numbers=("NWC", "WIO", "NWC"), feature_group_count=u.shape[-1])
    return out + bias.astype(u.dtype)


def dwconv_grid_columns(u, w, bias, rows):
    b, L, C = u.shape
    k = w.shape[0]
    ug = u.reshape(b, rows, GRID_W, C)
    out = lax.conv_general_dilated(
        ug, w[:, None, None, :].astype(u.dtype), window_strides=(1, 1),
        padding=[(k // 2, k // 2), (0, 0)],
        dimension_numbers=("NHWC", "HWIO", "NHWC"), feature_group_count=C)
    return out.reshape(b, L, C) + bias.astype(u.dtype)


def ssd_chunked(xs, bm, cm, dt, a, h0):
    b, L = xs.shape[:2]
    nc = L // CHUNK

    def to_chunks(t):
        return jnp.moveaxis(t.reshape(b, nc, CHUNK, *t.shape[2:]), 1, 0)

    xdt = (xs.astype(jnp.float32) * dt[..., None]).reshape(
        b, L, SSM_GROUPS, HEADS_PER_GROUP, SSM_HEAD_DIM)
    da = (dt * a).reshape(b, L, SSM_GROUPS, HEADS_PER_GROUP)
    lower = jnp.tril(jnp.ones((CHUNK, CHUNK), dtype=bool))[None, :, :, None, None]

    def step(h, inp):
        xdt_c, b_c, c_c, da_c = inp
        cs = jnp.cumsum(da_c, axis=1)
        seg = cs[:, :, None] - cs[:, None, :]
        decay = jnp.exp(jnp.where(lower, seg, -jnp.inf))
        cb = jnp.einsum("blgn,bsgn->blsg", c_c, b_c)
        y_diag = jnp.einsum("blsgj,bsgjp->blgjp", cb[..., None] * decay, xdt_c)
        y_off = jnp.einsum("blgn,bgjpn->blgjp", c_c, h) * jnp.exp(cs)[..., None]
        to_end = jnp.exp(cs[:, -1:] - cs)
        h_new = h * jnp.exp(cs[:, -1])[..., None, None] + jnp.einsum(
            "bsgn,bsgjp->bgjpn", b_c, xdt_c * to_end[..., None])
        return h_new, y_diag + y_off

    h_last, ys = lax.scan(step, h0, (to_chunks(xdt), to_chunks(bm.astype(jnp.float32)),
                                     to_chunks(cm.astype(jnp.float32)), to_chunks(da)))
    y = jnp.moveaxis(ys, 0, 1).reshape(b, L, SSM_HEADS, SSM_HEAD_DIM)
    return y, h_last


def flip_seq(t):
    return jnp.flip(t, axis=1)


def mixer_front(h, w_in, conv_w, conv_b, dt_bias, a_log, h0_f, h0_b):
    b, L, _ = h.shape
    proj = h @ w_in
    z, xbc, dt_raw, glu, gates = jnp.split(proj, [OFF_Z, OFF_XBC, OFF_DT, OFF_GLU], axis=-1)
    xbc = jax.nn.silu(dwconv_seq(xbc, conv_w, conv_b))
    xs, bm, cm = jnp.split(xbc, [D_INNER, D_INNER + SSM_GROUPS * SSM_STATE], axis=-1)
    xs = xs.reshape(b, L, SSM_HEADS, SSM_HEAD_DIM)
    bm = bm.reshape(b, L, SSM_GROUPS, SSM_STATE)
    cm = cm.reshape(b, L, SSM_GROUPS, SSM_STATE)
    dt = jax.nn.softplus(dt_raw.astype(jnp.float32).reshape(b, L, 2, SSM_HEADS)
                         + dt_bias.astype(jnp.float32))
    a = -jnp.exp(a_log.astype(jnp.float32))
    y_f, h_f = ssd_chunked(xs, bm, cm, dt[:, :, 0], a[0], h0_f)
    y_b, h_b = ssd_chunked(flip_seq(xs), flip_seq(bm), flip_seq(cm), flip_seq(dt[:, :, 1]), a[1], h0_b)
    y = y_f + flip_seq(y_b)
    return (z, xs, y, glu, gates), h_f, h_b


def mixer_back(parts, d_skip, ssm_norm, cf_conv, cf_ln_g, cf_ln_b, w_proj_a, w_proj_b, w_out):
    z, xs, y, glu, gates = parts
    b, L = z.shape[:2]
    y = (y + d_skip.astype(jnp.float32)[:, None] * xs.astype(jnp.float32)).reshape(
        b, L, D_INNER).astype(z.dtype)
    y_a = rmsnorm(y * jax.nn.silu(z), ssm_norm) @ w_proj_a
    u, v = jnp.split(glu, 2, axis=-1)
    cf = jax.nn.silu(layernorm(cf_conv(u * jax.nn.sigmoid(v)), cf_ln_g, cf_ln_b))
    y_b = cf @ w_proj_b
    g_a, g_b = jnp.split(gates, 2, axis=-1)
    return (jax.nn.sigmoid(g_a) * y_a + jax.nn.sigmoid(g_b) * y_b) @ w_out


def swiglu(h, w_gate, w_up, w_down):
    return (jax.nn.silu(h @ w_gate) * (h @ w_up)) @ w_down


def _fwd_setup_inputs(seed: int = 0) -> dict:
    key = jax.random.key(seed)
    ks = jax.random.split(key, 26)

    def nrm(k, shape, std):
        return jax.random.normal(k, shape, jnp.float32) * std

    x = nrm(ks[0], (BATCH, SEQ, D_MODEL), 1.0)
    c = nrm(ks[1], (BATCH, D_MODEL), 1.0)
    ctx = nrm(ks[2], (BATCH, CTX_LEN, D_MODEL), 1.0)
    c_ctx = nrm(ks[3], (D_MODEL,), 1.0)
    w_mod = nrm(ks[4], (DEPTH, D_MODEL, 6 * D_MODEL), 0.5 * D_MODEL ** -0.5)
    b_mod = nrm(ks[5], (DEPTH, 6 * D_MODEL), 0.02)
    norm_mix = 1.0 + nrm(ks[6], (DEPTH, D_MODEL), 0.02)
    w_in = nrm(ks[7], (DEPTH, D_MODEL, D_IN_TOTAL), D_MODEL ** -0.5)
    ssm_conv_w = nrm(ks[8], (DEPTH, SSM_CONV, CONV_DIM), SSM_CONV ** -0.5)
    ssm_conv_b = nrm(ks[9], (DEPTH, CONV_DIM), 0.02)
    dt0 = jnp.exp(jax.random.uniform(ks[10], (DEPTH, 2, SSM_HEADS), jnp.float32,
                                     minval=math.log(1e-3), maxval=math.log(1e-1)))
    dt_bias = dt0 + jnp.log(-jnp.expm1(-dt0))
    a_log = jnp.log(jax.random.uniform(ks[11], (DEPTH, 2, SSM_HEADS), jnp.float32,
                                       minval=1.0, maxval=16.0))
    d_skip = 1.0 + nrm(ks[12], (DEPTH, SSM_HEADS), 0.1)
    ssm_norm = 1.0 + nrm(ks[13], (DEPTH, D_INNER), 0.02)
    cf_conv_w = nrm(ks[14], (DEPTH, CF_KERNEL, D_CF), CF_KERNEL ** -0.5)
    cf_conv_b = nrm(ks[15], (DEPTH, D_CF), 0.02)
    cf_ln_g = 1.0 + nrm(ks[16], (DEPTH, D_CF), 0.02)
    cf_ln_b = nrm(ks[17], (DEPTH, D_CF), 0.02)
    w_proj_a = nrm(ks[18], (DEPTH, D_INNER, D_MODEL), D_INNER ** -0.5)
    w_proj_b = nrm(ks[19], (DEPTH, D_CF, D_MODEL), D_CF ** -0.5)
    w_out = nrm(ks[20], (DEPTH, D_MODEL, D_MODEL), D_MODEL ** -0.5)
    norm_ffn = 1.0 + nrm(ks[21], (DEPTH, D_MODEL), 0.02)
    w_ffn_gate = nrm(ks[22], (DEPTH, D_MODEL, D_FF), D_MODEL ** -0.5)
    w_ffn_up = nrm(ks[23], (DEPTH, D_MODEL, D_FF), D_MODEL ** -0.5)
    w_ffn_down = nrm(ks[24], (DEPTH, D_FF, D_MODEL), D_FF ** -0.5)
    norm_final = 1.0 + nrm(ks[25], (D_MODEL,), 0.02)
    return {"x": x, "c": c, "ctx": ctx, "c_ctx": c_ctx, "w_mod": w_mod, "b_mod": b_mod,
            "norm_mix": norm_mix, "w_in": w_in, "ssm_conv_w": ssm_conv_w, "ssm_conv_b": ssm_conv_b,
            "dt_bias": dt_bias, "a_log": a_log, "d_skip": d_skip, "ssm_norm": ssm_norm,
            "cf_conv_w": cf_conv_w, "cf_conv_b": cf_conv_b, "cf_ln_g": cf_ln_g, "cf_ln_b": cf_ln_b,
            "w_proj_a": w_proj_a, "w_proj_b": w_proj_b, "w_out": w_out, "norm_ffn": norm_ffn,
            "w_ffn_gate": w_ffn_gate, "w_ffn_up": w_ffn_up, "w_ffn_down": w_ffn_down,
            "norm_final": norm_final}


def _fwd_reference(x, c, ctx, c_ctx, w_mod, b_mod, norm_mix, w_in, ssm_conv_w, ssm_conv_b, dt_bias,
              a_log, d_skip, ssm_norm, cf_conv_w, cf_conv_b, cf_ln_g, cf_ln_b, w_proj_a, w_proj_b,
              w_out, norm_ffn, w_ffn_gate, w_ffn_up, w_ffn_down, norm_final):
    b = x.shape[0]
    rows = x.shape[1] // GRID_W
    zero_state = jnp.zeros((b, SSM_GROUPS, HEADS_PER_GROUP, SSM_HEAD_DIM, SSM_STATE), jnp.float32)
    for l in range(DEPTH):
        sh1, sc1, g1, sh2, sc2, g2 = [t[:, None, :] for t in ada_params(c, w_mod[l], b_mod[l])]
        csh1, csc1, cg1, csh2, csc2, cg2 = ada_params(c_ctx, w_mod[l], b_mod[l])

        hc = modulate(rmsnorm(ctx, norm_mix[l]), csh1, csc1)
        ctx_parts, h_f, h_b = mixer_front(hc, w_in[l], ssm_conv_w[l], ssm_conv_b[l], dt_bias[l],
                                          a_log[l], zero_state, zero_state)

        hx = modulate(rmsnorm(x, norm_mix[l]), sh1, sc1)
        x_parts, _, _ = mixer_front(hx, w_in[l], ssm_conv_w[l], ssm_conv_b[l], dt_bias[l],
                                    a_log[l], h_f, h_b)
        x = x + g1 * mixer_back(
            x_parts, d_skip[l], ssm_norm[l],
            lambda t: dwconv_grid_columns(t, cf_conv_w[l], cf_conv_b[l], rows),
            cf_ln_g[l], cf_ln_b[l], w_proj_a[l], w_proj_b[l], w_out[l])
        hx = modulate(rmsnorm(x, norm_ffn[l]), sh2, sc2)
        x = x + g2 * swiglu(hx, w_ffn_gate[l], w_ffn_up[l], w_ffn_down[l])

        if l + 1 < DEPTH:
            ctx = ctx + cg1 * mixer_back(
                ctx_parts, d_skip[l], ssm_norm[l],
                lambda t: dwconv_seq(t, cf_conv_w[l], cf_conv_b[l]),
                cf_ln_g[l], cf_ln_b[l], w_proj_a[l], w_proj_b[l], w_out[l])
            hc = modulate(rmsnorm(ctx, norm_ffn[l]), csh2, csc2)
            ctx = ctx + cg2 * swiglu(hc, w_ffn_gate[l], w_ffn_up[l], w_ffn_down[l])
    return rmsnorm(x, norm_final)


import jax as _jax
import jax.numpy as _jnp

TWIN_FORMAT = 'train_step'
FWD_PARAMS = ['x', 'c', 'ctx', 'c_ctx', 'w_mod', 'b_mod', 'norm_mix', 'w_in', 'ssm_conv_w', 'ssm_conv_b', 'dt_bias', 'a_log', 'd_skip', 'ssm_norm', 'cf_conv_w', 'cf_conv_b', 'cf_ln_g', 'cf_ln_b', 'w_proj_a', 'w_proj_b', 'w_out', 'norm_ffn', 'w_ffn_gate', 'w_ffn_up', 'w_ffn_down', 'norm_final']
TWIN_WEIGHTS = ['c_ctx', 'w_mod', 'b_mod', 'norm_mix', 'w_in', 'ssm_conv_w', 'ssm_conv_b', 'dt_bias', 'a_log', 'd_skip', 'ssm_norm', 'cf_conv_w', 'cf_conv_b', 'cf_ln_g', 'cf_ln_b', 'w_proj_a', 'w_proj_b', 'w_out', 'norm_ffn', 'w_ffn_gate', 'w_ffn_up', 'w_ffn_down', 'norm_final']
TWIN_DIFF_INPUT = 'x'
TWIN_INPUTS = ['x', 'c', 'ctx', 'c_ctx', 'w_mod', 'b_mod', 'norm_mix', 'w_in', 'ssm_conv_w', 'ssm_conv_b', 'dt_bias', 'a_log', 'd_skip', 'ssm_norm', 'cf_conv_w', 'cf_conv_b', 'cf_ln_g', 'cf_ln_b', 'w_proj_a', 'w_proj_b', 'w_out', 'norm_ffn', 'w_ffn_gate', 'w_ffn_up', 'w_ffn_down', 'norm_final', 'loss_target', 'm_c_ctx', 'm_w_mod', 'm_b_mod', 'm_norm_mix', 'm_w_in', 'm_ssm_conv_w', 'm_ssm_conv_b', 'm_dt_bias', 'm_a_log', 'm_d_skip', 'm_ssm_norm', 'm_cf_conv_w', 'm_cf_conv_b', 'm_cf_ln_g', 'm_cf_ln_b', 'm_w_proj_a', 'm_w_proj_b', 'm_w_out', 'm_norm_ffn', 'm_w_ffn_gate', 'm_w_ffn_up', 'm_w_ffn_down', 'm_norm_final', 'v_c_ctx', 'v_w_mod', 'v_b_mod', 'v_norm_mix', 'v_w_in', 'v_ssm_conv_w', 'v_ssm_conv_b', 'v_dt_bias', 'v_a_log', 'v_d_skip', 'v_ssm_norm', 'v_cf_conv_w', 'v_cf_conv_b', 'v_cf_ln_g', 'v_cf_ln_b', 'v_w_proj_a', 'v_w_proj_b', 'v_w_out', 'v_norm_ffn', 'v_w_ffn_gate', 'v_w_ffn_up', 'v_w_ffn_down', 'v_norm_final']
TWIN_OUTPUTS = ['loss', 'grad_x', 'grad_c_ctx', 'grad_w_mod', 'grad_b_mod', 'grad_norm_mix', 'grad_w_in', 'grad_ssm_conv_w', 'grad_ssm_conv_b', 'grad_dt_bias', 'grad_a_log', 'grad_d_skip', 'grad_ssm_norm', 'grad_cf_conv_w', 'grad_cf_conv_b', 'grad_cf_ln_g', 'grad_cf_ln_b', 'grad_w_proj_a', 'grad_w_proj_b', 'grad_w_out', 'grad_norm_ffn', 'grad_w_ffn_gate', 'grad_w_ffn_up', 'grad_w_ffn_down', 'grad_norm_final', 'delta_c_ctx', 'delta_w_mod', 'delta_b_mod', 'delta_norm_mix', 'delta_w_in', 'delta_ssm_conv_w', 'delta_ssm_conv_b', 'delta_dt_bias', 'delta_a_log', 'delta_d_skip', 'delta_ssm_norm', 'delta_cf_conv_w', 'delta_cf_conv_b', 'delta_cf_ln_g', 'delta_cf_ln_b', 'delta_w_proj_a', 'delta_w_proj_b', 'delta_w_out', 'delta_norm_ffn', 'delta_w_ffn_gate', 'delta_w_ffn_up', 'delta_w_ffn_down', 'delta_norm_final', 'new_m_c_ctx', 'new_m_w_mod', 'new_m_b_mod', 'new_m_norm_mix', 'new_m_w_in', 'new_m_ssm_conv_w', 'new_m_ssm_conv_b', 'new_m_dt_bias', 'new_m_a_log', 'new_m_d_skip', 'new_m_ssm_norm', 'new_m_cf_conv_w', 'new_m_cf_conv_b', 'new_m_cf_ln_g', 'new_m_cf_ln_b', 'new_m_w_proj_a', 'new_m_w_proj_b', 'new_m_w_out', 'new_m_norm_ffn', 'new_m_w_ffn_gate', 'new_m_w_ffn_up', 'new_m_w_ffn_down', 'new_m_norm_final', 'new_v_c_ctx', 'new_v_w_mod', 'new_v_b_mod', 'new_v_norm_mix', 'new_v_w_in', 'new_v_ssm_conv_w', 'new_v_ssm_conv_b', 'new_v_dt_bias', 'new_v_a_log', 'new_v_d_skip', 'new_v_ssm_norm', 'new_v_cf_conv_w', 'new_v_cf_conv_b', 'new_v_cf_ln_g', 'new_v_cf_ln_b', 'new_v_w_proj_a', 'new_v_w_proj_b', 'new_v_w_out', 'new_v_norm_ffn', 'new_v_w_ffn_gate', 'new_v_w_ffn_up', 'new_v_w_ffn_down', 'new_v_norm_final']
TWIN_LEAF_KINDS = {'loss': 'loss', 'grad_x': 'grad_x', 'grad_c_ctx': 'grad_w', 'grad_w_mod': 'grad_w', 'grad_b_mod': 'grad_w', 'grad_norm_mix': 'grad_w', 'grad_w_in': 'grad_w', 'grad_ssm_conv_w': 'grad_w', 'grad_ssm_conv_b': 'grad_w', 'grad_dt_bias': 'grad_w', 'grad_a_log': 'grad_w', 'grad_d_skip': 'grad_w', 'grad_ssm_norm': 'grad_w', 'grad_cf_conv_w': 'grad_w', 'grad_cf_conv_b': 'grad_w', 'grad_cf_ln_g': 'grad_w', 'grad_cf_ln_b': 'grad_w', 'grad_w_proj_a': 'grad_w', 'grad_w_proj_b': 'grad_w', 'grad_w_out': 'grad_w', 'grad_norm_ffn': 'grad_w', 'grad_w_ffn_gate': 'grad_w', 'grad_w_ffn_up': 'grad_w', 'grad_w_ffn_down': 'grad_w', 'grad_norm_final': 'grad_w', 'delta_c_ctx': 'delta_w', 'delta_w_mod': 'delta_w', 'delta_b_mod': 'delta_w', 'delta_norm_mix': 'delta_w', 'delta_w_in': 'delta_w', 'delta_ssm_conv_w': 'delta_w', 'delta_ssm_conv_b': 'delta_w', 'delta_dt_bias': 'delta_w', 'delta_a_log': 'delta_w', 'delta_d_skip': 'delta_w', 'delta_ssm_norm': 'delta_w', 'delta_cf_conv_w': 'delta_w', 'delta_cf_conv_b': 'delta_w', 'delta_cf_ln_g': 'delta_w', 'delta_cf_ln_b': 'delta_w', 'delta_w_proj_a': 'delta_w', 'delta_w_proj_b': 'delta_w', 'delta_w_out': 'delta_w', 'delta_norm_ffn': 'delta_w', 'delta_w_ffn_gate': 'delta_w', 'delta_w_ffn_up': 'delta_w', 'delta_w_ffn_down': 'delta_w', 'delta_norm_final': 'delta_w', 'new_m_c_ctx': 'new_m', 'new_m_w_mod': 'new_m', 'new_m_b_mod': 'new_m', 'new_m_norm_mix': 'new_m', 'new_m_w_in': 'new_m', 'new_m_ssm_conv_w': 'new_m', 'new_m_ssm_conv_b': 'new_m', 'new_m_dt_bias': 'new_m', 'new_m_a_log': 'new_m', 'new_m_d_skip': 'new_m', 'new_m_ssm_norm': 'new_m', 'new_m_cf_conv_w': 'new_m', 'new_m_cf_conv_b': 'new_m', 'new_m_cf_ln_g': 'new_m', 'new_m_cf_ln_b': 'new_m', 'new_m_w_proj_a': 'new_m', 'new_m_w_proj_b': 'new_m', 'new_m_w_out': 'new_m', 'new_m_norm_ffn': 'new_m', 'new_m_w_ffn_gate': 'new_m', 'new_m_w_ffn_up': 'new_m', 'new_m_w_ffn_down': 'new_m', 'new_m_norm_final': 'new_m', 'new_v_c_ctx': 'new_v', 'new_v_w_mod': 'new_v', 'new_v_b_mod': 'new_v', 'new_v_norm_mix': 'new_v', 'new_v_w_in': 'new_v', 'new_v_ssm_conv_w': 'new_v', 'new_v_ssm_conv_b': 'new_v', 'new_v_dt_bias': 'new_v', 'new_v_a_log': 'new_v', 'new_v_d_skip': 'new_v', 'new_v_ssm_norm': 'new_v', 'new_v_cf_conv_w': 'new_v', 'new_v_cf_conv_b': 'new_v', 'new_v_cf_ln_g': 'new_v', 'new_v_cf_ln_b': 'new_v', 'new_v_w_proj_a': 'new_v', 'new_v_w_proj_b': 'new_v', 'new_v_w_out': 'new_v', 'new_v_norm_ffn': 'new_v', 'new_v_w_ffn_gate': 'new_v', 'new_v_w_ffn_up': 'new_v', 'new_v_w_ffn_down': 'new_v', 'new_v_norm_final': 'new_v'}


def _forward(args):
    return _fwd_reference(*[args[k] for k in FWD_PARAMS])


def _output_shape():
    def fwd():
        inp = _fwd_setup_inputs(0)
        return _fwd_reference(*[inp[k] for k in FWD_PARAMS])
    out = _jax.eval_shape(fwd)
    return out.shape, out.dtype

N_MICROBATCH = 1
ADAM_LR = 0.001
ADAM_B1 = 0.9
ADAM_B2 = 0.999
ADAM_EPS = 1e-08
ADAM_WD = 0.01
ADAM_STEP = 10
PER_EXAMPLE_BATCH_AXIS = {'x': 0, 'c': 0, 'ctx': 0, 'loss_target': 0}
SHARED_INPUTS = []
_WEIGHT_DTYPES = {'c_ctx': _jnp.float32, 'w_mod': _jnp.float32, 'b_mod': _jnp.float32, 'norm_mix': _jnp.float32, 'w_in': _jnp.float32, 'ssm_conv_w': _jnp.float32, 'ssm_conv_b': _jnp.float32, 'dt_bias': _jnp.float32, 'a_log': _jnp.float32, 'd_skip': _jnp.float32, 'ssm_norm': _jnp.float32, 'cf_conv_w': _jnp.float32, 'cf_conv_b': _jnp.float32, 'cf_ln_g': _jnp.float32, 'cf_ln_b': _jnp.float32, 'w_proj_a': _jnp.float32, 'w_proj_b': _jnp.float32, 'w_out': _jnp.float32, 'norm_ffn': _jnp.float32, 'w_ffn_gate': _jnp.float32, 'w_ffn_up': _jnp.float32, 'w_ffn_down': _jnp.float32, 'norm_final': _jnp.float32}
MOMENT_SCALE = {'c_ctx': 1.704006e-03, 'w_mod': 3.529983e-02, 'b_mod': 6.094988e-02, 'norm_mix': 3.388857e-02, 'w_in': 1.158766e-02, 'ssm_conv_w': 1.232462e-02, 'ssm_conv_b': 1.768197e-02, 'dt_bias': 2.906446e-02, 'a_log': 3.951010e-02, 'd_skip': 4.439543e-02, 'ssm_norm': 1.495162e-02, 'cf_conv_w': 1.290225e-02, 'cf_conv_b': 5.707164e-02, 'cf_ln_g': 1.494938e-02, 'cf_ln_b': 1.360404e-02, 'w_proj_a': 2.036554e-02, 'w_proj_b': 1.252745e-02, 'w_out': 2.386257e-02, 'norm_ffn': 3.597264e-02, 'w_ffn_gate': 1.603749e-02, 'w_ffn_up': 1.554384e-02, 'w_ffn_down': 2.573539e-02, 'norm_final': 3.195931e+01}


def _to_microbatches(a, axis):
    t = _jnp.moveaxis(a, axis, 0)
    t = t.reshape((N_MICROBATCH, t.shape[0] // N_MICROBATCH) + t.shape[1:])
    return _jnp.moveaxis(t, 1, axis + 1)


def setup_inputs(seed: int = 0) -> dict:
    inp = _fwd_setup_inputs(seed)
    key = _jax.random.fold_in(_jax.random.key(seed), 7919)
    shape, _ = _output_shape()
    out = dict(inp)
    out["loss_target"] = _jax.random.normal(_jax.random.fold_in(key, 0), shape, _jnp.float32)
    for i, name in enumerate(TWIN_WEIGHTS):
        w = inp[name].astype(_jnp.float32)
        if MOMENT_SCALE is None:
            s = _jnp.sqrt(_jnp.mean(_jnp.square(w)) + 1e-30)
        else:
            s = MOMENT_SCALE[name]
        km, kv = _jax.random.split(_jax.random.fold_in(key, i + 1))
        out[name] = w
        out["m_" + name] = s * _jax.random.normal(km, w.shape, _jnp.float32)
        out["v_" + name] = (s * s) * _jax.random.uniform(kv, w.shape, _jnp.float32, 0.5, 1.5)
    if N_MICROBATCH > 1:
        for name, axis in PER_EXAMPLE_BATCH_AXIS.items():
            out[name] = _to_microbatches(out[name], axis)
    return {'x': out['x'], 'c': out['c'], 'ctx': out['ctx'], 'c_ctx': out['c_ctx'], 'w_mod': out['w_mod'], 'b_mod': out['b_mod'], 'norm_mix': out['norm_mix'], 'w_in': out['w_in'], 'ssm_conv_w': out['ssm_conv_w'], 'ssm_conv_b': out['ssm_conv_b'], 'dt_bias': out['dt_bias'], 'a_log': out['a_log'], 'd_skip': out['d_skip'], 'ssm_norm': out['ssm_norm'], 'cf_conv_w': out['cf_conv_w'], 'cf_conv_b': out['cf_conv_b'], 'cf_ln_g': out['cf_ln_g'], 'cf_ln_b': out['cf_ln_b'], 'w_proj_a': out['w_proj_a'], 'w_proj_b': out['w_proj_b'], 'w_out': out['w_out'], 'norm_ffn': out['norm_ffn'], 'w_ffn_gate': out['w_ffn_gate'], 'w_ffn_up': out['w_ffn_up'], 'w_ffn_down': out['w_ffn_down'], 'norm_final': out['norm_final'], 'loss_target': out['loss_target'], 'm_c_ctx': out['m_c_ctx'], 'm_w_mod': out['m_w_mod'], 'm_b_mod': out['m_b_mod'], 'm_norm_mix': out['m_norm_mix'], 'm_w_in': out['m_w_in'], 'm_ssm_conv_w': out['m_ssm_conv_w'], 'm_ssm_conv_b': out['m_ssm_conv_b'], 'm_dt_bias': out['m_dt_bias'], 'm_a_log': out['m_a_log'], 'm_d_skip': out['m_d_skip'], 'm_ssm_norm': out['m_ssm_norm'], 'm_cf_conv_w': out['m_cf_conv_w'], 'm_cf_conv_b': out['m_cf_conv_b'], 'm_cf_ln_g': out['m_cf_ln_g'], 'm_cf_ln_b': out['m_cf_ln_b'], 'm_w_proj_a': out['m_w_proj_a'], 'm_w_proj_b': out['m_w_proj_b'], 'm_w_out': out['m_w_out'], 'm_norm_ffn': out['m_norm_ffn'], 'm_w_ffn_gate': out['m_w_ffn_gate'], 'm_w_ffn_up': out['m_w_ffn_up'], 'm_w_ffn_down': out['m_w_ffn_down'], 'm_norm_final': out['m_norm_final'], 'v_c_ctx': out['v_c_ctx'], 'v_w_mod': out['v_w_mod'], 'v_b_mod': out['v_b_mod'], 'v_norm_mix': out['v_norm_mix'], 'v_w_in': out['v_w_in'], 'v_ssm_conv_w': out['v_ssm_conv_w'], 'v_ssm_conv_b': out['v_ssm_conv_b'], 'v_dt_bias': out['v_dt_bias'], 'v_a_log': out['v_a_log'], 'v_d_skip': out['v_d_skip'], 'v_ssm_norm': out['v_ssm_norm'], 'v_cf_conv_w': out['v_cf_conv_w'], 'v_cf_conv_b': out['v_cf_conv_b'], 'v_cf_ln_g': out['v_cf_ln_g'], 'v_cf_ln_b': out['v_cf_ln_b'], 'v_w_proj_a': out['v_w_proj_a'], 'v_w_proj_b': out['v_w_proj_b'], 'v_w_out': out['v_w_out'], 'v_norm_ffn': out['v_norm_ffn'], 'v_w_ffn_gate': out['v_w_ffn_gate'], 'v_w_ffn_up': out['v_w_ffn_up'], 'v_w_ffn_down': out['v_w_ffn_down'], 'v_norm_final': out['v_norm_final']}


def _loss(weights, diff, rest, loss_target):
    with _jax.named_scope("forward"):
        args = {**rest, TWIN_DIFF_INPUT: diff, **{k: w.astype(_WEIGHT_DTYPES[k]) for k, w in weights.items()}}
        y = _forward(args)
    with _jax.named_scope("loss_head"):
        err = _jnp.square(y.astype(_jnp.float32) - loss_target)
        return 0.5 * _jnp.sum(_jnp.mean(err, axis=-1)) if err.ndim else 0.5 * err


def _adamw(w, g, m, v):
    m = ADAM_B1 * m + (1.0 - ADAM_B1) * g
    v = ADAM_B2 * v + (1.0 - ADAM_B2) * _jnp.square(g)
    m_hat = m / (1.0 - ADAM_B1 ** ADAM_STEP)
    v_hat = v / (1.0 - ADAM_B2 ** ADAM_STEP)
    delta = -ADAM_LR * (m_hat / (_jnp.sqrt(v_hat) + ADAM_EPS) + ADAM_WD * w)
    return delta, m, v


def reference(x, c, ctx, c_ctx, w_mod, b_mod, norm_mix, w_in, ssm_conv_w, ssm_conv_b, dt_bias, a_log, d_skip, ssm_norm, cf_conv_w, cf_conv_b, cf_ln_g, cf_ln_b, w_proj_a, w_proj_b, w_out, norm_ffn, w_ffn_gate, w_ffn_up, w_ffn_down, norm_final, loss_target, m_c_ctx, m_w_mod, m_b_mod, m_norm_mix, m_w_in, m_ssm_conv_w, m_ssm_conv_b, m_dt_bias, m_a_log, m_d_skip, m_ssm_norm, m_cf_conv_w, m_cf_conv_b, m_cf_ln_g, m_cf_ln_b, m_w_proj_a, m_w_proj_b, m_w_out, m_norm_ffn, m_w_ffn_gate, m_w_ffn_up, m_w_ffn_down, m_norm_final, v_c_ctx, v_w_mod, v_b_mod, v_norm_mix, v_w_in, v_ssm_conv_w, v_ssm_conv_b, v_dt_bias, v_a_log, v_d_skip, v_ssm_norm, v_cf_conv_w, v_cf_conv_b, v_cf_ln_g, v_cf_ln_b, v_w_proj_a, v_w_proj_b, v_w_out, v_norm_ffn, v_w_ffn_gate, v_w_ffn_up, v_w_ffn_down, v_norm_final):
    given = dict(x=x, c=c, ctx=ctx, c_ctx=c_ctx, w_mod=w_mod, b_mod=b_mod, norm_mix=norm_mix, w_in=w_in, ssm_conv_w=ssm_conv_w, ssm_conv_b=ssm_conv_b, dt_bias=dt_bias, a_log=a_log, d_skip=d_skip, ssm_norm=ssm_norm, cf_conv_w=cf_conv_w, cf_conv_b=cf_conv_b, cf_ln_g=cf_ln_g, cf_ln_b=cf_ln_b, w_proj_a=w_proj_a, w_proj_b=w_proj_b, w_out=w_out, norm_ffn=norm_ffn, w_ffn_gate=w_ffn_gate, w_ffn_up=w_ffn_up, w_ffn_down=w_ffn_down, norm_final=norm_final, loss_target=loss_target, m_c_ctx=m_c_ctx, m_w_mod=m_w_mod, m_b_mod=m_b_mod, m_norm_mix=m_norm_mix, m_w_in=m_w_in, m_ssm_conv_w=m_ssm_conv_w, m_ssm_conv_b=m_ssm_conv_b, m_dt_bias=m_dt_bias, m_a_log=m_a_log, m_d_skip=m_d_skip, m_ssm_norm=m_ssm_norm, m_cf_conv_w=m_cf_conv_w, m_cf_conv_b=m_cf_conv_b, m_cf_ln_g=m_cf_ln_g, m_cf_ln_b=m_cf_ln_b, m_w_proj_a=m_w_proj_a, m_w_proj_b=m_w_proj_b, m_w_out=m_w_out, m_norm_ffn=m_norm_ffn, m_w_ffn_gate=m_w_ffn_gate, m_w_ffn_up=m_w_ffn_up, m_w_ffn_down=m_w_ffn_down, m_norm_final=m_norm_final, v_c_ctx=v_c_ctx, v_w_mod=v_w_mod, v_b_mod=v_b_mod, v_norm_mix=v_norm_mix, v_w_in=v_w_in, v_ssm_conv_w=v_ssm_conv_w, v_ssm_conv_b=v_ssm_conv_b, v_dt_bias=v_dt_bias, v_a_log=v_a_log, v_d_skip=v_d_skip, v_ssm_norm=v_ssm_norm, v_cf_conv_w=v_cf_conv_w, v_cf_conv_b=v_cf_conv_b, v_cf_ln_g=v_cf_ln_g, v_cf_ln_b=v_cf_ln_b, v_w_proj_a=v_w_proj_a, v_w_proj_b=v_w_proj_b, v_w_out=v_w_out, v_norm_ffn=v_norm_ffn, v_w_ffn_gate=v_w_ffn_gate, v_w_ffn_up=v_w_ffn_up, v_w_ffn_down=v_w_ffn_down, v_norm_final=v_norm_final)
    weights = {n: given[n] for n in TWIN_WEIGHTS}
    shared = {n: given[n] for n in SHARED_INPUTS}
    per_example = {n: given[n] for n in ['x', 'c', 'ctx']}
    grad_fn = _jax.value_and_grad(_loss, argnums=(0, 1))

    def one_microbatch(ex, loss_target):
        ex = dict(ex)
        diff = ex.pop(TWIN_DIFF_INPUT)
        return grad_fn(weights, diff, {**shared, **ex}, loss_target)

    if N_MICROBATCH == 1:
        loss, (grad_w, grad_x) = one_microbatch(per_example, given["loss_target"])
    else:
        def body(carry, xs):
            loss_sum, grad_sum = carry
            l_k, (gw_k, gx_k) = one_microbatch(xs[0], xs[1])
            with _jax.named_scope("update"):
                return (loss_sum + l_k, _jax.tree.map(_jnp.add, grad_sum, gw_k)), gx_k

        init = (_jnp.zeros((), _jnp.float32), _jax.tree.map(_jnp.zeros_like, weights))
        (loss, grad_w), grad_x = _jax.lax.scan(body, init, (per_example, given["loss_target"]))
    with _jax.named_scope("update"):
        delta_w, new_m, new_v = {}, {}, {}
        for n in TWIN_WEIGHTS:
            delta_w[n], new_m[n], new_v[n] = _adamw(weights[n], grad_w[n], given["m_" + n], given["v_" + n])
    return (loss, grad_x, *[grad_w[n] for n in TWIN_WEIGHTS], *[delta_w[n] for n in TWIN_WEIGHTS],
            *[new_m[n] for n in TWIN_WEIGHTS], *[new_v[n] for n in TWIN_WEIGHTS])
```

```python
import functools

import jax
import jax.numpy as jnp
from jax import lax
from jax.experimental import pallas as pl
from jax.experimental.pallas import tpu as pltpu

F32 = jnp.float32
MXU_DTYPE = jnp.bfloat16
WIRE_DTYPE = jnp.bfloat16
HI = lax.Precision.HIGHEST
EPS = 1e-6
SSM_GROUPS = 8
SSM_STATE = 128
CHUNK = 128
GRID_W = 64
LANES = 128
VMEM_LIMIT = 52 * 1024 * 1024
ADAM_LR, ADAM_B1, ADAM_B2, ADAM_EPS, ADAM_WD, ADAM_STEP = 0.001, 0.9, 0.999, 1e-08, 0.01, 10
MESH = pl.DeviceIdType.MESH
N_SHARD = 4
N_DEV = 8


def _cp(sem=None):
    kw = dict(vmem_limit_bytes=VMEM_LIMIT)
    if sem is not None:
        kw["dimension_semantics"] = sem
    return pltpu.CompilerParams(**kw)


def _tile(n, target, q):
    best = None
    for t in range(q, min(n, target) + 1, q):
        if n % t == 0:
            best = t
    return best if best is not None else n


def _acc(ref, val, i):
    @pl.when(i == 0)
    def _():
        ref[...] = val

    @pl.when(i > 0)
    def _():
        ref[...] += val


def _bc_spec(w):
    return pl.BlockSpec((1, w), lambda *_: (0, 0))


def _rms(x, w):
    return x * lax.rsqrt(jnp.mean(x * x, axis=-1, keepdims=True) + EPS) * w


def _silu(x):
    return x * jax.nn.sigmoid(x)


def _f_mod(x, w, sc, sh):
    return _rms(x, w) * (1.0 + sc) + sh


def _f_gate(yf, yr, xs, z, dsk, wn):
    return _rms((yf + yr + dsk * xs) * _silu(z), wn)


def _f_ln(cv, g, b):
    mu = jnp.mean(cv, axis=-1, keepdims=True)
    xc = cv - mu
    var = jnp.mean(xc * xc, axis=-1, keepdims=True)
    return _silu(xc * lax.rsqrt(var + EPS) * g + b)


def _f_merge(ya, yb, ga, gb):
    return jax.nn.sigmoid(ga) * ya + jax.nn.sigmoid(gb) * yb


def _f_res(x, mix, g1, wn, sc2, sh2):
    x1 = x + g1 * mix
    return x1, _rms(x1, wn) * (1.0 + sc2) + sh2


def _f_swiglu(gt, up):
    return _silu(gt) * up


def _f_loss(x1, dn, g2, wn, tgt):
    out = _rms(x1 + g2 * dn, wn)
    err = out - tgt
    per_tok = jnp.mean(err * err, axis=-1, keepdims=True)
    return 0.5 * jnp.sum(per_tok, axis=0, keepdims=True)


def _matmul(a, b, *, ta=False, tb=False, out_dtype=F32, tm=512, tn=512, tk=2048, name):
    m, k = (a.shape[1], a.shape[0]) if ta else a.shape
    n = b.shape[0] if tb else b.shape[1]
    assert (b.shape[1] if tb else b.shape[0]) == k, (a.shape, b.shape, ta, tb)
    tm, tn, tk = _tile(m, tm, LANES if ta else 16), _tile(n, tn, LANES), _tile(k, tk, LANES)
    nk = k // tk
    dims = (((0 if ta else 1,), (1 if tb else 0,)), ((), ()))

    def body(a_ref, b_ref, o_ref, *scratch):
        prod = lax.dot_general(a_ref[...].astype(MXU_DTYPE), b_ref[...].astype(MXU_DTYPE), dims,
                               preferred_element_type=F32)
        if nk == 1:
            o_ref[...] = prod.astype(o_ref.dtype)
        else:
            acc = scratch[0]
            kk = pl.program_id(2)
            _acc(acc, prod, kk)

            @pl.when(kk == nk - 1)
            def _():
                o_ref[...] = acc[...].astype(o_ref.dtype)

    a_spec = pl.BlockSpec((tk, tm), lambda i, j, kk: (kk, i)) if ta else pl.BlockSpec((tm, tk), lambda i, j, kk: (i, kk))
    b_spec = pl.BlockSpec((tn, tk), lambda i, j, kk: (j, kk)) if tb else pl.BlockSpec((tk, tn), lambda i, j, kk: (kk, j))
    return pl.pallas_call(
        body, name=name, grid=(m // tm, n // tn, nk), in_specs=[a_spec, b_spec],
        out_specs=pl.BlockSpec((tm, tn), lambda i, j, kk: (i, j)),
        out_shape=jax.ShapeDtypeStruct((m, n), out_dtype),
        scratch_shapes=[] if nk == 1 else [pltpu.VMEM((tm, tn), F32)],
        compiler_params=_cp(("parallel", "parallel", "arbitrary")),
    )(a, b)


def _mesh_pos():
    return lax.axis_index("x"), lax.axis_index("y"), lax.axis_index("c")


def _other_chips(x, y):
    return [(1 - x, y), (x, 1 - y), (1 - x, 1 - y)]


def _allgather_small(v, name):
    m_per, n = v.shape

    def body(x_ref, out_ref, send_sems, recv_sems, local_sem):
        x, y, c = _mesh_pos()
        me, sibling = (x, y, c), (x, y, 1 - c)
        chips = _other_chips(x, y)

        def rows(px, py, pc):
            return out_ref.at[pl.ds((4 * px + 2 * py + pc) * m_per, m_per), :]

        def copy(k, block, to, src=None):
            return pltpu.make_async_remote_copy(
                src_ref=rows(*block) if src is None else src, dst_ref=rows(*block),
                send_sem=send_sems.at[k], recv_sem=recv_sems.at[k], device_id=to, device_id_type=MESH)

        mine = pltpu.make_async_copy(x_ref, rows(*me), local_sem)
        mine.start()
        first = [copy(0, me, sibling, src=x_ref)]
        first += [copy(1 + j, me, (*chip, c), src=x_ref) for j, chip in enumerate(chips)]
        for cp in first:
            cp.start()
        passed = [copy(4 + j, (*chip, c), sibling) for j, chip in enumerate(chips)]
        for j, chip in enumerate(chips):
            copy(1 + j, (*chip, c), me).wait_recv()
            passed[j].start()
        copy(0, sibling, me).wait_recv()
        for j, chip in enumerate(chips):
            copy(4 + j, (*chip, 1 - c), me).wait_recv()
        for cp in first + passed:
            cp.wait_send()
        mine.wait()

    return pl.pallas_call(
        body, name=name, out_shape=jax.ShapeDtypeStruct((N_DEV * m_per, n), v.dtype),
        in_specs=[pl.BlockSpec(memory_space=pltpu.VMEM)], out_specs=pl.BlockSpec(memory_space=pltpu.VMEM),
        scratch_shapes=[pltpu.SemaphoreType.DMA((7,)), pltpu.SemaphoreType.DMA((7,)), pltpu.SemaphoreType.DMA],
        compiler_params=_cp(),
    )(v)


_HBM = pl.BlockSpec(memory_space=pltpu.HBM)


def _allgather_rows(shards, name):
    n = len(shards)

    def body(*refs):
        src, dst = refs[:n], refs[n:2 * n]
        send_sems, recv_sems, local_sems = refs[2 * n:]
        x, y, c = _mesh_pos()
        chips = _other_chips(x, y)

        def rows(i, px, py):
            r = src[i].shape[0]
            return dst[i].at[pl.ds(pl.multiple_of((2 * px + py) * r, 16), r), :]

        def copy(i, j, chip_from, to):
            return pltpu.make_async_remote_copy(
                src_ref=src[i], dst_ref=rows(i, *chip_from), send_sem=send_sems.at[3 * i + j],
                recv_sem=recv_sems.at[3 * i + j], device_id=to, device_id_type=MESH)

        local = [pltpu.make_async_copy(src[i], rows(i, x, y), local_sems.at[i]) for i in range(n)]
        sends = [copy(i, j, (x, y), (*chip, c)) for i in range(n) for j, chip in enumerate(chips)]
        for cp in local + sends:
            cp.start()
        for i in range(n):
            for j, chip in enumerate(chips):
                copy(i, j, chip, (x, y, c)).wait_recv()
        for cp in sends:
            cp.wait_send()
        for cp in local:
            cp.wait()

    return pl.pallas_call(
        body, name=name,
        out_shape=[jax.ShapeDtypeStruct((N_SHARD * s.shape[0], s.shape[1]), s.dtype) for s in shards],
        in_specs=[_HBM] * n, out_specs=[_HBM] * n,
        scratch_shapes=[pltpu.SemaphoreType.DMA((3 * n,)), pltpu.SemaphoreType.DMA((3 * n,)),
                        pltpu.SemaphoreType.DMA((n,))],
        compiler_params=_cp(),
    )(*shards)


def _swap_halves(parts, name):
    n = len(parts)

    def body(*refs):
        src, dst = refs[:n], refs[n:2 * n]
        send_sems, recv_sems = refs[2 * n:]
        x, y, c = _mesh_pos()
        copies = []
        for i in range(n):
            hw = src[i].shape[1] // 2
            theirs = src[i].at[:, pl.ds(pl.multiple_of((1 - c) * hw, LANES), hw)]
            copies.append(pltpu.make_async_remote_copy(
                src_ref=theirs, dst_ref=dst[i], send_sem=send_sems.at[i], recv_sem=recv_sems.at[i],
                device_id=(x, y, 1 - c), device_id_type=MESH))
        for cp in copies:
            cp.start()
        for cp in copies:
            cp.wait()

    return pl.pallas_call(
        body, name=name,
        out_shape=[jax.ShapeDtypeStruct((p.shape[0], p.shape[1] // 2), p.dtype) for p in parts],
        in_specs=[_HBM] * n, out_specs=[_HBM] * n,
        scratch_shapes=[pltpu.SemaphoreType.DMA((n,)), pltpu.SemaphoreType.DMA((n,))],
        compiler_params=_cp(),
    )(*parts)


def _scatter_partials(parts, name):
    n = len(parts)

    def body(*refs):
        src, dst = refs[:n], refs[n:2 * n]
        send_sems, recv_sems = refs[2 * n:]
        x, y, c = _mesh_pos()
        chips = _other_chips(x, y)

        def copy(i, j, chip_to):
            r = src[i].shape[0] // N_SHARD
            theirs = src[i].at[pl.ds(pl.multiple_of((2 * chip_to[0] + chip_to[1]) * r, 16), r), :]
            return pltpu.make_async_remote_copy(
                src_ref=theirs, dst_ref=dst[i].at[j], send_sem=send_sems.at[3 * i + j],
                recv_sem=recv_sems.at[3 * i + j], device_id=(*chip_to, c), device_id_type=MESH)

        copies = [copy(i, j, chip) for i in range(n) for j, chip in enumerate(chips)]
        for cp in copies:
            cp.start()
        for cp in copies:
            cp.wait()

    return pl.pallas_call(
        body, name=name,
        out_shape=[jax.ShapeDtypeStruct((3, p.shape[0] // N_SHARD, p.shape[1]), p.dtype) for p in parts],
        in_specs=[_HBM] * n, out_specs=[_HBM] * n,
        scratch_shapes=[pltpu.SemaphoreType.DMA((3 * n,)), pltpu.SemaphoreType.DMA((3 * n,))],
        compiler_params=_cp(),
    )(*parts)


def _join_halves(halves, name):
    n = len(halves)

    def body(*refs):
        src, dst = refs[:n], refs[n:2 * n]
        send_sems, recv_sems, local_sems = refs[2 * n:]
        x, y, c = _mesh_pos()
        local, remote = [], []
        for i in range(n):
            hw = src[i].shape[1]
            place = dst[i].at[:, pl.ds(pl.multiple_of(c * hw, LANES), hw)]
            local.append(pltpu.make_async_copy(src[i], place, local_sems.at[i]))
            remote.append(pltpu.make_async_remote_copy(
                src_ref=src[i], dst_ref=place, send_sem=send_sems.at[i], recv_sem=recv_sems.at[i],
                device_id=(x, y, 1 - c), device_id_type=MESH))
        for cp in local + remote:
            cp.start()
        for i in range(n):
            hw = src[i].shape[1]
            got = dst[i].at[:, pl.ds(pl.multiple_of((1 - c) * hw, LANES), hw)]
            pltpu.make_async_remote_copy(
                src_ref=src[i], dst_ref=got, send_sem=send_sems.at[i], recv_sem=recv_sems.at[i],
                device_id=(x, y, 1 - c), device_id_type=MESH).wait_recv()
        for cp in remote:
            cp.wait_send()
        for cp in local:
            cp.wait()

    return pl.pallas_call(
        body, name=name,
        out_shape=[jax.ShapeDtypeStruct((h.shape[0], 2 * h.shape[1]), h.dtype) for h in halves],
        in_specs=[_HBM] * n, out_specs=[_HBM] * n,
        scratch_shapes=[pltpu.SemaphoreType.DMA((n,)), pltpu.SemaphoreType.DMA((n,)), pltpu.SemaphoreType.DMA((n,))],
        compiler_params=_cp(),
    )(*halves)


def _pair_sum_wire(g, got, pos, name):
    rows, d = g.shape
    hw = d // 2
    t = _tile(rows, 512, 16)

    def body(pos_ref, g_ref, got_ref, o_ref):
        o_ref[...] = (g_ref[...] + got_ref[...]).astype(o_ref.dtype)

    return pl.pallas_call(
        body, name=name,
        grid_spec=pltpu.PrefetchScalarGridSpec(
            num_scalar_prefetch=1, grid=(rows // t,),
            in_specs=[pl.BlockSpec((t, hw), lambda i, p: (i, p[1])), pl.BlockSpec((t, hw), lambda i, p: (i, 0))],
            out_specs=pl.BlockSpec((t, hw), lambda i, p: (i, 0))),
        out_shape=jax.ShapeDtypeStruct((rows, hw), WIRE_DTYPE),
        compiler_params=_cp(("parallel",)),
    )(pos, g, got)


def _sum_partials(g, got, recv, pos, name):
    rows, d = g.shape
    r, hw = rows // N_SHARD, d // 2
    t = _tile(r, 512, 16)
    nt = r // t

    def body(pos_ref, g_ref, got_ref, recv_ref, o_ref):
        total = g_ref[...] + got_ref[...]
        for j in range(3):
            total = total + recv_ref[j].astype(F32)
        o_ref[...] = total

    return pl.pallas_call(
        body, name=name,
        grid_spec=pltpu.PrefetchScalarGridSpec(
            num_scalar_prefetch=1, grid=(nt,),
            in_specs=[pl.BlockSpec((t, hw), lambda i, p: (p[0] * nt + i, p[1])),
                      pl.BlockSpec((t, hw), lambda i, p: (p[0] * nt + i, 0)),
                      pl.BlockSpec((3, t, hw), lambda i, p: (0, i, 0))],
            out_specs=pl.BlockSpec((t, hw), lambda i, p: (i, 0))),
        out_shape=jax.ShapeDtypeStruct((r, hw), F32),
        compiler_params=_cp(("parallel",)),
    )(pos, g, got, recv)


def _reduce_scatter(grads, pos):
    got = _swap_halves(grads, "rs_swap_halves")
    wire = [_pair_sum_wire(g, h, pos, f"rs_pair_sum_{i}") for i, (g, h) in enumerate(zip(grads, got))]
    recv = _scatter_partials(wire, "rs_scatter")
    halves = [_sum_partials(g, h, rv, pos, f"rs_sum_{i}") for i, (g, h, rv) in enumerate(zip(grads, got, recv))]
    return _join_halves(halves, "rs_join_halves")


def _mod_fwd(x, ctx, nw, sc, sh, csc, csh):
    l, d = x.shape
    lc = ctx.shape[0]
    t = min(256, lc)
    nl, nc = l // t, lc // t

    def body(x_ref, c_ref, nw_ref, sc_ref, sh_ref, csc_ref, csh_ref, o_ref):
        i = pl.program_id(0)

        @pl.when(i < nl)
        def _():
            o_ref[...] = _f_mod(x_ref[...], nw_ref[...], sc_ref[...], sh_ref[...]).astype(o_ref.dtype)

        @pl.when(i >= nl)
        def _():
            o_ref[...] = _f_mod(c_ref[...], nw_ref[...], csc_ref[...], csh_ref[...]).astype(o_ref.dtype)

    return pl.pallas_call(
        body, name="mod_fwd", grid=(nl + nc,),
        in_specs=[pl.BlockSpec((t, d), lambda i: (jnp.minimum(i, nl - 1), 0)),
                  pl.BlockSpec((t, d), lambda i: (jnp.maximum(i - nl, 0), 0))] + [_bc_spec(d)] * 5,
        out_specs=pl.BlockSpec((t, d), lambda i: (i, 0)),
        out_shape=jax.ShapeDtypeStruct((l + lc, d), MXU_DTYPE), compiler_params=_cp(("arbitrary",)),
    )(x, ctx, nw, sc, sh, csc, csh)


def _mod_bwd(x, ctx, nw, sc, sh, csc, csh, dhx, dx_res):
    l, d = x.shape
    lc = ctx.shape[0]
    t = min(256, lc)
    nl, nc = l // t, lc // t

    def body(x_ref, c_ref, nw_ref, sc_ref, sh_ref, csc_ref, csh_ref, dh_ref, dr_ref,
             dx_ref, dnw_ref, dsc_ref, dsh_ref, dcsc_ref, dcsh_ref):
        i = pl.program_id(0)

        @pl.when(i == 0)
        def _():
            for r in (dnw_ref, dsc_ref, dsh_ref, dcsc_ref, dcsh_ref):
                r[...] = jnp.zeros_like(r)

        @pl.when(i < nl)
        def _():
            _, vjp = jax.vjp(_f_mod, x_ref[...], nw_ref[...], sc_ref[...], sh_ref[...])
            dx, dnw, dsc, dsh = vjp(dh_ref[...])
            dx_ref[...] = dx + dr_ref[...]
            dnw_ref[...] += dnw
            dsc_ref[...] += dsc
            dsh_ref[...] += dsh

        @pl.when(i >= nl)
        def _():
            _, vjp = jax.vjp(_f_mod, c_ref[...], nw_ref[...], csc_ref[...], csh_ref[...])
            _, dnw, dsc, dsh = vjp(dh_ref[...])
            dnw_ref[...] += dnw
            dcsc_ref[...] += dsc
            dcsh_ref[...] += dsh

    lat = pl.BlockSpec((t, d), lambda i: (jnp.minimum(i, nl - 1), 0))
    vec = jax.ShapeDtypeStruct((1, d), F32)
    return pl.pallas_call(
        body, name="mod_bwd", grid=(nl + nc,),
        in_specs=[lat, pl.BlockSpec((t, d), lambda i: (jnp.maximum(i - nl, 0), 0))] + [_bc_spec(d)] * 5
        + [pl.BlockSpec((t, d), lambda i: (i, 0)), lat],
        out_specs=[lat] + [_bc_spec(d)] * 5,
        out_shape=[jax.ShapeDtypeStruct((l, d), F32)] + [vec] * 5, compiler_params=_cp(("arbitrary",)),
    )(x, ctx, nw, sc, sh, csc, csh, dhx, dx_res)


def _gate_fwd(yf, yr, xbc, proj, dsk, wn, l, di):
    t = 128

    def body(yf_ref, yr_ref, xs_ref, z_ref, dsk_ref, wn_ref, o_ref):
        o_ref[...] = _f_gate(yf_ref[...], yr_ref[...], xs_ref[...], z_ref[...], dsk_ref[...], wn_ref[...]).astype(o_ref.dtype)

    row = pl.BlockSpec((t, di), lambda i: (i, 0))
    return pl.pallas_call(
        body, name="gate_fwd", grid=(l // t,), in_specs=[row] * 4 + [_bc_spec(di)] * 2, out_specs=row,
        out_shape=jax.ShapeDtypeStruct((l, di), MXU_DTYPE), compiler_params=_cp(("parallel",)),
    )(yf, yr, xbc, proj, dsk, wn)


def _gate_bwd(yf, yr, xbc, proj, dsk, wn, dya, l, lc, di):
    t = 128
    nl, nc = l // t, lc // t

    def body(yf_ref, yr_ref, xs_ref, z_ref, dsk_ref, wn_ref, g_ref, dy_ref, dz_ref, ddsk_ref, dwn_ref):
        i = pl.program_id(0)

        @pl.when(i == 0)
        def _():
            ddsk_ref[...] = jnp.zeros_like(ddsk_ref)
            dwn_ref[...] = jnp.zeros_like(dwn_ref)

        @pl.when(i < nl)
        def _():
            _, vjp = jax.vjp(_f_gate, yf_ref[...], yr_ref[...], xs_ref[...], z_ref[...], dsk_ref[...], wn_ref[...])
            dyf, _, _, dz, ddsk, dwn = vjp(g_ref[...])
            dy_ref[...] = dyf
            dz_ref[...] = dz.astype(dz_ref.dtype)
            ddsk_ref[...] += ddsk
            dwn_ref[...] += dwn

        @pl.when(i >= nl)
        def _():
            dz_ref[...] = jnp.zeros_like(dz_ref)

    lat = pl.BlockSpec((t, di), lambda i: (jnp.minimum(i, nl - 1), 0))
    vec = jax.ShapeDtypeStruct((1, di), F32)
    return pl.pallas_call(
        body, name="gate_bwd", grid=(nl + nc,),
        in_specs=[lat] * 4 + [_bc_spec(di)] * 2 + [lat],
        out_specs=[lat, pl.BlockSpec((t, di), lambda i: (i, 0))] + [_bc_spec(di)] * 2,
        out_shape=[jax.ShapeDtypeStruct((l, di), F32), jax.ShapeDtypeStruct((l + lc, di), MXU_DTYPE)] + [vec] * 2,
        compiler_params=_cp(("arbitrary",)),
    )(yf, yr, xbc, proj, dsk, wn, dya)


def _ln_fwd(cv, g, b):
    l, d = cv.shape
    t = 256

    def body(cv_ref, g_ref, b_ref, o_ref):
        o_ref[...] = _f_ln(cv_ref[...], g_ref[...], b_ref[...]).astype(o_ref.dtype)

    row = pl.BlockSpec((t, d), lambda i: (i, 0))
    return pl.pallas_call(
        body, name="ln_fwd", grid=(l // t,), in_specs=[row] + [_bc_spec(d)] * 2, out_specs=row,
        out_shape=jax.ShapeDtypeStruct((l, d), MXU_DTYPE), compiler_params=_cp(("parallel",)),
    )(cv, g, b)


def _ln_bwd(cv, g, b, dcf):
    l, d = cv.shape
    t = 256

    def body(cv_ref, g_ref, b_ref, dcf_ref, dcv_ref, dg_ref, db_ref):
        _, vjp = jax.vjp(_f_ln, cv_ref[...], g_ref[...], b_ref[...])
        dcv, dg, db = vjp(dcf_ref[...])
        dcv_ref[...] = dcv
        i = pl.program_id(0)
        _acc(dg_ref, dg, i)
        _acc(db_ref, db, i)

    row = pl.BlockSpec((t, d), lambda i: (i, 0))
    vec = jax.ShapeDtypeStruct((1, d), F32)
    return pl.pallas_call(
        body, name="ln_bwd", grid=(l // t,), in_specs=[row] + [_bc_spec(d)] * 2 + [row],
        out_specs=[row] + [_bc_spec(d)] * 2, out_shape=[jax.ShapeDtypeStruct((l, d), F32), vec, vec],
        compiler_params=_cp(("arbitrary",)),
    )(cv, g, b, dcf)


def _merge_fwd(ya, yb, proj, ga_blk):
    l, d = ya.shape
    t = 256

    def body(ya_ref, yb_ref, ga_ref, gb_ref, o_ref):
        o_ref[...] = _f_merge(ya_ref[...], yb_ref[...], ga_ref[...], gb_ref[...]).astype(o_ref.dtype)

    row = pl.BlockSpec((t, d), lambda i: (i, 0))
    return pl.pallas_call(
        body, name="merge_fwd", grid=(l // t,),
        in_specs=[row, row, pl.BlockSpec((t, d), lambda i: (i, ga_blk)), pl.BlockSpec((t, d), lambda i: (i, ga_blk + 1))],
        out_specs=row, out_shape=jax.ShapeDtypeStruct((l, d), MXU_DTYPE), compiler_params=_cp(("parallel",)),
    )(ya, yb, proj, proj)


def _merge_bwd(ya, yb, proj, ga_blk, dmerged, lc):
    l, d = ya.shape
    t = min(256, lc)
    nl, nc = l // t, lc // t

    def body(ya_ref, yb_ref, ga_ref, gb_ref, g_ref, dya_ref, dyb_ref, dga_ref, dgb_ref):
        i = pl.program_id(0)

        @pl.when(i < nl)
        def _():
            _, vjp = jax.vjp(_f_merge, ya_ref[...], yb_ref[...], ga_ref[...], gb_ref[...])
            dya, dyb, dga, dgb = vjp(g_ref[...])
            dya_ref[...] = dya.astype(dya_ref.dtype)
            dyb_ref[...] = dyb.astype(dyb_ref.dtype)
            dga_ref[...] = dga.astype(dga_ref.dtype)
            dgb_ref[...] = dgb.astype(dgb_ref.dtype)

        @pl.when(i >= nl)
        def _():
            dga_ref[...] = jnp.zeros_like(dga_ref)
            dgb_ref[...] = jnp.zeros_like(dgb_ref)

    lat = pl.BlockSpec((t, d), lambda i: (jnp.minimum(i, nl - 1), 0))
    full = pl.BlockSpec((t, d), lambda i: (i, 0))
    return pl.pallas_call(
        body, name="merge_bwd", grid=(nl + nc,),
        in_specs=[lat, lat, pl.BlockSpec((t, d), lambda i: (jnp.minimum(i, nl - 1), ga_blk)),
                  pl.BlockSpec((t, d), lambda i: (jnp.minimum(i, nl - 1), ga_blk + 1)), lat],
        out_specs=[lat, lat, full, full],
        out_shape=[jax.ShapeDtypeStruct((l, d), MXU_DTYPE)] * 2 + [jax.ShapeDtypeStruct((l + lc, d), MXU_DTYPE)] * 2,
        compiler_params=_cp(("arbitrary",)),
    )(ya, yb, proj, proj, dmerged)


def _res_fwd(x, mix, g1, wn, sc2, sh2):
    l, d = x.shape
    t = 256

    def body(x_ref, m_ref, g1_ref, wn_ref, sc_ref, sh_ref, x1_ref, hx_ref):
        x1, hx = _f_res(x_ref[...], m_ref[...], g1_ref[...], wn_ref[...], sc_ref[...], sh_ref[...])
        x1_ref[...] = x1
        hx_ref[...] = hx.astype(hx_ref.dtype)

    row = pl.BlockSpec((t, d), lambda i: (i, 0))
    return pl.pallas_call(
        body, name="res_fwd", grid=(l // t,), in_specs=[row, row] + [_bc_spec(d)] * 4, out_specs=[row, row],
        out_shape=[jax.ShapeDtypeStruct((l, d), F32), jax.ShapeDtypeStruct((l, d), MXU_DTYPE)],
        compiler_params=_cp(("parallel",)),
    )(x, mix, g1, wn, sc2, sh2)


def _res_bwd(x, mix, g1, wn, sc2, sh2, dx1, dhx2):
    l, d = x.shape
    t = 256

    def body(x_ref, m_ref, g1_ref, wn_ref, sc_ref, sh_ref, dx1_ref, dh_ref, dx_ref, dm_ref, dg1_ref, dwn_ref, dsc_ref, dsh_ref):
        _, vjp = jax.vjp(_f_res, x_ref[...], m_ref[...], g1_ref[...], wn_ref[...], sc_ref[...], sh_ref[...])
        dx, dm, dg1, dwn, dsc, dsh = vjp((dx1_ref[...], dh_ref[...]))
        dx_ref[...] = dx
        dm_ref[...] = dm.astype(dm_ref.dtype)
        i = pl.program_id(0)
        _acc(dg1_ref, dg1, i)
        _acc(dwn_ref, dwn, i)
        _acc(dsc_ref, dsc, i)
        _acc(dsh_ref, dsh, i)

    row = pl.BlockSpec((t, d), lambda i: (i, 0))
    vec = jax.ShapeDtypeStruct((1, d), F32)
    return pl.pallas_call(
        body, name="res_bwd", grid=(l // t,), in_specs=[row, row] + [_bc_spec(d)] * 4 + [row, row],
        out_specs=[row, row] + [_bc_spec(d)] * 4,
        out_shape=[jax.ShapeDtypeStruct((l, d), F32), jax.ShapeDtypeStruct((l, d), MXU_DTYPE)] + [vec] * 4,
        compiler_params=_cp(("arbitrary",)),
    )(x, mix, g1, wn, sc2, sh2, dx1, dhx2)


def _swiglu_fwd(gu, df):
    l = gu.shape[0]
    t = 256

    def body(g_ref, u_ref, o_ref):
        o_ref[...] = _f_swiglu(g_ref[...], u_ref[...]).astype(o_ref.dtype)

    return pl.pallas_call(
        body, name="swiglu_fwd", grid=(l // t,),
        in_specs=[pl.BlockSpec((t, df), lambda i: (i, 0)), pl.BlockSpec((t, df), lambda i: (i, 1))],
        out_specs=pl.BlockSpec((t, df), lambda i: (i, 0)),
        out_shape=jax.ShapeDtypeStruct((l, df), MXU_DTYPE), compiler_params=_cp(("parallel",)),
    )(gu, gu)


def _swiglu_bwd(gu, dact, df):
    l = gu.shape[0]
    t = 256

    lo, hi = pl.BlockSpec((t, df), lambda i: (i, 0)), pl.BlockSpec((t, df), lambda i: (i, 1))
    dgu = jax.ShapeDtypeStruct((l, 2 * df), MXU_DTYPE)

    def body(g_ref, u_ref, da_ref, dgu_ref):
        _, vjp = jax.vjp(_f_swiglu, g_ref[...], u_ref[...])
        dg, du = vjp(da_ref[...])
        dgu_ref[:, :df] = dg.astype(dgu_ref.dtype)
        dgu_ref[:, df:] = du.astype(dgu_ref.dtype)

    return pl.pallas_call(
        body, name="swiglu_bwd", grid=(l // t,), in_specs=[lo, hi, lo],
        out_specs=pl.BlockSpec((t, 2 * df), lambda i: (i, 0)), out_shape=dgu, compiler_params=_cp(("parallel",)),
    )(gu, gu, dact)


def _loss_and_grads(x1, dn, g2, wn, tgt):
    l, d = x1.shape
    t = 256

    def body(x1_ref, dn_ref, g2_ref, wn_ref, t_ref, loss_ref, dx_ref, ddn_ref, dg2_ref, dwn_ref):
        loss, vjp = jax.vjp(lambda a, b, c, e: _f_loss(a, b, c, e, t_ref[...]), x1_ref[...], dn_ref[...], g2_ref[...], wn_ref[...])
        dx, ddn, dg2, dwn = vjp(jnp.ones((1, 1), F32))
        dx_ref[...] = dx
        ddn_ref[...] = ddn.astype(ddn_ref.dtype)
        i = pl.program_id(0)
        _acc(loss_ref, loss, i)
        _acc(dg2_ref, dg2, i)
        _acc(dwn_ref, dwn, i)

    row = pl.BlockSpec((t, d), lambda i: (i, 0))
    vec = jax.ShapeDtypeStruct((1, d), F32)
    return pl.pallas_call(
        body, name="loss_and_grads", grid=(l // t,), in_specs=[row, row] + [_bc_spec(d)] * 2 + [row],
        out_specs=[pl.BlockSpec((1, 1), lambda i: (0, 0)), row, row] + [_bc_spec(d)] * 2,
        out_shape=[jax.ShapeDtypeStruct((1, 1), F32), jax.ShapeDtypeStruct((l, d), F32),
                   jax.ShapeDtypeStruct((l, d), MXU_DTYPE), vec, vec],
        compiler_params=_cp(("arbitrary",)),
    )(x1, dn, g2, wn, tgt)


PAD = 8


def _conv5_taps(s_ref, w_ref, l, lc, width):
    half = width // 2
    lat = sum(w_ref[k:k + 1, :] * s_ref[pl.ds(PAD + k - half, l), :] for k in range(width))
    ctx = sum(w_ref[k:k + 1, :] * s_ref[pl.ds(2 * PAD + l + k - half, lc), :] for k in range(width))
    return lat, ctx


def _fill_padded(s_ref, lat, ctx, l, lc):
    zeros = jnp.zeros((PAD, s_ref.shape[1]), F32)
    s_ref[pl.ds(0, PAD), :] = zeros
    s_ref[pl.ds(PAD, l), :] = lat
    s_ref[pl.ds(PAD + l, PAD), :] = zeros
    s_ref[pl.ds(2 * PAD + l, lc), :] = ctx
    s_ref[pl.ds(2 * PAD + l + lc, PAD), :] = zeros


def _conv5_fwd(proj, w, b, l, lc, col0, ncols):
    t_all = l + lc
    cw = LANES
    blk0 = col0 // cw
    width = w.shape[0]

    def body(x_ref, w_ref, b_ref, o_ref, s_ref):
        _fill_padded(s_ref, x_ref[pl.ds(0, l), :], x_ref[pl.ds(l, lc), :], l, lc)
        lat, ctx = _conv5_taps(s_ref, w_ref, l, lc, width)
        o_ref[pl.ds(0, l), :] = _silu(lat + b_ref[...])
        o_ref[pl.ds(l, lc), :] = _silu(ctx + b_ref[...])

    return pl.pallas_call(
        body, name="conv5_fwd", grid=(ncols // cw,),
        in_specs=[pl.BlockSpec((t_all, cw), lambda j: (0, blk0 + j)), pl.BlockSpec((width, cw), lambda j: (0, j)),
                  pl.BlockSpec((1, cw), lambda j: (0, j))],
        out_specs=pl.BlockSpec((t_all, cw), lambda j: (0, j)),
        out_shape=jax.ShapeDtypeStruct((t_all, ncols), F32),
        scratch_shapes=[pltpu.VMEM((t_all + 3 * PAD, cw), F32)], compiler_params=_cp(("parallel",)),
    )(proj, w, b)


def _conv5_bwd(proj, w, b, cots, l, lc, col0, seg0, ncols):
    t_all = l + lc
    cw = LANES
    blk0, sblk0 = col0 // cw, seg0 // cw
    width = w.shape[0]
    half = width // 2
    nc = len(cots)

    def body(*refs):
        x_ref, w_ref, b_ref = refs[:3]
        cot_refs = refs[3:3 + nc]
        dx_ref, dw_ref, db_ref, s_ref = refs[3 + nc:]
        x_lat, x_ctx = x_ref[pl.ds(0, l), :], x_ref[pl.ds(l, lc), :]
        _fill_padded(s_ref, x_lat, x_ctx, l, lc)
        pre_lat, pre_ctx = _conv5_taps(s_ref, w_ref, l, lc, width)
        g = sum(c[...] for c in cot_refs)

        def through_silu(pre, cot):
            _, vjp = jax.vjp(_silu, pre + b_ref[...])
            return vjp(cot)[0]

        d_lat = through_silu(pre_lat, g[:l])
        d_ctx = through_silu(pre_ctx, g[l:])
        db_ref[...] = jnp.sum(d_lat, axis=0, keepdims=True) + jnp.sum(d_ctx, axis=0, keepdims=True)
        for k in range(width):
            dw_ref[k:k + 1, :] = (
                jnp.sum(d_lat * s_ref[pl.ds(PAD + k - half, l), :], axis=0, keepdims=True)
                + jnp.sum(d_ctx * s_ref[pl.ds(2 * PAD + l + k - half, lc), :], axis=0, keepdims=True))
        _fill_padded(s_ref, d_lat, d_ctx, l, lc)
        dx_lat = sum(w_ref[k:k + 1, :] * s_ref[pl.ds(PAD - (k - half), l), :] for k in range(width))
        dx_ctx = sum(w_ref[k:k + 1, :] * s_ref[pl.ds(2 * PAD + l - (k - half), lc), :] for k in range(width))
        dx_ref[pl.ds(0, l), :] = dx_lat.astype(dx_ref.dtype)
        dx_ref[pl.ds(l, lc), :] = dx_ctx.astype(dx_ref.dtype)

    col = pl.BlockSpec((t_all, cw), lambda j: (0, j))
    return pl.pallas_call(
        body, name=f"conv5_bwd_{seg0}", grid=(ncols // cw,),
        in_specs=[pl.BlockSpec((t_all, cw), lambda j: (0, blk0 + sblk0 + j)),
                  pl.BlockSpec((width, cw), lambda j: (0, sblk0 + j)), pl.BlockSpec((1, cw), lambda j: (0, sblk0 + j))]
        + [col] * nc,
        out_specs=[col, pl.BlockSpec((width, cw), lambda j: (0, j)), pl.BlockSpec((1, cw), lambda j: (0, j))],
        out_shape=[jax.ShapeDtypeStruct((t_all, ncols), MXU_DTYPE), jax.ShapeDtypeStruct((width, ncols), F32),
                   jax.ShapeDtypeStruct((1, ncols), F32)],
        scratch_shapes=[pltpu.VMEM((t_all + 3 * PAD, cw), F32)], compiler_params=_cp(("parallel",)),
    )(proj, w, b, *cots)


def _conv31_fwd(proj, w, b, l, d, u_blk):
    cw = LANES
    width = w.shape[0]
    reach = (width // 2) * GRID_W
    nb = d // cw

    def body(u_ref, v_ref, w_ref, b_ref, o_ref, s_ref):
        s_ref[pl.ds(0, reach), :] = jnp.zeros((reach, cw), F32)
        s_ref[pl.ds(reach, l), :] = u_ref[...] * jax.nn.sigmoid(v_ref[...])
        s_ref[pl.ds(reach + l, reach), :] = jnp.zeros((reach, cw), F32)
        o_ref[...] = sum(w_ref[k:k + 1, :] * s_ref[pl.ds(k * GRID_W, l), :] for k in range(width)) + b_ref[...]

    return pl.pallas_call(
        body, name="conv31_fwd", grid=(nb,),
        in_specs=[pl.BlockSpec((l, cw), lambda j: (0, u_blk * nb + j)), pl.BlockSpec((l, cw), lambda j: (0, (u_blk + 1) * nb + j)),
                  pl.BlockSpec((width, cw), lambda j: (0, j)), pl.BlockSpec((1, cw), lambda j: (0, j))],
        out_specs=pl.BlockSpec((l, cw), lambda j: (0, j)), out_shape=jax.ShapeDtypeStruct((l, d), F32),
        scratch_shapes=[pltpu.VMEM((l + 2 * reach, cw), F32)], compiler_params=_cp(("parallel",)),
    )(proj, proj, w, b)


def _conv31_bwd(proj, w, dcv, l, lc, d, u_blk):
    cw = LANES
    width = w.shape[0]
    reach = (width // 2) * GRID_W
    nb = d // cw
    t_all = l + lc

    def body(u_ref, v_ref, w_ref, g_ref, du_ref, dv_ref, dw_ref, db_ref, s_ref):
        zeros = jnp.zeros((reach, cw), F32)
        s_ref[pl.ds(0, reach), :] = zeros
        s_ref[pl.ds(reach + l, reach), :] = zeros
        u, v, g = u_ref[...], v_ref[...], g_ref[...]
        s_ref[pl.ds(reach, l), :] = u * jax.nn.sigmoid(v)
        db_ref[...] = jnp.sum(g, axis=0, keepdims=True)
        for k in range(width):
            dw_ref[k:k + 1, :] = jnp.sum(g * s_ref[pl.ds(k * GRID_W, l), :], axis=0, keepdims=True)
        s_ref[pl.ds(reach, l), :] = g
        dt = sum(w_ref[k:k + 1, :] * s_ref[pl.ds((width - 1 - k) * GRID_W, l), :] for k in range(width))
        _, vjp = jax.vjp(lambda a, c: a * jax.nn.sigmoid(c), u, v)
        du, dv = vjp(dt)
        du_ref[pl.ds(0, l), :] = du.astype(du_ref.dtype)
        dv_ref[pl.ds(0, l), :] = dv.astype(dv_ref.dtype)
        du_ref[pl.ds(l, lc), :] = jnp.zeros((lc, cw), du_ref.dtype)
        dv_ref[pl.ds(l, lc), :] = jnp.zeros((lc, cw), dv_ref.dtype)

    pshape = jax.ShapeDtypeStruct((t_all, d), MXU_DTYPE)
    tall = pl.BlockSpec((t_all, cw), lambda j: (0, j))
    return pl.pallas_call(
        body, name="conv31_bwd", grid=(nb,),
        in_specs=[pl.BlockSpec((l, cw), lambda j: (0, u_blk * nb + j)), pl.BlockSpec((l, cw), lambda j: (0, (u_blk + 1) * nb + j)),
                  pl.BlockSpec((width, cw), lambda j: (0, j)), pl.BlockSpec((l, cw), lambda j: (0, j))],
        out_specs=[tall, tall, pl.BlockSpec((width, cw), lambda j: (0, j)), pl.BlockSpec((1, cw), lambda j: (0, j))],
        out_shape=[pshape, pshape, jax.ShapeDtypeStruct((width, d), F32), jax.ShapeDtypeStruct((1, d), F32)],
        scratch_shapes=[pltpu.VMEM((l + 2 * reach, cw), F32)], compiler_params=_cp(("parallel",)),
    )(proj, proj, w, dcv)


def _softplus(x):
    return jnp.maximum(x, 0.0) + jnp.log(1.0 + jnp.exp(-jnp.abs(x)))


def _dt_fwd(proj, bias, dt_blk):
    t_all = proj.shape[0]
    hh = bias.shape[1]
    q = LANES

    def body(r_ref, b_ref, dt_ref, dtt_ref):
        dt = _softplus(r_ref[...] + b_ref[...])
        dt_ref[...] = dt
        dtt_ref[...] = dt.T

    return pl.pallas_call(
        body, name="dt_fwd", grid=(t_all // q,),
        in_specs=[pl.BlockSpec((q, hh), lambda i: (i, dt_blk)), _bc_spec(hh)],
        out_specs=[pl.BlockSpec((q, hh), lambda i: (i, 0)), pl.BlockSpec((hh, q), lambda i: (0, i))],
        out_shape=[jax.ShapeDtypeStruct((t_all, hh), F32), jax.ShapeDtypeStruct((hh, t_all), F32)],
        compiler_params=_cp(("parallel",)),
    )(proj, bias)


def _dt_bwd(proj, bias, dt, ddt, dda, dt_blk):
    t_all = proj.shape[0]
    hh = bias.shape[1]
    q = _tile(t_all, 1024, LANES)

    def body(r_ref, b_ref, dt_ref, ddt_ref, dda_ref, dr_ref, db_ref, da_ref):
        dr = ddt_ref[...] * jax.nn.sigmoid(r_ref[...] + b_ref[...])
        dr_ref[...] = dr.astype(dr_ref.dtype)
        i = pl.program_id(0)
        _acc(db_ref, jnp.sum(dr, axis=0, keepdims=True), i)
        _acc(da_ref, jnp.sum(dda_ref[...] * dt_ref[...], axis=0, keepdims=True), i)

    row = pl.BlockSpec((q, hh), lambda i: (i, 0))
    vec = jax.ShapeDtypeStruct((1, hh), F32)
    return pl.pallas_call(
        body, name="dt_bwd", grid=(t_all // q,),
        in_specs=[pl.BlockSpec((q, hh), lambda i: (i, dt_blk)), _bc_spec(hh), row, row, row],
        out_specs=[row, _bc_spec(hh), _bc_spec(hh)],
        out_shape=[jax.ShapeDtypeStruct((t_all, hh), MXU_DTYPE), vec, vec], compiler_params=_cp(("arbitrary",)),
    )(proj, bias, dt, ddt, dda)


_NT = (((1,), (1,)), ((), ()))
_TN = (((0,), (0,)), ((), ()))


def _dot(a, b, dims=None, exact=False):
    kw = dict(preferred_element_type=F32)
    if exact:
        kw["precision"] = HI
    if dims is None:
        return jnp.dot(a, b, **kw)
    return lax.dot_general(a, b, dims, **kw)


def _iota(shape, dim):
    return lax.broadcasted_iota(jnp.int32, shape, dim)


class _Ssd:
    def __init__(self, l, lc, di, p, reverse):
        self.q, self.n, self.g = CHUNK, SSM_STATE, SSM_GROUPS
        self.nl, self.ncx = l // CHUNK, lc // CHUNK
        self.ns = self.nl + self.ncx
        self.t_all, self.di, self.p, self.reverse = l + lc, di, p, reverse
        self.hpg = di // p // SSM_GROUPS
        self.gw = self.hpg * p
        self.ntile = self.gw // LANES
        self.hpt = LANES // p
        self.log2p = p.bit_length() - 1
        assert 1 << self.log2p == p and self.gw % LANES == 0 and self.n == LANES and self.q == LANES
        self.d = 1 if reverse else 0

    def chunk_at(self, step):
        if self.reverse:
            return self.ns - 1 - step
        return jnp.where(step < self.ncx, self.nl + step, step - self.ncx)

    def in_specs(self, chunk_of):
        g, n, hpg, q = self.g, self.n, self.hpg, self.q
        b_blk, c_blk = self.di // n, self.di // n + g
        d = self.d
        return [
            pl.BlockSpec((q, self.gw), lambda gi, i: (chunk_of(i), gi)),
            pl.BlockSpec((q, n), lambda gi, i: (chunk_of(i), b_blk + gi)),
            pl.BlockSpec((q, n), lambda gi, i: (chunk_of(i), c_blk + gi)),
            pl.BlockSpec((hpg, q), lambda gi, i: (d * g + gi, chunk_of(i))),
            pl.BlockSpec((1, q, hpg), lambda gi, i: (d * g + gi, chunk_of(i), 0)),
            pl.BlockSpec((hpg, 1), lambda gi, i: (d * g + gi, 0)),
            pl.BlockSpec((1, 1, hpg), lambda gi, i: (d * g + gi, 0, 0)),
        ]

    def masks(self):
        li, si = _iota((self.q, self.q), 0), _iota((self.q, self.q), 1)
        if self.reverse:
            return si >= li, li >= si
        return si <= li, li <= si

    def cumsums(self, dtr_ref, dtc_ref, ar_ref, ac_ref, tri, cs_scr):
        dar = dtr_ref[...] * ar_ref[...]
        dtc = dtc_ref[0]
        ac = ac_ref[0]
        dac = dtc * ac
        cs_scr[...] = _dot(dar, tri, _NT, exact=True)
        cs_col = _dot(tri, dac, exact=True)
        tot_col = jnp.sum(dac, axis=0, keepdims=True)
        tot_row = jnp.sum(dar, axis=1, keepdims=True)
        return dtc, ac, cs_col, tot_col, tot_row

    def tile_expander(self, tt):
        hh, ll = _iota((self.hpg, LANES), 0), _iota((self.hpg, LANES), 1)
        return (hh == tt * self.hpt + lax.shift_right_logical(ll, self.log2p)).astype(F32)

    def head_expander(self, j):
        return (_iota((self.hpg, self.q), 0) == j).astype(F32)

    def head_lanes(self, qq):
        return lax.shift_right_logical(_iota((self.q, LANES), 1), self.log2p) == qq

    def state_scale(self, tot_row):
        rr = (lax.shift_right_logical(_iota((self.gw, self.hpg), 0), self.log2p) == _iota((self.gw, self.hpg), 1)).astype(F32)
        return _dot(rr, jnp.broadcast_to(jnp.exp(tot_row), (self.hpg, self.n)), exact=True)


def _ssd_fwd(xbc, dt_row, dt_col, a_row, a_col, l, lc, di, p, reverse):
    s = _Ssd(l, lc, di, p, reverse)
    q, n, gw = s.q, s.n, s.gw
    neg_inf = float("-inf")

    def body(xs_ref, b_ref, c_ref, dtr_ref, dtc_ref, ar_ref, ac_ref, y_ref, hp_ref, h_scr, cs_scr):
        i = pl.program_id(1)

        @pl.when(i == 0)
        def _():
            h_scr[...] = jnp.zeros_like(h_scr)

        h = h_scr[...]
        hp_ref[0, 0] = h
        mask, _ = s.masks()
        tri = mask.astype(F32)
        dtc, _, cs_col, tot_col, tot_row = s.cumsums(dtr_ref, dtc_ref, ar_ref, ac_ref, tri, cs_scr)
        bb, cb = b_ref[...].astype(MXU_DTYPE), c_ref[...].astype(MXU_DTYPE)
        cbt = _dot(cb, bb, _NT)
        y_off = _dot(cb, h.astype(MXU_DTYPE), _NT)
        w_tiles = []
        for tt in range(s.ntile):
            sl = slice(tt * LANES, (tt + 1) * LANES)
            e = s.tile_expander(tt)
            dt_b = _dot(dtc, e, exact=True)
            cs_b = _dot(cs_col, e, exact=True)
            te_b = jnp.exp(_dot(tot_col - cs_col, e, exact=True))
            x = xs_ref[:, sl] * dt_b
            yd = jnp.zeros((q, LANES), F32)
            for qq in range(s.hpt):
                j = tt * s.hpt + qq
                seg = _dot(cs_col, s.head_expander(j), exact=True) - cs_scr[j:j + 1, :]
                m = (cbt * jnp.exp(jnp.where(mask, seg, neg_inf))).astype(MXU_DTYPE)
                xh = jnp.where(s.head_lanes(qq), x, 0.0).astype(MXU_DTYPE)
                yd = yd + _dot(m, xh)
            y_ref[:, sl] = yd + y_off[:, sl] * jnp.exp(cs_b)
            w_tiles.append((x * te_b).astype(MXU_DTYPE))
        wm = w_tiles[0] if s.ntile == 1 else jnp.concatenate(w_tiles, axis=1)
        h_scr[...] = h * s.state_scale(tot_row) + _dot(wm, bb, _TN)

    d = "rev" if reverse else "fwd"
    return pl.pallas_call(
        body, name=f"ssd_{d}", grid=(s.g, s.ns), in_specs=s.in_specs(s.chunk_at),
        out_specs=[pl.BlockSpec((q, gw), lambda gi, i: (s.chunk_at(i), gi)),
                   pl.BlockSpec((1, 1, gw, n), lambda gi, i: (i, gi, 0, 0))],
        out_shape=[jax.ShapeDtypeStruct((s.t_all, di), F32), jax.ShapeDtypeStruct((s.ns, s.g, gw, n), F32)],
        scratch_shapes=[pltpu.VMEM((gw, n), F32), pltpu.VMEM((s.hpg, q), F32)],
        compiler_params=_cp(("parallel", "arbitrary")),
    )(xbc, xbc, xbc, dt_row, dt_col, a_row, a_col)


def _ssd_bwd(xbc, dt_row, dt_col, a_row, a_col, dy, hprev, l, lc, di, p, reverse, dsk=None, prev=None):
    s = _Ssd(l, lc, di, p, reverse)
    q, n, gw, hpg = s.q, s.n, s.gw, s.hpg
    neg_inf = float("-inf")
    n_extra = (dsk is not None) + (3 if prev is not None else 0)

    def chunk_of(i):
        return s.chunk_at(s.ns - 1 - i)

    def body(xs_ref, b_ref, c_ref, dtr_ref, dtc_ref, ar_ref, ac_ref, dy_ref, hp_ref, *rest):
        extra, (dxs_ref, db_ref, dc_ref, ddt_ref, dda_ref, dh_scr, cs_scr) = rest[:n_extra], rest[n_extra:]
        dsk_ref = extra[0] if dsk is not None else None
        prev_refs = extra[-3:] if prev is not None else None
        i = pl.program_id(1)

        @pl.when(i == 0)
        def _():
            dh_scr[...] = jnp.zeros_like(dh_scr)

        latent = (chunk_of(i) < s.nl).astype(F32)
        h, dh = hp_ref[0, 0], dh_scr[...]
        hb, dhb = h.astype(MXU_DTYPE), dh.astype(MXU_DTYPE)
        mask, mask_t = s.masks()
        tri, tri_t = mask.astype(F32), mask_t.astype(F32)
        dtc, ac, cs_col, tot_col, tot_row = s.cumsums(dtr_ref, dtc_ref, ar_ref, ac_ref, tri, cs_scr)
        bb, cb = b_ref[...].astype(MXU_DTYPE), c_ref[...].astype(MXU_DTYPE)
        cbt, bct = _dot(cb, bb, _NT), _dot(bb, cb, _NT)
        b_dh = _dot(bb, dhb, _NT)
        y_off0 = _dot(cb, hb, _NT)
        d_g, d_gt = jnp.zeros((q, q), F32), jnp.zeros((q, q), F32)
        dcs = jnp.zeros((q, hpg), F32)
        ddt_x = jnp.zeros((q, hpg), F32)
        r_state = jnp.zeros((q, hpg), F32)
        dye_tiles, xte_tiles = [], []
        for tt in range(s.ntile):
            sl = slice(tt * LANES, (tt + 1) * LANES)
            e = s.tile_expander(tt)
            dt_b = _dot(dtc, e, exact=True)
            ecs_b = jnp.exp(_dot(cs_col, e, exact=True))
            te_b = jnp.exp(_dot(tot_col - cs_col, e, exact=True))
            xs_t = xs_ref[:, sl]
            x = xs_t * dt_b
            xb = x.astype(MXU_DTYPE)
            d_y = dy_ref[:, sl] * latent
            d_yb = d_y.astype(MXU_DTYPE)
            dx_state = b_dh[:, sl] * te_b
            dx_diag = jnp.zeros((q, LANES), F32)
            for qq in range(s.hpt):
                j = tt * s.hpt + qq
                csc_b = _dot(cs_col, s.head_expander(j), exact=True)
                csr = cs_scr[j:j + 1, :]
                lm = jnp.exp(jnp.where(mask, csc_b - csr, neg_inf))
                lm_t = jnp.exp(jnp.where(mask_t, csr - csc_b, neg_inf))
                m_t = bct * lm_t
                lanes = s.head_lanes(qq)
                d_yh = jnp.where(lanes, d_y, 0.0).astype(MXU_DTYPE)
                xh = jnp.where(lanes, x, 0.0).astype(MXU_DTYPE)
                d_m = _dot(d_yh, xb, _NT)
                d_mt = _dot(xh, d_yb, _NT)
                r1 = jnp.sum(d_m * (cbt * lm), axis=1, keepdims=True)
                r2 = jnp.sum(d_mt * m_t, axis=1, keepdims=True)
                dcs = dcs + (r1 - r2) * (_iota((1, hpg), 1) == j).astype(F32)
                dx_diag = dx_diag + _dot(m_t.astype(MXU_DTYPE), d_yh)
                d_g = d_g + d_m * lm
                d_gt = d_gt + d_mt * lm_t
            d_x = dx_diag + dx_state
            d_xs = d_x * dt_b
            if dsk_ref is not None:
                d_xs = d_xs + d_y * dsk_ref[:, sl]
            if prev_refs is not None:
                d_xs = d_xs + prev_refs[0][:, sl]
            dxs_ref[:, sl] = d_xs
            ddt_x = ddt_x + _dot(d_x * xs_t, e, _NT, exact=True)
            dcs = dcs + _dot(d_y * y_off0[:, sl] * ecs_b, e, _NT, exact=True)
            r_state = r_state + _dot(x * dx_state, e, _NT, exact=True)
            dye_tiles.append((d_y * ecs_b).astype(MXU_DTYPE))
            xte_tiles.append((x * te_b).astype(MXU_DTYPE))
        dye = dye_tiles[0] if s.ntile == 1 else jnp.concatenate(dye_tiles, axis=1)
        xte = xte_tiles[0] if s.ntile == 1 else jnp.concatenate(xte_tiles, axis=1)
        d_c = _dot(d_g.astype(MXU_DTYPE), bb) + _dot(dye, hb)
        d_b = _dot(d_gt.astype(MXU_DTYPE), cb) + _dot(xte, dhb)
        if prev_refs is not None:
            d_b, d_c = d_b + prev_refs[1][...], d_c + prev_refs[2][...]
        dc_ref[...] = d_c
        db_ref[...] = d_b
        scale = s.state_scale(tot_row)
        rr_t = (lax.shift_right_logical(_iota((hpg, gw), 1), s.log2p) == _iota((hpg, gw), 0)).astype(F32)
        h_sum = jnp.sum(_dot(rr_t, dh * h * scale, exact=True), axis=1, keepdims=True)
        eye = (_iota((hpg, hpg), 0) == _iota((hpg, hpg), 1)).astype(F32)
        d_tot = jnp.sum(h_sum * eye, axis=0, keepdims=True) + jnp.sum(r_state, axis=0, keepdims=True)
        dda = _dot(tri_t, dcs - r_state, exact=True) + d_tot
        ddt_ref[0] = ddt_x + dda * ac
        dda_ref[0] = dda
        dh_scr[...] = dh * scale + _dot(dye, cb, _TN)

    d = "rev" if reverse else "fwd"
    col = pl.BlockSpec((1, q, hpg), lambda gi, i: (gi, chunk_of(i), 0))
    gn = pl.BlockSpec((q, n), lambda gi, i: (chunk_of(i), gi))
    wide = pl.BlockSpec((q, gw), lambda gi, i: (chunk_of(i), gi))
    extra_specs, extra_args, aliases = [], [], {}
    if dsk is not None:
        extra_specs.append(pl.BlockSpec((1, gw), lambda gi, i: (0, gi)))
        extra_args.append(dsk)
    if prev is not None:
        first = 9 + len(extra_args)
        extra_specs += [wide, gn, gn]
        extra_args += list(prev)
        aliases = {first: 0, first + 1: 1, first + 2: 2}
    return pl.pallas_call(
        body, name=f"ssd_bwd_{d}", grid=(s.g, s.ns),
        in_specs=s.in_specs(chunk_of) + [
            pl.BlockSpec((q, gw), lambda gi, i: (jnp.minimum(chunk_of(i), s.nl - 1), gi)),
            pl.BlockSpec((1, 1, gw, n), lambda gi, i: (s.ns - 1 - i, gi, 0, 0))] + extra_specs,
        out_specs=[wide, gn, gn, col, col],
        out_shape=[jax.ShapeDtypeStruct((s.t_all, di), F32), jax.ShapeDtypeStruct((s.t_all, s.g * n), F32),
                   jax.ShapeDtypeStruct((s.t_all, s.g * n), F32), jax.ShapeDtypeStruct((s.g, s.t_all, hpg), F32),
                   jax.ShapeDtypeStruct((s.g, s.t_all, hpg), F32)],
        scratch_shapes=[pltpu.VMEM((gw, n), F32), pltpu.VMEM((hpg, q), F32)],
        input_output_aliases=aliases, compiler_params=_cp(("parallel", "arbitrary")),
    )(xbc, xbc, xbc, dt_row, dt_col, a_row, a_col, dy, hprev, *extra_args)


def _ada_fwd(crows, w, b):
    r, d = crows.shape
    ws = w.shape[1]
    tn = _tile(ws, 512, LANES)

    def body(c_ref, w_ref, b_ref, m_ref, s_ref):
        s = _silu(c_ref[...])
        s_ref[...] = s
        m_ref[...] = _dot(s.astype(MXU_DTYPE), w_ref[...].astype(MXU_DTYPE)) + b_ref[...]

    full = pl.BlockSpec((r, d), lambda j: (0, 0))
    return pl.pallas_call(
        body, name="ada_fwd", grid=(ws // tn,),
        in_specs=[full, pl.BlockSpec((d, tn), lambda j: (0, j)), pl.BlockSpec((1, tn), lambda j: (0, j))],
        out_specs=[pl.BlockSpec((r, tn), lambda j: (0, j)), full],
        out_shape=[jax.ShapeDtypeStruct((r, ws), F32), jax.ShapeDtypeStruct((r, d), F32)],
        compiler_params=_cp(("arbitrary",)),
    )(crows, w, b)


def _ada_bwd(s_t, w, dm):
    d, r = s_t.shape
    ws = w.shape[1]
    tn = _tile(ws, 512, LANES)

    def body(st_ref, w_ref, dm_ref, dw_ref, ds_ref):
        dmb = dm_ref[...].astype(MXU_DTYPE)
        dw_ref[...] = _dot(st_ref[...].astype(MXU_DTYPE), dmb)
        _acc(ds_ref, _dot(dmb, w_ref[...].astype(MXU_DTYPE), _NT), pl.program_id(0))

    return pl.pallas_call(
        body, name="ada_bwd", grid=(ws // tn,),
        in_specs=[pl.BlockSpec((d, r), lambda j: (0, 0)), pl.BlockSpec((d, tn), lambda j: (0, j)),
                  pl.BlockSpec((r, tn), lambda j: (0, j))],
        out_specs=[pl.BlockSpec((d, tn), lambda j: (0, j)), pl.BlockSpec((r, d), lambda j: (0, 0))],
        out_shape=[jax.ShapeDtypeStruct((d, ws), F32), jax.ShapeDtypeStruct((r, d), F32)],
        compiler_params=_cp(("arbitrary",)),
    )(s_t, w, dm)


def _adamw(w, g, m, v, name):
    r, c = w.shape
    t = _tile(r, max(8, 300_000 // c), 8)

    def body(w_ref, g_ref, m_ref, v_ref, d_ref, m2_ref, v2_ref):
        g = g_ref[...]
        m2 = ADAM_B1 * m_ref[...] + (1.0 - ADAM_B1) * g
        v2 = ADAM_B2 * v_ref[...] + (1.0 - ADAM_B2) * (g * g)
        m_hat = m2 / (1.0 - ADAM_B1 ** ADAM_STEP)
        v_hat = v2 / (1.0 - ADAM_B2 ** ADAM_STEP)
        d_ref[...] = -ADAM_LR * (m_hat / (jnp.sqrt(v_hat) + ADAM_EPS) + ADAM_WD * w_ref[...])
        m2_ref[...] = m2
        v2_ref[...] = v2

    blk = pl.BlockSpec((t, c), lambda i: (i, 0))
    shape = jax.ShapeDtypeStruct((r, c), F32)
    return pl.pallas_call(
        body, name=name, grid=(r // t,), in_specs=[blk] * 4, out_specs=[blk] * 3, out_shape=[shape] * 3,
        compiler_params=_cp(("parallel",)),
    )(w, g, m, v)


def _sum_devices(gathered):
    rows, w = gathered.shape
    per = rows // N_DEV

    def body(g_ref, o_ref):
        total = g_ref[pl.ds(0, per), :]
        for dev in range(1, N_DEV):
            total = total + g_ref[pl.ds(dev * per, per), :]
        o_ref[...] = total

    return pl.pallas_call(
        body, name="sum_devices", out_shape=jax.ShapeDtypeStruct((per, w), F32),
        in_specs=[pl.BlockSpec(memory_space=pltpu.VMEM)], out_specs=pl.BlockSpec(memory_space=pltpu.VMEM),
        compiler_params=_cp(),
    )(gathered)


def _c_ctx_grad(parts, c_ctx):
    rows, d = parts.shape
    per = rows // N_DEV

    def body(p_ref, c_ref, o_ref):
        total = p_ref[pl.ds(0, 1), :]
        for chip in range(1, N_SHARD):
            total = total + p_ref[pl.ds(2 * chip * per, 1), :]
        _, vjp = jax.vjp(_silu, c_ref[...])
        o_ref[...] = vjp(total)[0]

    return pl.pallas_call(
        body, name="c_ctx_grad", out_shape=jax.ShapeDtypeStruct((1, d), F32),
        in_specs=[pl.BlockSpec(memory_space=pltpu.VMEM)] * 2, out_specs=pl.BlockSpec(memory_space=pltpu.VMEM),
        compiler_params=_cp(),
    )(parts, c_ctx)


def _pad_rows(a, rows, width):
    return jnp.pad(a, ((0, rows - a.shape[0]), (0, width - a.shape[1])))


def _pack(vectors, quantum):
    flat = jnp.concatenate([v.reshape(-1) for v in vectors])
    return jnp.pad(flat, (0, -flat.shape[0] % quantum))


def kernel(x, c, ctx, c_ctx, w_mod, b_mod, norm_mix, w_in, ssm_conv_w, ssm_conv_b, dt_bias, a_log, d_skip, ssm_norm, cf_conv_w, cf_conv_b, cf_ln_g, cf_ln_b, w_proj_a, w_proj_b, w_out, norm_ffn, w_ffn_gate, w_ffn_up, w_ffn_down, norm_final, loss_target, m_c_ctx, m_w_mod, m_b_mod, m_norm_mix, m_w_in, m_ssm_conv_w, m_ssm_conv_b, m_dt_bias, m_a_log, m_d_skip, m_ssm_norm, m_cf_conv_w, m_cf_conv_b, m_cf_ln_g, m_cf_ln_b, m_w_proj_a, m_w_proj_b, m_w_out, m_norm_ffn, m_w_ffn_gate, m_w_ffn_up, m_w_ffn_down, m_norm_final, v_c_ctx, v_w_mod, v_b_mod, v_norm_mix, v_w_in, v_ssm_conv_w, v_ssm_conv_b, v_dt_bias, v_a_log, v_d_skip, v_ssm_norm, v_cf_conv_w, v_cf_conv_b, v_cf_ln_g, v_cf_ln_b, v_w_proj_a, v_w_proj_b, v_w_out, v_norm_ffn, v_w_ffn_gate, v_w_ffn_up, v_w_ffn_down, v_norm_final):
    l, d = x.shape[1], x.shape[2]
    lc = ctx.shape[1]
    t_all = l + lc
    di = ssm_norm.shape[-1]
    h = d_skip.shape[-1]
    p = di // h
    g, n = SSM_GROUPS, SSM_STATE
    hpg = h // g
    conv_dim = di + 2 * g * n
    df = w_ffn_down.shape[1] * N_SHARD
    assert 2 * h == LANES and d % (2 * LANES) == 0

    my_x, my_y, my_c = _mesh_pos()
    chip = 2 * my_x + my_y
    dev = 2 * chip + my_c
    pos = jnp.stack([chip, my_c]).astype(jnp.int32)

    x2, ctx2, tgt = x[0], ctx[0], loss_target[0]
    row = lambda a: a.reshape(1, -1)

    cw_shard, cfw_shard = ssm_conv_w[0], cf_conv_w[0]
    k5, k31 = cw_shard.shape[0], cfw_shard.shape[0]
    r5, r31 = -(-k5 // 8) * 8, -(-k31 // 8) * 8
    wp = max(d, cw_shard.shape[1], cfw_shard.shape[1])
    packed = jnp.concatenate([_pad_rows(c, 8, wp), _pad_rows(cw_shard, r5, wp), _pad_rows(cfw_shard, r31, wp)], axis=0)
    got = _allgather_small(packed, "ag_params").reshape(N_DEV, 8 + r5 + r31, wp)
    c_all = got[:, 0, :d]
    conv_w = got[0::2, 8:8 + k5, :cw_shard.shape[1]].transpose(1, 0, 2).reshape(k5, conv_dim)
    cf_w = got[0::2, 8 + r5:8 + r5 + k31, :cfw_shard.shape[1]].transpose(1, 0, 2).reshape(k31, d)

    ws = w_mod.shape[2]
    crows = jnp.concatenate([c_all, row(c_ctx), jnp.zeros((7, d), F32)], axis=0)
    b_mod_mine = lax.dynamic_slice(b_mod, (0, chip * ws), (1, ws))
    m_part, s_rows = _ada_fwd(crows, w_mod[0], b_mod_mine)
    m_full = _allgather_small(m_part, "ag_mod").reshape(N_DEV, 16, ws)[0::2].transpose(1, 0, 2).reshape(16, N_SHARD * ws)
    m_lat = lax.dynamic_slice(m_full, (dev, 0), (1, 6 * d))
    sh1, sc1, g1, sh2, sc2, g2 = [m_lat[:, i * d:(i + 1) * d] for i in range(6)]
    csh1, csc1 = m_full[8:9, 0:d], m_full[8:9, d:2 * d]

    shards = [w_in[0].T, w_ffn_gate[0].T, w_ffn_up[0].T, w_proj_a[0], w_proj_b[0], w_out[0], w_ffn_down[0]]
    win_t, wg_t, wu_t, wpa, wpb, wout, wdn = _allgather_rows([s.astype(WIRE_DTYPE) for s in shards], "ag_weights")
    o_xbc, o_dt, o_glu, o_gates = di, di + conv_dim, di + conv_dim + 2 * h, di + conv_dim + 2 * h + 2 * d
    win_work = jnp.concatenate([win_t[:o_xbc], win_t[o_glu:], win_t[o_xbc:o_dt], win_t[o_dt:o_glu]], axis=0)
    c_u, c_ga, c_xbc, c_dt = di, di + 2 * d, di + 4 * d, di + 4 * d + conv_dim
    wgu = jnp.concatenate([wg_t, wu_t], axis=0)

    nm = norm_mix
    hx = _mod_fwd(x2, ctx2, nm, sc1, sh1, csc1, csh1)
    proj = _matmul(hx, win_work, tb=True, tm=768, tn=640, name="mm_proj")
    xbc = _conv5_fwd(proj, conv_w, ssm_conv_b, l, lc, c_xbc, conv_dim)
    dt, dt_row = _dt_fwd(proj, dt_bias.reshape(1, 2 * h), c_dt // LANES)
    dt_col = dt.reshape(t_all, 2 * g, hpg).transpose(1, 0, 2)
    a = -jnp.exp(a_log.reshape(2 * h))
    a_row, a_col = a.reshape(2 * h, 1), a.reshape(2 * g, 1, hpg)
    y_f, hp_f = _ssd_fwd(xbc, dt_row, dt_col, a_row, a_col, l, lc, di, p, False)
    y_r, hp_r = _ssd_fwd(xbc, dt_row, dt_col, a_row, a_col, l, lc, di, p, True)
    dsk = jnp.repeat(d_skip.reshape(h), p).reshape(1, di)
    ya_in = _gate_fwd(y_f, y_r, xbc, proj, dsk, ssm_norm, l, di)
    y_a = _matmul(ya_in, wpa, tk=di, name="mm_ya")
    u_blk = c_u // d
    cv = _conv31_fwd(proj, cf_w, cf_conv_b, l, d, u_blk)
    cf = _ln_fwd(cv, cf_ln_g, cf_ln_b)
    y_b = _matmul(cf, wpb, name="mm_yb")
    ga_blk = c_ga // d
    merged = _merge_fwd(y_a, y_b, proj, ga_blk)
    mix = _matmul(merged, wout, name="mm_mix")
    x1, hx2 = _res_fwd(x2, mix, g1, norm_ffn, sc2, sh2)
    gu = _matmul(hx2, wgu, tb=True, tn=_tile(df, 1024, LANES), name="mm_gu")
    act = _swiglu_fwd(gu, df)
    dn = _matmul(act, wdn, tk=_tile(df, 2048, LANES), name="mm_dn")
    loss, dx1, ddn, dg2, d_norm_final = _loss_and_grads(x1, dn, g2, row(norm_final), tgt)

    dact = _matmul(ddn, wdn, tb=True, tn=_tile(df, 1024, LANES), name="mm_dact")
    dw_dn = _matmul(act, ddn, ta=True, tn=d, tk=1024, name="mm_dw_dn")
    dgu = _swiglu_bwd(gu, dact, df)
    dhx2 = _matmul(dgu, wgu, tk=_tile(2 * df, 2048, LANES), name="mm_dhx2")
    dw_gu = _matmul(dgu, hx2, ta=True, tn=d, tk=1024, name="mm_dw_gu")
    dx_res, dmix, dg1, d_norm_ffn, dsc2, dsh2 = _res_bwd(x2, mix, g1, norm_ffn, sc2, sh2, dx1, dhx2)
    dmerged = _matmul(dmix, wout, tb=True, name="mm_dmerged")
    dw_out = _matmul(merged, dmix, ta=True, tn=d, tk=1024, name="mm_dw_out")
    dya, dyb, dga, dgb = _merge_bwd(y_a, y_b, proj, ga_blk, dmerged, lc)
    dcf = _matmul(dyb, wpb, tb=True, name="mm_dcf")
    dw_pb = _matmul(cf, dyb, ta=True, tn=d, tk=1024, name="mm_dw_pb")
    dcv, d_ln_g, d_ln_b = _ln_bwd(cv, cf_ln_g, cf_ln_b, dcf)
    du, dv, d_cf_w, d_cf_b = _conv31_bwd(proj, cf_w, dcv, l, lc, d, u_blk)
    dya_in = _matmul(dya, wpa, tb=True, tn=_tile(di, 1024, LANES), name="mm_dya_in")
    dw_pa = _matmul(ya_in, dya, ta=True, tn=d, tk=1024, name="mm_dw_pa")
    dy, dz, ddsk, d_ssm_norm = _gate_bwd(y_f, y_r, xbc, proj, dsk, ssm_norm, dya_in, l, lc, di)
    dxs_f, db_f, dc_f, ddt_f, dda_f = _ssd_bwd(xbc, dt_row, dt_col, a_row, a_col, dy, hp_f, l, lc, di, p, False, dsk=dsk)
    dxs, db, dc, ddt_r, dda_r = _ssd_bwd(xbc, dt_row, dt_col, a_row, a_col, dy, hp_r, l, lc, di, p, True,
                                         prev=(dxs_f, db_f, dc_f))
    dxs_raw, dcw_x, dcb_x = _conv5_bwd(proj, conv_w, ssm_conv_b, [dxs], l, lc, c_xbc, 0, di)
    db_raw, dcw_b, dcb_b = _conv5_bwd(proj, conv_w, ssm_conv_b, [db], l, lc, c_xbc, di, g * n)
    dc_raw, dcw_c, dcb_c = _conv5_bwd(proj, conv_w, ssm_conv_b, [dc], l, lc, c_xbc, di + g * n, g * n)
    d_conv_w = jnp.concatenate([dcw_x, dcw_b, dcw_c], axis=1)
    d_conv_b = jnp.concatenate([dcb_x, dcb_b, dcb_c], axis=1)
    heads = lambda f, r: jnp.concatenate([f.transpose(1, 0, 2).reshape(t_all, h), r.transpose(1, 0, 2).reshape(t_all, h)], axis=1)
    ddt_raw, d_dt_bias, dda_dt = _dt_bwd(proj, dt_bias.reshape(1, 2 * h), dt, heads(ddt_f, ddt_r), heads(dda_f, dda_r), c_dt // LANES)
    d_a_log = dda_dt * a.reshape(1, 2 * h)
    dproj = jnp.concatenate([dz, du, dv, dga, dgb, dxs_raw, db_raw, dc_raw, ddt_raw], axis=1)
    dhx = _matmul(dproj, win_work, tm=768, tk=_tile(win_work.shape[0], 4096, LANES), name="mm_dhx")
    dw_in_work = _matmul(dproj, hx, ta=True, tm=640, tn=d, tk=768, name="mm_dw_in")
    grad_x, d_norm_mix, dsc1, dsh1, dcsc1, dcsh1 = _mod_bwd(x2, ctx2, nm, sc1, sh1, csc1, csh1, dhx, dx_res)

    dw_in_t = jnp.concatenate([dw_in_work[:c_u], dw_in_work[c_xbc:], dw_in_work[c_u:c_xbc]], axis=0)
    reduced = _reduce_scatter([dw_in_t, dw_gu[:df], dw_gu[df:], dw_pa, dw_pb, dw_out, dw_dn], pos)
    g_in, g_gate, g_up = reduced[0].T, reduced[1].T, reduced[2].T
    g_pa, g_pb, g_out, g_dn = reduced[3:]

    zeros_d = jnp.zeros((1, d), F32)
    dm_lat = jnp.concatenate([dsh1, dsc1, dg1, dsh2, dsc2, dg2], axis=1)
    dm_ctx = jnp.concatenate([dcsh1, dcsc1] + [zeros_d] * 4, axis=1)
    d_d_skip = ddsk.reshape(h, p).sum(axis=1)
    replicated = [dm_lat + dm_ctx, d_norm_mix, d_conv_b, d_dt_bias, d_a_log, d_d_skip, d_ssm_norm, d_cf_b, d_ln_g, d_ln_b,
                  d_norm_ffn, d_norm_final]
    rep_w = [b_mod, norm_mix, ssm_conv_b, dt_bias, a_log, d_skip, ssm_norm, cf_conv_b, cf_ln_g, cf_ln_b, norm_ffn, norm_final]
    rep_m = [m_b_mod, m_norm_mix, m_ssm_conv_b, m_dt_bias, m_a_log, m_d_skip, m_ssm_norm, m_cf_conv_b, m_cf_ln_g, m_cf_ln_b,
             m_norm_ffn, m_norm_final]
    rep_v = [v_b_mod, v_norm_mix, v_ssm_conv_b, v_dt_bias, v_a_log, v_d_skip, v_ssm_norm, v_cf_conv_b, v_cf_ln_g, v_cf_ln_b,
             v_norm_ffn, v_norm_final]
    quantum = 8 * LANES
    rep_flat = _pack(replicated, quantum)
    n_rep = rep_flat.shape[0]
    summed_parts = [rep_flat, _pack([d_conv_w, d_cf_w, dm_ctx], quantum)]
    n_sum = n_rep + summed_parts[1].shape[0]
    everything = jnp.concatenate(summed_parts + [_pack([dm_lat], quantum)])
    gathered = _allgather_small(everything.reshape(8, -1), "ag_small_grads")
    w8 = gathered.shape[1]
    summed = _sum_devices(gathered).reshape(-1)
    dm_lat_all = gathered.reshape(N_DEV, 8 * w8)[:, n_sum:n_sum + 6 * d]
    off = n_rep
    g_conv_w_full = summed[off:off + k5 * conv_dim].reshape(k5, conv_dim)
    off += k5 * conv_dim
    g_cf_w_full = summed[off:off + k31 * d].reshape(k31, d)
    off += k31 * d
    dm_ctx_all = summed[off:off + 6 * d].reshape(1, 6 * d)
    g_conv_w = lax.dynamic_slice(g_conv_w_full, (0, chip * cw_shard.shape[1]), cw_shard.shape)
    g_cf_w = lax.dynamic_slice(g_cf_w_full, (0, chip * cfw_shard.shape[1]), cfw_shard.shape)

    dm_rows = jnp.concatenate([dm_lat_all, dm_ctx_all, jnp.zeros((7, 6 * d), F32)], axis=0)
    dm_mine = lax.dynamic_slice(dm_rows, (0, chip * ws), (16, ws))
    g_w_mod, ds_part = _ada_bwd(s_rows.T, w_mod[0], dm_mine)
    ds_all = _allgather_small(ds_part[8:16], "ag_c_ctx")
    g_c_ctx = _c_ctx_grad(ds_all, row(c_ctx))

    grads, deltas, new_ms, new_vs = {}, {}, {}, {}

    def update(name, w2, g2, m2, v2, shape):
        dl, mm, vv = _adamw(w2, g2, m2, v2, f"adamw_{name}")
        grads[name], deltas[name], new_ms[name], new_vs[name] = (t.reshape(shape) for t in (g2, dl, mm, vv))

    for name, w_, g_, m_, v_ in [
            ("w_mod", w_mod, g_w_mod, m_w_mod, v_w_mod), ("w_in", w_in, g_in, m_w_in, v_w_in),
            ("ssm_conv_w", ssm_conv_w, g_conv_w, m_ssm_conv_w, v_ssm_conv_w),
            ("cf_conv_w", cf_conv_w, g_cf_w, m_cf_conv_w, v_cf_conv_w),
            ("w_proj_a", w_proj_a, g_pa, m_w_proj_a, v_w_proj_a), ("w_proj_b", w_proj_b, g_pb, m_w_proj_b, v_w_proj_b),
            ("w_out", w_out, g_out, m_w_out, v_w_out), ("w_ffn_gate", w_ffn_gate, g_gate, m_w_ffn_gate, v_w_ffn_gate),
            ("w_ffn_up", w_ffn_up, g_up, m_w_ffn_up, v_w_ffn_up), ("w_ffn_down", w_ffn_down, g_dn, m_w_ffn_down, v_w_ffn_down)]:
        update(name, w_[0], g_, m_[0], v_[0], w_.shape)
    update("c_ctx", row(c_ctx), g_c_ctx, row(m_c_ctx), row(v_c_ctx), c_ctx.shape)

    rep_names = ["b_mod", "norm_mix", "ssm_conv_b", "dt_bias", "a_log", "d_skip", "ssm_norm", "cf_conv_b", "cf_ln_g", "cf_ln_b",
                 "norm_ffn", "norm_final"]
    as8 = lambda vs: _pack(vs, quantum).reshape(8, -1)
    g8 = summed[:n_rep].reshape(8, -1)
    d8, m8, v8 = _adamw(as8(rep_w), g8, as8(rep_m), as8(rep_v), "adamw_replicated")
    off = 0
    for name, w_ in zip(rep_names, rep_w):
        size = w_.size
        for store, packed8 in ((grads, g8), (deltas, d8), (new_ms, m8), (new_vs, v8)):
            store[name] = packed8.reshape(-1)[off:off + size].reshape(w_.shape)
        off += size

    order = ["c_ctx", "w_mod", "b_mod", "norm_mix", "w_in", "ssm_conv_w", "ssm_conv_b", "dt_bias", "a_log", "d_skip", "ssm_norm",
             "cf_conv_w", "cf_conv_b", "cf_ln_g", "cf_ln_b", "w_proj_a", "w_proj_b", "w_out", "norm_ffn", "w_ffn_gate", "w_ffn_up",
             "w_ffn_down", "norm_final"]
    total_loss = lax.psum(loss[0, 0], ("x", "y", "c"))
    return (total_loss, grad_x.reshape(x.shape), *[grads[k] for k in order], *[deltas[k] for k in order],
            *[new_ms[k] for k in order], *[new_vs[k] for k in order])
```

```python
import functools

import jax
import jax.numpy as jnp
from jax import lax
from jax.experimental import pallas as pl
from jax.experimental.pallas import tpu as pltpu

F32 = jnp.float32
MXU_DTYPE = jnp.bfloat16
WIRE_DTYPE = jnp.bfloat16
HI = lax.Precision.HIGHEST
EPS = 1e-6
SSM_GROUPS = 8
SSM_STATE = 128
CHUNK = 128
GRID_W = 64
LANES = 128
HEAD_COLS = 16
VMEM_LIMIT = 52 * 1024 * 1024
ADAM_LR, ADAM_B1, ADAM_B2, ADAM_EPS, ADAM_WD, ADAM_STEP = 0.001, 0.9, 0.999, 1e-08, 0.01, 10
MESH = pl.DeviceIdType.MESH
N_SHARD = 4
N_DEV = 8


def _cp(sem=None):
    kw = dict(vmem_limit_bytes=VMEM_LIMIT)
    if sem is not None:
        kw["dimension_semantics"] = sem
    return pltpu.CompilerParams(**kw)


def _tile(n, target, q):
    best = None
    for t in range(q, min(n, target) + 1, q):
        if n % t == 0:
            best = t
    return best if best is not None else n


def _acc(ref, val, i):
    @pl.when(i == 0)
    def _():
        ref[...] = val

    @pl.when(i > 0)
    def _():
        ref[...] += val


def _bc_spec(w):
    return pl.BlockSpec((1, w), lambda *_: (0, 0))


def _rms(x, w):
    return x * lax.rsqrt(jnp.mean(x * x, axis=-1, keepdims=True) + EPS) * w


def _silu(x):
    return x * jax.nn.sigmoid(x)


def _f_mod(x, w, sc, sh):
    return _rms(x, w) * (1.0 + sc) + sh


def _f_gate(yf, yr, xs, z, dsk, wn):
    return _rms((yf + yr + dsk * xs) * _silu(z), wn)


def _f_ln(cv, g, b):
    mu = jnp.mean(cv, axis=-1, keepdims=True)
    xc = cv - mu
    var = jnp.mean(xc * xc, axis=-1, keepdims=True)
    return _silu(xc * lax.rsqrt(var + EPS) * g + b)


def _f_merge(ya, yb, ga, gb):
    return jax.nn.sigmoid(ga) * ya + jax.nn.sigmoid(gb) * yb


def _f_res(x, mix, g1, wn, sc2, sh2):
    x1 = x + g1 * mix
    return x1, _rms(x1, wn) * (1.0 + sc2) + sh2


def _f_swiglu(gt, up):
    return _silu(gt) * up


def _f_loss(x1, dn, g2, wn, tgt):
    out = _rms(x1 + g2 * dn, wn)
    err = out - tgt
    per_tok = jnp.mean(err * err, axis=-1, keepdims=True)
    return 0.5 * jnp.sum(per_tok, axis=0, keepdims=True)


def _matmul(a, b, *, ta=False, tb=False, out_dtype=F32, tm=512, tn=512, tk=2048, name):
    m, k = (a.shape[1], a.shape[0]) if ta else a.shape
    n = b.shape[0] if tb else b.shape[1]
    assert (b.shape[1] if tb else b.shape[0]) == k, (a.shape, b.shape, ta, tb)
    tm, tn, tk = _tile(m, tm, LANES if ta else 16), _tile(n, tn, LANES), _tile(k, tk, LANES)
    nk = k // tk
    dims = (((0 if ta else 1,), (1 if tb else 0,)), ((), ()))

    def body(a_ref, b_ref, o_ref, *scratch):
        prod = lax.dot_general(a_ref[...].astype(MXU_DTYPE), b_ref[...].astype(MXU_DTYPE), dims,
                               preferred_element_type=F32)
        if nk == 1:
            o_ref[...] = prod.astype(o_ref.dtype)
        else:
            acc = scratch[0]
            kk = pl.program_id(2)
            _acc(acc, prod, kk)

            @pl.when(kk == nk - 1)
            def _():
                o_ref[...] = acc[...].astype(o_ref.dtype)

    a_spec = pl.BlockSpec((tk, tm), lambda i, j, kk: (kk, i)) if ta else pl.BlockSpec((tm, tk), lambda i, j, kk: (i, kk))
    b_spec = pl.BlockSpec((tn, tk), lambda i, j, kk: (j, kk)) if tb else pl.BlockSpec((tk, tn), lambda i, j, kk: (kk, j))
    return pl.pallas_call(
        body, name=name, grid=(m // tm, n // tn, nk), in_specs=[a_spec, b_spec],
        out_specs=pl.BlockSpec((tm, tn), lambda i, j, kk: (i, j)),
        out_shape=jax.ShapeDtypeStruct((m, n), out_dtype),
        scratch_shapes=[] if nk == 1 else [pltpu.VMEM((tm, tn), F32)],
        compiler_params=_cp(("parallel", "parallel", "arbitrary")),
    )(a, b)


def _mesh_pos():
    return lax.axis_index("x"), lax.axis_index("y"), lax.axis_index("c")


def _other_chips(x, y):
    return [(1 - x, y), (x, 1 - y), (1 - x, 1 - y)]


def _allgather_small(v, name):
    m_per, n = v.shape

    def body(x_ref, out_ref, send_sems, recv_sems, local_sem):
        x, y, c = _mesh_pos()
        me, sibling = (x, y, c), (x, y, 1 - c)
        chips = _other_chips(x, y)

        def rows(px, py, pc):
            return out_ref.at[pl.ds((4 * px + 2 * py + pc) * m_per, m_per), :]

        def copy(k, block, to, src=None):
            return pltpu.make_async_remote_copy(
                src_ref=rows(*block) if src is None else src, dst_ref=rows(*block),
                send_sem=send_sems.at[k], recv_sem=recv_sems.at[k], device_id=to, device_id_type=MESH)

        mine = pltpu.make_async_copy(x_ref, rows(*me), local_sem)
        mine.start()
        first = [copy(0, me, sibling, src=x_ref)]
        first += [copy(1 + j, me, (*chip, c), src=x_ref) for j, chip in enumerate(chips)]
        for cp in first:
            cp.start()
        passed = [copy(4 + j, (*chip, c), sibling) for j, chip in enumerate(chips)]
        for j, chip in enumerate(chips):
            copy(1 + j, (*chip, c), me).wait_recv()
            passed[j].start()
        copy(0, sibling, me).wait_recv()
        for j, chip in enumerate(chips):
            copy(4 + j, (*chip, 1 - c), me).wait_recv()
        for cp in first + passed:
            cp.wait_send()
        mine.wait()

    return pl.pallas_call(
        body, name=name, out_shape=jax.ShapeDtypeStruct((N_DEV * m_per, n), v.dtype),
        in_specs=[pl.BlockSpec(memory_space=pltpu.VMEM)], out_specs=pl.BlockSpec(memory_space=pltpu.VMEM),
        scratch_shapes=[pltpu.SemaphoreType.DMA((7,)), pltpu.SemaphoreType.DMA((7,)), pltpu.SemaphoreType.DMA],
        compiler_params=_cp(),
    )(v)


_HBM = pl.BlockSpec(memory_space=pltpu.HBM)


def _allgather_rows(shards, name):
    n = len(shards)

    def body(*refs):
        src, dst = refs[:n], refs[n:2 * n]
        send_sems, recv_sems, local_sems = refs[2 * n:]
        x, y, c = _mesh_pos()
        chips = _other_chips(x, y)

        def half(i, px, py, pc):
            r = src[i].shape[0]
            return dst[i].at[pl.ds(pl.multiple_of((2 * px + py) * r + pc * (r // 2), 16), r // 2), :]

        def copy(i, k, block, to, own=False):
            r = src[i].shape[0]
            mine = src[i].at[pl.ds(pl.multiple_of(c * (r // 2), 16), r // 2), :]
            return pltpu.make_async_remote_copy(
                src_ref=mine if own else half(i, *block), dst_ref=half(i, *block), send_sem=send_sems.at[6 * i + k],
                recv_sem=recv_sems.at[6 * i + k], device_id=to, device_id_type=MESH)

        local = [pltpu.make_async_copy(src[i], dst[i].at[pl.ds(pl.multiple_of((2 * x + y) * src[i].shape[0], 16), src[i].shape[0]), :],
                                       local_sems.at[i]) for i in range(n)]
        first = [copy(i, j, (x, y, c), (*chip, c), own=True) for i in range(n) for j, chip in enumerate(chips)]
        for cp in local + first:
            cp.start()
        passed = []
        for i in range(n):
            for j, chip in enumerate(chips):
                copy(i, j, (*chip, c), (x, y, c)).wait_recv()
                passed.append(copy(i, 3 + j, (*chip, c), (x, y, 1 - c)))
                passed[-1].start()
        for i in range(n):
            for j, chip in enumerate(chips):
                copy(i, 3 + j, (*chip, 1 - c), (x, y, c)).wait_recv()
        for cp in first + passed:
            cp.wait_send()
        for cp in local:
            cp.wait()

    return pl.pallas_call(
        body, name=name,
        out_shape=[jax.ShapeDtypeStruct((N_SHARD * s.shape[0], s.shape[1]), s.dtype) for s in shards],
        in_specs=[_HBM] * n, out_specs=[_HBM] * n,
        scratch_shapes=[pltpu.SemaphoreType.DMA((6 * n,)), pltpu.SemaphoreType.DMA((6 * n,)),
                        pltpu.SemaphoreType.DMA((n,))],
        compiler_params=_cp(),
    )(*shards)


def _swap_halves(parts, name):
    n = len(parts)

    def body(*refs):
        src, dst = refs[:n], refs[n:2 * n]
        send_sems, recv_sems = refs[2 * n:]
        x, y, c = _mesh_pos()
        copies = [pltpu.make_async_remote_copy(
            src_ref=src[i].at[s, 1 - c], dst_ref=dst[i].at[s], send_sem=send_sems.at[N_SHARD * i + s],
            recv_sem=recv_sems.at[N_SHARD * i + s], device_id=(x, y, 1 - c), device_id_type=MESH)
            for i in range(n) for s in range(N_SHARD)]
        for cp in copies:
            cp.start()
        for cp in copies:
            cp.wait()

    return pl.pallas_call(
        body, name=name,
        out_shape=[jax.ShapeDtypeStruct((N_SHARD,) + p.shape[2:], p.dtype) for p in parts],
        in_specs=[_HBM] * n, out_specs=[_HBM] * n,
        scratch_shapes=[pltpu.SemaphoreType.DMA((N_SHARD * n,)), pltpu.SemaphoreType.DMA((N_SHARD * n,))],
        compiler_params=_cp(),
    )(*parts)


def _scatter_partials(parts, name):
    n = len(parts)

    def body(*refs):
        src, dst = refs[:n], refs[n:2 * n]
        send_sems, recv_sems = refs[2 * n:]
        x, y, c = _mesh_pos()
        chips = _other_chips(x, y)

        def copy(i, j, chip_to):
            return pltpu.make_async_remote_copy(
                src_ref=src[i].at[2 * chip_to[0] + chip_to[1]], dst_ref=dst[i].at[j], send_sem=send_sems.at[3 * i + j],
                recv_sem=recv_sems.at[3 * i + j], device_id=(*chip_to, c), device_id_type=MESH)

        copies = [copy(i, j, chip) for i in range(n) for j, chip in enumerate(chips)]
        for cp in copies:
            cp.start()
        for cp in copies:
            cp.wait()

    return pl.pallas_call(
        body, name=name,
        out_shape=[jax.ShapeDtypeStruct((3,) + p.shape[1:], p.dtype) for p in parts],
        in_specs=[_HBM] * n, out_specs=[_HBM] * n,
        scratch_shapes=[pltpu.SemaphoreType.DMA((3 * n,)), pltpu.SemaphoreType.DMA((3 * n,))],
        compiler_params=_cp(),
    )(*parts)


def _join_halves(halves, name):
    n = len(halves)

    def body(*refs):
        src, dst = refs[:n], refs[n:2 * n]
        send_sems, recv_sems, local_sems = refs[2 * n:]
        x, y, c = _mesh_pos()
        local, remote = [], []
        for i in range(n):
            local.append(pltpu.make_async_copy(src[i], dst[i].at[c], local_sems.at[i]))
            remote.append(pltpu.make_async_remote_copy(
                src_ref=src[i], dst_ref=dst[i].at[c], send_sem=send_sems.at[i], recv_sem=recv_sems.at[i],
                device_id=(x, y, 1 - c), device_id_type=MESH))
        for cp in local + remote:
            cp.start()
        for i in range(n):
            pltpu.make_async_remote_copy(
                src_ref=src[i], dst_ref=dst[i].at[1 - c], send_sem=send_sems.at[i], recv_sem=recv_sems.at[i],
                device_id=(x, y, 1 - c), device_id_type=MESH).wait_recv()
        for cp in remote:
            cp.wait_send()
        for cp in local:
            cp.wait()

    return pl.pallas_call(
        body, name=name,
        out_shape=[jax.ShapeDtypeStruct((2,) + h.shape, h.dtype) for h in halves],
        in_specs=[_HBM] * n, out_specs=[_HBM] * n,
        scratch_shapes=[pltpu.SemaphoreType.DMA((n,)), pltpu.SemaphoreType.DMA((n,)), pltpu.SemaphoreType.DMA((n,))],
        compiler_params=_cp(),
    )(*halves)


def _pair_sum(g, got, name):
    _, _, hr, d = g.shape
    t = _tile(hr, 256, 16)

    def body(g0_ref, g1_ref, got_ref, wire_ref, own_ref):
        x, y, c = _mesh_pos()
        total = jnp.where(c == 0, g0_ref[0, 0], g1_ref[0, 0]) + got_ref[0]
        wire_ref[0] = total.astype(wire_ref.dtype)

        @pl.when(pl.program_id(1) == 2 * x + y)
        def _():
            own_ref[...] = total

    return pl.pallas_call(
        body, name=name, grid=(hr // t, N_SHARD),
        in_specs=[pl.BlockSpec((1, 1, t, d), lambda i, s: (s, 0, i, 0)), pl.BlockSpec((1, 1, t, d), lambda i, s: (s, 1, i, 0)),
                  pl.BlockSpec((1, t, d), lambda i, s: (s, i, 0))],
        out_specs=[pl.BlockSpec((1, t, d), lambda i, s: (s, i, 0)), pl.BlockSpec((t, d), lambda i, s: (i, 0))],
        out_shape=[jax.ShapeDtypeStruct((N_SHARD, hr, d), WIRE_DTYPE), jax.ShapeDtypeStruct((hr, d), F32)],
        compiler_params=_cp(("parallel", "arbitrary")),
    )(g, g, got)


def _sum_partials(own, recv, name):
    hr, d = own.shape
    t = _tile(hr, 256, 16)

    def body(own_ref, recv_ref, o_ref):
        total = own_ref[...]
        for j in range(3):
            total = total + recv_ref[j].astype(F32)
        o_ref[...] = total

    blk = pl.BlockSpec((t, d), lambda i: (i, 0))
    return pl.pallas_call(
        body, name=name, grid=(hr // t,), in_specs=[blk, pl.BlockSpec((3, t, d), lambda i: (0, i, 0))], out_specs=blk,
        out_shape=jax.ShapeDtypeStruct((hr, d), F32), compiler_params=_cp(("parallel",)),
    )(own, recv)


def _reduce_scatter(grads):
    split = [g.reshape(N_SHARD, 2, g.shape[0] // (2 * N_SHARD), g.shape[1]) for g in grads]
    got = _swap_halves(split, "rs_swap_halves")
    sums = [_pair_sum(g, h, f"rs_pair_sum_{i}") for i, (g, h) in enumerate(zip(split, got))]
    recv = _scatter_partials([w for w, _ in sums], "rs_scatter")
    halves = [_sum_partials(own, rv, f"rs_sum_{i}") for i, ((_, own), rv) in enumerate(zip(sums, recv))]
    return [j.reshape(-1, j.shape[-1]) for j in _join_halves(halves, "rs_join_halves")]


def _mod_fwd(x, ctx, nw, sc, sh, csc, csh):
    l, d = x.shape
    lc = ctx.shape[0]
    t = min(256, lc)
    nl, nc = l // t, lc // t

    def body(x_ref, c_ref, nw_ref, sc_ref, sh_ref, csc_ref, csh_ref, o_ref):
        i = pl.program_id(0)

        @pl.when(i < nl)
        def _():
            o_ref[...] = _f_mod(x_ref[...], nw_ref[...], sc_ref[...], sh_ref[...]).astype(o_ref.dtype)

        @pl.when(i >= nl)
        def _():
            o_ref[...] = _f_mod(c_ref[...], nw_ref[...], csc_ref[...], csh_ref[...]).astype(o_ref.dtype)

    return pl.pallas_call(
        body, name="mod_fwd", grid=(nl + nc,),
        in_specs=[pl.BlockSpec((t, d), lambda i: (jnp.minimum(i, nl - 1), 0)),
                  pl.BlockSpec((t, d), lambda i: (jnp.maximum(i - nl, 0), 0))] + [_bc_spec(d)] * 5,
        out_specs=pl.BlockSpec((t, d), lambda i: (i, 0)),
        out_shape=jax.ShapeDtypeStruct((l + lc, d), MXU_DTYPE), compiler_params=_cp(("arbitrary",)),
    )(x, ctx, nw, sc, sh, csc, csh)


def _mod_bwd(x, ctx, nw, sc, sh, csc, csh, dhx, dx_res):
    l, d = x.shape
    lc = ctx.shape[0]
    t = min(256, lc)
    nl, nc = l // t, lc // t

    def body(x_ref, c_ref, nw_ref, sc_ref, sh_ref, csc_ref, csh_ref, dh_ref, dr_ref,
             dx_ref, dnw_ref, dsc_ref, dsh_ref, dcsc_ref, dcsh_ref):
        i = pl.program_id(0)

        @pl.when(i == 0)
        def _():
            for r in (dnw_ref, dsc_ref, dsh_ref, dcsc_ref, dcsh_ref):
                r[...] = jnp.zeros_like(r)

        @pl.when(i < nl)
        def _():
            _, vjp = jax.vjp(_f_mod, x_ref[...], nw_ref[...], sc_ref[...], sh_ref[...])
            dx, dnw, dsc, dsh = vjp(dh_ref[...])
            dx_ref[...] = dx + dr_ref[...]
            dnw_ref[...] += dnw
            dsc_ref[...] += dsc
            dsh_ref[...] += dsh

        @pl.when(i >= nl)
        def _():
            _, vjp = jax.vjp(_f_mod, c_ref[...], nw_ref[...], csc_ref[...], csh_ref[...])
            _, dnw, dsc, dsh = vjp(dh_ref[...])
            dnw_ref[...] += dnw
            dcsc_ref[...] += dsc
            dcsh_ref[...] += dsh

    lat = pl.BlockSpec((t, d), lambda i: (jnp.minimum(i, nl - 1), 0))
    vec = jax.ShapeDtypeStruct((1, d), F32)
    return pl.pallas_call(
        body, name="mod_bwd", grid=(nl + nc,),
        in_specs=[lat, pl.BlockSpec((t, d), lambda i: (jnp.maximum(i - nl, 0), 0))] + [_bc_spec(d)] * 5
        + [pl.BlockSpec((t, d), lambda i: (i, 0)), lat],
        out_specs=[lat] + [_bc_spec(d)] * 5,
        out_shape=[jax.ShapeDtypeStruct((l, d), F32)] + [vec] * 5, compiler_params=_cp(("arbitrary",)),
    )(x, ctx, nw, sc, sh, csc, csh, dhx, dx_res)


def _gate_fwd(yf, yr, xbc, proj, dsk, wn, l, di):
    t = 128

    def body(yf_ref, yr_ref, xs_ref, z_ref, dsk_ref, wn_ref, o_ref):
        o_ref[...] = _f_gate(yf_ref[...], yr_ref[...], xs_ref[...], z_ref[...], dsk_ref[...], wn_ref[...]).astype(o_ref.dtype)

    row = pl.BlockSpec((t, di), lambda i: (i, 0))
    return pl.pallas_call(
        body, name="gate_fwd", grid=(l // t,), in_specs=[row] * 4 + [_bc_spec(di)] * 2, out_specs=row,
        out_shape=jax.ShapeDtypeStruct((l, di), MXU_DTYPE), compiler_params=_cp(("parallel",)),
    )(yf, yr, xbc, proj, dsk, wn)


def _gate_bwd(yf, yr, xbc, proj, dsk, wn, dya, l, lc, di):
    t = 128
    nl, nc = l // t, lc // t

    def body(yf_ref, yr_ref, xs_ref, z_ref, dsk_ref, wn_ref, g_ref, dy_ref, dz_ref, ddsk_ref, dwn_ref):
        i = pl.program_id(0)

        @pl.when(i == 0)
        def _():
            ddsk_ref[...] = jnp.zeros_like(ddsk_ref)
            dwn_ref[...] = jnp.zeros_like(dwn_ref)

        @pl.when(i < nl)
        def _():
            _, vjp = jax.vjp(_f_gate, yf_ref[...], yr_ref[...], xs_ref[...], z_ref[...], dsk_ref[...], wn_ref[...])
            dyf, _, _, dz, ddsk, dwn = vjp(g_ref[...])
            dy_ref[...] = dyf
            dz_ref[...] = dz.astype(dz_ref.dtype)
            ddsk_ref[...] += ddsk
            dwn_ref[...] += dwn

        @pl.when(i >= nl)
        def _():
            dz_ref[...] = jnp.zeros_like(dz_ref)

    lat = pl.BlockSpec((t, di), lambda i: (jnp.minimum(i, nl - 1), 0))
    vec = jax.ShapeDtypeStruct((1, di), F32)
    return pl.pallas_call(
        body, name="gate_bwd", grid=(nl + nc,),
        in_specs=[lat] * 4 + [_bc_spec(di)] * 2 + [lat],
        out_specs=[lat, pl.BlockSpec((t, di), lambda i: (i, 0))] + [_bc_spec(di)] * 2,
        out_shape=[jax.ShapeDtypeStruct((l, di), F32), jax.ShapeDtypeStruct((l + lc, di), MXU_DTYPE)] + [vec] * 2,
        compiler_params=_cp(("arbitrary",)),
    )(yf, yr, xbc, proj, dsk, wn, dya)


def _ln_fwd(cv, g, b):
    l, d = cv.shape
    t = 256

    def body(cv_ref, g_ref, b_ref, o_ref):
        o_ref[...] = _f_ln(cv_ref[...], g_ref[...], b_ref[...]).astype(o_ref.dtype)

    row = pl.BlockSpec((t, d), lambda i: (i, 0))
    return pl.pallas_call(
        body, name="ln_fwd", grid=(l // t,), in_specs=[row] + [_bc_spec(d)] * 2, out_specs=row,
        out_shape=jax.ShapeDtypeStruct((l, d), MXU_DTYPE), compiler_params=_cp(("parallel",)),
    )(cv, g, b)


def _ln_bwd(cv, g, b, dcf):
    l, d = cv.shape
    t = 256

    def body(cv_ref, g_ref, b_ref, dcf_ref, dcv_ref, dg_ref, db_ref):
        _, vjp = jax.vjp(_f_ln, cv_ref[...], g_ref[...], b_ref[...])
        dcv, dg, db = vjp(dcf_ref[...])
        dcv_ref[...] = dcv
        i = pl.program_id(0)
        _acc(dg_ref, dg, i)
        _acc(db_ref, db, i)

    row = pl.BlockSpec((t, d), lambda i: (i, 0))
    vec = jax.ShapeDtypeStruct((1, d), F32)
    return pl.pallas_call(
        body, name="ln_bwd", grid=(l // t,), in_specs=[row] + [_bc_spec(d)] * 2 + [row],
        out_specs=[row] + [_bc_spec(d)] * 2, out_shape=[jax.ShapeDtypeStruct((l, d), F32), vec, vec],
        compiler_params=_cp(("arbitrary",)),
    )(cv, g, b, dcf)


def _merge_fwd(ya, yb, proj, ga_blk):
    l, d = ya.shape
    t = 256

    def body(ya_ref, yb_ref, ga_ref, gb_ref, o_ref):
        o_ref[...] = _f_merge(ya_ref[...], yb_ref[...], ga_ref[...], gb_ref[...]).astype(o_ref.dtype)

    row = pl.BlockSpec((t, d), lambda i: (i, 0))
    return pl.pallas_call(
        body, name="merge_fwd", grid=(l // t,),
        in_specs=[row, row, pl.BlockSpec((t, d), lambda i: (i, ga_blk)), pl.BlockSpec((t, d), lambda i: (i, ga_blk + 1))],
        out_specs=row, out_shape=jax.ShapeDtypeStruct((l, d), MXU_DTYPE), compiler_params=_cp(("parallel",)),
    )(ya, yb, proj, proj)


def _merge_bwd(ya, yb, proj, ga_blk, dmerged, lc):
    l, d = ya.shape
    t = min(256, lc)
    nl, nc = l // t, lc // t

    def body(ya_ref, yb_ref, ga_ref, gb_ref, g_ref, dya_ref, dyb_ref, dga_ref, dgb_ref):
        i = pl.program_id(0)

        @pl.when(i < nl)
        def _():
            _, vjp = jax.vjp(_f_merge, ya_ref[...], yb_ref[...], ga_ref[...], gb_ref[...])
            dya, dyb, dga, dgb = vjp(g_ref[...])
            dya_ref[...] = dya.astype(dya_ref.dtype)
            dyb_ref[...] = dyb.astype(dyb_ref.dtype)
            dga_ref[...] = dga.astype(dga_ref.dtype)
            dgb_ref[...] = dgb.astype(dgb_ref.dtype)

        @pl.when(i >= nl)
        def _():
            dga_ref[...] = jnp.zeros_like(dga_ref)
            dgb_ref[...] = jnp.zeros_like(dgb_ref)

    lat = pl.BlockSpec((t, d), lambda i: (jnp.minimum(i, nl - 1), 0))
    full = pl.BlockSpec((t, d), lambda i: (i, 0))
    return pl.pallas_call(
        body, name="merge_bwd", grid=(nl + nc,),
        in_specs=[lat, lat, pl.BlockSpec((t, d), lambda i: (jnp.minimum(i, nl - 1), ga_blk)),
                  pl.BlockSpec((t, d), lambda i: (jnp.minimum(i, nl - 1), ga_blk + 1)), lat],
        out_specs=[lat, lat, full, full],
        out_shape=[jax.ShapeDtypeStruct((l, d), MXU_DTYPE)] * 2 + [jax.ShapeDtypeStruct((l + lc, d), MXU_DTYPE)] * 2,
        compiler_params=_cp(("arbitrary",)),
    )(ya, yb, proj, proj, dmerged)


def _res_fwd(x, mix, g1, wn, sc2, sh2):
    l, d = x.shape
    t = 256

    def body(x_ref, m_ref, g1_ref, wn_ref, sc_ref, sh_ref, x1_ref, hx_ref):
        x1, hx = _f_res(x_ref[...], m_ref[...], g1_ref[...], wn_ref[...], sc_ref[...], sh_ref[...])
        x1_ref[...] = x1
        hx_ref[...] = hx.astype(hx_ref.dtype)

    row = pl.BlockSpec((t, d), lambda i: (i, 0))
    return pl.pallas_call(
        body, name="res_fwd", grid=(l // t,), in_specs=[row, row] + [_bc_spec(d)] * 4, out_specs=[row, row],
        out_shape=[jax.ShapeDtypeStruct((l, d), F32), jax.ShapeDtypeStruct((l, d), MXU_DTYPE)],
        compiler_params=_cp(("parallel",)),
    )(x, mix, g1, wn, sc2, sh2)


def _res_bwd(x, mix, g1, wn, sc2, sh2, dx1, dhx2):
    l, d = x.shape
    t = 256

    def body(x_ref, m_ref, g1_ref, wn_ref, sc_ref, sh_ref, dx1_ref, dh_ref, dx_ref, dm_ref, dg1_ref, dwn_ref, dsc_ref, dsh_ref):
        _, vjp = jax.vjp(_f_res, x_ref[...], m_ref[...], g1_ref[...], wn_ref[...], sc_ref[...], sh_ref[...])
        dx, dm, dg1, dwn, dsc, dsh = vjp((dx1_ref[...], dh_ref[...]))
        dx_ref[...] = dx
        dm_ref[...] = dm.astype(dm_ref.dtype)
        i = pl.program_id(0)
        _acc(dg1_ref, dg1, i)
        _acc(dwn_ref, dwn, i)
        _acc(dsc_ref, dsc, i)
        _acc(dsh_ref, dsh, i)

    row = pl.BlockSpec((t, d), lambda i: (i, 0))
    vec = jax.ShapeDtypeStruct((1, d), F32)
    return pl.pallas_call(
        body, name="res_bwd", grid=(l // t,), in_specs=[row, row] + [_bc_spec(d)] * 4 + [row, row],
        out_specs=[row, row] + [_bc_spec(d)] * 4,
        out_shape=[jax.ShapeDtypeStruct((l, d), F32), jax.ShapeDtypeStruct((l, d), MXU_DTYPE)] + [vec] * 4,
        compiler_params=_cp(("arbitrary",)),
    )(x, mix, g1, wn, sc2, sh2, dx1, dhx2)


def _swiglu_fwd(gu, df):
    l = gu.shape[0]
    t = 256

    def body(g_ref, u_ref, o_ref):
        o_ref[...] = _f_swiglu(g_ref[...], u_ref[...]).astype(o_ref.dtype)

    return pl.pallas_call(
        body, name="swiglu_fwd", grid=(l // t,),
        in_specs=[pl.BlockSpec((t, df), lambda i: (i, 0)), pl.BlockSpec((t, df), lambda i: (i, 1))],
        out_specs=pl.BlockSpec((t, df), lambda i: (i, 0)),
        out_shape=jax.ShapeDtypeStruct((l, df), MXU_DTYPE), compiler_params=_cp(("parallel",)),
    )(gu, gu)


def _swiglu_bwd(gu, dact, df):
    l = gu.shape[0]
    t = 256

    lo, hi = pl.BlockSpec((t, df), lambda i: (i, 0)), pl.BlockSpec((t, df), lambda i: (i, 1))
    dgu = jax.ShapeDtypeStruct((l, 2 * df), MXU_DTYPE)

    def body(g_ref, u_ref, da_ref, dgu_ref):
        _, vjp = jax.vjp(_f_swiglu, g_ref[...], u_ref[...])
        dg, du = vjp(da_ref[...])
        dgu_ref[:, :df] = dg.astype(dgu_ref.dtype)
        dgu_ref[:, df:] = du.astype(dgu_ref.dtype)

    return pl.pallas_call(
        body, name="swiglu_bwd", grid=(l // t,), in_specs=[lo, hi, lo],
        out_specs=pl.BlockSpec((t, 2 * df), lambda i: (i, 0)), out_shape=dgu, compiler_params=_cp(("parallel",)),
    )(gu, gu, dact)


def _loss_and_grads(x1, dn, g2, wn, tgt):
    l, d = x1.shape
    t = 256

    def body(x1_ref, dn_ref, g2_ref, wn_ref, t_ref, loss_ref, dx_ref, ddn_ref, dg2_ref, dwn_ref):
        loss, vjp = jax.vjp(lambda a, b, c, e: _f_loss(a, b, c, e, t_ref[...]), x1_ref[...], dn_ref[...], g2_ref[...], wn_ref[...])
        dx, ddn, dg2, dwn = vjp(jnp.ones((1, 1), F32))
        dx_ref[...] = dx
        ddn_ref[...] = ddn.astype(ddn_ref.dtype)
        i = pl.program_id(0)
        _acc(loss_ref, loss, i)
        _acc(dg2_ref, dg2, i)
        _acc(dwn_ref, dwn, i)

    row = pl.BlockSpec((t, d), lambda i: (i, 0))
    vec = jax.ShapeDtypeStruct((1, d), F32)
    return pl.pallas_call(
        body, name="loss_and_grads", grid=(l // t,), in_specs=[row, row] + [_bc_spec(d)] * 2 + [row],
        out_specs=[pl.BlockSpec((1, 1), lambda i: (0, 0)), row, row] + [_bc_spec(d)] * 2,
        out_shape=[jax.ShapeDtypeStruct((1, 1), F32), jax.ShapeDtypeStruct((l, d), F32),
                   jax.ShapeDtypeStruct((l, d), MXU_DTYPE), vec, vec],
        compiler_params=_cp(("arbitrary",)),
    )(x1, dn, g2, wn, tgt)


PAD = 8


def _conv5_taps(s_ref, w_ref, l, lc, width):
    half = width // 2
    lat = sum(w_ref[k:k + 1, :] * s_ref[pl.ds(PAD + k - half, l), :] for k in range(width))
    ctx = sum(w_ref[k:k + 1, :] * s_ref[pl.ds(2 * PAD + l + k - half, lc), :] for k in range(width))
    return lat, ctx


def _fill_padded(s_ref, lat, ctx, l, lc):
    zeros = jnp.zeros((PAD, s_ref.shape[1]), F32)
    s_ref[pl.ds(0, PAD), :] = zeros
    s_ref[pl.ds(PAD, l), :] = lat
    s_ref[pl.ds(PAD + l, PAD), :] = zeros
    s_ref[pl.ds(2 * PAD + l, lc), :] = ctx
    s_ref[pl.ds(2 * PAD + l + lc, PAD), :] = zeros


def _conv5_fwd(proj, w, b, l, lc, col0, ncols):
    t_all = l + lc
    cw = LANES
    blk0 = col0 // cw
    width = w.shape[0]

    def body(x_ref, w_ref, b_ref, o_ref, s_ref):
        _fill_padded(s_ref, x_ref[pl.ds(0, l), :], x_ref[pl.ds(l, lc), :], l, lc)
        lat, ctx = _conv5_taps(s_ref, w_ref, l, lc, width)
        o_ref[pl.ds(0, l), :] = _silu(lat + b_ref[...])
        o_ref[pl.ds(l, lc), :] = _silu(ctx + b_ref[...])

    return pl.pallas_call(
        body, name="conv5_fwd", grid=(ncols // cw,),
        in_specs=[pl.BlockSpec((t_all, cw), lambda j: (0, blk0 + j)), pl.BlockSpec((width, cw), lambda j: (0, j)),
                  pl.BlockSpec((1, cw), lambda j: (0, j))],
        out_specs=pl.BlockSpec((t_all, cw), lambda j: (0, j)),
        out_shape=jax.ShapeDtypeStruct((t_all, ncols), F32),
        scratch_shapes=[pltpu.VMEM((t_all + 3 * PAD, cw), F32)], compiler_params=_cp(("parallel",)),
    )(proj, w, b)


def _conv5_bwd(proj, w, b, cots, l, lc, col0, seg0, ncols):
    t_all = l + lc
    cw = LANES
    blk0, sblk0 = col0 // cw, seg0 // cw
    width = w.shape[0]
    half = width // 2
    nc = len(cots)

    def body(*refs):
        x_ref, w_ref, b_ref = refs[:3]
        cot_refs = refs[3:3 + nc]
        dx_ref, dw_ref, db_ref, s_ref = refs[3 + nc:]
        x_lat, x_ctx = x_ref[pl.ds(0, l), :], x_ref[pl.ds(l, lc), :]
        _fill_padded(s_ref, x_lat, x_ctx, l, lc)
        pre_lat, pre_ctx = _conv5_taps(s_ref, w_ref, l, lc, width)
        g = sum(c[...] for c in cot_refs)

        def through_silu(pre, cot):
            _, vjp = jax.vjp(_silu, pre + b_ref[...])
            return vjp(cot)[0]

        d_lat = through_silu(pre_lat, g[:l])
        d_ctx = through_silu(pre_ctx, g[l:])
        db_ref[...] = jnp.sum(d_lat, axis=0, keepdims=True) + jnp.sum(d_ctx, axis=0, keepdims=True)
        for k in range(width):
            dw_ref[k:k + 1, :] = (
                jnp.sum(d_lat * s_ref[pl.ds(PAD + k - half, l), :], axis=0, keepdims=True)
                + jnp.sum(d_ctx * s_ref[pl.ds(2 * PAD + l + k - half, lc), :], axis=0, keepdims=True))
        _fill_padded(s_ref, d_lat, d_ctx, l, lc)
        dx_lat = sum(w_ref[k:k + 1, :] * s_ref[pl.ds(PAD - (k - half), l), :] for k in range(width))
        dx_ctx = sum(w_ref[k:k + 1, :] * s_ref[pl.ds(2 * PAD + l - (k - half), lc), :] for k in range(width))
        dx_ref[pl.ds(0, l), :] = dx_lat.astype(dx_ref.dtype)
        dx_ref[pl.ds(l, lc), :] = dx_ctx.astype(dx_ref.dtype)

    col = pl.BlockSpec((t_all, cw), lambda j: (0, j))
    return pl.pallas_call(
        body, name=f"conv5_bwd_{seg0}", grid=(ncols // cw,),
        in_specs=[pl.BlockSpec((t_all, cw), lambda j: (0, blk0 + sblk0 + j)),
                  pl.BlockSpec((width, cw), lambda j: (0, sblk0 + j)), pl.BlockSpec((1, cw), lambda j: (0, sblk0 + j))]
        + [col] * nc,
        out_specs=[col, pl.BlockSpec((width, cw), lambda j: (0, j)), pl.BlockSpec((1, cw), lambda j: (0, j))],
        out_shape=[jax.ShapeDtypeStruct((t_all, ncols), MXU_DTYPE), jax.ShapeDtypeStruct((width, ncols), F32),
                   jax.ShapeDtypeStruct((1, ncols), F32)],
        scratch_shapes=[pltpu.VMEM((t_all + 3 * PAD, cw), F32)], compiler_params=_cp(("parallel",)),
    )(proj, w, b, *cots)


def _conv31_fwd(proj, w, b, l, d, u_blk):
    cw = LANES
    width = w.shape[0]
    reach = (width // 2) * GRID_W
    nb = d // cw

    def body(u_ref, v_ref, w_ref, b_ref, o_ref, s_ref):
        s_ref[pl.ds(0, reach), :] = jnp.zeros((reach, cw), F32)
        s_ref[pl.ds(reach, l), :] = u_ref[...] * jax.nn.sigmoid(v_ref[...])
        s_ref[pl.ds(reach + l, reach), :] = jnp.zeros((reach, cw), F32)
        o_ref[...] = sum(w_ref[k:k + 1, :] * s_ref[pl.ds(k * GRID_W, l), :] for k in range(width)) + b_ref[...]

    return pl.pallas_call(
        body, name="conv31_fwd", grid=(nb,),
        in_specs=[pl.BlockSpec((l, cw), lambda j: (0, u_blk * nb + j)), pl.BlockSpec((l, cw), lambda j: (0, (u_blk + 1) * nb + j)),
                  pl.BlockSpec((width, cw), lambda j: (0, j)), pl.BlockSpec((1, cw), lambda j: (0, j))],
        out_specs=pl.BlockSpec((l, cw), lambda j: (0, j)), out_shape=jax.ShapeDtypeStruct((l, d), F32),
        scratch_shapes=[pltpu.VMEM((l + 2 * reach, cw), F32)], compiler_params=_cp(("parallel",)),
    )(proj, proj, w, b)


def _conv31_bwd(proj, w, dcv, l, lc, d, u_blk):
    cw = LANES
    width = w.shape[0]
    reach = (width // 2) * GRID_W
    nb = d // cw
    t_all = l + lc

    def body(u_ref, v_ref, w_ref, g_ref, du_ref, dv_ref, dw_ref, db_ref, s_ref):
        zeros = jnp.zeros((reach, cw), F32)
        s_ref[pl.ds(0, reach), :] = zeros
        s_ref[pl.ds(reach + l, reach), :] = zeros
        u, v, g = u_ref[...], v_ref[...], g_ref[...]
        s_ref[pl.ds(reach, l), :] = u * jax.nn.sigmoid(v)
        db_ref[...] = jnp.sum(g, axis=0, keepdims=True)
        for k in range(width):
            dw_ref[k:k + 1, :] = jnp.sum(g * s_ref[pl.ds(k * GRID_W, l), :], axis=0, keepdims=True)
        s_ref[pl.ds(reach, l), :] = g
        dt = sum(w_ref[k:k + 1, :] * s_ref[pl.ds((width - 1 - k) * GRID_W, l), :] for k in range(width))
        _, vjp = jax.vjp(lambda a, c: a * jax.nn.sigmoid(c), u, v)
        du, dv = vjp(dt)
        du_ref[pl.ds(0, l), :] = du.astype(du_ref.dtype)
        dv_ref[pl.ds(0, l), :] = dv.astype(dv_ref.dtype)
        du_ref[pl.ds(l, lc), :] = jnp.zeros((lc, cw), du_ref.dtype)
        dv_ref[pl.ds(l, lc), :] = jnp.zeros((lc, cw), dv_ref.dtype)

    pshape = jax.ShapeDtypeStruct((t_all, d), MXU_DTYPE)
    tall = pl.BlockSpec((t_all, cw), lambda j: (0, j))
    return pl.pallas_call(
        body, name="conv31_bwd", grid=(nb,),
        in_specs=[pl.BlockSpec((l, cw), lambda j: (0, u_blk * nb + j)), pl.BlockSpec((l, cw), lambda j: (0, (u_blk + 1) * nb + j)),
                  pl.BlockSpec((width, cw), lambda j: (0, j)), pl.BlockSpec((l, cw), lambda j: (0, j))],
        out_specs=[tall, tall, pl.BlockSpec((width, cw), lambda j: (0, j)), pl.BlockSpec((1, cw), lambda j: (0, j))],
        out_shape=[pshape, pshape, jax.ShapeDtypeStruct((width, d), F32), jax.ShapeDtypeStruct((1, d), F32)],
        scratch_shapes=[pltpu.VMEM((l + 2 * reach, cw), F32)], compiler_params=_cp(("parallel",)),
    )(proj, proj, w, dcv)


def _softplus(x):
    return jnp.maximum(x, 0.0) + jnp.log(1.0 + jnp.exp(-jnp.abs(x)))


def _dt_fwd(proj, bias, a, dt_blk):
    t_all = proj.shape[0]
    hh = bias.shape[1]
    q = CHUNK

    def body(r_ref, b_ref, a_ref, dt_ref, cs_ref, tc_ref, cst_ref):
        dt = _softplus(r_ref[...] + b_ref[...])
        dt_ref[...] = dt
        da = dt * a_ref[...]
        li, si = _iota((q, q), 0), _iota((q, q), 1)
        reverse_cols = _iota((q, hh), 1) >= hh // 2
        cs = jnp.where(reverse_cols, _dot((si >= li).astype(F32), da, exact=True), _dot((si <= li).astype(F32), da, exact=True))
        cs_ref[...] = cs
        cst_ref[...] = cs.T
        total = jnp.where(_iota((1, hh), 1) >= hh // 2, cs_ref[0:1, :], cs_ref[q - 1:q, :])
        tc_ref[...] = total - cs

    row = pl.BlockSpec((q, hh), lambda i: (i, 0))
    shape = jax.ShapeDtypeStruct((t_all, hh), F32)
    return pl.pallas_call(
        body, name="dt_fwd", grid=(t_all // q,),
        in_specs=[pl.BlockSpec((q, hh), lambda i: (i, dt_blk)), _bc_spec(hh), _bc_spec(hh)],
        out_specs=[row, row, row, pl.BlockSpec((hh, q), lambda i: (0, i))],
        out_shape=[shape, shape, shape, jax.ShapeDtypeStruct((hh, t_all), F32)],
        compiler_params=_cp(("parallel",)),
    )(proj, bias, a)


def _three_way(x):
    def top(v):
        word = lax.bitcast_convert_type(v, jnp.uint32) & jnp.uint32(0xFFFF0000)
        return lax.bitcast_convert_type(word, F32)

    hi = top(x)
    rest = x - hi
    mid = top(rest)
    return hi.astype(jnp.bfloat16), mid.astype(jnp.bfloat16), (rest - mid).astype(jnp.bfloat16)


def _scan_columns(dt, cs, tc, groups2, hpg):
    t_all = dt.shape[0]
    parts = [part.reshape(t_all, groups2, 1, hpg) for arr in (dt, cs, tc) for part in _three_way(arr)]
    cols = jnp.concatenate(parts, axis=2).transpose(1, 0, 2, 3).reshape(groups2, t_all, 9 * hpg)
    return jnp.pad(cols, ((0, 0), (0, 0), (0, LANES - 9 * hpg)))


def _dt_bwd(proj, bias, dt, ddt, dda, dt_blk):
    t_all = proj.shape[0]
    hh = bias.shape[1]
    q = _tile(t_all, 1024, LANES)

    def body(r_ref, b_ref, dt_ref, ddt_ref, dda_ref, dr_ref, db_ref, da_ref):
        dr = ddt_ref[...] * jax.nn.sigmoid(r_ref[...] + b_ref[...])
        dr_ref[...] = dr.astype(dr_ref.dtype)
        i = pl.program_id(0)
        _acc(db_ref, jnp.sum(dr, axis=0, keepdims=True), i)
        _acc(da_ref, jnp.sum(dda_ref[...] * dt_ref[...], axis=0, keepdims=True), i)

    row = pl.BlockSpec((q, hh), lambda i: (i, 0))
    vec = jax.ShapeDtypeStruct((1, hh), F32)
    return pl.pallas_call(
        body, name="dt_bwd", grid=(t_all // q,),
        in_specs=[pl.BlockSpec((q, hh), lambda i: (i, dt_blk)), _bc_spec(hh), row, row, row],
        out_specs=[row, _bc_spec(hh), _bc_spec(hh)],
        out_shape=[jax.ShapeDtypeStruct((t_all, hh), MXU_DTYPE), vec, vec], compiler_params=_cp(("arbitrary",)),
    )(proj, bias, dt, ddt, dda)


_NT = (((1,), (1,)), ((), ()))
_TN = (((0,), (0,)), ((), ()))


def _dot(a, b, dims=None, exact=False):
    kw = dict(preferred_element_type=F32)
    if exact:
        kw["precision"] = HI
    if dims is None:
        return jnp.dot(a, b, **kw)
    return lax.dot_general(a, b, dims, **kw)


def _iota(shape, dim):
    return lax.broadcasted_iota(jnp.int32, shape, dim)


class _Ssd:
    def __init__(self, l, lc, di, p, reverse):
        self.q, self.n, self.g = CHUNK, SSM_STATE, SSM_GROUPS
        self.nl, self.ncx = l // CHUNK, lc // CHUNK
        self.ns = self.nl + self.ncx
        self.t_all, self.di, self.p, self.reverse = l + lc, di, p, reverse
        self.hpg = di // p // SSM_GROUPS
        self.gw = self.hpg * p
        self.ntile = self.gw // LANES
        self.hpt = LANES // p
        self.log2p = p.bit_length() - 1
        assert 1 << self.log2p == p and self.gw % LANES == 0 and self.n == LANES and self.q == LANES
        assert 9 * self.hpg <= LANES
        self.d = 1 if reverse else 0

    def chunk_at(self, step):
        if self.reverse:
            return self.ns - 1 - step
        return jnp.where(step < self.ncx, self.nl + step, step - self.ncx)

    def selectors(self):
        hpg = self.hpg
        k = jnp.arange(LANES)
        quantity, head, used = k // (3 * hpg), k % hpg, k < 9 * hpg
        lane_head = jnp.arange(LANES) // self.p
        tiles = jnp.stack([
            jnp.concatenate([(used & (quantity == qo))[:, None] & (head[:, None] == tt * self.hpt + lane_head[None, :])
                             for qo in range(3)], axis=1) for tt in range(self.ntile)])
        heads = jnp.stack([jnp.broadcast_to((used & (quantity == 1) & (head == j))[:, None], (LANES, LANES))
                           for j in range(hpg)])
        return tiles.astype(jnp.bfloat16), heads.astype(jnp.bfloat16)

    def in_specs(self, chunk_of):
        g, n, hpg, q = self.g, self.n, self.hpg, self.q
        b_blk, c_blk = self.di // n, self.di // n + g
        d = self.d
        return [
            pl.BlockSpec((q, self.gw), lambda gi, i: (chunk_of(i), gi)),
            pl.BlockSpec((q, n), lambda gi, i: (chunk_of(i), b_blk + gi)),
            pl.BlockSpec((q, n), lambda gi, i: (chunk_of(i), c_blk + gi)),
            pl.BlockSpec((1, q, LANES), lambda gi, i: (d * g + gi, chunk_of(i), 0)),
            pl.BlockSpec((hpg, q), lambda gi, i: (d * g + gi, chunk_of(i))),
            pl.BlockSpec((self.ntile, LANES, 3 * LANES), lambda gi, i: (0, 0, 0)),
            pl.BlockSpec((hpg, LANES, LANES), lambda gi, i: (0, 0, 0)),
        ]

    def masks(self):
        li, si = _iota((self.q, self.q), 0), _iota((self.q, self.q), 1)
        if self.reverse:
            return si >= li, li >= si
        return si <= li, li <= si

    def spread(self, cols, et_ref, tt):
        ex = _dot(cols, et_ref[tt])
        return ex[:, :LANES], ex[:, LANES:2 * LANES], ex[:, 2 * LANES:]

    def head_lanes(self, qq):
        return lax.shift_right_logical(_iota((self.q, LANES), 1), self.log2p) == qq

    def head_sums(self, v, tt):
        sel = _iota((HEAD_COLS, LANES), 0) == tt * self.hpt + lax.shift_right_logical(_iota((HEAD_COLS, LANES), 1), self.log2p)
        sel = sel.astype(jnp.bfloat16)
        return sum(_dot(part, sel, _NT) for part in _three_way(v))

    def state_scale(self, csr_ref):
        last = 0 if self.reverse else self.q - 1
        total = jnp.sum(jnp.where(_iota((self.hpg, self.q), 1) == last, csr_ref[...], 0.0), axis=1, keepdims=True)
        decay = jnp.broadcast_to(jnp.exp(total), (self.hpg, self.n))
        decay = jnp.concatenate([decay, jnp.zeros((HEAD_COLS - self.hpg, self.n), F32)], axis=0)
        rows = lax.shift_right_logical(_iota((self.gw, HEAD_COLS), 0), self.log2p) == _iota((self.gw, HEAD_COLS), 1)
        return _dot_parts(rows.astype(jnp.bfloat16), decay)


def _dot_parts(sel, v, dims=None):
    return sum(_dot(sel, part, dims) for part in _three_way(v))


def _ssd_fwd(xbc, cols, cs_t, l, lc, di, p, reverse):
    s = _Ssd(l, lc, di, p, reverse)
    q, n, gw = s.q, s.n, s.gw
    neg_inf = float("-inf")

    def body(xs_ref, b_ref, c_ref, cols_ref, csr_ref, et_ref, eh_ref, y_ref, hp_ref, h_scr):
        i = pl.program_id(1)

        @pl.when(i == 0)
        def _():
            h_scr[...] = jnp.zeros_like(h_scr)

        h = h_scr[...]
        hp_ref[0, 0] = h
        mask, _ = s.masks()
        cols = cols_ref[0]
        bb, cb = b_ref[...].astype(MXU_DTYPE), c_ref[...].astype(MXU_DTYPE)
        cbt = _dot(cb, bb, _NT)
        y_off = _dot(cb, h.astype(MXU_DTYPE), _NT)
        w_tiles = []
        for tt in range(s.ntile):
            sl = slice(tt * LANES, (tt + 1) * LANES)
            dt_b, cs_b, tc_b = s.spread(cols, et_ref, tt)
            x = xs_ref[:, sl] * dt_b
            yd = jnp.zeros((q, LANES), F32)
            for qq in range(s.hpt):
                j = tt * s.hpt + qq
                seg = _dot(cols, eh_ref[j]) - csr_ref[j:j + 1, :]
                m = (cbt * jnp.exp(jnp.where(mask, seg, neg_inf))).astype(MXU_DTYPE)
                xh = jnp.where(s.head_lanes(qq), x, 0.0).astype(MXU_DTYPE)
                yd = yd + _dot(m, xh)
            y_ref[:, sl] = yd + y_off[:, sl] * jnp.exp(cs_b)
            w_tiles.append((x * jnp.exp(tc_b)).astype(MXU_DTYPE))
        wm = w_tiles[0] if s.ntile == 1 else jnp.concatenate(w_tiles, axis=1)
        h_scr[...] = h * s.state_scale(csr_ref) + _dot(wm, bb, _TN)

    d = "rev" if reverse else "fwd"
    e_tiles, e_heads = s.selectors()
    return pl.pallas_call(
        body, name=f"ssd_{d}", grid=(s.g, s.ns), in_specs=s.in_specs(s.chunk_at),
        out_specs=[pl.BlockSpec((q, gw), lambda gi, i: (s.chunk_at(i), gi)),
                   pl.BlockSpec((1, 1, gw, n), lambda gi, i: (i, gi, 0, 0))],
        out_shape=[jax.ShapeDtypeStruct((s.t_all, di), F32), jax.ShapeDtypeStruct((s.ns, s.g, gw, n), F32)],
        scratch_shapes=[pltpu.VMEM((gw, n), F32)],
        compiler_params=_cp(("parallel", "arbitrary")),
    )(xbc, xbc, xbc, cols, cs_t, e_tiles, e_heads)


def _ssd_bwd(xbc, cols, cs_t, a_cols, dy, hprev, l, lc, di, p, reverse, dsk=None, prev=None):
    s = _Ssd(l, lc, di, p, reverse)
    q, n, gw, hpg = s.q, s.n, s.gw, s.hpg
    neg_inf = float("-inf")
    n_extra = (dsk is not None) + (3 if prev is not None else 0)

    def chunk_of(i):
        return s.chunk_at(s.ns - 1 - i)

    def body(xs_ref, b_ref, c_ref, cols_ref, csr_ref, et_ref, eh_ref, ac_ref, dy_ref, hp_ref, *rest):
        extra, (dxs_ref, db_ref, dc_ref, ddt_ref, dda_ref, dh_scr) = rest[:n_extra], rest[n_extra:]
        dsk_ref = extra[0] if dsk is not None else None
        prev_refs = extra[-3:] if prev is not None else None
        i = pl.program_id(1)

        @pl.when(i == 0)
        def _():
            dh_scr[...] = jnp.zeros_like(dh_scr)

        latent = (chunk_of(i) < s.nl).astype(F32)
        h, dh = hp_ref[0, 0], dh_scr[...]
        hb, dhb = h.astype(MXU_DTYPE), dh.astype(MXU_DTYPE)
        mask, mask_t = s.masks()
        cols = cols_ref[0]
        bb, cb = b_ref[...].astype(MXU_DTYPE), c_ref[...].astype(MXU_DTYPE)
        cbt, bct = _dot(cb, bb, _NT), _dot(bb, cb, _NT)
        b_dh = _dot(bb, dhb, _NT)
        y_off0 = _dot(cb, hb, _NT)
        d_g, d_gt = jnp.zeros((q, q), F32), jnp.zeros((q, q), F32)
        dcs = jnp.zeros((q, HEAD_COLS), F32)
        ddt_x = jnp.zeros((q, HEAD_COLS), F32)
        r_state = jnp.zeros((8, HEAD_COLS), F32)
        dye_tiles, xte_tiles = [], []
        for tt in range(s.ntile):
            sl = slice(tt * LANES, (tt + 1) * LANES)
            dt_b, cs_b, tc_b = s.spread(cols, et_ref, tt)
            ecs_b, te_b = jnp.exp(cs_b), jnp.exp(tc_b)
            xs_t = xs_ref[:, sl]
            x = xs_t * dt_b
            xb = x.astype(MXU_DTYPE)
            d_y = dy_ref[:, sl] * latent
            d_yb = d_y.astype(MXU_DTYPE)
            dx_state = b_dh[:, sl] * te_b
            dx_diag = jnp.zeros((q, LANES), F32)
            for qq in range(s.hpt):
                j = tt * s.hpt + qq
                csc_b = _dot(cols, eh_ref[j])
                csr = csr_ref[j:j + 1, :]
                lm = jnp.exp(jnp.where(mask, csc_b - csr, neg_inf))
                lm_t = jnp.exp(jnp.where(mask_t, csr - csc_b, neg_inf))
                m_t = bct * lm_t
                lanes = s.head_lanes(qq)
                d_yh = jnp.where(lanes, d_y, 0.0).astype(MXU_DTYPE)
                xh = jnp.where(lanes, x, 0.0).astype(MXU_DTYPE)
                d_m = _dot(d_yh, xb, _NT)
                d_mt = _dot(xh, d_yb, _NT)
                r1 = jnp.sum(d_m * (cbt * lm), axis=1, keepdims=True)
                r2 = jnp.sum(d_mt * m_t, axis=1, keepdims=True)
                dcs = dcs + (r1 - r2) * (_iota((1, HEAD_COLS), 1) == j).astype(F32)
                dx_diag = dx_diag + _dot(m_t.astype(MXU_DTYPE), d_yh)
                d_g = d_g + d_m * lm
                d_gt = d_gt + d_mt * lm_t
            d_x = dx_diag + dx_state
            d_xs = d_x * dt_b
            if dsk_ref is not None:
                d_xs = d_xs + d_y * dsk_ref[:, sl]
            if prev_refs is not None:
                d_xs = d_xs + prev_refs[0][:, sl]
            dxs_ref[:, sl] = d_xs
            ddt_x = ddt_x + s.head_sums(d_x * xs_t, tt)
            fed = x * dx_state
            dcs = dcs + s.head_sums(d_y * y_off0[:, sl] * ecs_b - fed, tt)
            fed_rows = jnp.broadcast_to(jnp.sum(fed, axis=0, keepdims=True), (8, LANES))
            r_state = r_state + s.head_sums(fed_rows, tt)
            dye_tiles.append((d_y * ecs_b).astype(MXU_DTYPE))
            xte_tiles.append((x * te_b).astype(MXU_DTYPE))
        dye = dye_tiles[0] if s.ntile == 1 else jnp.concatenate(dye_tiles, axis=1)
        xte = xte_tiles[0] if s.ntile == 1 else jnp.concatenate(xte_tiles, axis=1)
        d_c = _dot(d_g.astype(MXU_DTYPE), bb) + _dot(dye, hb)
        d_b = _dot(d_gt.astype(MXU_DTYPE), cb) + _dot(xte, dhb)
        if prev_refs is not None:
            d_b, d_c = d_b + prev_refs[1][...], d_c + prev_refs[2][...]
        dc_ref[...] = d_c
        db_ref[...] = d_b
        scale = s.state_scale(csr_ref)
        carried = dh * h * scale
        d_tot = jnp.sum(r_state, axis=0, keepdims=True) * 0.125
        for j in range(hpg):
            part = jnp.sum(carried[j * p:(j + 1) * p, :], axis=0, keepdims=True)
            d_tot = d_tot + jnp.sum(part, axis=1, keepdims=True) * (_iota((1, HEAD_COLS), 1) == j).astype(F32)
        dda = _dot_parts(mask_t.astype(jnp.bfloat16), dcs) + d_tot
        ddt_ref[0] = ddt_x + dda * ac_ref[0]
        dda_ref[0] = dda
        dh_scr[...] = dh * scale + _dot(dye, cb, _TN)

    d = "rev" if reverse else "fwd"
    e_tiles, e_heads = s.selectors()
    col = pl.BlockSpec((1, q, HEAD_COLS), lambda gi, i: (gi, chunk_of(i), 0))
    gn = pl.BlockSpec((q, n), lambda gi, i: (chunk_of(i), gi))
    wide = pl.BlockSpec((q, gw), lambda gi, i: (chunk_of(i), gi))
    extra_specs, extra_args, aliases = [], [], {}
    if dsk is not None:
        extra_specs.append(pl.BlockSpec((1, gw), lambda gi, i: (0, gi)))
        extra_args.append(dsk)
    if prev is not None:
        first = 10 + len(extra_args)
        extra_specs += [wide, gn, gn]
        extra_args += list(prev)
        aliases = {first: 0, first + 1: 1, first + 2: 2}
    return pl.pallas_call(
        body, name=f"ssd_bwd_{d}", grid=(s.g, s.ns),
        in_specs=s.in_specs(chunk_of) + [
            pl.BlockSpec((1, 1, HEAD_COLS), lambda gi, i: (s.d * s.g + gi, 0, 0)),
            pl.BlockSpec((q, gw), lambda gi, i: (jnp.minimum(chunk_of(i), s.nl - 1), gi)),
            pl.BlockSpec((1, 1, gw, n), lambda gi, i: (s.ns - 1 - i, gi, 0, 0))] + extra_specs,
        out_specs=[wide, gn, gn, col, col],
        out_shape=[jax.ShapeDtypeStruct((s.t_all, di), F32), jax.ShapeDtypeStruct((s.t_all, s.g * n), F32),
                   jax.ShapeDtypeStruct((s.t_all, s.g * n), F32), jax.ShapeDtypeStruct((s.g, s.t_all, HEAD_COLS), F32),
                   jax.ShapeDtypeStruct((s.g, s.t_all, HEAD_COLS), F32)],
        scratch_shapes=[pltpu.VMEM((gw, n), F32)],
        input_output_aliases=aliases, compiler_params=_cp(("parallel", "arbitrary")),
    )(xbc, xbc, xbc, cols, cs_t, e_tiles, e_heads, a_cols, dy, hprev, *extra_args)


def _ada_fwd(crows, w, b):
    r, d = crows.shape
    ws = w.shape[1]
    tn = _tile(ws, 512, LANES)

    def body(c_ref, w_ref, b_ref, m_ref, s_ref):
        s = _silu(c_ref[...])
        s_ref[...] = s
        m_ref[...] = _dot(s.astype(MXU_DTYPE), w_ref[...].astype(MXU_DTYPE)) + b_ref[...]

    full = pl.BlockSpec((r, d), lambda j: (0, 0))
    return pl.pallas_call(
        body, name="ada_fwd", grid=(ws // tn,),
        in_specs=[full, pl.BlockSpec((d, tn), lambda j: (0, j)), pl.BlockSpec((1, tn), lambda j: (0, j))],
        out_specs=[pl.BlockSpec((r, tn), lambda j: (0, j)), full],
        out_shape=[jax.ShapeDtypeStruct((r, ws), F32), jax.ShapeDtypeStruct((r, d), F32)],
        compiler_params=_cp(("arbitrary",)),
    )(crows, w, b)


def _ada_bwd(s_t, w, dm):
    d, r = s_t.shape
    ws = w.shape[1]
    tn = _tile(ws, 512, LANES)

    def body(st_ref, w_ref, dm_ref, dw_ref, ds_ref):
        dmb = dm_ref[...].astype(MXU_DTYPE)
        dw_ref[...] = _dot(st_ref[...].astype(MXU_DTYPE), dmb)
        _acc(ds_ref, _dot(dmb, w_ref[...].astype(MXU_DTYPE), _NT), pl.program_id(0))

    return pl.pallas_call(
        body, name="ada_bwd", grid=(ws // tn,),
        in_specs=[pl.BlockSpec((d, r), lambda j: (0, 0)), pl.BlockSpec((d, tn), lambda j: (0, j)),
                  pl.BlockSpec((r, tn), lambda j: (0, j))],
        out_specs=[pl.BlockSpec((d, tn), lambda j: (0, j)), pl.BlockSpec((r, d), lambda j: (0, 0))],
        out_shape=[jax.ShapeDtypeStruct((d, ws), F32), jax.ShapeDtypeStruct((r, d), F32)],
        compiler_params=_cp(("arbitrary",)),
    )(s_t, w, dm)


def _adamw(w, g, m, v, name):
    r, c = w.shape
    t = _tile(r, max(8, 300_000 // c), 8)

    def body(w_ref, g_ref, m_ref, v_ref, d_ref, m2_ref, v2_ref):
        g = g_ref[...]
        m2 = ADAM_B1 * m_ref[...] + (1.0 - ADAM_B1) * g
        v2 = ADAM_B2 * v_ref[...] + (1.0 - ADAM_B2) * (g * g)
        m_hat = m2 / (1.0 - ADAM_B1 ** ADAM_STEP)
        v_hat = v2 / (1.0 - ADAM_B2 ** ADAM_STEP)
        d_ref[...] = -ADAM_LR * (m_hat / (jnp.sqrt(v_hat) + ADAM_EPS) + ADAM_WD * w_ref[...])
        m2_ref[...] = m2
        v2_ref[...] = v2

    blk = pl.BlockSpec((t, c), lambda i: (i, 0))
    shape = jax.ShapeDtypeStruct((r, c), F32)
    return pl.pallas_call(
        body, name=name, grid=(r // t,), in_specs=[blk] * 4, out_specs=[blk] * 3, out_shape=[shape] * 3,
        compiler_params=_cp(("parallel",)),
    )(w, g, m, v)


def _sum_devices(gathered):
    rows, w = gathered.shape
    per = rows // N_DEV

    def body(g_ref, o_ref):
        total = g_ref[pl.ds(0, per), :]
        for dev in range(1, N_DEV):
            total = total + g_ref[pl.ds(dev * per, per), :]
        o_ref[...] = total

    return pl.pallas_call(
        body, name="sum_devices", out_shape=jax.ShapeDtypeStruct((per, w), F32),
        in_specs=[pl.BlockSpec(memory_space=pltpu.VMEM)], out_specs=pl.BlockSpec(memory_space=pltpu.VMEM),
        compiler_params=_cp(),
    )(gathered)


def _c_ctx_grad(parts, c_ctx):
    rows, d = parts.shape
    per = rows // N_DEV

    def body(p_ref, c_ref, o_ref):
        total = p_ref[pl.ds(0, 1), :]
        for chip in range(1, N_SHARD):
            total = total + p_ref[pl.ds(2 * chip * per, 1), :]
        _, vjp = jax.vjp(_silu, c_ref[...])
        o_ref[...] = vjp(total)[0]

    return pl.pallas_call(
        body, name="c_ctx_grad", out_shape=jax.ShapeDtypeStruct((1, d), F32),
        in_specs=[pl.BlockSpec(memory_space=pltpu.VMEM)] * 2, out_specs=pl.BlockSpec(memory_space=pltpu.VMEM),
        compiler_params=_cp(),
    )(parts, c_ctx)


def _pad_rows(a, rows, width):
    return jnp.pad(a, ((0, rows - a.shape[0]), (0, width - a.shape[1])))


def _pack(vectors, quantum):
    flat = jnp.concatenate([v.reshape(-1) for v in vectors])
    return jnp.pad(flat, (0, -flat.shape[0] % quantum))


def kernel(x, c, ctx, c_ctx, w_mod, b_mod, norm_mix, w_in, ssm_conv_w, ssm_conv_b, dt_bias, a_log, d_skip, ssm_norm, cf_conv_w, cf_conv_b, cf_ln_g, cf_ln_b, w_proj_a, w_proj_b, w_out, norm_ffn, w_ffn_gate, w_ffn_up, w_ffn_down, norm_final, loss_target, m_c_ctx, m_w_mod, m_b_mod, m_norm_mix, m_w_in, m_ssm_conv_w, m_ssm_conv_b, m_dt_bias, m_a_log, m_d_skip, m_ssm_norm, m_cf_conv_w, m_cf_conv_b, m_cf_ln_g, m_cf_ln_b, m_w_proj_a, m_w_proj_b, m_w_out, m_norm_ffn, m_w_ffn_gate, m_w_ffn_up, m_w_ffn_down, m_norm_final, v_c_ctx, v_w_mod, v_b_mod, v_norm_mix, v_w_in, v_ssm_conv_w, v_ssm_conv_b, v_dt_bias, v_a_log, v_d_skip, v_ssm_norm, v_cf_conv_w, v_cf_conv_b, v_cf_ln_g, v_cf_ln_b, v_w_proj_a, v_w_proj_b, v_w_out, v_norm_ffn, v_w_ffn_gate, v_w_ffn_up, v_w_ffn_down, v_norm_final):
    l, d = x.shape[1], x.shape[2]
    lc = ctx.shape[1]
    t_all = l + lc
    di = ssm_norm.shape[-1]
    h = d_skip.shape[-1]
    p = di // h
    g, n = SSM_GROUPS, SSM_STATE
    hpg = h // g
    conv_dim = di + 2 * g * n
    df = w_ffn_down.shape[1] * N_SHARD
    assert 2 * h == LANES and d % (2 * LANES) == 0

    my_x, my_y, my_c = _mesh_pos()
    chip = 2 * my_x + my_y
    dev = 2 * chip + my_c

    x2, ctx2, tgt = x[0], ctx[0], loss_target[0]
    row = lambda a: a.reshape(1, -1)

    cw_shard, cfw_shard = ssm_conv_w[0], cf_conv_w[0]
    k5, k31 = cw_shard.shape[0], cfw_shard.shape[0]
    r5, r31 = -(-k5 // 8) * 8, -(-k31 // 8) * 8
    wp = max(d, cw_shard.shape[1], cfw_shard.shape[1])
    packed = jnp.concatenate([_pad_rows(c, 8, wp), _pad_rows(cw_shard, r5, wp), _pad_rows(cfw_shard, r31, wp)], axis=0)
    got = _allgather_small(packed, "ag_params").reshape(N_DEV, 8 + r5 + r31, wp)
    c_all = got[:, 0, :d]
    conv_w = got[0::2, 8:8 + k5, :cw_shard.shape[1]].transpose(1, 0, 2).reshape(k5, conv_dim)
    cf_w = got[0::2, 8 + r5:8 + r5 + k31, :cfw_shard.shape[1]].transpose(1, 0, 2).reshape(k31, d)

    ws = w_mod.shape[2]
    crows = jnp.concatenate([c_all, row(c_ctx), jnp.zeros((7, d), F32)], axis=0)
    b_mod_mine = lax.dynamic_slice(b_mod, (0, chip * ws), (1, ws))
    m_part, s_rows = _ada_fwd(crows, w_mod[0], b_mod_mine)
    m_full = _allgather_small(m_part, "ag_mod").reshape(N_DEV, 16, ws)[0::2].transpose(1, 0, 2).reshape(16, N_SHARD * ws)
    m_lat = lax.dynamic_slice(m_full, (dev, 0), (1, 6 * d))
    sh1, sc1, g1, sh2, sc2, g2 = [m_lat[:, i * d:(i + 1) * d] for i in range(6)]
    csh1, csc1 = m_full[8:9, 0:d], m_full[8:9, d:2 * d]

    shards = [w_in[0].T, w_ffn_gate[0].T, w_ffn_up[0].T, w_proj_a[0], w_proj_b[0], w_out[0], w_ffn_down[0]]
    win_t, wg_t, wu_t, wpa, wpb, wout, wdn = _allgather_rows([s.astype(WIRE_DTYPE) for s in shards], "ag_weights")
    o_xbc, o_dt, o_glu, o_gates = di, di + conv_dim, di + conv_dim + 2 * h, di + conv_dim + 2 * h + 2 * d
    win_work = jnp.concatenate([win_t[:o_xbc], win_t[o_glu:], win_t[o_xbc:o_dt], win_t[o_dt:o_glu]], axis=0)
    c_u, c_ga, c_xbc, c_dt = di, di + 2 * d, di + 4 * d, di + 4 * d + conv_dim
    wgu = jnp.concatenate([wg_t, wu_t], axis=0)

    nm = norm_mix
    hx = _mod_fwd(x2, ctx2, nm, sc1, sh1, csc1, csh1)
    proj = _matmul(hx, win_work, tb=True, tm=768, tn=640, name="mm_proj")
    xbc = _conv5_fwd(proj, conv_w, ssm_conv_b, l, lc, c_xbc, conv_dim)
    a = -jnp.exp(a_log.reshape(1, 2 * h))
    dt, cs, tc, cs_t = _dt_fwd(proj, dt_bias.reshape(1, 2 * h), a, c_dt // LANES)
    cols = _scan_columns(dt, cs, tc, 2 * g, hpg)
    a_cols = jnp.pad(a.reshape(2 * g, 1, hpg), ((0, 0), (0, 0), (0, HEAD_COLS - hpg)))
    y_f, hp_f = _ssd_fwd(xbc, cols, cs_t, l, lc, di, p, False)
    y_r, hp_r = _ssd_fwd(xbc, cols, cs_t, l, lc, di, p, True)
    dsk = jnp.repeat(d_skip.reshape(h), p).reshape(1, di)
    ya_in = _gate_fwd(y_f, y_r, xbc, proj, dsk, ssm_norm, l, di)
    y_a = _matmul(ya_in, wpa, tk=di, name="mm_ya")
    u_blk = c_u // d
    cv = _conv31_fwd(proj, cf_w, cf_conv_b, l, d, u_blk)
    cf = _ln_fwd(cv, cf_ln_g, cf_ln_b)
    y_b = _matmul(cf, wpb, name="mm_yb")
    ga_blk = c_ga // d
    merged = _merge_fwd(y_a, y_b, proj, ga_blk)
    mix = _matmul(merged, wout, name="mm_mix")
    x1, hx2 = _res_fwd(x2, mix, g1, norm_ffn, sc2, sh2)
    gu = _matmul(hx2, wgu, tb=True, tn=_tile(df, 1024, LANES), name="mm_gu")
    act = _swiglu_fwd(gu, df)
    dn = _matmul(act, wdn, tk=_tile(df, 2048, LANES), name="mm_dn")
    loss, dx1, ddn, dg2, d_norm_final = _loss_and_grads(x1, dn, g2, row(norm_final), tgt)

    dact = _matmul(ddn, wdn, tb=True, tn=_tile(df, 1024, LANES), name="mm_dact")
    dw_dn = _matmul(act, ddn, ta=True, tn=d, tk=1024, name="mm_dw_dn")
    dgu = _swiglu_bwd(gu, dact, df)
    dhx2 = _matmul(dgu, wgu, tk=_tile(2 * df, 2048, LANES), name="mm_dhx2")
    dw_gu = _matmul(dgu, hx2, ta=True, tn=d, tk=1024, name="mm_dw_gu")
    dx_res, dmix, dg1, d_norm_ffn, dsc2, dsh2 = _res_bwd(x2, mix, g1, norm_ffn, sc2, sh2, dx1, dhx2)
    dmerged = _matmul(dmix, wout, tb=True, name="mm_dmerged")
    dw_out = _matmul(merged, dmix, ta=True, tn=d, tk=1024, name="mm_dw_out")
    dya, dyb, dga, dgb = _merge_bwd(y_a, y_b, proj, ga_blk, dmerged, lc)
    dcf = _matmul(dyb, wpb, tb=True, name="mm_dcf")
    dw_pb = _matmul(cf, dyb, ta=True, tn=d, tk=1024, name="mm_dw_pb")
    dcv, d_ln_g, d_ln_b = _ln_bwd(cv, cf_ln_g, cf_ln_b, dcf)
    du, dv, d_cf_w, d_cf_b = _conv31_bwd(proj, cf_w, dcv, l, lc, d, u_blk)
    dya_in = _matmul(dya, wpa, tb=True, tn=_tile(di, 1024, LANES), name="mm_dya_in")
    dw_pa = _matmul(ya_in, dya, ta=True, tn=d, tk=1024, name="mm_dw_pa")
    dy, dz, ddsk, d_ssm_norm = _gate_bwd(y_f, y_r, xbc, proj, dsk, ssm_norm, dya_in, l, lc, di)
    dxs_f, db_f, dc_f, ddt_f, dda_f = _ssd_bwd(xbc, cols, cs_t, a_cols, dy, hp_f, l, lc, di, p, False, dsk=dsk)
    dxs, db, dc, ddt_r, dda_r = _ssd_bwd(xbc, cols, cs_t, a_cols, dy, hp_r, l, lc, di, p, True, prev=(dxs_f, db_f, dc_f))
    dxs_raw, dcw_x, dcb_x = _conv5_bwd(proj, conv_w, ssm_conv_b, [dxs], l, lc, c_xbc, 0, di)
    db_raw, dcw_b, dcb_b = _conv5_bwd(proj, conv_w, ssm_conv_b, [db], l, lc, c_xbc, di, g * n)
    dc_raw, dcw_c, dcb_c = _conv5_bwd(proj, conv_w, ssm_conv_b, [dc], l, lc, c_xbc, di + g * n, g * n)
    d_conv_w = jnp.concatenate([dcw_x, dcw_b, dcw_c], axis=1)
    d_conv_b = jnp.concatenate([dcb_x, dcb_b, dcb_c], axis=1)
    heads = lambda f, r: jnp.concatenate([t[:, :, :hpg].transpose(1, 0, 2).reshape(t_all, h) for t in (f, r)], axis=1)
    ddt_raw, d_dt_bias, dda_dt = _dt_bwd(proj, dt_bias.reshape(1, 2 * h), dt, heads(ddt_f, ddt_r), heads(dda_f, dda_r), c_dt // LANES)
    d_a_log = dda_dt * a
    dproj = jnp.concatenate([dz, du, dv, dga, dgb, dxs_raw, db_raw, dc_raw, ddt_raw], axis=1)
    dhx = _matmul(dproj, win_work, tm=768, tk=_tile(win_work.shape[0], 4096, LANES), name="mm_dhx")
    dw_in_work = _matmul(dproj, hx, ta=True, tm=640, tn=d, tk=768, name="mm_dw_in")
    grad_x, d_norm_mix, dsc1, dsh1, dcsc1, dcsh1 = _mod_bwd(x2, ctx2, nm, sc1, sh1, csc1, csh1, dhx, dx_res)

    dw_in_t = jnp.concatenate([dw_in_work[:c_u], dw_in_work[c_xbc:], dw_in_work[c_u:c_xbc]], axis=0)
    reduced = _reduce_scatter([dw_in_t, dw_gu[:df], dw_gu[df:], dw_pa, dw_pb, dw_out, dw_dn])
    g_in, g_gate, g_up = reduced[0].T, reduced[1].T, reduced[2].T
    g_pa, g_pb, g_out, g_dn = reduced[3:]

    zeros_d = jnp.zeros((1, d), F32)
    dm_lat = jnp.concatenate([dsh1, dsc1, dg1, dsh2, dsc2, dg2], axis=1)
    dm_ctx = jnp.concatenate([dcsh1, dcsc1] + [zeros_d] * 4, axis=1)
    d_d_skip = ddsk.reshape(h, p).sum(axis=1)
    replicated = [dm_lat + dm_ctx, d_norm_mix, d_conv_b, d_dt_bias, d_a_log, d_d_skip, d_ssm_norm, d_cf_b, d_ln_g, d_ln_b,
                  d_norm_ffn, d_norm_final]
    rep_w = [b_mod, norm_mix, ssm_conv_b, dt_bias, a_log, d_skip, ssm_norm, cf_conv_b, cf_ln_g, cf_ln_b, norm_ffn, norm_final]
    rep_m = [m_b_mod, m_norm_mix, m_ssm_conv_b, m_dt_bias, m_a_log, m_d_skip, m_ssm_norm, m_cf_conv_b, m_cf_ln_g, m_cf_ln_b,
             m_norm_ffn, m_norm_final]
    rep_v = [v_b_mod, v_norm_mix, v_ssm_conv_b, v_dt_bias, v_a_log, v_d_skip, v_ssm_norm, v_cf_conv_b, v_cf_ln_g, v_cf_ln_b,
             v_norm_ffn, v_norm_final]
    quantum = 8 * LANES
    rep_flat = _pack(replicated, quantum)
    n_rep = rep_flat.shape[0]
    summed_parts = [rep_flat, _pack([d_conv_w, d_cf_w, dm_ctx], quantum)]
    n_sum = n_rep + summed_parts[1].shape[0]
    everything = jnp.concatenate(summed_parts + [_pack([dm_lat], quantum)])
    gathered = _allgather_small(everything.reshape(8, -1), "ag_small_grads")
    w8 = gathered.shape[1]
    summed = _sum_devices(gathered).reshape(-1)
    dm_lat_all = gathered.reshape(N_DEV, 8 * w8)[:, n_sum:n_sum + 6 * d]
    off = n_rep
    g_conv_w_full = summed[off:off + k5 * conv_dim].reshape(k5, conv_dim)
    off += k5 * conv_dim
    g_cf_w_full = summed[off:off + k31 * d].reshape(k31, d)
    off += k31 * d
    dm_ctx_all = summed[off:off + 6 * d].reshape(1, 6 * d)
    g_conv_w = lax.dynamic_slice(g_conv_w_full, (0, chip * cw_shard.shape[1]), cw_shard.shape)
    g_cf_w = lax.dynamic_slice(g_cf_w_full, (0, chip * cfw_shard.shape[1]), cfw_shard.shape)

    dm_rows = jnp.concatenate([dm_lat_all, dm_ctx_all, jnp.zeros((7, 6 * d), F32)], axis=0)
    dm_mine = lax.dynamic_slice(dm_rows, (0, chip * ws), (16, ws))
    g_w_mod, ds_part = _ada_bwd(s_rows.T, w_mod[0], dm_mine)
    ds_all = _allgather_small(ds_part[8:16], "ag_c_ctx")
    g_c_ctx = _c_ctx_grad(ds_all, row(c_ctx))

    grads, deltas, new_ms, new_vs = {}, {}, {}, {}

    def update(name, w2, g2, m2, v2, shape):
        dl, mm, vv = _adamw(w2, g2, m2, v2, f"adamw_{name}")
        grads[name], deltas[name], new_ms[name], new_vs[name] = (t.reshape(shape) for t in (g2, dl, mm, vv))

    for name, w_, g_, m_, v_ in [
            ("w_mod", w_mod, g_w_mod, m_w_mod, v_w_mod), ("w_in", w_in, g_in, m_w_in, v_w_in),
            ("ssm_conv_w", ssm_conv_w, g_conv_w, m_ssm_conv_w, v_ssm_conv_w),
            ("cf_conv_w", cf_conv_w, g_cf_w, m_cf_conv_w, v_cf_conv_w),
            ("w_proj_a", w_proj_a, g_pa, m_w_proj_a, v_w_proj_a), ("w_proj_b", w_proj_b, g_pb, m_w_proj_b, v_w_proj_b),
            ("w_out", w_out, g_out, m_w_out, v_w_out), ("w_ffn_gate", w_ffn_gate, g_gate, m_w_ffn_gate, v_w_ffn_gate),
            ("w_ffn_up", w_ffn_up, g_up, m_w_ffn_up, v_w_ffn_up), ("w_ffn_down", w_ffn_down, g_dn, m_w_ffn_down, v_w_ffn_down)]:
        update(name, w_[0], g_, m_[0], v_[0], w_.shape)
    update("c_ctx", row(c_ctx), g_c_ctx, row(m_c_ctx), row(v_c_ctx), c_ctx.shape)

    rep_names = ["b_mod", "norm_mix", "ssm_conv_b", "dt_bias", "a_log", "d_skip", "ssm_norm", "cf_conv_b", "cf_ln_g", "cf_ln_b",
                 "norm_ffn", "norm_final"]
    as8 = lambda vs: _pack(vs, quantum).reshape(8, -1)
    g8 = summed[:n_rep].reshape(8, -1)
    d8, m8, v8 = _adamw(as8(rep_w), g8, as8(rep_m), as8(rep_v), "adamw_replicated")
    off = 0
    for name, w_ in zip(rep_names, rep_w):
        size = w_.size
        for store, packed8 in ((grads, g8), (deltas, d8), (new_ms, m8), (new_vs, v8)):
            store[name] = packed8.reshape(-1)[off:off + size].reshape(w_.shape)
        off += size

    order = ["c_ctx", "w_mod", "b_mod", "norm_mix", "w_in", "ssm_conv_w", "ssm_conv_b", "dt_bias", "a_log", "d_skip", "ssm_norm",
             "cf_conv_w", "cf_conv_b", "cf_ln_g", "cf_ln_b", "w_proj_a", "w_proj_b", "w_out", "norm_ffn", "w_ffn_gate", "w_ffn_up",
             "w_ffn_down", "norm_final"]
    total_loss = lax.psum(loss[0, 0], ("x", "y", "c"))
    return (total_loss, grad_x.reshape(x.shape), *[grads[k] for k in order], *[deltas[k] for k in order],
            *[new_ms[k] for k in order], *[new_vs[k] for k in order])
```

```python
import functools

import jax
import jax.numpy as jnp
from jax import lax
from jax.experimental import pallas as pl
from jax.experimental.pallas import tpu as pltpu

F32 = jnp.float32
MXU_DTYPE = jnp.bfloat16
WIRE_DTYPE = jnp.bfloat16
HI = lax.Precision.HIGHEST
EPS = 1e-6
SSM_GROUPS = 8
SSM_STATE = 128
CHUNK = 128
GRID_W = 64
LANES = 128
HEAD_COLS = 16
VMEM_LIMIT = 52 * 1024 * 1024
ADAM_LR, ADAM_B1, ADAM_B2, ADAM_EPS, ADAM_WD, ADAM_STEP = 0.001, 0.9, 0.999, 1e-08, 0.01, 10
MESH = pl.DeviceIdType.MESH
N_SHARD = 4
N_DEV = 8


def _cp(sem=None):
    kw = dict(vmem_limit_bytes=VMEM_LIMIT)
    if sem is not None:
        kw["dimension_semantics"] = sem
    return pltpu.CompilerParams(**kw)


def _tile(n, target, q):
    best = None
    for t in range(q, min(n, target) + 1, q):
        if n % t == 0:
            best = t
    return best if best is not None else n


def _acc(ref, val, i):
    @pl.when(i == 0)
    def _():
        ref[...] = val

    @pl.when(i > 0)
    def _():
        ref[...] += val


def _bc_spec(w):
    return pl.BlockSpec((1, w), lambda *_: (0, 0))


def _rms(x, w):
    return x * lax.rsqrt(jnp.mean(x * x, axis=-1, keepdims=True) + EPS) * w


def _silu(x):
    return x * jax.nn.sigmoid(x)


def _f_mod(x, w, sc, sh):
    return _rms(x, w) * (1.0 + sc) + sh


def _f_gate(yf, yr, xs, z, dsk, wn):
    return _rms((yf + yr + dsk * xs) * _silu(z), wn)


def _f_ln(cv, g, b):
    mu = jnp.mean(cv, axis=-1, keepdims=True)
    xc = cv - mu
    var = jnp.mean(xc * xc, axis=-1, keepdims=True)
    return _silu(xc * lax.rsqrt(var + EPS) * g + b)


def _f_merge(ya, yb, ga, gb):
    return jax.nn.sigmoid(ga) * ya + jax.nn.sigmoid(gb) * yb


def _f_res(x, mix, g1, wn, sc2, sh2):
    x1 = x + g1 * mix
    return x1, _rms(x1, wn) * (1.0 + sc2) + sh2


def _f_swiglu(gt, up):
    return _silu(gt) * up


def _f_loss(x1, dn, g2, wn, tgt):
    out = _rms(x1 + g2 * dn, wn)
    err = out - tgt
    per_tok = jnp.mean(err * err, axis=-1, keepdims=True)
    return 0.5 * jnp.sum(per_tok, axis=0, keepdims=True)


def _matmul(a, b, *, ta=False, tb=False, out_dtype=F32, tm=512, tn=512, tk=2048, name):
    m, k = (a.shape[1], a.shape[0]) if ta else a.shape
    n = b.shape[0] if tb else b.shape[1]
    assert (b.shape[1] if tb else b.shape[0]) == k, (a.shape, b.shape, ta, tb)
    tm, tn, tk = _tile(m, tm, LANES if ta else 16), _tile(n, tn, LANES), _tile(k, tk, LANES)
    nk = k // tk
    dims = (((0 if ta else 1,), (1 if tb else 0,)), ((), ()))

    def body(a_ref, b_ref, o_ref, *scratch):
        prod = lax.dot_general(a_ref[...].astype(MXU_DTYPE), b_ref[...].astype(MXU_DTYPE), dims,
                               preferred_element_type=F32)
        if nk == 1:
            o_ref[...] = prod.astype(o_ref.dtype)
        else:
            acc = scratch[0]
            kk = pl.program_id(2)
            _acc(acc, prod, kk)

            @pl.when(kk == nk - 1)
            def _():
                o_ref[...] = acc[...].astype(o_ref.dtype)

    a_spec = pl.BlockSpec((tk, tm), lambda i, j, kk: (kk, i)) if ta else pl.BlockSpec((tm, tk), lambda i, j, kk: (i, kk))
    b_spec = pl.BlockSpec((tn, tk), lambda i, j, kk: (j, kk)) if tb else pl.BlockSpec((tk, tn), lambda i, j, kk: (kk, j))
    return pl.pallas_call(
        body, name=name, grid=(m // tm, n // tn, nk), in_specs=[a_spec, b_spec],
        out_specs=pl.BlockSpec((tm, tn), lambda i, j, kk: (i, j)),
        out_shape=jax.ShapeDtypeStruct((m, n), out_dtype),
        scratch_shapes=[] if nk == 1 else [pltpu.VMEM((tm, tn), F32)],
        compiler_params=_cp(("parallel", "parallel", "arbitrary")),
    )(a, b)


def _mesh_pos():
    return lax.axis_index("x"), lax.axis_index("y"), lax.axis_index("c")


def _other_chips(x, y):
    return [(1 - x, y), (x, 1 - y), (1 - x, 1 - y)]


def _allgather_small(v, name):
    m_per, n = v.shape

    def body(x_ref, out_ref, send_sems, recv_sems, local_sem):
        x, y, c = _mesh_pos()
        me, sibling = (x, y, c), (x, y, 1 - c)
        chips = _other_chips(x, y)

        def rows(px, py, pc):
            return out_ref.at[pl.ds((4 * px + 2 * py + pc) * m_per, m_per), :]

        def copy(k, block, to, src=None):
            return pltpu.make_async_remote_copy(
                src_ref=rows(*block) if src is None else src, dst_ref=rows(*block),
                send_sem=send_sems.at[k], recv_sem=recv_sems.at[k], device_id=to, device_id_type=MESH)

        mine = pltpu.make_async_copy(x_ref, rows(*me), local_sem)
        mine.start()
        first = [copy(0, me, sibling, src=x_ref)]
        first += [copy(1 + j, me, (*chip, c), src=x_ref) for j, chip in enumerate(chips)]
        for cp in first:
            cp.start()
        passed = [copy(4 + j, (*chip, c), sibling) for j, chip in enumerate(chips)]
        for j, chip in enumerate(chips):
            copy(1 + j, (*chip, c), me).wait_recv()
            passed[j].start()
        copy(0, sibling, me).wait_recv()
        for j, chip in enumerate(chips):
            copy(4 + j, (*chip, 1 - c), me).wait_recv()
        for cp in first + passed:
            cp.wait_send()
        mine.wait()

    return pl.pallas_call(
        body, name=name, out_shape=jax.ShapeDtypeStruct((N_DEV * m_per, n), v.dtype),
        in_specs=[pl.BlockSpec(memory_space=pltpu.VMEM)], out_specs=pl.BlockSpec(memory_space=pltpu.VMEM),
        scratch_shapes=[pltpu.SemaphoreType.DMA((7,)), pltpu.SemaphoreType.DMA((7,)), pltpu.SemaphoreType.DMA],
        compiler_params=_cp(),
    )(v)


_HBM = pl.BlockSpec(memory_space=pltpu.HBM)


def _allgather_rows(shards, name):
    n = len(shards)

    def body(*refs):
        src, dst = refs[:n], refs[n:2 * n]
        send_sems, recv_sems = refs[2 * n:]
        x, y, c = _mesh_pos()
        chips = _other_chips(x, y)

        def half(i, px, py, pc):
            r = src[i].shape[0]
            return dst[i].at[pl.ds(pl.multiple_of((2 * px + py) * r + pc * (r // 2), 16), r // 2), :]

        def copy(i, k, block, to, own=False):
            r = src[i].shape[0]
            mine = src[i].at[pl.ds(pl.multiple_of(c * (r // 2), 16), r // 2), :]
            return pltpu.make_async_remote_copy(
                src_ref=mine if own else half(i, *block), dst_ref=half(i, *block), send_sem=send_sems.at[6 * i + k],
                recv_sem=recv_sems.at[6 * i + k], device_id=to, device_id_type=MESH)

        first = [copy(i, j, (x, y, c), (*chip, c), own=True) for i in range(n) for j, chip in enumerate(chips)]
        for cp in first:
            cp.start()
        passed = []
        for i in range(n):
            for j, chip in enumerate(chips):
                copy(i, j, (*chip, c), (x, y, c)).wait_recv()
                passed.append(copy(i, 3 + j, (*chip, c), (x, y, 1 - c)))
                passed[-1].start()
        for i in range(n):
            for j, chip in enumerate(chips):
                copy(i, 3 + j, (*chip, 1 - c), (x, y, c)).wait_recv()
        for cp in first + passed:
            cp.wait_send()

    gathered = pl.pallas_call(
        body, name=name,
        out_shape=[jax.ShapeDtypeStruct((N_SHARD * s.shape[0], s.shape[1]), s.dtype) for s in shards],
        in_specs=[_HBM] * n, out_specs=[_HBM] * n,
        scratch_shapes=[pltpu.SemaphoreType.DMA((6 * n,)), pltpu.SemaphoreType.DMA((6 * n,))],
        compiler_params=_cp(),
    )(*shards)
    chip = 2 * lax.axis_index("x") + lax.axis_index("y")
    return [lax.dynamic_update_slice(full, s, (chip * s.shape[0], 0)) for full, s in zip(gathered, shards)]


def _swap_halves(parts, name):
    n = len(parts)

    def body(*refs):
        src, dst = refs[:n], refs[n:2 * n]
        send_sems, recv_sems = refs[2 * n:]
        x, y, c = _mesh_pos()
        copies = [pltpu.make_async_remote_copy(
            src_ref=src[i].at[s, 1 - c], dst_ref=dst[i].at[s], send_sem=send_sems.at[N_SHARD * i + s],
            recv_sem=recv_sems.at[N_SHARD * i + s], device_id=(x, y, 1 - c), device_id_type=MESH)
            for i in range(n) for s in range(N_SHARD)]
        for cp in copies:
            cp.start()
        for cp in copies:
            cp.wait()

    return pl.pallas_call(
        body, name=name,
        out_shape=[jax.ShapeDtypeStruct((N_SHARD,) + p.shape[2:], p.dtype) for p in parts],
        in_specs=[_HBM] * n, out_specs=[_HBM] * n,
        scratch_shapes=[pltpu.SemaphoreType.DMA((N_SHARD * n,)), pltpu.SemaphoreType.DMA((N_SHARD * n,))],
        compiler_params=_cp(),
    )(*parts)


def _scatter_partials(parts, name):
    n = len(parts)

    def body(*refs):
        src, dst = refs[:n], refs[n:2 * n]
        send_sems, recv_sems = refs[2 * n:]
        x, y, c = _mesh_pos()
        chips = _other_chips(x, y)

        def copy(i, j, chip_to):
            return pltpu.make_async_remote_copy(
                src_ref=src[i].at[2 * chip_to[0] + chip_to[1]], dst_ref=dst[i].at[j], send_sem=send_sems.at[3 * i + j],
                recv_sem=recv_sems.at[3 * i + j], device_id=(*chip_to, c), device_id_type=MESH)

        copies = [copy(i, j, chip) for i in range(n) for j, chip in enumerate(chips)]
        for cp in copies:
            cp.start()
        for cp in copies:
            cp.wait()

    return pl.pallas_call(
        body, name=name,
        out_shape=[jax.ShapeDtypeStruct((3,) + p.shape[1:], p.dtype) for p in parts],
        in_specs=[_HBM] * n, out_specs=[_HBM] * n,
        scratch_shapes=[pltpu.SemaphoreType.DMA((3 * n,)), pltpu.SemaphoreType.DMA((3 * n,))],
        compiler_params=_cp(),
    )(*parts)


def _join_halves(halves, name):
    n = len(halves)

    def body(*refs):
        src, dst = refs[:n], refs[n:2 * n]
        send_sems, recv_sems = refs[2 * n:]
        x, y, c = _mesh_pos()
        remote = [pltpu.make_async_remote_copy(
            src_ref=src[i], dst_ref=dst[i].at[c], send_sem=send_sems.at[i], recv_sem=recv_sems.at[i],
            device_id=(x, y, 1 - c), device_id_type=MESH) for i in range(n)]
        for cp in remote:
            cp.start()
        for i in range(n):
            pltpu.make_async_remote_copy(
                src_ref=src[i], dst_ref=dst[i].at[1 - c], send_sem=send_sems.at[i], recv_sem=recv_sems.at[i],
                device_id=(x, y, 1 - c), device_id_type=MESH).wait_recv()
        for cp in remote:
            cp.wait_send()

    joined = pl.pallas_call(
        body, name=name,
        out_shape=[jax.ShapeDtypeStruct((2,) + h.shape, h.dtype) for h in halves],
        in_specs=[_HBM] * n, out_specs=[_HBM] * n,
        scratch_shapes=[pltpu.SemaphoreType.DMA((n,)), pltpu.SemaphoreType.DMA((n,))],
        compiler_params=_cp(),
    )(*halves)
    c = lax.axis_index("c")
    return [lax.dynamic_update_slice(j, h[None], (c, 0, 0)) for j, h in zip(joined, halves)]


def _pair_sum(g, got, name):
    _, _, hr, d = g.shape
    t = _tile(hr, 256, 16)

    def body(g0_ref, g1_ref, got_ref, wire_ref, own_ref):
        x, y, c = _mesh_pos()
        total = jnp.where(c == 0, g0_ref[0, 0], g1_ref[0, 0]) + got_ref[0]
        wire_ref[0] = total.astype(wire_ref.dtype)

        @pl.when(pl.program_id(1) == 2 * x + y)
        def _():
            own_ref[...] = total

    return pl.pallas_call(
        body, name=name, grid=(hr // t, N_SHARD),
        in_specs=[pl.BlockSpec((1, 1, t, d), lambda i, s: (s, 0, i, 0)), pl.BlockSpec((1, 1, t, d), lambda i, s: (s, 1, i, 0)),
                  pl.BlockSpec((1, t, d), lambda i, s: (s, i, 0))],
        out_specs=[pl.BlockSpec((1, t, d), lambda i, s: (s, i, 0)), pl.BlockSpec((t, d), lambda i, s: (i, 0))],
        out_shape=[jax.ShapeDtypeStruct((N_SHARD, hr, d), WIRE_DTYPE), jax.ShapeDtypeStruct((hr, d), F32)],
        compiler_params=_cp(("parallel", "arbitrary")),
    )(g, g, got)


def _sum_partials(own, recv, name):
    hr, d = own.shape
    t = _tile(hr, 256, 16)

    def body(own_ref, recv_ref, o_ref):
        total = own_ref[...]
        for j in range(3):
            total = total + recv_ref[j].astype(F32)
        o_ref[...] = total

    blk = pl.BlockSpec((t, d), lambda i: (i, 0))
    return pl.pallas_call(
        body, name=name, grid=(hr // t,), in_specs=[blk, pl.BlockSpec((3, t, d), lambda i: (0, i, 0))], out_specs=blk,
        out_shape=jax.ShapeDtypeStruct((hr, d), F32), compiler_params=_cp(("parallel",)),
    )(own, recv)


def _reduce_scatter(grads):
    split = [g.reshape(N_SHARD, 2, g.shape[0] // (2 * N_SHARD), g.shape[1]) for g in grads]
    got = _swap_halves(split, "rs_swap_halves")
    sums = [_pair_sum(g, h, f"rs_pair_sum_{i}") for i, (g, h) in enumerate(zip(split, got))]
    recv = _scatter_partials([w for w, _ in sums], "rs_scatter")
    halves = [_sum_partials(own, rv, f"rs_sum_{i}") for i, ((_, own), rv) in enumerate(zip(sums, recv))]
    return [j.reshape(-1, j.shape[-1]) for j in _join_halves(halves, "rs_join_halves")]


def _mod_fwd(x, ctx, nw, sc, sh, csc, csh):
    l, d = x.shape
    lc = ctx.shape[0]
    t = min(256, lc)
    nl, nc = l // t, lc // t

    def body(x_ref, c_ref, nw_ref, sc_ref, sh_ref, csc_ref, csh_ref, o_ref):
        i = pl.program_id(0)

        @pl.when(i < nl)
        def _():
            o_ref[...] = _f_mod(x_ref[...], nw_ref[...], sc_ref[...], sh_ref[...]).astype(o_ref.dtype)

        @pl.when(i >= nl)
        def _():
            o_ref[...] = _f_mod(c_ref[...], nw_ref[...], csc_ref[...], csh_ref[...]).astype(o_ref.dtype)

    return pl.pallas_call(
        body, name="mod_fwd", grid=(nl + nc,),
        in_specs=[pl.BlockSpec((t, d), lambda i: (jnp.minimum(i, nl - 1), 0)),
                  pl.BlockSpec((t, d), lambda i: (jnp.maximum(i - nl, 0), 0))] + [_bc_spec(d)] * 5,
        out_specs=pl.BlockSpec((t, d), lambda i: (i, 0)),
        out_shape=jax.ShapeDtypeStruct((l + lc, d), MXU_DTYPE), compiler_params=_cp(("arbitrary",)),
    )(x, ctx, nw, sc, sh, csc, csh)


def _mod_bwd(x, ctx, nw, sc, sh, csc, csh, dhx, dx_res):
    l, d = x.shape
    lc = ctx.shape[0]
    t = min(256, lc)
    nl, nc = l // t, lc // t

    def body(x_ref, c_ref, nw_ref, sc_ref, sh_ref, csc_ref, csh_ref, dh_ref, dr_ref,
             dx_ref, dnw_ref, dsc_ref, dsh_ref, dcsc_ref, dcsh_ref):
        i = pl.program_id(0)

        @pl.when(i == 0)
        def _():
            for r in (dnw_ref, dsc_ref, dsh_ref, dcsc_ref, dcsh_ref):
                r[...] = jnp.zeros_like(r)

        @pl.when(i < nl)
        def _():
            _, vjp = jax.vjp(_f_mod, x_ref[...], nw_ref[...], sc_ref[...], sh_ref[...])
            dx, dnw, dsc, dsh = vjp(dh_ref[...])
            dx_ref[...] = dx + dr_ref[...]
            dnw_ref[...] += dnw
            dsc_ref[...] += dsc
            dsh_ref[...] += dsh

        @pl.when(i >= nl)
        def _():
            _, vjp = jax.vjp(_f_mod, c_ref[...], nw_ref[...], csc_ref[...], csh_ref[...])
            _, dnw, dsc, dsh = vjp(dh_ref[...])
            dnw_ref[...] += dnw
            dcsc_ref[...] += dsc
            dcsh_ref[...] += dsh

    lat = pl.BlockSpec((t, d), lambda i: (jnp.minimum(i, nl - 1), 0))
    vec = jax.ShapeDtypeStruct((1, d), F32)
    return pl.pallas_call(
        body, name="mod_bwd", grid=(nl + nc,),
        in_specs=[lat, pl.BlockSpec((t, d), lambda i: (jnp.maximum(i - nl, 0), 0))] + [_bc_spec(d)] * 5
        + [pl.BlockSpec((t, d), lambda i: (i, 0)), lat],
        out_specs=[lat] + [_bc_spec(d)] * 5,
        out_shape=[jax.ShapeDtypeStruct((l, d), F32)] + [vec] * 5, compiler_params=_cp(("arbitrary",)),
    )(x, ctx, nw, sc, sh, csc, csh, dhx, dx_res)


def _gate_fwd(yf, yr, xbc, proj, dsk, wn, l, di):
    t = 128

    def body(yf_ref, yr_ref, xs_ref, z_ref, dsk_ref, wn_ref, o_ref):
        o_ref[...] = _f_gate(yf_ref[...], yr_ref[...], xs_ref[...], z_ref[...], dsk_ref[...], wn_ref[...]).astype(o_ref.dtype)

    row = pl.BlockSpec((t, di), lambda i: (i, 0))
    return pl.pallas_call(
        body, name="gate_fwd", grid=(l // t,), in_specs=[row] * 4 + [_bc_spec(di)] * 2, out_specs=row,
        out_shape=jax.ShapeDtypeStruct((l, di), MXU_DTYPE), compiler_params=_cp(("parallel",)),
    )(yf, yr, xbc, proj, dsk, wn)


def _gate_bwd(yf, yr, xbc, proj, dsk, wn, dya, l, lc, di):
    t = 128
    nl, nc = l // t, lc // t

    def body(yf_ref, yr_ref, xs_ref, z_ref, dsk_ref, wn_ref, g_ref, dy_ref, dz_ref, ddsk_ref, dwn_ref):
        i = pl.program_id(0)

        @pl.when(i == 0)
        def _():
            ddsk_ref[...] = jnp.zeros_like(ddsk_ref)
            dwn_ref[...] = jnp.zeros_like(dwn_ref)

        @pl.when(i < nl)
        def _():
            _, vjp = jax.vjp(_f_gate, yf_ref[...], yr_ref[...], xs_ref[...], z_ref[...], dsk_ref[...], wn_ref[...])
            dyf, _, _, dz, ddsk, dwn = vjp(g_ref[...])
            dy_ref[...] = dyf
            dz_ref[...] = dz.astype(dz_ref.dtype)
            ddsk_ref[...] += ddsk
            dwn_ref[...] += dwn

        @pl.when(i >= nl)
        def _():
            dz_ref[...] = jnp.zeros_like(dz_ref)

    lat = pl.BlockSpec((t, di), lambda i: (jnp.minimum(i, nl - 1), 0))
    vec = jax.ShapeDtypeStruct((1, di), F32)
    return pl.pallas_call(
        body, name="gate_bwd", grid=(nl + nc,),
        in_specs=[lat] * 4 + [_bc_spec(di)] * 2 + [lat],
        out_specs=[lat, pl.BlockSpec((t, di), lambda i: (i, 0))] + [_bc_spec(di)] * 2,
        out_shape=[jax.ShapeDtypeStruct((l, di), F32), jax.ShapeDtypeStruct((l + lc, di), MXU_DTYPE)] + [vec] * 2,
        compiler_params=_cp(("arbitrary",)),
    )(yf, yr, xbc, proj, dsk, wn, dya)


def _ln_fwd(cv, g, b):
    l, d = cv.shape
    t = 256

    def body(cv_ref, g_ref, b_ref, o_ref):
        o_ref[...] = _f_ln(cv_ref[...], g_ref[...], b_ref[...]).astype(o_ref.dtype)

    row = pl.BlockSpec((t, d), lambda i: (i, 0))
    return pl.pallas_call(
        body, name="ln_fwd", grid=(l // t,), in_specs=[row] + [_bc_spec(d)] * 2, out_specs=row,
        out_shape=jax.ShapeDtypeStruct((l, d), MXU_DTYPE), compiler_params=_cp(("parallel",)),
    )(cv, g, b)


def _ln_bwd(cv, g, b, dcf):
    l, d = cv.shape
    t = 256

    def body(cv_ref, g_ref, b_ref, dcf_ref, dcv_ref, dg_ref, db_ref):
        _, vjp = jax.vjp(_f_ln, cv_ref[...], g_ref[...], b_ref[...])
        dcv, dg, db = vjp(dcf_ref[...])
        dcv_ref[...] = dcv
        i = pl.program_id(0)
        _acc(dg_ref, dg, i)
        _acc(db_ref, db, i)

    row = pl.BlockSpec((t, d), lambda i: (i, 0))
    vec = jax.ShapeDtypeStruct((1, d), F32)
    return pl.pallas_call(
        body, name="ln_bwd", grid=(l // t,), in_specs=[row] + [_bc_spec(d)] * 2 + [row],
        out_specs=[row] + [_bc_spec(d)] * 2, out_shape=[jax.ShapeDtypeStruct((l, d), F32), vec, vec],
        compiler_params=_cp(("arbitrary",)),
    )(cv, g, b, dcf)


def _merge_fwd(ya, yb, proj, ga_blk):
    l, d = ya.shape
    t = 256

    def body(ya_ref, yb_ref, ga_ref, gb_ref, o_ref):
        o_ref[...] = _f_merge(ya_ref[...], yb_ref[...], ga_ref[...], gb_ref[...]).astype(o_ref.dtype)

    row = pl.BlockSpec((t, d), lambda i: (i, 0))
    return pl.pallas_call(
        body, name="merge_fwd", grid=(l // t,),
        in_specs=[row, row, pl.BlockSpec((t, d), lambda i: (i, ga_blk)), pl.BlockSpec((t, d), lambda i: (i, ga_blk + 1))],
        out_specs=row, out_shape=jax.ShapeDtypeStruct((l, d), MXU_DTYPE), compiler_params=_cp(("parallel",)),
    )(ya, yb, proj, proj)


def _merge_bwd(ya, yb, proj, ga_blk, dmerged, lc):
    l, d = ya.shape
    t = min(256, lc)
    nl, nc = l // t, lc // t

    def body(ya_ref, yb_ref, ga_ref, gb_ref, g_ref, dya_ref, dyb_ref, dga_ref, dgb_ref):
        i = pl.program_id(0)

        @pl.when(i < nl)
        def _():
            _, vjp = jax.vjp(_f_merge, ya_ref[...], yb_ref[...], ga_ref[...], gb_ref[...])
            dya, dyb, dga, dgb = vjp(g_ref[...])
            dya_ref[...] = dya.astype(dya_ref.dtype)
            dyb_ref[...] = dyb.astype(dyb_ref.dtype)
            dga_ref[...] = dga.astype(dga_ref.dtype)
            dgb_ref[...] = dgb.astype(dgb_ref.dtype)

        @pl.when(i >= nl)
        def _():
            dga_ref[...] = jnp.zeros_like(dga_ref)
            dgb_ref[...] = jnp.zeros_like(dgb_ref)

    lat = pl.BlockSpec((t, d), lambda i: (jnp.minimum(i, nl - 1), 0))
    full = pl.BlockSpec((t, d), lambda i: (i, 0))
    return pl.pallas_call(
        body, name="merge_bwd", grid=(nl + nc,),
        in_specs=[lat, lat, pl.BlockSpec((t, d), lambda i: (jnp.minimum(i, nl - 1), ga_blk)),
                  pl.BlockSpec((t, d), lambda i: (jnp.minimum(i, nl - 1), ga_blk + 1)), lat],
        out_specs=[lat, lat, full, full],
        out_shape=[jax.ShapeDtypeStruct((l, d), MXU_DTYPE)] * 2 + [jax.ShapeDtypeStruct((l + lc, d), MXU_DTYPE)] * 2,
        compiler_params=_cp(("arbitrary",)),
    )(ya, yb, proj, proj, dmerged)


def _res_fwd(x, mix, g1, wn, sc2, sh2):
    l, d = x.shape
    t = 256

    def body(x_ref, m_ref, g1_ref, wn_ref, sc_ref, sh_ref, x1_ref, hx_ref):
        x1, hx = _f_res(x_ref[...], m_ref[...], g1_ref[...], wn_ref[...], sc_ref[...], sh_ref[...])
        x1_ref[...] = x1
        hx_ref[...] = hx.astype(hx_ref.dtype)

    row = pl.BlockSpec((t, d), lambda i: (i, 0))
    return pl.pallas_call(
        body, name="res_fwd", grid=(l // t,), in_specs=[row, row] + [_bc_spec(d)] * 4, out_specs=[row, row],
        out_shape=[jax.ShapeDtypeStruct((l, d), F32), jax.ShapeDtypeStruct((l, d), MXU_DTYPE)],
        compiler_params=_cp(("parallel",)),
    )(x, mix, g1, wn, sc2, sh2)


def _res_bwd(x, mix, g1, wn, sc2, sh2, dx1, dhx2):
    l, d = x.shape
    t = 256

    def body(x_ref, m_ref, g1_ref, wn_ref, sc_ref, sh_ref, dx1_ref, dh_ref, dx_ref, dm_ref, dg1_ref, dwn_ref, dsc_ref, dsh_ref):
        _, vjp = jax.vjp(_f_res, x_ref[...], m_ref[...], g1_ref[...], wn_ref[...], sc_ref[...], sh_ref[...])
        dx, dm, dg1, dwn, dsc, dsh = vjp((dx1_ref[...], dh_ref[...]))
        dx_ref[...] = dx
        dm_ref[...] = dm.astype(dm_ref.dtype)
        i = pl.program_id(0)
        _acc(dg1_ref, dg1, i)
        _acc(dwn_ref, dwn, i)
        _acc(dsc_ref, dsc, i)
        _acc(dsh_ref, dsh, i)

    row = pl.BlockSpec((t, d), lambda i: (i, 0))
    vec = jax.ShapeDtypeStruct((1, d), F32)
    return pl.pallas_call(
        body, name="res_bwd", grid=(l // t,), in_specs=[row, row] + [_bc_spec(d)] * 4 + [row, row],
        out_specs=[row, row] + [_bc_spec(d)] * 4,
        out_shape=[jax.ShapeDtypeStruct((l, d), F32), jax.ShapeDtypeStruct((l, d), MXU_DTYPE)] + [vec] * 4,
        compiler_params=_cp(("arbitrary",)),
    )(x, mix, g1, wn, sc2, sh2, dx1, dhx2)


def _swiglu_fwd(gu, df):
    l = gu.shape[0]
    t = 256

    def body(g_ref, u_ref, o_ref):
        o_ref[...] = _f_swiglu(g_ref[...], u_ref[...]).astype(o_ref.dtype)

    return pl.pallas_call(
        body, name="swiglu_fwd", grid=(l // t,),
        in_specs=[pl.BlockSpec((t, df), lambda i: (i, 0)), pl.BlockSpec((t, df), lambda i: (i, 1))],
        out_specs=pl.BlockSpec((t, df), lambda i: (i, 0)),
        out_shape=jax.ShapeDtypeStruct((l, df), MXU_DTYPE), compiler_params=_cp(("parallel",)),
    )(gu, gu)


def _swiglu_bwd(gu, dact, df):
    l = gu.shape[0]
    t = 256

    lo, hi = pl.BlockSpec((t, df), lambda i: (i, 0)), pl.BlockSpec((t, df), lambda i: (i, 1))
    dgu = jax.ShapeDtypeStruct((l, 2 * df), MXU_DTYPE)

    def body(g_ref, u_ref, da_ref, dgu_ref):
        _, vjp = jax.vjp(_f_swiglu, g_ref[...], u_ref[...])
        dg, du = vjp(da_ref[...])
        dgu_ref[:, :df] = dg.astype(dgu_ref.dtype)
        dgu_ref[:, df:] = du.astype(dgu_ref.dtype)

    return pl.pallas_call(
        body, name="swiglu_bwd", grid=(l // t,), in_specs=[lo, hi, lo],
        out_specs=pl.BlockSpec((t, 2 * df), lambda i: (i, 0)), out_shape=dgu, compiler_params=_cp(("parallel",)),
    )(gu, gu, dact)


def _loss_and_grads(x1, dn, g2, wn, tgt):
    l, d = x1.shape
    t = 256

    def body(x1_ref, dn_ref, g2_ref, wn_ref, t_ref, loss_ref, dx_ref, ddn_ref, dg2_ref, dwn_ref):
        loss, vjp = jax.vjp(lambda a, b, c, e: _f_loss(a, b, c, e, t_ref[...]), x1_ref[...], dn_ref[...], g2_ref[...], wn_ref[...])
        dx, ddn, dg2, dwn = vjp(jnp.ones((1, 1), F32))
        dx_ref[...] = dx
        ddn_ref[...] = ddn.astype(ddn_ref.dtype)
        i = pl.program_id(0)
        _acc(loss_ref, loss, i)
        _acc(dg2_ref, dg2, i)
        _acc(dwn_ref, dwn, i)

    row = pl.BlockSpec((t, d), lambda i: (i, 0))
    vec = jax.ShapeDtypeStruct((1, d), F32)
    return pl.pallas_call(
        body, name="loss_and_grads", grid=(l // t,), in_specs=[row, row] + [_bc_spec(d)] * 2 + [row],
        out_specs=[pl.BlockSpec((1, 1), lambda i: (0, 0)), row, row] + [_bc_spec(d)] * 2,
        out_shape=[jax.ShapeDtypeStruct((1, 1), F32), jax.ShapeDtypeStruct((l, d), F32),
                   jax.ShapeDtypeStruct((l, d), MXU_DTYPE), vec, vec],
        compiler_params=_cp(("arbitrary",)),
    )(x1, dn, g2, wn, tgt)


PAD = 8


def _conv5_taps(s_ref, w_ref, l, lc, width):
    half = width // 2
    lat = sum(w_ref[k:k + 1, :] * s_ref[pl.ds(PAD + k - half, l), :] for k in range(width))
    ctx = sum(w_ref[k:k + 1, :] * s_ref[pl.ds(2 * PAD + l + k - half, lc), :] for k in range(width))
    return lat, ctx


def _fill_padded(s_ref, lat, ctx, l, lc):
    zeros = jnp.zeros((PAD, s_ref.shape[1]), F32)
    s_ref[pl.ds(0, PAD), :] = zeros
    s_ref[pl.ds(PAD, l), :] = lat
    s_ref[pl.ds(PAD + l, PAD), :] = zeros
    s_ref[pl.ds(2 * PAD + l, lc), :] = ctx
    s_ref[pl.ds(2 * PAD + l + lc, PAD), :] = zeros


def _conv5_fwd(proj, w, b, l, lc, col0, ncols):
    t_all = l + lc
    cw = LANES
    blk0 = col0 // cw
    width = w.shape[0]

    def body(x_ref, w_ref, b_ref, o_ref, s_ref):
        _fill_padded(s_ref, x_ref[pl.ds(0, l), :], x_ref[pl.ds(l, lc), :], l, lc)
        lat, ctx = _conv5_taps(s_ref, w_ref, l, lc, width)
        o_ref[pl.ds(0, l), :] = _silu(lat + b_ref[...])
        o_ref[pl.ds(l, lc), :] = _silu(ctx + b_ref[...])

    return pl.pallas_call(
        body, name="conv5_fwd", grid=(ncols // cw,),
        in_specs=[pl.BlockSpec((t_all, cw), lambda j: (0, blk0 + j)), pl.BlockSpec((width, cw), lambda j: (0, j)),
                  pl.BlockSpec((1, cw), lambda j: (0, j))],
        out_specs=pl.BlockSpec((t_all, cw), lambda j: (0, j)),
        out_shape=jax.ShapeDtypeStruct((t_all, ncols), F32),
        scratch_shapes=[pltpu.VMEM((t_all + 3 * PAD, cw), F32)], compiler_params=_cp(("parallel",)),
    )(proj, w, b)


def _conv5_bwd(proj, w, b, cots, l, lc, col0, seg0, ncols):
    t_all = l + lc
    cw = LANES
    blk0, sblk0 = col0 // cw, seg0 // cw
    width = w.shape[0]
    half = width // 2
    nc = len(cots)

    def body(*refs):
        x_ref, w_ref, b_ref = refs[:3]
        cot_refs = refs[3:3 + nc]
        dx_ref, dw_ref, db_ref, s_ref = refs[3 + nc:]
        x_lat, x_ctx = x_ref[pl.ds(0, l), :], x_ref[pl.ds(l, lc), :]
        _fill_padded(s_ref, x_lat, x_ctx, l, lc)
        pre_lat, pre_ctx = _conv5_taps(s_ref, w_ref, l, lc, width)
        g = sum(c[...] for c in cot_refs)

        def through_silu(pre, cot):
            _, vjp = jax.vjp(_silu, pre + b_ref[...])
            return vjp(cot)[0]

        d_lat = through_silu(pre_lat, g[:l])
        d_ctx = through_silu(pre_ctx, g[l:])
        db_ref[...] = jnp.sum(d_lat, axis=0, keepdims=True) + jnp.sum(d_ctx, axis=0, keepdims=True)
        for k in range(width):
            dw_ref[k:k + 1, :] = (
                jnp.sum(d_lat * s_ref[pl.ds(PAD + k - half, l), :], axis=0, keepdims=True)
                + jnp.sum(d_ctx * s_ref[pl.ds(2 * PAD + l + k - half, lc), :], axis=0, keepdims=True))
        _fill_padded(s_ref, d_lat, d_ctx, l, lc)
        dx_lat = sum(w_ref[k:k + 1, :] * s_ref[pl.ds(PAD - (k - half), l), :] for k in range(width))
        dx_ctx = sum(w_ref[k:k + 1, :] * s_ref[pl.ds(2 * PAD + l - (k - half), lc), :] for k in range(width))
        dx_ref[pl.ds(0, l), :] = dx_lat.astype(dx_ref.dtype)
        dx_ref[pl.ds(l, lc), :] = dx_ctx.astype(dx_ref.dtype)

    col = pl.BlockSpec((t_all, cw), lambda j: (0, j))
    return pl.pallas_call(
        body, name=f"conv5_bwd_{seg0}", grid=(ncols // cw,),
        in_specs=[pl.BlockSpec((t_all, cw), lambda j: (0, blk0 + sblk0 + j)),
                  pl.BlockSpec((width, cw), lambda j: (0, sblk0 + j)), pl.BlockSpec((1, cw), lambda j: (0, sblk0 + j))]
        + [col] * nc,
        out_specs=[col, pl.BlockSpec((width, cw), lambda j: (0, j)), pl.BlockSpec((1, cw), lambda j: (0, j))],
        out_shape=[jax.ShapeDtypeStruct((t_all, ncols), MXU_DTYPE), jax.ShapeDtypeStruct((width, ncols), F32),
                   jax.ShapeDtypeStruct((1, ncols), F32)],
        scratch_shapes=[pltpu.VMEM((t_all + 3 * PAD, cw), F32)], compiler_params=_cp(("parallel",)),
    )(proj, w, b, *cots)


def _conv31_fwd(proj, w, b, l, d, u_blk):
    cw = LANES
    width = w.shape[0]
    reach = (width // 2) * GRID_W
    nb = d // cw

    def body(u_ref, v_ref, w_ref, b_ref, o_ref, s_ref):
        s_ref[pl.ds(0, reach), :] = jnp.zeros((reach, cw), F32)
        s_ref[pl.ds(reach, l), :] = u_ref[...] * jax.nn.sigmoid(v_ref[...])
        s_ref[pl.ds(reach + l, reach), :] = jnp.zeros((reach, cw), F32)
        o_ref[...] = sum(w_ref[k:k + 1, :] * s_ref[pl.ds(k * GRID_W, l), :] for k in range(width)) + b_ref[...]

    return pl.pallas_call(
        body, name="conv31_fwd", grid=(nb,),
        in_specs=[pl.BlockSpec((l, cw), lambda j: (0, u_blk * nb + j)), pl.BlockSpec((l, cw), lambda j: (0, (u_blk + 1) * nb + j)),
                  pl.BlockSpec((width, cw), lambda j: (0, j)), pl.BlockSpec((1, cw), lambda j: (0, j))],
        out_specs=pl.BlockSpec((l, cw), lambda j: (0, j)), out_shape=jax.ShapeDtypeStruct((l, d), F32),
        scratch_shapes=[pltpu.VMEM((l + 2 * reach, cw), F32)], compiler_params=_cp(("parallel",)),
    )(proj, proj, w, b)


def _conv31_bwd(proj, w, dcv, l, lc, d, u_blk):
    cw = LANES
    width = w.shape[0]
    reach = (width // 2) * GRID_W
    nb = d // cw
    t_all = l + lc

    def body(u_ref, v_ref, w_ref, g_ref, du_ref, dv_ref, dw_ref, db_ref, s_ref):
        zeros = jnp.zeros((reach, cw), F32)
        s_ref[pl.ds(0, reach), :] = zeros
        s_ref[pl.ds(reach + l, reach), :] = zeros
        u, v, g = u_ref[...], v_ref[...], g_ref[...]
        s_ref[pl.ds(reach, l), :] = u * jax.nn.sigmoid(v)
        db_ref[...] = jnp.sum(g, axis=0, keepdims=True)
        for k in range(width):
            dw_ref[k:k + 1, :] = jnp.sum(g * s_ref[pl.ds(k * GRID_W, l), :], axis=0, keepdims=True)
        s_ref[pl.ds(reach, l), :] = g
        dt = sum(w_ref[k:k + 1, :] * s_ref[pl.ds((width - 1 - k) * GRID_W, l), :] for k in range(width))
        _, vjp = jax.vjp(lambda a, c: a * jax.nn.sigmoid(c), u, v)
        du, dv = vjp(dt)
        du_ref[pl.ds(0, l), :] = du.astype(du_ref.dtype)
        dv_ref[pl.ds(0, l), :] = dv.astype(dv_ref.dtype)
        du_ref[pl.ds(l, lc), :] = jnp.zeros((lc, cw), du_ref.dtype)
        dv_ref[pl.ds(l, lc), :] = jnp.zeros((lc, cw), dv_ref.dtype)

    pshape = jax.ShapeDtypeStruct((t_all, d), MXU_DTYPE)
    tall = pl.BlockSpec((t_all, cw), lambda j: (0, j))
    return pl.pallas_call(
        body, name="conv31_bwd", grid=(nb,),
        in_specs=[pl.BlockSpec((l, cw), lambda j: (0, u_blk * nb + j)), pl.BlockSpec((l, cw), lambda j: (0, (u_blk + 1) * nb + j)),
                  pl.BlockSpec((width, cw), lambda j: (0, j)), pl.BlockSpec((l, cw), lambda j: (0, j))],
        out_specs=[tall, tall, pl.BlockSpec((width, cw), lambda j: (0, j)), pl.BlockSpec((1, cw), lambda j: (0, j))],
        out_shape=[pshape, pshape, jax.ShapeDtypeStruct((width, d), F32), jax.ShapeDtypeStruct((1, d), F32)],
        scratch_shapes=[pltpu.VMEM((l + 2 * reach, cw), F32)], compiler_params=_cp(("parallel",)),
    )(proj, proj, w, dcv)


def _softplus(x):
    return jnp.maximum(x, 0.0) + jnp.log(1.0 + jnp.exp(-jnp.abs(x)))


def _dt_fwd(proj, bias, a, dt_blk):
    t_all = proj.shape[0]
    hh = bias.shape[1]
    q = CHUNK

    def body(r_ref, b_ref, a_ref, dt_ref, cs_ref, tc_ref, cst_ref):
        dt = _softplus(r_ref[...] + b_ref[...])
        dt_ref[...] = dt
        da = dt * a_ref[...]
        li, si = _iota((q, q), 0), _iota((q, q), 1)
        reverse_cols = _iota((q, hh), 1) >= hh // 2
        cs = jnp.where(reverse_cols, _dot((si >= li).astype(F32), da, exact=True), _dot((si <= li).astype(F32), da, exact=True))
        cs_ref[...] = cs
        cst_ref[...] = cs.T
        total = jnp.where(_iota((1, hh), 1) >= hh // 2, cs_ref[0:1, :], cs_ref[q - 1:q, :])
        tc_ref[...] = total - cs

    row = pl.BlockSpec((q, hh), lambda i: (i, 0))
    shape = jax.ShapeDtypeStruct((t_all, hh), F32)
    return pl.pallas_call(
        body, name="dt_fwd", grid=(t_all // q,),
        in_specs=[pl.BlockSpec((q, hh), lambda i: (i, dt_blk)), _bc_spec(hh), _bc_spec(hh)],
        out_specs=[row, row, row, pl.BlockSpec((hh, q), lambda i: (0, i))],
        out_shape=[shape, shape, shape, jax.ShapeDtypeStruct((hh, t_all), F32)],
        compiler_params=_cp(("parallel",)),
    )(proj, bias, a)


def _three_way(x):
    def top(v):
        word = lax.bitcast_convert_type(v, jnp.uint32) & jnp.uint32(0xFFFF0000)
        return lax.bitcast_convert_type(word, F32)

    hi = top(x)
    rest = x - hi
    mid = top(rest)
    return hi.astype(jnp.bfloat16), mid.astype(jnp.bfloat16), (rest - mid).astype(jnp.bfloat16)


def _scan_columns(dt, cs, tc, groups2, hpg):
    t_all = dt.shape[0]
    parts = [part.reshape(t_all, groups2, 1, hpg) for arr in (dt, cs, tc) for part in _three_way(arr)]
    cols = jnp.concatenate(parts, axis=2).transpose(1, 0, 2, 3).reshape(groups2, t_all, 9 * hpg)
    return jnp.pad(cols, ((0, 0), (0, 0), (0, LANES - 9 * hpg)))


def _dt_bwd(proj, bias, dt, ddt, dda, dt_blk):
    t_all = proj.shape[0]
    hh = bias.shape[1]
    q = _tile(t_all, 1024, LANES)

    def body(r_ref, b_ref, dt_ref, ddt_ref, dda_ref, dr_ref, db_ref, da_ref):
        dr = ddt_ref[...] * jax.nn.sigmoid(r_ref[...] + b_ref[...])
        dr_ref[...] = dr.astype(dr_ref.dtype)
        i = pl.program_id(0)
        _acc(db_ref, jnp.sum(dr, axis=0, keepdims=True), i)
        _acc(da_ref, jnp.sum(dda_ref[...] * dt_ref[...], axis=0, keepdims=True), i)

    row = pl.BlockSpec((q, hh), lambda i: (i, 0))
    vec = jax.ShapeDtypeStruct((1, hh), F32)
    return pl.pallas_call(
        body, name="dt_bwd", grid=(t_all // q,),
        in_specs=[pl.BlockSpec((q, hh), lambda i: (i, dt_blk)), _bc_spec(hh), row, row, row],
        out_specs=[row, _bc_spec(hh), _bc_spec(hh)],
        out_shape=[jax.ShapeDtypeStruct((t_all, hh), MXU_DTYPE), vec, vec], compiler_params=_cp(("arbitrary",)),
    )(proj, bias, dt, ddt, dda)


_NT = (((1,), (1,)), ((), ()))
_TN = (((0,), (0,)), ((), ()))


def _dot(a, b, dims=None, exact=False):
    kw = dict(preferred_element_type=F32)
    if exact:
        kw["precision"] = HI
    if dims is None:
        return jnp.dot(a, b, **kw)
    return lax.dot_general(a, b, dims, **kw)


def _iota(shape, dim):
    return lax.broadcasted_iota(jnp.int32, shape, dim)


class _Ssd:
    def __init__(self, l, lc, di, p, reverse):
        self.q, self.n, self.g = CHUNK, SSM_STATE, SSM_GROUPS
        self.nl, self.ncx = l // CHUNK, lc // CHUNK
        self.ns = self.nl + self.ncx
        self.t_all, self.di, self.p, self.reverse = l + lc, di, p, reverse
        self.hpg = di // p // SSM_GROUPS
        self.gw = self.hpg * p
        self.ntile = self.gw // LANES
        self.hpt = LANES // p
        self.log2p = p.bit_length() - 1
        assert 1 << self.log2p == p and self.gw % LANES == 0 and self.n == LANES and self.q == LANES
        assert 9 * self.hpg <= LANES
        self.d = 1 if reverse else 0

    def chunk_at(self, step):
        if self.reverse:
            return self.ns - 1 - step
        return jnp.where(step < self.ncx, self.nl + step, step - self.ncx)

    def selectors(self):
        hpg = self.hpg
        k = jnp.arange(LANES)
        quantity, head, used = k // (3 * hpg), k % hpg, k < 9 * hpg
        lane_head = jnp.arange(LANES) // self.p
        tiles = jnp.stack([
            jnp.concatenate([(used & (quantity == qo))[:, None] & (head[:, None] == tt * self.hpt + lane_head[None, :])
                             for qo in range(3)], axis=1) for tt in range(self.ntile)])
        heads = jnp.stack([jnp.broadcast_to((used & (quantity == 1) & (head == j))[:, None], (LANES, LANES))
                           for j in range(hpg)])
        return tiles.astype(jnp.bfloat16), heads.astype(jnp.bfloat16)

    def in_specs(self, chunk_of):
        g, n, hpg, q = self.g, self.n, self.hpg, self.q
        b_blk, c_blk = self.di // n, self.di // n + g
        d = self.d
        return [
            pl.BlockSpec((q, self.gw), lambda gi, i: (chunk_of(i), gi)),
            pl.BlockSpec((q, n), lambda gi, i: (chunk_of(i), b_blk + gi)),
            pl.BlockSpec((q, n), lambda gi, i: (chunk_of(i), c_blk + gi)),
            pl.BlockSpec((1, q, LANES), lambda gi, i: (d * g + gi, chunk_of(i), 0)),
            pl.BlockSpec((hpg, q), lambda gi, i: (d * g + gi, chunk_of(i))),
            pl.BlockSpec((self.ntile, LANES, 3 * LANES), lambda gi, i: (0, 0, 0)),
            pl.BlockSpec((hpg, LANES, LANES), lambda gi, i: (0, 0, 0)),
        ]

    def masks(self):
        li, si = _iota((self.q, self.q), 0), _iota((self.q, self.q), 1)
        if self.reverse:
            return si >= li, li >= si
        return si <= li, li <= si

    def spread(self, cols, et_ref, tt):
        ex = _dot(cols, et_ref[tt])
        return ex[:, :LANES], ex[:, LANES:2 * LANES], ex[:, 2 * LANES:]

    def head_lanes(self, qq):
        return lax.shift_right_logical(_iota((self.q, LANES), 1), self.log2p) == qq

    def head_sums(self, v, tt):
        sel = _iota((HEAD_COLS, LANES), 0) == tt * self.hpt + lax.shift_right_logical(_iota((HEAD_COLS, LANES), 1), self.log2p)
        sel = sel.astype(jnp.bfloat16)
        return sum(_dot(part, sel, _NT) for part in _three_way(v))

    def state_scale(self, csr_ref):
        last = 0 if self.reverse else self.q - 1
        total = jnp.sum(jnp.where(_iota((self.hpg, self.q), 1) == last, csr_ref[...], 0.0), axis=1, keepdims=True)
        decay = jnp.broadcast_to(jnp.exp(total), (self.hpg, self.n))
        decay = jnp.concatenate([decay, jnp.zeros((HEAD_COLS - self.hpg, self.n), F32)], axis=0)
        rows = lax.shift_right_logical(_iota((self.gw, HEAD_COLS), 0), self.log2p) == _iota((self.gw, HEAD_COLS), 1)
        return _dot_parts(rows.astype(jnp.bfloat16), decay)


def _dot_parts(sel, v, dims=None):
    return sum(_dot(sel, part, dims) for part in _three_way(v))


def _ssd_fwd(xbc, cols, cs_t, l, lc, di, p, reverse):
    s = _Ssd(l, lc, di, p, reverse)
    q, n, gw = s.q, s.n, s.gw
    neg_inf = float("-inf")

    def body(xs_ref, b_ref, c_ref, cols_ref, csr_ref, et_ref, eh_ref, y_ref, hp_ref, h_scr):
        i = pl.program_id(1)

        @pl.when(i == 0)
        def _():
            h_scr[...] = jnp.zeros_like(h_scr)

        h = h_scr[...]
        hp_ref[0, 0] = h
        mask, _ = s.masks()
        cols = cols_ref[0]
        bb, cb = b_ref[...].astype(MXU_DTYPE), c_ref[...].astype(MXU_DTYPE)
        cbt = _dot(cb, bb, _NT)
        y_off = _dot(cb, h.astype(MXU_DTYPE), _NT)
        w_tiles = []
        for tt in range(s.ntile):
            sl = slice(tt * LANES, (tt + 1) * LANES)
            dt_b, cs_b, tc_b = s.spread(cols, et_ref, tt)
            x = xs_ref[:, sl] * dt_b
            yd = jnp.zeros((q, LANES), F32)
            for qq in range(s.hpt):
                j = tt * s.hpt + qq
                seg = _dot(cols, eh_ref[j]) - csr_ref[j:j + 1, :]
                m = (cbt * jnp.exp(jnp.where(mask, seg, neg_inf))).astype(MXU_DTYPE)
                xh = jnp.where(s.head_lanes(qq), x, 0.0).astype(MXU_DTYPE)
                yd = yd + _dot(m, xh)
            y_ref[:, sl] = yd + y_off[:, sl] * jnp.exp(cs_b)
            w_tiles.append((x * jnp.exp(tc_b)).astype(MXU_DTYPE))
        wm = w_tiles[0] if s.ntile == 1 else jnp.concatenate(w_tiles, axis=1)
        h_scr[...] = h * s.state_scale(csr_ref) + _dot(wm, bb, _TN)

    d = "rev" if reverse else "fwd"
    e_tiles, e_heads = s.selectors()
    return pl.pallas_call(
        body, name=f"ssd_{d}", grid=(s.g, s.ns), in_specs=s.in_specs(s.chunk_at),
        out_specs=[pl.BlockSpec((q, gw), lambda gi, i: (s.chunk_at(i), gi)),
                   pl.BlockSpec((1, 1, gw, n), lambda gi, i: (i, gi, 0, 0))],
        out_shape=[jax.ShapeDtypeStruct((s.t_all, di), F32), jax.ShapeDtypeStruct((s.ns, s.g, gw, n), F32)],
        scratch_shapes=[pltpu.VMEM((gw, n), F32)],
        compiler_params=_cp(("parallel", "arbitrary")),
    )(xbc, xbc, xbc, cols, cs_t, e_tiles, e_heads)


def _ssd_bwd(xbc, cols, cs_t, a_cols, dy, hprev, l, lc, di, p, reverse, dsk=None, prev=None):
    s = _Ssd(l, lc, di, p, reverse)
    q, n, gw, hpg = s.q, s.n, s.gw, s.hpg
    neg_inf = float("-inf")
    n_extra = (dsk is not None) + (3 if prev is not None else 0)

    def chunk_of(i):
        return s.chunk_at(s.ns - 1 - i)

    def body(xs_ref, b_ref, c_ref, cols_ref, csr_ref, et_ref, eh_ref, ac_ref, dy_ref, hp_ref, *rest):
        extra, (dxs_ref, db_ref, dc_ref, ddt_ref, dda_ref, dh_scr) = rest[:n_extra], rest[n_extra:]
        dsk_ref = extra[0] if dsk is not None else None
        prev_refs = extra[-3:] if prev is not None else None
        i = pl.program_id(1)

        @pl.when(i == 0)
        def _():
            dh_scr[...] = jnp.zeros_like(dh_scr)

        latent = (chunk_of(i) < s.nl).astype(F32)
        h, dh = hp_ref[0, 0], dh_scr[...]
        hb, dhb = h.astype(MXU_DTYPE), dh.astype(MXU_DTYPE)
        mask, mask_t = s.masks()
        cols = cols_ref[0]
        bb, cb = b_ref[...].astype(MXU_DTYPE), c_ref[...].astype(MXU_DTYPE)
        cbt, bct = _dot(cb, bb, _NT), _dot(bb, cb, _NT)
        b_dh = _dot(bb, dhb, _NT)
        y_off0 = _dot(cb, hb, _NT)
        d_g, d_gt = jnp.zeros((q, q), F32), jnp.zeros((q, q), F32)
        dcs = jnp.zeros((q, HEAD_COLS), F32)
        ddt_x = jnp.zeros((q, HEAD_COLS), F32)
        r_state = jnp.zeros((8, HEAD_COLS), F32)
        dye_tiles, xte_tiles = [], []
        for tt in range(s.ntile):
            sl = slice(tt * LANES, (tt + 1) * LANES)
            dt_b, cs_b, tc_b = s.spread(cols, et_ref, tt)
            ecs_b, te_b = jnp.exp(cs_b), jnp.exp(tc_b)
            xs_t = xs_ref[:, sl]
            x = xs_t * dt_b
            xb = x.astype(MXU_DTYPE)
            d_y = dy_ref[:, sl] * latent
            d_yb = d_y.astype(MXU_DTYPE)
            dx_state = b_dh[:, sl] * te_b
            dx_diag = jnp.zeros((q, LANES), F32)
            for qq in range(s.hpt):
                j = tt * s.hpt + qq
                csc_b = _dot(cols, eh_ref[j])
                csr = csr_ref[j:j + 1, :]
                lm = jnp.exp(jnp.where(mask, csc_b - csr, neg_inf))
                lm_t = jnp.exp(jnp.where(mask_t, csr - csc_b, neg_inf))
                m_t = bct * lm_t
                lanes = s.head_lanes(qq)
                d_yh = jnp.where(lanes, d_y, 0.0).astype(MXU_DTYPE)
                xh = jnp.where(lanes, x, 0.0).astype(MXU_DTYPE)
                d_m = _dot(d_yh, xb, _NT)
                d_mt = _dot(xh, d_yb, _NT)
                r1 = jnp.sum(d_m * (cbt * lm), axis=1, keepdims=True)
                r2 = jnp.sum(d_mt * m_t, axis=1, keepdims=True)
                dcs = dcs + (r1 - r2) * (_iota((1, HEAD_COLS), 1) == j).astype(F32)
                dx_diag = dx_diag + _dot(m_t.astype(MXU_DTYPE), d_yh)
                d_g = d_g + d_m * lm
                d_gt = d_gt + d_mt * lm_t
            d_x = dx_diag + dx_state
            d_xs = d_x * dt_b
            if dsk_ref is not None:
                d_xs = d_xs + d_y * dsk_ref[:, sl]
            if prev_refs is not None:
                d_xs = d_xs + prev_refs[0][:, sl]
            dxs_ref[:, sl] = d_xs
            ddt_x = ddt_x + s.head_sums(d_x * xs_t, tt)
            fed = x * dx_state
            dcs = dcs + s.head_sums(d_y * y_off0[:, sl] * ecs_b - fed, tt)
            fed_rows = jnp.broadcast_to(jnp.sum(fed, axis=0, keepdims=True), (8, LANES))
            r_state = r_state + s.head_sums(fed_rows, tt)
            dye_tiles.append((d_y * ecs_b).astype(MXU_DTYPE))
            xte_tiles.append((x * te_b).astype(MXU_DTYPE))
        dye = dye_tiles[0] if s.ntile == 1 else jnp.concatenate(dye_tiles, axis=1)
        xte = xte_tiles[0] if s.ntile == 1 else jnp.concatenate(xte_tiles, axis=1)
        d_c = _dot(d_g.astype(MXU_DTYPE), bb) + _dot(dye, hb)
        d_b = _dot(d_gt.astype(MXU_DTYPE), cb) + _dot(xte, dhb)
        if prev_refs is not None:
            d_b, d_c = d_b + prev_refs[1][...], d_c + prev_refs[2][...]
        dc_ref[...] = d_c
        db_ref[...] = d_b
        scale = s.state_scale(csr_ref)
        carried = dh * h * scale
        d_tot = jnp.sum(r_state, axis=0, keepdims=True) * 0.125
        for j in range(hpg):
            part = jnp.sum(carried[j * p:(j + 1) * p, :], axis=0, keepdims=True)
            d_tot = d_tot + jnp.sum(part, axis=1, keepdims=True) * (_iota((1, HEAD_COLS), 1) == j).astype(F32)
        dda = _dot_parts(mask_t.astype(jnp.bfloat16), dcs) + d_tot
        ddt_ref[0] = ddt_x + dda * ac_ref[0]
        dda_ref[0] = dda
        dh_scr[...] = dh * scale + _dot(dye, cb, _TN)

    d = "rev" if reverse else "fwd"
    e_tiles, e_heads = s.selectors()
    col = pl.BlockSpec((1, q, HEAD_COLS), lambda gi, i: (gi, chunk_of(i), 0))
    gn = pl.BlockSpec((q, n), lambda gi, i: (chunk_of(i), gi))
    wide = pl.BlockSpec((q, gw), lambda gi, i: (chunk_of(i), gi))
    extra_specs, extra_args, aliases = [], [], {}
    if dsk is not None:
        extra_specs.append(pl.BlockSpec((1, gw), lambda gi, i: (0, gi)))
        extra_args.append(dsk)
    if prev is not None:
        first = 10 + len(extra_args)
        extra_specs += [wide, gn, gn]
        extra_args += list(prev)
        aliases = {first: 0, first + 1: 1, first + 2: 2}
    return pl.pallas_call(
        body, name=f"ssd_bwd_{d}", grid=(s.g, s.ns),
        in_specs=s.in_specs(chunk_of) + [
            pl.BlockSpec((1, 1, HEAD_COLS), lambda gi, i: (s.d * s.g + gi, 0, 0)),
            pl.BlockSpec((q, gw), lambda gi, i: (jnp.minimum(chunk_of(i), s.nl - 1), gi)),
            pl.BlockSpec((1, 1, gw, n), lambda gi, i: (s.ns - 1 - i, gi, 0, 0))] + extra_specs,
        out_specs=[wide, gn, gn, col, col],
        out_shape=[jax.ShapeDtypeStruct((s.t_all, di), F32), jax.ShapeDtypeStruct((s.t_all, s.g * n), F32),
                   jax.ShapeDtypeStruct((s.t_all, s.g * n), F32), jax.ShapeDtypeStruct((s.g, s.t_all, HEAD_COLS), F32),
                   jax.ShapeDtypeStruct((s.g, s.t_all, HEAD_COLS), F32)],
        scratch_shapes=[pltpu.VMEM((gw, n), F32)],
        input_output_aliases=aliases, compiler_params=_cp(("parallel", "arbitrary")),
    )(xbc, xbc, xbc, cols, cs_t, e_tiles, e_heads, a_cols, dy, hprev, *extra_args)


def _ada_fwd(crows, w, b):
    r, d = crows.shape
    ws = w.shape[1]
    tn = _tile(ws, 512, LANES)

    def body(c_ref, w_ref, b_ref, m_ref, s_ref):
        s = _silu(c_ref[...])
        s_ref[...] = s
        m_ref[...] = _dot(s.astype(MXU_DTYPE), w_ref[...].astype(MXU_DTYPE)) + b_ref[...]

    full = pl.BlockSpec((r, d), lambda j: (0, 0))
    return pl.pallas_call(
        body, name="ada_fwd", grid=(ws // tn,),
        in_specs=[full, pl.BlockSpec((d, tn), lambda j: (0, j)), pl.BlockSpec((1, tn), lambda j: (0, j))],
        out_specs=[pl.BlockSpec((r, tn), lambda j: (0, j)), full],
        out_shape=[jax.ShapeDtypeStruct((r, ws), F32), jax.ShapeDtypeStruct((r, d), F32)],
        compiler_params=_cp(("arbitrary",)),
    )(crows, w, b)


def _ada_bwd(s_t, w, dm):
    d, r = s_t.shape
    ws = w.shape[1]
    tn = _tile(ws, 512, LANES)

    def body(st_ref, w_ref, dm_ref, dw_ref, ds_ref):
        dmb = dm_ref[...].astype(MXU_DTYPE)
        dw_ref[...] = _dot(st_ref[...].astype(MXU_DTYPE), dmb)
        _acc(ds_ref, _dot(dmb, w_ref[...].astype(MXU_DTYPE), _NT), pl.program_id(0))

    return pl.pallas_call(
        body, name="ada_bwd", grid=(ws // tn,),
        in_specs=[pl.BlockSpec((d, r), lambda j: (0, 0)), pl.BlockSpec((d, tn), lambda j: (0, j)),
                  pl.BlockSpec((r, tn), lambda j: (0, j))],
        out_specs=[pl.BlockSpec((d, tn), lambda j: (0, j)), pl.BlockSpec((r, d), lambda j: (0, 0))],
        out_shape=[jax.ShapeDtypeStruct((d, ws), F32), jax.ShapeDtypeStruct((r, d), F32)],
        compiler_params=_cp(("arbitrary",)),
    )(s_t, w, dm)


def _adamw(w, g, m, v, name):
    r, c = w.shape
    t = _tile(r, max(8, 300_000 // c), 8)

    def body(w_ref, g_ref, m_ref, v_ref, d_ref, m2_ref, v2_ref):
        g = g_ref[...]
        m2 = ADAM_B1 * m_ref[...] + (1.0 - ADAM_B1) * g
        v2 = ADAM_B2 * v_ref[...] + (1.0 - ADAM_B2) * (g * g)
        m_hat = m2 / (1.0 - ADAM_B1 ** ADAM_STEP)
        v_hat = v2 / (1.0 - ADAM_B2 ** ADAM_STEP)
        d_ref[...] = -ADAM_LR * (m_hat / (jnp.sqrt(v_hat) + ADAM_EPS) + ADAM_WD * w_ref[...])
        m2_ref[...] = m2
        v2_ref[...] = v2

    blk = pl.BlockSpec((t, c), lambda i: (i, 0))
    shape = jax.ShapeDtypeStruct((r, c), F32)
    return pl.pallas_call(
        body, name=name, grid=(r // t,), in_specs=[blk] * 4, out_specs=[blk] * 3, out_shape=[shape] * 3,
        compiler_params=_cp(("parallel",)),
    )(w, g, m, v)


def _sum_devices(gathered):
    rows, w = gathered.shape
    per = rows // N_DEV

    def body(g_ref, o_ref):
        total = g_ref[pl.ds(0, per), :]
        for dev in range(1, N_DEV):
            total = total + g_ref[pl.ds(dev * per, per), :]
        o_ref[...] = total

    return pl.pallas_call(
        body, name="sum_devices", out_shape=jax.ShapeDtypeStruct((per, w), F32),
        in_specs=[pl.BlockSpec(memory_space=pltpu.VMEM)], out_specs=pl.BlockSpec(memory_space=pltpu.VMEM),
        compiler_params=_cp(),
    )(gathered)


def _c_ctx_grad(parts, c_ctx):
    rows, d = parts.shape
    per = rows // N_DEV

    def body(p_ref, c_ref, o_ref):
        total = p_ref[pl.ds(0, 1), :]
        for chip in range(1, N_SHARD):
            total = total + p_ref[pl.ds(2 * chip * per, 1), :]
        _, vjp = jax.vjp(_silu, c_ref[...])
        o_ref[...] = vjp(total)[0]

    return pl.pallas_call(
        body, name="c_ctx_grad", out_shape=jax.ShapeDtypeStruct((1, d), F32),
        in_specs=[pl.BlockSpec(memory_space=pltpu.VMEM)] * 2, out_specs=pl.BlockSpec(memory_space=pltpu.VMEM),
        compiler_params=_cp(),
    )(parts, c_ctx)


def _pad_rows(a, rows, width):
    return jnp.pad(a, ((0, rows - a.shape[0]), (0, width - a.shape[1])))


def _pack(vectors, quantum):
    flat = jnp.concatenate([v.reshape(-1) for v in vectors])
    return jnp.pad(flat, (0, -flat.shape[0] % quantum))


def kernel(x, c, ctx, c_ctx, w_mod, b_mod, norm_mix, w_in, ssm_conv_w, ssm_conv_b, dt_bias, a_log, d_skip, ssm_norm, cf_conv_w, cf_conv_b, cf_ln_g, cf_ln_b, w_proj_a, w_proj_b, w_out, norm_ffn, w_ffn_gate, w_ffn_up, w_ffn_down, norm_final, loss_target, m_c_ctx, m_w_mod, m_b_mod, m_norm_mix, m_w_in, m_ssm_conv_w, m_ssm_conv_b, m_dt_bias, m_a_log, m_d_skip, m_ssm_norm, m_cf_conv_w, m_cf_conv_b, m_cf_ln_g, m_cf_ln_b, m_w_proj_a, m_w_proj_b, m_w_out, m_norm_ffn, m_w_ffn_gate, m_w_ffn_up, m_w_ffn_down, m_norm_final, v_c_ctx, v_w_mod, v_b_mod, v_norm_mix, v_w_in, v_ssm_conv_w, v_ssm_conv_b, v_dt_bias, v_a_log, v_d_skip, v_ssm_norm, v_cf_conv_w, v_cf_conv_b, v_cf_ln_g, v_cf_ln_b, v_w_proj_a, v_w_proj_b, v_w_out, v_norm_ffn, v_w_ffn_gate, v_w_ffn_up, v_w_ffn_down, v_norm_final):
    l, d = x.shape[1], x.shape[2]
    lc = ctx.shape[1]
    t_all = l + lc
    di = ssm_norm.shape[-1]
    h = d_skip.shape[-1]
    p = di // h
    g, n = SSM_GROUPS, SSM_STATE
    hpg = h // g
    conv_dim = di + 2 * g * n
    df = w_ffn_down.shape[1] * N_SHARD
    assert 2 * h == LANES and d % (2 * LANES) == 0

    my_x, my_y, my_c = _mesh_pos()
    chip = 2 * my_x + my_y
    dev = 2 * chip + my_c

    x2, ctx2, tgt = x[0], ctx[0], loss_target[0]
    row = lambda a: a.reshape(1, -1)

    cw_shard, cfw_shard = ssm_conv_w[0], cf_conv_w[0]
    k5, k31 = cw_shard.shape[0], cfw_shard.shape[0]
    r5, r31 = -(-k5 // 8) * 8, -(-k31 // 8) * 8
    wp = max(d, cw_shard.shape[1], cfw_shard.shape[1])
    packed = jnp.concatenate([_pad_rows(c, 8, wp), _pad_rows(cw_shard, r5, wp), _pad_rows(cfw_shard, r31, wp)], axis=0)
    got = _allgather_small(packed, "ag_params").reshape(N_DEV, 8 + r5 + r31, wp)
    c_all = got[:, 0, :d]
    conv_w = got[0::2, 8:8 + k5, :cw_shard.shape[1]].transpose(1, 0, 2).reshape(k5, conv_dim)
    cf_w = got[0::2, 8 + r5:8 + r5 + k31, :cfw_shard.shape[1]].transpose(1, 0, 2).reshape(k31, d)

    ws = w_mod.shape[2]
    crows = jnp.concatenate([c_all, row(c_ctx), jnp.zeros((7, d), F32)], axis=0)
    b_mod_mine = lax.dynamic_slice(b_mod, (0, chip * ws), (1, ws))
    m_part, s_rows = _ada_fwd(crows, w_mod[0], b_mod_mine)
    m_full = _allgather_small(m_part, "ag_mod").reshape(N_DEV, 16, ws)[0::2].transpose(1, 0, 2).reshape(16, N_SHARD * ws)
    m_lat = lax.dynamic_slice(m_full, (dev, 0), (1, 6 * d))
    sh1, sc1, g1, sh2, sc2, g2 = [m_lat[:, i * d:(i + 1) * d] for i in range(6)]
    csh1, csc1 = m_full[8:9, 0:d], m_full[8:9, d:2 * d]

    shards = [w_in[0].T, w_ffn_gate[0].T, w_ffn_up[0].T, w_proj_a[0], w_proj_b[0], w_out[0], w_ffn_down[0]]
    win_t, wg_t, wu_t, wpa, wpb, wout, wdn = _allgather_rows([s.astype(WIRE_DTYPE) for s in shards], "ag_weights")
    o_xbc, o_dt, o_glu, o_gates = di, di + conv_dim, di + conv_dim + 2 * h, di + conv_dim + 2 * h + 2 * d
    win_work = jnp.concatenate([win_t[:o_xbc], win_t[o_glu:], win_t[o_xbc:o_dt], win_t[o_dt:o_glu]], axis=0)
    c_u, c_ga, c_xbc, c_dt = di, di + 2 * d, di + 4 * d, di + 4 * d + conv_dim
    wgu = jnp.concatenate([wg_t, wu_t], axis=0)

    nm = norm_mix
    hx = _mod_fwd(x2, ctx2, nm, sc1, sh1, csc1, csh1)
    proj = _matmul(hx, win_work, tb=True, tm=768, tn=640, name="mm_proj")
    xbc = _conv5_fwd(proj, conv_w, ssm_conv_b, l, lc, c_xbc, conv_dim)
    a = -jnp.exp(a_log.reshape(1, 2 * h))
    dt, cs, tc, cs_t = _dt_fwd(proj, dt_bias.reshape(1, 2 * h), a, c_dt // LANES)
    cols = _scan_columns(dt, cs, tc, 2 * g, hpg)
    a_cols = jnp.pad(a.reshape(2 * g, 1, hpg), ((0, 0), (0, 0), (0, HEAD_COLS - hpg)))
    y_f, hp_f = _ssd_fwd(xbc, cols, cs_t, l, lc, di, p, False)
    y_r, hp_r = _ssd_fwd(xbc, cols, cs_t, l, lc, di, p, True)
    dsk = jnp.repeat(d_skip.reshape(h), p).reshape(1, di)
    ya_in = _gate_fwd(y_f, y_r, xbc, proj, dsk, ssm_norm, l, di)
    y_a = _matmul(ya_in, wpa, tk=di, name="mm_ya")
    u_blk = c_u // d
    cv = _conv31_fwd(proj, cf_w, cf_conv_b, l, d, u_blk)
    cf = _ln_fwd(cv, cf_ln_g, cf_ln_b)
    y_b = _matmul(cf, wpb, name="mm_yb")
    ga_blk = c_ga // d
    merged = _merge_fwd(y_a, y_b, proj, ga_blk)
    mix = _matmul(merged, wout, name="mm_mix")
    x1, hx2 = _res_fwd(x2, mix, g1, norm_ffn, sc2, sh2)
    gu = _matmul(hx2, wgu, tb=True, tn=_tile(df, 1024, LANES), name="mm_gu")
    act = _swiglu_fwd(gu, df)
    dn = _matmul(act, wdn, tk=df, name="mm_dn")
    loss, dx1, ddn, dg2, d_norm_final = _loss_and_grads(x1, dn, g2, row(norm_final), tgt)

    dact = _matmul(ddn, wdn, tb=True, tn=_tile(df, 1024, LANES), name="mm_dact")
    dw_dn = _matmul(act, ddn, ta=True, tn=d, tk=1024, name="mm_dw_dn")
    dgu = _swiglu_bwd(gu, dact, df)
    dhx2 = _matmul(dgu, wgu, tk=df, name="mm_dhx2")
    dw_gu = _matmul(dgu, hx2, ta=True, tn=d, tk=1024, name="mm_dw_gu")
    dx_res, dmix, dg1, d_norm_ffn, dsc2, dsh2 = _res_bwd(x2, mix, g1, norm_ffn, sc2, sh2, dx1, dhx2)
    dmerged = _matmul(dmix, wout, tb=True, name="mm_dmerged")
    dw_out = _matmul(merged, dmix, ta=True, tn=d, tk=1024, name="mm_dw_out")
    dya, dyb, dga, dgb = _merge_bwd(y_a, y_b, proj, ga_blk, dmerged, lc)
    dcf = _matmul(dyb, wpb, tb=True, name="mm_dcf")
    dw_pb = _matmul(cf, dyb, ta=True, tn=d, tk=1024, name="mm_dw_pb")
    dcv, d_ln_g, d_ln_b = _ln_bwd(cv, cf_ln_g, cf_ln_b, dcf)
    du, dv, d_cf_w, d_cf_b = _conv31_bwd(proj, cf_w, dcv, l, lc, d, u_blk)
    dya_in = _matmul(dya, wpa, tb=True, tn=_tile(di, 1024, LANES), name="mm_dya_in")
    dw_pa = _matmul(ya_in, dya, ta=True, tn=d, tk=1024, name="mm_dw_pa")
    dy, dz, ddsk, d_ssm_norm = _gate_bwd(y_f, y_r, xbc, proj, dsk, ssm_norm, dya_in, l, lc, di)
    dxs_f, db_f, dc_f, ddt_f, dda_f = _ssd_bwd(xbc, cols, cs_t, a_cols, dy, hp_f, l, lc, di, p, False, dsk=dsk)
    dxs, db, dc, ddt_r, dda_r = _ssd_bwd(xbc, cols, cs_t, a_cols, dy, hp_r, l, lc, di, p, True, prev=(dxs_f, db_f, dc_f))
    dxs_raw, dcw_x, dcb_x = _conv5_bwd(proj, conv_w, ssm_conv_b, [dxs], l, lc, c_xbc, 0, di)
    db_raw, dcw_b, dcb_b = _conv5_bwd(proj, conv_w, ssm_conv_b, [db], l, lc, c_xbc, di, g * n)
    dc_raw, dcw_c, dcb_c = _conv5_bwd(proj, conv_w, ssm_conv_b, [dc], l, lc, c_xbc, di + g * n, g * n)
    d_conv_w = jnp.concatenate([dcw_x, dcw_b, dcw_c], axis=1)
    d_conv_b = jnp.concatenate([dcb_x, dcb_b, dcb_c], axis=1)
    heads = lambda f, r: jnp.concatenate([t[:, :, :hpg].transpose(1, 0, 2).reshape(t_all, h) for t in (f, r)], axis=1)
    ddt_raw, d_dt_bias, dda_dt = _dt_bwd(proj, dt_bias.reshape(1, 2 * h), dt, heads(ddt_f, ddt_r), heads(dda_f, dda_r), c_dt // LANES)
    d_a_log = dda_dt * a
    dproj = jnp.concatenate([dz, du, dv, dga, dgb, dxs_raw, db_raw, dc_raw, ddt_raw], axis=1)
    dhx = _matmul(dproj, win_work, tm=768, tn=1024, tk=_tile(win_work.shape[0], 4096, LANES), name="mm_dhx")
    dw_in_work = _matmul(dproj, hx, ta=True, tm=640, tn=d, tk=768, name="mm_dw_in")
    grad_x, d_norm_mix, dsc1, dsh1, dcsc1, dcsh1 = _mod_bwd(x2, ctx2, nm, sc1, sh1, csc1, csh1, dhx, dx_res)

    dw_in_t = jnp.concatenate([dw_in_work[:c_u], dw_in_work[c_xbc:], dw_in_work[c_u:c_xbc]], axis=0)
    reduced = _reduce_scatter([dw_in_t, dw_gu[:df], dw_gu[df:], dw_pa, dw_pb, dw_out, dw_dn])
    g_in, g_gate, g_up = reduced[0].T, reduced[1].T, reduced[2].T
    g_pa, g_pb, g_out, g_dn = reduced[3:]

    zeros_d = jnp.zeros((1, d), F32)
    dm_lat = jnp.concatenate([dsh1, dsc1, dg1, dsh2, dsc2, dg2], axis=1)
    dm_ctx = jnp.concatenate([dcsh1, dcsc1] + [zeros_d] * 4, axis=1)
    d_d_skip = ddsk.reshape(h, p).sum(axis=1)
    replicated = [dm_lat + dm_ctx, d_norm_mix, d_conv_b, d_dt_bias, d_a_log, d_d_skip, d_ssm_norm, d_cf_b, d_ln_g, d_ln_b,
                  d_norm_ffn, d_norm_final]
    rep_w = [b_mod, norm_mix, ssm_conv_b, dt_bias, a_log, d_skip, ssm_norm, cf_conv_b, cf_ln_g, cf_ln_b, norm_ffn, norm_final]
    rep_m = [m_b_mod, m_norm_mix, m_ssm_conv_b, m_dt_bias, m_a_log, m_d_skip, m_ssm_norm, m_cf_conv_b, m_cf_ln_g, m_cf_ln_b,
             m_norm_ffn, m_norm_final]
    rep_v = [v_b_mod, v_norm_mix, v_ssm_conv_b, v_dt_bias, v_a_log, v_d_skip, v_ssm_norm, v_cf_conv_b, v_cf_ln_g, v_cf_ln_b,
             v_norm_ffn, v_norm_final]
    quantum = 8 * LANES
    rep_flat = _pack(replicated, quantum)
    n_rep = rep_flat.shape[0]
    summed_parts = [rep_flat, _pack([d_conv_w, d_cf_w, dm_ctx], quantum)]
    n_sum = n_rep + summed_parts[1].shape[0]
    everything = jnp.concatenate(summed_parts + [_pack([dm_lat], quantum)])
    gathered = _allgather_small(everything.reshape(8, -1), "ag_small_grads")
    w8 = gathered.shape[1]
    summed = _sum_devices(gathered).reshape(-1)
    dm_lat_all = gathered.reshape(N_DEV, 8 * w8)[:, n_sum:n_sum + 6 * d]
    off = n_rep
    g_conv_w_full = summed[off:off + k5 * conv_dim].reshape(k5, conv_dim)
    off += k5 * conv_dim
    g_cf_w_full = summed[off:off + k31 * d].reshape(k31, d)
    off += k31 * d
    dm_ctx_all = summed[off:off + 6 * d].reshape(1, 6 * d)
    g_conv_w = lax.dynamic_slice(g_conv_w_full, (0, chip * cw_shard.shape[1]), cw_shard.shape)
    g_cf_w = lax.dynamic_slice(g_cf_w_full, (0, chip * cfw_shard.shape[1]), cfw_shard.shape)

    dm_rows = jnp.concatenate([dm_lat_all, dm_ctx_all, jnp.zeros((7, 6 * d), F32)], axis=0)
    dm_mine = lax.dynamic_slice(dm_rows, (0, chip * ws), (16, ws))
    g_w_mod, ds_part = _ada_bwd(s_rows.T, w_mod[0], dm_mine)
    ds_all = _allgather_small(ds_part[8:16], "ag_c_ctx")
    g_c_ctx = _c_ctx_grad(ds_all, row(c_ctx))

    grads, deltas, new_ms, new_vs = {}, {}, {}, {}

    def update(name, w2, g2, m2, v2, shape):
        dl, mm, vv = _adamw(w2, g2, m2, v2, f"adamw_{name}")
        grads[name], deltas[name], new_ms[name], new_vs[name] = (t.reshape(shape) for t in (g2, dl, mm, vv))

    for name, w_, g_, m_, v_ in [
            ("w_mod", w_mod, g_w_mod, m_w_mod, v_w_mod), ("w_in", w_in, g_in, m_w_in, v_w_in),
            ("ssm_conv_w", ssm_conv_w, g_conv_w, m_ssm_conv_w, v_ssm_conv_w),
            ("cf_conv_w", cf_conv_w, g_cf_w, m_cf_conv_w, v_cf_conv_w),
            ("w_proj_a", w_proj_a, g_pa, m_w_proj_a, v_w_proj_a), ("w_proj_b", w_proj_b, g_pb, m_w_proj_b, v_w_proj_b),
            ("w_out", w_out, g_out, m_w_out, v_w_out), ("w_ffn_gate", w_ffn_gate, g_gate, m_w_ffn_gate, v_w_ffn_gate),
            ("w_ffn_up", w_ffn_up, g_up, m_w_ffn_up, v_w_ffn_up), ("w_ffn_down", w_ffn_down, g_dn, m_w_ffn_down, v_w_ffn_down)]:
        update(name, w_[0], g_, m_[0], v_[0], w_.shape)
    update("c_ctx", row(c_ctx), g_c_ctx, row(m_c_ctx), row(v_c_ctx), c_ctx.shape)

    rep_names = ["b_mod", "norm_mix", "ssm_conv_b", "dt_bias", "a_log", "d_skip", "ssm_norm", "cf_conv_b", "cf_ln_g", "cf_ln_b",
                 "norm_ffn", "norm_final"]
    as8 = lambda vs: _pack(vs, quantum).reshape(8, -1)
    g8 = summed[:n_rep].reshape(8, -1)
    d8, m8, v8 = _adamw(as8(rep_w), g8, as8(rep_m), as8(rep_v), "adamw_replicated")
    off = 0
    for name, w_ in zip(rep_names, rep_w):
        size = w_.size
        for store, packed8 in ((grads, g8), (deltas, d8), (new_ms, m8), (new_vs, v8)):
            store[name] = packed8.reshape(-1)[off:off + size].reshape(w_.shape)
        off += size

    order = ["c_ctx", "w_mod", "b_mod", "norm_mix", "w_in", "ssm_conv_w", "ssm_conv_b", "dt_bias", "a_log", "d_skip", "ssm_norm",
             "cf_conv_w", "cf_conv_b", "cf_ln_g", "cf_ln_b", "w_proj_a", "w_proj_b", "w_out", "norm_ffn", "w_ffn_gate", "w_ffn_up",
             "w_ffn_down", "norm_final"]
    total_loss = lax.psum(loss[0, 0], ("x", "y", "c"))
    return (total_loss, grad_x.reshape(x.shape), *[grads[k] for k in order], *[deltas[k] for k in order],
            *[new_ms[k] for k in order], *[new_vs[k] for k in order])
```

```python
import functools

import jax
import jax.numpy as jnp
from jax import lax
from jax.experimental import pallas as pl
from jax.experimental.pallas import tpu as pltpu

F32 = jnp.float32
MXU_DTYPE = jnp.bfloat16
WIRE_DTYPE = jnp.bfloat16
HI = lax.Precision.HIGHEST
EPS = 1e-6
SSM_GROUPS = 8
SSM_STATE = 128
CHUNK = 128
GRID_W = 64
LANES = 128
HEAD_COLS = 16
VMEM_LIMIT = 52 * 1024 * 1024
ADAM_LR, ADAM_B1, ADAM_B2, ADAM_EPS, ADAM_WD, ADAM_STEP = 0.001, 0.9, 0.999, 1e-08, 0.01, 10
MESH = pl.DeviceIdType.MESH
N_SHARD = 4
N_DEV = 8


def _cp(sem=None):
    kw = dict(vmem_limit_bytes=VMEM_LIMIT)
    if sem is not None:
        kw["dimension_semantics"] = sem
    return pltpu.CompilerParams(**kw)


def _tile(n, target, q):
    best = None
    for t in range(q, min(n, target) + 1, q):
        if n % t == 0:
            best = t
    return best if best is not None else n


def _acc(ref, val, i):
    @pl.when(i == 0)
    def _():
        ref[...] = val

    @pl.when(i > 0)
    def _():
        ref[...] += val


def _bc_spec(w):
    return pl.BlockSpec((1, w), lambda *_: (0, 0))


def _rms(x, w):
    return x * lax.rsqrt(jnp.mean(x * x, axis=-1, keepdims=True) + EPS) * w


def _silu(x):
    return x * jax.nn.sigmoid(x)


def _f_mod(x, w, sc, sh):
    return _rms(x, w) * (1.0 + sc) + sh


def _f_gate(yf, yr, xs, z, dsk, wn):
    return _rms((yf + yr + dsk * xs) * _silu(z), wn)


def _f_ln(cv, g, b):
    mu = jnp.mean(cv, axis=-1, keepdims=True)
    xc = cv - mu
    var = jnp.mean(xc * xc, axis=-1, keepdims=True)
    return _silu(xc * lax.rsqrt(var + EPS) * g + b)


def _f_merge(ya, yb, ga, gb):
    return jax.nn.sigmoid(ga) * ya + jax.nn.sigmoid(gb) * yb


def _f_res(x, mix, g1, wn, sc2, sh2):
    x1 = x + g1 * mix
    return x1, _rms(x1, wn) * (1.0 + sc2) + sh2


def _f_swiglu(gt, up):
    return _silu(gt) * up


def _f_loss(x1, dn, g2, wn, tgt):
    out = _rms(x1 + g2 * dn, wn)
    err = out - tgt
    per_tok = jnp.mean(err * err, axis=-1, keepdims=True)
    return 0.5 * jnp.sum(per_tok, axis=0, keepdims=True)


def _matmul(a, b, *, ta=False, tb=False, out_dtype=F32, tm=512, tn=512, tk=2048, name):
    m, k = (a.shape[1], a.shape[0]) if ta else a.shape
    n = b.shape[0] if tb else b.shape[1]
    assert (b.shape[1] if tb else b.shape[0]) == k, (a.shape, b.shape, ta, tb)
    tm, tn, tk = _tile(m, tm, LANES if ta else 16), _tile(n, tn, LANES), _tile(k, tk, LANES)
    nk = k // tk
    dims = (((0 if ta else 1,), (1 if tb else 0,)), ((), ()))

    def body(a_ref, b_ref, o_ref, *scratch):
        prod = lax.dot_general(a_ref[...].astype(MXU_DTYPE), b_ref[...].astype(MXU_DTYPE), dims,
                               preferred_element_type=F32)
        if nk == 1:
            o_ref[...] = prod.astype(o_ref.dtype)
        else:
            acc = scratch[0]
            kk = pl.program_id(2)
            _acc(acc, prod, kk)

            @pl.when(kk == nk - 1)
            def _():
                o_ref[...] = acc[...].astype(o_ref.dtype)

    a_spec = pl.BlockSpec((tk, tm), lambda i, j, kk: (kk, i)) if ta else pl.BlockSpec((tm, tk), lambda i, j, kk: (i, kk))
    b_spec = pl.BlockSpec((tn, tk), lambda i, j, kk: (j, kk)) if tb else pl.BlockSpec((tk, tn), lambda i, j, kk: (kk, j))
    return pl.pallas_call(
        body, name=name, grid=(m // tm, n // tn, nk), in_specs=[a_spec, b_spec],
        out_specs=pl.BlockSpec((tm, tn), lambda i, j, kk: (i, j)),
        out_shape=jax.ShapeDtypeStruct((m, n), out_dtype),
        scratch_shapes=[] if nk == 1 else [pltpu.VMEM((tm, tn), F32)],
        compiler_params=_cp(("parallel", "parallel", "arbitrary")),
    )(a, b)


def _mesh_pos():
    return lax.axis_index("x"), lax.axis_index("y"), lax.axis_index("c")


def _other_chips(x, y):
    return [(1 - x, y), (x, 1 - y), (1 - x, 1 - y)]


def _allgather_small(v, name):
    m_per, n = v.shape

    def body(x_ref, out_ref, send_sems, recv_sems, local_sem):
        x, y, c = _mesh_pos()
        me, sibling = (x, y, c), (x, y, 1 - c)
        chips = _other_chips(x, y)

        def rows(px, py, pc):
            return out_ref.at[pl.ds((4 * px + 2 * py + pc) * m_per, m_per), :]

        def copy(k, block, to, src=None):
            return pltpu.make_async_remote_copy(
                src_ref=rows(*block) if src is None else src, dst_ref=rows(*block),
                send_sem=send_sems.at[k], recv_sem=recv_sems.at[k], device_id=to, device_id_type=MESH)

        mine = pltpu.make_async_copy(x_ref, rows(*me), local_sem)
        mine.start()
        first = [copy(0, me, sibling, src=x_ref)]
        first += [copy(1 + j, me, (*chip, c), src=x_ref) for j, chip in enumerate(chips)]
        for cp in first:
            cp.start()
        passed = [copy(4 + j, (*chip, c), sibling) for j, chip in enumerate(chips)]
        for j, chip in enumerate(chips):
            copy(1 + j, (*chip, c), me).wait_recv()
            passed[j].start()
        copy(0, sibling, me).wait_recv()
        for j, chip in enumerate(chips):
            copy(4 + j, (*chip, 1 - c), me).wait_recv()
        for cp in first + passed:
            cp.wait_send()
        mine.wait()

    return pl.pallas_call(
        body, name=name, out_shape=jax.ShapeDtypeStruct((N_DEV * m_per, n), v.dtype),
        in_specs=[pl.BlockSpec(memory_space=pltpu.VMEM)], out_specs=pl.BlockSpec(memory_space=pltpu.VMEM),
        scratch_shapes=[pltpu.SemaphoreType.DMA((7,)), pltpu.SemaphoreType.DMA((7,)), pltpu.SemaphoreType.DMA],
        compiler_params=_cp(),
    )(v)


_HBM = pl.BlockSpec(memory_space=pltpu.HBM)


def _allgather_rows(shards, name):
    n = len(shards)

    def body(*refs):
        src, dst = refs[:n], refs[n:2 * n]
        send_sems, recv_sems = refs[2 * n:]
        x, y, c = _mesh_pos()
        chips = _other_chips(x, y)

        def half(i, px, py, pc):
            r = src[i].shape[0]
            return dst[i].at[pl.ds(pl.multiple_of((2 * px + py) * r + pc * (r // 2), 16), r // 2), :]

        def copy(i, k, block, to, own=False):
            r = src[i].shape[0]
            mine = src[i].at[pl.ds(pl.multiple_of(c * (r // 2), 16), r // 2), :]
            return pltpu.make_async_remote_copy(
                src_ref=mine if own else half(i, *block), dst_ref=half(i, *block), send_sem=send_sems.at[6 * i + k],
                recv_sem=recv_sems.at[6 * i + k], device_id=to, device_id_type=MESH)

        first = [copy(i, j, (x, y, c), (*chip, c), own=True) for i in range(n) for j, chip in enumerate(chips)]
        for cp in first:
            cp.start()
        passed = []
        for i in range(n):
            for j, chip in enumerate(chips):
                copy(i, j, (*chip, c), (x, y, c)).wait_recv()
                passed.append(copy(i, 3 + j, (*chip, c), (x, y, 1 - c)))
                passed[-1].start()
        for i in range(n):
            for j, chip in enumerate(chips):
                copy(i, 3 + j, (*chip, 1 - c), (x, y, c)).wait_recv()
        for cp in first + passed:
            cp.wait_send()

    gathered = pl.pallas_call(
        body, name=name,
        out_shape=[jax.ShapeDtypeStruct((N_SHARD * s.shape[0], s.shape[1]), s.dtype) for s in shards],
        in_specs=[_HBM] * n, out_specs=[_HBM] * n,
        scratch_shapes=[pltpu.SemaphoreType.DMA((6 * n,)), pltpu.SemaphoreType.DMA((6 * n,))],
        compiler_params=_cp(),
    )(*shards)
    chip = 2 * lax.axis_index("x") + lax.axis_index("y")
    return [lax.dynamic_update_slice(full, s, (chip * s.shape[0], 0)) for full, s in zip(gathered, shards)]


def _swap_halves(parts, name):
    n = len(parts)

    def body(*refs):
        src, dst = refs[:n], refs[n:2 * n]
        send_sems, recv_sems = refs[2 * n:]
        x, y, c = _mesh_pos()
        copies = [pltpu.make_async_remote_copy(
            src_ref=src[i].at[s, 1 - c], dst_ref=dst[i].at[s], send_sem=send_sems.at[N_SHARD * i + s],
            recv_sem=recv_sems.at[N_SHARD * i + s], device_id=(x, y, 1 - c), device_id_type=MESH)
            for i in range(n) for s in range(N_SHARD)]
        for cp in copies:
            cp.start()
        for cp in copies:
            cp.wait()

    return pl.pallas_call(
        body, name=name,
        out_shape=[jax.ShapeDtypeStruct((N_SHARD,) + p.shape[2:], p.dtype) for p in parts],
        in_specs=[_HBM] * n, out_specs=[_HBM] * n,
        scratch_shapes=[pltpu.SemaphoreType.DMA((N_SHARD * n,)), pltpu.SemaphoreType.DMA((N_SHARD * n,))],
        compiler_params=_cp(),
    )(*parts)


def _scatter_partials(parts, name):
    n = len(parts)

    def body(*refs):
        src, dst = refs[:n], refs[n:2 * n]
        send_sems, recv_sems = refs[2 * n:]
        x, y, c = _mesh_pos()
        chips = _other_chips(x, y)

        def copy(i, j, chip_to):
            return pltpu.make_async_remote_copy(
                src_ref=src[i].at[2 * chip_to[0] + chip_to[1]], dst_ref=dst[i].at[j], send_sem=send_sems.at[3 * i + j],
                recv_sem=recv_sems.at[3 * i + j], device_id=(*chip_to, c), device_id_type=MESH)

        copies = [copy(i, j, chip) for i in range(n) for j, chip in enumerate(chips)]
        for cp in copies:
            cp.start()
        for cp in copies:
            cp.wait()

    return pl.pallas_call(
        body, name=name,
        out_shape=[jax.ShapeDtypeStruct((3,) + p.shape[1:], p.dtype) for p in parts],
        in_specs=[_HBM] * n, out_specs=[_HBM] * n,
        scratch_shapes=[pltpu.SemaphoreType.DMA((3 * n,)), pltpu.SemaphoreType.DMA((3 * n,))],
        compiler_params=_cp(),
    )(*parts)


def _join_halves(halves, name):
    n = len(halves)

    def body(*refs):
        src, dst = refs[:n], refs[n:2 * n]
        send_sems, recv_sems = refs[2 * n:]
        x, y, c = _mesh_pos()
        remote = [pltpu.make_async_remote_copy(
            src_ref=src[i], dst_ref=dst[i].at[c], send_sem=send_sems.at[i], recv_sem=recv_sems.at[i],
            device_id=(x, y, 1 - c), device_id_type=MESH) for i in range(n)]
        for cp in remote:
            cp.start()
        for i in range(n):
            pltpu.make_async_remote_copy(
                src_ref=src[i], dst_ref=dst[i].at[1 - c], send_sem=send_sems.at[i], recv_sem=recv_sems.at[i],
                device_id=(x, y, 1 - c), device_id_type=MESH).wait_recv()
        for cp in remote:
            cp.wait_send()

    joined = pl.pallas_call(
        body, name=name,
        out_shape=[jax.ShapeDtypeStruct((2,) + h.shape, h.dtype) for h in halves],
        in_specs=[_HBM] * n, out_specs=[_HBM] * n,
        scratch_shapes=[pltpu.SemaphoreType.DMA((n,)), pltpu.SemaphoreType.DMA((n,))],
        compiler_params=_cp(),
    )(*halves)
    c = lax.axis_index("c")
    return [lax.dynamic_update_slice(j, h[None], (c, 0, 0)) for j, h in zip(joined, halves)]


def _pair_sum(g, got, name):
    _, _, hr, d = g.shape
    t = _tile(hr, 256, 16)

    def body(g0_ref, g1_ref, got_ref, wire_ref, own_ref):
        x, y, c = _mesh_pos()
        total = jnp.where(c == 0, g0_ref[0, 0], g1_ref[0, 0]) + got_ref[0]
        wire_ref[0] = total.astype(wire_ref.dtype)

        @pl.when(pl.program_id(1) == 2 * x + y)
        def _():
            own_ref[...] = total

    return pl.pallas_call(
        body, name=name, grid=(hr // t, N_SHARD),
        in_specs=[pl.BlockSpec((1, 1, t, d), lambda i, s: (s, 0, i, 0)), pl.BlockSpec((1, 1, t, d), lambda i, s: (s, 1, i, 0)),
                  pl.BlockSpec((1, t, d), lambda i, s: (s, i, 0))],
        out_specs=[pl.BlockSpec((1, t, d), lambda i, s: (s, i, 0)), pl.BlockSpec((t, d), lambda i, s: (i, 0))],
        out_shape=[jax.ShapeDtypeStruct((N_SHARD, hr, d), WIRE_DTYPE), jax.ShapeDtypeStruct((hr, d), F32)],
        compiler_params=_cp(("parallel", "arbitrary")),
    )(g, g, got)


def _sum_partials(own, recv, name):
    hr, d = own.shape
    t = _tile(hr, 256, 16)

    def body(own_ref, recv_ref, o_ref):
        total = own_ref[...]
        for j in range(3):
            total = total + recv_ref[j].astype(F32)
        o_ref[...] = total

    blk = pl.BlockSpec((t, d), lambda i: (i, 0))
    return pl.pallas_call(
        body, name=name, grid=(hr // t,), in_specs=[blk, pl.BlockSpec((3, t, d), lambda i: (0, i, 0))], out_specs=blk,
        out_shape=jax.ShapeDtypeStruct((hr, d), F32), compiler_params=_cp(("parallel",)),
    )(own, recv)


def _reduce_scatter(grads):
    split = [g.reshape(N_SHARD, 2, g.shape[0] // (2 * N_SHARD), g.shape[1]) for g in grads]
    got = _swap_halves(split, "rs_swap_halves")
    sums = [_pair_sum(g, h, f"rs_pair_sum_{i}") for i, (g, h) in enumerate(zip(split, got))]
    recv = _scatter_partials([w for w, _ in sums], "rs_scatter")
    halves = [_sum_partials(own, rv, f"rs_sum_{i}") for i, ((_, own), rv) in enumerate(zip(sums, recv))]
    return [j.reshape(-1, j.shape[-1]) for j in _join_halves(halves, "rs_join_halves")]


def _mod_fwd(x, ctx, nw, sc, sh, csc, csh):
    l, d = x.shape
    lc = ctx.shape[0]
    t = min(256, lc)
    nl, nc = l // t, lc // t

    def body(x_ref, c_ref, nw_ref, sc_ref, sh_ref, csc_ref, csh_ref, o_ref):
        i = pl.program_id(0)

        @pl.when(i < nl)
        def _():
            o_ref[...] = _f_mod(x_ref[...], nw_ref[...], sc_ref[...], sh_ref[...]).astype(o_ref.dtype)

        @pl.when(i >= nl)
        def _():
            o_ref[...] = _f_mod(c_ref[...], nw_ref[...], csc_ref[...], csh_ref[...]).astype(o_ref.dtype)

    return pl.pallas_call(
        body, name="mod_fwd", grid=(nl + nc,),
        in_specs=[pl.BlockSpec((t, d), lambda i: (jnp.minimum(i, nl - 1), 0)),
                  pl.BlockSpec((t, d), lambda i: (jnp.maximum(i - nl, 0), 0))] + [_bc_spec(d)] * 5,
        out_specs=pl.BlockSpec((t, d), lambda i: (i, 0)),
        out_shape=jax.ShapeDtypeStruct((l + lc, d), MXU_DTYPE), compiler_params=_cp(("arbitrary",)),
    )(x, ctx, nw, sc, sh, csc, csh)


def _mod_bwd(x, ctx, nw, sc, sh, csc, csh, dhx, dx_res):
    l, d = x.shape
    lc = ctx.shape[0]
    t = min(256, lc)
    nl, nc = l // t, lc // t

    def body(x_ref, c_ref, nw_ref, sc_ref, sh_ref, csc_ref, csh_ref, dh_ref, dr_ref,
             dx_ref, dnw_ref, dsc_ref, dsh_ref, dcsc_ref, dcsh_ref):
        i = pl.program_id(0)

        @pl.when(i == 0)
        def _():
            for r in (dnw_ref, dsc_ref, dsh_ref, dcsc_ref, dcsh_ref):
                r[...] = jnp.zeros_like(r)

        @pl.when(i < nl)
        def _():
            _, vjp = jax.vjp(_f_mod, x_ref[...], nw_ref[...], sc_ref[...], sh_ref[...])
            dx, dnw, dsc, dsh = vjp(dh_ref[...])
            dx_ref[...] = dx + dr_ref[...]
            dnw_ref[...] += dnw
            dsc_ref[...] += dsc
            dsh_ref[...] += dsh

        @pl.when(i >= nl)
        def _():
            _, vjp = jax.vjp(_f_mod, c_ref[...], nw_ref[...], csc_ref[...], csh_ref[...])
            _, dnw, dsc, dsh = vjp(dh_ref[...])
            dnw_ref[...] += dnw
            dcsc_ref[...] += dsc
            dcsh_ref[...] += dsh

    lat = pl.BlockSpec((t, d), lambda i: (jnp.minimum(i, nl - 1), 0))
    vec = jax.ShapeDtypeStruct((1, d), F32)
    return pl.pallas_call(
        body, name="mod_bwd", grid=(nl + nc,),
        in_specs=[lat, pl.BlockSpec((t, d), lambda i: (jnp.maximum(i - nl, 0), 0))] + [_bc_spec(d)] * 5
        + [pl.BlockSpec((t, d), lambda i: (i, 0)), lat],
        out_specs=[lat] + [_bc_spec(d)] * 5,
        out_shape=[jax.ShapeDtypeStruct((l, d), F32)] + [vec] * 5, compiler_params=_cp(("arbitrary",)),
    )(x, ctx, nw, sc, sh, csc, csh, dhx, dx_res)


def _gate_fwd(yf, yr, xbc, proj, dsk, wn, l, di):
    t = 128

    def body(yf_ref, yr_ref, xs_ref, z_ref, dsk_ref, wn_ref, o_ref):
        o_ref[...] = _f_gate(yf_ref[...], yr_ref[...], xs_ref[...], z_ref[...], dsk_ref[...], wn_ref[...]).astype(o_ref.dtype)

    row = pl.BlockSpec((t, di), lambda i: (i, 0))
    return pl.pallas_call(
        body, name="gate_fwd", grid=(l // t,), in_specs=[row] * 4 + [_bc_spec(di)] * 2, out_specs=row,
        out_shape=jax.ShapeDtypeStruct((l, di), MXU_DTYPE), compiler_params=_cp(("parallel",)),
    )(yf, yr, xbc, proj, dsk, wn)


def _gate_bwd(yf, yr, xbc, proj, dsk, wn, dya, l, lc, di):
    t = 128
    nl, nc = l // t, lc // t

    def body(yf_ref, yr_ref, xs_ref, z_ref, dsk_ref, wn_ref, g_ref, dy_ref, dz_ref, ddsk_ref, dwn_ref):
        i = pl.program_id(0)

        @pl.when(i == 0)
        def _():
            ddsk_ref[...] = jnp.zeros_like(ddsk_ref)
            dwn_ref[...] = jnp.zeros_like(dwn_ref)

        @pl.when(i < nl)
        def _():
            _, vjp = jax.vjp(_f_gate, yf_ref[...], yr_ref[...], xs_ref[...], z_ref[...], dsk_ref[...], wn_ref[...])
            dyf, _, _, dz, ddsk, dwn = vjp(g_ref[...])
            dy_ref[...] = dyf
            dz_ref[...] = dz.astype(dz_ref.dtype)
            ddsk_ref[...] += ddsk
            dwn_ref[...] += dwn

        @pl.when(i >= nl)
        def _():
            dz_ref[...] = jnp.zeros_like(dz_ref)

    lat = pl.BlockSpec((t, di), lambda i: (jnp.minimum(i, nl - 1), 0))
    vec = jax.ShapeDtypeStruct((1, di), F32)
    return pl.pallas_call(
        body, name="gate_bwd", grid=(nl + nc,),
        in_specs=[lat] * 4 + [_bc_spec(di)] * 2 + [lat],
        out_specs=[lat, pl.BlockSpec((t, di), lambda i: (i, 0))] + [_bc_spec(di)] * 2,
        out_shape=[jax.ShapeDtypeStruct((l, di), F32), jax.ShapeDtypeStruct((l + lc, di), MXU_DTYPE)] + [vec] * 2,
        compiler_params=_cp(("arbitrary",)),
    )(yf, yr, xbc, proj, dsk, wn, dya)


def _ln_fwd(cv, g, b):
    l, d = cv.shape
    t = 256

    def body(cv_ref, g_ref, b_ref, o_ref):
        o_ref[...] = _f_ln(cv_ref[...], g_ref[...], b_ref[...]).astype(o_ref.dtype)

    row = pl.BlockSpec((t, d), lambda i: (i, 0))
    return pl.pallas_call(
        body, name="ln_fwd", grid=(l // t,), in_specs=[row] + [_bc_spec(d)] * 2, out_specs=row,
        out_shape=jax.ShapeDtypeStruct((l, d), MXU_DTYPE), compiler_params=_cp(("parallel",)),
    )(cv, g, b)


def _ln_bwd(cv, g, b, dcf):
    l, d = cv.shape
    t = 256

    def body(cv_ref, g_ref, b_ref, dcf_ref, dcv_ref, dg_ref, db_ref):
        _, vjp = jax.vjp(_f_ln, cv_ref[...], g_ref[...], b_ref[...])
        dcv, dg, db = vjp(dcf_ref[...])
        dcv_ref[...] = dcv
        i = pl.program_id(0)
        _acc(dg_ref, dg, i)
        _acc(db_ref, db, i)

    row = pl.BlockSpec((t, d), lambda i: (i, 0))
    vec = jax.ShapeDtypeStruct((1, d), F32)
    return pl.pallas_call(
        body, name="ln_bwd", grid=(l // t,), in_specs=[row] + [_bc_spec(d)] * 2 + [row],
        out_specs=[row] + [_bc_spec(d)] * 2, out_shape=[jax.ShapeDtypeStruct((l, d), F32), vec, vec],
        compiler_params=_cp(("arbitrary",)),
    )(cv, g, b, dcf)


def _merge_fwd(ya, yb, proj, ga_blk):
    l, d = ya.shape
    t = 256

    def body(ya_ref, yb_ref, ga_ref, gb_ref, o_ref):
        o_ref[...] = _f_merge(ya_ref[...], yb_ref[...], ga_ref[...], gb_ref[...]).astype(o_ref.dtype)

    row = pl.BlockSpec((t, d), lambda i: (i, 0))
    return pl.pallas_call(
        body, name="merge_fwd", grid=(l // t,),
        in_specs=[row, row, pl.BlockSpec((t, d), lambda i: (i, ga_blk)), pl.BlockSpec((t, d), lambda i: (i, ga_blk + 1))],
        out_specs=row, out_shape=jax.ShapeDtypeStruct((l, d), MXU_DTYPE), compiler_params=_cp(("parallel",)),
    )(ya, yb, proj, proj)


def _merge_bwd(ya, yb, proj, ga_blk, dmerged, lc):
    l, d = ya.shape
    t = min(256, lc)
    nl, nc = l // t, lc // t

    def body(ya_ref, yb_ref, ga_ref, gb_ref, g_ref, dya_ref, dyb_ref, dga_ref, dgb_ref):
        i = pl.program_id(0)

        @pl.when(i < nl)
        def _():
            _, vjp = jax.vjp(_f_merge, ya_ref[...], yb_ref[...], ga_ref[...], gb_ref[...])
            dya, dyb, dga, dgb = vjp(g_ref[...])
            dya_ref[...] = dya.astype(dya_ref.dtype)
            dyb_ref[...] = dyb.astype(dyb_ref.dtype)
            dga_ref[...] = dga.astype(dga_ref.dtype)
            dgb_ref[...] = dgb.astype(dgb_ref.dtype)

        @pl.when(i >= nl)
        def _():
            dga_ref[...] = jnp.zeros_like(dga_ref)
            dgb_ref[...] = jnp.zeros_like(dgb_ref)

    lat = pl.BlockSpec((t, d), lambda i: (jnp.minimum(i, nl - 1), 0))
    full = pl.BlockSpec((t, d), lambda i: (i, 0))
    return pl.pallas_call(
        body, name="merge_bwd", grid=(nl + nc,),
        in_specs=[lat, lat, pl.BlockSpec((t, d), lambda i: (jnp.minimum(i, nl - 1), ga_blk)),
                  pl.BlockSpec((t, d), lambda i: (jnp.minimum(i, nl - 1), ga_blk + 1)), lat],
        out_specs=[lat, lat, full, full],
        out_shape=[jax.ShapeDtypeStruct((l, d), MXU_DTYPE)] * 2 + [jax.ShapeDtypeStruct((l + lc, d), MXU_DTYPE)] * 2,
        compiler_params=_cp(("arbitrary",)),
    )(ya, yb, proj, proj, dmerged)


def _res_fwd(x, mix, g1, wn, sc2, sh2):
    l, d = x.shape
    t = 256

    def body(x_ref, m_ref, g1_ref, wn_ref, sc_ref, sh_ref, x1_ref, hx_ref):
        x1, hx = _f_res(x_ref[...], m_ref[...], g1_ref[...], wn_ref[...], sc_ref[...], sh_ref[...])
        x1_ref[...] = x1
        hx_ref[...] = hx.astype(hx_ref.dtype)

    row = pl.BlockSpec((t, d), lambda i: (i, 0))
    return pl.pallas_call(
        body, name="res_fwd", grid=(l // t,), in_specs=[row, row] + [_bc_spec(d)] * 4, out_specs=[row, row],
        out_shape=[jax.ShapeDtypeStruct((l, d), F32), jax.ShapeDtypeStruct((l, d), MXU_DTYPE)],
        compiler_params=_cp(("parallel",)),
    )(x, mix, g1, wn, sc2, sh2)


def _res_bwd(x, mix, g1, wn, sc2, sh2, dx1, dhx2):
    l, d = x.shape
    t = 256

    def body(x_ref, m_ref, g1_ref, wn_ref, sc_ref, sh_ref, dx1_ref, dh_ref, dx_ref, dm_ref, dg1_ref, dwn_ref, dsc_ref, dsh_ref):
        _, vjp = jax.vjp(_f_res, x_ref[...], m_ref[...], g1_ref[...], wn_ref[...], sc_ref[...], sh_ref[...])
        dx, dm, dg1, dwn, dsc, dsh = vjp((dx1_ref[...], dh_ref[...]))
        dx_ref[...] = dx
        dm_ref[...] = dm.astype(dm_ref.dtype)
        i = pl.program_id(0)
        _acc(dg1_ref, dg1, i)
        _acc(dwn_ref, dwn, i)
        _acc(dsc_ref, dsc, i)
        _acc(dsh_ref, dsh, i)

    row = pl.BlockSpec((t, d), lambda i: (i, 0))
    vec = jax.ShapeDtypeStruct((1, d), F32)
    return pl.pallas_call(
        body, name="res_bwd", grid=(l // t,), in_specs=[row, row] + [_bc_spec(d)] * 4 + [row, row],
        out_specs=[row, row] + [_bc_spec(d)] * 4,
        out_shape=[jax.ShapeDtypeStruct((l, d), F32), jax.ShapeDtypeStruct((l, d), MXU_DTYPE)] + [vec] * 4,
        compiler_params=_cp(("arbitrary",)),
    )(x, mix, g1, wn, sc2, sh2, dx1, dhx2)


def _swiglu_fwd(gu, df):
    l = gu.shape[0]
    t = 256

    def body(g_ref, u_ref, o_ref):
        o_ref[...] = _f_swiglu(g_ref[...], u_ref[...]).astype(o_ref.dtype)

    return pl.pallas_call(
        body, name="swiglu_fwd", grid=(l // t,),
        in_specs=[pl.BlockSpec((t, df), lambda i: (i, 0)), pl.BlockSpec((t, df), lambda i: (i, 1))],
        out_specs=pl.BlockSpec((t, df), lambda i: (i, 0)),
        out_shape=jax.ShapeDtypeStruct((l, df), MXU_DTYPE), compiler_params=_cp(("parallel",)),
    )(gu, gu)


def _swiglu_bwd(gu, dact, df):
    l = gu.shape[0]
    t = 256

    lo, hi = pl.BlockSpec((t, df), lambda i: (i, 0)), pl.BlockSpec((t, df), lambda i: (i, 1))
    dgu = jax.ShapeDtypeStruct((l, 2 * df), MXU_DTYPE)

    def body(g_ref, u_ref, da_ref, dgu_ref):
        _, vjp = jax.vjp(_f_swiglu, g_ref[...], u_ref[...])
        dg, du = vjp(da_ref[...])
        dgu_ref[:, :df] = dg.astype(dgu_ref.dtype)
        dgu_ref[:, df:] = du.astype(dgu_ref.dtype)

    return pl.pallas_call(
        body, name="swiglu_bwd", grid=(l // t,), in_specs=[lo, hi, lo],
        out_specs=pl.BlockSpec((t, 2 * df), lambda i: (i, 0)), out_shape=dgu, compiler_params=_cp(("parallel",)),
    )(gu, gu, dact)


def _loss_and_grads(x1, dn, g2, wn, tgt):
    l, d = x1.shape
    t = 256

    def body(x1_ref, dn_ref, g2_ref, wn_ref, t_ref, loss_ref, dx_ref, ddn_ref, dg2_ref, dwn_ref):
        loss, vjp = jax.vjp(lambda a, b, c, e: _f_loss(a, b, c, e, t_ref[...]), x1_ref[...], dn_ref[...], g2_ref[...], wn_ref[...])
        dx, ddn, dg2, dwn = vjp(jnp.ones((1, 1), F32))
        dx_ref[...] = dx
        ddn_ref[...] = ddn.astype(ddn_ref.dtype)
        i = pl.program_id(0)
        _acc(loss_ref, loss, i)
        _acc(dg2_ref, dg2, i)
        _acc(dwn_ref, dwn, i)

    row = pl.BlockSpec((t, d), lambda i: (i, 0))
    vec = jax.ShapeDtypeStruct((1, d), F32)
    return pl.pallas_call(
        body, name="loss_and_grads", grid=(l // t,), in_specs=[row, row] + [_bc_spec(d)] * 2 + [row],
        out_specs=[pl.BlockSpec((1, 1), lambda i: (0, 0)), row, row] + [_bc_spec(d)] * 2,
        out_shape=[jax.ShapeDtypeStruct((1, 1), F32), jax.ShapeDtypeStruct((l, d), F32),
                   jax.ShapeDtypeStruct((l, d), MXU_DTYPE), vec, vec],
        compiler_params=_cp(("arbitrary",)),
    )(x1, dn, g2, wn, tgt)


PAD = 8


def _conv5_taps(s_ref, w_ref, l, lc, width):
    half = width // 2
    lat = sum(w_ref[k:k + 1, :] * s_ref[pl.ds(PAD + k - half, l), :] for k in range(width))
    ctx = sum(w_ref[k:k + 1, :] * s_ref[pl.ds(2 * PAD + l + k - half, lc), :] for k in range(width))
    return lat, ctx


def _fill_padded(s_ref, lat, ctx, l, lc):
    zeros = jnp.zeros((PAD, s_ref.shape[1]), F32)
    s_ref[pl.ds(0, PAD), :] = zeros
    s_ref[pl.ds(PAD, l), :] = lat
    s_ref[pl.ds(PAD + l, PAD), :] = zeros
    s_ref[pl.ds(2 * PAD + l, lc), :] = ctx
    s_ref[pl.ds(2 * PAD + l + lc, PAD), :] = zeros


def _conv5_fwd(proj, w, b, l, lc, col0, ncols):
    t_all = l + lc
    cw = LANES
    blk0 = col0 // cw
    width = w.shape[0]

    def body(x_ref, w_ref, b_ref, o_ref, s_ref):
        _fill_padded(s_ref, x_ref[pl.ds(0, l), :], x_ref[pl.ds(l, lc), :], l, lc)
        lat, ctx = _conv5_taps(s_ref, w_ref, l, lc, width)
        o_ref[pl.ds(0, l), :] = _silu(lat + b_ref[...])
        o_ref[pl.ds(l, lc), :] = _silu(ctx + b_ref[...])

    return pl.pallas_call(
        body, name="conv5_fwd", grid=(ncols // cw,),
        in_specs=[pl.BlockSpec((t_all, cw), lambda j: (0, blk0 + j)), pl.BlockSpec((width, cw), lambda j: (0, j)),
                  pl.BlockSpec((1, cw), lambda j: (0, j))],
        out_specs=pl.BlockSpec((t_all, cw), lambda j: (0, j)),
        out_shape=jax.ShapeDtypeStruct((t_all, ncols), F32),
        scratch_shapes=[pltpu.VMEM((t_all + 3 * PAD, cw), F32)], compiler_params=_cp(("parallel",)),
    )(proj, w, b)


def _conv5_bwd(proj, w, b, cots, l, lc, col0, seg0, ncols):
    t_all = l + lc
    cw = LANES
    blk0, sblk0 = col0 // cw, seg0 // cw
    width = w.shape[0]
    half = width // 2
    nc = len(cots)

    def body(*refs):
        x_ref, w_ref, b_ref = refs[:3]
        cot_refs = refs[3:3 + nc]
        dx_ref, dw_ref, db_ref, s_ref = refs[3 + nc:]
        x_lat, x_ctx = x_ref[pl.ds(0, l), :], x_ref[pl.ds(l, lc), :]
        _fill_padded(s_ref, x_lat, x_ctx, l, lc)
        pre_lat, pre_ctx = _conv5_taps(s_ref, w_ref, l, lc, width)
        g = sum(c[...] for c in cot_refs)

        def through_silu(pre, cot):
            _, vjp = jax.vjp(_silu, pre + b_ref[...])
            return vjp(cot)[0]

        d_lat = through_silu(pre_lat, g[:l])
        d_ctx = through_silu(pre_ctx, g[l:])
        db_ref[...] = jnp.sum(d_lat, axis=0, keepdims=True) + jnp.sum(d_ctx, axis=0, keepdims=True)
        for k in range(width):
            dw_ref[k:k + 1, :] = (
                jnp.sum(d_lat * s_ref[pl.ds(PAD + k - half, l), :], axis=0, keepdims=True)
                + jnp.sum(d_ctx * s_ref[pl.ds(2 * PAD + l + k - half, lc), :], axis=0, keepdims=True))
        _fill_padded(s_ref, d_lat, d_ctx, l, lc)
        dx_lat = sum(w_ref[k:k + 1, :] * s_ref[pl.ds(PAD - (k - half), l), :] for k in range(width))
        dx_ctx = sum(w_ref[k:k + 1, :] * s_ref[pl.ds(2 * PAD + l - (k - half), lc), :] for k in range(width))
        dx_ref[pl.ds(0, l), :] = dx_lat.astype(dx_ref.dtype)
        dx_ref[pl.ds(l, lc), :] = dx_ctx.astype(dx_ref.dtype)

    col = pl.BlockSpec((t_all, cw), lambda j: (0, j))
    return pl.pallas_call(
        body, name=f"conv5_bwd_{seg0}", grid=(ncols // cw,),
        in_specs=[pl.BlockSpec((t_all, cw), lambda j: (0, blk0 + sblk0 + j)),
                  pl.BlockSpec((width, cw), lambda j: (0, sblk0 + j)), pl.BlockSpec((1, cw), lambda j: (0, sblk0 + j))]
        + [col] * nc,
        out_specs=[col, pl.BlockSpec((width, cw), lambda j: (0, j)), pl.BlockSpec((1, cw), lambda j: (0, j))],
        out_shape=[jax.ShapeDtypeStruct((t_all, ncols), MXU_DTYPE), jax.ShapeDtypeStruct((width, ncols), F32),
                   jax.ShapeDtypeStruct((1, ncols), F32)],
        scratch_shapes=[pltpu.VMEM((t_all + 3 * PAD, cw), F32)], compiler_params=_cp(("parallel",)),
    )(proj, w, b, *cots)


def _conv31_fwd(proj, w, b, l, d, u_blk):
    cw = LANES
    width = w.shape[0]
    reach = (width // 2) * GRID_W
    nb = d // cw

    def body(u_ref, v_ref, w_ref, b_ref, o_ref, s_ref):
        s_ref[pl.ds(0, reach), :] = jnp.zeros((reach, cw), F32)
        s_ref[pl.ds(reach, l), :] = u_ref[...] * jax.nn.sigmoid(v_ref[...])
        s_ref[pl.ds(reach + l, reach), :] = jnp.zeros((reach, cw), F32)
        o_ref[...] = sum(w_ref[k:k + 1, :] * s_ref[pl.ds(k * GRID_W, l), :] for k in range(width)) + b_ref[...]

    return pl.pallas_call(
        body, name="conv31_fwd", grid=(nb,),
        in_specs=[pl.BlockSpec((l, cw), lambda j: (0, u_blk * nb + j)), pl.BlockSpec((l, cw), lambda j: (0, (u_blk + 1) * nb + j)),
                  pl.BlockSpec((width, cw), lambda j: (0, j)), pl.BlockSpec((1, cw), lambda j: (0, j))],
        out_specs=pl.BlockSpec((l, cw), lambda j: (0, j)), out_shape=jax.ShapeDtypeStruct((l, d), F32),
        scratch_shapes=[pltpu.VMEM((l + 2 * reach, cw), F32)], compiler_params=_cp(("parallel",)),
    )(proj, proj, w, b)


def _conv31_bwd(proj, w, dcv, l, lc, d, u_blk):
    cw = LANES
    width = w.shape[0]
    reach = (width // 2) * GRID_W
    nb = d // cw
    t_all = l + lc

    def body(u_ref, v_ref, w_ref, g_ref, du_ref, dv_ref, dw_ref, db_ref, s_ref):
        zeros = jnp.zeros((reach, cw), F32)
        s_ref[pl.ds(0, reach), :] = zeros
        s_ref[pl.ds(reach + l, reach), :] = zeros
        u, v, g = u_ref[...], v_ref[...], g_ref[...]
        s_ref[pl.ds(reach, l), :] = u * jax.nn.sigmoid(v)
        db_ref[...] = jnp.sum(g, axis=0, keepdims=True)
        for k in range(width):
            dw_ref[k:k + 1, :] = jnp.sum(g * s_ref[pl.ds(k * GRID_W, l), :], axis=0, keepdims=True)
        s_ref[pl.ds(reach, l), :] = g
        dt = sum(w_ref[k:k + 1, :] * s_ref[pl.ds((width - 1 - k) * GRID_W, l), :] for k in range(width))
        _, vjp = jax.vjp(lambda a, c: a * jax.nn.sigmoid(c), u, v)
        du, dv = vjp(dt)
        du_ref[pl.ds(0, l), :] = du.astype(du_ref.dtype)
        dv_ref[pl.ds(0, l), :] = dv.astype(dv_ref.dtype)
        du_ref[pl.ds(l, lc), :] = jnp.zeros((lc, cw), du_ref.dtype)
        dv_ref[pl.ds(l, lc), :] = jnp.zeros((lc, cw), dv_ref.dtype)

    pshape = jax.ShapeDtypeStruct((t_all, d), MXU_DTYPE)
    tall = pl.BlockSpec((t_all, cw), lambda j: (0, j))
    return pl.pallas_call(
        body, name="conv31_bwd", grid=(nb,),
        in_specs=[pl.BlockSpec((l, cw), lambda j: (0, u_blk * nb + j)), pl.BlockSpec((l, cw), lambda j: (0, (u_blk + 1) * nb + j)),
                  pl.BlockSpec((width, cw), lambda j: (0, j)), pl.BlockSpec((l, cw), lambda j: (0, j))],
        out_specs=[tall, tall, pl.BlockSpec((width, cw), lambda j: (0, j)), pl.BlockSpec((1, cw), lambda j: (0, j))],
        out_shape=[pshape, pshape, jax.ShapeDtypeStruct((width, d), F32), jax.ShapeDtypeStruct((1, d), F32)],
        scratch_shapes=[pltpu.VMEM((l + 2 * reach, cw), F32)], compiler_params=_cp(("parallel",)),
    )(proj, proj, w, dcv)


def _softplus(x):
    return jnp.maximum(x, 0.0) + jnp.log(1.0 + jnp.exp(-jnp.abs(x)))


def _dt_fwd(proj, bias, a, dt_blk):
    t_all = proj.shape[0]
    hh = bias.shape[1]
    q = CHUNK

    def body(r_ref, b_ref, a_ref, dt_ref, cs_ref, tc_ref, cst_ref):
        dt = _softplus(r_ref[...] + b_ref[...])
        dt_ref[...] = dt
        da = dt * a_ref[...]
        li, si = _iota((q, q), 0), _iota((q, q), 1)
        reverse_cols = _iota((q, hh), 1) >= hh // 2
        cs = jnp.where(reverse_cols, _dot((si >= li).astype(F32), da, exact=True), _dot((si <= li).astype(F32), da, exact=True))
        cs_ref[...] = cs
        cst_ref[...] = cs.T
        total = jnp.where(_iota((1, hh), 1) >= hh // 2, cs_ref[0:1, :], cs_ref[q - 1:q, :])
        tc_ref[...] = total - cs

    row = pl.BlockSpec((q, hh), lambda i: (i, 0))
    shape = jax.ShapeDtypeStruct((t_all, hh), F32)
    return pl.pallas_call(
        body, name="dt_fwd", grid=(t_all // q,),
        in_specs=[pl.BlockSpec((q, hh), lambda i: (i, dt_blk)), _bc_spec(hh), _bc_spec(hh)],
        out_specs=[row, row, row, pl.BlockSpec((hh, q), lambda i: (0, i))],
        out_shape=[shape, shape, shape, jax.ShapeDtypeStruct((hh, t_all), F32)],
        compiler_params=_cp(("parallel",)),
    )(proj, bias, a)


def _three_way(x):
    def top(v):
        word = lax.bitcast_convert_type(v, jnp.uint32) & jnp.uint32(0xFFFF0000)
        return lax.bitcast_convert_type(word, F32)

    hi = top(x)
    rest = x - hi
    mid = top(rest)
    return hi.astype(jnp.bfloat16), mid.astype(jnp.bfloat16), (rest - mid).astype(jnp.bfloat16)


def _scan_columns(dt, cs, tc, groups2, hpg):
    t_all = dt.shape[0]
    parts = [part.reshape(t_all, groups2, 1, hpg) for arr in (dt, cs, tc) for part in _three_way(arr)]
    cols = jnp.concatenate(parts, axis=2).transpose(1, 0, 2, 3).reshape(groups2, t_all, 9 * hpg)
    return jnp.pad(cols, ((0, 0), (0, 0), (0, LANES - 9 * hpg)))


def _dt_bwd(proj, bias, dt, ddt, dda, dt_blk):
    t_all = proj.shape[0]
    hh = bias.shape[1]
    q = _tile(t_all, 1024, LANES)

    def body(r_ref, b_ref, dt_ref, ddt_ref, dda_ref, dr_ref, db_ref, da_ref):
        dr = ddt_ref[...] * jax.nn.sigmoid(r_ref[...] + b_ref[...])
        dr_ref[...] = dr.astype(dr_ref.dtype)
        i = pl.program_id(0)
        _acc(db_ref, jnp.sum(dr, axis=0, keepdims=True), i)
        _acc(da_ref, jnp.sum(dda_ref[...] * dt_ref[...], axis=0, keepdims=True), i)

    row = pl.BlockSpec((q, hh), lambda i: (i, 0))
    vec = jax.ShapeDtypeStruct((1, hh), F32)
    return pl.pallas_call(
        body, name="dt_bwd", grid=(t_all // q,),
        in_specs=[pl.BlockSpec((q, hh), lambda i: (i, dt_blk)), _bc_spec(hh), row, row, row],
        out_specs=[row, _bc_spec(hh), _bc_spec(hh)],
        out_shape=[jax.ShapeDtypeStruct((t_all, hh), MXU_DTYPE), vec, vec], compiler_params=_cp(("arbitrary",)),
    )(proj, bias, dt, ddt, dda)


_NT = (((1,), (1,)), ((), ()))
_TN = (((0,), (0,)), ((), ()))


def _dot(a, b, dims=None, exact=False):
    kw = dict(preferred_element_type=F32)
    if exact:
        kw["precision"] = HI
    if dims is None:
        return jnp.dot(a, b, **kw)
    return lax.dot_general(a, b, dims, **kw)


def _iota(shape, dim):
    return lax.broadcasted_iota(jnp.int32, shape, dim)


class _Ssd:
    def __init__(self, l, lc, di, p, reverse):
        self.q, self.n, self.g = CHUNK, SSM_STATE, SSM_GROUPS
        self.nl, self.ncx = l // CHUNK, lc // CHUNK
        self.ns = self.nl + self.ncx
        self.t_all, self.di, self.p, self.reverse = l + lc, di, p, reverse
        self.hpg = di // p // SSM_GROUPS
        self.gw = self.hpg * p
        self.ntile = self.gw // LANES
        self.hpt = LANES // p
        self.log2p = p.bit_length() - 1
        assert 1 << self.log2p == p and self.gw % LANES == 0 and self.n == LANES and self.q == LANES
        assert 9 * self.hpg <= LANES
        self.d = 1 if reverse else 0

    def chunk_at(self, step):
        if self.reverse:
            return self.ns - 1 - step
        return jnp.where(step < self.ncx, self.nl + step, step - self.ncx)

    def selectors(self):
        hpg = self.hpg
        k = jnp.arange(LANES)
        quantity, head, used = k // (3 * hpg), k % hpg, k < 9 * hpg
        lane_head = jnp.arange(LANES) // self.p
        tiles = jnp.concatenate([(used & (quantity == qo))[:, None] & (head[:, None] == tt * self.hpt + lane_head[None, :])
                                 for tt in range(self.ntile) for qo in range(3)], axis=1)
        heads = jnp.concatenate([jnp.broadcast_to((used & (quantity == 1) & (head == j))[:, None], (LANES, LANES))
                                 for j in range(hpg)], axis=1)
        return tiles.astype(jnp.bfloat16), heads.astype(jnp.bfloat16)

    def in_specs(self, chunk_of):
        g, n, hpg, q = self.g, self.n, self.hpg, self.q
        b_blk, c_blk = self.di // n, self.di // n + g
        d = self.d
        return [
            pl.BlockSpec((q, self.gw), lambda gi, i: (chunk_of(i), gi)),
            pl.BlockSpec((q, n), lambda gi, i: (chunk_of(i), b_blk + gi)),
            pl.BlockSpec((q, n), lambda gi, i: (chunk_of(i), c_blk + gi)),
            pl.BlockSpec((1, q, LANES), lambda gi, i: (d * g + gi, chunk_of(i), 0)),
            pl.BlockSpec((hpg, q), lambda gi, i: (d * g + gi, chunk_of(i))),
            pl.BlockSpec((LANES, self.ntile * 3 * LANES), lambda gi, i: (0, 0)),
            pl.BlockSpec((LANES, hpg * LANES), lambda gi, i: (0, 0)),
        ]

    def masks(self):
        li, si = _iota((self.q, self.q), 0), _iota((self.q, self.q), 1)
        if self.reverse:
            return si >= li, li >= si
        return si <= li, li <= si

    def spread(self, spread_all, tt):
        at = 3 * LANES * tt
        return tuple(spread_all[:, at + k * LANES:at + (k + 1) * LANES] for k in range(3))

    def head_lanes(self, qq):
        return lax.shift_right_logical(_iota((self.q, LANES), 1), self.log2p) == qq

    def head_sums(self, values, tt):
        sel = _iota((HEAD_COLS, LANES), 0) == tt * self.hpt + lax.shift_right_logical(_iota((HEAD_COLS, LANES), 1), self.log2p)
        parts = [part for v in values for part in _three_way(v)]
        sums = _dot(jnp.concatenate(parts, axis=0), sel.astype(jnp.bfloat16), _NT)
        out, at = [], 0
        for v in values:
            rows = v.shape[0]
            out.append(sums[at:at + rows] + sums[at + rows:at + 2 * rows] + sums[at + 2 * rows:at + 3 * rows])
            at += 3 * rows
        return out

    def state_scale(self, csr_ref):
        last = 0 if self.reverse else self.q - 1
        total = jnp.sum(jnp.where(_iota((self.hpg, self.q), 1) == last, csr_ref[...], 0.0), axis=1, keepdims=True)
        decay = jnp.broadcast_to(jnp.exp(total), (self.hpg, self.n))
        decay = jnp.concatenate([decay, jnp.zeros((HEAD_COLS - self.hpg, self.n), F32)], axis=0)
        rows = lax.shift_right_logical(_iota((self.gw, HEAD_COLS), 0), self.log2p) == _iota((self.gw, HEAD_COLS), 1)
        return _dot_parts(rows.astype(jnp.bfloat16), decay)


def _dot_parts(sel, v, dims=None):
    return sum(_dot(sel, part, dims) for part in _three_way(v))


def _ssd_fwd(xbc, cols, cs_t, l, lc, di, p, reverse):
    s = _Ssd(l, lc, di, p, reverse)
    q, n, gw = s.q, s.n, s.gw
    neg_inf = float("-inf")

    def body(xs_ref, b_ref, c_ref, cols_ref, csr_ref, et_ref, eh_ref, y_ref, hp_ref, h_scr):
        i = pl.program_id(1)

        @pl.when(i == 0)
        def _():
            h_scr[...] = jnp.zeros_like(h_scr)

        h = h_scr[...]
        hp_ref[0, 0] = h
        mask, _ = s.masks()
        cols = cols_ref[0]
        bb, cb = b_ref[...].astype(MXU_DTYPE), c_ref[...].astype(MXU_DTYPE)
        cbt = _dot(cb, bb, _NT)
        y_off = _dot(cb, h.astype(MXU_DTYPE), _NT)
        spread_all, cs_heads = _dot(cols, et_ref[...]), _dot(cols, eh_ref[...])
        w_tiles = []
        for tt in range(s.ntile):
            sl = slice(tt * LANES, (tt + 1) * LANES)
            dt_b, cs_b, tc_b = s.spread(spread_all, tt)
            x = xs_ref[:, sl] * dt_b
            ms, xhs = [], []
            for qq in range(s.hpt):
                j = tt * s.hpt + qq
                seg = cs_heads[:, j * LANES:(j + 1) * LANES] - csr_ref[j:j + 1, :]
                ms.append((cbt * jnp.exp(jnp.where(mask, seg, neg_inf))).astype(MXU_DTYPE))
                xhs.append(jnp.where(s.head_lanes(qq), x, 0.0).astype(MXU_DTYPE))
            yd = _dot(jnp.concatenate(ms, axis=1), jnp.concatenate(xhs, axis=0))
            y_ref[:, sl] = yd + y_off[:, sl] * jnp.exp(cs_b)
            w_tiles.append((x * jnp.exp(tc_b)).astype(MXU_DTYPE))
        wm = w_tiles[0] if s.ntile == 1 else jnp.concatenate(w_tiles, axis=1)
        h_scr[...] = h * s.state_scale(csr_ref) + _dot(wm, bb, _TN)

    d = "rev" if reverse else "fwd"
    e_tiles, e_heads = s.selectors()
    return pl.pallas_call(
        body, name=f"ssd_{d}", grid=(s.g, s.ns), in_specs=s.in_specs(s.chunk_at),
        out_specs=[pl.BlockSpec((q, gw), lambda gi, i: (s.chunk_at(i), gi)),
                   pl.BlockSpec((1, 1, gw, n), lambda gi, i: (i, gi, 0, 0))],
        out_shape=[jax.ShapeDtypeStruct((s.t_all, di), F32), jax.ShapeDtypeStruct((s.ns, s.g, gw, n), F32)],
        scratch_shapes=[pltpu.VMEM((gw, n), F32)],
        compiler_params=_cp(("parallel", "arbitrary")),
    )(xbc, xbc, xbc, cols, cs_t, e_tiles, e_heads)


def _ssd_bwd(xbc, cols, cs_t, a_cols, dy, hprev, l, lc, di, p, reverse, dsk=None, prev=None):
    s = _Ssd(l, lc, di, p, reverse)
    q, n, gw, hpg = s.q, s.n, s.gw, s.hpg
    neg_inf = float("-inf")
    n_extra = (dsk is not None) + (3 if prev is not None else 0)

    def chunk_of(i):
        return s.chunk_at(s.ns - 1 - i)

    def body(xs_ref, b_ref, c_ref, cols_ref, csr_ref, et_ref, eh_ref, ac_ref, dy_ref, hp_ref, *rest):
        extra, (dxs_ref, db_ref, dc_ref, ddt_ref, dda_ref, dh_scr) = rest[:n_extra], rest[n_extra:]
        dsk_ref = extra[0] if dsk is not None else None
        prev_refs = extra[-3:] if prev is not None else None
        i = pl.program_id(1)

        @pl.when(i == 0)
        def _():
            dh_scr[...] = jnp.zeros_like(dh_scr)

        latent = (chunk_of(i) < s.nl).astype(F32)
        h, dh = hp_ref[0, 0], dh_scr[...]
        hb, dhb = h.astype(MXU_DTYPE), dh.astype(MXU_DTYPE)
        mask, mask_t = s.masks()
        cols = cols_ref[0]
        bb, cb = b_ref[...].astype(MXU_DTYPE), c_ref[...].astype(MXU_DTYPE)
        cbt, bct = _dot(cb, bb, _NT), _dot(bb, cb, _NT)
        b_dh = _dot(bb, dhb, _NT)
        y_off0 = _dot(cb, hb, _NT)
        d_g, d_gt = jnp.zeros((q, q), F32), jnp.zeros((q, q), F32)
        dcs = jnp.zeros((q, HEAD_COLS), F32)
        ddt_x = jnp.zeros((q, HEAD_COLS), F32)
        r_state = jnp.zeros((16, HEAD_COLS), F32)
        spread_all, cs_heads = _dot(cols, et_ref[...]), _dot(cols, eh_ref[...])
        dye_tiles, xte_tiles = [], []
        for tt in range(s.ntile):
            sl = slice(tt * LANES, (tt + 1) * LANES)
            dt_b, cs_b, tc_b = s.spread(spread_all, tt)
            ecs_b, te_b = jnp.exp(cs_b), jnp.exp(tc_b)
            xs_t = xs_ref[:, sl]
            x = xs_t * dt_b
            d_y = dy_ref[:, sl] * latent
            dx_state = b_dh[:, sl] * te_b
            lms, lm_ts, m_ts, d_yhs, xhs = [], [], [], [], []
            for qq in range(s.hpt):
                j = tt * s.hpt + qq
                csc_b = cs_heads[:, j * LANES:(j + 1) * LANES]
                csr = csr_ref[j:j + 1, :]
                lms.append(jnp.exp(jnp.where(mask, csc_b - csr, neg_inf)))
                lm_ts.append(jnp.exp(jnp.where(mask_t, csr - csc_b, neg_inf)))
                m_ts.append(bct * lm_ts[-1])
                lanes = s.head_lanes(qq)
                d_yhs.append(jnp.where(lanes, d_y, 0.0).astype(MXU_DTYPE))
                xhs.append(jnp.where(lanes, x, 0.0).astype(MXU_DTYPE))
            d_yh_rows = jnp.concatenate(d_yhs, axis=0)
            d_m_all = _dot(d_yh_rows, x.astype(MXU_DTYPE), _NT)
            d_mt_all = _dot(jnp.concatenate(xhs, axis=0), d_y.astype(MXU_DTYPE), _NT)
            for qq in range(s.hpt):
                j = tt * s.hpt + qq
                d_m, d_mt = d_m_all[qq * q:(qq + 1) * q], d_mt_all[qq * q:(qq + 1) * q]
                r1 = jnp.sum(d_m * (cbt * lms[qq]), axis=1, keepdims=True)
                r2 = jnp.sum(d_mt * m_ts[qq], axis=1, keepdims=True)
                dcs = dcs + (r1 - r2) * (_iota((1, HEAD_COLS), 1) == j).astype(F32)
                d_g = d_g + d_m * lms[qq]
                d_gt = d_gt + d_mt * lm_ts[qq]
            d_x = _dot(jnp.concatenate([m.astype(MXU_DTYPE) for m in m_ts], axis=1), d_yh_rows) + dx_state
            d_xs = d_x * dt_b
            if dsk_ref is not None:
                d_xs = d_xs + d_y * dsk_ref[:, sl]
            if prev_refs is not None:
                d_xs = d_xs + prev_refs[0][:, sl]
            dxs_ref[:, sl] = d_xs
            fed = x * dx_state
            fed_rows = jnp.broadcast_to(jnp.sum(fed, axis=0, keepdims=True), (16, LANES))
            sums = s.head_sums([d_x * xs_t, d_y * y_off0[:, sl] * ecs_b - fed, fed_rows], tt)
            ddt_x, dcs, r_state = ddt_x + sums[0], dcs + sums[1], r_state + sums[2]
            dye_tiles.append((d_y * ecs_b).astype(MXU_DTYPE))
            xte_tiles.append((x * te_b).astype(MXU_DTYPE))
        dye = dye_tiles[0] if s.ntile == 1 else jnp.concatenate(dye_tiles, axis=1)
        xte = xte_tiles[0] if s.ntile == 1 else jnp.concatenate(xte_tiles, axis=1)
        d_c = _dot(d_g.astype(MXU_DTYPE), bb) + _dot(dye, hb)
        d_b = _dot(d_gt.astype(MXU_DTYPE), cb) + _dot(xte, dhb)
        if prev_refs is not None:
            d_b, d_c = d_b + prev_refs[1][...], d_c + prev_refs[2][...]
        dc_ref[...] = d_c
        db_ref[...] = d_b
        scale = s.state_scale(csr_ref)
        carried = dh * h * scale
        d_tot = jnp.sum(r_state, axis=0, keepdims=True) * 0.0625
        for j in range(hpg):
            part = jnp.sum(carried[j * p:(j + 1) * p, :], axis=0, keepdims=True)
            d_tot = d_tot + jnp.sum(part, axis=1, keepdims=True) * (_iota((1, HEAD_COLS), 1) == j).astype(F32)
        dda = _dot_parts(mask_t.astype(jnp.bfloat16), dcs) + d_tot
        ddt_ref[0] = ddt_x + dda * ac_ref[0]
        dda_ref[0] = dda
        dh_scr[...] = dh * scale + _dot(dye, cb, _TN)

    d = "rev" if reverse else "fwd"
    e_tiles, e_heads = s.selectors()
    col = pl.BlockSpec((1, q, HEAD_COLS), lambda gi, i: (gi, chunk_of(i), 0))
    gn = pl.BlockSpec((q, n), lambda gi, i: (chunk_of(i), gi))
    wide = pl.BlockSpec((q, gw), lambda gi, i: (chunk_of(i), gi))
    extra_specs, extra_args, aliases = [], [], {}
    if dsk is not None:
        extra_specs.append(pl.BlockSpec((1, gw), lambda gi, i: (0, gi)))
        extra_args.append(dsk)
    if prev is not None:
        first = 10 + len(extra_args)
        extra_specs += [wide, gn, gn]
        extra_args += list(prev)
        aliases = {first: 0, first + 1: 1, first + 2: 2}
    return pl.pallas_call(
        body, name=f"ssd_bwd_{d}", grid=(s.g, s.ns),
        in_specs=s.in_specs(chunk_of) + [
            pl.BlockSpec((1, 1, HEAD_COLS), lambda gi, i: (s.d * s.g + gi, 0, 0)),
            pl.BlockSpec((q, gw), lambda gi, i: (jnp.minimum(chunk_of(i), s.nl - 1), gi)),
            pl.BlockSpec((1, 1, gw, n), lambda gi, i: (s.ns - 1 - i, gi, 0, 0))] + extra_specs,
        out_specs=[wide, gn, gn, col, col],
        out_shape=[jax.ShapeDtypeStruct((s.t_all, di), F32), jax.ShapeDtypeStruct((s.t_all, s.g * n), F32),
                   jax.ShapeDtypeStruct((s.t_all, s.g * n), F32), jax.ShapeDtypeStruct((s.g, s.t_all, HEAD_COLS), F32),
                   jax.ShapeDtypeStruct((s.g, s.t_all, HEAD_COLS), F32)],
        scratch_shapes=[pltpu.VMEM((gw, n), F32)],
        input_output_aliases=aliases, compiler_params=_cp(("parallel", "arbitrary")),
    )(xbc, xbc, xbc, cols, cs_t, e_tiles, e_heads, a_cols, dy, hprev, *extra_args)


def _ada_fwd(crows, w, b):
    r, d = crows.shape
    ws = w.shape[1]
    tn = _tile(ws, 512, LANES)

    def body(c_ref, w_ref, b_ref, m_ref, s_ref):
        s = _silu(c_ref[...])
        s_ref[...] = s
        m_ref[...] = _dot(s.astype(MXU_DTYPE), w_ref[...].astype(MXU_DTYPE)) + b_ref[...]

    full = pl.BlockSpec((r, d), lambda j: (0, 0))
    return pl.pallas_call(
        body, name="ada_fwd", grid=(ws // tn,),
        in_specs=[full, pl.BlockSpec((d, tn), lambda j: (0, j)), pl.BlockSpec((1, tn), lambda j: (0, j))],
        out_specs=[pl.BlockSpec((r, tn), lambda j: (0, j)), full],
        out_shape=[jax.ShapeDtypeStruct((r, ws), F32), jax.ShapeDtypeStruct((r, d), F32)],
        compiler_params=_cp(("arbitrary",)),
    )(crows, w, b)


def _ada_bwd(s_t, w, dm):
    d, r = s_t.shape
    ws = w.shape[1]
    tn = _tile(ws, 512, LANES)

    def body(st_ref, w_ref, dm_ref, dw_ref, ds_ref):
        dmb = dm_ref[...].astype(MXU_DTYPE)
        dw_ref[...] = _dot(st_ref[...].astype(MXU_DTYPE), dmb)
        _acc(ds_ref, _dot(dmb, w_ref[...].astype(MXU_DTYPE), _NT), pl.program_id(0))

    return pl.pallas_call(
        body, name="ada_bwd", grid=(ws // tn,),
        in_specs=[pl.BlockSpec((d, r), lambda j: (0, 0)), pl.BlockSpec((d, tn), lambda j: (0, j)),
                  pl.BlockSpec((r, tn), lambda j: (0, j))],
        out_specs=[pl.BlockSpec((d, tn), lambda j: (0, j)), pl.BlockSpec((r, d), lambda j: (0, 0))],
        out_shape=[jax.ShapeDtypeStruct((d, ws), F32), jax.ShapeDtypeStruct((r, d), F32)],
        compiler_params=_cp(("arbitrary",)),
    )(s_t, w, dm)


def _adamw(w, g, m, v, name):
    r, c = w.shape
    t = _tile(r, max(8, 300_000 // c), 8)

    def body(w_ref, g_ref, m_ref, v_ref, d_ref, m2_ref, v2_ref):
        g = g_ref[...]
        m2 = ADAM_B1 * m_ref[...] + (1.0 - ADAM_B1) * g
        v2 = ADAM_B2 * v_ref[...] + (1.0 - ADAM_B2) * (g * g)
        m_hat = m2 / (1.0 - ADAM_B1 ** ADAM_STEP)
        v_hat = v2 / (1.0 - ADAM_B2 ** ADAM_STEP)
        d_ref[...] = -ADAM_LR * (m_hat / (jnp.sqrt(v_hat) + ADAM_EPS) + ADAM_WD * w_ref[...])
        m2_ref[...] = m2
        v2_ref[...] = v2

    blk = pl.BlockSpec((t, c), lambda i: (i, 0))
    shape = jax.ShapeDtypeStruct((r, c), F32)
    return pl.pallas_call(
        body, name=name, grid=(r // t,), in_specs=[blk] * 4, out_specs=[blk] * 3, out_shape=[shape] * 3,
        compiler_params=_cp(("parallel",)),
    )(w, g, m, v)


def _sum_devices(gathered):
    rows, w = gathered.shape
    per = rows // N_DEV

    def body(g_ref, o_ref):
        total = g_ref[pl.ds(0, per), :]
        for dev in range(1, N_DEV):
            total = total + g_ref[pl.ds(dev * per, per), :]
        o_ref[...] = total

    return pl.pallas_call(
        body, name="sum_devices", out_shape=jax.ShapeDtypeStruct((per, w), F32),
        in_specs=[pl.BlockSpec(memory_space=pltpu.VMEM)], out_specs=pl.BlockSpec(memory_space=pltpu.VMEM),
        compiler_params=_cp(),
    )(gathered)


def _c_ctx_grad(parts, c_ctx):
    rows, d = parts.shape
    per = rows // N_DEV

    def body(p_ref, c_ref, o_ref):
        total = p_ref[pl.ds(0, 1), :]
        for chip in range(1, N_SHARD):
            total = total + p_ref[pl.ds(2 * chip * per, 1), :]
        _, vjp = jax.vjp(_silu, c_ref[...])
        o_ref[...] = vjp(total)[0]

    return pl.pallas_call(
        body, name="c_ctx_grad", out_shape=jax.ShapeDtypeStruct((1, d), F32),
        in_specs=[pl.BlockSpec(memory_space=pltpu.VMEM)] * 2, out_specs=pl.BlockSpec(memory_space=pltpu.VMEM),
        compiler_params=_cp(),
    )(parts, c_ctx)


def _pad_rows(a, rows, width):
    return jnp.pad(a, ((0, rows - a.shape[0]), (0, width - a.shape[1])))


def _pack(vectors, quantum):
    flat = jnp.concatenate([v.reshape(-1) for v in vectors])
    return jnp.pad(flat, (0, -flat.shape[0] % quantum))


def kernel(x, c, ctx, c_ctx, w_mod, b_mod, norm_mix, w_in, ssm_conv_w, ssm_conv_b, dt_bias, a_log, d_skip, ssm_norm, cf_conv_w, cf_conv_b, cf_ln_g, cf_ln_b, w_proj_a, w_proj_b, w_out, norm_ffn, w_ffn_gate, w_ffn_up, w_ffn_down, norm_final, loss_target, m_c_ctx, m_w_mod, m_b_mod, m_norm_mix, m_w_in, m_ssm_conv_w, m_ssm_conv_b, m_dt_bias, m_a_log, m_d_skip, m_ssm_norm, m_cf_conv_w, m_cf_conv_b, m_cf_ln_g, m_cf_ln_b, m_w_proj_a, m_w_proj_b, m_w_out, m_norm_ffn, m_w_ffn_gate, m_w_ffn_up, m_w_ffn_down, m_norm_final, v_c_ctx, v_w_mod, v_b_mod, v_norm_mix, v_w_in, v_ssm_conv_w, v_ssm_conv_b, v_dt_bias, v_a_log, v_d_skip, v_ssm_norm, v_cf_conv_w, v_cf_conv_b, v_cf_ln_g, v_cf_ln_b, v_w_proj_a, v_w_proj_b, v_w_out, v_norm_ffn, v_w_ffn_gate, v_w_ffn_up, v_w_ffn_down, v_norm_final):
    l, d = x.shape[1], x.shape[2]
    lc = ctx.shape[1]
    t_all = l + lc
    di = ssm_norm.shape[-1]
    h = d_skip.shape[-1]
    p = di // h
    g, n = SSM_GROUPS, SSM_STATE
    hpg = h // g
    conv_dim = di + 2 * g * n
    df = w_ffn_down.shape[1] * N_SHARD
    assert 2 * h == LANES and d % (2 * LANES) == 0

    my_x, my_y, my_c = _mesh_pos()
    chip = 2 * my_x + my_y
    dev = 2 * chip + my_c

    x2, ctx2, tgt = x[0], ctx[0], loss_target[0]
    row = lambda a: a.reshape(1, -1)

    cw_shard, cfw_shard = ssm_conv_w[0], cf_conv_w[0]
    k5, k31 = cw_shard.shape[0], cfw_shard.shape[0]
    r5, r31 = -(-k5 // 8) * 8, -(-k31 // 8) * 8
    wp = max(d, cw_shard.shape[1], cfw_shard.shape[1])
    packed = jnp.concatenate([_pad_rows(c, 8, wp), _pad_rows(cw_shard, r5, wp), _pad_rows(cfw_shard, r31, wp)], axis=0)
    got = _allgather_small(packed, "ag_params").reshape(N_DEV, 8 + r5 + r31, wp)
    c_all = got[:, 0, :d]
    conv_w = got[0::2, 8:8 + k5, :cw_shard.shape[1]].transpose(1, 0, 2).reshape(k5, conv_dim)
    cf_w = got[0::2, 8 + r5:8 + r5 + k31, :cfw_shard.shape[1]].transpose(1, 0, 2).reshape(k31, d)

    ws = w_mod.shape[2]
    crows = jnp.concatenate([c_all, row(c_ctx), jnp.zeros((7, d), F32)], axis=0)
    b_mod_mine = lax.dynamic_slice(b_mod, (0, chip * ws), (1, ws))
    m_part, s_rows = _ada_fwd(crows, w_mod[0], b_mod_mine)
    m_full = _allgather_small(m_part, "ag_mod").reshape(N_DEV, 16, ws)[0::2].transpose(1, 0, 2).reshape(16, N_SHARD * ws)
    m_lat = lax.dynamic_slice(m_full, (dev, 0), (1, 6 * d))
    sh1, sc1, g1, sh2, sc2, g2 = [m_lat[:, i * d:(i + 1) * d] for i in range(6)]
    csh1, csc1 = m_full[8:9, 0:d], m_full[8:9, d:2 * d]

    shards = [w_in[0].T, w_ffn_gate[0].T, w_ffn_up[0].T, w_proj_a[0], w_proj_b[0], w_out[0], w_ffn_down[0]]
    win_t, wg_t, wu_t, wpa, wpb, wout, wdn = _allgather_rows([s.astype(WIRE_DTYPE) for s in shards], "ag_weights")
    o_xbc, o_dt, o_glu, o_gates = di, di + conv_dim, di + conv_dim + 2 * h, di + conv_dim + 2 * h + 2 * d
    win_work = jnp.concatenate([win_t[:o_xbc], win_t[o_glu:], win_t[o_xbc:o_dt], win_t[o_dt:o_glu]], axis=0)
    c_u, c_ga, c_xbc, c_dt = di, di + 2 * d, di + 4 * d, di + 4 * d + conv_dim
    wgu = jnp.concatenate([wg_t, wu_t], axis=0)

    nm = norm_mix
    hx = _mod_fwd(x2, ctx2, nm, sc1, sh1, csc1, csh1)
    proj = _matmul(hx, win_work, tb=True, tm=768, tn=640, name="mm_proj")
    xbc = _conv5_fwd(proj, conv_w, ssm_conv_b, l, lc, c_xbc, conv_dim)
    a = -jnp.exp(a_log.reshape(1, 2 * h))
    dt, cs, tc, cs_t = _dt_fwd(proj, dt_bias.reshape(1, 2 * h), a, c_dt // LANES)
    cols = _scan_columns(dt, cs, tc, 2 * g, hpg)
    a_cols = jnp.pad(a.reshape(2 * g, 1, hpg), ((0, 0), (0, 0), (0, HEAD_COLS - hpg)))
    y_f, hp_f = _ssd_fwd(xbc, cols, cs_t, l, lc, di, p, False)
    y_r, hp_r = _ssd_fwd(xbc, cols, cs_t, l, lc, di, p, True)
    dsk = jnp.repeat(d_skip.reshape(h), p).reshape(1, di)
    ya_in = _gate_fwd(y_f, y_r, xbc, proj, dsk, ssm_norm, l, di)
    y_a = _matmul(ya_in, wpa, tk=di, name="mm_ya")
    u_blk = c_u // d
    cv = _conv31_fwd(proj, cf_w, cf_conv_b, l, d, u_blk)
    cf = _ln_fwd(cv, cf_ln_g, cf_ln_b)
    y_b = _matmul(cf, wpb, name="mm_yb")
    ga_blk = c_ga // d
    merged = _merge_fwd(y_a, y_b, proj, ga_blk)
    mix = _matmul(merged, wout, name="mm_mix")
    x1, hx2 = _res_fwd(x2, mix, g1, norm_ffn, sc2, sh2)
    gu = _matmul(hx2, wgu, tb=True, tn=_tile(df, 1024, LANES), name="mm_gu")
    act = _swiglu_fwd(gu, df)
    dn = _matmul(act, wdn, tk=df, name="mm_dn")
    loss, dx1, ddn, dg2, d_norm_final = _loss_and_grads(x1, dn, g2, row(norm_final), tgt)

    dact = _matmul(ddn, wdn, tb=True, tn=_tile(df, 1024, LANES), name="mm_dact")
    dw_dn = _matmul(act, ddn, ta=True, tn=d, tk=1024, name="mm_dw_dn")
    dgu = _swiglu_bwd(gu, dact, df)
    dhx2 = _matmul(dgu, wgu, tk=df, name="mm_dhx2")
    dw_gu = _matmul(dgu, hx2, ta=True, tn=d, tk=1024, name="mm_dw_gu")
    dx_res, dmix, dg1, d_norm_ffn, dsc2, dsh2 = _res_bwd(x2, mix, g1, norm_ffn, sc2, sh2, dx1, dhx2)
    dmerged = _matmul(dmix, wout, tb=True, name="mm_dmerged")
    dw_out = _matmul(merged, dmix, ta=True, tn=d, tk=1024, name="mm_dw_out")
    dya, dyb, dga, dgb = _merge_bwd(y_a, y_b, proj, ga_blk, dmerged, lc)
    dcf = _matmul(dyb, wpb, tb=True, name="mm_dcf")
    dw_pb = _matmul(cf, dyb, ta=True, tn=d, tk=1024, name="mm_dw_pb")
    dcv, d_ln_g, d_ln_b = _ln_bwd(cv, cf_ln_g, cf_ln_b, dcf)
    du, dv, d_cf_w, d_cf_b = _conv31_bwd(proj, cf_w, dcv, l, lc, d, u_blk)
    dya_in = _matmul(dya, wpa, tb=True, tn=_tile(di, 1024, LANES), name="mm_dya_in")
    dw_pa = _matmul(ya_in, dya, ta=True, tn=d, tk=1024, name="mm_dw_pa")
    dy, dz, ddsk, d_ssm_norm = _gate_bwd(y_f, y_r, xbc, proj, dsk, ssm_norm, dya_in, l, lc, di)
    dxs_f, db_f, dc_f, ddt_f, dda_f = _ssd_bwd(xbc, cols, cs_t, a_cols, dy, hp_f, l, lc, di, p, False, dsk=dsk)
    dxs, db, dc, ddt_r, dda_r = _ssd_bwd(xbc, cols, cs_t, a_cols, dy, hp_r, l, lc, di, p, True, prev=(dxs_f, db_f, dc_f))
    dxs_raw, dcw_x, dcb_x = _conv5_bwd(proj, conv_w, ssm_conv_b, [dxs], l, lc, c_xbc, 0, di)
    db_raw, dcw_b, dcb_b = _conv5_bwd(proj, conv_w, ssm_conv_b, [db], l, lc, c_xbc, di, g * n)
    dc_raw, dcw_c, dcb_c = _conv5_bwd(proj, conv_w, ssm_conv_b, [dc], l, lc, c_xbc, di + g * n, g * n)
    d_conv_w = jnp.concatenate([dcw_x, dcw_b, dcw_c], axis=1)
    d_conv_b = jnp.concatenate([dcb_x, dcb_b, dcb_c], axis=1)
    heads = lambda f, r: jnp.concatenate([t[:, :, :hpg].transpose(1, 0, 2).reshape(t_all, h) for t in (f, r)], axis=1)
    ddt_raw, d_dt_bias, dda_dt = _dt_bwd(proj, dt_bias.reshape(1, 2 * h), dt, heads(ddt_f, ddt_r), heads(dda_f, dda_r), c_dt // LANES)
    d_a_log = dda_dt * a
    dproj = jnp.concatenate([dz, du, dv, dga, dgb, dxs_raw, db_raw, dc_raw, ddt_raw], axis=1)
    dhx = _matmul(dproj, win_work, tm=768, tn=1024, tk=_tile(win_work.shape[0], 4096, LANES), name="mm_dhx")
    dw_in_work = _matmul(dproj, hx, ta=True, tm=640, tn=d, tk=768, name="mm_dw_in")
    grad_x, d_norm_mix, dsc1, dsh1, dcsc1, dcsh1 = _mod_bwd(x2, ctx2, nm, sc1, sh1, csc1, csh1, dhx, dx_res)

    dw_in_t = jnp.concatenate([dw_in_work[:c_u], dw_in_work[c_xbc:], dw_in_work[c_u:c_xbc]], axis=0)
    reduced = _reduce_scatter([dw_in_t, dw_gu[:df], dw_gu[df:], dw_pa, dw_pb, dw_out, dw_dn])
    g_in, g_gate, g_up = reduced[0].T, reduced[1].T, reduced[2].T
    g_pa, g_pb, g_out, g_dn = reduced[3:]

    zeros_d = jnp.zeros((1, d), F32)
    dm_lat = jnp.concatenate([dsh1, dsc1, dg1, dsh2, dsc2, dg2], axis=1)
    dm_ctx = jnp.concatenate([dcsh1, dcsc1] + [zeros_d] * 4, axis=1)
    d_d_skip = ddsk.reshape(h, p).sum(axis=1)
    replicated = [dm_lat + dm_ctx, d_norm_mix, d_conv_b, d_dt_bias, d_a_log, d_d_skip, d_ssm_norm, d_cf_b, d_ln_g, d_ln_b,
                  d_norm_ffn, d_norm_final]
    rep_w = [b_mod, norm_mix, ssm_conv_b, dt_bias, a_log, d_skip, ssm_norm, cf_conv_b, cf_ln_g, cf_ln_b, norm_ffn, norm_final]
    rep_m = [m_b_mod, m_norm_mix, m_ssm_conv_b, m_dt_bias, m_a_log, m_d_skip, m_ssm_norm, m_cf_conv_b, m_cf_ln_g, m_cf_ln_b,
             m_norm_ffn, m_norm_final]
    rep_v = [v_b_mod, v_norm_mix, v_ssm_conv_b, v_dt_bias, v_a_log, v_d_skip, v_ssm_norm, v_cf_conv_b, v_cf_ln_g, v_cf_ln_b,
             v_norm_ffn, v_norm_final]
    quantum = 8 * LANES
    rep_flat = _pack(replicated, quantum)
    n_rep = rep_flat.shape[0]
    summed_parts = [rep_flat, _pack([d_conv_w, d_cf_w, dm_ctx], quantum)]
    n_sum = n_rep + summed_parts[1].shape[0]
    everything = jnp.concatenate(summed_parts + [_pack([dm_lat], quantum)])
    gathered = _allgather_small(everything.reshape(8, -1), "ag_small_grads")
    w8 = gathered.shape[1]
    summed = _sum_devices(gathered).reshape(-1)
    dm_lat_all = gathered.reshape(N_DEV, 8 * w8)[:, n_sum:n_sum + 6 * d]
    off = n_rep
    g_conv_w_full = summed[off:off + k5 * conv_dim].reshape(k5, conv_dim)
    off += k5 * conv_dim
    g_cf_w_full = summed[off:off + k31 * d].reshape(k31, d)
    off += k31 * d
    dm_ctx_all = summed[off:off + 6 * d].reshape(1, 6 * d)
    g_conv_w = lax.dynamic_slice(g_conv_w_full, (0, chip * cw_shard.shape[1]), cw_shard.shape)
    g_cf_w = lax.dynamic_slice(g_cf_w_full, (0, chip * cfw_shard.shape[1]), cfw_shard.shape)

    dm_rows = jnp.concatenate([dm_lat_all, dm_ctx_all, jnp.zeros((7, 6 * d), F32)], axis=0)
    dm_mine = lax.dynamic_slice(dm_rows, (0, chip * ws), (16, ws))
    g_w_mod, ds_part = _ada_bwd(s_rows.T, w_mod[0], dm_mine)
    ds_all = _allgather_small(ds_part[8:16], "ag_c_ctx")
    g_c_ctx = _c_ctx_grad(ds_all, row(c_ctx))

    grads, deltas, new_ms, new_vs = {}, {}, {}, {}

    def update(name, w2, g2, m2, v2, shape):
        dl, mm, vv = _adamw(w2, g2, m2, v2, f"adamw_{name}")
        grads[name], deltas[name], new_ms[name], new_vs[name] = (t.reshape(shape) for t in (g2, dl, mm, vv))

    for name, w_, g_, m_, v_ in [
            ("w_mod", w_mod, g_w_mod, m_w_mod, v_w_mod), ("w_in", w_in, g_in, m_w_in, v_w_in),
            ("ssm_conv_w", ssm_conv_w, g_conv_w, m_ssm_conv_w, v_ssm_conv_w),
            ("cf_conv_w", cf_conv_w, g_cf_w, m_cf_conv_w, v_cf_conv_w),
            ("w_proj_a", w_proj_a, g_pa, m_w_proj_a, v_w_proj_a), ("w_proj_b", w_proj_b, g_pb, m_w_proj_b, v_w_proj_b),
            ("w_out", w_out, g_out, m_w_out, v_w_out), ("w_ffn_gate", w_ffn_gate, g_gate, m_w_ffn_gate, v_w_ffn_gate),
            ("w_ffn_up", w_ffn_up, g_up, m_w_ffn_up, v_w_ffn_up), ("w_ffn_down", w_ffn_down, g_dn, m_w_ffn_down, v_w_ffn_down)]:
        update(name, w_[0], g_, m_[0], v_[0], w_.shape)
    update("c_ctx", row(c_ctx), g_c_ctx, row(m_c_ctx), row(v_c_ctx), c_ctx.shape)

    rep_names = ["b_mod", "norm_mix", "ssm_conv_b", "dt_bias", "a_log", "d_skip", "ssm_norm", "cf_conv_b", "cf_ln_g", "cf_ln_b",
                 "norm_ffn", "norm_final"]
    as8 = lambda vs: _pack(vs, quantum).reshape(8, -1)
    g8 = summed[:n_rep].reshape(8, -1)
    d8, m8, v8 = _adamw(as8(rep_w), g8, as8(rep_m), as8(rep_v), "adamw_replicated")
    off = 0
    for name, w_ in zip(rep_names, rep_w):
        size = w_.size
        for store, packed8 in ((grads, g8), (deltas, d8), (new_ms, m8), (new_vs, v8)):
            store[name] = packed8.reshape(-1)[off:off + size].reshape(w_.shape)
        off += size

    order = ["c_ctx", "w_mod", "b_mod", "norm_mix", "w_in", "ssm_conv_w", "ssm_conv_b", "dt_bias", "a_log", "d_skip", "ssm_norm",
             "cf_conv_w", "cf_conv_b", "cf_ln_g", "cf_ln_b", "w_proj_a", "w_proj_b", "w_out", "norm_ffn", "w_ffn_gate", "w_ffn_up",
             "w_ffn_down", "norm_final"]
    total_loss = lax.psum(loss[0, 0], ("x", "y", "c"))
    return (total_loss, grad_x.reshape(x.shape), *[grads[k] for k in order], *[deltas[k] for k in order],
            *[new_ms[k] for k in order], *[new_vs[k] for k in order])
```

```python
import functools

import jax
import jax.numpy as jnp
from jax import lax
from jax.experimental import pallas as pl
from jax.experimental.pallas import tpu as pltpu

F32 = jnp.float32
MXU_DTYPE = jnp.bfloat16
WIRE_DTYPE = jnp.bfloat16
HI = lax.Precision.HIGHEST
EPS = 1e-6
SSM_GROUPS = 8
SSM_STATE = 128
CHUNK = 128
GRID_W = 64
LANES = 128
HEAD_COLS = 16
VMEM_LIMIT = 52 * 1024 * 1024
ADAM_LR, ADAM_B1, ADAM_B2, ADAM_EPS, ADAM_WD, ADAM_STEP = 0.001, 0.9, 0.999, 1e-08, 0.01, 10
MESH = pl.DeviceIdType.MESH
N_SHARD = 4
N_DEV = 8


def _cp(sem=None):
    kw = dict(vmem_limit_bytes=VMEM_LIMIT)
    if sem is not None:
        kw["dimension_semantics"] = sem
    return pltpu.CompilerParams(**kw)


def _tile(n, target, q):
    best = None
    for t in range(q, min(n, target) + 1, q):
        if n % t == 0:
            best = t
    return best if best is not None else n


def _acc(ref, val, i):
    @pl.when(i == 0)
    def _():
        ref[...] = val

    @pl.when(i > 0)
    def _():
        ref[...] += val


def _bc_spec(w):
    return pl.BlockSpec((1, w), lambda *_: (0, 0))


def _rms(x, w):
    return x * lax.rsqrt(jnp.mean(x * x, axis=-1, keepdims=True) + EPS) * w


def _silu(x):
    return x * jax.nn.sigmoid(x)


def _f_mod(x, w, sc, sh):
    return _rms(x, w) * (1.0 + sc) + sh


def _f_gate(yf, yr, xs, z, dsk, wn):
    return _rms((yf + yr + dsk * xs) * _silu(z), wn)


def _f_ln(cv, g, b):
    mu = jnp.mean(cv, axis=-1, keepdims=True)
    xc = cv - mu
    var = jnp.mean(xc * xc, axis=-1, keepdims=True)
    return _silu(xc * lax.rsqrt(var + EPS) * g + b)


def _f_merge(ya, yb, ga, gb):
    return jax.nn.sigmoid(ga) * ya + jax.nn.sigmoid(gb) * yb


def _f_res(x, mix, g1, wn, sc2, sh2):
    x1 = x + g1 * mix
    return x1, _rms(x1, wn) * (1.0 + sc2) + sh2


def _f_swiglu(gt, up):
    return _silu(gt) * up


def _f_loss(x1, dn, g2, wn, tgt):
    out = _rms(x1 + g2 * dn, wn)
    err = out - tgt
    per_tok = jnp.mean(err * err, axis=-1, keepdims=True)
    return 0.5 * jnp.sum(per_tok, axis=0, keepdims=True)


def _matmul(a, b, *, ta=False, tb=False, out_dtype=F32, tm=512, tn=512, tk=2048, name, side=None):
    m, k = (a.shape[1], a.shape[0]) if ta else a.shape
    n = b.shape[0] if tb else b.shape[1]
    assert (b.shape[1] if tb else b.shape[0]) == k, (a.shape, b.shape, ta, tb)
    tm, tn, tk = _tile(m, tm, LANES if ta else 16), _tile(n, tn, LANES), _tile(k, tk, LANES)
    grid = (m // tm, n // tn, k // tk)
    nk = grid[2]
    dims = (((0 if ta else 1,), (1 if tb else 0,)), ((), ()))
    n_in = len(side.inputs) if side else 0
    n_out = len(side.out_shapes) if side else 0

    def body(a_ref, b_ref, *rest):
        side_in, o_ref, side_out, scratch = rest[:n_in], rest[n_in], rest[n_in + 1:n_in + 1 + n_out], rest[n_in + 1 + n_out:]
        steps = [pl.program_id(axis) for axis in range(3)]
        if side:
            @pl.when((steps[0] == 0) & (steps[1] == 0) & (steps[2] == 0))
            def _():
                side.start(side_in, side_out, *scratch[-2:])

        prod = lax.dot_general(a_ref[...].astype(MXU_DTYPE), b_ref[...].astype(MXU_DTYPE), dims,
                               preferred_element_type=F32)
        if nk == 1:
            o_ref[...] = prod.astype(o_ref.dtype)
        else:
            acc = scratch[0]
            _acc(acc, prod, steps[2])

            @pl.when(steps[2] == nk - 1)
            def _():
                o_ref[...] = acc[...].astype(o_ref.dtype)

        if side:
            @pl.when((steps[0] == grid[0] - 1) & (steps[1] == grid[1] - 1) & (steps[2] == nk - 1))
            def _():
                side.finish(side_in, side_out, *scratch[-2:])

    a_spec = pl.BlockSpec((tk, tm), lambda i, j, kk: (kk, i)) if ta else pl.BlockSpec((tm, tk), lambda i, j, kk: (i, kk))
    b_spec = pl.BlockSpec((tn, tk), lambda i, j, kk: (j, kk)) if tb else pl.BlockSpec((tk, tn), lambda i, j, kk: (kk, j))
    out = pl.pallas_call(
        body, name=name, grid=grid, in_specs=[a_spec, b_spec] + [_HBM] * n_in,
        out_specs=[pl.BlockSpec((tm, tn), lambda i, j, kk: (i, j))] + [_HBM] * n_out,
        out_shape=[jax.ShapeDtypeStruct((m, n), out_dtype)] + (side.out_shapes if side else []),
        scratch_shapes=([] if nk == 1 else [pltpu.VMEM((tm, tn), F32)]) + (side.scratch() if side else []),
        compiler_params=_cp(("arbitrary",) * 3 if side else ("parallel", "parallel", "arbitrary")),
    )(a, b, *(side.inputs if side else []))
    return out if side else out[0]


def _mesh_pos():
    return lax.axis_index("x"), lax.axis_index("y"), lax.axis_index("c")


def _other_chips(x, y):
    return [(1 - x, y), (x, 1 - y), (1 - x, 1 - y)]


def _allgather_small(v, name):
    m_per, n = v.shape

    def body(x_ref, out_ref, send_sems, recv_sems, local_sem):
        x, y, c = _mesh_pos()
        me, sibling = (x, y, c), (x, y, 1 - c)
        chips = _other_chips(x, y)

        def rows(px, py, pc):
            return out_ref.at[pl.ds((4 * px + 2 * py + pc) * m_per, m_per), :]

        def copy(k, block, to, src=None):
            return pltpu.make_async_remote_copy(
                src_ref=rows(*block) if src is None else src, dst_ref=rows(*block),
                send_sem=send_sems.at[k], recv_sem=recv_sems.at[k], device_id=to, device_id_type=MESH)

        mine = pltpu.make_async_copy(x_ref, rows(*me), local_sem)
        mine.start()
        first = [copy(0, me, sibling, src=x_ref)]
        first += [copy(1 + j, me, (*chip, c), src=x_ref) for j, chip in enumerate(chips)]
        for cp in first:
            cp.start()
        passed = [copy(4 + j, (*chip, c), sibling) for j, chip in enumerate(chips)]
        for j, chip in enumerate(chips):
            copy(1 + j, (*chip, c), me).wait_recv()
            passed[j].start()
        copy(0, sibling, me).wait_recv()
        for j, chip in enumerate(chips):
            copy(4 + j, (*chip, 1 - c), me).wait_recv()
        for cp in first + passed:
            cp.wait_send()
        mine.wait()

    return pl.pallas_call(
        body, name=name, out_shape=jax.ShapeDtypeStruct((N_DEV * m_per, n), v.dtype),
        in_specs=[pl.BlockSpec(memory_space=pltpu.VMEM)], out_specs=pl.BlockSpec(memory_space=pltpu.VMEM),
        scratch_shapes=[pltpu.SemaphoreType.DMA((7,)), pltpu.SemaphoreType.DMA((7,)), pltpu.SemaphoreType.DMA],
        compiler_params=_cp(),
    )(v)


_HBM = pl.BlockSpec(memory_space=pltpu.HBM)


class _Side:
    def __init__(self, inputs, out_shapes, n_sems, start, finish):
        self.inputs, self.out_shapes, self.n_sems, self.start, self.finish = inputs, out_shapes, n_sems, start, finish

    def scratch(self):
        return [pltpu.SemaphoreType.DMA((self.n_sems,)), pltpu.SemaphoreType.DMA((self.n_sems,))]


def _run_side(side, name):
    n_in = len(side.inputs)

    def body(*refs):
        src, dst, sems = refs[:n_in], refs[n_in:-2], refs[-2:]
        side.start(src, dst, *sems)
        side.finish(src, dst, *sems)

    return pl.pallas_call(
        body, name=name, out_shape=side.out_shapes, in_specs=[_HBM] * n_in, out_specs=[_HBM] * len(side.out_shapes),
        scratch_shapes=side.scratch(), compiler_params=_cp(),
    )(*side.inputs)


def _gather_side(shards):
    n = len(shards)

    def plan(src, dst, send_sems, recv_sems):
        x, y, c = _mesh_pos()
        chips = _other_chips(x, y)

        def half(i, px, py, pc):
            r = src[i].shape[0]
            return dst[i].at[pl.ds(pl.multiple_of((2 * px + py) * r + pc * (r // 2), 16), r // 2), :]

        def copy(i, k, block, to, own=False):
            r = src[i].shape[0]
            mine = src[i].at[pl.ds(pl.multiple_of(c * (r // 2), 16), r // 2), :]
            return pltpu.make_async_remote_copy(
                src_ref=mine if own else half(i, *block), dst_ref=half(i, *block), send_sem=send_sems.at[6 * i + k],
                recv_sem=recv_sems.at[6 * i + k], device_id=to, device_id_type=MESH)

        first = [copy(i, j, (x, y, c), (*chip, c), own=True) for i in range(n) for j, chip in enumerate(chips)]
        return (x, y, c), chips, copy, first

    def start(src, dst, send_sems, recv_sems):
        for cp in plan(src, dst, send_sems, recv_sems)[3]:
            cp.start()

    def finish(src, dst, send_sems, recv_sems):
        (x, y, c), chips, copy, first = plan(src, dst, send_sems, recv_sems)
        passed = []
        for i in range(n):
            for j, chip in enumerate(chips):
                copy(i, j, (*chip, c), (x, y, c)).wait_recv()
                passed.append(copy(i, 3 + j, (*chip, c), (x, y, 1 - c)))
                passed[-1].start()
        for i in range(n):
            for j, chip in enumerate(chips):
                copy(i, 3 + j, (*chip, 1 - c), (x, y, c)).wait_recv()
        for cp in first + passed:
            cp.wait_send()

    shapes = [jax.ShapeDtypeStruct((N_SHARD * s.shape[0], s.shape[1]), s.dtype) for s in shards]
    return _Side(list(shards), shapes, 6 * n, start, finish)


def _fill_own_rows(gathered, shards):
    chip = 2 * lax.axis_index("x") + lax.axis_index("y")
    return [lax.dynamic_update_slice(full, s, (chip * s.shape[0], 0)) for full, s in zip(gathered, shards)]


def _swap_halves(parts, name):
    n = len(parts)

    def body(*refs):
        src, dst = refs[:n], refs[n:2 * n]
        send_sems, recv_sems = refs[2 * n:]
        x, y, c = _mesh_pos()
        copies = [pltpu.make_async_remote_copy(
            src_ref=src[i].at[s, 1 - c], dst_ref=dst[i].at[s], send_sem=send_sems.at[N_SHARD * i + s],
            recv_sem=recv_sems.at[N_SHARD * i + s], device_id=(x, y, 1 - c), device_id_type=MESH)
            for i in range(n) for s in range(N_SHARD)]
        for cp in copies:
            cp.start()
        for cp in copies:
            cp.wait()

    return pl.pallas_call(
        body, name=name,
        out_shape=[jax.ShapeDtypeStruct((N_SHARD,) + p.shape[2:], p.dtype) for p in parts],
        in_specs=[_HBM] * n, out_specs=[_HBM] * n,
        scratch_shapes=[pltpu.SemaphoreType.DMA((N_SHARD * n,)), pltpu.SemaphoreType.DMA((N_SHARD * n,))],
        compiler_params=_cp(),
    )(*parts)


def _scatter_side(parts):
    n = len(parts)

    def copies(src, dst, send_sems, recv_sems):
        x, y, c = _mesh_pos()
        return [pltpu.make_async_remote_copy(
            src_ref=src[i].at[2 * chip[0] + chip[1]], dst_ref=dst[i].at[j], send_sem=send_sems.at[3 * i + j],
            recv_sem=recv_sems.at[3 * i + j], device_id=(*chip, c), device_id_type=MESH)
            for i in range(n) for j, chip in enumerate(_other_chips(x, y))]

    def start(*refs):
        for cp in copies(*refs):
            cp.start()

    def finish(*refs):
        for cp in copies(*refs):
            cp.wait()

    shapes = [jax.ShapeDtypeStruct((3,) + p.shape[1:], p.dtype) for p in parts]
    return _Side(list(parts), shapes, 3 * n, start, finish)


def _join_halves(halves, name):
    n = len(halves)

    def body(*refs):
        src, dst = refs[:n], refs[n:2 * n]
        send_sems, recv_sems = refs[2 * n:]
        x, y, c = _mesh_pos()
        remote = [pltpu.make_async_remote_copy(
            src_ref=src[i], dst_ref=dst[i].at[c], send_sem=send_sems.at[i], recv_sem=recv_sems.at[i],
            device_id=(x, y, 1 - c), device_id_type=MESH) for i in range(n)]
        for cp in remote:
            cp.start()
        for i in range(n):
            pltpu.make_async_remote_copy(
                src_ref=src[i], dst_ref=dst[i].at[1 - c], send_sem=send_sems.at[i], recv_sem=recv_sems.at[i],
                device_id=(x, y, 1 - c), device_id_type=MESH).wait_recv()
        for cp in remote:
            cp.wait_send()

    joined = pl.pallas_call(
        body, name=name,
        out_shape=[jax.ShapeDtypeStruct((2,) + h.shape, h.dtype) for h in halves],
        in_specs=[_HBM] * n, out_specs=[_HBM] * n,
        scratch_shapes=[pltpu.SemaphoreType.DMA((n,)), pltpu.SemaphoreType.DMA((n,))],
        compiler_params=_cp(),
    )(*halves)
    c = lax.axis_index("c")
    return [lax.dynamic_update_slice(j, h[None], (c, 0, 0)) for j, h in zip(joined, halves)]


def _pair_sum(g, got, name):
    _, _, hr, d = g.shape
    t = _tile(hr, 256, 16)

    def body(g0_ref, g1_ref, got_ref, wire_ref, own_ref):
        x, y, c = _mesh_pos()
        total = jnp.where(c == 0, g0_ref[0, 0], g1_ref[0, 0]) + got_ref[0]
        wire_ref[0] = total.astype(wire_ref.dtype)

        @pl.when(pl.program_id(1) == 2 * x + y)
        def _():
            own_ref[...] = total

    return pl.pallas_call(
        body, name=name, grid=(hr // t, N_SHARD),
        in_specs=[pl.BlockSpec((1, 1, t, d), lambda i, s: (s, 0, i, 0)), pl.BlockSpec((1, 1, t, d), lambda i, s: (s, 1, i, 0)),
                  pl.BlockSpec((1, t, d), lambda i, s: (s, i, 0))],
        out_specs=[pl.BlockSpec((1, t, d), lambda i, s: (s, i, 0)), pl.BlockSpec((t, d), lambda i, s: (i, 0))],
        out_shape=[jax.ShapeDtypeStruct((N_SHARD, hr, d), WIRE_DTYPE), jax.ShapeDtypeStruct((hr, d), F32)],
        compiler_params=_cp(("parallel", "arbitrary")),
    )(g, g, got)


def _sum_partials(own, recv, name):
    hr, d = own.shape
    t = _tile(hr, 256, 16)

    def body(own_ref, recv_ref, o_ref):
        total = own_ref[...]
        for j in range(3):
            total = total + recv_ref[j].astype(F32)
        o_ref[...] = total

    blk = pl.BlockSpec((t, d), lambda i: (i, 0))
    return pl.pallas_call(
        body, name=name, grid=(hr // t,), in_specs=[blk, pl.BlockSpec((3, t, d), lambda i: (0, i, 0))], out_specs=blk,
        out_shape=jax.ShapeDtypeStruct((hr, d), F32), compiler_params=_cp(("parallel",)),
    )(own, recv)


def _reduce_scatter_begin(grads, tag):
    split = [g.reshape(N_SHARD, 2, g.shape[0] // (2 * N_SHARD), g.shape[1]) for g in grads]
    got = _swap_halves(split, f"rs_swap_halves_{tag}")
    sums = [_pair_sum(g, h, f"rs_pair_sum_{tag}{i}") for i, (g, h) in enumerate(zip(split, got))]
    return [w for w, _ in sums], [own for _, own in sums]


def _reduce_scatter_end(own, recv, tag):
    halves = [_sum_partials(o, rv, f"rs_sum_{tag}{i}") for i, (o, rv) in enumerate(zip(own, recv))]
    return [j.reshape(-1, j.shape[-1]) for j in _join_halves(halves, f"rs_join_halves_{tag}")]


def _mod_fwd(x, ctx, nw, sc, sh, csc, csh):
    l, d = x.shape
    lc = ctx.shape[0]
    t = min(256, lc)
    nl, nc = l // t, lc // t

    def body(x_ref, c_ref, nw_ref, sc_ref, sh_ref, csc_ref, csh_ref, o_ref):
        i = pl.program_id(0)

        @pl.when(i < nl)
        def _():
            o_ref[...] = _f_mod(x_ref[...], nw_ref[...], sc_ref[...], sh_ref[...]).astype(o_ref.dtype)

        @pl.when(i >= nl)
        def _():
            o_ref[...] = _f_mod(c_ref[...], nw_ref[...], csc_ref[...], csh_ref[...]).astype(o_ref.dtype)

    return pl.pallas_call(
        body, name="mod_fwd", grid=(nl + nc,),
        in_specs=[pl.BlockSpec((t, d), lambda i: (jnp.minimum(i, nl - 1), 0)),
                  pl.BlockSpec((t, d), lambda i: (jnp.maximum(i - nl, 0), 0))] + [_bc_spec(d)] * 5,
        out_specs=pl.BlockSpec((t, d), lambda i: (i, 0)),
        out_shape=jax.ShapeDtypeStruct((l + lc, d), MXU_DTYPE), compiler_params=_cp(("arbitrary",)),
    )(x, ctx, nw, sc, sh, csc, csh)


def _mod_bwd(x, ctx, nw, sc, sh, csc, csh, dhx, dx_res):
    l, d = x.shape
    lc = ctx.shape[0]
    t = min(256, lc)
    nl, nc = l // t, lc // t

    def body(x_ref, c_ref, nw_ref, sc_ref, sh_ref, csc_ref, csh_ref, dh_ref, dr_ref,
             dx_ref, dnw_ref, dsc_ref, dsh_ref, dcsc_ref, dcsh_ref):
        i = pl.program_id(0)

        @pl.when(i == 0)
        def _():
            for r in (dnw_ref, dsc_ref, dsh_ref, dcsc_ref, dcsh_ref):
                r[...] = jnp.zeros_like(r)

        @pl.when(i < nl)
        def _():
            _, vjp = jax.vjp(_f_mod, x_ref[...], nw_ref[...], sc_ref[...], sh_ref[...])
            dx, dnw, dsc, dsh = vjp(dh_ref[...])
            dx_ref[...] = dx + dr_ref[...]
            dnw_ref[...] += dnw
            dsc_ref[...] += dsc
            dsh_ref[...] += dsh

        @pl.when(i >= nl)
        def _():
            _, vjp = jax.vjp(_f_mod, c_ref[...], nw_ref[...], csc_ref[...], csh_ref[...])
            _, dnw, dsc, dsh = vjp(dh_ref[...])
            dnw_ref[...] += dnw
            dcsc_ref[...] += dsc
            dcsh_ref[...] += dsh

    lat = pl.BlockSpec((t, d), lambda i: (jnp.minimum(i, nl - 1), 0))
    vec = jax.ShapeDtypeStruct((1, d), F32)
    return pl.pallas_call(
        body, name="mod_bwd", grid=(nl + nc,),
        in_specs=[lat, pl.BlockSpec((t, d), lambda i: (jnp.maximum(i - nl, 0), 0))] + [_bc_spec(d)] * 5
        + [pl.BlockSpec((t, d), lambda i: (i, 0)), lat],
        out_specs=[lat] + [_bc_spec(d)] * 5,
        out_shape=[jax.ShapeDtypeStruct((l, d), F32)] + [vec] * 5, compiler_params=_cp(("arbitrary",)),
    )(x, ctx, nw, sc, sh, csc, csh, dhx, dx_res)


def _gate_fwd(yf, yr, xbc, proj, dsk, wn, l, di):
    t = 128

    def body(yf_ref, yr_ref, xs_ref, z_ref, dsk_ref, wn_ref, o_ref):
        o_ref[...] = _f_gate(yf_ref[...], yr_ref[...], xs_ref[...], z_ref[...], dsk_ref[...], wn_ref[...]).astype(o_ref.dtype)

    row = pl.BlockSpec((t, di), lambda i: (i, 0))
    return pl.pallas_call(
        body, name="gate_fwd", grid=(l // t,), in_specs=[row] * 4 + [_bc_spec(di)] * 2, out_specs=row,
        out_shape=jax.ShapeDtypeStruct((l, di), MXU_DTYPE), compiler_params=_cp(("parallel",)),
    )(yf, yr, xbc, proj, dsk, wn)


def _gate_bwd(yf, yr, xbc, proj, dsk, wn, dya, l, lc, di):
    t = 128
    nl, nc = l // t, lc // t

    def body(yf_ref, yr_ref, xs_ref, z_ref, dsk_ref, wn_ref, g_ref, dy_ref, dz_ref, ddsk_ref, dwn_ref):
        i = pl.program_id(0)

        @pl.when(i == 0)
        def _():
            ddsk_ref[...] = jnp.zeros_like(ddsk_ref)
            dwn_ref[...] = jnp.zeros_like(dwn_ref)

        @pl.when(i < nl)
        def _():
            _, vjp = jax.vjp(_f_gate, yf_ref[...], yr_ref[...], xs_ref[...], z_ref[...], dsk_ref[...], wn_ref[...])
            dyf, _, _, dz, ddsk, dwn = vjp(g_ref[...])
            dy_ref[...] = dyf
            dz_ref[...] = dz.astype(dz_ref.dtype)
            ddsk_ref[...] += ddsk
            dwn_ref[...] += dwn

        @pl.when(i >= nl)
        def _():
            dz_ref[...] = jnp.zeros_like(dz_ref)

    lat = pl.BlockSpec((t, di), lambda i: (jnp.minimum(i, nl - 1), 0))
    vec = jax.ShapeDtypeStruct((1, di), F32)
    return pl.pallas_call(
        body, name="gate_bwd", grid=(nl + nc,),
        in_specs=[lat] * 4 + [_bc_spec(di)] * 2 + [lat],
        out_specs=[lat, pl.BlockSpec((t, di), lambda i: (i, 0))] + [_bc_spec(di)] * 2,
        out_shape=[jax.ShapeDtypeStruct((l, di), F32), jax.ShapeDtypeStruct((l + lc, di), MXU_DTYPE)] + [vec] * 2,
        compiler_params=_cp(("arbitrary",)),
    )(yf, yr, xbc, proj, dsk, wn, dya)


def _ln_fwd(cv, g, b):
    l, d = cv.shape
    t = 256

    def body(cv_ref, g_ref, b_ref, o_ref):
        o_ref[...] = _f_ln(cv_ref[...], g_ref[...], b_ref[...]).astype(o_ref.dtype)

    row = pl.BlockSpec((t, d), lambda i: (i, 0))
    return pl.pallas_call(
        body, name="ln_fwd", grid=(l // t,), in_specs=[row] + [_bc_spec(d)] * 2, out_specs=row,
        out_shape=jax.ShapeDtypeStruct((l, d), MXU_DTYPE), compiler_params=_cp(("parallel",)),
    )(cv, g, b)


def _ln_bwd(cv, g, b, dcf):
    l, d = cv.shape
    t = 256

    def body(cv_ref, g_ref, b_ref, dcf_ref, dcv_ref, dg_ref, db_ref):
        _, vjp = jax.vjp(_f_ln, cv_ref[...], g_ref[...], b_ref[...])
        dcv, dg, db = vjp(dcf_ref[...])
        dcv_ref[...] = dcv
        i = pl.program_id(0)
        _acc(dg_ref, dg, i)
        _acc(db_ref, db, i)

    row = pl.BlockSpec((t, d), lambda i: (i, 0))
    vec = jax.ShapeDtypeStruct((1, d), F32)
    return pl.pallas_call(
        body, name="ln_bwd", grid=(l // t,), in_specs=[row] + [_bc_spec(d)] * 2 + [row],
        out_specs=[row] + [_bc_spec(d)] * 2, out_shape=[jax.ShapeDtypeStruct((l, d), F32), vec, vec],
        compiler_params=_cp(("arbitrary",)),
    )(cv, g, b, dcf)


def _merge_fwd(ya, yb, proj, ga_blk):
    l, d = ya.shape
    t = 256

    def body(ya_ref, yb_ref, ga_ref, gb_ref, o_ref):
        o_ref[...] = _f_merge(ya_ref[...], yb_ref[...], ga_ref[...], gb_ref[...]).astype(o_ref.dtype)

    row = pl.BlockSpec((t, d), lambda i: (i, 0))
    return pl.pallas_call(
        body, name="merge_fwd", grid=(l // t,),
        in_specs=[row, row, pl.BlockSpec((t, d), lambda i: (i, ga_blk)), pl.BlockSpec((t, d), lambda i: (i, ga_blk + 1))],
        out_specs=row, out_shape=jax.ShapeDtypeStruct((l, d), MXU_DTYPE), compiler_params=_cp(("parallel",)),
    )(ya, yb, proj, proj)


def _merge_bwd(ya, yb, proj, ga_blk, dmerged, lc):
    l, d = ya.shape
    t = min(256, lc)
    nl, nc = l // t, lc // t

    def body(ya_ref, yb_ref, ga_ref, gb_ref, g_ref, dya_ref, dyb_ref, dga_ref, dgb_ref):
        i = pl.program_id(0)

        @pl.when(i < nl)
        def _():
            _, vjp = jax.vjp(_f_merge, ya_ref[...], yb_ref[...], ga_ref[...], gb_ref[...])
            dya, dyb, dga, dgb = vjp(g_ref[...])
            dya_ref[...] = dya.astype(dya_ref.dtype)
            dyb_ref[...] = dyb.astype(dyb_ref.dtype)
            dga_ref[...] = dga.astype(dga_ref.dtype)
            dgb_ref[...] = dgb.astype(dgb_ref.dtype)

        @pl.when(i >= nl)
        def _():
            dga_ref[...] = jnp.zeros_like(dga_ref)
            dgb_ref[...] = jnp.zeros_like(dgb_ref)

    lat = pl.BlockSpec((t, d), lambda i: (jnp.minimum(i, nl - 1), 0))
    full = pl.BlockSpec((t, d), lambda i: (i, 0))
    return pl.pallas_call(
        body, name="merge_bwd", grid=(nl + nc,),
        in_specs=[lat, lat, pl.BlockSpec((t, d), lambda i: (jnp.minimum(i, nl - 1), ga_blk)),
                  pl.BlockSpec((t, d), lambda i: (jnp.minimum(i, nl - 1), ga_blk + 1)), lat],
        out_specs=[lat, lat, full, full],
        out_shape=[jax.ShapeDtypeStruct((l, d), MXU_DTYPE)] * 2 + [jax.ShapeDtypeStruct((l + lc, d), MXU_DTYPE)] * 2,
        compiler_params=_cp(("arbitrary",)),
    )(ya, yb, proj, proj, dmerged)


def _res_fwd(x, mix, g1, wn, sc2, sh2):
    l, d = x.shape
    t = 256

    def body(x_ref, m_ref, g1_ref, wn_ref, sc_ref, sh_ref, x1_ref, hx_ref):
        x1, hx = _f_res(x_ref[...], m_ref[...], g1_ref[...], wn_ref[...], sc_ref[...], sh_ref[...])
        x1_ref[...] = x1
        hx_ref[...] = hx.astype(hx_ref.dtype)

    row = pl.BlockSpec((t, d), lambda i: (i, 0))
    return pl.pallas_call(
        body, name="res_fwd", grid=(l // t,), in_specs=[row, row] + [_bc_spec(d)] * 4, out_specs=[row, row],
        out_shape=[jax.ShapeDtypeStruct((l, d), F32), jax.ShapeDtypeStruct((l, d), MXU_DTYPE)],
        compiler_params=_cp(("parallel",)),
    )(x, mix, g1, wn, sc2, sh2)


def _res_bwd(x, mix, g1, wn, sc2, sh2, dx1, dhx2):
    l, d = x.shape
    t = 256

    def body(x_ref, m_ref, g1_ref, wn_ref, sc_ref, sh_ref, dx1_ref, dh_ref, dx_ref, dm_ref, dg1_ref, dwn_ref, dsc_ref, dsh_ref):
        _, vjp = jax.vjp(_f_res, x_ref[...], m_ref[...], g1_ref[...], wn_ref[...], sc_ref[...], sh_ref[...])
        dx, dm, dg1, dwn, dsc, dsh = vjp((dx1_ref[...], dh_ref[...]))
        dx_ref[...] = dx
        dm_ref[...] = dm.astype(dm_ref.dtype)
        i = pl.program_id(0)
        _acc(dg1_ref, dg1, i)
        _acc(dwn_ref, dwn, i)
        _acc(dsc_ref, dsc, i)
        _acc(dsh_ref, dsh, i)

    row = pl.BlockSpec((t, d), lambda i: (i, 0))
    vec = jax.ShapeDtypeStruct((1, d), F32)
    return pl.pallas_call(
        body, name="res_bwd", grid=(l // t,), in_specs=[row, row] + [_bc_spec(d)] * 4 + [row, row],
        out_specs=[row, row] + [_bc_spec(d)] * 4,
        out_shape=[jax.ShapeDtypeStruct((l, d), F32), jax.ShapeDtypeStruct((l, d), MXU_DTYPE)] + [vec] * 4,
        compiler_params=_cp(("arbitrary",)),
    )(x, mix, g1, wn, sc2, sh2, dx1, dhx2)


def _swiglu_fwd(gu, df):
    l = gu.shape[0]
    t = 256

    def body(g_ref, u_ref, o_ref):
        o_ref[...] = _f_swiglu(g_ref[...], u_ref[...]).astype(o_ref.dtype)

    return pl.pallas_call(
        body, name="swiglu_fwd", grid=(l // t,),
        in_specs=[pl.BlockSpec((t, df), lambda i: (i, 0)), pl.BlockSpec((t, df), lambda i: (i, 1))],
        out_specs=pl.BlockSpec((t, df), lambda i: (i, 0)),
        out_shape=jax.ShapeDtypeStruct((l, df), MXU_DTYPE), compiler_params=_cp(("parallel",)),
    )(gu, gu)


def _swiglu_bwd(gu, dact, df):
    l = gu.shape[0]
    t = 256

    lo, hi = pl.BlockSpec((t, df), lambda i: (i, 0)), pl.BlockSpec((t, df), lambda i: (i, 1))
    dgu = jax.ShapeDtypeStruct((l, 2 * df), MXU_DTYPE)

    def body(g_ref, u_ref, da_ref, dgu_ref):
        _, vjp = jax.vjp(_f_swiglu, g_ref[...], u_ref[...])
        dg, du = vjp(da_ref[...])
        dgu_ref[:, :df] = dg.astype(dgu_ref.dtype)
        dgu_ref[:, df:] = du.astype(dgu_ref.dtype)

    return pl.pallas_call(
        body, name="swiglu_bwd", grid=(l // t,), in_specs=[lo, hi, lo],
        out_specs=pl.BlockSpec((t, 2 * df), lambda i: (i, 0)), out_shape=dgu, compiler_params=_cp(("parallel",)),
    )(gu, gu, dact)


def _loss_and_grads(x1, dn, g2, wn, tgt):
    l, d = x1.shape
    t = 256

    def body(x1_ref, dn_ref, g2_ref, wn_ref, t_ref, loss_ref, dx_ref, ddn_ref, dg2_ref, dwn_ref):
        loss, vjp = jax.vjp(lambda a, b, c, e: _f_loss(a, b, c, e, t_ref[...]), x1_ref[...], dn_ref[...], g2_ref[...], wn_ref[...])
        dx, ddn, dg2, dwn = vjp(jnp.ones((1, 1), F32))
        dx_ref[...] = dx
        ddn_ref[...] = ddn.astype(ddn_ref.dtype)
        i = pl.program_id(0)
        _acc(loss_ref, loss, i)
        _acc(dg2_ref, dg2, i)
        _acc(dwn_ref, dwn, i)

    row = pl.BlockSpec((t, d), lambda i: (i, 0))
    vec = jax.ShapeDtypeStruct((1, d), F32)
    return pl.pallas_call(
        body, name="loss_and_grads", grid=(l // t,), in_specs=[row, row] + [_bc_spec(d)] * 2 + [row],
        out_specs=[pl.BlockSpec((1, 1), lambda i: (0, 0)), row, row] + [_bc_spec(d)] * 2,
        out_shape=[jax.ShapeDtypeStruct((1, 1), F32), jax.ShapeDtypeStruct((l, d), F32),
                   jax.ShapeDtypeStruct((l, d), MXU_DTYPE), vec, vec],
        compiler_params=_cp(("arbitrary",)),
    )(x1, dn, g2, wn, tgt)


PAD = 8


def _conv5_taps(s_ref, w_ref, l, lc, width):
    half = width // 2
    lat = sum(w_ref[k:k + 1, :] * s_ref[pl.ds(PAD + k - half, l), :] for k in range(width))
    ctx = sum(w_ref[k:k + 1, :] * s_ref[pl.ds(2 * PAD + l + k - half, lc), :] for k in range(width))
    return lat, ctx


def _fill_padded(s_ref, lat, ctx, l, lc):
    zeros = jnp.zeros((PAD, s_ref.shape[1]), F32)
    s_ref[pl.ds(0, PAD), :] = zeros
    s_ref[pl.ds(PAD, l), :] = lat
    s_ref[pl.ds(PAD + l, PAD), :] = zeros
    s_ref[pl.ds(2 * PAD + l, lc), :] = ctx
    s_ref[pl.ds(2 * PAD + l + lc, PAD), :] = zeros


def _conv5_fwd(proj, w, b, l, lc, col0, ncols):
    t_all = l + lc
    cw = LANES
    blk0 = col0 // cw
    width = w.shape[0]

    def body(x_ref, w_ref, b_ref, o_ref, s_ref):
        _fill_padded(s_ref, x_ref[pl.ds(0, l), :], x_ref[pl.ds(l, lc), :], l, lc)
        lat, ctx = _conv5_taps(s_ref, w_ref, l, lc, width)
        o_ref[pl.ds(0, l), :] = _silu(lat + b_ref[...])
        o_ref[pl.ds(l, lc), :] = _silu(ctx + b_ref[...])

    return pl.pallas_call(
        body, name="conv5_fwd", grid=(ncols // cw,),
        in_specs=[pl.BlockSpec((t_all, cw), lambda j: (0, blk0 + j)), pl.BlockSpec((width, cw), lambda j: (0, j)),
                  pl.BlockSpec((1, cw), lambda j: (0, j))],
        out_specs=pl.BlockSpec((t_all, cw), lambda j: (0, j)),
        out_shape=jax.ShapeDtypeStruct((t_all, ncols), F32),
        scratch_shapes=[pltpu.VMEM((t_all + 3 * PAD, cw), F32)], compiler_params=_cp(("parallel",)),
    )(proj, w, b)


def _conv5_bwd(proj, w, b, cots, l, lc, col0, seg0, ncols):
    t_all = l + lc
    cw = LANES
    blk0, sblk0 = col0 // cw, seg0 // cw
    width = w.shape[0]
    half = width // 2
    nc = len(cots)

    def body(*refs):
        x_ref, w_ref, b_ref = refs[:3]
        cot_refs = refs[3:3 + nc]
        dx_ref, dw_ref, db_ref, s_ref = refs[3 + nc:]
        x_lat, x_ctx = x_ref[pl.ds(0, l), :], x_ref[pl.ds(l, lc), :]
        _fill_padded(s_ref, x_lat, x_ctx, l, lc)
        pre_lat, pre_ctx = _conv5_taps(s_ref, w_ref, l, lc, width)
        g = sum(c[...] for c in cot_refs)

        def through_silu(pre, cot):
            _, vjp = jax.vjp(_silu, pre + b_ref[...])
            return vjp(cot)[0]

        d_lat = through_silu(pre_lat, g[:l])
        d_ctx = through_silu(pre_ctx, g[l:])
        db_ref[...] = jnp.sum(d_lat, axis=0, keepdims=True) + jnp.sum(d_ctx, axis=0, keepdims=True)
        for k in range(width):
            dw_ref[k:k + 1, :] = (
                jnp.sum(d_lat * s_ref[pl.ds(PAD + k - half, l), :], axis=0, keepdims=True)
                + jnp.sum(d_ctx * s_ref[pl.ds(2 * PAD + l + k - half, lc), :], axis=0, keepdims=True))
        _fill_padded(s_ref, d_lat, d_ctx, l, lc)
        dx_lat = sum(w_ref[k:k + 1, :] * s_ref[pl.ds(PAD - (k - half), l), :] for k in range(width))
        dx_ctx = sum(w_ref[k:k + 1, :] * s_ref[pl.ds(2 * PAD + l - (k - half), lc), :] for k in range(width))
        dx_ref[pl.ds(0, l), :] = dx_lat.astype(dx_ref.dtype)
        dx_ref[pl.ds(l, lc), :] = dx_ctx.astype(dx_ref.dtype)

    col = pl.BlockSpec((t_all, cw), lambda j: (0, j))
    return pl.pallas_call(
        body, name=f"conv5_bwd_{seg0}", grid=(ncols // cw,),
        in_specs=[pl.BlockSpec((t_all, cw), lambda j: (0, blk0 + sblk0 + j)),
                  pl.BlockSpec((width, cw), lambda j: (0, sblk0 + j)), pl.BlockSpec((1, cw), lambda j: (0, sblk0 + j))]
        + [col] * nc,
        out_specs=[col, pl.BlockSpec((width, cw), lambda j: (0, j)), pl.BlockSpec((1, cw), lambda j: (0, j))],
        out_shape=[jax.ShapeDtypeStruct((t_all, ncols), MXU_DTYPE), jax.ShapeDtypeStruct((width, ncols), F32),
                   jax.ShapeDtypeStruct((1, ncols), F32)],
        scratch_shapes=[pltpu.VMEM((t_all + 3 * PAD, cw), F32)], compiler_params=_cp(("parallel",)),
    )(proj, w, b, *cots)


def _conv31_fwd(proj, w, b, l, d, u_blk):
    cw = LANES
    width = w.shape[0]
    reach = (width // 2) * GRID_W
    nb = d // cw

    def body(u_ref, v_ref, w_ref, b_ref, o_ref, s_ref):
        s_ref[pl.ds(0, reach), :] = jnp.zeros((reach, cw), F32)
        s_ref[pl.ds(reach, l), :] = u_ref[...] * jax.nn.sigmoid(v_ref[...])
        s_ref[pl.ds(reach + l, reach), :] = jnp.zeros((reach, cw), F32)
        o_ref[...] = sum(w_ref[k:k + 1, :] * s_ref[pl.ds(k * GRID_W, l), :] for k in range(width)) + b_ref[...]

    return pl.pallas_call(
        body, name="conv31_fwd", grid=(nb,),
        in_specs=[pl.BlockSpec((l, cw), lambda j: (0, u_blk * nb + j)), pl.BlockSpec((l, cw), lambda j: (0, (u_blk + 1) * nb + j)),
                  pl.BlockSpec((width, cw), lambda j: (0, j)), pl.BlockSpec((1, cw), lambda j: (0, j))],
        out_specs=pl.BlockSpec((l, cw), lambda j: (0, j)), out_shape=jax.ShapeDtypeStruct((l, d), F32),
        scratch_shapes=[pltpu.VMEM((l + 2 * reach, cw), F32)], compiler_params=_cp(("parallel",)),
    )(proj, proj, w, b)


def _conv31_bwd(proj, w, dcv, l, lc, d, u_blk):
    cw = LANES
    width = w.shape[0]
    reach = (width // 2) * GRID_W
    nb = d // cw
    t_all = l + lc

    def body(u_ref, v_ref, w_ref, g_ref, du_ref, dv_ref, dw_ref, db_ref, s_ref):
        zeros = jnp.zeros((reach, cw), F32)
        s_ref[pl.ds(0, reach), :] = zeros
        s_ref[pl.ds(reach + l, reach), :] = zeros
        u, v, g = u_ref[...], v_ref[...], g_ref[...]
        s_ref[pl.ds(reach, l), :] = u * jax.nn.sigmoid(v)
        db_ref[...] = jnp.sum(g, axis=0, keepdims=True)
        for k in range(width):
            dw_ref[k:k + 1, :] = jnp.sum(g * s_ref[pl.ds(k * GRID_W, l), :], axis=0, keepdims=True)
        s_ref[pl.ds(reach, l), :] = g
        dt = sum(w_ref[k:k + 1, :] * s_ref[pl.ds((width - 1 - k) * GRID_W, l), :] for k in range(width))
        _, vjp = jax.vjp(lambda a, c: a * jax.nn.sigmoid(c), u, v)
        du, dv = vjp(dt)
        du_ref[pl.ds(0, l), :] = du.astype(du_ref.dtype)
        dv_ref[pl.ds(0, l), :] = dv.astype(dv_ref.dtype)
        du_ref[pl.ds(l, lc), :] = jnp.zeros((lc, cw), du_ref.dtype)
        dv_ref[pl.ds(l, lc), :] = jnp.zeros((lc, cw), dv_ref.dtype)

    pshape = jax.ShapeDtypeStruct((t_all, d), MXU_DTYPE)
    tall = pl.BlockSpec((t_all, cw), lambda j: (0, j))
    return pl.pallas_call(
        body, name="conv31_bwd", grid=(nb,),
        in_specs=[pl.BlockSpec((l, cw), lambda j: (0, u_blk * nb + j)), pl.BlockSpec((l, cw), lambda j: (0, (u_blk + 1) * nb + j)),
                  pl.BlockSpec((width, cw), lambda j: (0, j)), pl.BlockSpec((l, cw), lambda j: (0, j))],
        out_specs=[tall, tall, pl.BlockSpec((width, cw), lambda j: (0, j)), pl.BlockSpec((1, cw), lambda j: (0, j))],
        out_shape=[pshape, pshape, jax.ShapeDtypeStruct((width, d), F32), jax.ShapeDtypeStruct((1, d), F32)],
        scratch_shapes=[pltpu.VMEM((l + 2 * reach, cw), F32)], compiler_params=_cp(("parallel",)),
    )(proj, proj, w, dcv)


def _softplus(x):
    return jnp.maximum(x, 0.0) + jnp.log(1.0 + jnp.exp(-jnp.abs(x)))


def _dt_fwd(proj, bias, a, dt_blk):
    t_all = proj.shape[0]
    hh = bias.shape[1]
    q = CHUNK

    def body(r_ref, b_ref, a_ref, dt_ref, cs_ref, tc_ref, cst_ref):
        dt = _softplus(r_ref[...] + b_ref[...])
        dt_ref[...] = dt
        da = dt * a_ref[...]
        li, si = _iota((q, q), 0), _iota((q, q), 1)
        reverse_cols = _iota((q, hh), 1) >= hh // 2
        cs = jnp.where(reverse_cols, _dot((si >= li).astype(F32), da, exact=True), _dot((si <= li).astype(F32), da, exact=True))
        cs_ref[...] = cs
        cst_ref[...] = cs.T
        total = jnp.where(_iota((1, hh), 1) >= hh // 2, cs_ref[0:1, :], cs_ref[q - 1:q, :])
        tc_ref[...] = total - cs

    row = pl.BlockSpec((q, hh), lambda i: (i, 0))
    shape = jax.ShapeDtypeStruct((t_all, hh), F32)
    return pl.pallas_call(
        body, name="dt_fwd", grid=(t_all // q,),
        in_specs=[pl.BlockSpec((q, hh), lambda i: (i, dt_blk)), _bc_spec(hh), _bc_spec(hh)],
        out_specs=[row, row, row, pl.BlockSpec((hh, q), lambda i: (0, i))],
        out_shape=[shape, shape, shape, jax.ShapeDtypeStruct((hh, t_all), F32)],
        compiler_params=_cp(("parallel",)),
    )(proj, bias, a)


def _three_way(x):
    def top(v):
        word = lax.bitcast_convert_type(v, jnp.uint32) & jnp.uint32(0xFFFF0000)
        return lax.bitcast_convert_type(word, F32)

    hi = top(x)
    rest = x - hi
    mid = top(rest)
    return hi.astype(jnp.bfloat16), mid.astype(jnp.bfloat16), (rest - mid).astype(jnp.bfloat16)


def _scan_columns(dt, cs, tc, groups2, hpg):
    t_all = dt.shape[0]
    parts = [part.reshape(t_all, groups2, 1, hpg) for arr in (dt, cs, tc) for part in _three_way(arr)]
    cols = jnp.concatenate(parts, axis=2).transpose(1, 0, 2, 3).reshape(groups2, t_all, 9 * hpg)
    return jnp.pad(cols, ((0, 0), (0, 0), (0, LANES - 9 * hpg)))


def _dt_bwd(proj, bias, dt, ddt, dda, dt_blk):
    t_all = proj.shape[0]
    hh = bias.shape[1]
    q = _tile(t_all, 1024, LANES)

    def body(r_ref, b_ref, dt_ref, ddt_ref, dda_ref, dr_ref, db_ref, da_ref):
        dr = ddt_ref[...] * jax.nn.sigmoid(r_ref[...] + b_ref[...])
        dr_ref[...] = dr.astype(dr_ref.dtype)
        i = pl.program_id(0)
        _acc(db_ref, jnp.sum(dr, axis=0, keepdims=True), i)
        _acc(da_ref, jnp.sum(dda_ref[...] * dt_ref[...], axis=0, keepdims=True), i)

    row = pl.BlockSpec((q, hh), lambda i: (i, 0))
    vec = jax.ShapeDtypeStruct((1, hh), F32)
    return pl.pallas_call(
        body, name="dt_bwd", grid=(t_all // q,),
        in_specs=[pl.BlockSpec((q, hh), lambda i: (i, dt_blk)), _bc_spec(hh), row, row, row],
        out_specs=[row, _bc_spec(hh), _bc_spec(hh)],
        out_shape=[jax.ShapeDtypeStruct((t_all, hh), MXU_DTYPE), vec, vec], compiler_params=_cp(("arbitrary",)),
    )(proj, bias, dt, ddt, dda)


_NT = (((1,), (1,)), ((), ()))
_TN = (((0,), (0,)), ((), ()))


def _dot(a, b, dims=None, exact=False):
    kw = dict(preferred_element_type=F32)
    if exact:
        kw["precision"] = HI
    if dims is None:
        return jnp.dot(a, b, **kw)
    return lax.dot_general(a, b, dims, **kw)


def _iota(shape, dim):
    return lax.broadcasted_iota(jnp.int32, shape, dim)


class _Ssd:
    def __init__(self, l, lc, di, p, reverse):
        self.q, self.n, self.g = CHUNK, SSM_STATE, SSM_GROUPS
        self.nl, self.ncx = l // CHUNK, lc // CHUNK
        self.ns = self.nl + self.ncx
        self.t_all, self.di, self.p, self.reverse = l + lc, di, p, reverse
        self.hpg = di // p // SSM_GROUPS
        self.gw = self.hpg * p
        self.ntile = self.gw // LANES
        self.hpt = LANES // p
        self.log2p = p.bit_length() - 1
        assert 1 << self.log2p == p and self.gw % LANES == 0 and self.n == LANES and self.q == LANES
        assert 9 * self.hpg <= LANES
        self.d = 1 if reverse else 0

    def chunk_at(self, step):
        if self.reverse:
            return self.ns - 1 - step
        return jnp.where(step < self.ncx, self.nl + step, step - self.ncx)

    def selectors(self):
        hpg = self.hpg
        k = jnp.arange(LANES)
        quantity, head, used = k // (3 * hpg), k % hpg, k < 9 * hpg
        lane_head = jnp.arange(LANES) // self.p
        tiles = jnp.concatenate([(used & (quantity == qo))[:, None] & (head[:, None] == tt * self.hpt + lane_head[None, :])
                                 for tt in range(self.ntile) for qo in range(3)], axis=1)
        heads = jnp.concatenate([jnp.broadcast_to((used & (quantity == 1) & (head == j))[:, None], (LANES, LANES))
                                 for j in range(hpg)], axis=1)
        return tiles.astype(jnp.bfloat16), heads.astype(jnp.bfloat16)

    def in_specs(self, chunk_of):
        g, n, hpg, q = self.g, self.n, self.hpg, self.q
        b_blk, c_blk = self.di // n, self.di // n + g
        d = self.d
        return [
            pl.BlockSpec((q, self.gw), lambda gi, i: (chunk_of(i), gi)),
            pl.BlockSpec((q, n), lambda gi, i: (chunk_of(i), b_blk + gi)),
            pl.BlockSpec((q, n), lambda gi, i: (chunk_of(i), c_blk + gi)),
            pl.BlockSpec((1, q, LANES), lambda gi, i: (d * g + gi, chunk_of(i), 0)),
            pl.BlockSpec((hpg, q), lambda gi, i: (d * g + gi, chunk_of(i))),
            pl.BlockSpec((LANES, self.ntile * 3 * LANES), lambda gi, i: (0, 0)),
            pl.BlockSpec((LANES, hpg * LANES), lambda gi, i: (0, 0)),
        ]

    def masks(self):
        li, si = _iota((self.q, self.q), 0), _iota((self.q, self.q), 1)
        if self.reverse:
            return si >= li, li >= si
        return si <= li, li <= si

    def spread(self, spread_all, tt):
        at = 3 * LANES * tt
        return tuple(spread_all[:, at + k * LANES:at + (k + 1) * LANES] for k in range(3))

    def head_lanes(self, qq):
        return lax.shift_right_logical(_iota((self.q, LANES), 1), self.log2p) == qq

    def head_sums(self, values, tt):
        sel = _iota((HEAD_COLS, LANES), 0) == tt * self.hpt + lax.shift_right_logical(_iota((HEAD_COLS, LANES), 1), self.log2p)
        parts = [part for v in values for part in _three_way(v)]
        sums = _dot(jnp.concatenate(parts, axis=0), sel.astype(jnp.bfloat16), _NT)
        out, at = [], 0
        for v in values:
            rows = v.shape[0]
            out.append(sums[at:at + rows] + sums[at + rows:at + 2 * rows] + sums[at + 2 * rows:at + 3 * rows])
            at += 3 * rows
        return out

    def state_scale(self, csr_ref):
        last = 0 if self.reverse else self.q - 1
        total = jnp.sum(jnp.where(_iota((self.hpg, self.q), 1) == last, csr_ref[...], 0.0), axis=1, keepdims=True)
        decay = jnp.broadcast_to(jnp.exp(total), (self.hpg, self.n))
        decay = jnp.concatenate([decay, jnp.zeros((HEAD_COLS - self.hpg, self.n), F32)], axis=0)
        rows = lax.shift_right_logical(_iota((self.gw, HEAD_COLS), 0), self.log2p) == _iota((self.gw, HEAD_COLS), 1)
        return _dot_parts(rows.astype(jnp.bfloat16), decay)


def _dot_parts(sel, v, dims=None):
    return sum(_dot(sel, part, dims) for part in _three_way(v))


def _ssd_fwd(xbc, cols, cs_t, l, lc, di, p, reverse):
    s = _Ssd(l, lc, di, p, reverse)
    q, n, gw = s.q, s.n, s.gw
    neg_inf = float("-inf")

    def body(xs_ref, b_ref, c_ref, cols_ref, csr_ref, et_ref, eh_ref, y_ref, hp_ref, h_scr):
        i = pl.program_id(1)

        @pl.when(i == 0)
        def _():
            h_scr[...] = jnp.zeros_like(h_scr)

        h = h_scr[...]
        hp_ref[0, 0] = h
        mask, _ = s.masks()
        cols = cols_ref[0]
        bb, cb = b_ref[...].astype(MXU_DTYPE), c_ref[...].astype(MXU_DTYPE)
        cbt = _dot(cb, bb, _NT)
        y_off = _dot(cb, h.astype(MXU_DTYPE), _NT)
        spread_all, cs_heads = _dot(cols, et_ref[...]), _dot(cols, eh_ref[...])
        w_tiles = []
        for tt in range(s.ntile):
            sl = slice(tt * LANES, (tt + 1) * LANES)
            dt_b, cs_b, tc_b = s.spread(spread_all, tt)
            x = xs_ref[:, sl] * dt_b
            ms, xhs = [], []
            for qq in range(s.hpt):
                j = tt * s.hpt + qq
                seg = cs_heads[:, j * LANES:(j + 1) * LANES] - csr_ref[j:j + 1, :]
                ms.append((cbt * jnp.exp(jnp.where(mask, seg, neg_inf))).astype(MXU_DTYPE))
                xhs.append(jnp.where(s.head_lanes(qq), x, 0.0).astype(MXU_DTYPE))
            yd = _dot(jnp.concatenate(ms, axis=1), jnp.concatenate(xhs, axis=0))
            y_ref[:, sl] = yd + y_off[:, sl] * jnp.exp(cs_b)
            w_tiles.append((x * jnp.exp(tc_b)).astype(MXU_DTYPE))
        wm = w_tiles[0] if s.ntile == 1 else jnp.concatenate(w_tiles, axis=1)
        h_scr[...] = h * s.state_scale(csr_ref) + _dot(wm, bb, _TN)

    d = "rev" if reverse else "fwd"
    e_tiles, e_heads = s.selectors()
    return pl.pallas_call(
        body, name=f"ssd_{d}", grid=(s.g, s.ns), in_specs=s.in_specs(s.chunk_at),
        out_specs=[pl.BlockSpec((q, gw), lambda gi, i: (s.chunk_at(i), gi)),
                   pl.BlockSpec((1, 1, gw, n), lambda gi, i: (i, gi, 0, 0))],
        out_shape=[jax.ShapeDtypeStruct((s.t_all, di), F32), jax.ShapeDtypeStruct((s.ns, s.g, gw, n), F32)],
        scratch_shapes=[pltpu.VMEM((gw, n), F32)],
        compiler_params=_cp(("parallel", "arbitrary")),
    )(xbc, xbc, xbc, cols, cs_t, e_tiles, e_heads)


def _ssd_bwd(xbc, cols, cs_t, a_cols, dy, hprev, l, lc, di, p, reverse, dsk=None, prev=None):
    s = _Ssd(l, lc, di, p, reverse)
    q, n, gw, hpg = s.q, s.n, s.gw, s.hpg
    neg_inf = float("-inf")
    n_extra = (dsk is not None) + (3 if prev is not None else 0)

    def chunk_of(i):
        return s.chunk_at(s.ns - 1 - i)

    def body(xs_ref, b_ref, c_ref, cols_ref, csr_ref, et_ref, eh_ref, ac_ref, dy_ref, hp_ref, *rest):
        extra, (dxs_ref, db_ref, dc_ref, ddt_ref, dda_ref, dh_scr) = rest[:n_extra], rest[n_extra:]
        dsk_ref = extra[0] if dsk is not None else None
        prev_refs = extra[-3:] if prev is not None else None
        i = pl.program_id(1)

        @pl.when(i == 0)
        def _():
            dh_scr[...] = jnp.zeros_like(dh_scr)

        latent = (chunk_of(i) < s.nl).astype(F32)
        h, dh = hp_ref[0, 0], dh_scr[...]
        hb, dhb = h.astype(MXU_DTYPE), dh.astype(MXU_DTYPE)
        mask, mask_t = s.masks()
        cols = cols_ref[0]
        bb, cb = b_ref[...].astype(MXU_DTYPE), c_ref[...].astype(MXU_DTYPE)
        cbt, bct = _dot(cb, bb, _NT), _dot(bb, cb, _NT)
        b_dh = _dot(bb, dhb, _NT)
        y_off0 = _dot(cb, hb, _NT)
        d_g, d_gt = jnp.zeros((q, q), F32), jnp.zeros((q, q), F32)
        dcs = jnp.zeros((q, HEAD_COLS), F32)
        ddt_x = jnp.zeros((q, HEAD_COLS), F32)
        r_state = jnp.zeros((16, HEAD_COLS), F32)
        spread_all, cs_heads = _dot(cols, et_ref[...]), _dot(cols, eh_ref[...])
        dye_tiles, xte_tiles = [], []
        for tt in range(s.ntile):
            sl = slice(tt * LANES, (tt + 1) * LANES)
            dt_b, cs_b, tc_b = s.spread(spread_all, tt)
            ecs_b, te_b = jnp.exp(cs_b), jnp.exp(tc_b)
            xs_t = xs_ref[:, sl]
            x = xs_t * dt_b
            d_y = dy_ref[:, sl] * latent
            dx_state = b_dh[:, sl] * te_b
            lms, lm_ts, m_ts, d_yhs, xhs = [], [], [], [], []
            for qq in range(s.hpt):
                j = tt * s.hpt + qq
                csc_b = cs_heads[:, j * LANES:(j + 1) * LANES]
                csr = csr_ref[j:j + 1, :]
                lms.append(jnp.exp(jnp.where(mask, csc_b - csr, neg_inf)))
                lm_ts.append(jnp.exp(jnp.where(mask_t, csr - csc_b, neg_inf)))
                m_ts.append(bct * lm_ts[-1])
                lanes = s.head_lanes(qq)
                d_yhs.append(jnp.where(lanes, d_y, 0.0).astype(MXU_DTYPE))
                xhs.append(jnp.where(lanes, x, 0.0).astype(MXU_DTYPE))
            d_yh_rows = jnp.concatenate(d_yhs, axis=0)
            d_m_all = _dot(d_yh_rows, x.astype(MXU_DTYPE), _NT)
            d_mt_all = _dot(jnp.concatenate(xhs, axis=0), d_y.astype(MXU_DTYPE), _NT)
            for qq in range(s.hpt):
                j = tt * s.hpt + qq
                d_m, d_mt = d_m_all[qq * q:(qq + 1) * q], d_mt_all[qq * q:(qq + 1) * q]
                r1 = jnp.sum(d_m * (cbt * lms[qq]), axis=1, keepdims=True)
                r2 = jnp.sum(d_mt * m_ts[qq], axis=1, keepdims=True)
                dcs = dcs + (r1 - r2) * (_iota((1, HEAD_COLS), 1) == j).astype(F32)
                d_g = d_g + d_m * lms[qq]
                d_gt = d_gt + d_mt * lm_ts[qq]
            d_x = _dot(jnp.concatenate([m.astype(MXU_DTYPE) for m in m_ts], axis=1), d_yh_rows) + dx_state
            d_xs = d_x * dt_b
            if dsk_ref is not None:
                d_xs = d_xs + d_y * dsk_ref[:, sl]
            if prev_refs is not None:
                d_xs = d_xs + prev_refs[0][:, sl]
            dxs_ref[:, sl] = d_xs
            fed = x * dx_state
            fed_rows = jnp.broadcast_to(jnp.sum(fed, axis=0, keepdims=True), (16, LANES))
            sums = s.head_sums([d_x * xs_t, d_y * y_off0[:, sl] * ecs_b - fed, fed_rows], tt)
            ddt_x, dcs, r_state = ddt_x + sums[0], dcs + sums[1], r_state + sums[2]
            dye_tiles.append((d_y * ecs_b).astype(MXU_DTYPE))
            xte_tiles.append((x * te_b).astype(MXU_DTYPE))
        dye = dye_tiles[0] if s.ntile == 1 else jnp.concatenate(dye_tiles, axis=1)
        xte = xte_tiles[0] if s.ntile == 1 else jnp.concatenate(xte_tiles, axis=1)
        d_c = _dot(d_g.astype(MXU_DTYPE), bb) + _dot(dye, hb)
        d_b = _dot(d_gt.astype(MXU_DTYPE), cb) + _dot(xte, dhb)
        if prev_refs is not None:
            d_b, d_c = d_b + prev_refs[1][...], d_c + prev_refs[2][...]
        dc_ref[...] = d_c
        db_ref[...] = d_b
        scale = s.state_scale(csr_ref)
        carried = dh * h * scale
        d_tot = jnp.sum(r_state, axis=0, keepdims=True) * 0.0625
        for j in range(hpg):
            part = jnp.sum(carried[j * p:(j + 1) * p, :], axis=0, keepdims=True)
            d_tot = d_tot + jnp.sum(part, axis=1, keepdims=True) * (_iota((1, HEAD_COLS), 1) == j).astype(F32)
        dda = _dot_parts(mask_t.astype(jnp.bfloat16), dcs) + d_tot
        ddt_ref[0] = ddt_x + dda * ac_ref[0]
        dda_ref[0] = dda
        dh_scr[...] = dh * scale + _dot(dye, cb, _TN)

    d = "rev" if reverse else "fwd"
    e_tiles, e_heads = s.selectors()
    col = pl.BlockSpec((1, q, HEAD_COLS), lambda gi, i: (gi, chunk_of(i), 0))
    gn = pl.BlockSpec((q, n), lambda gi, i: (chunk_of(i), gi))
    wide = pl.BlockSpec((q, gw), lambda gi, i: (chunk_of(i), gi))
    extra_specs, extra_args, aliases = [], [], {}
    if dsk is not None:
        extra_specs.append(pl.BlockSpec((1, gw), lambda gi, i: (0, gi)))
        extra_args.append(dsk)
    if prev is not None:
        first = 10 + len(extra_args)
        extra_specs += [wide, gn, gn]
        extra_args += list(prev)
        aliases = {first: 0, first + 1: 1, first + 2: 2}
    return pl.pallas_call(
        body, name=f"ssd_bwd_{d}", grid=(s.g, s.ns),
        in_specs=s.in_specs(chunk_of) + [
            pl.BlockSpec((1, 1, HEAD_COLS), lambda gi, i: (s.d * s.g + gi, 0, 0)),
            pl.BlockSpec((q, gw), lambda gi, i: (jnp.minimum(chunk_of(i), s.nl - 1), gi)),
            pl.BlockSpec((1, 1, gw, n), lambda gi, i: (s.ns - 1 - i, gi, 0, 0))] + extra_specs,
        out_specs=[wide, gn, gn, col, col],
        out_shape=[jax.ShapeDtypeStruct((s.t_all, di), F32), jax.ShapeDtypeStruct((s.t_all, s.g * n), F32),
                   jax.ShapeDtypeStruct((s.t_all, s.g * n), F32), jax.ShapeDtypeStruct((s.g, s.t_all, HEAD_COLS), F32),
                   jax.ShapeDtypeStruct((s.g, s.t_all, HEAD_COLS), F32)],
        scratch_shapes=[pltpu.VMEM((gw, n), F32)],
        input_output_aliases=aliases, compiler_params=_cp(("parallel", "arbitrary")),
    )(xbc, xbc, xbc, cols, cs_t, e_tiles, e_heads, a_cols, dy, hprev, *extra_args)


def _ada_fwd(crows, w, b):
    r, d = crows.shape
    ws = w.shape[1]
    tn = _tile(ws, 512, LANES)

    def body(c_ref, w_ref, b_ref, m_ref, s_ref):
        s = _silu(c_ref[...])
        s_ref[...] = s
        m_ref[...] = _dot(s.astype(MXU_DTYPE), w_ref[...].astype(MXU_DTYPE)) + b_ref[...]

    full = pl.BlockSpec((r, d), lambda j: (0, 0))
    return pl.pallas_call(
        body, name="ada_fwd", grid=(ws // tn,),
        in_specs=[full, pl.BlockSpec((d, tn), lambda j: (0, j)), pl.BlockSpec((1, tn), lambda j: (0, j))],
        out_specs=[pl.BlockSpec((r, tn), lambda j: (0, j)), full],
        out_shape=[jax.ShapeDtypeStruct((r, ws), F32), jax.ShapeDtypeStruct((r, d), F32)],
        compiler_params=_cp(("arbitrary",)),
    )(crows, w, b)


def _ada_bwd(s_t, w, dm):
    d, r = s_t.shape
    ws = w.shape[1]
    tn = _tile(ws, 512, LANES)

    def body(st_ref, w_ref, dm_ref, dw_ref, ds_ref):
        dmb = dm_ref[...].astype(MXU_DTYPE)
        dw_ref[...] = _dot(st_ref[...].astype(MXU_DTYPE), dmb)
        _acc(ds_ref, _dot(dmb, w_ref[...].astype(MXU_DTYPE), _NT), pl.program_id(0))

    return pl.pallas_call(
        body, name="ada_bwd", grid=(ws // tn,),
        in_specs=[pl.BlockSpec((d, r), lambda j: (0, 0)), pl.BlockSpec((d, tn), lambda j: (0, j)),
                  pl.BlockSpec((r, tn), lambda j: (0, j))],
        out_specs=[pl.BlockSpec((d, tn), lambda j: (0, j)), pl.BlockSpec((r, d), lambda j: (0, 0))],
        out_shape=[jax.ShapeDtypeStruct((d, ws), F32), jax.ShapeDtypeStruct((r, d), F32)],
        compiler_params=_cp(("arbitrary",)),
    )(s_t, w, dm)


def _adamw(w, g, m, v, name):
    r, c = w.shape
    t = _tile(r, max(8, 300_000 // c), 8)

    def body(w_ref, g_ref, m_ref, v_ref, d_ref, m2_ref, v2_ref):
        g = g_ref[...]
        m2 = ADAM_B1 * m_ref[...] + (1.0 - ADAM_B1) * g
        v2 = ADAM_B2 * v_ref[...] + (1.0 - ADAM_B2) * (g * g)
        m_hat = m2 / (1.0 - ADAM_B1 ** ADAM_STEP)
        v_hat = v2 / (1.0 - ADAM_B2 ** ADAM_STEP)
        d_ref[...] = -ADAM_LR * (m_hat / (jnp.sqrt(v_hat) + ADAM_EPS) + ADAM_WD * w_ref[...])
        m2_ref[...] = m2
        v2_ref[...] = v2

    blk = pl.BlockSpec((t, c), lambda i: (i, 0))
    shape = jax.ShapeDtypeStruct((r, c), F32)
    return pl.pallas_call(
        body, name=name, grid=(r // t,), in_specs=[blk] * 4, out_specs=[blk] * 3, out_shape=[shape] * 3,
        compiler_params=_cp(("parallel",)),
    )(w, g, m, v)


def _sum_devices(gathered):
    rows, w = gathered.shape
    per = rows // N_DEV

    def body(g_ref, o_ref):
        total = g_ref[pl.ds(0, per), :]
        for dev in range(1, N_DEV):
            total = total + g_ref[pl.ds(dev * per, per), :]
        o_ref[...] = total

    return pl.pallas_call(
        body, name="sum_devices", out_shape=jax.ShapeDtypeStruct((per, w), F32),
        in_specs=[pl.BlockSpec(memory_space=pltpu.VMEM)], out_specs=pl.BlockSpec(memory_space=pltpu.VMEM),
        compiler_params=_cp(),
    )(gathered)


def _c_ctx_grad(parts, c_ctx):
    rows, d = parts.shape
    per = rows // N_DEV

    def body(p_ref, c_ref, o_ref):
        total = p_ref[pl.ds(0, 1), :]
        for chip in range(1, N_SHARD):
            total = total + p_ref[pl.ds(2 * chip * per, 1), :]
        _, vjp = jax.vjp(_silu, c_ref[...])
        o_ref[...] = vjp(total)[0]

    return pl.pallas_call(
        body, name="c_ctx_grad", out_shape=jax.ShapeDtypeStruct((1, d), F32),
        in_specs=[pl.BlockSpec(memory_space=pltpu.VMEM)] * 2, out_specs=pl.BlockSpec(memory_space=pltpu.VMEM),
        compiler_params=_cp(),
    )(parts, c_ctx)


def _pad_rows(a, rows, width):
    return jnp.pad(a, ((0, rows - a.shape[0]), (0, width - a.shape[1])))


def _pack(vectors, quantum):
    flat = jnp.concatenate([v.reshape(-1) for v in vectors])
    return jnp.pad(flat, (0, -flat.shape[0] % quantum))


def kernel(x, c, ctx, c_ctx, w_mod, b_mod, norm_mix, w_in, ssm_conv_w, ssm_conv_b, dt_bias, a_log, d_skip, ssm_norm, cf_conv_w, cf_conv_b, cf_ln_g, cf_ln_b, w_proj_a, w_proj_b, w_out, norm_ffn, w_ffn_gate, w_ffn_up, w_ffn_down, norm_final, loss_target, m_c_ctx, m_w_mod, m_b_mod, m_norm_mix, m_w_in, m_ssm_conv_w, m_ssm_conv_b, m_dt_bias, m_a_log, m_d_skip, m_ssm_norm, m_cf_conv_w, m_cf_conv_b, m_cf_ln_g, m_cf_ln_b, m_w_proj_a, m_w_proj_b, m_w_out, m_norm_ffn, m_w_ffn_gate, m_w_ffn_up, m_w_ffn_down, m_norm_final, v_c_ctx, v_w_mod, v_b_mod, v_norm_mix, v_w_in, v_ssm_conv_w, v_ssm_conv_b, v_dt_bias, v_a_log, v_d_skip, v_ssm_norm, v_cf_conv_w, v_cf_conv_b, v_cf_ln_g, v_cf_ln_b, v_w_proj_a, v_w_proj_b, v_w_out, v_norm_ffn, v_w_ffn_gate, v_w_ffn_up, v_w_ffn_down, v_norm_final):
    l, d = x.shape[1], x.shape[2]
    lc = ctx.shape[1]
    t_all = l + lc
    di = ssm_norm.shape[-1]
    h = d_skip.shape[-1]
    p = di // h
    g, n = SSM_GROUPS, SSM_STATE
    hpg = h // g
    conv_dim = di + 2 * g * n
    df = w_ffn_down.shape[1] * N_SHARD
    assert 2 * h == LANES and d % (2 * LANES) == 0

    my_x, my_y, my_c = _mesh_pos()
    chip = 2 * my_x + my_y
    dev = 2 * chip + my_c

    x2, ctx2, tgt = x[0], ctx[0], loss_target[0]
    row = lambda a: a.reshape(1, -1)

    cw_shard, cfw_shard = ssm_conv_w[0], cf_conv_w[0]
    k5, k31 = cw_shard.shape[0], cfw_shard.shape[0]
    r5, r31 = -(-k5 // 8) * 8, -(-k31 // 8) * 8
    wp = max(d, cw_shard.shape[1], cfw_shard.shape[1])
    packed = jnp.concatenate([_pad_rows(c, 8, wp), _pad_rows(cw_shard, r5, wp), _pad_rows(cfw_shard, r31, wp)], axis=0)
    got = _allgather_small(packed, "ag_params").reshape(N_DEV, 8 + r5 + r31, wp)
    c_all = got[:, 0, :d]
    conv_w = got[0::2, 8:8 + k5, :cw_shard.shape[1]].transpose(1, 0, 2).reshape(k5, conv_dim)
    cf_w = got[0::2, 8 + r5:8 + r5 + k31, :cfw_shard.shape[1]].transpose(1, 0, 2).reshape(k31, d)

    ws = w_mod.shape[2]
    crows = jnp.concatenate([c_all, row(c_ctx), jnp.zeros((7, d), F32)], axis=0)
    b_mod_mine = lax.dynamic_slice(b_mod, (0, chip * ws), (1, ws))
    m_part, s_rows = _ada_fwd(crows, w_mod[0], b_mod_mine)
    m_full = _allgather_small(m_part, "ag_mod").reshape(N_DEV, 16, ws)[0::2].transpose(1, 0, 2).reshape(16, N_SHARD * ws)
    m_lat = lax.dynamic_slice(m_full, (dev, 0), (1, 6 * d))
    sh1, sc1, g1, sh2, sc2, g2 = [m_lat[:, i * d:(i + 1) * d] for i in range(6)]
    csh1, csc1 = m_full[8:9, 0:d], m_full[8:9, d:2 * d]

    shards = [w_in[0].T, w_ffn_gate[0].T, w_ffn_up[0].T, w_proj_a[0], w_proj_b[0], w_out[0], w_ffn_down[0]]
    shards = [s.astype(WIRE_DTYPE) for s in shards]
    (win_t,) = _fill_own_rows(_run_side(_gather_side(shards[:1]), "ag_w_in"), shards[:1])
    o_xbc, o_dt, o_glu, o_gates = di, di + conv_dim, di + conv_dim + 2 * h, di + conv_dim + 2 * h + 2 * d
    win_work = jnp.concatenate([win_t[:o_xbc], win_t[o_glu:], win_t[o_xbc:o_dt], win_t[o_dt:o_glu]], axis=0)
    c_u, c_ga, c_xbc, c_dt = di, di + 2 * d, di + 4 * d, di + 4 * d + conv_dim

    nm = norm_mix
    hx = _mod_fwd(x2, ctx2, nm, sc1, sh1, csc1, csh1)
    proj, *rest = _matmul(hx, win_work, tb=True, tm=768, tn=640, name="mm_proj", side=_gather_side(shards[1:]))
    wg_t, wu_t, wpa, wpb, wout, wdn = _fill_own_rows(rest, shards[1:])
    wgu = jnp.concatenate([wg_t, wu_t], axis=0)
    xbc = _conv5_fwd(proj, conv_w, ssm_conv_b, l, lc, c_xbc, conv_dim)
    a = -jnp.exp(a_log.reshape(1, 2 * h))
    dt, cs, tc, cs_t = _dt_fwd(proj, dt_bias.reshape(1, 2 * h), a, c_dt // LANES)
    cols = _scan_columns(dt, cs, tc, 2 * g, hpg)
    a_cols = jnp.pad(a.reshape(2 * g, 1, hpg), ((0, 0), (0, 0), (0, HEAD_COLS - hpg)))
    y_f, hp_f = _ssd_fwd(xbc, cols, cs_t, l, lc, di, p, False)
    y_r, hp_r = _ssd_fwd(xbc, cols, cs_t, l, lc, di, p, True)
    dsk = jnp.repeat(d_skip.reshape(h), p).reshape(1, di)
    ya_in = _gate_fwd(y_f, y_r, xbc, proj, dsk, ssm_norm, l, di)
    y_a = _matmul(ya_in, wpa, tk=di, name="mm_ya")
    u_blk = c_u // d
    cv = _conv31_fwd(proj, cf_w, cf_conv_b, l, d, u_blk)
    cf = _ln_fwd(cv, cf_ln_g, cf_ln_b)
    y_b = _matmul(cf, wpb, name="mm_yb")
    ga_blk = c_ga // d
    merged = _merge_fwd(y_a, y_b, proj, ga_blk)
    mix = _matmul(merged, wout, name="mm_mix")
    x1, hx2 = _res_fwd(x2, mix, g1, norm_ffn, sc2, sh2)
    gu = _matmul(hx2, wgu, tb=True, tn=_tile(df, 1024, LANES), name="mm_gu")
    act = _swiglu_fwd(gu, df)
    dn = _matmul(act, wdn, tk=df, name="mm_dn")
    loss, dx1, ddn, dg2, d_norm_final = _loss_and_grads(x1, dn, g2, row(norm_final), tgt)

    dact = _matmul(ddn, wdn, tb=True, tn=_tile(df, 1024, LANES), name="mm_dact")
    dw_dn = _matmul(act, ddn, ta=True, tn=d, tk=1024, name="mm_dw_dn")
    dgu = _swiglu_bwd(gu, dact, df)
    dhx2 = _matmul(dgu, wgu, tk=df, name="mm_dhx2")
    dw_gu = _matmul(dgu, hx2, ta=True, tn=d, tk=1024, name="mm_dw_gu")
    dx_res, dmix, dg1, d_norm_ffn, dsc2, dsh2 = _res_bwd(x2, mix, g1, norm_ffn, sc2, sh2, dx1, dhx2)
    dmerged = _matmul(dmix, wout, tb=True, name="mm_dmerged")
    dw_out = _matmul(merged, dmix, ta=True, tn=d, tk=1024, name="mm_dw_out")
    dya, dyb, dga, dgb = _merge_bwd(y_a, y_b, proj, ga_blk, dmerged, lc)
    dcf = _matmul(dyb, wpb, tb=True, name="mm_dcf")
    dw_pb = _matmul(cf, dyb, ta=True, tn=d, tk=1024, name="mm_dw_pb")
    dcv, d_ln_g, d_ln_b = _ln_bwd(cv, cf_ln_g, cf_ln_b, dcf)
    du, dv, d_cf_w, d_cf_b = _conv31_bwd(proj, cf_w, dcv, l, lc, d, u_blk)
    dya_in = _matmul(dya, wpa, tb=True, tn=_tile(di, 1024, LANES), name="mm_dya_in")
    dw_pa = _matmul(ya_in, dya, ta=True, tn=d, tk=1024, name="mm_dw_pa")
    dy, dz, ddsk, d_ssm_norm = _gate_bwd(y_f, y_r, xbc, proj, dsk, ssm_norm, dya_in, l, lc, di)
    dxs_f, db_f, dc_f, ddt_f, dda_f = _ssd_bwd(xbc, cols, cs_t, a_cols, dy, hp_f, l, lc, di, p, False, dsk=dsk)
    dxs, db, dc, ddt_r, dda_r = _ssd_bwd(xbc, cols, cs_t, a_cols, dy, hp_r, l, lc, di, p, True, prev=(dxs_f, db_f, dc_f))
    dxs_raw, dcw_x, dcb_x = _conv5_bwd(proj, conv_w, ssm_conv_b, [dxs], l, lc, c_xbc, 0, di)
    db_raw, dcw_b, dcb_b = _conv5_bwd(proj, conv_w, ssm_conv_b, [db], l, lc, c_xbc, di, g * n)
    dc_raw, dcw_c, dcb_c = _conv5_bwd(proj, conv_w, ssm_conv_b, [dc], l, lc, c_xbc, di + g * n, g * n)
    d_conv_w = jnp.concatenate([dcw_x, dcw_b, dcw_c], axis=1)
    d_conv_b = jnp.concatenate([dcb_x, dcb_b, dcb_c], axis=1)
    heads = lambda f, r: jnp.concatenate([t[:, :, :hpg].transpose(1, 0, 2).reshape(t_all, h) for t in (f, r)], axis=1)
    ddt_raw, d_dt_bias, dda_dt = _dt_bwd(proj, dt_bias.reshape(1, 2 * h), dt, heads(ddt_f, ddt_r), heads(dda_f, dda_r), c_dt // LANES)
    d_a_log = dda_dt * a
    dproj = jnp.concatenate([dz, du, dv, dga, dgb, dxs_raw, db_raw, dc_raw, ddt_raw], axis=1)
    wire, own = _reduce_scatter_begin([dw_gu[:df], dw_gu[df:], dw_pa, dw_pb, dw_out, dw_dn], "a")
    dhx, *recv = _matmul(dproj, win_work, tm=768, tn=1024, tk=_tile(win_work.shape[0], 4096, LANES), name="mm_dhx",
                         side=_scatter_side(wire))
    g_gate_t, g_up_t, g_pa, g_pb, g_out, g_dn = _reduce_scatter_end(own, recv, "a")
    dw_in_work = _matmul(dproj, hx, ta=True, tm=640, tn=d, tk=768, name="mm_dw_in")
    grad_x, d_norm_mix, dsc1, dsh1, dcsc1, dcsh1 = _mod_bwd(x2, ctx2, nm, sc1, sh1, csc1, csh1, dhx, dx_res)

    dw_in_t = jnp.concatenate([dw_in_work[:c_u], dw_in_work[c_xbc:], dw_in_work[c_u:c_xbc]], axis=0)
    wire, own = _reduce_scatter_begin([dw_in_t], "b")
    (g_in_t,) = _reduce_scatter_end(own, _run_side(_scatter_side(wire), "rs_scatter_b"), "b")
    g_in, g_gate, g_up = g_in_t.T, g_gate_t.T, g_up_t.T

    zeros_d = jnp.zeros((1, d), F32)
    dm_lat = jnp.concatenate([dsh1, dsc1, dg1, dsh2, dsc2, dg2], axis=1)
    dm_ctx = jnp.concatenate([dcsh1, dcsc1] + [zeros_d] * 4, axis=1)
    d_d_skip = ddsk.reshape(h, p).sum(axis=1)
    replicated = [dm_lat + dm_ctx, d_norm_mix, d_conv_b, d_dt_bias, d_a_log, d_d_skip, d_ssm_norm, d_cf_b, d_ln_g, d_ln_b,
                  d_norm_ffn, d_norm_final]
    rep_w = [b_mod, norm_mix, ssm_conv_b, dt_bias, a_log, d_skip, ssm_norm, cf_conv_b, cf_ln_g, cf_ln_b, norm_ffn, norm_final]
    rep_m = [m_b_mod, m_norm_mix, m_ssm_conv_b, m_dt_bias, m_a_log, m_d_skip, m_ssm_norm, m_cf_conv_b, m_cf_ln_g, m_cf_ln_b,
             m_norm_ffn, m_norm_final]
    rep_v = [v_b_mod, v_norm_mix, v_ssm_conv_b, v_dt_bias, v_a_log, v_d_skip, v_ssm_norm, v_cf_conv_b, v_cf_ln_g, v_cf_ln_b,
             v_norm_ffn, v_norm_final]
    quantum = 8 * LANES
    rep_flat = _pack(replicated, quantum)
    n_rep = rep_flat.shape[0]
    summed_parts = [rep_flat, _pack([d_conv_w, d_cf_w, dm_ctx], quantum)]
    n_sum = n_rep + summed_parts[1].shape[0]
    everything = jnp.concatenate(summed_parts + [_pack([dm_lat], quantum)])
    gathered = _allgather_small(everything.reshape(8, -1), "ag_small_grads")
    w8 = gathered.shape[1]
    summed = _sum_devices(gathered).reshape(-1)
    dm_lat_all = gathered.reshape(N_DEV, 8 * w8)[:, n_sum:n_sum + 6 * d]
    off = n_rep
    g_conv_w_full = summed[off:off + k5 * conv_dim].reshape(k5, conv_dim)
    off += k5 * conv_dim
    g_cf_w_full = summed[off:off + k31 * d].reshape(k31, d)
    off += k31 * d
    dm_ctx_all = summed[off:off + 6 * d].reshape(1, 6 * d)
    g_conv_w = lax.dynamic_slice(g_conv_w_full, (0, chip * cw_shard.shape[1]), cw_shard.shape)
    g_cf_w = lax.dynamic_slice(g_cf_w_full, (0, chip * cfw_shard.shape[1]), cfw_shard.shape)

    dm_rows = jnp.concatenate([dm_lat_all, dm_ctx_all, jnp.zeros((7, 6 * d), F32)], axis=0)
    dm_mine = lax.dynamic_slice(dm_rows, (0, chip * ws), (16, ws))
    g_w_mod, ds_part = _ada_bwd(s_rows.T, w_mod[0], dm_mine)
    ds_all = _allgather_small(ds_part[8:16], "ag_c_ctx")
    g_c_ctx = _c_ctx_grad(ds_all, row(c_ctx))

    grads, deltas, new_ms, new_vs = {}, {}, {}, {}

    def update(name, w2, g2, m2, v2, shape):
        dl, mm, vv = _adamw(w2, g2, m2, v2, f"adamw_{name}")
        grads[name], deltas[name], new_ms[name], new_vs[name] = (t.reshape(shape) for t in (g2, dl, mm, vv))

    for name, w_, g_, m_, v_ in [
            ("w_mod", w_mod, g_w_mod, m_w_mod, v_w_mod), ("w_in", w_in, g_in, m_w_in, v_w_in),
            ("ssm_conv_w", ssm_conv_w, g_conv_w, m_ssm_conv_w, v_ssm_conv_w),
            ("cf_conv_w", cf_conv_w, g_cf_w, m_cf_conv_w, v_cf_conv_w),
            ("w_proj_a", w_proj_a, g_pa, m_w_proj_a, v_w_proj_a), ("w_proj_b", w_proj_b, g_pb, m_w_proj_b, v_w_proj_b),
            ("w_out", w_out, g_out, m_w_out, v_w_out), ("w_ffn_gate", w_ffn_gate, g_gate, m_w_ffn_gate, v_w_ffn_gate),
            ("w_ffn_up", w_ffn_up, g_up, m_w_ffn_up, v_w_ffn_up), ("w_ffn_down", w_ffn_down, g_dn, m_w_ffn_down, v_w_ffn_down)]:
        update(name, w_[0], g_, m_[0], v_[0], w_.shape)
    update("c_ctx", row(c_ctx), g_c_ctx, row(m_c_ctx), row(v_c_ctx), c_ctx.shape)

    rep_names = ["b_mod", "norm_mix", "ssm_conv_b", "dt_bias", "a_log", "d_skip", "ssm_norm", "cf_conv_b", "cf_ln_g", "cf_ln_b",
                 "norm_ffn", "norm_final"]
    as8 = lambda vs: _pack(vs, quantum).reshape(8, -1)
    g8 = summed[:n_rep].reshape(8, -1)
    d8, m8, v8 = _adamw(as8(rep_w), g8, as8(rep_m), as8(rep_v), "adamw_replicated")
    off = 0
    for name, w_ in zip(rep_names, rep_w):
        size = w_.size
        for store, packed8 in ((grads, g8), (deltas, d8), (new_ms, m8), (new_vs, v8)):
            store[name] = packed8.reshape(-1)[off:off + size].reshape(w_.shape)
        off += size

    order = ["c_ctx", "w_mod", "b_mod", "norm_mix", "w_in", "ssm_conv_w", "ssm_conv_b", "dt_bias", "a_log", "d_skip", "ssm_norm",
             "cf_conv_w", "cf_conv_b", "cf_ln_g", "cf_ln_b", "w_proj_a", "w_proj_b", "w_out", "norm_ffn", "w_ffn_gate", "w_ffn_up",
             "w_ffn_down", "norm_final"]
    total_loss = lax.psum(loss[0, 0], ("x", "y", "c"))
    return (total_loss, grad_x.reshape(x.shape), *[grads[k] for k in order], *[deltas[k] for k in order],
            *[new_ms[k] for k in order], *[new_vs[k] for k in order])
```

```python
import functools

import jax
import jax.numpy as jnp
from jax import lax
from jax.experimental import pallas as pl
from jax.experimental.pallas import tpu as pltpu

F32 = jnp.float32
MXU_DTYPE = jnp.bfloat16
WIRE_DTYPE = jnp.bfloat16
HI = lax.Precision.HIGHEST
EPS = 1e-6
SSM_GROUPS = 8
SSM_STATE = 128
CHUNK = 128
GRID_W = 64
LANES = 128
HEAD_COLS = 16
VMEM_LIMIT = 52 * 1024 * 1024
ADAM_LR, ADAM_B1, ADAM_B2, ADAM_EPS, ADAM_WD, ADAM_STEP = 0.001, 0.9, 0.999, 1e-08, 0.01, 10
MESH = pl.DeviceIdType.MESH
N_SHARD = 4
N_DEV = 8


def _cp(sem=None):
    kw = dict(vmem_limit_bytes=VMEM_LIMIT)
    if sem is not None:
        kw["dimension_semantics"] = sem
    return pltpu.CompilerParams(**kw)


def _tile(n, target, q):
    best = None
    for t in range(q, min(n, target) + 1, q):
        if n % t == 0:
            best = t
    return best if best is not None else n


def _acc(ref, val, i):
    @pl.when(i == 0)
    def _():
        ref[...] = val

    @pl.when(i > 0)
    def _():
        ref[...] += val


def _bc_spec(w):
    return pl.BlockSpec((1, w), lambda *_: (0, 0))


def _rms(x, w):
    return x * lax.rsqrt(jnp.mean(x * x, axis=-1, keepdims=True) + EPS) * w


def _silu(x):
    return x * jax.nn.sigmoid(x)


def _f_mod(x, w, sc, sh):
    return _rms(x, w) * (1.0 + sc) + sh


def _f_gate(yf, yr, xs, z, dsk, wn):
    return _rms((yf + yr + dsk * xs) * _silu(z), wn)


def _f_ln(cv, g, b):
    mu = jnp.mean(cv, axis=-1, keepdims=True)
    xc = cv - mu
    var = jnp.mean(xc * xc, axis=-1, keepdims=True)
    return _silu(xc * lax.rsqrt(var + EPS) * g + b)


def _f_merge(ya, yb, ga, gb):
    return jax.nn.sigmoid(ga) * ya + jax.nn.sigmoid(gb) * yb


def _f_res(x, mix, g1, wn, sc2, sh2):
    x1 = x + g1 * mix
    return x1, _rms(x1, wn) * (1.0 + sc2) + sh2


def _f_swiglu(gt, up):
    return _silu(gt) * up


def _f_loss(x1, dn, g2, wn, tgt):
    out = _rms(x1 + g2 * dn, wn)
    err = out - tgt
    per_tok = jnp.mean(err * err, axis=-1, keepdims=True)
    return 0.5 * jnp.sum(per_tok, axis=0, keepdims=True)


def _matmul(a, b, *, ta=False, tb=False, out_dtype=F32, tm=512, tn=512, tk=2048, name, side=None, n_outer=False):
    m, k = (a.shape[1], a.shape[0]) if ta else a.shape
    n = b.shape[0] if tb else b.shape[1]
    assert (b.shape[1] if tb else b.shape[0]) == k, (a.shape, b.shape, ta, tb)
    tm, tn, tk = _tile(m, tm, LANES if ta else 16), _tile(n, tn, LANES), _tile(k, tk, LANES)
    grid = (n // tn, m // tm, k // tk) if n_outer else (m // tm, n // tn, k // tk)
    ij = (lambda g0, g1: (g1, g0)) if n_outer else (lambda g0, g1: (g0, g1))
    nk = grid[2]
    dims = (((0 if ta else 1,), (1 if tb else 0,)), ((), ()))
    n_in = len(side.inputs) if side else 0
    n_out = len(side.out_shapes) if side else 0

    def body(a_ref, b_ref, *rest):
        side_in, o_ref, side_out, scratch = rest[:n_in], rest[n_in], rest[n_in + 1:n_in + 1 + n_out], rest[n_in + 1 + n_out:]
        steps = [pl.program_id(axis) for axis in range(3)]
        if side:
            @pl.when((steps[0] == 0) & (steps[1] == 0) & (steps[2] == 0))
            def _():
                side.start(side_in, side_out, *scratch[-2:])

        prod = lax.dot_general(a_ref[...].astype(MXU_DTYPE), b_ref[...].astype(MXU_DTYPE), dims,
                               preferred_element_type=F32)
        if nk == 1:
            o_ref[...] = prod.astype(o_ref.dtype)
        else:
            acc = scratch[0]
            _acc(acc, prod, steps[2])

            @pl.when(steps[2] == nk - 1)
            def _():
                o_ref[...] = acc[...].astype(o_ref.dtype)

        if side:
            @pl.when((steps[0] == grid[0] - 1) & (steps[1] == grid[1] - 1) & (steps[2] == nk - 1))
            def _():
                side.finish(side_in, side_out, *scratch[-2:])

    a_spec = (pl.BlockSpec((tk, tm), lambda g0, g1, kk: (kk, ij(g0, g1)[0])) if ta
              else pl.BlockSpec((tm, tk), lambda g0, g1, kk: (ij(g0, g1)[0], kk)))
    b_spec = (pl.BlockSpec((tn, tk), lambda g0, g1, kk: (ij(g0, g1)[1], kk)) if tb
              else pl.BlockSpec((tk, tn), lambda g0, g1, kk: (kk, ij(g0, g1)[1])))
    out = pl.pallas_call(
        body, name=name, grid=grid, in_specs=[a_spec, b_spec] + [_HBM] * n_in,
        out_specs=[pl.BlockSpec((tm, tn), lambda g0, g1, kk: ij(g0, g1))] + [_HBM] * n_out,
        out_shape=[jax.ShapeDtypeStruct((m, n), out_dtype)] + (side.out_shapes if side else []),
        scratch_shapes=([] if nk == 1 else [pltpu.VMEM((tm, tn), F32)]) + (side.scratch() if side else []),
        compiler_params=_cp(("arbitrary",) * 3 if side else ("parallel", "parallel", "arbitrary")),
    )(a, b, *(side.inputs if side else []))
    return out if side else out[0]


def _mesh_pos():
    return lax.axis_index("x"), lax.axis_index("y"), lax.axis_index("c")


def _other_chips(x, y):
    return [(1 - x, y), (x, 1 - y), (1 - x, 1 - y)]


def _allgather_small(v, name):
    m_per, n = v.shape

    def body(x_ref, out_ref, send_sems, recv_sems, local_sem):
        x, y, c = _mesh_pos()
        me, sibling = (x, y, c), (x, y, 1 - c)
        chips = _other_chips(x, y)

        def rows(px, py, pc):
            return out_ref.at[pl.ds((4 * px + 2 * py + pc) * m_per, m_per), :]

        def copy(k, block, to, src=None):
            return pltpu.make_async_remote_copy(
                src_ref=rows(*block) if src is None else src, dst_ref=rows(*block),
                send_sem=send_sems.at[k], recv_sem=recv_sems.at[k], device_id=to, device_id_type=MESH)

        mine = pltpu.make_async_copy(x_ref, rows(*me), local_sem)
        mine.start()
        first = [copy(0, me, sibling, src=x_ref)]
        first += [copy(1 + j, me, (*chip, c), src=x_ref) for j, chip in enumerate(chips)]
        for cp in first:
            cp.start()
        passed = [copy(4 + j, (*chip, c), sibling) for j, chip in enumerate(chips)]
        for j, chip in enumerate(chips):
            copy(1 + j, (*chip, c), me).wait_recv()
            passed[j].start()
        copy(0, sibling, me).wait_recv()
        for j, chip in enumerate(chips):
            copy(4 + j, (*chip, 1 - c), me).wait_recv()
        for cp in first + passed:
            cp.wait_send()
        mine.wait()

    return pl.pallas_call(
        body, name=name, out_shape=jax.ShapeDtypeStruct((N_DEV * m_per, n), v.dtype),
        in_specs=[pl.BlockSpec(memory_space=pltpu.VMEM)], out_specs=pl.BlockSpec(memory_space=pltpu.VMEM),
        scratch_shapes=[pltpu.SemaphoreType.DMA((7,)), pltpu.SemaphoreType.DMA((7,)), pltpu.SemaphoreType.DMA],
        compiler_params=_cp(),
    )(v)


_HBM = pl.BlockSpec(memory_space=pltpu.HBM)


class _Side:
    def __init__(self, inputs, out_shapes, n_sems, start, finish):
        self.inputs, self.out_shapes, self.n_sems, self.start, self.finish = inputs, out_shapes, n_sems, start, finish

    def scratch(self):
        return [pltpu.SemaphoreType.DMA((self.n_sems,)), pltpu.SemaphoreType.DMA((self.n_sems,))]


def _run_side(side, name):
    n_in = len(side.inputs)

    def body(*refs):
        src, dst, sems = refs[:n_in], refs[n_in:-2], refs[-2:]
        side.start(src, dst, *sems)
        side.finish(src, dst, *sems)

    return pl.pallas_call(
        body, name=name, out_shape=side.out_shapes, in_specs=[_HBM] * n_in, out_specs=[_HBM] * len(side.out_shapes),
        scratch_shapes=side.scratch(), compiler_params=_cp(),
    )(*side.inputs)


def _gather_side(shards):
    n = len(shards)

    def plan(src, dst, send_sems, recv_sems):
        x, y, c = _mesh_pos()
        chips = _other_chips(x, y)

        def half(i, px, py, pc):
            r = src[i].shape[0]
            return dst[i].at[pl.ds(pl.multiple_of((2 * px + py) * r + pc * (r // 2), 16), r // 2), :]

        def copy(i, k, block, to, own=False):
            r = src[i].shape[0]
            mine = src[i].at[pl.ds(pl.multiple_of(c * (r // 2), 16), r // 2), :]
            return pltpu.make_async_remote_copy(
                src_ref=mine if own else half(i, *block), dst_ref=half(i, *block), send_sem=send_sems.at[6 * i + k],
                recv_sem=recv_sems.at[6 * i + k], device_id=to, device_id_type=MESH)

        first = [copy(i, j, (x, y, c), (*chip, c), own=True) for i in range(n) for j, chip in enumerate(chips)]
        return (x, y, c), chips, copy, first

    def start(src, dst, send_sems, recv_sems):
        for cp in plan(src, dst, send_sems, recv_sems)[3]:
            cp.start()

    def finish(src, dst, send_sems, recv_sems):
        (x, y, c), chips, copy, first = plan(src, dst, send_sems, recv_sems)
        passed = []
        for i in range(n):
            for j, chip in enumerate(chips):
                copy(i, j, (*chip, c), (x, y, c)).wait_recv()
                passed.append(copy(i, 3 + j, (*chip, c), (x, y, 1 - c)))
                passed[-1].start()
        for i in range(n):
            for j, chip in enumerate(chips):
                copy(i, 3 + j, (*chip, 1 - c), (x, y, c)).wait_recv()
        for cp in first + passed:
            cp.wait_send()

    shapes = [jax.ShapeDtypeStruct((N_SHARD * s.shape[0], s.shape[1]), s.dtype) for s in shards]
    return _Side(list(shards), shapes, 6 * n, start, finish)


def _fill_own_rows(gathered, shards):
    chip = 2 * lax.axis_index("x") + lax.axis_index("y")
    return [lax.dynamic_update_slice(full, s, (chip * s.shape[0], 0)) for full, s in zip(gathered, shards)]


def _swap_halves(parts, name):
    n = len(parts)

    def body(*refs):
        src, dst = refs[:n], refs[n:2 * n]
        send_sems, recv_sems = refs[2 * n:]
        x, y, c = _mesh_pos()
        copies = [pltpu.make_async_remote_copy(
            src_ref=src[i].at[s, 1 - c], dst_ref=dst[i].at[s], send_sem=send_sems.at[N_SHARD * i + s],
            recv_sem=recv_sems.at[N_SHARD * i + s], device_id=(x, y, 1 - c), device_id_type=MESH)
            for i in range(n) for s in range(N_SHARD)]
        for cp in copies:
            cp.start()
        for cp in copies:
            cp.wait()

    return pl.pallas_call(
        body, name=name,
        out_shape=[jax.ShapeDtypeStruct((N_SHARD,) + p.shape[2:], p.dtype) for p in parts],
        in_specs=[_HBM] * n, out_specs=[_HBM] * n,
        scratch_shapes=[pltpu.SemaphoreType.DMA((N_SHARD * n,)), pltpu.SemaphoreType.DMA((N_SHARD * n,))],
        compiler_params=_cp(),
    )(*parts)


def _scatter_side(parts):
    n = len(parts)

    def copies(src, dst, send_sems, recv_sems):
        x, y, c = _mesh_pos()
        return [pltpu.make_async_remote_copy(
            src_ref=src[i].at[2 * chip[0] + chip[1]], dst_ref=dst[i].at[j], send_sem=send_sems.at[3 * i + j],
            recv_sem=recv_sems.at[3 * i + j], device_id=(*chip, c), device_id_type=MESH)
            for i in range(n) for j, chip in enumerate(_other_chips(x, y))]

    def start(*refs):
        for cp in copies(*refs):
            cp.start()

    def finish(*refs):
        for cp in copies(*refs):
            cp.wait()

    shapes = [jax.ShapeDtypeStruct((3,) + p.shape[1:], p.dtype) for p in parts]
    return _Side(list(parts), shapes, 3 * n, start, finish)


def _join_halves(halves, name):
    n = len(halves)

    def body(*refs):
        src, dst = refs[:n], refs[n:2 * n]
        send_sems, recv_sems = refs[2 * n:]
        x, y, c = _mesh_pos()
        remote = [pltpu.make_async_remote_copy(
            src_ref=src[i], dst_ref=dst[i].at[c], send_sem=send_sems.at[i], recv_sem=recv_sems.at[i],
            device_id=(x, y, 1 - c), device_id_type=MESH) for i in range(n)]
        for cp in remote:
            cp.start()
        for i in range(n):
            pltpu.make_async_remote_copy(
                src_ref=src[i], dst_ref=dst[i].at[1 - c], send_sem=send_sems.at[i], recv_sem=recv_sems.at[i],
                device_id=(x, y, 1 - c), device_id_type=MESH).wait_recv()
        for cp in remote:
            cp.wait_send()

    joined = pl.pallas_call(
        body, name=name,
        out_shape=[jax.ShapeDtypeStruct((2,) + h.shape, h.dtype) for h in halves],
        in_specs=[_HBM] * n, out_specs=[_HBM] * n,
        scratch_shapes=[pltpu.SemaphoreType.DMA((n,)), pltpu.SemaphoreType.DMA((n,))],
        compiler_params=_cp(),
    )(*halves)
    c = lax.axis_index("c")
    return [lax.dynamic_update_slice(j, h[None], (c, 0, 0)) for j, h in zip(joined, halves)]


def _pair_sum(g, got, name):
    _, _, hr, d = g.shape
    t = _tile(hr, 256, 16)

    def body(g0_ref, g1_ref, got_ref, wire_ref, own_ref):
        x, y, c = _mesh_pos()
        total = jnp.where(c == 0, g0_ref[0, 0], g1_ref[0, 0]) + got_ref[0]
        wire_ref[0] = total.astype(wire_ref.dtype)

        @pl.when(pl.program_id(1) == 2 * x + y)
        def _():
            own_ref[...] = total

    return pl.pallas_call(
        body, name=name, grid=(hr // t, N_SHARD),
        in_specs=[pl.BlockSpec((1, 1, t, d), lambda i, s: (s, 0, i, 0)), pl.BlockSpec((1, 1, t, d), lambda i, s: (s, 1, i, 0)),
                  pl.BlockSpec((1, t, d), lambda i, s: (s, i, 0))],
        out_specs=[pl.BlockSpec((1, t, d), lambda i, s: (s, i, 0)), pl.BlockSpec((t, d), lambda i, s: (i, 0))],
        out_shape=[jax.ShapeDtypeStruct((N_SHARD, hr, d), WIRE_DTYPE), jax.ShapeDtypeStruct((hr, d), F32)],
        compiler_params=_cp(("parallel", "arbitrary")),
    )(g, g, got)


def _sum_partials(own, recv, name):
    hr, d = own.shape
    t = _tile(hr, 256, 16)

    def body(own_ref, recv_ref, o_ref):
        total = own_ref[...]
        for j in range(3):
            total = total + recv_ref[j].astype(F32)
        o_ref[...] = total

    blk = pl.BlockSpec((t, d), lambda i: (i, 0))
    return pl.pallas_call(
        body, name=name, grid=(hr // t,), in_specs=[blk, pl.BlockSpec((3, t, d), lambda i: (0, i, 0))], out_specs=blk,
        out_shape=jax.ShapeDtypeStruct((hr, d), F32), compiler_params=_cp(("parallel",)),
    )(own, recv)


def _reduce_scatter_begin(grads, tag):
    split = [g.reshape(N_SHARD, 2, g.shape[0] // (2 * N_SHARD), g.shape[1]) for g in grads]
    got = _swap_halves(split, f"rs_swap_halves_{tag}")
    sums = [_pair_sum(g, h, f"rs_pair_sum_{tag}{i}") for i, (g, h) in enumerate(zip(split, got))]
    return [w for w, _ in sums], [own for _, own in sums]


def _reduce_scatter_end(own, recv, tag):
    halves = [_sum_partials(o, rv, f"rs_sum_{tag}{i}") for i, (o, rv) in enumerate(zip(own, recv))]
    return [j.reshape(-1, j.shape[-1]) for j in _join_halves(halves, f"rs_join_halves_{tag}")]


def _mod_fwd(x, ctx, nw, sc, sh, csc, csh):
    l, d = x.shape
    lc = ctx.shape[0]
    t = min(256, lc)
    nl, nc = l // t, lc // t

    def body(x_ref, c_ref, nw_ref, sc_ref, sh_ref, csc_ref, csh_ref, o_ref):
        i = pl.program_id(0)

        @pl.when(i < nl)
        def _():
            o_ref[...] = _f_mod(x_ref[...], nw_ref[...], sc_ref[...], sh_ref[...]).astype(o_ref.dtype)

        @pl.when(i >= nl)
        def _():
            o_ref[...] = _f_mod(c_ref[...], nw_ref[...], csc_ref[...], csh_ref[...]).astype(o_ref.dtype)

    return pl.pallas_call(
        body, name="mod_fwd", grid=(nl + nc,),
        in_specs=[pl.BlockSpec((t, d), lambda i: (jnp.minimum(i, nl - 1), 0)),
                  pl.BlockSpec((t, d), lambda i: (jnp.maximum(i - nl, 0), 0))] + [_bc_spec(d)] * 5,
        out_specs=pl.BlockSpec((t, d), lambda i: (i, 0)),
        out_shape=jax.ShapeDtypeStruct((l + lc, d), MXU_DTYPE), compiler_params=_cp(("arbitrary",)),
    )(x, ctx, nw, sc, sh, csc, csh)


def _mod_bwd(x, ctx, nw, sc, sh, csc, csh, dhx, dx_res):
    l, d = x.shape
    lc = ctx.shape[0]
    t = min(256, lc)
    nl, nc = l // t, lc // t

    def body(x_ref, c_ref, nw_ref, sc_ref, sh_ref, csc_ref, csh_ref, dh_ref, dr_ref,
             dx_ref, dnw_ref, dsc_ref, dsh_ref, dcsc_ref, dcsh_ref):
        i = pl.program_id(0)

        @pl.when(i == 0)
        def _():
            for r in (dnw_ref, dsc_ref, dsh_ref, dcsc_ref, dcsh_ref):
                r[...] = jnp.zeros_like(r)

        @pl.when(i < nl)
        def _():
            _, vjp = jax.vjp(_f_mod, x_ref[...], nw_ref[...], sc_ref[...], sh_ref[...])
            dx, dnw, dsc, dsh = vjp(dh_ref[...])
            dx_ref[...] = dx + dr_ref[...]
            dnw_ref[...] += dnw
            dsc_ref[...] += dsc
            dsh_ref[...] += dsh

        @pl.when(i >= nl)
        def _():
            _, vjp = jax.vjp(_f_mod, c_ref[...], nw_ref[...], csc_ref[...], csh_ref[...])
            _, dnw, dsc, dsh = vjp(dh_ref[...])
            dnw_ref[...] += dnw
            dcsc_ref[...] += dsc
            dcsh_ref[...] += dsh

    lat = pl.BlockSpec((t, d), lambda i: (jnp.minimum(i, nl - 1), 0))
    vec = jax.ShapeDtypeStruct((1, d), F32)
    return pl.pallas_call(
        body, name="mod_bwd", grid=(nl + nc,),
        in_specs=[lat, pl.BlockSpec((t, d), lambda i: (jnp.maximum(i - nl, 0), 0))] + [_bc_spec(d)] * 5
        + [pl.BlockSpec((t, d), lambda i: (i, 0)), lat],
        out_specs=[lat] + [_bc_spec(d)] * 5,
        out_shape=[jax.ShapeDtypeStruct((l, d), F32)] + [vec] * 5, compiler_params=_cp(("arbitrary",)),
    )(x, ctx, nw, sc, sh, csc, csh, dhx, dx_res)


def _gate_fwd(yf, yr, xbc, proj, dsk, wn, l, di):
    t = 128

    def body(yf_ref, yr_ref, xs_ref, z_ref, dsk_ref, wn_ref, o_ref):
        o_ref[...] = _f_gate(yf_ref[...], yr_ref[...], xs_ref[...], z_ref[...], dsk_ref[...], wn_ref[...]).astype(o_ref.dtype)

    row = pl.BlockSpec((t, di), lambda i: (i, 0))
    return pl.pallas_call(
        body, name="gate_fwd", grid=(l // t,), in_specs=[row] * 4 + [_bc_spec(di)] * 2, out_specs=row,
        out_shape=jax.ShapeDtypeStruct((l, di), MXU_DTYPE), compiler_params=_cp(("parallel",)),
    )(yf, yr, xbc, proj, dsk, wn)


def _gate_bwd(yf, yr, xbc, proj, dsk, wn, dya, l, lc, di):
    t = 128
    nl, nc = l // t, lc // t

    def body(yf_ref, yr_ref, xs_ref, z_ref, dsk_ref, wn_ref, g_ref, dy_ref, dz_ref, ddsk_ref, dwn_ref):
        i = pl.program_id(0)

        @pl.when(i == 0)
        def _():
            ddsk_ref[...] = jnp.zeros_like(ddsk_ref)
            dwn_ref[...] = jnp.zeros_like(dwn_ref)

        @pl.when(i < nl)
        def _():
            _, vjp = jax.vjp(_f_gate, yf_ref[...], yr_ref[...], xs_ref[...], z_ref[...], dsk_ref[...], wn_ref[...])
            dyf, _, _, dz, ddsk, dwn = vjp(g_ref[...])
            dy_ref[...] = dyf
            dz_ref[...] = dz.astype(dz_ref.dtype)
            ddsk_ref[...] += ddsk
            dwn_ref[...] += dwn

        @pl.when(i >= nl)
        def _():
            dz_ref[...] = jnp.zeros_like(dz_ref)

    lat = pl.BlockSpec((t, di), lambda i: (jnp.minimum(i, nl - 1), 0))
    vec = jax.ShapeDtypeStruct((1, di), F32)
    return pl.pallas_call(
        body, name="gate_bwd", grid=(nl + nc,),
        in_specs=[lat] * 4 + [_bc_spec(di)] * 2 + [lat],
        out_specs=[lat, pl.BlockSpec((t, di), lambda i: (i, 0))] + [_bc_spec(di)] * 2,
        out_shape=[jax.ShapeDtypeStruct((l, di), F32), jax.ShapeDtypeStruct((l + lc, di), MXU_DTYPE)] + [vec] * 2,
        compiler_params=_cp(("arbitrary",)),
    )(yf, yr, xbc, proj, dsk, wn, dya)


def _ln_fwd(cv, g, b):
    l, d = cv.shape
    t = 256

    def body(cv_ref, g_ref, b_ref, o_ref):
        o_ref[...] = _f_ln(cv_ref[...], g_ref[...], b_ref[...]).astype(o_ref.dtype)

    row = pl.BlockSpec((t, d), lambda i: (i, 0))
    return pl.pallas_call(
        body, name="ln_fwd", grid=(l // t,), in_specs=[row] + [_bc_spec(d)] * 2, out_specs=row,
        out_shape=jax.ShapeDtypeStruct((l, d), MXU_DTYPE), compiler_params=_cp(("parallel",)),
    )(cv, g, b)


def _ln_bwd(cv, g, b, dcf):
    l, d = cv.shape
    t = 256

    def body(cv_ref, g_ref, b_ref, dcf_ref, dcv_ref, dg_ref, db_ref):
        _, vjp = jax.vjp(_f_ln, cv_ref[...], g_ref[...], b_ref[...])
        dcv, dg, db = vjp(dcf_ref[...])
        dcv_ref[...] = dcv
        i = pl.program_id(0)
        _acc(dg_ref, dg, i)
        _acc(db_ref, db, i)

    row = pl.BlockSpec((t, d), lambda i: (i, 0))
    vec = jax.ShapeDtypeStruct((1, d), F32)
    return pl.pallas_call(
        body, name="ln_bwd", grid=(l // t,), in_specs=[row] + [_bc_spec(d)] * 2 + [row],
        out_specs=[row] + [_bc_spec(d)] * 2, out_shape=[jax.ShapeDtypeStruct((l, d), F32), vec, vec],
        compiler_params=_cp(("arbitrary",)),
    )(cv, g, b, dcf)


def _merge_fwd(ya, yb, proj, ga_blk):
    l, d = ya.shape
    t = 256

    def body(ya_ref, yb_ref, ga_ref, gb_ref, o_ref):
        o_ref[...] = _f_merge(ya_ref[...], yb_ref[...], ga_ref[...], gb_ref[...]).astype(o_ref.dtype)

    row = pl.BlockSpec((t, d), lambda i: (i, 0))
    return pl.pallas_call(
        body, name="merge_fwd", grid=(l // t,),
        in_specs=[row, row, pl.BlockSpec((t, d), lambda i: (i, ga_blk)), pl.BlockSpec((t, d), lambda i: (i, ga_blk + 1))],
        out_specs=row, out_shape=jax.ShapeDtypeStruct((l, d), MXU_DTYPE), compiler_params=_cp(("parallel",)),
    )(ya, yb, proj, proj)


def _merge_bwd(ya, yb, proj, ga_blk, dmerged, lc):
    l, d = ya.shape
    t = min(256, lc)
    nl, nc = l // t, lc // t

    def body(ya_ref, yb_ref, ga_ref, gb_ref, g_ref, dya_ref, dyb_ref, dga_ref, dgb_ref):
        i = pl.program_id(0)

        @pl.when(i < nl)
        def _():
            _, vjp = jax.vjp(_f_merge, ya_ref[...], yb_ref[...], ga_ref[...], gb_ref[...])
            dya, dyb, dga, dgb = vjp(g_ref[...])
            dya_ref[...] = dya.astype(dya_ref.dtype)
            dyb_ref[...] = dyb.astype(dyb_ref.dtype)
            dga_ref[...] = dga.astype(dga_ref.dtype)
            dgb_ref[...] = dgb.astype(dgb_ref.dtype)

        @pl.when(i >= nl)
        def _():
            dga_ref[...] = jnp.zeros_like(dga_ref)
            dgb_ref[...] = jnp.zeros_like(dgb_ref)

    lat = pl.BlockSpec((t, d), lambda i: (jnp.minimum(i, nl - 1), 0))
    full = pl.BlockSpec((t, d), lambda i: (i, 0))
    return pl.pallas_call(
        body, name="merge_bwd", grid=(nl + nc,),
        in_specs=[lat, lat, pl.BlockSpec((t, d), lambda i: (jnp.minimum(i, nl - 1), ga_blk)),
                  pl.BlockSpec((t, d), lambda i: (jnp.minimum(i, nl - 1), ga_blk + 1)), lat],
        out_specs=[lat, lat, full, full],
        out_shape=[jax.ShapeDtypeStruct((l, d), MXU_DTYPE)] * 2 + [jax.ShapeDtypeStruct((l + lc, d), MXU_DTYPE)] * 2,
        compiler_params=_cp(("arbitrary",)),
    )(ya, yb, proj, proj, dmerged)


def _res_fwd(x, mix, g1, wn, sc2, sh2):
    l, d = x.shape
    t = 256

    def body(x_ref, m_ref, g1_ref, wn_ref, sc_ref, sh_ref, x1_ref, hx_ref):
        x1, hx = _f_res(x_ref[...], m_ref[...], g1_ref[...], wn_ref[...], sc_ref[...], sh_ref[...])
        x1_ref[...] = x1
        hx_ref[...] = hx.astype(hx_ref.dtype)

    row = pl.BlockSpec((t, d), lambda i: (i, 0))
    return pl.pallas_call(
        body, name="res_fwd", grid=(l // t,), in_specs=[row, row] + [_bc_spec(d)] * 4, out_specs=[row, row],
        out_shape=[jax.ShapeDtypeStruct((l, d), F32), jax.ShapeDtypeStruct((l, d), MXU_DTYPE)],
        compiler_params=_cp(("parallel",)),
    )(x, mix, g1, wn, sc2, sh2)


def _res_bwd(x, mix, g1, wn, sc2, sh2, dx1, dhx2):
    l, d = x.shape
    t = 256

    def body(x_ref, m_ref, g1_ref, wn_ref, sc_ref, sh_ref, dx1_ref, dh_ref, dx_ref, dm_ref, dg1_ref, dwn_ref, dsc_ref, dsh_ref):
        _, vjp = jax.vjp(_f_res, x_ref[...], m_ref[...], g1_ref[...], wn_ref[...], sc_ref[...], sh_ref[...])
        dx, dm, dg1, dwn, dsc, dsh = vjp((dx1_ref[...], dh_ref[...]))
        dx_ref[...] = dx
        dm_ref[...] = dm.astype(dm_ref.dtype)
        i = pl.program_id(0)
        _acc(dg1_ref, dg1, i)
        _acc(dwn_ref, dwn, i)
        _acc(dsc_ref, dsc, i)
        _acc(dsh_ref, dsh, i)

    row = pl.BlockSpec((t, d), lambda i: (i, 0))
    vec = jax.ShapeDtypeStruct((1, d), F32)
    return pl.pallas_call(
        body, name="res_bwd", grid=(l // t,), in_specs=[row, row] + [_bc_spec(d)] * 4 + [row, row],
        out_specs=[row, row] + [_bc_spec(d)] * 4,
        out_shape=[jax.ShapeDtypeStruct((l, d), F32), jax.ShapeDtypeStruct((l, d), MXU_DTYPE)] + [vec] * 4,
        compiler_params=_cp(("arbitrary",)),
    )(x, mix, g1, wn, sc2, sh2, dx1, dhx2)


def _swiglu_fwd(gu, df):
    l = gu.shape[0]
    t = 256

    def body(g_ref, u_ref, o_ref):
        o_ref[...] = _f_swiglu(g_ref[...], u_ref[...]).astype(o_ref.dtype)

    return pl.pallas_call(
        body, name="swiglu_fwd", grid=(l // t,),
        in_specs=[pl.BlockSpec((t, df), lambda i: (i, 0)), pl.BlockSpec((t, df), lambda i: (i, 1))],
        out_specs=pl.BlockSpec((t, df), lambda i: (i, 0)),
        out_shape=jax.ShapeDtypeStruct((l, df), MXU_DTYPE), compiler_params=_cp(("parallel",)),
    )(gu, gu)


def _swiglu_bwd(gu, dact, df):
    l = gu.shape[0]
    t = 256

    lo, hi = pl.BlockSpec((t, df), lambda i: (i, 0)), pl.BlockSpec((t, df), lambda i: (i, 1))
    dgu = jax.ShapeDtypeStruct((l, 2 * df), MXU_DTYPE)

    def body(g_ref, u_ref, da_ref, dgu_ref):
        _, vjp = jax.vjp(_f_swiglu, g_ref[...], u_ref[...])
        dg, du = vjp(da_ref[...])
        dgu_ref[:, :df] = dg.astype(dgu_ref.dtype)
        dgu_ref[:, df:] = du.astype(dgu_ref.dtype)

    return pl.pallas_call(
        body, name="swiglu_bwd", grid=(l // t,), in_specs=[lo, hi, lo],
        out_specs=pl.BlockSpec((t, 2 * df), lambda i: (i, 0)), out_shape=dgu, compiler_params=_cp(("parallel",)),
    )(gu, gu, dact)


def _loss_and_grads(x1, dn, g2, wn, tgt):
    l, d = x1.shape
    t = 256

    def body(x1_ref, dn_ref, g2_ref, wn_ref, t_ref, loss_ref, dx_ref, ddn_ref, dg2_ref, dwn_ref):
        loss, vjp = jax.vjp(lambda a, b, c, e: _f_loss(a, b, c, e, t_ref[...]), x1_ref[...], dn_ref[...], g2_ref[...], wn_ref[...])
        dx, ddn, dg2, dwn = vjp(jnp.ones((1, 1), F32))
        dx_ref[...] = dx
        ddn_ref[...] = ddn.astype(ddn_ref.dtype)
        i = pl.program_id(0)
        _acc(loss_ref, loss, i)
        _acc(dg2_ref, dg2, i)
        _acc(dwn_ref, dwn, i)

    row = pl.BlockSpec((t, d), lambda i: (i, 0))
    vec = jax.ShapeDtypeStruct((1, d), F32)
    return pl.pallas_call(
        body, name="loss_and_grads", grid=(l // t,), in_specs=[row, row] + [_bc_spec(d)] * 2 + [row],
        out_specs=[pl.BlockSpec((1, 1), lambda i: (0, 0)), row, row] + [_bc_spec(d)] * 2,
        out_shape=[jax.ShapeDtypeStruct((1, 1), F32), jax.ShapeDtypeStruct((l, d), F32),
                   jax.ShapeDtypeStruct((l, d), MXU_DTYPE), vec, vec],
        compiler_params=_cp(("arbitrary",)),
    )(x1, dn, g2, wn, tgt)


PAD = 8


def _conv5_taps(s_ref, w_ref, l, lc, width):
    half = width // 2
    lat = sum(w_ref[k:k + 1, :] * s_ref[pl.ds(PAD + k - half, l), :] for k in range(width))
    ctx = sum(w_ref[k:k + 1, :] * s_ref[pl.ds(2 * PAD + l + k - half, lc), :] for k in range(width))
    return lat, ctx


def _fill_padded(s_ref, lat, ctx, l, lc):
    zeros = jnp.zeros((PAD, s_ref.shape[1]), F32)
    s_ref[pl.ds(0, PAD), :] = zeros
    s_ref[pl.ds(PAD, l), :] = lat
    s_ref[pl.ds(PAD + l, PAD), :] = zeros
    s_ref[pl.ds(2 * PAD + l, lc), :] = ctx
    s_ref[pl.ds(2 * PAD + l + lc, PAD), :] = zeros


def _conv5_fwd(proj, w, b, l, lc, col0, ncols):
    t_all = l + lc
    cw = LANES
    blk0 = col0 // cw
    width = w.shape[0]

    def body(x_ref, w_ref, b_ref, o_ref, s_ref):
        _fill_padded(s_ref, x_ref[pl.ds(0, l), :], x_ref[pl.ds(l, lc), :], l, lc)
        lat, ctx = _conv5_taps(s_ref, w_ref, l, lc, width)
        o_ref[pl.ds(0, l), :] = _silu(lat + b_ref[...])
        o_ref[pl.ds(l, lc), :] = _silu(ctx + b_ref[...])

    return pl.pallas_call(
        body, name="conv5_fwd", grid=(ncols // cw,),
        in_specs=[pl.BlockSpec((t_all, cw), lambda j: (0, blk0 + j)), pl.BlockSpec((width, cw), lambda j: (0, j)),
                  pl.BlockSpec((1, cw), lambda j: (0, j))],
        out_specs=pl.BlockSpec((t_all, cw), lambda j: (0, j)),
        out_shape=jax.ShapeDtypeStruct((t_all, ncols), F32),
        scratch_shapes=[pltpu.VMEM((t_all + 3 * PAD, cw), F32)], compiler_params=_cp(("parallel",)),
    )(proj, w, b)


def _conv5_bwd(proj, w, b, cots, l, lc, col0, seg0, ncols):
    t_all = l + lc
    cw = LANES
    blk0, sblk0 = col0 // cw, seg0 // cw
    width = w.shape[0]
    half = width // 2
    nc = len(cots)

    def body(*refs):
        x_ref, w_ref, b_ref = refs[:3]
        cot_refs = refs[3:3 + nc]
        dx_ref, dw_ref, db_ref, s_ref = refs[3 + nc:]
        x_lat, x_ctx = x_ref[pl.ds(0, l), :], x_ref[pl.ds(l, lc), :]
        _fill_padded(s_ref, x_lat, x_ctx, l, lc)
        pre_lat, pre_ctx = _conv5_taps(s_ref, w_ref, l, lc, width)
        g = sum(c[...] for c in cot_refs)

        def through_silu(pre, cot):
            _, vjp = jax.vjp(_silu, pre + b_ref[...])
            return vjp(cot)[0]

        d_lat = through_silu(pre_lat, g[:l])
        d_ctx = through_silu(pre_ctx, g[l:])
        db_ref[...] = jnp.sum(d_lat, axis=0, keepdims=True) + jnp.sum(d_ctx, axis=0, keepdims=True)
        for k in range(width):
            dw_ref[k:k + 1, :] = (
                jnp.sum(d_lat * s_ref[pl.ds(PAD + k - half, l), :], axis=0, keepdims=True)
                + jnp.sum(d_ctx * s_ref[pl.ds(2 * PAD + l + k - half, lc), :], axis=0, keepdims=True))
        _fill_padded(s_ref, d_lat, d_ctx, l, lc)
        dx_lat = sum(w_ref[k:k + 1, :] * s_ref[pl.ds(PAD - (k - half), l), :] for k in range(width))
        dx_ctx = sum(w_ref[k:k + 1, :] * s_ref[pl.ds(2 * PAD + l - (k - half), lc), :] for k in range(width))
        dx_ref[pl.ds(0, l), :] = dx_lat.astype(dx_ref.dtype)
        dx_ref[pl.ds(l, lc), :] = dx_ctx.astype(dx_ref.dtype)

    col = pl.BlockSpec((t_all, cw), lambda j: (0, j))
    return pl.pallas_call(
        body, name=f"conv5_bwd_{seg0}", grid=(ncols // cw,),
        in_specs=[pl.BlockSpec((t_all, cw), lambda j: (0, blk0 + sblk0 + j)),
                  pl.BlockSpec((width, cw), lambda j: (0, sblk0 + j)), pl.BlockSpec((1, cw), lambda j: (0, sblk0 + j))]
        + [col] * nc,
        out_specs=[col, pl.BlockSpec((width, cw), lambda j: (0, j)), pl.BlockSpec((1, cw), lambda j: (0, j))],
        out_shape=[jax.ShapeDtypeStruct((t_all, ncols), MXU_DTYPE), jax.ShapeDtypeStruct((width, ncols), F32),
                   jax.ShapeDtypeStruct((1, ncols), F32)],
        scratch_shapes=[pltpu.VMEM((t_all + 3 * PAD, cw), F32)], compiler_params=_cp(("parallel",)),
    )(proj, w, b, *cots)


def _conv31_fwd(proj, w, b, l, d, u_blk):
    cw = LANES
    width = w.shape[0]
    reach = (width // 2) * GRID_W
    nb = d // cw

    def body(u_ref, v_ref, w_ref, b_ref, o_ref, s_ref):
        s_ref[pl.ds(0, reach), :] = jnp.zeros((reach, cw), F32)
        s_ref[pl.ds(reach, l), :] = u_ref[...] * jax.nn.sigmoid(v_ref[...])
        s_ref[pl.ds(reach + l, reach), :] = jnp.zeros((reach, cw), F32)
        o_ref[...] = sum(w_ref[k:k + 1, :] * s_ref[pl.ds(k * GRID_W, l), :] for k in range(width)) + b_ref[...]

    return pl.pallas_call(
        body, name="conv31_fwd", grid=(nb,),
        in_specs=[pl.BlockSpec((l, cw), lambda j: (0, u_blk * nb + j)), pl.BlockSpec((l, cw), lambda j: (0, (u_blk + 1) * nb + j)),
                  pl.BlockSpec((width, cw), lambda j: (0, j)), pl.BlockSpec((1, cw), lambda j: (0, j))],
        out_specs=pl.BlockSpec((l, cw), lambda j: (0, j)), out_shape=jax.ShapeDtypeStruct((l, d), F32),
        scratch_shapes=[pltpu.VMEM((l + 2 * reach, cw), F32)], compiler_params=_cp(("parallel",)),
    )(proj, proj, w, b)


def _conv31_bwd(proj, w, dcv, l, lc, d, u_blk):
    cw = LANES
    width = w.shape[0]
    reach = (width // 2) * GRID_W
    nb = d // cw
    t_all = l + lc

    def body(u_ref, v_ref, w_ref, g_ref, du_ref, dv_ref, dw_ref, db_ref, s_ref):
        zeros = jnp.zeros((reach, cw), F32)
        s_ref[pl.ds(0, reach), :] = zeros
        s_ref[pl.ds(reach + l, reach), :] = zeros
        u, v, g = u_ref[...], v_ref[...], g_ref[...]
        s_ref[pl.ds(reach, l), :] = u * jax.nn.sigmoid(v)
        db_ref[...] = jnp.sum(g, axis=0, keepdims=True)
        for k in range(width):
            dw_ref[k:k + 1, :] = jnp.sum(g * s_ref[pl.ds(k * GRID_W, l), :], axis=0, keepdims=True)
        s_ref[pl.ds(reach, l), :] = g
        dt = sum(w_ref[k:k + 1, :] * s_ref[pl.ds((width - 1 - k) * GRID_W, l), :] for k in range(width))
        _, vjp = jax.vjp(lambda a, c: a * jax.nn.sigmoid(c), u, v)
        du, dv = vjp(dt)
        du_ref[pl.ds(0, l), :] = du.astype(du_ref.dtype)
        dv_ref[pl.ds(0, l), :] = dv.astype(dv_ref.dtype)
        du_ref[pl.ds(l, lc), :] = jnp.zeros((lc, cw), du_ref.dtype)
        dv_ref[pl.ds(l, lc), :] = jnp.zeros((lc, cw), dv_ref.dtype)

    pshape = jax.ShapeDtypeStruct((t_all, d), MXU_DTYPE)
    tall = pl.BlockSpec((t_all, cw), lambda j: (0, j))
    return pl.pallas_call(
        body, name="conv31_bwd", grid=(nb,),
        in_specs=[pl.BlockSpec((l, cw), lambda j: (0, u_blk * nb + j)), pl.BlockSpec((l, cw), lambda j: (0, (u_blk + 1) * nb + j)),
                  pl.BlockSpec((width, cw), lambda j: (0, j)), pl.BlockSpec((l, cw), lambda j: (0, j))],
        out_specs=[tall, tall, pl.BlockSpec((width, cw), lambda j: (0, j)), pl.BlockSpec((1, cw), lambda j: (0, j))],
        out_shape=[pshape, pshape, jax.ShapeDtypeStruct((width, d), F32), jax.ShapeDtypeStruct((1, d), F32)],
        scratch_shapes=[pltpu.VMEM((l + 2 * reach, cw), F32)], compiler_params=_cp(("parallel",)),
    )(proj, proj, w, dcv)


def _softplus(x):
    return jnp.maximum(x, 0.0) + jnp.log(1.0 + jnp.exp(-jnp.abs(x)))


def _dt_fwd(proj, bias, a, dt_blk):
    t_all = proj.shape[0]
    hh = bias.shape[1]
    q = CHUNK

    def body(r_ref, b_ref, a_ref, dt_ref, cs_ref, tc_ref, cst_ref):
        dt = _softplus(r_ref[...] + b_ref[...])
        dt_ref[...] = dt
        da = dt * a_ref[...]
        li, si = _iota((q, q), 0), _iota((q, q), 1)
        reverse_cols = _iota((q, hh), 1) >= hh // 2
        cs = jnp.where(reverse_cols, _dot((si >= li).astype(F32), da, exact=True), _dot((si <= li).astype(F32), da, exact=True))
        cs_ref[...] = cs
        cst_ref[...] = cs.T
        total = jnp.where(_iota((1, hh), 1) >= hh // 2, cs_ref[0:1, :], cs_ref[q - 1:q, :])
        tc_ref[...] = total - cs

    row = pl.BlockSpec((q, hh), lambda i: (i, 0))
    shape = jax.ShapeDtypeStruct((t_all, hh), F32)
    return pl.pallas_call(
        body, name="dt_fwd", grid=(t_all // q,),
        in_specs=[pl.BlockSpec((q, hh), lambda i: (i, dt_blk)), _bc_spec(hh), _bc_spec(hh)],
        out_specs=[row, row, row, pl.BlockSpec((hh, q), lambda i: (0, i))],
        out_shape=[shape, shape, shape, jax.ShapeDtypeStruct((hh, t_all), F32)],
        compiler_params=_cp(("parallel",)),
    )(proj, bias, a)


def _three_way(x):
    def top(v):
        word = lax.bitcast_convert_type(v, jnp.uint32) & jnp.uint32(0xFFFF0000)
        return lax.bitcast_convert_type(word, F32)

    hi = top(x)
    rest = x - hi
    mid = top(rest)
    return hi.astype(jnp.bfloat16), mid.astype(jnp.bfloat16), (rest - mid).astype(jnp.bfloat16)


def _scan_columns(dt, cs, tc, groups2, hpg):
    t_all = dt.shape[0]
    parts = [part.reshape(t_all, groups2, 1, hpg) for arr in (dt, cs, tc) for part in _three_way(arr)]
    cols = jnp.concatenate(parts, axis=2).transpose(1, 0, 2, 3).reshape(groups2, t_all, 9 * hpg)
    return jnp.pad(cols, ((0, 0), (0, 0), (0, LANES - 9 * hpg)))


def _dt_bwd(proj, bias, dt, ddt, dda, dt_blk):
    t_all = proj.shape[0]
    hh = bias.shape[1]
    q = _tile(t_all, 1024, LANES)

    def body(r_ref, b_ref, dt_ref, ddt_ref, dda_ref, dr_ref, db_ref, da_ref):
        dr = ddt_ref[...] * jax.nn.sigmoid(r_ref[...] + b_ref[...])
        dr_ref[...] = dr.astype(dr_ref.dtype)
        i = pl.program_id(0)
        _acc(db_ref, jnp.sum(dr, axis=0, keepdims=True), i)
        _acc(da_ref, jnp.sum(dda_ref[...] * dt_ref[...], axis=0, keepdims=True), i)

    row = pl.BlockSpec((q, hh), lambda i: (i, 0))
    vec = jax.ShapeDtypeStruct((1, hh), F32)
    return pl.pallas_call(
        body, name="dt_bwd", grid=(t_all // q,),
        in_specs=[pl.BlockSpec((q, hh), lambda i: (i, dt_blk)), _bc_spec(hh), row, row, row],
        out_specs=[row, _bc_spec(hh), _bc_spec(hh)],
        out_shape=[jax.ShapeDtypeStruct((t_all, hh), MXU_DTYPE), vec, vec], compiler_params=_cp(("arbitrary",)),
    )(proj, bias, dt, ddt, dda)


_NT = (((1,), (1,)), ((), ()))
_TN = (((0,), (0,)), ((), ()))


def _dot(a, b, dims=None, exact=False):
    kw = dict(preferred_element_type=F32)
    if exact:
        kw["precision"] = HI
    if dims is None:
        return jnp.dot(a, b, **kw)
    return lax.dot_general(a, b, dims, **kw)


def _iota(shape, dim):
    return lax.broadcasted_iota(jnp.int32, shape, dim)


class _Ssd:
    def __init__(self, l, lc, di, p, reverse):
        self.q, self.n, self.g = CHUNK, SSM_STATE, SSM_GROUPS
        self.nl, self.ncx = l // CHUNK, lc // CHUNK
        self.ns = self.nl + self.ncx
        self.t_all, self.di, self.p, self.reverse = l + lc, di, p, reverse
        self.hpg = di // p // SSM_GROUPS
        self.gw = self.hpg * p
        self.ntile = self.gw // LANES
        self.hpt = LANES // p
        self.log2p = p.bit_length() - 1
        assert 1 << self.log2p == p and self.gw % LANES == 0 and self.n == LANES and self.q == LANES
        assert 9 * self.hpg <= LANES
        self.d = 1 if reverse else 0

    def chunk_at(self, step):
        if self.reverse:
            return self.ns - 1 - step
        return jnp.where(step < self.ncx, self.nl + step, step - self.ncx)

    def selectors(self):
        hpg = self.hpg
        k = jnp.arange(LANES)
        quantity, head, used = k // (3 * hpg), k % hpg, k < 9 * hpg
        lane_head = jnp.arange(LANES) // self.p
        tiles = jnp.concatenate([(used & (quantity == qo))[:, None] & (head[:, None] == tt * self.hpt + lane_head[None, :])
                                 for tt in range(self.ntile) for qo in range(3)], axis=1)
        heads = jnp.concatenate([jnp.broadcast_to((used & (quantity == 1) & (head == j))[:, None], (LANES, LANES))
                                 for j in range(hpg)], axis=1)
        return tiles.astype(jnp.bfloat16), heads.astype(jnp.bfloat16)

    def in_specs(self, chunk_of):
        g, n, hpg, q = self.g, self.n, self.hpg, self.q
        b_blk, c_blk = self.di // n, self.di // n + g
        d = self.d
        return [
            pl.BlockSpec((q, self.gw), lambda gi, i: (chunk_of(i), gi)),
            pl.BlockSpec((q, n), lambda gi, i: (chunk_of(i), b_blk + gi)),
            pl.BlockSpec((q, n), lambda gi, i: (chunk_of(i), c_blk + gi)),
            pl.BlockSpec((1, q, LANES), lambda gi, i: (d * g + gi, chunk_of(i), 0)),
            pl.BlockSpec((hpg, q), lambda gi, i: (d * g + gi, chunk_of(i))),
            pl.BlockSpec((LANES, self.ntile * 3 * LANES), lambda gi, i: (0, 0)),
            pl.BlockSpec((LANES, hpg * LANES), lambda gi, i: (0, 0)),
        ]

    def masks(self):
        li, si = _iota((self.q, self.q), 0), _iota((self.q, self.q), 1)
        if self.reverse:
            return si >= li, li >= si
        return si <= li, li <= si

    def spread(self, spread_all, tt):
        at = 3 * LANES * tt
        return tuple(spread_all[:, at + k * LANES:at + (k + 1) * LANES] for k in range(3))

    def head_lanes(self, qq):
        return lax.shift_right_logical(_iota((self.q, LANES), 1), self.log2p) == qq

    def head_sums(self, values, tt):
        sel = _iota((HEAD_COLS, LANES), 0) == tt * self.hpt + lax.shift_right_logical(_iota((HEAD_COLS, LANES), 1), self.log2p)
        parts = [part for v in values for part in _three_way(v)]
        sums = _dot(jnp.concatenate(parts, axis=0), sel.astype(jnp.bfloat16), _NT)
        out, at = [], 0
        for v in values:
            rows = v.shape[0]
            out.append(sums[at:at + rows] + sums[at + rows:at + 2 * rows] + sums[at + 2 * rows:at + 3 * rows])
            at += 3 * rows
        return out

    def state_scale(self, csr_ref):
        last = 0 if self.reverse else self.q - 1
        total = jnp.sum(jnp.where(_iota((self.hpg, self.q), 1) == last, csr_ref[...], 0.0), axis=1, keepdims=True)
        decay = jnp.broadcast_to(jnp.exp(total), (self.hpg, self.n))
        decay = jnp.concatenate([decay, jnp.zeros((HEAD_COLS - self.hpg, self.n), F32)], axis=0)
        rows = lax.shift_right_logical(_iota((self.gw, HEAD_COLS), 0), self.log2p) == _iota((self.gw, HEAD_COLS), 1)
        return _dot_parts(rows.astype(jnp.bfloat16), decay)


def _dot_parts(sel, v, dims=None):
    return sum(_dot(sel, part, dims) for part in _three_way(v))


def _ssd_fwd(xbc, cols, cs_t, l, lc, di, p, reverse):
    s = _Ssd(l, lc, di, p, reverse)
    q, n, gw = s.q, s.n, s.gw
    neg_inf = float("-inf")

    def body(xs_ref, b_ref, c_ref, cols_ref, csr_ref, et_ref, eh_ref, y_ref, hp_ref, h_scr):
        i = pl.program_id(1)

        @pl.when(i == 0)
        def _():
            h_scr[...] = jnp.zeros_like(h_scr)

        h = h_scr[...]
        hp_ref[0, 0] = h
        mask, _ = s.masks()
        cols = cols_ref[0]
        bb, cb = b_ref[...].astype(MXU_DTYPE), c_ref[...].astype(MXU_DTYPE)
        cbt = _dot(cb, bb, _NT)
        y_off = _dot(cb, h.astype(MXU_DTYPE), _NT)
        spread_all, cs_heads = _dot(cols, et_ref[...]), _dot(cols, eh_ref[...])
        w_tiles = []
        for tt in range(s.ntile):
            sl = slice(tt * LANES, (tt + 1) * LANES)
            dt_b, cs_b, tc_b = s.spread(spread_all, tt)
            x = xs_ref[:, sl] * dt_b
            ms, xhs = [], []
            for qq in range(s.hpt):
                j = tt * s.hpt + qq
                seg = cs_heads[:, j * LANES:(j + 1) * LANES] - csr_ref[j:j + 1, :]
                ms.append((cbt * jnp.exp(jnp.where(mask, seg, neg_inf))).astype(MXU_DTYPE))
                xhs.append(jnp.where(s.head_lanes(qq), x, 0.0).astype(MXU_DTYPE))
            yd = _dot(jnp.concatenate(ms, axis=1), jnp.concatenate(xhs, axis=0))
            y_ref[:, sl] = yd + y_off[:, sl] * jnp.exp(cs_b)
            w_tiles.append((x * jnp.exp(tc_b)).astype(MXU_DTYPE))
        wm = w_tiles[0] if s.ntile == 1 else jnp.concatenate(w_tiles, axis=1)
        h_scr[...] = h * s.state_scale(csr_ref) + _dot(wm, bb, _TN)

    d = "rev" if reverse else "fwd"
    e_tiles, e_heads = s.selectors()
    return pl.pallas_call(
        body, name=f"ssd_{d}", grid=(s.g, s.ns), in_specs=s.in_specs(s.chunk_at),
        out_specs=[pl.BlockSpec((q, gw), lambda gi, i: (s.chunk_at(i), gi)),
                   pl.BlockSpec((1, 1, gw, n), lambda gi, i: (i, gi, 0, 0))],
        out_shape=[jax.ShapeDtypeStruct((s.t_all, di), F32), jax.ShapeDtypeStruct((s.ns, s.g, gw, n), F32)],
        scratch_shapes=[pltpu.VMEM((gw, n), F32)],
        compiler_params=_cp(("parallel", "arbitrary")),
    )(xbc, xbc, xbc, cols, cs_t, e_tiles, e_heads)


def _ssd_bwd(xbc, cols, cs_t, a_cols, dy, hprev, l, lc, di, p, reverse, dsk=None, prev=None):
    s = _Ssd(l, lc, di, p, reverse)
    q, n, gw, hpg = s.q, s.n, s.gw, s.hpg
    neg_inf = float("-inf")
    n_extra = (dsk is not None) + (3 if prev is not None else 0)

    def chunk_of(i):
        return s.chunk_at(s.ns - 1 - i)

    def body(xs_ref, b_ref, c_ref, cols_ref, csr_ref, et_ref, eh_ref, ac_ref, dy_ref, hp_ref, *rest):
        extra, (dxs_ref, db_ref, dc_ref, ddt_ref, dda_ref, dh_scr) = rest[:n_extra], rest[n_extra:]
        dsk_ref = extra[0] if dsk is not None else None
        prev_refs = extra[-3:] if prev is not None else None
        i = pl.program_id(1)

        @pl.when(i == 0)
        def _():
            dh_scr[...] = jnp.zeros_like(dh_scr)

        latent = (chunk_of(i) < s.nl).astype(F32)
        h, dh = hp_ref[0, 0], dh_scr[...]
        hb, dhb = h.astype(MXU_DTYPE), dh.astype(MXU_DTYPE)
        mask, mask_t = s.masks()
        cols = cols_ref[0]
        bb, cb = b_ref[...].astype(MXU_DTYPE), c_ref[...].astype(MXU_DTYPE)
        cbt, bct = _dot(cb, bb, _NT), _dot(bb, cb, _NT)
        b_dh = _dot(bb, dhb, _NT)
        y_off0 = _dot(cb, hb, _NT)
        d_g, d_gt = jnp.zeros((q, q), F32), jnp.zeros((q, q), F32)
        dcs = jnp.zeros((q, HEAD_COLS), F32)
        ddt_x = jnp.zeros((q, HEAD_COLS), F32)
        r_state = jnp.zeros((16, HEAD_COLS), F32)
        spread_all, cs_heads = _dot(cols, et_ref[...]), _dot(cols, eh_ref[...])
        dye_tiles, xte_tiles = [], []
        for tt in range(s.ntile):
            sl = slice(tt * LANES, (tt + 1) * LANES)
            dt_b, cs_b, tc_b = s.spread(spread_all, tt)
            ecs_b, te_b = jnp.exp(cs_b), jnp.exp(tc_b)
            xs_t = xs_ref[:, sl]
            x = xs_t * dt_b
            d_y = dy_ref[:, sl] * latent
            dx_state = b_dh[:, sl] * te_b
            lms, lm_ts, m_ts, d_yhs, xhs = [], [], [], [], []
            for qq in range(s.hpt):
                j = tt * s.hpt + qq
                csc_b = cs_heads[:, j * LANES:(j + 1) * LANES]
                csr = csr_ref[j:j + 1, :]
                lms.append(jnp.exp(jnp.where(mask, csc_b - csr, neg_inf)))
                lm_ts.append(jnp.exp(jnp.where(mask_t, csr - csc_b, neg_inf)))
                m_ts.append(bct * lm_ts[-1])
                lanes = s.head_lanes(qq)
                d_yhs.append(jnp.where(lanes, d_y, 0.0).astype(MXU_DTYPE))
                xhs.append(jnp.where(lanes, x, 0.0).astype(MXU_DTYPE))
            d_yh_rows = jnp.concatenate(d_yhs, axis=0)
            d_m_all = _dot(d_yh_rows, x.astype(MXU_DTYPE), _NT)
            d_mt_all = _dot(jnp.concatenate(xhs, axis=0), d_y.astype(MXU_DTYPE), _NT)
            for qq in range(s.hpt):
                j = tt * s.hpt + qq
                d_m, d_mt = d_m_all[qq * q:(qq + 1) * q], d_mt_all[qq * q:(qq + 1) * q]
                r1 = jnp.sum(d_m * (cbt * lms[qq]), axis=1, keepdims=True)
                r2 = jnp.sum(d_mt * m_ts[qq], axis=1, keepdims=True)
                dcs = dcs + (r1 - r2) * (_iota((1, HEAD_COLS), 1) == j).astype(F32)
                d_g = d_g + d_m * lms[qq]
                d_gt = d_gt + d_mt * lm_ts[qq]
            d_x = _dot(jnp.concatenate([m.astype(MXU_DTYPE) for m in m_ts], axis=1), d_yh_rows) + dx_state
            d_xs = d_x * dt_b
            if dsk_ref is not None:
                d_xs = d_xs + d_y * dsk_ref[:, sl]
            if prev_refs is not None:
                d_xs = d_xs + prev_refs[0][:, sl]
            dxs_ref[:, sl] = d_xs
            fed = x * dx_state
            fed_rows = jnp.broadcast_to(jnp.sum(fed, axis=0, keepdims=True), (16, LANES))
            sums = s.head_sums([d_x * xs_t, d_y * y_off0[:, sl] * ecs_b - fed, fed_rows], tt)
            ddt_x, dcs, r_state = ddt_x + sums[0], dcs + sums[1], r_state + sums[2]
            dye_tiles.append((d_y * ecs_b).astype(MXU_DTYPE))
            xte_tiles.append((x * te_b).astype(MXU_DTYPE))
        dye = dye_tiles[0] if s.ntile == 1 else jnp.concatenate(dye_tiles, axis=1)
        xte = xte_tiles[0] if s.ntile == 1 else jnp.concatenate(xte_tiles, axis=1)
        d_c = _dot(d_g.astype(MXU_DTYPE), bb) + _dot(dye, hb)
        d_b = _dot(d_gt.astype(MXU_DTYPE), cb) + _dot(xte, dhb)
        if prev_refs is not None:
            d_b, d_c = d_b + prev_refs[1][...], d_c + prev_refs[2][...]
        dc_ref[...] = d_c
        db_ref[...] = d_b
        scale = s.state_scale(csr_ref)
        carried = dh * h * scale
        d_tot = jnp.sum(r_state, axis=0, keepdims=True) * 0.0625
        for j in range(hpg):
            part = jnp.sum(carried[j * p:(j + 1) * p, :], axis=0, keepdims=True)
            d_tot = d_tot + jnp.sum(part, axis=1, keepdims=True) * (_iota((1, HEAD_COLS), 1) == j).astype(F32)
        dda = _dot_parts(mask_t.astype(jnp.bfloat16), dcs) + d_tot
        ddt_ref[0] = ddt_x + dda * ac_ref[0]
        dda_ref[0] = dda
        dh_scr[...] = dh * scale + _dot(dye, cb, _TN)

    d = "rev" if reverse else "fwd"
    e_tiles, e_heads = s.selectors()
    col = pl.BlockSpec((1, q, HEAD_COLS), lambda gi, i: (gi, chunk_of(i), 0))
    gn = pl.BlockSpec((q, n), lambda gi, i: (chunk_of(i), gi))
    wide = pl.BlockSpec((q, gw), lambda gi, i: (chunk_of(i), gi))
    extra_specs, extra_args, aliases = [], [], {}
    if dsk is not None:
        extra_specs.append(pl.BlockSpec((1, gw), lambda gi, i: (0, gi)))
        extra_args.append(dsk)
    if prev is not None:
        first = 10 + len(extra_args)
        extra_specs += [wide, gn, gn]
        extra_args += list(prev)
        aliases = {first: 0, first + 1: 1, first + 2: 2}
    return pl.pallas_call(
        body, name=f"ssd_bwd_{d}", grid=(s.g, s.ns),
        in_specs=s.in_specs(chunk_of) + [
            pl.BlockSpec((1, 1, HEAD_COLS), lambda gi, i: (s.d * s.g + gi, 0, 0)),
            pl.BlockSpec((q, gw), lambda gi, i: (jnp.minimum(chunk_of(i), s.nl - 1), gi)),
            pl.BlockSpec((1, 1, gw, n), lambda gi, i: (s.ns - 1 - i, gi, 0, 0))] + extra_specs,
        out_specs=[wide, gn, gn, col, col],
        out_shape=[jax.ShapeDtypeStruct((s.t_all, di), F32), jax.ShapeDtypeStruct((s.t_all, s.g * n), F32),
                   jax.ShapeDtypeStruct((s.t_all, s.g * n), F32), jax.ShapeDtypeStruct((s.g, s.t_all, HEAD_COLS), F32),
                   jax.ShapeDtypeStruct((s.g, s.t_all, HEAD_COLS), F32)],
        scratch_shapes=[pltpu.VMEM((gw, n), F32)],
        input_output_aliases=aliases, compiler_params=_cp(("parallel", "arbitrary")),
    )(xbc, xbc, xbc, cols, cs_t, e_tiles, e_heads, a_cols, dy, hprev, *extra_args)


def _ada_fwd(crows, w, b):
    r, d = crows.shape
    ws = w.shape[1]
    tn = _tile(ws, 512, LANES)

    def body(c_ref, w_ref, b_ref, m_ref, s_ref):
        s = _silu(c_ref[...])
        s_ref[...] = s
        m_ref[...] = _dot(s.astype(MXU_DTYPE), w_ref[...].astype(MXU_DTYPE)) + b_ref[...]

    full = pl.BlockSpec((r, d), lambda j: (0, 0))
    return pl.pallas_call(
        body, name="ada_fwd", grid=(ws // tn,),
        in_specs=[full, pl.BlockSpec((d, tn), lambda j: (0, j)), pl.BlockSpec((1, tn), lambda j: (0, j))],
        out_specs=[pl.BlockSpec((r, tn), lambda j: (0, j)), full],
        out_shape=[jax.ShapeDtypeStruct((r, ws), F32), jax.ShapeDtypeStruct((r, d), F32)],
        compiler_params=_cp(("arbitrary",)),
    )(crows, w, b)


def _ada_bwd(s_t, w, dm):
    d, r = s_t.shape
    ws = w.shape[1]
    tn = _tile(ws, 512, LANES)

    def body(st_ref, w_ref, dm_ref, dw_ref, ds_ref):
        dmb = dm_ref[...].astype(MXU_DTYPE)
        dw_ref[...] = _dot(st_ref[...].astype(MXU_DTYPE), dmb)
        _acc(ds_ref, _dot(dmb, w_ref[...].astype(MXU_DTYPE), _NT), pl.program_id(0))

    return pl.pallas_call(
        body, name="ada_bwd", grid=(ws // tn,),
        in_specs=[pl.BlockSpec((d, r), lambda j: (0, 0)), pl.BlockSpec((d, tn), lambda j: (0, j)),
                  pl.BlockSpec((r, tn), lambda j: (0, j))],
        out_specs=[pl.BlockSpec((d, tn), lambda j: (0, j)), pl.BlockSpec((r, d), lambda j: (0, 0))],
        out_shape=[jax.ShapeDtypeStruct((d, ws), F32), jax.ShapeDtypeStruct((r, d), F32)],
        compiler_params=_cp(("arbitrary",)),
    )(s_t, w, dm)


def _adamw(w, g, m, v, name):
    r, c = w.shape
    t = _tile(r, max(8, 300_000 // c), 8)

    def body(w_ref, g_ref, m_ref, v_ref, d_ref, m2_ref, v2_ref):
        g = g_ref[...]
        m2 = ADAM_B1 * m_ref[...] + (1.0 - ADAM_B1) * g
        v2 = ADAM_B2 * v_ref[...] + (1.0 - ADAM_B2) * (g * g)
        m_hat = m2 / (1.0 - ADAM_B1 ** ADAM_STEP)
        v_hat = v2 / (1.0 - ADAM_B2 ** ADAM_STEP)
        d_ref[...] = -ADAM_LR * (m_hat / (jnp.sqrt(v_hat) + ADAM_EPS) + ADAM_WD * w_ref[...])
        m2_ref[...] = m2
        v2_ref[...] = v2

    blk = pl.BlockSpec((t, c), lambda i: (i, 0))
    shape = jax.ShapeDtypeStruct((r, c), F32)
    return pl.pallas_call(
        body, name=name, grid=(r // t,), in_specs=[blk] * 4, out_specs=[blk] * 3, out_shape=[shape] * 3,
        compiler_params=_cp(("parallel",)),
    )(w, g, m, v)


def _sum_devices(gathered):
    rows, w = gathered.shape
    per = rows // N_DEV

    def body(g_ref, o_ref):
        total = g_ref[pl.ds(0, per), :]
        for dev in range(1, N_DEV):
            total = total + g_ref[pl.ds(dev * per, per), :]
        o_ref[...] = total

    return pl.pallas_call(
        body, name="sum_devices", out_shape=jax.ShapeDtypeStruct((per, w), F32),
        in_specs=[pl.BlockSpec(memory_space=pltpu.VMEM)], out_specs=pl.BlockSpec(memory_space=pltpu.VMEM),
        compiler_params=_cp(),
    )(gathered)


def _c_ctx_grad(parts, c_ctx):
    rows, d = parts.shape
    per = rows // N_DEV

    def body(p_ref, c_ref, o_ref):
        total = p_ref[pl.ds(0, 1), :]
        for chip in range(1, N_SHARD):
            total = total + p_ref[pl.ds(2 * chip * per, 1), :]
        _, vjp = jax.vjp(_silu, c_ref[...])
        o_ref[...] = vjp(total)[0]

    return pl.pallas_call(
        body, name="c_ctx_grad", out_shape=jax.ShapeDtypeStruct((1, d), F32),
        in_specs=[pl.BlockSpec(memory_space=pltpu.VMEM)] * 2, out_specs=pl.BlockSpec(memory_space=pltpu.VMEM),
        compiler_params=_cp(),
    )(parts, c_ctx)


def _pad_rows(a, rows, width):
    return jnp.pad(a, ((0, rows - a.shape[0]), (0, width - a.shape[1])))


def _pack(vectors, quantum):
    flat = jnp.concatenate([v.reshape(-1) for v in vectors])
    return jnp.pad(flat, (0, -flat.shape[0] % quantum))


def kernel(x, c, ctx, c_ctx, w_mod, b_mod, norm_mix, w_in, ssm_conv_w, ssm_conv_b, dt_bias, a_log, d_skip, ssm_norm, cf_conv_w, cf_conv_b, cf_ln_g, cf_ln_b, w_proj_a, w_proj_b, w_out, norm_ffn, w_ffn_gate, w_ffn_up, w_ffn_down, norm_final, loss_target, m_c_ctx, m_w_mod, m_b_mod, m_norm_mix, m_w_in, m_ssm_conv_w, m_ssm_conv_b, m_dt_bias, m_a_log, m_d_skip, m_ssm_norm, m_cf_conv_w, m_cf_conv_b, m_cf_ln_g, m_cf_ln_b, m_w_proj_a, m_w_proj_b, m_w_out, m_norm_ffn, m_w_ffn_gate, m_w_ffn_up, m_w_ffn_down, m_norm_final, v_c_ctx, v_w_mod, v_b_mod, v_norm_mix, v_w_in, v_ssm_conv_w, v_ssm_conv_b, v_dt_bias, v_a_log, v_d_skip, v_ssm_norm, v_cf_conv_w, v_cf_conv_b, v_cf_ln_g, v_cf_ln_b, v_w_proj_a, v_w_proj_b, v_w_out, v_norm_ffn, v_w_ffn_gate, v_w_ffn_up, v_w_ffn_down, v_norm_final):
    l, d = x.shape[1], x.shape[2]
    lc = ctx.shape[1]
    t_all = l + lc
    di = ssm_norm.shape[-1]
    h = d_skip.shape[-1]
    p = di // h
    g, n = SSM_GROUPS, SSM_STATE
    hpg = h // g
    conv_dim = di + 2 * g * n
    df = w_ffn_down.shape[1] * N_SHARD
    assert 2 * h == LANES and d % (2 * LANES) == 0

    my_x, my_y, my_c = _mesh_pos()
    chip = 2 * my_x + my_y
    dev = 2 * chip + my_c

    x2, ctx2, tgt = x[0], ctx[0], loss_target[0]
    row = lambda a: a.reshape(1, -1)

    cw_shard, cfw_shard = ssm_conv_w[0], cf_conv_w[0]
    k5, k31 = cw_shard.shape[0], cfw_shard.shape[0]
    r5, r31 = -(-k5 // 8) * 8, -(-k31 // 8) * 8
    wp = max(d, cw_shard.shape[1], cfw_shard.shape[1])
    packed = jnp.concatenate([_pad_rows(c, 8, wp), _pad_rows(cw_shard, r5, wp), _pad_rows(cfw_shard, r31, wp)], axis=0)
    got = _allgather_small(packed, "ag_params").reshape(N_DEV, 8 + r5 + r31, wp)
    c_all = got[:, 0, :d]
    conv_w = got[0::2, 8:8 + k5, :cw_shard.shape[1]].transpose(1, 0, 2).reshape(k5, conv_dim)
    cf_w = got[0::2, 8 + r5:8 + r5 + k31, :cfw_shard.shape[1]].transpose(1, 0, 2).reshape(k31, d)

    ws = w_mod.shape[2]
    crows = jnp.concatenate([c_all, row(c_ctx), jnp.zeros((7, d), F32)], axis=0)
    b_mod_mine = lax.dynamic_slice(b_mod, (0, chip * ws), (1, ws))
    m_part, s_rows = _ada_fwd(crows, w_mod[0], b_mod_mine)
    m_full = _allgather_small(m_part, "ag_mod").reshape(N_DEV, 16, ws)[0::2].transpose(1, 0, 2).reshape(16, N_SHARD * ws)
    m_lat = lax.dynamic_slice(m_full, (dev, 0), (1, 6 * d))
    sh1, sc1, g1, sh2, sc2, g2 = [m_lat[:, i * d:(i + 1) * d] for i in range(6)]
    csh1, csc1 = m_full[8:9, 0:d], m_full[8:9, d:2 * d]

    shards = [w_in[0].T, w_ffn_gate[0].T, w_ffn_up[0].T, w_proj_a[0], w_proj_b[0], w_out[0], w_ffn_down[0]]
    shards = [s.astype(WIRE_DTYPE) for s in shards]
    (win_t,) = _fill_own_rows(_run_side(_gather_side(shards[:1]), "ag_w_in"), shards[:1])
    o_xbc, o_dt, o_glu, o_gates = di, di + conv_dim, di + conv_dim + 2 * h, di + conv_dim + 2 * h + 2 * d
    win_work = jnp.concatenate([win_t[:o_xbc], win_t[o_glu:], win_t[o_xbc:o_dt], win_t[o_dt:o_glu]], axis=0)
    c_u, c_ga, c_xbc, c_dt = di, di + 2 * d, di + 4 * d, di + 4 * d + conv_dim

    nm = norm_mix
    hx = _mod_fwd(x2, ctx2, nm, sc1, sh1, csc1, csh1)
    proj, *rest = _matmul(hx, win_work.T, tm=384, tn=29 * LANES, n_outer=True, name="mm_proj", side=_gather_side(shards[1:]))
    wg_t, wu_t, wpa, wpb, wout, wdn = _fill_own_rows(rest, shards[1:])
    wgu = jnp.concatenate([wg_t, wu_t], axis=0)
    xbc = _conv5_fwd(proj, conv_w, ssm_conv_b, l, lc, c_xbc, conv_dim)
    a = -jnp.exp(a_log.reshape(1, 2 * h))
    dt, cs, tc, cs_t = _dt_fwd(proj, dt_bias.reshape(1, 2 * h), a, c_dt // LANES)
    cols = _scan_columns(dt, cs, tc, 2 * g, hpg)
    a_cols = jnp.pad(a.reshape(2 * g, 1, hpg), ((0, 0), (0, 0), (0, HEAD_COLS - hpg)))
    y_f, hp_f = _ssd_fwd(xbc, cols, cs_t, l, lc, di, p, False)
    y_r, hp_r = _ssd_fwd(xbc, cols, cs_t, l, lc, di, p, True)
    dsk = jnp.repeat(d_skip.reshape(h), p).reshape(1, di)
    ya_in = _gate_fwd(y_f, y_r, xbc, proj, dsk, ssm_norm, l, di)
    y_a = _matmul(ya_in, wpa, tk=di, name="mm_ya")
    u_blk = c_u // d
    cv = _conv31_fwd(proj, cf_w, cf_conv_b, l, d, u_blk)
    cf = _ln_fwd(cv, cf_ln_g, cf_ln_b)
    y_b = _matmul(cf, wpb, name="mm_yb")
    ga_blk = c_ga // d
    merged = _merge_fwd(y_a, y_b, proj, ga_blk)
    mix = _matmul(merged, wout, name="mm_mix")
    x1, hx2 = _res_fwd(x2, mix, g1, norm_ffn, sc2, sh2)
    gu = _matmul(hx2, wgu.T, tn=_tile(df, 1024, LANES), name="mm_gu")
    act = _swiglu_fwd(gu, df)
    dn = _matmul(act, wdn, tk=df, name="mm_dn")
    loss, dx1, ddn, dg2, d_norm_final = _loss_and_grads(x1, dn, g2, row(norm_final), tgt)

    dact = _matmul(ddn, wdn.T, tn=_tile(df, 1024, LANES), name="mm_dact")
    dw_dn = _matmul(act, ddn, ta=True, tn=d, tk=1024, name="mm_dw_dn")
    dgu = _swiglu_bwd(gu, dact, df)
    dhx2 = _matmul(dgu, wgu, tk=df, name="mm_dhx2")
    dw_gu = _matmul(dgu, hx2, ta=True, tn=d, tk=1024, name="mm_dw_gu")
    dx_res, dmix, dg1, d_norm_ffn, dsc2, dsh2 = _res_bwd(x2, mix, g1, norm_ffn, sc2, sh2, dx1, dhx2)
    dmerged = _matmul(dmix, wout, tb=True, name="mm_dmerged")
    dw_out = _matmul(merged, dmix, ta=True, tn=d, tk=1024, name="mm_dw_out")
    dya, dyb, dga, dgb = _merge_bwd(y_a, y_b, proj, ga_blk, dmerged, lc)
    dcf = _matmul(dyb, wpb, tb=True, name="mm_dcf")
    dw_pb = _matmul(cf, dyb, ta=True, tn=d, tk=1024, name="mm_dw_pb")
    dcv, d_ln_g, d_ln_b = _ln_bwd(cv, cf_ln_g, cf_ln_b, dcf)
    du, dv, d_cf_w, d_cf_b = _conv31_bwd(proj, cf_w, dcv, l, lc, d, u_blk)
    dya_in = _matmul(dya, wpa, tb=True, tn=_tile(di, 1024, LANES), name="mm_dya_in")
    dw_pa = _matmul(ya_in, dya, ta=True, tn=d, tk=1024, name="mm_dw_pa")
    dy, dz, ddsk, d_ssm_norm = _gate_bwd(y_f, y_r, xbc, proj, dsk, ssm_norm, dya_in, l, lc, di)
    dxs_f, db_f, dc_f, ddt_f, dda_f = _ssd_bwd(xbc, cols, cs_t, a_cols, dy, hp_f, l, lc, di, p, False, dsk=dsk)
    dxs, db, dc, ddt_r, dda_r = _ssd_bwd(xbc, cols, cs_t, a_cols, dy, hp_r, l, lc, di, p, True, prev=(dxs_f, db_f, dc_f))
    dxs_raw, dcw_x, dcb_x = _conv5_bwd(proj, conv_w, ssm_conv_b, [dxs], l, lc, c_xbc, 0, di)
    db_raw, dcw_b, dcb_b = _conv5_bwd(proj, conv_w, ssm_conv_b, [db], l, lc, c_xbc, di, g * n)
    dc_raw, dcw_c, dcb_c = _conv5_bwd(proj, conv_w, ssm_conv_b, [dc], l, lc, c_xbc, di + g * n, g * n)
    d_conv_w = jnp.concatenate([dcw_x, dcw_b, dcw_c], axis=1)
    d_conv_b = jnp.concatenate([dcb_x, dcb_b, dcb_c], axis=1)
    heads = lambda f, r: jnp.concatenate([t[:, :, :hpg].transpose(1, 0, 2).reshape(t_all, h) for t in (f, r)], axis=1)
    ddt_raw, d_dt_bias, dda_dt = _dt_bwd(proj, dt_bias.reshape(1, 2 * h), dt, heads(ddt_f, ddt_r), heads(dda_f, dda_r), c_dt // LANES)
    d_a_log = dda_dt * a
    dproj = jnp.concatenate([dz, du, dv, dga, dgb, dxs_raw, db_raw, dc_raw, ddt_raw], axis=1)
    wire, own = _reduce_scatter_begin([dw_gu[:df], dw_gu[df:], dw_pa, dw_pb, dw_out, dw_dn], "a")
    dhx, *recv = _matmul(dproj, win_work, tm=768, tn=1024, tk=_tile(win_work.shape[0], 4096, LANES), name="mm_dhx",
                         side=_scatter_side(wire))
    g_gate_t, g_up_t, g_pa, g_pb, g_out, g_dn = _reduce_scatter_end(own, recv, "a")
    dw_in_work = _matmul(dproj, hx, ta=True, tm=640, tn=d, tk=768, name="mm_dw_in")
    grad_x, d_norm_mix, dsc1, dsh1, dcsc1, dcsh1 = _mod_bwd(x2, ctx2, nm, sc1, sh1, csc1, csh1, dhx, dx_res)

    dw_in_t = jnp.concatenate([dw_in_work[:c_u], dw_in_work[c_xbc:], dw_in_work[c_u:c_xbc]], axis=0)
    wire, own = _reduce_scatter_begin([dw_in_t], "b")
    (g_in_t,) = _reduce_scatter_end(own, _run_side(_scatter_side(wire), "rs_scatter_b"), "b")
    g_in, g_gate, g_up = g_in_t.T, g_gate_t.T, g_up_t.T

    zeros_d = jnp.zeros((1, d), F32)
    dm_lat = jnp.concatenate([dsh1, dsc1, dg1, dsh2, dsc2, dg2], axis=1)
    dm_ctx = jnp.concatenate([dcsh1, dcsc1] + [zeros_d] * 4, axis=1)
    d_d_skip = ddsk.reshape(h, p).sum(axis=1)
    replicated = [dm_lat + dm_ctx, d_norm_mix, d_conv_b, d_dt_bias, d_a_log, d_d_skip, d_ssm_norm, d_cf_b, d_ln_g, d_ln_b,
                  d_norm_ffn, d_norm_final]
    rep_w = [b_mod, norm_mix, ssm_conv_b, dt_bias, a_log, d_skip, ssm_norm, cf_conv_b, cf_ln_g, cf_ln_b, norm_ffn, norm_final]
    rep_m = [m_b_mod, m_norm_mix, m_ssm_conv_b, m_dt_bias, m_a_log, m_d_skip, m_ssm_norm, m_cf_conv_b, m_cf_ln_g, m_cf_ln_b,
             m_norm_ffn, m_norm_final]
    rep_v = [v_b_mod, v_norm_mix, v_ssm_conv_b, v_dt_bias, v_a_log, v_d_skip, v_ssm_norm, v_cf_conv_b, v_cf_ln_g, v_cf_ln_b,
             v_norm_ffn, v_norm_final]
    quantum = 8 * LANES
    rep_flat = _pack(replicated, quantum)
    n_rep = rep_flat.shape[0]
    summed_parts = [rep_flat, _pack([d_conv_w, d_cf_w, dm_ctx], quantum)]
    n_sum = n_rep + summed_parts[1].shape[0]
    everything = jnp.concatenate(summed_parts + [_pack([dm_lat], quantum)])
    gathered = _allgather_small(everything.reshape(8, -1), "ag_small_grads")
    w8 = gathered.shape[1]
    summed = _sum_devices(gathered).reshape(-1)
    dm_lat_all = gathered.reshape(N_DEV, 8 * w8)[:, n_sum:n_sum + 6 * d]
    off = n_rep
    g_conv_w_full = summed[off:off + k5 * conv_dim].reshape(k5, conv_dim)
    off += k5 * conv_dim
    g_cf_w_full = summed[off:off + k31 * d].reshape(k31, d)
    off += k31 * d
    dm_ctx_all = summed[off:off + 6 * d].reshape(1, 6 * d)
    g_conv_w = lax.dynamic_slice(g_conv_w_full, (0, chip * cw_shard.shape[1]), cw_shard.shape)
    g_cf_w = lax.dynamic_slice(g_cf_w_full, (0, chip * cfw_shard.shape[1]), cfw_shard.shape)

    dm_rows = jnp.concatenate([dm_lat_all, dm_ctx_all, jnp.zeros((7, 6 * d), F32)], axis=0)
    dm_mine = lax.dynamic_slice(dm_rows, (0, chip * ws), (16, ws))
    g_w_mod, ds_part = _ada_bwd(s_rows.T, w_mod[0], dm_mine)
    ds_all = _allgather_small(ds_part[8:16], "ag_c_ctx")
    g_c_ctx = _c_ctx_grad(ds_all, row(c_ctx))

    grads, deltas, new_ms, new_vs = {}, {}, {}, {}

    def update(name, w2, g2, m2, v2, shape):
        dl, mm, vv = _adamw(w2, g2, m2, v2, f"adamw_{name}")
        grads[name], deltas[name], new_ms[name], new_vs[name] = (t.reshape(shape) for t in (g2, dl, mm, vv))

    for name, w_, g_, m_, v_ in [
            ("w_mod", w_mod, g_w_mod, m_w_mod, v_w_mod), ("w_in", w_in, g_in, m_w_in, v_w_in),
            ("ssm_conv_w", ssm_conv_w, g_conv_w, m_ssm_conv_w, v_ssm_conv_w),
            ("cf_conv_w", cf_conv_w, g_cf_w, m_cf_conv_w, v_cf_conv_w),
            ("w_proj_a", w_proj_a, g_pa, m_w_proj_a, v_w_proj_a), ("w_proj_b", w_proj_b, g_pb, m_w_proj_b, v_w_proj_b),
            ("w_out", w_out, g_out, m_w_out, v_w_out), ("w_ffn_gate", w_ffn_gate, g_gate, m_w_ffn_gate, v_w_ffn_gate),
            ("w_ffn_up", w_ffn_up, g_up, m_w_ffn_up, v_w_ffn_up), ("w_ffn_down", w_ffn_down, g_dn, m_w_ffn_down, v_w_ffn_down)]:
        update(name, w_[0], g_, m_[0], v_[0], w_.shape)
    update("c_ctx", row(c_ctx), g_c_ctx, row(m_c_ctx), row(v_c_ctx), c_ctx.shape)

    rep_names = ["b_mod", "norm_mix", "ssm_conv_b", "dt_bias", "a_log", "d_skip", "ssm_norm", "cf_conv_b", "cf_ln_g", "cf_ln_b",
                 "norm_ffn", "norm_final"]
    as8 = lambda vs: _pack(vs, quantum).reshape(8, -1)
    g8 = summed[:n_rep].reshape(8, -1)
    d8, m8, v8 = _adamw(as8(rep_w), g8, as8(rep_m), as8(rep_v), "adamw_replicated")
    off = 0
    for name, w_ in zip(rep_names, rep_w):
        size = w_.size
        for store, packed8 in ((grads, g8), (deltas, d8), (new_ms, m8), (new_vs, v8)):
            store[name] = packed8.reshape(-1)[off:off + size].reshape(w_.shape)
        off += size

    order = ["c_ctx", "w_mod", "b_mod", "norm_mix", "w_in", "ssm_conv_w", "ssm_conv_b", "dt_bias", "a_log", "d_skip", "ssm_norm",
             "cf_conv_w", "cf_conv_b", "cf_ln_g", "cf_ln_b", "w_proj_a", "w_proj_b", "w_out", "norm_ffn", "w_ffn_gate", "w_ffn_up",
             "w_ffn_down", "norm_final"]
    total_loss = lax.psum(loss[0, 0], ("x", "y", "c"))
    return (total_loss, grad_x.reshape(x.shape), *[grads[k] for k in order], *[deltas[k] for k in order],
            *[new_ms[k] for k in order], *[new_vs[k] for k in order])
```

```python
import functools

import jax
import jax.numpy as jnp
from jax import lax
from jax.experimental import pallas as pl
from jax.experimental.pallas import tpu as pltpu

F32 = jnp.float32
MXU_DTYPE = jnp.bfloat16
WIRE_DTYPE = jnp.bfloat16
HI = lax.Precision.HIGHEST
EPS = 1e-6
SSM_GROUPS = 8
SSM_STATE = 128
CHUNK = 128
GRID_W = 64
LANES = 128
HEAD_COLS = 16
VMEM_LIMIT = 52 * 1024 * 1024
ADAM_LR, ADAM_B1, ADAM_B2, ADAM_EPS, ADAM_WD, ADAM_STEP = 0.001, 0.9, 0.999, 1e-08, 0.01, 10
MESH = pl.DeviceIdType.MESH
N_SHARD = 4
N_DEV = 8


def _cp(sem=None):
    kw = dict(vmem_limit_bytes=VMEM_LIMIT)
    if sem is not None:
        kw["dimension_semantics"] = sem
    return pltpu.CompilerParams(**kw)


def _tile(n, target, q):
    best = None
    for t in range(q, min(n, target) + 1, q):
        if n % t == 0:
            best = t
    return best if best is not None else n


def _acc(ref, val, i):
    @pl.when(i == 0)
    def _():
        ref[...] = val

    @pl.when(i > 0)
    def _():
        ref[...] += val


def _bc_spec(w):
    return pl.BlockSpec((1, w), lambda *_: (0, 0))


def _rms(x, w):
    return x * lax.rsqrt(jnp.mean(x * x, axis=-1, keepdims=True) + EPS) * w


def _silu(x):
    return x * jax.nn.sigmoid(x)


def _f_mod(x, w, sc, sh):
    return _rms(x, w) * (1.0 + sc) + sh


def _f_gate(yf, yr, xs, z, dsk, wn):
    return _rms((yf + yr + dsk * xs) * _silu(z), wn)


def _f_ln(cv, g, b):
    mu = jnp.mean(cv, axis=-1, keepdims=True)
    xc = cv - mu
    var = jnp.mean(xc * xc, axis=-1, keepdims=True)
    return _silu(xc * lax.rsqrt(var + EPS) * g + b)


def _f_merge(ya, yb, ga, gb):
    return jax.nn.sigmoid(ga) * ya + jax.nn.sigmoid(gb) * yb


def _f_res(x, mix, g1, wn, sc2, sh2):
    x1 = x + g1 * mix
    return x1, _rms(x1, wn) * (1.0 + sc2) + sh2


def _f_swiglu(gt, up):
    return _silu(gt) * up


def _f_loss(x1, dn, g2, wn, tgt):
    out = _rms(x1 + g2 * dn, wn)
    err = out - tgt
    per_tok = jnp.mean(err * err, axis=-1, keepdims=True)
    return 0.5 * jnp.sum(per_tok, axis=0, keepdims=True)


def _matmul(a, b, *, ta=False, tb=False, out_dtype=F32, tm=1024, tn=512, tk=2048, name, side=None, n_outer=False):
    m, k = (a.shape[1], a.shape[0]) if ta else a.shape
    n = b.shape[0] if tb else b.shape[1]
    assert (b.shape[1] if tb else b.shape[0]) == k, (a.shape, b.shape, ta, tb)
    tm, tn, tk = _tile(m, tm, LANES if ta else 16), _tile(n, tn, LANES), _tile(k, tk, LANES)
    grid = (n // tn, m // tm, k // tk) if n_outer else (m // tm, n // tn, k // tk)
    ij = (lambda g0, g1: (g1, g0)) if n_outer else (lambda g0, g1: (g0, g1))
    nk = grid[2]
    dims = (((0 if ta else 1,), (1 if tb else 0,)), ((), ()))
    n_in = len(side.inputs) if side else 0
    n_out = len(side.out_shapes) if side else 0

    def body(a_ref, b_ref, *rest):
        side_in, o_ref, side_out, scratch = rest[:n_in], rest[n_in], rest[n_in + 1:n_in + 1 + n_out], rest[n_in + 1 + n_out:]
        steps = [pl.program_id(axis) for axis in range(3)]
        if side:
            @pl.when((steps[0] == 0) & (steps[1] == 0) & (steps[2] == 0))
            def _():
                side.start(side_in, side_out, *scratch[-2:])

        prod = lax.dot_general(a_ref[...].astype(MXU_DTYPE), b_ref[...].astype(MXU_DTYPE), dims,
                               preferred_element_type=F32)
        if nk == 1:
            o_ref[...] = prod.astype(o_ref.dtype)
        else:
            acc = scratch[0]
            _acc(acc, prod, steps[2])

            @pl.when(steps[2] == nk - 1)
            def _():
                o_ref[...] = acc[...].astype(o_ref.dtype)

        if side:
            @pl.when((steps[0] == grid[0] - 1) & (steps[1] == grid[1] - 1) & (steps[2] == nk - 1))
            def _():
                side.finish(side_in, side_out, *scratch[-2:])

    a_spec = (pl.BlockSpec((tk, tm), lambda g0, g1, kk: (kk, ij(g0, g1)[0])) if ta
              else pl.BlockSpec((tm, tk), lambda g0, g1, kk: (ij(g0, g1)[0], kk)))
    b_spec = (pl.BlockSpec((tn, tk), lambda g0, g1, kk: (ij(g0, g1)[1], kk)) if tb
              else pl.BlockSpec((tk, tn), lambda g0, g1, kk: (kk, ij(g0, g1)[1])))
    out = pl.pallas_call(
        body, name=name, grid=grid, in_specs=[a_spec, b_spec] + [_HBM] * n_in,
        out_specs=[pl.BlockSpec((tm, tn), lambda g0, g1, kk: ij(g0, g1))] + [_HBM] * n_out,
        out_shape=[jax.ShapeDtypeStruct((m, n), out_dtype)] + (side.out_shapes if side else []),
        scratch_shapes=([] if nk == 1 else [pltpu.VMEM((tm, tn), F32)]) + (side.scratch() if side else []),
        compiler_params=_cp(("arbitrary",) * 3 if side else ("parallel", "parallel", "arbitrary")),
    )(a, b, *(side.inputs if side else []))
    return out if side else out[0]


def _mesh_pos():
    return lax.axis_index("x"), lax.axis_index("y"), lax.axis_index("c")


def _other_chips(x, y):
    return [(1 - x, y), (x, 1 - y), (1 - x, 1 - y)]


def _allgather_small(v, name):
    m_per, n = v.shape

    def body(x_ref, out_ref, send_sems, recv_sems, local_sem):
        x, y, c = _mesh_pos()
        me, sibling = (x, y, c), (x, y, 1 - c)
        chips = _other_chips(x, y)

        def rows(px, py, pc):
            return out_ref.at[pl.ds((4 * px + 2 * py + pc) * m_per, m_per), :]

        def copy(k, block, to, src=None):
            return pltpu.make_async_remote_copy(
                src_ref=rows(*block) if src is None else src, dst_ref=rows(*block),
                send_sem=send_sems.at[k], recv_sem=recv_sems.at[k], device_id=to, device_id_type=MESH)

        mine = pltpu.make_async_copy(x_ref, rows(*me), local_sem)
        mine.start()
        first = [copy(0, me, sibling, src=x_ref)]
        first += [copy(1 + j, me, (*chip, c), src=x_ref) for j, chip in enumerate(chips)]
        for cp in first:
            cp.start()
        passed = [copy(4 + j, (*chip, c), sibling) for j, chip in enumerate(chips)]
        for j, chip in enumerate(chips):
            copy(1 + j, (*chip, c), me).wait_recv()
            passed[j].start()
        copy(0, sibling, me).wait_recv()
        for j, chip in enumerate(chips):
            copy(4 + j, (*chip, 1 - c), me).wait_recv()
        for cp in first + passed:
            cp.wait_send()
        mine.wait()

    return pl.pallas_call(
        body, name=name, out_shape=jax.ShapeDtypeStruct((N_DEV * m_per, n), v.dtype),
        in_specs=[pl.BlockSpec(memory_space=pltpu.VMEM)], out_specs=pl.BlockSpec(memory_space=pltpu.VMEM),
        scratch_shapes=[pltpu.SemaphoreType.DMA((7,)), pltpu.SemaphoreType.DMA((7,)), pltpu.SemaphoreType.DMA],
        compiler_params=_cp(),
    )(v)


_HBM = pl.BlockSpec(memory_space=pltpu.HBM)


class _Side:
    def __init__(self, inputs, out_shapes, n_sems, start, finish):
        self.inputs, self.out_shapes, self.n_sems, self.start, self.finish = inputs, out_shapes, n_sems, start, finish

    def scratch(self):
        return [pltpu.SemaphoreType.DMA((self.n_sems,)), pltpu.SemaphoreType.DMA((self.n_sems,))]


def _run_side(side, name):
    n_in = len(side.inputs)

    def body(*refs):
        src, dst, sems = refs[:n_in], refs[n_in:-2], refs[-2:]
        side.start(src, dst, *sems)
        side.finish(src, dst, *sems)

    return pl.pallas_call(
        body, name=name, out_shape=side.out_shapes, in_specs=[_HBM] * n_in, out_specs=[_HBM] * len(side.out_shapes),
        scratch_shapes=side.scratch(), compiler_params=_cp(),
    )(*side.inputs)


def _gather_side(shards):
    n = len(shards)

    def plan(src, dst, send_sems, recv_sems):
        x, y, c = _mesh_pos()
        chips = _other_chips(x, y)

        def half(i, px, py, pc):
            r = src[i].shape[0]
            return dst[i].at[pl.ds(pl.multiple_of((2 * px + py) * r + pc * (r // 2), 16), r // 2), :]

        def copy(i, k, block, to, own=False):
            r = src[i].shape[0]
            mine = src[i].at[pl.ds(pl.multiple_of(c * (r // 2), 16), r // 2), :]
            return pltpu.make_async_remote_copy(
                src_ref=mine if own else half(i, *block), dst_ref=half(i, *block), send_sem=send_sems.at[6 * i + k],
                recv_sem=recv_sems.at[6 * i + k], device_id=to, device_id_type=MESH)

        first = [copy(i, j, (x, y, c), (*chip, c), own=True) for i in range(n) for j, chip in enumerate(chips)]
        return (x, y, c), chips, copy, first

    def start(src, dst, send_sems, recv_sems):
        for cp in plan(src, dst, send_sems, recv_sems)[3]:
            cp.start()

    def finish(src, dst, send_sems, recv_sems):
        (x, y, c), chips, copy, first = plan(src, dst, send_sems, recv_sems)
        passed = []
        for i in range(n):
            for j, chip in enumerate(chips):
                copy(i, j, (*chip, c), (x, y, c)).wait_recv()
                passed.append(copy(i, 3 + j, (*chip, c), (x, y, 1 - c)))
                passed[-1].start()
        for i in range(n):
            for j, chip in enumerate(chips):
                copy(i, 3 + j, (*chip, 1 - c), (x, y, c)).wait_recv()
        for cp in first + passed:
            cp.wait_send()

    shapes = [jax.ShapeDtypeStruct((N_SHARD * s.shape[0], s.shape[1]), s.dtype) for s in shards]
    return _Side(list(shards), shapes, 6 * n, start, finish)


def _fill_own_rows(gathered, shards):
    chip = 2 * lax.axis_index("x") + lax.axis_index("y")
    return [lax.dynamic_update_slice(full, s, (chip * s.shape[0], 0)) for full, s in zip(gathered, shards)]


def _swap_halves(parts, name):
    n = len(parts)

    def body(*refs):
        src, dst = refs[:n], refs[n:2 * n]
        send_sems, recv_sems = refs[2 * n:]
        x, y, c = _mesh_pos()
        copies = [pltpu.make_async_remote_copy(
            src_ref=src[i].at[s, 1 - c], dst_ref=dst[i].at[s], send_sem=send_sems.at[N_SHARD * i + s],
            recv_sem=recv_sems.at[N_SHARD * i + s], device_id=(x, y, 1 - c), device_id_type=MESH)
            for i in range(n) for s in range(N_SHARD)]
        for cp in copies:
            cp.start()
        for cp in copies:
            cp.wait()

    return pl.pallas_call(
        body, name=name,
        out_shape=[jax.ShapeDtypeStruct((N_SHARD,) + p.shape[2:], p.dtype) for p in parts],
        in_specs=[_HBM] * n, out_specs=[_HBM] * n,
        scratch_shapes=[pltpu.SemaphoreType.DMA((N_SHARD * n,)), pltpu.SemaphoreType.DMA((N_SHARD * n,))],
        compiler_params=_cp(),
    )(*parts)


def _scatter_side(parts):
    n = len(parts)

    def copies(src, dst, send_sems, recv_sems):
        x, y, c = _mesh_pos()
        return [pltpu.make_async_remote_copy(
            src_ref=src[i].at[2 * chip[0] + chip[1]], dst_ref=dst[i].at[j], send_sem=send_sems.at[3 * i + j],
            recv_sem=recv_sems.at[3 * i + j], device_id=(*chip, c), device_id_type=MESH)
            for i in range(n) for j, chip in enumerate(_other_chips(x, y))]

    def start(*refs):
        for cp in copies(*refs):
            cp.start()

    def finish(*refs):
        for cp in copies(*refs):
            cp.wait()

    shapes = [jax.ShapeDtypeStruct((3,) + p.shape[1:], p.dtype) for p in parts]
    return _Side(list(parts), shapes, 3 * n, start, finish)


def _join_halves(halves, name):
    n = len(halves)

    def body(*refs):
        src, dst = refs[:n], refs[n:2 * n]
        send_sems, recv_sems = refs[2 * n:]
        x, y, c = _mesh_pos()
        remote = [pltpu.make_async_remote_copy(
            src_ref=src[i], dst_ref=dst[i].at[c], send_sem=send_sems.at[i], recv_sem=recv_sems.at[i],
            device_id=(x, y, 1 - c), device_id_type=MESH) for i in range(n)]
        for cp in remote:
            cp.start()
        for i in range(n):
            pltpu.make_async_remote_copy(
                src_ref=src[i], dst_ref=dst[i].at[1 - c], send_sem=send_sems.at[i], recv_sem=recv_sems.at[i],
                device_id=(x, y, 1 - c), device_id_type=MESH).wait_recv()
        for cp in remote:
            cp.wait_send()

    joined = pl.pallas_call(
        body, name=name,
        out_shape=[jax.ShapeDtypeStruct((2,) + h.shape, h.dtype) for h in halves],
        in_specs=[_HBM] * n, out_specs=[_HBM] * n,
        scratch_shapes=[pltpu.SemaphoreType.DMA((n,)), pltpu.SemaphoreType.DMA((n,))],
        compiler_params=_cp(),
    )(*halves)
    c = lax.axis_index("c")
    return [lax.dynamic_update_slice(j, h[None], (c, 0, 0)) for j, h in zip(joined, halves)]


def _pair_sum(g, got, name):
    _, _, hr, d = g.shape
    t = _tile(hr, 256, 16)

    def body(g0_ref, g1_ref, got_ref, wire_ref, own_ref):
        x, y, c = _mesh_pos()
        total = jnp.where(c == 0, g0_ref[0, 0], g1_ref[0, 0]) + got_ref[0]
        wire_ref[0] = total.astype(wire_ref.dtype)

        @pl.when(pl.program_id(1) == 2 * x + y)
        def _():
            own_ref[...] = total

    return pl.pallas_call(
        body, name=name, grid=(hr // t, N_SHARD),
        in_specs=[pl.BlockSpec((1, 1, t, d), lambda i, s: (s, 0, i, 0)), pl.BlockSpec((1, 1, t, d), lambda i, s: (s, 1, i, 0)),
                  pl.BlockSpec((1, t, d), lambda i, s: (s, i, 0))],
        out_specs=[pl.BlockSpec((1, t, d), lambda i, s: (s, i, 0)), pl.BlockSpec((t, d), lambda i, s: (i, 0))],
        out_shape=[jax.ShapeDtypeStruct((N_SHARD, hr, d), WIRE_DTYPE), jax.ShapeDtypeStruct((hr, d), F32)],
        compiler_params=_cp(("parallel", "arbitrary")),
    )(g, g, got)


def _sum_partials(own, recv, name):
    hr, d = own.shape
    t = _tile(hr, 256, 16)

    def body(own_ref, recv_ref, o_ref):
        total = own_ref[...]
        for j in range(3):
            total = total + recv_ref[j].astype(F32)
        o_ref[...] = total

    blk = pl.BlockSpec((t, d), lambda i: (i, 0))
    return pl.pallas_call(
        body, name=name, grid=(hr // t,), in_specs=[blk, pl.BlockSpec((3, t, d), lambda i: (0, i, 0))], out_specs=blk,
        out_shape=jax.ShapeDtypeStruct((hr, d), F32), compiler_params=_cp(("parallel",)),
    )(own, recv)


def _reduce_scatter_begin(grads, tag):
    split = [g.reshape(N_SHARD, 2, g.shape[0] // (2 * N_SHARD), g.shape[1]) for g in grads]
    got = _swap_halves(split, f"rs_swap_halves_{tag}")
    sums = [_pair_sum(g, h, f"rs_pair_sum_{tag}{i}") for i, (g, h) in enumerate(zip(split, got))]
    return [w for w, _ in sums], [own for _, own in sums]


def _reduce_scatter_end(own, recv, tag):
    halves = [_sum_partials(o, rv, f"rs_sum_{tag}{i}") for i, (o, rv) in enumerate(zip(own, recv))]
    return [j.reshape(-1, j.shape[-1]) for j in _join_halves(halves, f"rs_join_halves_{tag}")]


def _mod_fwd(x, ctx, nw, sc, sh, csc, csh):
    l, d = x.shape
    lc = ctx.shape[0]
    t = min(256, lc)
    nl, nc = l // t, lc // t

    def body(x_ref, c_ref, nw_ref, sc_ref, sh_ref, csc_ref, csh_ref, o_ref):
        i = pl.program_id(0)

        @pl.when(i < nl)
        def _():
            o_ref[...] = _f_mod(x_ref[...], nw_ref[...], sc_ref[...], sh_ref[...]).astype(o_ref.dtype)

        @pl.when(i >= nl)
        def _():
            o_ref[...] = _f_mod(c_ref[...], nw_ref[...], csc_ref[...], csh_ref[...]).astype(o_ref.dtype)

    return pl.pallas_call(
        body, name="mod_fwd", grid=(nl + nc,),
        in_specs=[pl.BlockSpec((t, d), lambda i: (jnp.minimum(i, nl - 1), 0)),
                  pl.BlockSpec((t, d), lambda i: (jnp.maximum(i - nl, 0), 0))] + [_bc_spec(d)] * 5,
        out_specs=pl.BlockSpec((t, d), lambda i: (i, 0)),
        out_shape=jax.ShapeDtypeStruct((l + lc, d), MXU_DTYPE), compiler_params=_cp(("arbitrary",)),
    )(x, ctx, nw, sc, sh, csc, csh)


def _mod_bwd(x, ctx, nw, sc, sh, csc, csh, dhx, dx_res):
    l, d = x.shape
    lc = ctx.shape[0]
    t = min(256, lc)
    nl, nc = l // t, lc // t

    def body(x_ref, c_ref, nw_ref, sc_ref, sh_ref, csc_ref, csh_ref, dh_ref, dr_ref,
             dx_ref, dnw_ref, dsc_ref, dsh_ref, dcsc_ref, dcsh_ref):
        i = pl.program_id(0)

        @pl.when(i == 0)
        def _():
            for r in (dnw_ref, dsc_ref, dsh_ref, dcsc_ref, dcsh_ref):
                r[...] = jnp.zeros_like(r)

        @pl.when(i < nl)
        def _():
            _, vjp = jax.vjp(_f_mod, x_ref[...], nw_ref[...], sc_ref[...], sh_ref[...])
            dx, dnw, dsc, dsh = vjp(dh_ref[...])
            dx_ref[...] = dx + dr_ref[...]
            dnw_ref[...] += dnw
            dsc_ref[...] += dsc
            dsh_ref[...] += dsh

        @pl.when(i >= nl)
        def _():
            _, vjp = jax.vjp(_f_mod, c_ref[...], nw_ref[...], csc_ref[...], csh_ref[...])
            _, dnw, dsc, dsh = vjp(dh_ref[...])
            dnw_ref[...] += dnw
            dcsc_ref[...] += dsc
            dcsh_ref[...] += dsh

    lat = pl.BlockSpec((t, d), lambda i: (jnp.minimum(i, nl - 1), 0))
    vec = jax.ShapeDtypeStruct((1, d), F32)
    return pl.pallas_call(
        body, name="mod_bwd", grid=(nl + nc,),
        in_specs=[lat, pl.BlockSpec((t, d), lambda i: (jnp.maximum(i - nl, 0), 0))] + [_bc_spec(d)] * 5
        + [pl.BlockSpec((t, d), lambda i: (i, 0)), lat],
        out_specs=[lat] + [_bc_spec(d)] * 5,
        out_shape=[jax.ShapeDtypeStruct((l, d), F32)] + [vec] * 5, compiler_params=_cp(("arbitrary",)),
    )(x, ctx, nw, sc, sh, csc, csh, dhx, dx_res)


def _gate_fwd(yf, yr, xbc, proj, dsk, wn, l, di):
    t = 128

    def body(yf_ref, yr_ref, xs_ref, z_ref, dsk_ref, wn_ref, o_ref):
        o_ref[...] = _f_gate(yf_ref[...], yr_ref[...], xs_ref[...], z_ref[...], dsk_ref[...], wn_ref[...]).astype(o_ref.dtype)

    row = pl.BlockSpec((t, di), lambda i: (i, 0))
    return pl.pallas_call(
        body, name="gate_fwd", grid=(l // t,), in_specs=[row] * 4 + [_bc_spec(di)] * 2, out_specs=row,
        out_shape=jax.ShapeDtypeStruct((l, di), MXU_DTYPE), compiler_params=_cp(("parallel",)),
    )(yf, yr, xbc, proj, dsk, wn)


def _gate_bwd(yf, yr, xbc, proj, dsk, wn, dya, l, lc, di):
    t = 128
    nl, nc = l // t, lc // t

    def body(yf_ref, yr_ref, xs_ref, z_ref, dsk_ref, wn_ref, g_ref, dy_ref, dz_ref, ddsk_ref, dwn_ref):
        i = pl.program_id(0)

        @pl.when(i == 0)
        def _():
            ddsk_ref[...] = jnp.zeros_like(ddsk_ref)
            dwn_ref[...] = jnp.zeros_like(dwn_ref)

        @pl.when(i < nl)
        def _():
            _, vjp = jax.vjp(_f_gate, yf_ref[...], yr_ref[...], xs_ref[...], z_ref[...], dsk_ref[...], wn_ref[...])
            dyf, _, _, dz, ddsk, dwn = vjp(g_ref[...])
            dy_ref[...] = dyf
            dz_ref[...] = dz.astype(dz_ref.dtype)
            ddsk_ref[...] += ddsk
            dwn_ref[...] += dwn

        @pl.when(i >= nl)
        def _():
            dz_ref[...] = jnp.zeros_like(dz_ref)

    lat = pl.BlockSpec((t, di), lambda i: (jnp.minimum(i, nl - 1), 0))
    vec = jax.ShapeDtypeStruct((1, di), F32)
    return pl.pallas_call(
        body, name="gate_bwd", grid=(nl + nc,),
        in_specs=[lat] * 4 + [_bc_spec(di)] * 2 + [lat],
        out_specs=[lat, pl.BlockSpec((t, di), lambda i: (i, 0))] + [_bc_spec(di)] * 2,
        out_shape=[jax.ShapeDtypeStruct((l, di), F32), jax.ShapeDtypeStruct((l + lc, di), MXU_DTYPE)] + [vec] * 2,
        compiler_params=_cp(("arbitrary",)),
    )(yf, yr, xbc, proj, dsk, wn, dya)


def _ln_fwd(cv, g, b):
    l, d = cv.shape
    t = 256

    def body(cv_ref, g_ref, b_ref, o_ref):
        o_ref[...] = _f_ln(cv_ref[...], g_ref[...], b_ref[...]).astype(o_ref.dtype)

    row = pl.BlockSpec((t, d), lambda i: (i, 0))
    return pl.pallas_call(
        body, name="ln_fwd", grid=(l // t,), in_specs=[row] + [_bc_spec(d)] * 2, out_specs=row,
        out_shape=jax.ShapeDtypeStruct((l, d), MXU_DTYPE), compiler_params=_cp(("parallel",)),
    )(cv, g, b)


def _ln_bwd(cv, g, b, dcf):
    l, d = cv.shape
    t = 256

    def body(cv_ref, g_ref, b_ref, dcf_ref, dcv_ref, dg_ref, db_ref):
        _, vjp = jax.vjp(_f_ln, cv_ref[...], g_ref[...], b_ref[...])
        dcv, dg, db = vjp(dcf_ref[...])
        dcv_ref[...] = dcv
        i = pl.program_id(0)
        _acc(dg_ref, dg, i)
        _acc(db_ref, db, i)

    row = pl.BlockSpec((t, d), lambda i: (i, 0))
    vec = jax.ShapeDtypeStruct((1, d), F32)
    return pl.pallas_call(
        body, name="ln_bwd", grid=(l // t,), in_specs=[row] + [_bc_spec(d)] * 2 + [row],
        out_specs=[row] + [_bc_spec(d)] * 2, out_shape=[jax.ShapeDtypeStruct((l, d), F32), vec, vec],
        compiler_params=_cp(("arbitrary",)),
    )(cv, g, b, dcf)


def _merge_fwd(ya, yb, proj, ga_blk):
    l, d = ya.shape
    t = 256

    def body(ya_ref, yb_ref, ga_ref, gb_ref, o_ref):
        o_ref[...] = _f_merge(ya_ref[...], yb_ref[...], ga_ref[...], gb_ref[...]).astype(o_ref.dtype)

    row = pl.BlockSpec((t, d), lambda i: (i, 0))
    return pl.pallas_call(
        body, name="merge_fwd", grid=(l // t,),
        in_specs=[row, row, pl.BlockSpec((t, d), lambda i: (i, ga_blk)), pl.BlockSpec((t, d), lambda i: (i, ga_blk + 1))],
        out_specs=row, out_shape=jax.ShapeDtypeStruct((l, d), MXU_DTYPE), compiler_params=_cp(("parallel",)),
    )(ya, yb, proj, proj)


def _merge_bwd(ya, yb, proj, ga_blk, dmerged, lc):
    l, d = ya.shape
    t = min(256, lc)
    nl, nc = l // t, lc // t

    def body(ya_ref, yb_ref, ga_ref, gb_ref, g_ref, dya_ref, dyb_ref, dga_ref, dgb_ref):
        i = pl.program_id(0)

        @pl.when(i < nl)
        def _():
            _, vjp = jax.vjp(_f_merge, ya_ref[...], yb_ref[...], ga_ref[...], gb_ref[...])
            dya, dyb, dga, dgb = vjp(g_ref[...])
            dya_ref[...] = dya.astype(dya_ref.dtype)
            dyb_ref[...] = dyb.astype(dyb_ref.dtype)
            dga_ref[...] = dga.astype(dga_ref.dtype)
            dgb_ref[...] = dgb.astype(dgb_ref.dtype)

        @pl.when(i >= nl)
        def _():
            dga_ref[...] = jnp.zeros_like(dga_ref)
            dgb_ref[...] = jnp.zeros_like(dgb_ref)

    lat = pl.BlockSpec((t, d), lambda i: (jnp.minimum(i, nl - 1), 0))
    full = pl.BlockSpec((t, d), lambda i: (i, 0))
    return pl.pallas_call(
        body, name="merge_bwd", grid=(nl + nc,),
        in_specs=[lat, lat, pl.BlockSpec((t, d), lambda i: (jnp.minimum(i, nl - 1), ga_blk)),
                  pl.BlockSpec((t, d), lambda i: (jnp.minimum(i, nl - 1), ga_blk + 1)), lat],
        out_specs=[lat, lat, full, full],
        out_shape=[jax.ShapeDtypeStruct((l, d), MXU_DTYPE)] * 2 + [jax.ShapeDtypeStruct((l + lc, d), MXU_DTYPE)] * 2,
        compiler_params=_cp(("arbitrary",)),
    )(ya, yb, proj, proj, dmerged)


def _res_fwd(x, mix, g1, wn, sc2, sh2):
    l, d = x.shape
    t = 256

    def body(x_ref, m_ref, g1_ref, wn_ref, sc_ref, sh_ref, x1_ref, hx_ref):
        x1, hx = _f_res(x_ref[...], m_ref[...], g1_ref[...], wn_ref[...], sc_ref[...], sh_ref[...])
        x1_ref[...] = x1
        hx_ref[...] = hx.astype(hx_ref.dtype)

    row = pl.BlockSpec((t, d), lambda i: (i, 0))
    return pl.pallas_call(
        body, name="res_fwd", grid=(l // t,), in_specs=[row, row] + [_bc_spec(d)] * 4, out_specs=[row, row],
        out_shape=[jax.ShapeDtypeStruct((l, d), F32), jax.ShapeDtypeStruct((l, d), MXU_DTYPE)],
        compiler_params=_cp(("parallel",)),
    )(x, mix, g1, wn, sc2, sh2)


def _res_bwd(x, mix, g1, wn, sc2, sh2, dx1, dhx2):
    l, d = x.shape
    t = 256

    def body(x_ref, m_ref, g1_ref, wn_ref, sc_ref, sh_ref, dx1_ref, dh_ref, dx_ref, dm_ref, dg1_ref, dwn_ref, dsc_ref, dsh_ref):
        _, vjp = jax.vjp(_f_res, x_ref[...], m_ref[...], g1_ref[...], wn_ref[...], sc_ref[...], sh_ref[...])
        dx, dm, dg1, dwn, dsc, dsh = vjp((dx1_ref[...], dh_ref[...]))
        dx_ref[...] = dx
        dm_ref[...] = dm.astype(dm_ref.dtype)
        i = pl.program_id(0)
        _acc(dg1_ref, dg1, i)
        _acc(dwn_ref, dwn, i)
        _acc(dsc_ref, dsc, i)
        _acc(dsh_ref, dsh, i)

    row = pl.BlockSpec((t, d), lambda i: (i, 0))
    vec = jax.ShapeDtypeStruct((1, d), F32)
    return pl.pallas_call(
        body, name="res_bwd", grid=(l // t,), in_specs=[row, row] + [_bc_spec(d)] * 4 + [row, row],
        out_specs=[row, row] + [_bc_spec(d)] * 4,
        out_shape=[jax.ShapeDtypeStruct((l, d), F32), jax.ShapeDtypeStruct((l, d), MXU_DTYPE)] + [vec] * 4,
        compiler_params=_cp(("arbitrary",)),
    )(x, mix, g1, wn, sc2, sh2, dx1, dhx2)


def _swiglu_fwd(gu, df):
    l = gu.shape[0]
    t = 256

    def body(g_ref, u_ref, o_ref):
        o_ref[...] = _f_swiglu(g_ref[...], u_ref[...]).astype(o_ref.dtype)

    return pl.pallas_call(
        body, name="swiglu_fwd", grid=(l // t,),
        in_specs=[pl.BlockSpec((t, df), lambda i: (i, 0)), pl.BlockSpec((t, df), lambda i: (i, 1))],
        out_specs=pl.BlockSpec((t, df), lambda i: (i, 0)),
        out_shape=jax.ShapeDtypeStruct((l, df), MXU_DTYPE), compiler_params=_cp(("parallel",)),
    )(gu, gu)


def _swiglu_bwd(gu, dact, df):
    l = gu.shape[0]
    t = 256

    lo, hi = pl.BlockSpec((t, df), lambda i: (i, 0)), pl.BlockSpec((t, df), lambda i: (i, 1))
    dgu = jax.ShapeDtypeStruct((l, 2 * df), MXU_DTYPE)

    def body(g_ref, u_ref, da_ref, dgu_ref):
        _, vjp = jax.vjp(_f_swiglu, g_ref[...], u_ref[...])
        dg, du = vjp(da_ref[...])
        dgu_ref[:, :df] = dg.astype(dgu_ref.dtype)
        dgu_ref[:, df:] = du.astype(dgu_ref.dtype)

    return pl.pallas_call(
        body, name="swiglu_bwd", grid=(l // t,), in_specs=[lo, hi, lo],
        out_specs=pl.BlockSpec((t, 2 * df), lambda i: (i, 0)), out_shape=dgu, compiler_params=_cp(("parallel",)),
    )(gu, gu, dact)


def _loss_and_grads(x1, dn, g2, wn, tgt):
    l, d = x1.shape
    t = 256

    def body(x1_ref, dn_ref, g2_ref, wn_ref, t_ref, loss_ref, dx_ref, ddn_ref, dg2_ref, dwn_ref):
        loss, vjp = jax.vjp(lambda a, b, c, e: _f_loss(a, b, c, e, t_ref[...]), x1_ref[...], dn_ref[...], g2_ref[...], wn_ref[...])
        dx, ddn, dg2, dwn = vjp(jnp.ones((1, 1), F32))
        dx_ref[...] = dx
        ddn_ref[...] = ddn.astype(ddn_ref.dtype)
        i = pl.program_id(0)
        _acc(loss_ref, loss, i)
        _acc(dg2_ref, dg2, i)
        _acc(dwn_ref, dwn, i)

    row = pl.BlockSpec((t, d), lambda i: (i, 0))
    vec = jax.ShapeDtypeStruct((1, d), F32)
    return pl.pallas_call(
        body, name="loss_and_grads", grid=(l // t,), in_specs=[row, row] + [_bc_spec(d)] * 2 + [row],
        out_specs=[pl.BlockSpec((1, 1), lambda i: (0, 0)), row, row] + [_bc_spec(d)] * 2,
        out_shape=[jax.ShapeDtypeStruct((1, 1), F32), jax.ShapeDtypeStruct((l, d), F32),
                   jax.ShapeDtypeStruct((l, d), MXU_DTYPE), vec, vec],
        compiler_params=_cp(("arbitrary",)),
    )(x1, dn, g2, wn, tgt)


PAD = 8


def _conv5_taps(s_ref, w_ref, l, lc, width):
    half = width // 2
    lat = sum(w_ref[k:k + 1, :] * s_ref[pl.ds(PAD + k - half, l), :] for k in range(width))
    ctx = sum(w_ref[k:k + 1, :] * s_ref[pl.ds(2 * PAD + l + k - half, lc), :] for k in range(width))
    return lat, ctx


def _fill_padded(s_ref, lat, ctx, l, lc):
    zeros = jnp.zeros((PAD, s_ref.shape[1]), F32)
    s_ref[pl.ds(0, PAD), :] = zeros
    s_ref[pl.ds(PAD, l), :] = lat
    s_ref[pl.ds(PAD + l, PAD), :] = zeros
    s_ref[pl.ds(2 * PAD + l, lc), :] = ctx
    s_ref[pl.ds(2 * PAD + l + lc, PAD), :] = zeros


def _conv5_fwd(proj, w, b, l, lc, col0, ncols):
    t_all = l + lc
    cw = LANES
    blk0 = col0 // cw
    width = w.shape[0]

    def body(x_ref, w_ref, b_ref, o_ref, s_ref):
        _fill_padded(s_ref, x_ref[pl.ds(0, l), :], x_ref[pl.ds(l, lc), :], l, lc)
        lat, ctx = _conv5_taps(s_ref, w_ref, l, lc, width)
        o_ref[pl.ds(0, l), :] = _silu(lat + b_ref[...])
        o_ref[pl.ds(l, lc), :] = _silu(ctx + b_ref[...])

    return pl.pallas_call(
        body, name="conv5_fwd", grid=(ncols // cw,),
        in_specs=[pl.BlockSpec((t_all, cw), lambda j: (0, blk0 + j)), pl.BlockSpec((width, cw), lambda j: (0, j)),
                  pl.BlockSpec((1, cw), lambda j: (0, j))],
        out_specs=pl.BlockSpec((t_all, cw), lambda j: (0, j)),
        out_shape=jax.ShapeDtypeStruct((t_all, ncols), F32),
        scratch_shapes=[pltpu.VMEM((t_all + 3 * PAD, cw), F32)], compiler_params=_cp(("parallel",)),
    )(proj, w, b)


def _conv5_bwd(proj, w, b, cots, l, lc, col0, seg0, ncols):
    t_all = l + lc
    cw = LANES
    blk0, sblk0 = col0 // cw, seg0 // cw
    width = w.shape[0]
    half = width // 2
    nc = len(cots)

    def body(*refs):
        x_ref, w_ref, b_ref = refs[:3]
        cot_refs = refs[3:3 + nc]
        dx_ref, dw_ref, db_ref, s_ref = refs[3 + nc:]
        x_lat, x_ctx = x_ref[pl.ds(0, l), :], x_ref[pl.ds(l, lc), :]
        _fill_padded(s_ref, x_lat, x_ctx, l, lc)
        pre_lat, pre_ctx = _conv5_taps(s_ref, w_ref, l, lc, width)
        g = sum(c[...] for c in cot_refs)

        def through_silu(pre, cot):
            _, vjp = jax.vjp(_silu, pre + b_ref[...])
            return vjp(cot)[0]

        d_lat = through_silu(pre_lat, g[:l])
        d_ctx = through_silu(pre_ctx, g[l:])
        db_ref[...] = jnp.sum(d_lat, axis=0, keepdims=True) + jnp.sum(d_ctx, axis=0, keepdims=True)
        for k in range(width):
            dw_ref[k:k + 1, :] = (
                jnp.sum(d_lat * s_ref[pl.ds(PAD + k - half, l), :], axis=0, keepdims=True)
                + jnp.sum(d_ctx * s_ref[pl.ds(2 * PAD + l + k - half, lc), :], axis=0, keepdims=True))
        _fill_padded(s_ref, d_lat, d_ctx, l, lc)
        dx_lat = sum(w_ref[k:k + 1, :] * s_ref[pl.ds(PAD - (k - half), l), :] for k in range(width))
        dx_ctx = sum(w_ref[k:k + 1, :] * s_ref[pl.ds(2 * PAD + l - (k - half), lc), :] for k in range(width))
        dx_ref[pl.ds(0, l), :] = dx_lat.astype(dx_ref.dtype)
        dx_ref[pl.ds(l, lc), :] = dx_ctx.astype(dx_ref.dtype)

    col = pl.BlockSpec((t_all, cw), lambda j: (0, j))
    return pl.pallas_call(
        body, name=f"conv5_bwd_{seg0}", grid=(ncols // cw,),
        in_specs=[pl.BlockSpec((t_all, cw), lambda j: (0, blk0 + sblk0 + j)),
                  pl.BlockSpec((width, cw), lambda j: (0, sblk0 + j)), pl.BlockSpec((1, cw), lambda j: (0, sblk0 + j))]
        + [col] * nc,
        out_specs=[col, pl.BlockSpec((width, cw), lambda j: (0, j)), pl.BlockSpec((1, cw), lambda j: (0, j))],
        out_shape=[jax.ShapeDtypeStruct((t_all, ncols), MXU_DTYPE), jax.ShapeDtypeStruct((width, ncols), F32),
                   jax.ShapeDtypeStruct((1, ncols), F32)],
        scratch_shapes=[pltpu.VMEM((t_all + 3 * PAD, cw), F32)], compiler_params=_cp(("parallel",)),
    )(proj, w, b, *cots)


def _conv31_fwd(proj, w, b, l, d, u_blk):
    cw = LANES
    width = w.shape[0]
    reach = (width // 2) * GRID_W
    nb = d // cw

    def body(u_ref, v_ref, w_ref, b_ref, o_ref, s_ref):
        s_ref[pl.ds(0, reach), :] = jnp.zeros((reach, cw), F32)
        s_ref[pl.ds(reach, l), :] = u_ref[...] * jax.nn.sigmoid(v_ref[...])
        s_ref[pl.ds(reach + l, reach), :] = jnp.zeros((reach, cw), F32)
        o_ref[...] = sum(w_ref[k:k + 1, :] * s_ref[pl.ds(k * GRID_W, l), :] for k in range(width)) + b_ref[...]

    return pl.pallas_call(
        body, name="conv31_fwd", grid=(nb,),
        in_specs=[pl.BlockSpec((l, cw), lambda j: (0, u_blk * nb + j)), pl.BlockSpec((l, cw), lambda j: (0, (u_blk + 1) * nb + j)),
                  pl.BlockSpec((width, cw), lambda j: (0, j)), pl.BlockSpec((1, cw), lambda j: (0, j))],
        out_specs=pl.BlockSpec((l, cw), lambda j: (0, j)), out_shape=jax.ShapeDtypeStruct((l, d), F32),
        scratch_shapes=[pltpu.VMEM((l + 2 * reach, cw), F32)], compiler_params=_cp(("parallel",)),
    )(proj, proj, w, b)


def _conv31_bwd(proj, w, dcv, l, lc, d, u_blk):
    cw = LANES
    width = w.shape[0]
    reach = (width // 2) * GRID_W
    nb = d // cw
    t_all = l + lc

    def body(u_ref, v_ref, w_ref, g_ref, du_ref, dv_ref, dw_ref, db_ref, s_ref):
        zeros = jnp.zeros((reach, cw), F32)
        s_ref[pl.ds(0, reach), :] = zeros
        s_ref[pl.ds(reach + l, reach), :] = zeros
        u, v, g = u_ref[...], v_ref[...], g_ref[...]
        s_ref[pl.ds(reach, l), :] = u * jax.nn.sigmoid(v)
        db_ref[...] = jnp.sum(g, axis=0, keepdims=True)
        for k in range(width):
            dw_ref[k:k + 1, :] = jnp.sum(g * s_ref[pl.ds(k * GRID_W, l), :], axis=0, keepdims=True)
        s_ref[pl.ds(reach, l), :] = g
        dt = sum(w_ref[k:k + 1, :] * s_ref[pl.ds((width - 1 - k) * GRID_W, l), :] for k in range(width))
        _, vjp = jax.vjp(lambda a, c: a * jax.nn.sigmoid(c), u, v)
        du, dv = vjp(dt)
        du_ref[pl.ds(0, l), :] = du.astype(du_ref.dtype)
        dv_ref[pl.ds(0, l), :] = dv.astype(dv_ref.dtype)
        du_ref[pl.ds(l, lc), :] = jnp.zeros((lc, cw), du_ref.dtype)
        dv_ref[pl.ds(l, lc), :] = jnp.zeros((lc, cw), dv_ref.dtype)

    pshape = jax.ShapeDtypeStruct((t_all, d), MXU_DTYPE)
    tall = pl.BlockSpec((t_all, cw), lambda j: (0, j))
    return pl.pallas_call(
        body, name="conv31_bwd", grid=(nb,),
        in_specs=[pl.BlockSpec((l, cw), lambda j: (0, u_blk * nb + j)), pl.BlockSpec((l, cw), lambda j: (0, (u_blk + 1) * nb + j)),
                  pl.BlockSpec((width, cw), lambda j: (0, j)), pl.BlockSpec((l, cw), lambda j: (0, j))],
        out_specs=[tall, tall, pl.BlockSpec((width, cw), lambda j: (0, j)), pl.BlockSpec((1, cw), lambda j: (0, j))],
        out_shape=[pshape, pshape, jax.ShapeDtypeStruct((width, d), F32), jax.ShapeDtypeStruct((1, d), F32)],
        scratch_shapes=[pltpu.VMEM((l + 2 * reach, cw), F32)], compiler_params=_cp(("parallel",)),
    )(proj, proj, w, dcv)


def _softplus(x):
    return jnp.maximum(x, 0.0) + jnp.log(1.0 + jnp.exp(-jnp.abs(x)))


def _dt_fwd(proj, bias, a, dt_blk):
    t_all = proj.shape[0]
    hh = bias.shape[1]
    q = CHUNK

    def body(r_ref, b_ref, a_ref, dt_ref, cs_ref, tc_ref, cst_ref):
        dt = _softplus(r_ref[...] + b_ref[...])
        dt_ref[...] = dt
        da = dt * a_ref[...]
        li, si = _iota((q, q), 0), _iota((q, q), 1)
        reverse_cols = _iota((q, hh), 1) >= hh // 2
        cs = jnp.where(reverse_cols, _dot((si >= li).astype(F32), da, exact=True), _dot((si <= li).astype(F32), da, exact=True))
        cs_ref[...] = cs
        cst_ref[...] = cs.T
        total = jnp.where(_iota((1, hh), 1) >= hh // 2, cs_ref[0:1, :], cs_ref[q - 1:q, :])
        tc_ref[...] = total - cs

    row = pl.BlockSpec((q, hh), lambda i: (i, 0))
    shape = jax.ShapeDtypeStruct((t_all, hh), F32)
    return pl.pallas_call(
        body, name="dt_fwd", grid=(t_all // q,),
        in_specs=[pl.BlockSpec((q, hh), lambda i: (i, dt_blk)), _bc_spec(hh), _bc_spec(hh)],
        out_specs=[row, row, row, pl.BlockSpec((hh, q), lambda i: (0, i))],
        out_shape=[shape, shape, shape, jax.ShapeDtypeStruct((hh, t_all), F32)],
        compiler_params=_cp(("parallel",)),
    )(proj, bias, a)


def _three_way(x):
    def top(v):
        word = lax.bitcast_convert_type(v, jnp.uint32) & jnp.uint32(0xFFFF0000)
        return lax.bitcast_convert_type(word, F32)

    hi = top(x)
    rest = x - hi
    mid = top(rest)
    return hi.astype(jnp.bfloat16), mid.astype(jnp.bfloat16), (rest - mid).astype(jnp.bfloat16)


def _scan_columns(dt, cs, tc, groups2, hpg):
    t_all = dt.shape[0]
    parts = [part.reshape(t_all, groups2, 1, hpg) for arr in (dt, cs, tc) for part in _three_way(arr)]
    cols = jnp.concatenate(parts, axis=2).transpose(1, 0, 2, 3).reshape(groups2, t_all, 9 * hpg)
    return jnp.pad(cols, ((0, 0), (0, 0), (0, LANES - 9 * hpg)))


def _dt_bwd(proj, bias, dt, ddt, dda, dt_blk):
    t_all = proj.shape[0]
    hh = bias.shape[1]
    q = _tile(t_all, 1024, LANES)

    def body(r_ref, b_ref, dt_ref, ddt_ref, dda_ref, dr_ref, db_ref, da_ref):
        dr = ddt_ref[...] * jax.nn.sigmoid(r_ref[...] + b_ref[...])
        dr_ref[...] = dr.astype(dr_ref.dtype)
        i = pl.program_id(0)
        _acc(db_ref, jnp.sum(dr, axis=0, keepdims=True), i)
        _acc(da_ref, jnp.sum(dda_ref[...] * dt_ref[...], axis=0, keepdims=True), i)

    row = pl.BlockSpec((q, hh), lambda i: (i, 0))
    vec = jax.ShapeDtypeStruct((1, hh), F32)
    return pl.pallas_call(
        body, name="dt_bwd", grid=(t_all // q,),
        in_specs=[pl.BlockSpec((q, hh), lambda i: (i, dt_blk)), _bc_spec(hh), row, row, row],
        out_specs=[row, _bc_spec(hh), _bc_spec(hh)],
        out_shape=[jax.ShapeDtypeStruct((t_all, hh), MXU_DTYPE), vec, vec], compiler_params=_cp(("arbitrary",)),
    )(proj, bias, dt, ddt, dda)


_NT = (((1,), (1,)), ((), ()))
_TN = (((0,), (0,)), ((), ()))


def _dot(a, b, dims=None, exact=False):
    kw = dict(preferred_element_type=F32)
    if exact:
        kw["precision"] = HI
    if dims is None:
        return jnp.dot(a, b, **kw)
    return lax.dot_general(a, b, dims, **kw)


def _iota(shape, dim):
    return lax.broadcasted_iota(jnp.int32, shape, dim)


class _Ssd:
    def __init__(self, l, lc, di, p, reverse):
        self.q, self.n, self.g = CHUNK, SSM_STATE, SSM_GROUPS
        self.nl, self.ncx = l // CHUNK, lc // CHUNK
        self.ns = self.nl + self.ncx
        self.t_all, self.di, self.p, self.reverse = l + lc, di, p, reverse
        self.hpg = di // p // SSM_GROUPS
        self.gw = self.hpg * p
        self.ntile = self.gw // LANES
        self.hpt = LANES // p
        self.log2p = p.bit_length() - 1
        assert 1 << self.log2p == p and self.gw % LANES == 0 and self.n == LANES and self.q == LANES
        assert 9 * self.hpg <= LANES
        self.d = 1 if reverse else 0

    def chunk_at(self, step):
        if self.reverse:
            return self.ns - 1 - step
        return jnp.where(step < self.ncx, self.nl + step, step - self.ncx)

    def selectors(self):
        hpg = self.hpg
        k = jnp.arange(LANES)
        quantity, head, used = k // (3 * hpg), k % hpg, k < 9 * hpg
        lane_head = jnp.arange(LANES) // self.p
        tiles = jnp.concatenate([(used & (quantity == qo))[:, None] & (head[:, None] == tt * self.hpt + lane_head[None, :])
                                 for tt in range(self.ntile) for qo in range(3)], axis=1)
        heads = jnp.concatenate([jnp.broadcast_to((used & (quantity == 1) & (head == j))[:, None], (LANES, LANES))
                                 for j in range(hpg)], axis=1)
        return tiles.astype(jnp.bfloat16), heads.astype(jnp.bfloat16)

    def in_specs(self, chunk_of):
        g, n, hpg, q = self.g, self.n, self.hpg, self.q
        b_blk, c_blk = self.di // n, self.di // n + g
        d = self.d
        return [
            pl.BlockSpec((q, self.gw), lambda gi, i: (chunk_of(i), gi)),
            pl.BlockSpec((q, n), lambda gi, i: (chunk_of(i), b_blk + gi)),
            pl.BlockSpec((q, n), lambda gi, i: (chunk_of(i), c_blk + gi)),
            pl.BlockSpec((1, q, LANES), lambda gi, i: (d * g + gi, chunk_of(i), 0)),
            pl.BlockSpec((hpg, q), lambda gi, i: (d * g + gi, chunk_of(i))),
            pl.BlockSpec((LANES, self.ntile * 3 * LANES), lambda gi, i: (0, 0)),
            pl.BlockSpec((LANES, hpg * LANES), lambda gi, i: (0, 0)),
        ]

    def masks(self):
        li, si = _iota((self.q, self.q), 0), _iota((self.q, self.q), 1)
        if self.reverse:
            return si >= li, li >= si
        return si <= li, li <= si

    def spread(self, spread_all, tt):
        at = 3 * LANES * tt
        return tuple(spread_all[:, at + k * LANES:at + (k + 1) * LANES] for k in range(3))

    def head_lanes(self, qq):
        return lax.shift_right_logical(_iota((self.q, LANES), 1), self.log2p) == qq

    def head_sums(self, values, tt):
        sel = _iota((HEAD_COLS, LANES), 0) == tt * self.hpt + lax.shift_right_logical(_iota((HEAD_COLS, LANES), 1), self.log2p)
        parts = [part for v in values for part in _three_way(v)]
        sums = _dot(jnp.concatenate(parts, axis=0), sel.astype(jnp.bfloat16), _NT)
        out, at = [], 0
        for v in values:
            rows = v.shape[0]
            out.append(sums[at:at + rows] + sums[at + rows:at + 2 * rows] + sums[at + 2 * rows:at + 3 * rows])
            at += 3 * rows
        return out

    def state_scale(self, csr_ref):
        last = 0 if self.reverse else self.q - 1
        total = jnp.sum(jnp.where(_iota((self.hpg, self.q), 1) == last, csr_ref[...], 0.0), axis=1, keepdims=True)
        decay = jnp.broadcast_to(jnp.exp(total), (self.hpg, self.n))
        decay = jnp.concatenate([decay, jnp.zeros((HEAD_COLS - self.hpg, self.n), F32)], axis=0)
        rows = lax.shift_right_logical(_iota((self.gw, HEAD_COLS), 0), self.log2p) == _iota((self.gw, HEAD_COLS), 1)
        return _dot_parts(rows.astype(jnp.bfloat16), decay)


def _dot_parts(sel, v, dims=None):
    return sum(_dot(sel, part, dims) for part in _three_way(v))


def _ssd_fwd(xbc, cols, cs_t, l, lc, di, p, reverse):
    s = _Ssd(l, lc, di, p, reverse)
    q, n, gw = s.q, s.n, s.gw
    neg_inf = float("-inf")

    def body(xs_ref, b_ref, c_ref, cols_ref, csr_ref, et_ref, eh_ref, y_ref, hp_ref, h_scr):
        i = pl.program_id(1)

        @pl.when(i == 0)
        def _():
            h_scr[...] = jnp.zeros_like(h_scr)

        h = h_scr[...]
        hp_ref[0, 0] = h
        mask, _ = s.masks()
        cols = cols_ref[0]
        bb, cb = b_ref[...].astype(MXU_DTYPE), c_ref[...].astype(MXU_DTYPE)
        cbt = _dot(cb, bb, _NT)
        y_off = _dot(cb, h.astype(MXU_DTYPE), _NT)
        spread_all, cs_heads = _dot(cols, et_ref[...]), _dot(cols, eh_ref[...])
        w_tiles = []
        for tt in range(s.ntile):
            sl = slice(tt * LANES, (tt + 1) * LANES)
            dt_b, cs_b, tc_b = s.spread(spread_all, tt)
            x = xs_ref[:, sl] * dt_b
            ms, xhs = [], []
            for qq in range(s.hpt):
                j = tt * s.hpt + qq
                seg = cs_heads[:, j * LANES:(j + 1) * LANES] - csr_ref[j:j + 1, :]
                ms.append((cbt * jnp.exp(jnp.where(mask, seg, neg_inf))).astype(MXU_DTYPE))
                xhs.append(jnp.where(s.head_lanes(qq), x, 0.0).astype(MXU_DTYPE))
            yd = _dot(jnp.concatenate(ms, axis=1), jnp.concatenate(xhs, axis=0))
            y_ref[:, sl] = yd + y_off[:, sl] * jnp.exp(cs_b)
            w_tiles.append((x * jnp.exp(tc_b)).astype(MXU_DTYPE))
        wm = w_tiles[0] if s.ntile == 1 else jnp.concatenate(w_tiles, axis=1)
        h_scr[...] = h * s.state_scale(csr_ref) + _dot(wm, bb, _TN)

    d = "rev" if reverse else "fwd"
    e_tiles, e_heads = s.selectors()
    return pl.pallas_call(
        body, name=f"ssd_{d}", grid=(s.g, s.ns), in_specs=s.in_specs(s.chunk_at),
        out_specs=[pl.BlockSpec((q, gw), lambda gi, i: (s.chunk_at(i), gi)),
                   pl.BlockSpec((1, 1, gw, n), lambda gi, i: (i, gi, 0, 0))],
        out_shape=[jax.ShapeDtypeStruct((s.t_all, di), F32), jax.ShapeDtypeStruct((s.ns, s.g, gw, n), F32)],
        scratch_shapes=[pltpu.VMEM((gw, n), F32)],
        compiler_params=_cp(("parallel", "arbitrary")),
    )(xbc, xbc, xbc, cols, cs_t, e_tiles, e_heads)


def _ssd_bwd(xbc, cols, cs_t, a_cols, dy, hprev, l, lc, di, p, reverse, dsk=None, prev=None):
    s = _Ssd(l, lc, di, p, reverse)
    q, n, gw, hpg = s.q, s.n, s.gw, s.hpg
    neg_inf = float("-inf")
    n_extra = (dsk is not None) + (3 if prev is not None else 0)

    def chunk_of(i):
        return s.chunk_at(s.ns - 1 - i)

    def body(xs_ref, b_ref, c_ref, cols_ref, csr_ref, et_ref, eh_ref, ac_ref, dy_ref, hp_ref, *rest):
        extra, (dxs_ref, db_ref, dc_ref, ddt_ref, dda_ref, dh_scr) = rest[:n_extra], rest[n_extra:]
        dsk_ref = extra[0] if dsk is not None else None
        prev_refs = extra[-3:] if prev is not None else None
        i = pl.program_id(1)

        @pl.when(i == 0)
        def _():
            dh_scr[...] = jnp.zeros_like(dh_scr)

        latent = (chunk_of(i) < s.nl).astype(F32)
        h, dh = hp_ref[0, 0], dh_scr[...]
        hb, dhb = h.astype(MXU_DTYPE), dh.astype(MXU_DTYPE)
        mask, mask_t = s.masks()
        cols = cols_ref[0]
        bb, cb = b_ref[...].astype(MXU_DTYPE), c_ref[...].astype(MXU_DTYPE)
        cbt, bct = _dot(cb, bb, _NT), _dot(bb, cb, _NT)
        b_dh = _dot(bb, dhb, _NT)
        y_off0 = _dot(cb, hb, _NT)
        d_g, d_gt = jnp.zeros((q, q), F32), jnp.zeros((q, q), F32)
        dcs = jnp.zeros((q, HEAD_COLS), F32)
        ddt_x = jnp.zeros((q, HEAD_COLS), F32)
        r_state = jnp.zeros((16, HEAD_COLS), F32)
        spread_all, cs_heads = _dot(cols, et_ref[...]), _dot(cols, eh_ref[...])
        dye_tiles, xte_tiles = [], []
        for tt in range(s.ntile):
            sl = slice(tt * LANES, (tt + 1) * LANES)
            dt_b, cs_b, tc_b = s.spread(spread_all, tt)
            ecs_b, te_b = jnp.exp(cs_b), jnp.exp(tc_b)
            xs_t = xs_ref[:, sl]
            x = xs_t * dt_b
            d_y = dy_ref[:, sl] * latent
            dx_state = b_dh[:, sl] * te_b
            lms, lm_ts, m_ts, d_yhs, xhs = [], [], [], [], []
            for qq in range(s.hpt):
                j = tt * s.hpt + qq
                csc_b = cs_heads[:, j * LANES:(j + 1) * LANES]
                csr = csr_ref[j:j + 1, :]
                lms.append(jnp.exp(jnp.where(mask, csc_b - csr, neg_inf)))
                lm_ts.append(jnp.exp(jnp.where(mask_t, csr - csc_b, neg_inf)))
                m_ts.append(bct * lm_ts[-1])
                lanes = s.head_lanes(qq)
                d_yhs.append(jnp.where(lanes, d_y, 0.0).astype(MXU_DTYPE))
                xhs.append(jnp.where(lanes, x, 0.0).astype(MXU_DTYPE))
            d_yh_rows = jnp.concatenate(d_yhs, axis=0)
            d_m_all = _dot(d_yh_rows, x.astype(MXU_DTYPE), _NT)
            d_mt_all = _dot(jnp.concatenate(xhs, axis=0), d_y.astype(MXU_DTYPE), _NT)
            for qq in range(s.hpt):
                j = tt * s.hpt + qq
                d_m, d_mt = d_m_all[qq * q:(qq + 1) * q], d_mt_all[qq * q:(qq + 1) * q]
                r1 = jnp.sum(d_m * (cbt * lms[qq]), axis=1, keepdims=True)
                r2 = jnp.sum(d_mt * m_ts[qq], axis=1, keepdims=True)
                dcs = dcs + (r1 - r2) * (_iota((1, HEAD_COLS), 1) == j).astype(F32)
                d_g = d_g + d_m * lms[qq]
                d_gt = d_gt + d_mt * lm_ts[qq]
            d_x = _dot(jnp.concatenate([m.astype(MXU_DTYPE) for m in m_ts], axis=1), d_yh_rows) + dx_state
            d_xs = d_x * dt_b
            if dsk_ref is not None:
                d_xs = d_xs + d_y * dsk_ref[:, sl]
            if prev_refs is not None:
                d_xs = d_xs + prev_refs[0][:, sl]
            dxs_ref[:, sl] = d_xs
            fed = x * dx_state
            fed_rows = jnp.broadcast_to(jnp.sum(fed, axis=0, keepdims=True), (16, LANES))
            sums = s.head_sums([d_x * xs_t, d_y * y_off0[:, sl] * ecs_b - fed, fed_rows], tt)
            ddt_x, dcs, r_state = ddt_x + sums[0], dcs + sums[1], r_state + sums[2]
            dye_tiles.append((d_y * ecs_b).astype(MXU_DTYPE))
            xte_tiles.append((x * te_b).astype(MXU_DTYPE))
        dye = dye_tiles[0] if s.ntile == 1 else jnp.concatenate(dye_tiles, axis=1)
        xte = xte_tiles[0] if s.ntile == 1 else jnp.concatenate(xte_tiles, axis=1)
        d_c = _dot(d_g.astype(MXU_DTYPE), bb) + _dot(dye, hb)
        d_b = _dot(d_gt.astype(MXU_DTYPE), cb) + _dot(xte, dhb)
        if prev_refs is not None:
            d_b, d_c = d_b + prev_refs[1][...], d_c + prev_refs[2][...]
        dc_ref[...] = d_c
        db_ref[...] = d_b
        scale = s.state_scale(csr_ref)
        carried = dh * h * scale
        d_tot = jnp.sum(r_state, axis=0, keepdims=True) * 0.0625
        for j in range(hpg):
            part = jnp.sum(carried[j * p:(j + 1) * p, :], axis=0, keepdims=True)
            d_tot = d_tot + jnp.sum(part, axis=1, keepdims=True) * (_iota((1, HEAD_COLS), 1) == j).astype(F32)
        dda = _dot_parts(mask_t.astype(jnp.bfloat16), dcs) + d_tot
        ddt_ref[0] = ddt_x + dda * ac_ref[0]
        dda_ref[0] = dda
        dh_scr[...] = dh * scale + _dot(dye, cb, _TN)

    d = "rev" if reverse else "fwd"
    e_tiles, e_heads = s.selectors()
    col = pl.BlockSpec((1, q, HEAD_COLS), lambda gi, i: (gi, chunk_of(i), 0))
    gn = pl.BlockSpec((q, n), lambda gi, i: (chunk_of(i), gi))
    wide = pl.BlockSpec((q, gw), lambda gi, i: (chunk_of(i), gi))
    extra_specs, extra_args, aliases = [], [], {}
    if dsk is not None:
        extra_specs.append(pl.BlockSpec((1, gw), lambda gi, i: (0, gi)))
        extra_args.append(dsk)
    if prev is not None:
        first = 10 + len(extra_args)
        extra_specs += [wide, gn, gn]
        extra_args += list(prev)
        aliases = {first: 0, first + 1: 1, first + 2: 2}
    return pl.pallas_call(
        body, name=f"ssd_bwd_{d}", grid=(s.g, s.ns),
        in_specs=s.in_specs(chunk_of) + [
            pl.BlockSpec((1, 1, HEAD_COLS), lambda gi, i: (s.d * s.g + gi, 0, 0)),
            pl.BlockSpec((q, gw), lambda gi, i: (jnp.minimum(chunk_of(i), s.nl - 1), gi)),
            pl.BlockSpec((1, 1, gw, n), lambda gi, i: (s.ns - 1 - i, gi, 0, 0))] + extra_specs,
        out_specs=[wide, gn, gn, col, col],
        out_shape=[jax.ShapeDtypeStruct((s.t_all, di), F32), jax.ShapeDtypeStruct((s.t_all, s.g * n), F32),
                   jax.ShapeDtypeStruct((s.t_all, s.g * n), F32), jax.ShapeDtypeStruct((s.g, s.t_all, HEAD_COLS), F32),
                   jax.ShapeDtypeStruct((s.g, s.t_all, HEAD_COLS), F32)],
        scratch_shapes=[pltpu.VMEM((gw, n), F32)],
        input_output_aliases=aliases, compiler_params=_cp(("parallel", "arbitrary")),
    )(xbc, xbc, xbc, cols, cs_t, e_tiles, e_heads, a_cols, dy, hprev, *extra_args)


def _ada_fwd(crows, w, b):
    r, d = crows.shape
    ws = w.shape[1]
    tn = _tile(ws, 512, LANES)

    def body(c_ref, w_ref, b_ref, m_ref, s_ref):
        s = _silu(c_ref[...])
        s_ref[...] = s
        m_ref[...] = _dot(s.astype(MXU_DTYPE), w_ref[...].astype(MXU_DTYPE)) + b_ref[...]

    full = pl.BlockSpec((r, d), lambda j: (0, 0))
    return pl.pallas_call(
        body, name="ada_fwd", grid=(ws // tn,),
        in_specs=[full, pl.BlockSpec((d, tn), lambda j: (0, j)), pl.BlockSpec((1, tn), lambda j: (0, j))],
        out_specs=[pl.BlockSpec((r, tn), lambda j: (0, j)), full],
        out_shape=[jax.ShapeDtypeStruct((r, ws), F32), jax.ShapeDtypeStruct((r, d), F32)],
        compiler_params=_cp(("arbitrary",)),
    )(crows, w, b)


def _ada_bwd(s_t, w, dm):
    d, r = s_t.shape
    ws = w.shape[1]
    tn = _tile(ws, 512, LANES)

    def body(st_ref, w_ref, dm_ref, dw_ref, ds_ref):
        dmb = dm_ref[...].astype(MXU_DTYPE)
        dw_ref[...] = _dot(st_ref[...].astype(MXU_DTYPE), dmb)
        _acc(ds_ref, _dot(dmb, w_ref[...].astype(MXU_DTYPE), _NT), pl.program_id(0))

    return pl.pallas_call(
        body, name="ada_bwd", grid=(ws // tn,),
        in_specs=[pl.BlockSpec((d, r), lambda j: (0, 0)), pl.BlockSpec((d, tn), lambda j: (0, j)),
                  pl.BlockSpec((r, tn), lambda j: (0, j))],
        out_specs=[pl.BlockSpec((d, tn), lambda j: (0, j)), pl.BlockSpec((r, d), lambda j: (0, 0))],
        out_shape=[jax.ShapeDtypeStruct((d, ws), F32), jax.ShapeDtypeStruct((r, d), F32)],
        compiler_params=_cp(("arbitrary",)),
    )(s_t, w, dm)


def _adamw(w, g, m, v, name):
    r, c = w.shape
    t = _tile(r, max(8, 300_000 // c), 8)

    def body(w_ref, g_ref, m_ref, v_ref, d_ref, m2_ref, v2_ref):
        g = g_ref[...]
        m2 = ADAM_B1 * m_ref[...] + (1.0 - ADAM_B1) * g
        v2 = ADAM_B2 * v_ref[...] + (1.0 - ADAM_B2) * (g * g)
        m_hat = m2 / (1.0 - ADAM_B1 ** ADAM_STEP)
        v_hat = v2 / (1.0 - ADAM_B2 ** ADAM_STEP)
        d_ref[...] = -ADAM_LR * (m_hat / (jnp.sqrt(v_hat) + ADAM_EPS) + ADAM_WD * w_ref[...])
        m2_ref[...] = m2
        v2_ref[...] = v2

    blk = pl.BlockSpec((t, c), lambda i: (i, 0))
    shape = jax.ShapeDtypeStruct((r, c), F32)
    return pl.pallas_call(
        body, name=name, grid=(r // t,), in_specs=[blk] * 4, out_specs=[blk] * 3, out_shape=[shape] * 3,
        compiler_params=_cp(("parallel",)),
    )(w, g, m, v)


def _sum_devices(gathered):
    rows, w = gathered.shape
    per = rows // N_DEV

    def body(g_ref, o_ref):
        total = g_ref[pl.ds(0, per), :]
        for dev in range(1, N_DEV):
            total = total + g_ref[pl.ds(dev * per, per), :]
        o_ref[...] = total

    return pl.pallas_call(
        body, name="sum_devices", out_shape=jax.ShapeDtypeStruct((per, w), F32),
        in_specs=[pl.BlockSpec(memory_space=pltpu.VMEM)], out_specs=pl.BlockSpec(memory_space=pltpu.VMEM),
        compiler_params=_cp(),
    )(gathered)


def _c_ctx_grad(parts, c_ctx):
    rows, d = parts.shape
    per = rows // N_DEV

    def body(p_ref, c_ref, o_ref):
        total = p_ref[pl.ds(0, 1), :]
        for chip in range(1, N_SHARD):
            total = total + p_ref[pl.ds(2 * chip * per, 1), :]
        _, vjp = jax.vjp(_silu, c_ref[...])
        o_ref[...] = vjp(total)[0]

    return pl.pallas_call(
        body, name="c_ctx_grad", out_shape=jax.ShapeDtypeStruct((1, d), F32),
        in_specs=[pl.BlockSpec(memory_space=pltpu.VMEM)] * 2, out_specs=pl.BlockSpec(memory_space=pltpu.VMEM),
        compiler_params=_cp(),
    )(parts, c_ctx)


def _pad_rows(a, rows, width):
    return jnp.pad(a, ((0, rows - a.shape[0]), (0, width - a.shape[1])))


def _pack(vectors, quantum):
    flat = jnp.concatenate([v.reshape(-1) for v in vectors])
    return jnp.pad(flat, (0, -flat.shape[0] % quantum))


def kernel(x, c, ctx, c_ctx, w_mod, b_mod, norm_mix, w_in, ssm_conv_w, ssm_conv_b, dt_bias, a_log, d_skip, ssm_norm, cf_conv_w, cf_conv_b, cf_ln_g, cf_ln_b, w_proj_a, w_proj_b, w_out, norm_ffn, w_ffn_gate, w_ffn_up, w_ffn_down, norm_final, loss_target, m_c_ctx, m_w_mod, m_b_mod, m_norm_mix, m_w_in, m_ssm_conv_w, m_ssm_conv_b, m_dt_bias, m_a_log, m_d_skip, m_ssm_norm, m_cf_conv_w, m_cf_conv_b, m_cf_ln_g, m_cf_ln_b, m_w_proj_a, m_w_proj_b, m_w_out, m_norm_ffn, m_w_ffn_gate, m_w_ffn_up, m_w_ffn_down, m_norm_final, v_c_ctx, v_w_mod, v_b_mod, v_norm_mix, v_w_in, v_ssm_conv_w, v_ssm_conv_b, v_dt_bias, v_a_log, v_d_skip, v_ssm_norm, v_cf_conv_w, v_cf_conv_b, v_cf_ln_g, v_cf_ln_b, v_w_proj_a, v_w_proj_b, v_w_out, v_norm_ffn, v_w_ffn_gate, v_w_ffn_up, v_w_ffn_down, v_norm_final):
    l, d = x.shape[1], x.shape[2]
    lc = ctx.shape[1]
    t_all = l + lc
    di = ssm_norm.shape[-1]
    h = d_skip.shape[-1]
    p = di // h
    g, n = SSM_GROUPS, SSM_STATE
    hpg = h // g
    conv_dim = di + 2 * g * n
    df = w_ffn_down.shape[1] * N_SHARD
    assert 2 * h == LANES and d % (2 * LANES) == 0

    my_x, my_y, my_c = _mesh_pos()
    chip = 2 * my_x + my_y
    dev = 2 * chip + my_c

    x2, ctx2, tgt = x[0], ctx[0], loss_target[0]
    row = lambda a: a.reshape(1, -1)

    cw_shard, cfw_shard = ssm_conv_w[0], cf_conv_w[0]
    k5, k31 = cw_shard.shape[0], cfw_shard.shape[0]
    r5, r31 = -(-k5 // 8) * 8, -(-k31 // 8) * 8
    wp = max(d, cw_shard.shape[1], cfw_shard.shape[1])
    packed = jnp.concatenate([_pad_rows(c, 8, wp), _pad_rows(cw_shard, r5, wp), _pad_rows(cfw_shard, r31, wp)], axis=0)
    got = _allgather_small(packed, "ag_params").reshape(N_DEV, 8 + r5 + r31, wp)
    c_all = got[:, 0, :d]
    conv_w = got[0::2, 8:8 + k5, :cw_shard.shape[1]].transpose(1, 0, 2).reshape(k5, conv_dim)
    cf_w = got[0::2, 8 + r5:8 + r5 + k31, :cfw_shard.shape[1]].transpose(1, 0, 2).reshape(k31, d)

    ws = w_mod.shape[2]
    crows = jnp.concatenate([c_all, row(c_ctx), jnp.zeros((7, d), F32)], axis=0)
    b_mod_mine = lax.dynamic_slice(b_mod, (0, chip * ws), (1, ws))
    m_part, s_rows = _ada_fwd(crows, w_mod[0], b_mod_mine)
    m_full = _allgather_small(m_part, "ag_mod").reshape(N_DEV, 16, ws)[0::2].transpose(1, 0, 2).reshape(16, N_SHARD * ws)
    m_lat = lax.dynamic_slice(m_full, (dev, 0), (1, 6 * d))
    sh1, sc1, g1, sh2, sc2, g2 = [m_lat[:, i * d:(i + 1) * d] for i in range(6)]
    csh1, csc1 = m_full[8:9, 0:d], m_full[8:9, d:2 * d]

    shards = [w_in[0].T, w_ffn_gate[0].T, w_ffn_up[0].T, w_proj_a[0], w_proj_b[0], w_out[0], w_ffn_down[0]]
    shards = [s.astype(WIRE_DTYPE) for s in shards]
    (win_t,) = _fill_own_rows(_run_side(_gather_side(shards[:1]), "ag_w_in"), shards[:1])
    o_xbc, o_dt, o_glu, o_gates = di, di + conv_dim, di + conv_dim + 2 * h, di + conv_dim + 2 * h + 2 * d
    win_work = jnp.concatenate([win_t[:o_xbc], win_t[o_glu:], win_t[o_xbc:o_dt], win_t[o_dt:o_glu]], axis=0)
    c_u, c_ga, c_xbc, c_dt = di, di + 2 * d, di + 4 * d, di + 4 * d + conv_dim

    nm = norm_mix
    hx = _mod_fwd(x2, ctx2, nm, sc1, sh1, csc1, csh1)
    proj, *rest = _matmul(hx, win_work.T, tm=384, tn=29 * LANES, n_outer=True, name="mm_proj", side=_gather_side(shards[1:]))
    wg_t, wu_t, wpa, wpb, wout, wdn = _fill_own_rows(rest, shards[1:])
    wgu = jnp.concatenate([wg_t, wu_t], axis=0)
    xbc = _conv5_fwd(proj, conv_w, ssm_conv_b, l, lc, c_xbc, conv_dim)
    a = -jnp.exp(a_log.reshape(1, 2 * h))
    dt, cs, tc, cs_t = _dt_fwd(proj, dt_bias.reshape(1, 2 * h), a, c_dt // LANES)
    cols = _scan_columns(dt, cs, tc, 2 * g, hpg)
    a_cols = jnp.pad(a.reshape(2 * g, 1, hpg), ((0, 0), (0, 0), (0, HEAD_COLS - hpg)))
    y_f, hp_f = _ssd_fwd(xbc, cols, cs_t, l, lc, di, p, False)
    y_r, hp_r = _ssd_fwd(xbc, cols, cs_t, l, lc, di, p, True)
    dsk = jnp.repeat(d_skip.reshape(h), p).reshape(1, di)
    ya_in = _gate_fwd(y_f, y_r, xbc, proj, dsk, ssm_norm, l, di)
    y_a = _matmul(ya_in, wpa, tk=di, name="mm_ya")
    u_blk = c_u // d
    cv = _conv31_fwd(proj, cf_w, cf_conv_b, l, d, u_blk)
    cf = _ln_fwd(cv, cf_ln_g, cf_ln_b)
    y_b = _matmul(cf, wpb, name="mm_yb")
    ga_blk = c_ga // d
    merged = _merge_fwd(y_a, y_b, proj, ga_blk)
    mix = _matmul(merged, wout, name="mm_mix")
    x1, hx2 = _res_fwd(x2, mix, g1, norm_ffn, sc2, sh2)
    gu = _matmul(hx2, wgu, tb=True, tm=1024, tn=_tile(df, 1024, LANES), name="mm_gu")
    act = _swiglu_fwd(gu, df)
    dn = _matmul(act, wdn, tk=df, name="mm_dn")
    loss, dx1, ddn, dg2, d_norm_final = _loss_and_grads(x1, dn, g2, row(norm_final), tgt)

    dact = _matmul(ddn, wdn, tb=True, tm=1024, tn=_tile(df, 1024, LANES), name="mm_dact")
    dw_dn = _matmul(act, ddn, ta=True, tn=d, tk=1024, name="mm_dw_dn")
    dgu = _swiglu_bwd(gu, dact, df)
    dhx2 = _matmul(dgu, wgu, tm=1024, tn=1024, tk=_tile(2 * df, 1536, LANES), name="mm_dhx2")
    dw_gu = _matmul(dgu, hx2, ta=True, tm=1024, tn=d, tk=1024, name="mm_dw_gu")
    dx_res, dmix, dg1, d_norm_ffn, dsc2, dsh2 = _res_bwd(x2, mix, g1, norm_ffn, sc2, sh2, dx1, dhx2)
    dmerged = _matmul(dmix, wout, tb=True, name="mm_dmerged")
    dw_out = _matmul(merged, dmix, ta=True, tn=d, tk=1024, name="mm_dw_out")
    dya, dyb, dga, dgb = _merge_bwd(y_a, y_b, proj, ga_blk, dmerged, lc)
    dcf = _matmul(dyb, wpb, tb=True, name="mm_dcf")
    dw_pb = _matmul(cf, dyb, ta=True, tn=d, tk=1024, name="mm_dw_pb")
    dcv, d_ln_g, d_ln_b = _ln_bwd(cv, cf_ln_g, cf_ln_b, dcf)
    du, dv, d_cf_w, d_cf_b = _conv31_bwd(proj, cf_w, dcv, l, lc, d, u_blk)
    dya_in = _matmul(dya, wpa, tb=True, tn=_tile(di, 1024, LANES), name="mm_dya_in")
    dw_pa = _matmul(ya_in, dya, ta=True, tn=d, tk=1024, name="mm_dw_pa")
    dy, dz, ddsk, d_ssm_norm = _gate_bwd(y_f, y_r, xbc, proj, dsk, ssm_norm, dya_in, l, lc, di)
    dxs_f, db_f, dc_f, ddt_f, dda_f = _ssd_bwd(xbc, cols, cs_t, a_cols, dy, hp_f, l, lc, di, p, False, dsk=dsk)
    dxs, db, dc, ddt_r, dda_r = _ssd_bwd(xbc, cols, cs_t, a_cols, dy, hp_r, l, lc, di, p, True, prev=(dxs_f, db_f, dc_f))
    dxs_raw, dcw_x, dcb_x = _conv5_bwd(proj, conv_w, ssm_conv_b, [dxs], l, lc, c_xbc, 0, di)
    db_raw, dcw_b, dcb_b = _conv5_bwd(proj, conv_w, ssm_conv_b, [db], l, lc, c_xbc, di, g * n)
    dc_raw, dcw_c, dcb_c = _conv5_bwd(proj, conv_w, ssm_conv_b, [dc], l, lc, c_xbc, di + g * n, g * n)
    d_conv_w = jnp.concatenate([dcw_x, dcw_b, dcw_c], axis=1)
    d_conv_b = jnp.concatenate([dcb_x, dcb_b, dcb_c], axis=1)
    heads = lambda f, r: jnp.concatenate([t[:, :, :hpg].transpose(1, 0, 2).reshape(t_all, h) for t in (f, r)], axis=1)
    ddt_raw, d_dt_bias, dda_dt = _dt_bwd(proj, dt_bias.reshape(1, 2 * h), dt, heads(ddt_f, ddt_r), heads(dda_f, dda_r), c_dt // LANES)
    d_a_log = dda_dt * a
    dproj = jnp.concatenate([dz, du, dv, dga, dgb, dxs_raw, db_raw, dc_raw, ddt_raw], axis=1)
    wire, own = _reduce_scatter_begin([dw_gu[:df], dw_gu[df:], dw_pa, dw_pb, dw_out, dw_dn], "a")
    dhx, *recv = _matmul(dproj, win_work, tm=768, tn=1024, tk=_tile(win_work.shape[0], 4096, LANES), name="mm_dhx",
                         side=_scatter_side(wire))
    g_gate_t, g_up_t, g_pa, g_pb, g_out, g_dn = _reduce_scatter_end(own, recv, "a")
    dw_in_work = _matmul(dproj, hx, ta=True, tm=640, tn=d, tk=768, name="mm_dw_in")
    grad_x, d_norm_mix, dsc1, dsh1, dcsc1, dcsh1 = _mod_bwd(x2, ctx2, nm, sc1, sh1, csc1, csh1, dhx, dx_res)

    dw_in_t = jnp.concatenate([dw_in_work[:c_u], dw_in_work[c_xbc:], dw_in_work[c_u:c_xbc]], axis=0)
    wire, own = _reduce_scatter_begin([dw_in_t], "b")
    (g_in_t,) = _reduce_scatter_end(own, _run_side(_scatter_side(wire), "rs_scatter_b"), "b")
    g_in, g_gate, g_up = g_in_t.T, g_gate_t.T, g_up_t.T

    zeros_d = jnp.zeros((1, d), F32)
    dm_lat = jnp.concatenate([dsh1, dsc1, dg1, dsh2, dsc2, dg2], axis=1)
    dm_ctx = jnp.concatenate([dcsh1, dcsc1] + [zeros_d] * 4, axis=1)
    d_d_skip = ddsk.reshape(h, p).sum(axis=1)
    replicated = [dm_lat + dm_ctx, d_norm_mix, d_conv_b, d_dt_bias, d_a_log, d_d_skip, d_ssm_norm, d_cf_b, d_ln_g, d_ln_b,
                  d_norm_ffn, d_norm_final]
    rep_w = [b_mod, norm_mix, ssm_conv_b, dt_bias, a_log, d_skip, ssm_norm, cf_conv_b, cf_ln_g, cf_ln_b, norm_ffn, norm_final]
    rep_m = [m_b_mod, m_norm_mix, m_ssm_conv_b, m_dt_bias, m_a_log, m_d_skip, m_ssm_norm, m_cf_conv_b, m_cf_ln_g, m_cf_ln_b,
             m_norm_ffn, m_norm_final]
    rep_v = [v_b_mod, v_norm_mix, v_ssm_conv_b, v_dt_bias, v_a_log, v_d_skip, v_ssm_norm, v_cf_conv_b, v_cf_ln_g, v_cf_ln_b,
             v_norm_ffn, v_norm_final]
    quantum = 8 * LANES
    rep_flat = _pack(replicated, quantum)
    n_rep = rep_flat.shape[0]
    summed_parts = [rep_flat, _pack([d_conv_w, d_cf_w, dm_ctx], quantum)]
    n_sum = n_rep + summed_parts[1].shape[0]
    everything = jnp.concatenate(summed_parts + [_pack([dm_lat], quantum)])
    gathered = _allgather_small(everything.reshape(8, -1), "ag_small_grads")
    w8 = gathered.shape[1]
    summed = _sum_devices(gathered).reshape(-1)
    dm_lat_all = gathered.reshape(N_DEV, 8 * w8)[:, n_sum:n_sum + 6 * d]
    off = n_rep
    g_conv_w_full = summed[off:off + k5 * conv_dim].reshape(k5, conv_dim)
    off += k5 * conv_dim
    g_cf_w_full = summed[off:off + k31 * d].reshape(k31, d)
    off += k31 * d
    dm_ctx_all = summed[off:off + 6 * d].reshape(1, 6 * d)
    g_conv_w = lax.dynamic_slice(g_conv_w_full, (0, chip * cw_shard.shape[1]), cw_shard.shape)
    g_cf_w = lax.dynamic_slice(g_cf_w_full, (0, chip * cfw_shard.shape[1]), cfw_shard.shape)

    dm_rows = jnp.concatenate([dm_lat_all, dm_ctx_all, jnp.zeros((7, 6 * d), F32)], axis=0)
    dm_mine = lax.dynamic_slice(dm_rows, (0, chip * ws), (16, ws))
    g_w_mod, ds_part = _ada_bwd(s_rows.T, w_mod[0], dm_mine)
    ds_all = _allgather_small(ds_part[8:16], "ag_c_ctx")
    g_c_ctx = _c_ctx_grad(ds_all, row(c_ctx))

    grads, deltas, new_ms, new_vs = {}, {}, {}, {}

    def update(name, w2, g2, m2, v2, shape):
        dl, mm, vv = _adamw(w2, g2, m2, v2, f"adamw_{name}")
        grads[name], deltas[name], new_ms[name], new_vs[name] = (t.reshape(shape) for t in (g2, dl, mm, vv))

    for name, w_, g_, m_, v_ in [
            ("w_mod", w_mod, g_w_mod, m_w_mod, v_w_mod), ("w_in", w_in, g_in, m_w_in, v_w_in),
            ("ssm_conv_w", ssm_conv_w, g_conv_w, m_ssm_conv_w, v_ssm_conv_w),
            ("cf_conv_w", cf_conv_w, g_cf_w, m_cf_conv_w, v_cf_conv_w),
            ("w_proj_a", w_proj_a, g_pa, m_w_proj_a, v_w_proj_a), ("w_proj_b", w_proj_b, g_pb, m_w_proj_b, v_w_proj_b),
            ("w_out", w_out, g_out, m_w_out, v_w_out), ("w_ffn_gate", w_ffn_gate, g_gate, m_w_ffn_gate, v_w_ffn_gate),
            ("w_ffn_up", w_ffn_up, g_up, m_w_ffn_up, v_w_ffn_up), ("w_ffn_down", w_ffn_down, g_dn, m_w_ffn_down, v_w_ffn_down)]:
        update(name, w_[0], g_, m_[0], v_[0], w_.shape)
    update("c_ctx", row(c_ctx), g_c_ctx, row(m_c_ctx), row(v_c_ctx), c_ctx.shape)

    rep_names = ["b_mod", "norm_mix", "ssm_conv_b", "dt_bias", "a_log", "d_skip", "ssm_norm", "cf_conv_b", "cf_ln_g", "cf_ln_b",
                 "norm_ffn", "norm_final"]
    as8 = lambda vs: _pack(vs, quantum).reshape(8, -1)
    g8 = summed[:n_rep].reshape(8, -1)
    d8, m8, v8 = _adamw(as8(rep_w), g8, as8(rep_m), as8(rep_v), "adamw_replicated")
    off = 0
    for name, w_ in zip(rep_names, rep_w):
        size = w_.size
        for store, packed8 in ((grads, g8), (deltas, d8), (new_ms, m8), (new_vs, v8)):
            store[name] = packed8.reshape(-1)[off:off + size].reshape(w_.shape)
        off += size

    order = ["c_ctx", "w_mod", "b_mod", "norm_mix", "w_in", "ssm_conv_w", "ssm_conv_b", "dt_bias", "a_log", "d_skip", "ssm_norm",
             "cf_conv_w", "cf_conv_b", "cf_ln_g", "cf_ln_b", "w_proj_a", "w_proj_b", "w_out", "norm_ffn", "w_ffn_gate", "w_ffn_up",
             "w_ffn_down", "norm_final"]
    total_loss = lax.psum(loss[0, 0], ("x", "y", "c"))
    return (total_loss, grad_x.reshape(x.shape), *[grads[k] for k in order], *[deltas[k] for k in order],
            *[new_ms[k] for k in order], *[new_vs[k] for k in order])
```

```python
import functools

import jax
import jax.numpy as jnp
from jax import lax
from jax.experimental import pallas as pl
from jax.experimental.pallas import tpu as pltpu

F32 = jnp.float32
MXU_DTYPE = jnp.bfloat16
WIRE_DTYPE = jnp.bfloat16
HI = lax.Precision.HIGHEST
EPS = 1e-6
SSM_GROUPS = 8
SSM_STATE = 128
CHUNK = 128
GRID_W = 64
LANES = 128
HEAD_COLS = 16
VMEM_LIMIT = 52 * 1024 * 1024
ADAM_LR, ADAM_B1, ADAM_B2, ADAM_EPS, ADAM_WD, ADAM_STEP = 0.001, 0.9, 0.999, 1e-08, 0.01, 10
MESH = pl.DeviceIdType.MESH
N_SHARD = 4
N_DEV = 8


def _cp(sem=None):
    kw = dict(vmem_limit_bytes=VMEM_LIMIT)
    if sem is not None:
        kw["dimension_semantics"] = sem
    return pltpu.CompilerParams(**kw)


def _tile(n, target, q):
    best = None
    for t in range(q, min(n, target) + 1, q):
        if n % t == 0:
            best = t
    return best if best is not None else n


def _acc(ref, val, i):
    @pl.when(i == 0)
    def _():
        ref[...] = val

    @pl.when(i > 0)
    def _():
        ref[...] += val


def _bc_spec(w):
    return pl.BlockSpec((1, w), lambda *_: (0, 0))


def _rms(x, w):
    return x * lax.rsqrt(jnp.mean(x * x, axis=-1, keepdims=True) + EPS) * w


def _silu(x):
    return x * jax.nn.sigmoid(x)


def _f_mod(x, w, sc, sh):
    return _rms(x, w) * (1.0 + sc) + sh


def _f_gate(yf, yr, xs, z, dsk, wn):
    return _rms((yf + yr + dsk * xs) * _silu(z), wn)


def _f_ln(cv, g, b):
    mu = jnp.mean(cv, axis=-1, keepdims=True)
    xc = cv - mu
    var = jnp.mean(xc * xc, axis=-1, keepdims=True)
    return _silu(xc * lax.rsqrt(var + EPS) * g + b)


def _f_merge(ya, yb, ga, gb):
    return jax.nn.sigmoid(ga) * ya + jax.nn.sigmoid(gb) * yb


def _f_res(x, mix, g1, wn, sc2, sh2):
    x1 = x + g1 * mix
    return x1, _rms(x1, wn) * (1.0 + sc2) + sh2


def _f_swiglu(gt, up):
    return _silu(gt) * up


def _f_loss(x1, dn, g2, wn, tgt):
    out = _rms(x1 + g2 * dn, wn)
    err = out - tgt
    per_tok = jnp.mean(err * err, axis=-1, keepdims=True)
    return 0.5 * jnp.sum(per_tok, axis=0, keepdims=True)


def _matmul(a, b, *, ta=False, tb=False, out_dtype=F32, tm=1024, tn=512, tk=2048, name, side=None, n_outer=False):
    m, k = (a.shape[1], a.shape[0]) if ta else a.shape
    n = b.shape[0] if tb else b.shape[1]
    assert (b.shape[1] if tb else b.shape[0]) == k, (a.shape, b.shape, ta, tb)
    tm, tn, tk = _tile(m, tm, LANES if ta else 16), _tile(n, tn, LANES), _tile(k, tk, LANES)
    grid = (n // tn, m // tm, k // tk) if n_outer else (m // tm, n // tn, k // tk)
    ij = (lambda g0, g1: (g1, g0)) if n_outer else (lambda g0, g1: (g0, g1))
    nk = grid[2]
    dims = (((0 if ta else 1,), (1 if tb else 0,)), ((), ()))
    n_in = len(side.inputs) if side else 0
    n_out = len(side.out_shapes) if side else 0

    def body(a_ref, b_ref, *rest):
        side_in, o_ref, side_out, scratch = rest[:n_in], rest[n_in], rest[n_in + 1:n_in + 1 + n_out], rest[n_in + 1 + n_out:]
        steps = [pl.program_id(axis) for axis in range(3)]
        if side:
            @pl.when((steps[0] == 0) & (steps[1] == 0) & (steps[2] == 0))
            def _():
                side.start(side_in, side_out, *scratch[-2:])

        prod = lax.dot_general(a_ref[...].astype(MXU_DTYPE), b_ref[...].astype(MXU_DTYPE), dims,
                               preferred_element_type=F32)
        if nk == 1:
            o_ref[...] = prod.astype(o_ref.dtype)
        else:
            acc = scratch[0]
            _acc(acc, prod, steps[2])

            @pl.when(steps[2] == nk - 1)
            def _():
                o_ref[...] = acc[...].astype(o_ref.dtype)

        if side:
            @pl.when((steps[0] == grid[0] - 1) & (steps[1] == grid[1] - 1) & (steps[2] == nk - 1))
            def _():
                side.finish(side_in, side_out, *scratch[-2:])

    a_spec = (pl.BlockSpec((tk, tm), lambda g0, g1, kk: (kk, ij(g0, g1)[0])) if ta
              else pl.BlockSpec((tm, tk), lambda g0, g1, kk: (ij(g0, g1)[0], kk)))
    b_spec = (pl.BlockSpec((tn, tk), lambda g0, g1, kk: (ij(g0, g1)[1], kk)) if tb
              else pl.BlockSpec((tk, tn), lambda g0, g1, kk: (kk, ij(g0, g1)[1])))
    out = pl.pallas_call(
        body, name=name, grid=grid, in_specs=[a_spec, b_spec] + [_HBM] * n_in,
        out_specs=[pl.BlockSpec((tm, tn), lambda g0, g1, kk: ij(g0, g1))] + [_HBM] * n_out,
        out_shape=[jax.ShapeDtypeStruct((m, n), out_dtype)] + (side.out_shapes if side else []),
        scratch_shapes=([] if nk == 1 else [pltpu.VMEM((tm, tn), F32)]) + (side.scratch() if side else []),
        compiler_params=_cp(("arbitrary",) * 3 if side else ("parallel", "parallel", "arbitrary")),
    )(a, b, *(side.inputs if side else []))
    return out if side else out[0]


def _mesh_pos():
    return lax.axis_index("x"), lax.axis_index("y"), lax.axis_index("c")


def _other_chips(x, y):
    return [(1 - x, y), (x, 1 - y), (1 - x, 1 - y)]


def _allgather_small(v, name):
    m_per, n = v.shape

    def body(x_ref, out_ref, send_sems, recv_sems, local_sem):
        x, y, c = _mesh_pos()
        me, sibling = (x, y, c), (x, y, 1 - c)
        chips = _other_chips(x, y)

        def rows(px, py, pc):
            return out_ref.at[pl.ds((4 * px + 2 * py + pc) * m_per, m_per), :]

        def copy(k, block, to, src=None):
            return pltpu.make_async_remote_copy(
                src_ref=rows(*block) if src is None else src, dst_ref=rows(*block),
                send_sem=send_sems.at[k], recv_sem=recv_sems.at[k], device_id=to, device_id_type=MESH)

        mine = pltpu.make_async_copy(x_ref, rows(*me), local_sem)
        mine.start()
        first = [copy(0, me, sibling, src=x_ref)]
        first += [copy(1 + j, me, (*chip, c), src=x_ref) for j, chip in enumerate(chips)]
        for cp in first:
            cp.start()
        passed = [copy(4 + j, (*chip, c), sibling) for j, chip in enumerate(chips)]
        for j, chip in enumerate(chips):
            copy(1 + j, (*chip, c), me).wait_recv()
            passed[j].start()
        copy(0, sibling, me).wait_recv()
        for j, chip in enumerate(chips):
            copy(4 + j, (*chip, 1 - c), me).wait_recv()
        for cp in first + passed:
            cp.wait_send()
        mine.wait()

    return pl.pallas_call(
        body, name=name, out_shape=jax.ShapeDtypeStruct((N_DEV * m_per, n), v.dtype),
        in_specs=[pl.BlockSpec(memory_space=pltpu.VMEM)], out_specs=pl.BlockSpec(memory_space=pltpu.VMEM),
        scratch_shapes=[pltpu.SemaphoreType.DMA((7,)), pltpu.SemaphoreType.DMA((7,)), pltpu.SemaphoreType.DMA],
        compiler_params=_cp(),
    )(v)


_HBM = pl.BlockSpec(memory_space=pltpu.HBM)


class _Side:
    def __init__(self, inputs, out_shapes, n_sems, start, finish):
        self.inputs, self.out_shapes, self.n_sems, self.start, self.finish = inputs, out_shapes, n_sems, start, finish

    def scratch(self):
        return [pltpu.SemaphoreType.DMA((self.n_sems,)), pltpu.SemaphoreType.DMA((self.n_sems,))]


def _run_side(side, name):
    n_in = len(side.inputs)

    def body(*refs):
        src, dst, sems = refs[:n_in], refs[n_in:-2], refs[-2:]
        side.start(src, dst, *sems)
        side.finish(src, dst, *sems)

    return pl.pallas_call(
        body, name=name, out_shape=side.out_shapes, in_specs=[_HBM] * n_in, out_specs=[_HBM] * len(side.out_shapes),
        scratch_shapes=side.scratch(), compiler_params=_cp(),
    )(*side.inputs)


def _gather_side(shards):
    n = len(shards)

    def plan(src, dst, send_sems, recv_sems):
        x, y, c = _mesh_pos()
        chips = _other_chips(x, y)

        def half(i, px, py, pc):
            r = src[i].shape[0]
            return dst[i].at[pl.ds(pl.multiple_of((2 * px + py) * r + pc * (r // 2), 16), r // 2), :]

        def copy(i, k, block, to, own=False):
            r = src[i].shape[0]
            mine = src[i].at[pl.ds(pl.multiple_of(c * (r // 2), 16), r // 2), :]
            return pltpu.make_async_remote_copy(
                src_ref=mine if own else half(i, *block), dst_ref=half(i, *block), send_sem=send_sems.at[6 * i + k],
                recv_sem=recv_sems.at[6 * i + k], device_id=to, device_id_type=MESH)

        first = [copy(i, j, (x, y, c), (*chip, c), own=True) for i in range(n) for j, chip in enumerate(chips)]
        return (x, y, c), chips, copy, first

    def start(src, dst, send_sems, recv_sems):
        for cp in plan(src, dst, send_sems, recv_sems)[3]:
            cp.start()

    def finish(src, dst, send_sems, recv_sems):
        (x, y, c), chips, copy, first = plan(src, dst, send_sems, recv_sems)
        passed = []
        for i in range(n):
            for j, chip in enumerate(chips):
                copy(i, j, (*chip, c), (x, y, c)).wait_recv()
                passed.append(copy(i, 3 + j, (*chip, c), (x, y, 1 - c)))
                passed[-1].start()
        for i in range(n):
            for j, chip in enumerate(chips):
                copy(i, 3 + j, (*chip, 1 - c), (x, y, c)).wait_recv()
        for cp in first + passed:
            cp.wait_send()

    shapes = [jax.ShapeDtypeStruct((N_SHARD * s.shape[0], s.shape[1]), s.dtype) for s in shards]
    return _Side(list(shards), shapes, 6 * n, start, finish)


def _fill_own_rows(gathered, shards):
    chip = 2 * lax.axis_index("x") + lax.axis_index("y")
    return [lax.dynamic_update_slice(full, s, (chip * s.shape[0], 0)) for full, s in zip(gathered, shards)]


def _swap_halves(parts, name):
    n = len(parts)

    def body(*refs):
        src, dst = refs[:n], refs[n:2 * n]
        send_sems, recv_sems = refs[2 * n:]
        x, y, c = _mesh_pos()
        copies = [pltpu.make_async_remote_copy(
            src_ref=src[i].at[s, 1 - c], dst_ref=dst[i].at[s], send_sem=send_sems.at[N_SHARD * i + s],
            recv_sem=recv_sems.at[N_SHARD * i + s], device_id=(x, y, 1 - c), device_id_type=MESH)
            for i in range(n) for s in range(N_SHARD)]
        for cp in copies:
            cp.start()
        for cp in copies:
            cp.wait()

    return pl.pallas_call(
        body, name=name,
        out_shape=[jax.ShapeDtypeStruct((N_SHARD,) + p.shape[2:], p.dtype) for p in parts],
        in_specs=[_HBM] * n, out_specs=[_HBM] * n,
        scratch_shapes=[pltpu.SemaphoreType.DMA((N_SHARD * n,)), pltpu.SemaphoreType.DMA((N_SHARD * n,))],
        compiler_params=_cp(),
    )(*parts)


def _scatter_side(parts):
    n = len(parts)

    def copies(src, dst, send_sems, recv_sems):
        x, y, c = _mesh_pos()
        return [pltpu.make_async_remote_copy(
            src_ref=src[i].at[2 * chip[0] + chip[1]], dst_ref=dst[i].at[j], send_sem=send_sems.at[3 * i + j],
            recv_sem=recv_sems.at[3 * i + j], device_id=(*chip, c), device_id_type=MESH)
            for i in range(n) for j, chip in enumerate(_other_chips(x, y))]

    def start(*refs):
        for cp in copies(*refs):
            cp.start()

    def finish(*refs):
        for cp in copies(*refs):
            cp.wait()

    shapes = [jax.ShapeDtypeStruct((3,) + p.shape[1:], p.dtype) for p in parts]
    return _Side(list(parts), shapes, 3 * n, start, finish)


def _join_halves(halves, name):
    n = len(halves)

    def body(*refs):
        src, dst = refs[:n], refs[n:2 * n]
        send_sems, recv_sems = refs[2 * n:]
        x, y, c = _mesh_pos()
        remote = [pltpu.make_async_remote_copy(
            src_ref=src[i], dst_ref=dst[i].at[c], send_sem=send_sems.at[i], recv_sem=recv_sems.at[i],
            device_id=(x, y, 1 - c), device_id_type=MESH) for i in range(n)]
        for cp in remote:
            cp.start()
        for i in range(n):
            pltpu.make_async_remote_copy(
                src_ref=src[i], dst_ref=dst[i].at[1 - c], send_sem=send_sems.at[i], recv_sem=recv_sems.at[i],
                device_id=(x, y, 1 - c), device_id_type=MESH).wait_recv()
        for cp in remote:
            cp.wait_send()

    joined = pl.pallas_call(
        body, name=name,
        out_shape=[jax.ShapeDtypeStruct((2,) + h.shape, h.dtype) for h in halves],
        in_specs=[_HBM] * n, out_specs=[_HBM] * n,
        scratch_shapes=[pltpu.SemaphoreType.DMA((n,)), pltpu.SemaphoreType.DMA((n,))],
        compiler_params=_cp(),
    )(*halves)
    c = lax.axis_index("c")
    return [lax.dynamic_update_slice(j, h[None], (c, 0, 0)) for j, h in zip(joined, halves)]


def _pair_sum(g, got, name):
    _, _, hr, d = g.shape
    t = _tile(hr, 256, 16)

    def body(g0_ref, g1_ref, got_ref, wire_ref, own_ref):
        x, y, c = _mesh_pos()
        total = jnp.where(c == 0, g0_ref[0, 0], g1_ref[0, 0]) + got_ref[0]
        wire_ref[0] = total.astype(wire_ref.dtype)

        @pl.when(pl.program_id(1) == 2 * x + y)
        def _():
            own_ref[...] = total

    return pl.pallas_call(
        body, name=name, grid=(hr // t, N_SHARD),
        in_specs=[pl.BlockSpec((1, 1, t, d), lambda i, s: (s, 0, i, 0)), pl.BlockSpec((1, 1, t, d), lambda i, s: (s, 1, i, 0)),
                  pl.BlockSpec((1, t, d), lambda i, s: (s, i, 0))],
        out_specs=[pl.BlockSpec((1, t, d), lambda i, s: (s, i, 0)), pl.BlockSpec((t, d), lambda i, s: (i, 0))],
        out_shape=[jax.ShapeDtypeStruct((N_SHARD, hr, d), WIRE_DTYPE), jax.ShapeDtypeStruct((hr, d), F32)],
        compiler_params=_cp(("parallel", "arbitrary")),
    )(g, g, got)


def _sum_partials(own, recv, name):
    hr, d = own.shape
    t = _tile(hr, 256, 16)

    def body(own_ref, recv_ref, o_ref):
        total = own_ref[...]
        for j in range(3):
            total = total + recv_ref[j].astype(F32)
        o_ref[...] = total

    blk = pl.BlockSpec((t, d), lambda i: (i, 0))
    return pl.pallas_call(
        body, name=name, grid=(hr // t,), in_specs=[blk, pl.BlockSpec((3, t, d), lambda i: (0, i, 0))], out_specs=blk,
        out_shape=jax.ShapeDtypeStruct((hr, d), F32), compiler_params=_cp(("parallel",)),
    )(own, recv)


def _reduce_scatter_begin(grads, tag):
    split = [g.reshape(N_SHARD, 2, g.shape[0] // (2 * N_SHARD), g.shape[1]) for g in grads]
    got = _swap_halves(split, f"rs_swap_halves_{tag}")
    sums = [_pair_sum(g, h, f"rs_pair_sum_{tag}{i}") for i, (g, h) in enumerate(zip(split, got))]
    return [w for w, _ in sums], [own for _, own in sums]


def _reduce_scatter_end(own, recv, tag):
    halves = [_sum_partials(o, rv, f"rs_sum_{tag}{i}") for i, (o, rv) in enumerate(zip(own, recv))]
    return [j.reshape(-1, j.shape[-1]) for j in _join_halves(halves, f"rs_join_halves_{tag}")]


def _mod_fwd(x, ctx, nw, sc, sh, csc, csh):
    l, d = x.shape
    lc = ctx.shape[0]
    t = min(256, lc)
    nl, nc = l // t, lc // t

    def body(x_ref, c_ref, nw_ref, sc_ref, sh_ref, csc_ref, csh_ref, o_ref):
        i = pl.program_id(0)

        @pl.when(i < nl)
        def _():
            o_ref[...] = _f_mod(x_ref[...], nw_ref[...], sc_ref[...], sh_ref[...]).astype(o_ref.dtype)

        @pl.when(i >= nl)
        def _():
            o_ref[...] = _f_mod(c_ref[...], nw_ref[...], csc_ref[...], csh_ref[...]).astype(o_ref.dtype)

    return pl.pallas_call(
        body, name="mod_fwd", grid=(nl + nc,),
        in_specs=[pl.BlockSpec((t, d), lambda i: (jnp.minimum(i, nl - 1), 0)),
                  pl.BlockSpec((t, d), lambda i: (jnp.maximum(i - nl, 0), 0))] + [_bc_spec(d)] * 5,
        out_specs=pl.BlockSpec((t, d), lambda i: (i, 0)),
        out_shape=jax.ShapeDtypeStruct((l + lc, d), MXU_DTYPE), compiler_params=_cp(("arbitrary",)),
    )(x, ctx, nw, sc, sh, csc, csh)


def _mod_bwd(x, ctx, nw, sc, sh, csc, csh, dhx, dx_res):
    l, d = x.shape
    lc = ctx.shape[0]
    t = min(256, lc)
    nl, nc = l // t, lc // t

    def body(x_ref, c_ref, nw_ref, sc_ref, sh_ref, csc_ref, csh_ref, dh_ref, dr_ref,
             dx_ref, dnw_ref, dsc_ref, dsh_ref, dcsc_ref, dcsh_ref):
        i = pl.program_id(0)

        @pl.when(i == 0)
        def _():
            for r in (dnw_ref, dsc_ref, dsh_ref, dcsc_ref, dcsh_ref):
                r[...] = jnp.zeros_like(r)

        @pl.when(i < nl)
        def _():
            _, vjp = jax.vjp(_f_mod, x_ref[...], nw_ref[...], sc_ref[...], sh_ref[...])
            dx, dnw, dsc, dsh = vjp(dh_ref[...])
            dx_ref[...] = dx + dr_ref[...]
            dnw_ref[...] += dnw
            dsc_ref[...] += dsc
            dsh_ref[...] += dsh

        @pl.when(i >= nl)
        def _():
            _, vjp = jax.vjp(_f_mod, c_ref[...], nw_ref[...], csc_ref[...], csh_ref[...])
            _, dnw, dsc, dsh = vjp(dh_ref[...])
            dnw_ref[...] += dnw
            dcsc_ref[...] += dsc
            dcsh_ref[...] += dsh

    lat = pl.BlockSpec((t, d), lambda i: (jnp.minimum(i, nl - 1), 0))
    vec = jax.ShapeDtypeStruct((1, d), F32)
    return pl.pallas_call(
        body, name="mod_bwd", grid=(nl + nc,),
        in_specs=[lat, pl.BlockSpec((t, d), lambda i: (jnp.maximum(i - nl, 0), 0))] + [_bc_spec(d)] * 5
        + [pl.BlockSpec((t, d), lambda i: (i, 0)), lat],
        out_specs=[lat] + [_bc_spec(d)] * 5,
        out_shape=[jax.ShapeDtypeStruct((l, d), F32)] + [vec] * 5, compiler_params=_cp(("arbitrary",)),
    )(x, ctx, nw, sc, sh, csc, csh, dhx, dx_res)


def _gate_fwd(yf, yr, xbc, proj, dsk, wn, l, di):
    t = 128

    def body(yf_ref, yr_ref, xs_ref, z_ref, dsk_ref, wn_ref, o_ref):
        o_ref[...] = _f_gate(yf_ref[...], yr_ref[...], xs_ref[...], z_ref[...], dsk_ref[...], wn_ref[...]).astype(o_ref.dtype)

    row = pl.BlockSpec((t, di), lambda i: (i, 0))
    return pl.pallas_call(
        body, name="gate_fwd", grid=(l // t,), in_specs=[row] * 4 + [_bc_spec(di)] * 2, out_specs=row,
        out_shape=jax.ShapeDtypeStruct((l, di), MXU_DTYPE), compiler_params=_cp(("parallel",)),
    )(yf, yr, xbc, proj, dsk, wn)


def _gate_bwd(yf, yr, xbc, proj, dsk, wn, dya, l, lc, di):
    t = 128
    nl, nc = l // t, lc // t

    def body(yf_ref, yr_ref, xs_ref, z_ref, dsk_ref, wn_ref, g_ref, dy_ref, dz_ref, ddsk_ref, dwn_ref):
        i = pl.program_id(0)

        @pl.when(i == 0)
        def _():
            ddsk_ref[...] = jnp.zeros_like(ddsk_ref)
            dwn_ref[...] = jnp.zeros_like(dwn_ref)

        @pl.when(i < nl)
        def _():
            _, vjp = jax.vjp(_f_gate, yf_ref[...], yr_ref[...], xs_ref[...], z_ref[...], dsk_ref[...], wn_ref[...])
            dyf, _, _, dz, ddsk, dwn = vjp(g_ref[...])
            dy_ref[...] = dyf
            dz_ref[...] = dz.astype(dz_ref.dtype)
            ddsk_ref[...] += ddsk
            dwn_ref[...] += dwn

        @pl.when(i >= nl)
        def _():
            dz_ref[...] = jnp.zeros_like(dz_ref)

    lat = pl.BlockSpec((t, di), lambda i: (jnp.minimum(i, nl - 1), 0))
    vec = jax.ShapeDtypeStruct((1, di), F32)
    return pl.pallas_call(
        body, name="gate_bwd", grid=(nl + nc,),
        in_specs=[lat] * 4 + [_bc_spec(di)] * 2 + [lat],
        out_specs=[lat, pl.BlockSpec((t, di), lambda i: (i, 0))] + [_bc_spec(di)] * 2,
        out_shape=[jax.ShapeDtypeStruct((l, di), F32), jax.ShapeDtypeStruct((l + lc, di), MXU_DTYPE)] + [vec] * 2,
        compiler_params=_cp(("arbitrary",)),
    )(yf, yr, xbc, proj, dsk, wn, dya)


def _ln_fwd(cv, g, b):
    l, d = cv.shape
    t = 256

    def body(cv_ref, g_ref, b_ref, o_ref):
        o_ref[...] = _f_ln(cv_ref[...], g_ref[...], b_ref[...]).astype(o_ref.dtype)

    row = pl.BlockSpec((t, d), lambda i: (i, 0))
    return pl.pallas_call(
        body, name="ln_fwd", grid=(l // t,), in_specs=[row] + [_bc_spec(d)] * 2, out_specs=row,
        out_shape=jax.ShapeDtypeStruct((l, d), MXU_DTYPE), compiler_params=_cp(("parallel",)),
    )(cv, g, b)


def _ln_bwd(cv, g, b, dcf):
    l, d = cv.shape
    t = 256

    def body(cv_ref, g_ref, b_ref, dcf_ref, dcv_ref, dg_ref, db_ref):
        _, vjp = jax.vjp(_f_ln, cv_ref[...], g_ref[...], b_ref[...])
        dcv, dg, db = vjp(dcf_ref[...])
        dcv_ref[...] = dcv
        i = pl.program_id(0)
        _acc(dg_ref, dg, i)
        _acc(db_ref, db, i)

    row = pl.BlockSpec((t, d), lambda i: (i, 0))
    vec = jax.ShapeDtypeStruct((1, d), F32)
    return pl.pallas_call(
        body, name="ln_bwd", grid=(l // t,), in_specs=[row] + [_bc_spec(d)] * 2 + [row],
        out_specs=[row] + [_bc_spec(d)] * 2, out_shape=[jax.ShapeDtypeStruct((l, d), F32), vec, vec],
        compiler_params=_cp(("arbitrary",)),
    )(cv, g, b, dcf)


def _merge_fwd(ya, yb, proj, ga_blk):
    l, d = ya.shape
    t = 256

    def body(ya_ref, yb_ref, ga_ref, gb_ref, o_ref):
        o_ref[...] = _f_merge(ya_ref[...], yb_ref[...], ga_ref[...], gb_ref[...]).astype(o_ref.dtype)

    row = pl.BlockSpec((t, d), lambda i: (i, 0))
    return pl.pallas_call(
        body, name="merge_fwd", grid=(l // t,),
        in_specs=[row, row, pl.BlockSpec((t, d), lambda i: (i, ga_blk)), pl.BlockSpec((t, d), lambda i: (i, ga_blk + 1))],
        out_specs=row, out_shape=jax.ShapeDtypeStruct((l, d), MXU_DTYPE), compiler_params=_cp(("parallel",)),
    )(ya, yb, proj, proj)


def _merge_bwd(ya, yb, proj, ga_blk, dmerged, lc):
    l, d = ya.shape
    t = min(256, lc)
    nl, nc = l // t, lc // t

    def body(ya_ref, yb_ref, ga_ref, gb_ref, g_ref, dya_ref, dyb_ref, dga_ref, dgb_ref):
        i = pl.program_id(0)

        @pl.when(i < nl)
        def _():
            _, vjp = jax.vjp(_f_merge, ya_ref[...], yb_ref[...], ga_ref[...], gb_ref[...])
            dya, dyb, dga, dgb = vjp(g_ref[...])
            dya_ref[...] = dya.astype(dya_ref.dtype)
            dyb_ref[...] = dyb.astype(dyb_ref.dtype)
            dga_ref[...] = dga.astype(dga_ref.dtype)
            dgb_ref[...] = dgb.astype(dgb_ref.dtype)

        @pl.when(i >= nl)
        def _():
            dga_ref[...] = jnp.zeros_like(dga_ref)
            dgb_ref[...] = jnp.zeros_like(dgb_ref)

    lat = pl.BlockSpec((t, d), lambda i: (jnp.minimum(i, nl - 1), 0))
    full = pl.BlockSpec((t, d), lambda i: (i, 0))
    return pl.pallas_call(
        body, name="merge_bwd", grid=(nl + nc,),
        in_specs=[lat, lat, pl.BlockSpec((t, d), lambda i: (jnp.minimum(i, nl - 1), ga_blk)),
                  pl.BlockSpec((t, d), lambda i: (jnp.minimum(i, nl - 1), ga_blk + 1)), lat],
        out_specs=[lat, lat, full, full],
        out_shape=[jax.ShapeDtypeStruct((l, d), MXU_DTYPE)] * 2 + [jax.ShapeDtypeStruct((l + lc, d), MXU_DTYPE)] * 2,
        compiler_params=_cp(("arbitrary",)),
    )(ya, yb, proj, proj, dmerged)


def _res_fwd(x, mix, g1, wn, sc2, sh2):
    l, d = x.shape
    t = 256

    def body(x_ref, m_ref, g1_ref, wn_ref, sc_ref, sh_ref, x1_ref, hx_ref):
        x1, hx = _f_res(x_ref[...], m_ref[...], g1_ref[...], wn_ref[...], sc_ref[...], sh_ref[...])
        x1_ref[...] = x1
        hx_ref[...] = hx.astype(hx_ref.dtype)

    row = pl.BlockSpec((t, d), lambda i: (i, 0))
    return pl.pallas_call(
        body, name="res_fwd", grid=(l // t,), in_specs=[row, row] + [_bc_spec(d)] * 4, out_specs=[row, row],
        out_shape=[jax.ShapeDtypeStruct((l, d), F32), jax.ShapeDtypeStruct((l, d), MXU_DTYPE)],
        compiler_params=_cp(("parallel",)),
    )(x, mix, g1, wn, sc2, sh2)


def _res_bwd(x, mix, g1, wn, sc2, sh2, dx1, dhx2):
    l, d = x.shape
    t = 256

    def body(x_ref, m_ref, g1_ref, wn_ref, sc_ref, sh_ref, dx1_ref, dh_ref, dx_ref, dm_ref, dg1_ref, dwn_ref, dsc_ref, dsh_ref):
        _, vjp = jax.vjp(_f_res, x_ref[...], m_ref[...], g1_ref[...], wn_ref[...], sc_ref[...], sh_ref[...])
        dx, dm, dg1, dwn, dsc, dsh = vjp((dx1_ref[...], dh_ref[...]))
        dx_ref[...] = dx
        dm_ref[...] = dm.astype(dm_ref.dtype)
        i = pl.program_id(0)
        _acc(dg1_ref, dg1, i)
        _acc(dwn_ref, dwn, i)
        _acc(dsc_ref, dsc, i)
        _acc(dsh_ref, dsh, i)

    row = pl.BlockSpec((t, d), lambda i: (i, 0))
    vec = jax.ShapeDtypeStruct((1, d), F32)
    return pl.pallas_call(
        body, name="res_bwd", grid=(l // t,), in_specs=[row, row] + [_bc_spec(d)] * 4 + [row, row],
        out_specs=[row, row] + [_bc_spec(d)] * 4,
        out_shape=[jax.ShapeDtypeStruct((l, d), F32), jax.ShapeDtypeStruct((l, d), MXU_DTYPE)] + [vec] * 4,
        compiler_params=_cp(("arbitrary",)),
    )(x, mix, g1, wn, sc2, sh2, dx1, dhx2)


def _swiglu_fwd(gu, df):
    l = gu.shape[0]
    r = df // N_SHARD
    t = 256

    def body(gu_ref, o_ref):
        o_ref[...] = _f_swiglu(gu_ref[:, :r], gu_ref[:, r:]).astype(o_ref.dtype)

    return pl.pallas_call(
        body, name="swiglu_fwd", grid=(l // t, N_SHARD),
        in_specs=[pl.BlockSpec((t, 2 * r), lambda i, s: (i, s))], out_specs=pl.BlockSpec((t, r), lambda i, s: (i, s)),
        out_shape=jax.ShapeDtypeStruct((l, df), MXU_DTYPE), compiler_params=_cp(("parallel", "parallel")),
    )(gu)


def _swiglu_bwd(gu, dact, df):
    l = gu.shape[0]
    r = df // N_SHARD
    t = 256

    def body(gu_ref, da_ref, dgu_ref):
        _, vjp = jax.vjp(_f_swiglu, gu_ref[:, :r], gu_ref[:, r:])
        dg, du = vjp(da_ref[...])
        dgu_ref[:, :r] = dg.astype(dgu_ref.dtype)
        dgu_ref[:, r:] = du.astype(dgu_ref.dtype)

    pair = pl.BlockSpec((t, 2 * r), lambda i, s: (i, s))
    return pl.pallas_call(
        body, name="swiglu_bwd", grid=(l // t, N_SHARD), in_specs=[pair, pl.BlockSpec((t, r), lambda i, s: (i, s))],
        out_specs=pair, out_shape=jax.ShapeDtypeStruct((l, 2 * df), MXU_DTYPE),
        compiler_params=_cp(("parallel", "parallel")),
    )(gu, dact)


def _loss_and_grads(x1, dn, g2, wn, tgt):
    l, d = x1.shape
    t = 256

    def body(x1_ref, dn_ref, g2_ref, wn_ref, t_ref, loss_ref, dx_ref, ddn_ref, dg2_ref, dwn_ref):
        loss, vjp = jax.vjp(lambda a, b, c, e: _f_loss(a, b, c, e, t_ref[...]), x1_ref[...], dn_ref[...], g2_ref[...], wn_ref[...])
        dx, ddn, dg2, dwn = vjp(jnp.ones((1, 1), F32))
        dx_ref[...] = dx
        ddn_ref[...] = ddn.astype(ddn_ref.dtype)
        i = pl.program_id(0)
        _acc(loss_ref, loss, i)
        _acc(dg2_ref, dg2, i)
        _acc(dwn_ref, dwn, i)

    row = pl.BlockSpec((t, d), lambda i: (i, 0))
    vec = jax.ShapeDtypeStruct((1, d), F32)
    return pl.pallas_call(
        body, name="loss_and_grads", grid=(l // t,), in_specs=[row, row] + [_bc_spec(d)] * 2 + [row],
        out_specs=[pl.BlockSpec((1, 1), lambda i: (0, 0)), row, row] + [_bc_spec(d)] * 2,
        out_shape=[jax.ShapeDtypeStruct((1, 1), F32), jax.ShapeDtypeStruct((l, d), F32),
                   jax.ShapeDtypeStruct((l, d), MXU_DTYPE), vec, vec],
        compiler_params=_cp(("arbitrary",)),
    )(x1, dn, g2, wn, tgt)


PAD = 8


def _conv5_taps(s_ref, w_ref, l, lc, width):
    half = width // 2
    lat = sum(w_ref[k:k + 1, :] * s_ref[pl.ds(PAD + k - half, l), :] for k in range(width))
    ctx = sum(w_ref[k:k + 1, :] * s_ref[pl.ds(2 * PAD + l + k - half, lc), :] for k in range(width))
    return lat, ctx


def _fill_padded(s_ref, lat, ctx, l, lc):
    zeros = jnp.zeros((PAD, s_ref.shape[1]), F32)
    s_ref[pl.ds(0, PAD), :] = zeros
    s_ref[pl.ds(PAD, l), :] = lat
    s_ref[pl.ds(PAD + l, PAD), :] = zeros
    s_ref[pl.ds(2 * PAD + l, lc), :] = ctx
    s_ref[pl.ds(2 * PAD + l + lc, PAD), :] = zeros


def _conv5_fwd(proj, w, b, l, lc, col0, ncols):
    t_all = l + lc
    cw = LANES
    blk0 = col0 // cw
    width = w.shape[0]

    def body(x_ref, w_ref, b_ref, o_ref, s_ref):
        _fill_padded(s_ref, x_ref[pl.ds(0, l), :], x_ref[pl.ds(l, lc), :], l, lc)
        lat, ctx = _conv5_taps(s_ref, w_ref, l, lc, width)
        o_ref[pl.ds(0, l), :] = _silu(lat + b_ref[...])
        o_ref[pl.ds(l, lc), :] = _silu(ctx + b_ref[...])

    return pl.pallas_call(
        body, name="conv5_fwd", grid=(ncols // cw,),
        in_specs=[pl.BlockSpec((t_all, cw), lambda j: (0, blk0 + j)), pl.BlockSpec((width, cw), lambda j: (0, j)),
                  pl.BlockSpec((1, cw), lambda j: (0, j))],
        out_specs=pl.BlockSpec((t_all, cw), lambda j: (0, j)),
        out_shape=jax.ShapeDtypeStruct((t_all, ncols), F32),
        scratch_shapes=[pltpu.VMEM((t_all + 3 * PAD, cw), F32)], compiler_params=_cp(("parallel",)),
    )(proj, w, b)


def _conv5_bwd(proj, w, b, cots, l, lc, col0, seg0, ncols):
    t_all = l + lc
    cw = LANES
    blk0, sblk0 = col0 // cw, seg0 // cw
    width = w.shape[0]
    half = width // 2
    nc = len(cots)

    def body(*refs):
        x_ref, w_ref, b_ref = refs[:3]
        cot_refs = refs[3:3 + nc]
        dx_ref, dw_ref, db_ref, s_ref = refs[3 + nc:]
        x_lat, x_ctx = x_ref[pl.ds(0, l), :], x_ref[pl.ds(l, lc), :]
        _fill_padded(s_ref, x_lat, x_ctx, l, lc)
        pre_lat, pre_ctx = _conv5_taps(s_ref, w_ref, l, lc, width)
        g = sum(c[...] for c in cot_refs)

        def through_silu(pre, cot):
            _, vjp = jax.vjp(_silu, pre + b_ref[...])
            return vjp(cot)[0]

        d_lat = through_silu(pre_lat, g[:l])
        d_ctx = through_silu(pre_ctx, g[l:])
        db_ref[...] = jnp.sum(d_lat, axis=0, keepdims=True) + jnp.sum(d_ctx, axis=0, keepdims=True)
        for k in range(width):
            dw_ref[k:k + 1, :] = (
                jnp.sum(d_lat * s_ref[pl.ds(PAD + k - half, l), :], axis=0, keepdims=True)
                + jnp.sum(d_ctx * s_ref[pl.ds(2 * PAD + l + k - half, lc), :], axis=0, keepdims=True))
        _fill_padded(s_ref, d_lat, d_ctx, l, lc)
        dx_lat = sum(w_ref[k:k + 1, :] * s_ref[pl.ds(PAD - (k - half), l), :] for k in range(width))
        dx_ctx = sum(w_ref[k:k + 1, :] * s_ref[pl.ds(2 * PAD + l - (k - half), lc), :] for k in range(width))
        dx_ref[pl.ds(0, l), :] = dx_lat.astype(dx_ref.dtype)
        dx_ref[pl.ds(l, lc), :] = dx_ctx.astype(dx_ref.dtype)

    col = pl.BlockSpec((t_all, cw), lambda j: (0, j))
    return pl.pallas_call(
        body, name=f"conv5_bwd_{seg0}", grid=(ncols // cw,),
        in_specs=[pl.BlockSpec((t_all, cw), lambda j: (0, blk0 + sblk0 + j)),
                  pl.BlockSpec((width, cw), lambda j: (0, sblk0 + j)), pl.BlockSpec((1, cw), lambda j: (0, sblk0 + j))]
        + [col] * nc,
        out_specs=[col, pl.BlockSpec((width, cw), lambda j: (0, j)), pl.BlockSpec((1, cw), lambda j: (0, j))],
        out_shape=[jax.ShapeDtypeStruct((t_all, ncols), MXU_DTYPE), jax.ShapeDtypeStruct((width, ncols), F32),
                   jax.ShapeDtypeStruct((1, ncols), F32)],
        scratch_shapes=[pltpu.VMEM((t_all + 3 * PAD, cw), F32)], compiler_params=_cp(("parallel",)),
    )(proj, w, b, *cots)


def _conv31_fwd(proj, w, b, l, d, u_blk):
    cw = LANES
    width = w.shape[0]
    reach = (width // 2) * GRID_W
    nb = d // cw

    def body(u_ref, v_ref, w_ref, b_ref, o_ref, s_ref):
        s_ref[pl.ds(0, reach), :] = jnp.zeros((reach, cw), F32)
        s_ref[pl.ds(reach, l), :] = u_ref[...] * jax.nn.sigmoid(v_ref[...])
        s_ref[pl.ds(reach + l, reach), :] = jnp.zeros((reach, cw), F32)
        o_ref[...] = sum(w_ref[k:k + 1, :] * s_ref[pl.ds(k * GRID_W, l), :] for k in range(width)) + b_ref[...]

    return pl.pallas_call(
        body, name="conv31_fwd", grid=(nb,),
        in_specs=[pl.BlockSpec((l, cw), lambda j: (0, u_blk * nb + j)), pl.BlockSpec((l, cw), lambda j: (0, (u_blk + 1) * nb + j)),
                  pl.BlockSpec((width, cw), lambda j: (0, j)), pl.BlockSpec((1, cw), lambda j: (0, j))],
        out_specs=pl.BlockSpec((l, cw), lambda j: (0, j)), out_shape=jax.ShapeDtypeStruct((l, d), F32),
        scratch_shapes=[pltpu.VMEM((l + 2 * reach, cw), F32)], compiler_params=_cp(("parallel",)),
    )(proj, proj, w, b)


def _conv31_bwd(proj, w, dcv, l, lc, d, u_blk):
    cw = LANES
    width = w.shape[0]
    reach = (width // 2) * GRID_W
    nb = d // cw
    t_all = l + lc

    def body(u_ref, v_ref, w_ref, g_ref, du_ref, dv_ref, dw_ref, db_ref, s_ref):
        zeros = jnp.zeros((reach, cw), F32)
        s_ref[pl.ds(0, reach), :] = zeros
        s_ref[pl.ds(reach + l, reach), :] = zeros
        u, v, g = u_ref[...], v_ref[...], g_ref[...]
        s_ref[pl.ds(reach, l), :] = u * jax.nn.sigmoid(v)
        db_ref[...] = jnp.sum(g, axis=0, keepdims=True)
        for k in range(width):
            dw_ref[k:k + 1, :] = jnp.sum(g * s_ref[pl.ds(k * GRID_W, l), :], axis=0, keepdims=True)
        s_ref[pl.ds(reach, l), :] = g
        dt = sum(w_ref[k:k + 1, :] * s_ref[pl.ds((width - 1 - k) * GRID_W, l), :] for k in range(width))
        _, vjp = jax.vjp(lambda a, c: a * jax.nn.sigmoid(c), u, v)
        du, dv = vjp(dt)
        du_ref[pl.ds(0, l), :] = du.astype(du_ref.dtype)
        dv_ref[pl.ds(0, l), :] = dv.astype(dv_ref.dtype)
        du_ref[pl.ds(l, lc), :] = jnp.zeros((lc, cw), du_ref.dtype)
        dv_ref[pl.ds(l, lc), :] = jnp.zeros((lc, cw), dv_ref.dtype)

    pshape = jax.ShapeDtypeStruct((t_all, d), MXU_DTYPE)
    tall = pl.BlockSpec((t_all, cw), lambda j: (0, j))
    return pl.pallas_call(
        body, name="conv31_bwd", grid=(nb,),
        in_specs=[pl.BlockSpec((l, cw), lambda j: (0, u_blk * nb + j)), pl.BlockSpec((l, cw), lambda j: (0, (u_blk + 1) * nb + j)),
                  pl.BlockSpec((width, cw), lambda j: (0, j)), pl.BlockSpec((l, cw), lambda j: (0, j))],
        out_specs=[tall, tall, pl.BlockSpec((width, cw), lambda j: (0, j)), pl.BlockSpec((1, cw), lambda j: (0, j))],
        out_shape=[pshape, pshape, jax.ShapeDtypeStruct((width, d), F32), jax.ShapeDtypeStruct((1, d), F32)],
        scratch_shapes=[pltpu.VMEM((l + 2 * reach, cw), F32)], compiler_params=_cp(("parallel",)),
    )(proj, proj, w, dcv)


def _softplus(x):
    return jnp.maximum(x, 0.0) + jnp.log(1.0 + jnp.exp(-jnp.abs(x)))


def _dt_fwd(proj, bias, a, dt_blk):
    t_all = proj.shape[0]
    hh = bias.shape[1]
    q = CHUNK

    def body(r_ref, b_ref, a_ref, dt_ref, cs_ref, tc_ref, cst_ref):
        dt = _softplus(r_ref[...] + b_ref[...])
        dt_ref[...] = dt
        da = dt * a_ref[...]
        li, si = _iota((q, q), 0), _iota((q, q), 1)
        reverse_cols = _iota((q, hh), 1) >= hh // 2
        cs = jnp.where(reverse_cols, _dot((si >= li).astype(F32), da, exact=True), _dot((si <= li).astype(F32), da, exact=True))
        cs_ref[...] = cs
        cst_ref[...] = cs.T
        total = jnp.where(_iota((1, hh), 1) >= hh // 2, cs_ref[0:1, :], cs_ref[q - 1:q, :])
        tc_ref[...] = total - cs

    row = pl.BlockSpec((q, hh), lambda i: (i, 0))
    shape = jax.ShapeDtypeStruct((t_all, hh), F32)
    return pl.pallas_call(
        body, name="dt_fwd", grid=(t_all // q,),
        in_specs=[pl.BlockSpec((q, hh), lambda i: (i, dt_blk)), _bc_spec(hh), _bc_spec(hh)],
        out_specs=[row, row, row, pl.BlockSpec((hh, q), lambda i: (0, i))],
        out_shape=[shape, shape, shape, jax.ShapeDtypeStruct((hh, t_all), F32)],
        compiler_params=_cp(("parallel",)),
    )(proj, bias, a)


def _three_way(x):
    def top(v):
        word = lax.bitcast_convert_type(v, jnp.uint32) & jnp.uint32(0xFFFF0000)
        return lax.bitcast_convert_type(word, F32)

    hi = top(x)
    rest = x - hi
    mid = top(rest)
    return hi.astype(jnp.bfloat16), mid.astype(jnp.bfloat16), (rest - mid).astype(jnp.bfloat16)


def _scan_columns(dt, cs, tc, groups2, hpg):
    t_all = dt.shape[0]
    parts = [part.reshape(t_all, groups2, 1, hpg) for arr in (dt, cs, tc) for part in _three_way(arr)]
    cols = jnp.concatenate(parts, axis=2).transpose(1, 0, 2, 3).reshape(groups2, t_all, 9 * hpg)
    return jnp.pad(cols, ((0, 0), (0, 0), (0, LANES - 9 * hpg)))


def _dt_bwd(proj, bias, dt, ddt, dda, dt_blk):
    t_all = proj.shape[0]
    hh = bias.shape[1]
    q = _tile(t_all, 1024, LANES)

    def body(r_ref, b_ref, dt_ref, ddt_ref, dda_ref, dr_ref, db_ref, da_ref):
        dr = ddt_ref[...] * jax.nn.sigmoid(r_ref[...] + b_ref[...])
        dr_ref[...] = dr.astype(dr_ref.dtype)
        i = pl.program_id(0)
        _acc(db_ref, jnp.sum(dr, axis=0, keepdims=True), i)
        _acc(da_ref, jnp.sum(dda_ref[...] * dt_ref[...], axis=0, keepdims=True), i)

    row = pl.BlockSpec((q, hh), lambda i: (i, 0))
    vec = jax.ShapeDtypeStruct((1, hh), F32)
    return pl.pallas_call(
        body, name="dt_bwd", grid=(t_all // q,),
        in_specs=[pl.BlockSpec((q, hh), lambda i: (i, dt_blk)), _bc_spec(hh), row, row, row],
        out_specs=[row, _bc_spec(hh), _bc_spec(hh)],
        out_shape=[jax.ShapeDtypeStruct((t_all, hh), MXU_DTYPE), vec, vec], compiler_params=_cp(("arbitrary",)),
    )(proj, bias, dt, ddt, dda)


_NT = (((1,), (1,)), ((), ()))
_TN = (((0,), (0,)), ((), ()))


def _dot(a, b, dims=None, exact=False):
    kw = dict(preferred_element_type=F32)
    if exact:
        kw["precision"] = HI
    if dims is None:
        return jnp.dot(a, b, **kw)
    return lax.dot_general(a, b, dims, **kw)


def _iota(shape, dim):
    return lax.broadcasted_iota(jnp.int32, shape, dim)


class _Ssd:
    def __init__(self, l, lc, di, p, reverse):
        self.q, self.n, self.g = CHUNK, SSM_STATE, SSM_GROUPS
        self.nl, self.ncx = l // CHUNK, lc // CHUNK
        self.ns = self.nl + self.ncx
        self.t_all, self.di, self.p, self.reverse = l + lc, di, p, reverse
        self.hpg = di // p // SSM_GROUPS
        self.gw = self.hpg * p
        self.ntile = self.gw // LANES
        self.hpt = LANES // p
        self.log2p = p.bit_length() - 1
        assert 1 << self.log2p == p and self.gw % LANES == 0 and self.n == LANES and self.q == LANES
        assert 9 * self.hpg <= LANES
        self.d = 1 if reverse else 0

    def chunk_at(self, step):
        if self.reverse:
            return self.ns - 1 - step
        return jnp.where(step < self.ncx, self.nl + step, step - self.ncx)

    def selectors(self):
        hpg = self.hpg
        k = jnp.arange(LANES)
        quantity, head, used = k // (3 * hpg), k % hpg, k < 9 * hpg
        lane_head = jnp.arange(LANES) // self.p
        tiles = jnp.concatenate([(used & (quantity == qo))[:, None] & (head[:, None] == tt * self.hpt + lane_head[None, :])
                                 for tt in range(self.ntile) for qo in range(3)], axis=1)
        heads = jnp.concatenate([jnp.broadcast_to((used & (quantity == 1) & (head == j))[:, None], (LANES, LANES))
                                 for j in range(hpg)], axis=1)
        return tiles.astype(jnp.bfloat16), heads.astype(jnp.bfloat16)

    def in_specs(self, chunk_of):
        g, n, hpg, q = self.g, self.n, self.hpg, self.q
        b_blk, c_blk = self.di // n, self.di // n + g
        d = self.d
        return [
            pl.BlockSpec((q, self.gw), lambda gi, i: (chunk_of(i), gi)),
            pl.BlockSpec((q, n), lambda gi, i: (chunk_of(i), b_blk + gi)),
            pl.BlockSpec((q, n), lambda gi, i: (chunk_of(i), c_blk + gi)),
            pl.BlockSpec((1, q, LANES), lambda gi, i: (d * g + gi, chunk_of(i), 0)),
            pl.BlockSpec((hpg, q), lambda gi, i: (d * g + gi, chunk_of(i))),
            pl.BlockSpec((LANES, self.ntile * 3 * LANES), lambda gi, i: (0, 0)),
            pl.BlockSpec((LANES, hpg * LANES), lambda gi, i: (0, 0)),
        ]

    def masks(self):
        li, si = _iota((self.q, self.q), 0), _iota((self.q, self.q), 1)
        if self.reverse:
            return si >= li, li >= si
        return si <= li, li <= si

    def spread(self, spread_all, tt):
        at = 3 * LANES * tt
        return tuple(spread_all[:, at + k * LANES:at + (k + 1) * LANES] for k in range(3))

    def head_lanes(self, qq):
        return lax.shift_right_logical(_iota((self.q, LANES), 1), self.log2p) == qq

    def head_sums(self, values, tt):
        sel = _iota((HEAD_COLS, LANES), 0) == tt * self.hpt + lax.shift_right_logical(_iota((HEAD_COLS, LANES), 1), self.log2p)
        parts = [part for v in values for part in _three_way(v)]
        sums = _dot(jnp.concatenate(parts, axis=0), sel.astype(jnp.bfloat16), _NT)
        out, at = [], 0
        for v in values:
            rows = v.shape[0]
            out.append(sums[at:at + rows] + sums[at + rows:at + 2 * rows] + sums[at + 2 * rows:at + 3 * rows])
            at += 3 * rows
        return out

    def state_scale(self, csr_ref):
        last = 0 if self.reverse else self.q - 1
        total = jnp.sum(jnp.where(_iota((self.hpg, self.q), 1) == last, csr_ref[...], 0.0), axis=1, keepdims=True)
        decay = jnp.broadcast_to(jnp.exp(total), (self.hpg, self.n))
        decay = jnp.concatenate([decay, jnp.zeros((HEAD_COLS - self.hpg, self.n), F32)], axis=0)
        rows = lax.shift_right_logical(_iota((self.gw, HEAD_COLS), 0), self.log2p) == _iota((self.gw, HEAD_COLS), 1)
        return _dot_parts(rows.astype(jnp.bfloat16), decay)


def _dot_parts(sel, v, dims=None):
    return sum(_dot(sel, part, dims) for part in _three_way(v))


def _ssd_fwd(xbc, cols, cs_t, l, lc, di, p, reverse):
    s = _Ssd(l, lc, di, p, reverse)
    q, n, gw = s.q, s.n, s.gw
    neg_inf = float("-inf")

    def body(xs_ref, b_ref, c_ref, cols_ref, csr_ref, et_ref, eh_ref, y_ref, hp_ref, h_scr):
        i = pl.program_id(1)

        @pl.when(i == 0)
        def _():
            h_scr[...] = jnp.zeros_like(h_scr)

        h = h_scr[...]
        hp_ref[0, 0] = h
        mask, _ = s.masks()
        cols = cols_ref[0]
        bb, cb = b_ref[...].astype(MXU_DTYPE), c_ref[...].astype(MXU_DTYPE)
        cbt = _dot(cb, bb, _NT)
        y_off = _dot(cb, h.astype(MXU_DTYPE), _NT)
        spread_all, cs_heads = _dot(cols, et_ref[...]), _dot(cols, eh_ref[...])
        w_tiles = []
        for tt in range(s.ntile):
            sl = slice(tt * LANES, (tt + 1) * LANES)
            dt_b, cs_b, tc_b = s.spread(spread_all, tt)
            x = xs_ref[:, sl] * dt_b
            ms, xhs = [], []
            for qq in range(s.hpt):
                j = tt * s.hpt + qq
                seg = cs_heads[:, j * LANES:(j + 1) * LANES] - csr_ref[j:j + 1, :]
                ms.append((cbt * jnp.exp(jnp.where(mask, seg, neg_inf))).astype(MXU_DTYPE))
                xhs.append(jnp.where(s.head_lanes(qq), x, 0.0).astype(MXU_DTYPE))
            yd = _dot(jnp.concatenate(ms, axis=1), jnp.concatenate(xhs, axis=0))
            y_ref[:, sl] = yd + y_off[:, sl] * jnp.exp(cs_b)
            w_tiles.append((x * jnp.exp(tc_b)).astype(MXU_DTYPE))
        wm = w_tiles[0] if s.ntile == 1 else jnp.concatenate(w_tiles, axis=1)
        h_scr[...] = h * s.state_scale(csr_ref) + _dot(wm, bb, _TN)

    d = "rev" if reverse else "fwd"
    e_tiles, e_heads = s.selectors()
    return pl.pallas_call(
        body, name=f"ssd_{d}", grid=(s.g, s.ns), in_specs=s.in_specs(s.chunk_at),
        out_specs=[pl.BlockSpec((q, gw), lambda gi, i: (s.chunk_at(i), gi)),
                   pl.BlockSpec((1, 1, gw, n), lambda gi, i: (i, gi, 0, 0))],
        out_shape=[jax.ShapeDtypeStruct((s.t_all, di), F32), jax.ShapeDtypeStruct((s.ns, s.g, gw, n), F32)],
        scratch_shapes=[pltpu.VMEM((gw, n), F32)],
        compiler_params=_cp(("parallel", "arbitrary")),
    )(xbc, xbc, xbc, cols, cs_t, e_tiles, e_heads)


def _ssd_bwd(xbc, cols, cs_t, a_cols, dy, hprev, l, lc, di, p, reverse, dsk=None, prev=None):
    s = _Ssd(l, lc, di, p, reverse)
    q, n, gw, hpg = s.q, s.n, s.gw, s.hpg
    neg_inf = float("-inf")
    n_extra = (dsk is not None) + (3 if prev is not None else 0)

    def chunk_of(i):
        return s.chunk_at(s.ns - 1 - i)

    def body(xs_ref, b_ref, c_ref, cols_ref, csr_ref, et_ref, eh_ref, ac_ref, dy_ref, hp_ref, *rest):
        extra, (dxs_ref, db_ref, dc_ref, ddt_ref, dda_ref, dh_scr) = rest[:n_extra], rest[n_extra:]
        dsk_ref = extra[0] if dsk is not None else None
        prev_refs = extra[-3:] if prev is not None else None
        i = pl.program_id(1)

        @pl.when(i == 0)
        def _():
            dh_scr[...] = jnp.zeros_like(dh_scr)

        latent = (chunk_of(i) < s.nl).astype(F32)
        h, dh = hp_ref[0, 0], dh_scr[...]
        hb, dhb = h.astype(MXU_DTYPE), dh.astype(MXU_DTYPE)
        mask, mask_t = s.masks()
        cols = cols_ref[0]
        bb, cb = b_ref[...].astype(MXU_DTYPE), c_ref[...].astype(MXU_DTYPE)
        cbt, bct = _dot(cb, bb, _NT), _dot(bb, cb, _NT)
        b_dh = _dot(bb, dhb, _NT)
        y_off0 = _dot(cb, hb, _NT)
        d_g, d_gt = jnp.zeros((q, q), F32), jnp.zeros((q, q), F32)
        dcs = jnp.zeros((q, HEAD_COLS), F32)
        ddt_x = jnp.zeros((q, HEAD_COLS), F32)
        r_state = jnp.zeros((16, HEAD_COLS), F32)
        spread_all, cs_heads = _dot(cols, et_ref[...]), _dot(cols, eh_ref[...])
        dye_tiles, xte_tiles = [], []
        for tt in range(s.ntile):
            sl = slice(tt * LANES, (tt + 1) * LANES)
            dt_b, cs_b, tc_b = s.spread(spread_all, tt)
            ecs_b, te_b = jnp.exp(cs_b), jnp.exp(tc_b)
            xs_t = xs_ref[:, sl]
            x = xs_t * dt_b
            d_y = dy_ref[:, sl] * latent
            dx_state = b_dh[:, sl] * te_b
            lms, lm_ts, m_ts, d_yhs, xhs = [], [], [], [], []
            for qq in range(s.hpt):
                j = tt * s.hpt + qq
                csc_b = cs_heads[:, j * LANES:(j + 1) * LANES]
                csr = csr_ref[j:j + 1, :]
                lms.append(jnp.exp(jnp.where(mask, csc_b - csr, neg_inf)))
                lm_ts.append(jnp.exp(jnp.where(mask_t, csr - csc_b, neg_inf)))
                m_ts.append(bct * lm_ts[-1])
                lanes = s.head_lanes(qq)
                d_yhs.append(jnp.where(lanes, d_y, 0.0).astype(MXU_DTYPE))
                xhs.append(jnp.where(lanes, x, 0.0).astype(MXU_DTYPE))
            d_yh_rows = jnp.concatenate(d_yhs, axis=0)
            d_m_all = _dot(d_yh_rows, x.astype(MXU_DTYPE), _NT)
            d_mt_all = _dot(jnp.concatenate(xhs, axis=0), d_y.astype(MXU_DTYPE), _NT)
            for qq in range(s.hpt):
                j = tt * s.hpt + qq
                d_m, d_mt = d_m_all[qq * q:(qq + 1) * q], d_mt_all[qq * q:(qq + 1) * q]
                r1 = jnp.sum(d_m * (cbt * lms[qq]), axis=1, keepdims=True)
                r2 = jnp.sum(d_mt * m_ts[qq], axis=1, keepdims=True)
                dcs = dcs + (r1 - r2) * (_iota((1, HEAD_COLS), 1) == j).astype(F32)
                d_g = d_g + d_m * lms[qq]
                d_gt = d_gt + d_mt * lm_ts[qq]
            d_x = _dot(jnp.concatenate([m.astype(MXU_DTYPE) for m in m_ts], axis=1), d_yh_rows) + dx_state
            d_xs = d_x * dt_b
            if dsk_ref is not None:
                d_xs = d_xs + d_y * dsk_ref[:, sl]
            if prev_refs is not None:
                d_xs = d_xs + prev_refs[0][:, sl]
            dxs_ref[:, sl] = d_xs
            fed = x * dx_state
            fed_rows = jnp.broadcast_to(jnp.sum(fed, axis=0, keepdims=True), (16, LANES))
            sums = s.head_sums([d_x * xs_t, d_y * y_off0[:, sl] * ecs_b - fed, fed_rows], tt)
            ddt_x, dcs, r_state = ddt_x + sums[0], dcs + sums[1], r_state + sums[2]
            dye_tiles.append((d_y * ecs_b).astype(MXU_DTYPE))
            xte_tiles.append((x * te_b).astype(MXU_DTYPE))
        dye = dye_tiles[0] if s.ntile == 1 else jnp.concatenate(dye_tiles, axis=1)
        xte = xte_tiles[0] if s.ntile == 1 else jnp.concatenate(xte_tiles, axis=1)
        d_c = _dot(d_g.astype(MXU_DTYPE), bb) + _dot(dye, hb)
        d_b = _dot(d_gt.astype(MXU_DTYPE), cb) + _dot(xte, dhb)
        if prev_refs is not None:
            d_b, d_c = d_b + prev_refs[1][...], d_c + prev_refs[2][...]
        dc_ref[...] = d_c
        db_ref[...] = d_b
        scale = s.state_scale(csr_ref)
        carried = dh * h * scale
        d_tot = jnp.sum(r_state, axis=0, keepdims=True) * 0.0625
        for j in range(hpg):
            part = jnp.sum(carried[j * p:(j + 1) * p, :], axis=0, keepdims=True)
            d_tot = d_tot + jnp.sum(part, axis=1, keepdims=True) * (_iota((1, HEAD_COLS), 1) == j).astype(F32)
        dda = _dot_parts(mask_t.astype(jnp.bfloat16), dcs) + d_tot
        ddt_ref[0] = ddt_x + dda * ac_ref[0]
        dda_ref[0] = dda
        dh_scr[...] = dh * scale + _dot(dye, cb, _TN)

    d = "rev" if reverse else "fwd"
    e_tiles, e_heads = s.selectors()
    col = pl.BlockSpec((1, q, HEAD_COLS), lambda gi, i: (gi, chunk_of(i), 0))
    gn = pl.BlockSpec((q, n), lambda gi, i: (chunk_of(i), gi))
    wide = pl.BlockSpec((q, gw), lambda gi, i: (chunk_of(i), gi))
    extra_specs, extra_args, aliases = [], [], {}
    if dsk is not None:
        extra_specs.append(pl.BlockSpec((1, gw), lambda gi, i: (0, gi)))
        extra_args.append(dsk)
    if prev is not None:
        first = 10 + len(extra_args)
        extra_specs += [wide, gn, gn]
        extra_args += list(prev)
        aliases = {first: 0, first + 1: 1, first + 2: 2}
    return pl.pallas_call(
        body, name=f"ssd_bwd_{d}", grid=(s.g, s.ns),
        in_specs=s.in_specs(chunk_of) + [
            pl.BlockSpec((1, 1, HEAD_COLS), lambda gi, i: (s.d * s.g + gi, 0, 0)),
            pl.BlockSpec((q, gw), lambda gi, i: (jnp.minimum(chunk_of(i), s.nl - 1), gi)),
            pl.BlockSpec((1, 1, gw, n), lambda gi, i: (s.ns - 1 - i, gi, 0, 0))] + extra_specs,
        out_specs=[wide, gn, gn, col, col],
        out_shape=[jax.ShapeDtypeStruct((s.t_all, di), F32), jax.ShapeDtypeStruct((s.t_all, s.g * n), F32),
                   jax.ShapeDtypeStruct((s.t_all, s.g * n), F32), jax.ShapeDtypeStruct((s.g, s.t_all, HEAD_COLS), F32),
                   jax.ShapeDtypeStruct((s.g, s.t_all, HEAD_COLS), F32)],
        scratch_shapes=[pltpu.VMEM((gw, n), F32)],
        input_output_aliases=aliases, compiler_params=_cp(("parallel", "arbitrary")),
    )(xbc, xbc, xbc, cols, cs_t, e_tiles, e_heads, a_cols, dy, hprev, *extra_args)


def _ada_fwd(crows, w, b):
    r, d = crows.shape
    ws = w.shape[1]
    tn = _tile(ws, 512, LANES)

    def body(c_ref, w_ref, b_ref, m_ref, s_ref):
        s = _silu(c_ref[...])
        s_ref[...] = s
        m_ref[...] = _dot(s.astype(MXU_DTYPE), w_ref[...].astype(MXU_DTYPE)) + b_ref[...]

    full = pl.BlockSpec((r, d), lambda j: (0, 0))
    return pl.pallas_call(
        body, name="ada_fwd", grid=(ws // tn,),
        in_specs=[full, pl.BlockSpec((d, tn), lambda j: (0, j)), pl.BlockSpec((1, tn), lambda j: (0, j))],
        out_specs=[pl.BlockSpec((r, tn), lambda j: (0, j)), full],
        out_shape=[jax.ShapeDtypeStruct((r, ws), F32), jax.ShapeDtypeStruct((r, d), F32)],
        compiler_params=_cp(("arbitrary",)),
    )(crows, w, b)


def _ada_bwd(s_t, w, dm):
    d, r = s_t.shape
    ws = w.shape[1]
    tn = _tile(ws, 512, LANES)

    def body(st_ref, w_ref, dm_ref, dw_ref, ds_ref):
        dmb = dm_ref[...].astype(MXU_DTYPE)
        dw_ref[...] = _dot(st_ref[...].astype(MXU_DTYPE), dmb)
        _acc(ds_ref, _dot(dmb, w_ref[...].astype(MXU_DTYPE), _NT), pl.program_id(0))

    return pl.pallas_call(
        body, name="ada_bwd", grid=(ws // tn,),
        in_specs=[pl.BlockSpec((d, r), lambda j: (0, 0)), pl.BlockSpec((d, tn), lambda j: (0, j)),
                  pl.BlockSpec((r, tn), lambda j: (0, j))],
        out_specs=[pl.BlockSpec((d, tn), lambda j: (0, j)), pl.BlockSpec((r, d), lambda j: (0, 0))],
        out_shape=[jax.ShapeDtypeStruct((d, ws), F32), jax.ShapeDtypeStruct((r, d), F32)],
        compiler_params=_cp(("arbitrary",)),
    )(s_t, w, dm)


def _adamw(w, g, m, v, name):
    r, c = w.shape
    t = _tile(r, max(8, 300_000 // c), 8)

    def body(w_ref, g_ref, m_ref, v_ref, d_ref, m2_ref, v2_ref):
        g = g_ref[...]
        m2 = ADAM_B1 * m_ref[...] + (1.0 - ADAM_B1) * g
        v2 = ADAM_B2 * v_ref[...] + (1.0 - ADAM_B2) * (g * g)
        m_hat = m2 / (1.0 - ADAM_B1 ** ADAM_STEP)
        v_hat = v2 / (1.0 - ADAM_B2 ** ADAM_STEP)
        d_ref[...] = -ADAM_LR * (m_hat / (jnp.sqrt(v_hat) + ADAM_EPS) + ADAM_WD * w_ref[...])
        m2_ref[...] = m2
        v2_ref[...] = v2

    blk = pl.BlockSpec((t, c), lambda i: (i, 0))
    shape = jax.ShapeDtypeStruct((r, c), F32)
    return pl.pallas_call(
        body, name=name, grid=(r // t,), in_specs=[blk] * 4, out_specs=[blk] * 3, out_shape=[shape] * 3,
        compiler_params=_cp(("parallel",)),
    )(w, g, m, v)


def _sum_devices(gathered):
    rows, w = gathered.shape
    per = rows // N_DEV

    def body(g_ref, o_ref):
        total = g_ref[pl.ds(0, per), :]
        for dev in range(1, N_DEV):
            total = total + g_ref[pl.ds(dev * per, per), :]
        o_ref[...] = total

    return pl.pallas_call(
        body, name="sum_devices", out_shape=jax.ShapeDtypeStruct((per, w), F32),
        in_specs=[pl.BlockSpec(memory_space=pltpu.VMEM)], out_specs=pl.BlockSpec(memory_space=pltpu.VMEM),
        compiler_params=_cp(),
    )(gathered)


def _c_ctx_grad(parts, c_ctx):
    rows, d = parts.shape
    per = rows // N_DEV

    def body(p_ref, c_ref, o_ref):
        total = p_ref[pl.ds(0, 1), :]
        for chip in range(1, N_SHARD):
            total = total + p_ref[pl.ds(2 * chip * per, 1), :]
        _, vjp = jax.vjp(_silu, c_ref[...])
        o_ref[...] = vjp(total)[0]

    return pl.pallas_call(
        body, name="c_ctx_grad", out_shape=jax.ShapeDtypeStruct((1, d), F32),
        in_specs=[pl.BlockSpec(memory_space=pltpu.VMEM)] * 2, out_specs=pl.BlockSpec(memory_space=pltpu.VMEM),
        compiler_params=_cp(),
    )(parts, c_ctx)


def _pad_rows(a, rows, width):
    return jnp.pad(a, ((0, rows - a.shape[0]), (0, width - a.shape[1])))


def _pack(vectors, quantum):
    flat = jnp.concatenate([v.reshape(-1) for v in vectors])
    return jnp.pad(flat, (0, -flat.shape[0] % quantum))


def kernel(x, c, ctx, c_ctx, w_mod, b_mod, norm_mix, w_in, ssm_conv_w, ssm_conv_b, dt_bias, a_log, d_skip, ssm_norm, cf_conv_w, cf_conv_b, cf_ln_g, cf_ln_b, w_proj_a, w_proj_b, w_out, norm_ffn, w_ffn_gate, w_ffn_up, w_ffn_down, norm_final, loss_target, m_c_ctx, m_w_mod, m_b_mod, m_norm_mix, m_w_in, m_ssm_conv_w, m_ssm_conv_b, m_dt_bias, m_a_log, m_d_skip, m_ssm_norm, m_cf_conv_w, m_cf_conv_b, m_cf_ln_g, m_cf_ln_b, m_w_proj_a, m_w_proj_b, m_w_out, m_norm_ffn, m_w_ffn_gate, m_w_ffn_up, m_w_ffn_down, m_norm_final, v_c_ctx, v_w_mod, v_b_mod, v_norm_mix, v_w_in, v_ssm_conv_w, v_ssm_conv_b, v_dt_bias, v_a_log, v_d_skip, v_ssm_norm, v_cf_conv_w, v_cf_conv_b, v_cf_ln_g, v_cf_ln_b, v_w_proj_a, v_w_proj_b, v_w_out, v_norm_ffn, v_w_ffn_gate, v_w_ffn_up, v_w_ffn_down, v_norm_final):
    l, d = x.shape[1], x.shape[2]
    lc = ctx.shape[1]
    t_all = l + lc
    di = ssm_norm.shape[-1]
    h = d_skip.shape[-1]
    p = di // h
    g, n = SSM_GROUPS, SSM_STATE
    hpg = h // g
    conv_dim = di + 2 * g * n
    df = w_ffn_down.shape[1] * N_SHARD
    assert 2 * h == LANES and d % (2 * LANES) == 0

    my_x, my_y, my_c = _mesh_pos()
    chip = 2 * my_x + my_y
    dev = 2 * chip + my_c

    x2, ctx2, tgt = x[0], ctx[0], loss_target[0]
    row = lambda a: a.reshape(1, -1)

    cw_shard, cfw_shard = ssm_conv_w[0], cf_conv_w[0]
    k5, k31 = cw_shard.shape[0], cfw_shard.shape[0]
    r5, r31 = -(-k5 // 8) * 8, -(-k31 // 8) * 8
    wp = max(d, cw_shard.shape[1], cfw_shard.shape[1])
    packed = jnp.concatenate([_pad_rows(c, 8, wp), _pad_rows(cw_shard, r5, wp), _pad_rows(cfw_shard, r31, wp)], axis=0)
    got = _allgather_small(packed, "ag_params").reshape(N_DEV, 8 + r5 + r31, wp)
    c_all = got[:, 0, :d]
    conv_w = got[0::2, 8:8 + k5, :cw_shard.shape[1]].transpose(1, 0, 2).reshape(k5, conv_dim)
    cf_w = got[0::2, 8 + r5:8 + r5 + k31, :cfw_shard.shape[1]].transpose(1, 0, 2).reshape(k31, d)

    ws = w_mod.shape[2]
    crows = jnp.concatenate([c_all, row(c_ctx), jnp.zeros((7, d), F32)], axis=0)
    b_mod_mine = lax.dynamic_slice(b_mod, (0, chip * ws), (1, ws))
    m_part, s_rows = _ada_fwd(crows, w_mod[0], b_mod_mine)
    m_full = _allgather_small(m_part, "ag_mod").reshape(N_DEV, 16, ws)[0::2].transpose(1, 0, 2).reshape(16, N_SHARD * ws)
    m_lat = lax.dynamic_slice(m_full, (dev, 0), (1, 6 * d))
    sh1, sc1, g1, sh2, sc2, g2 = [m_lat[:, i * d:(i + 1) * d] for i in range(6)]
    csh1, csc1 = m_full[8:9, 0:d], m_full[8:9, d:2 * d]

    gate_up = jnp.concatenate([w_ffn_gate[0].T, w_ffn_up[0].T], axis=0)
    shards = [w_in[0].T, gate_up, w_proj_a[0], w_proj_b[0], w_out[0], w_ffn_down[0]]
    shards = [s.astype(WIRE_DTYPE) for s in shards]
    (win_t,) = _fill_own_rows(_run_side(_gather_side(shards[:1]), "ag_w_in"), shards[:1])
    o_xbc, o_dt, o_glu, o_gates = di, di + conv_dim, di + conv_dim + 2 * h, di + conv_dim + 2 * h + 2 * d
    win_work = jnp.concatenate([win_t[:o_xbc], win_t[o_glu:], win_t[o_xbc:o_dt], win_t[o_dt:o_glu]], axis=0)
    c_u, c_ga, c_xbc, c_dt = di, di + 2 * d, di + 4 * d, di + 4 * d + conv_dim

    nm = norm_mix
    hx = _mod_fwd(x2, ctx2, nm, sc1, sh1, csc1, csh1)
    proj, *rest = _matmul(hx, win_work.T, tm=384, tn=29 * LANES, n_outer=True, name="mm_proj", side=_gather_side(shards[1:]))
    wgu, wpa, wpb, wout, wdn = _fill_own_rows(rest, shards[1:])
    xbc = _conv5_fwd(proj, conv_w, ssm_conv_b, l, lc, c_xbc, conv_dim)
    a = -jnp.exp(a_log.reshape(1, 2 * h))
    dt, cs, tc, cs_t = _dt_fwd(proj, dt_bias.reshape(1, 2 * h), a, c_dt // LANES)
    cols = _scan_columns(dt, cs, tc, 2 * g, hpg)
    a_cols = jnp.pad(a.reshape(2 * g, 1, hpg), ((0, 0), (0, 0), (0, HEAD_COLS - hpg)))
    y_f, hp_f = _ssd_fwd(xbc, cols, cs_t, l, lc, di, p, False)
    y_r, hp_r = _ssd_fwd(xbc, cols, cs_t, l, lc, di, p, True)
    dsk = jnp.repeat(d_skip.reshape(h), p).reshape(1, di)
    ya_in = _gate_fwd(y_f, y_r, xbc, proj, dsk, ssm_norm, l, di)
    y_a = _matmul(ya_in, wpa, tk=di, name="mm_ya")
    u_blk = c_u // d
    cv = _conv31_fwd(proj, cf_w, cf_conv_b, l, d, u_blk)
    cf = _ln_fwd(cv, cf_ln_g, cf_ln_b)
    y_b = _matmul(cf, wpb, name="mm_yb")
    ga_blk = c_ga // d
    merged = _merge_fwd(y_a, y_b, proj, ga_blk)
    mix = _matmul(merged, wout, name="mm_mix")
    x1, hx2 = _res_fwd(x2, mix, g1, norm_ffn, sc2, sh2)
    gu = _matmul(hx2, wgu, tb=True, tm=1024, tn=1024, name="mm_gu")
    act = _swiglu_fwd(gu, df)
    dn = _matmul(act, wdn, tk=df, name="mm_dn")
    loss, dx1, ddn, dg2, d_norm_final = _loss_and_grads(x1, dn, g2, row(norm_final), tgt)

    dact = _matmul(ddn, wdn, tb=True, tm=1024, tn=_tile(df, 1024, LANES), name="mm_dact")
    dw_dn = _matmul(act, ddn, ta=True, tn=d, tk=1024, name="mm_dw_dn")
    dgu = _swiglu_bwd(gu, dact, df)
    dhx2 = _matmul(dgu, wgu, tm=1024, tn=1024, tk=_tile(2 * df, 1536, LANES), name="mm_dhx2")
    dw_gu = _matmul(dgu, hx2, ta=True, tm=1024, tn=d, tk=1024, name="mm_dw_gu")
    dx_res, dmix, dg1, d_norm_ffn, dsc2, dsh2 = _res_bwd(x2, mix, g1, norm_ffn, sc2, sh2, dx1, dhx2)
    dmerged = _matmul(dmix, wout, tb=True, name="mm_dmerged")
    dw_out = _matmul(merged, dmix, ta=True, tn=d, tk=1024, name="mm_dw_out")
    dya, dyb, dga, dgb = _merge_bwd(y_a, y_b, proj, ga_blk, dmerged, lc)
    dcf = _matmul(dyb, wpb, tb=True, name="mm_dcf")
    dw_pb = _matmul(cf, dyb, ta=True, tn=d, tk=1024, name="mm_dw_pb")
    dcv, d_ln_g, d_ln_b = _ln_bwd(cv, cf_ln_g, cf_ln_b, dcf)
    du, dv, d_cf_w, d_cf_b = _conv31_bwd(proj, cf_w, dcv, l, lc, d, u_blk)
    dya_in = _matmul(dya, wpa, tb=True, tn=_tile(di, 1024, LANES), name="mm_dya_in")
    dw_pa = _matmul(ya_in, dya, ta=True, tn=d, tk=1024, name="mm_dw_pa")
    dy, dz, ddsk, d_ssm_norm = _gate_bwd(y_f, y_r, xbc, proj, dsk, ssm_norm, dya_in, l, lc, di)
    dxs_f, db_f, dc_f, ddt_f, dda_f = _ssd_bwd(xbc, cols, cs_t, a_cols, dy, hp_f, l, lc, di, p, False, dsk=dsk)
    dxs, db, dc, ddt_r, dda_r = _ssd_bwd(xbc, cols, cs_t, a_cols, dy, hp_r, l, lc, di, p, True, prev=(dxs_f, db_f, dc_f))
    dxs_raw, dcw_x, dcb_x = _conv5_bwd(proj, conv_w, ssm_conv_b, [dxs], l, lc, c_xbc, 0, di)
    db_raw, dcw_b, dcb_b = _conv5_bwd(proj, conv_w, ssm_conv_b, [db], l, lc, c_xbc, di, g * n)
    dc_raw, dcw_c, dcb_c = _conv5_bwd(proj, conv_w, ssm_conv_b, [dc], l, lc, c_xbc, di + g * n, g * n)
    d_conv_w = jnp.concatenate([dcw_x, dcw_b, dcw_c], axis=1)
    d_conv_b = jnp.concatenate([dcb_x, dcb_b, dcb_c], axis=1)
    heads = lambda f, r: jnp.concatenate([t[:, :, :hpg].transpose(1, 0, 2).reshape(t_all, h) for t in (f, r)], axis=1)
    ddt_raw, d_dt_bias, dda_dt = _dt_bwd(proj, dt_bias.reshape(1, 2 * h), dt, heads(ddt_f, ddt_r), heads(dda_f, dda_r), c_dt // LANES)
    d_a_log = dda_dt * a
    dproj = jnp.concatenate([dz, du, dv, dga, dgb, dxs_raw, db_raw, dc_raw, ddt_raw], axis=1)
    wire, own = _reduce_scatter_begin([dw_gu, dw_pa, dw_pb, dw_out, dw_dn], "a")
    dhx, *recv = _matmul(dproj, win_work, tm=768, tn=1024, tk=_tile(win_work.shape[0], 4096, LANES), name="mm_dhx",
                         side=_scatter_side(wire))
    g_gu_t, g_pa, g_pb, g_out, g_dn = _reduce_scatter_end(own, recv, "a")
    g_gate_t, g_up_t = g_gu_t[:df // N_SHARD], g_gu_t[df // N_SHARD:]
    dw_in_work = _matmul(dproj, hx, ta=True, tm=640, tn=d, tk=1408, name="mm_dw_in")
    grad_x, d_norm_mix, dsc1, dsh1, dcsc1, dcsh1 = _mod_bwd(x2, ctx2, nm, sc1, sh1, csc1, csh1, dhx, dx_res)

    dw_in_t = jnp.concatenate([dw_in_work[:c_u], dw_in_work[c_xbc:], dw_in_work[c_u:c_xbc]], axis=0)
    wire, own = _reduce_scatter_begin([dw_in_t], "b")
    (g_in_t,) = _reduce_scatter_end(own, _run_side(_scatter_side(wire), "rs_scatter_b"), "b")
    g_in, g_gate, g_up = g_in_t.T, g_gate_t.T, g_up_t.T

    zeros_d = jnp.zeros((1, d), F32)
    dm_lat = jnp.concatenate([dsh1, dsc1, dg1, dsh2, dsc2, dg2], axis=1)
    dm_ctx = jnp.concatenate([dcsh1, dcsc1] + [zeros_d] * 4, axis=1)
    d_d_skip = ddsk.reshape(h, p).sum(axis=1)
    replicated = [dm_lat + dm_ctx, d_norm_mix, d_conv_b, d_dt_bias, d_a_log, d_d_skip, d_ssm_norm, d_cf_b, d_ln_g, d_ln_b,
                  d_norm_ffn, d_norm_final]
    rep_w = [b_mod, norm_mix, ssm_conv_b, dt_bias, a_log, d_skip, ssm_norm, cf_conv_b, cf_ln_g, cf_ln_b, norm_ffn, norm_final]
    rep_m = [m_b_mod, m_norm_mix, m_ssm_conv_b, m_dt_bias, m_a_log, m_d_skip, m_ssm_norm, m_cf_conv_b, m_cf_ln_g, m_cf_ln_b,
             m_norm_ffn, m_norm_final]
    rep_v = [v_b_mod, v_norm_mix, v_ssm_conv_b, v_dt_bias, v_a_log, v_d_skip, v_ssm_norm, v_cf_conv_b, v_cf_ln_g, v_cf_ln_b,
             v_norm_ffn, v_norm_final]
    quantum = 8 * LANES
    rep_flat = _pack(replicated, quantum)
    n_rep = rep_flat.shape[0]
    summed_parts = [rep_flat, _pack([d_conv_w, d_cf_w, dm_ctx], quantum)]
    n_sum = n_rep + summed_parts[1].shape[0]
    everything = jnp.concatenate(summed_parts + [_pack([dm_lat], quantum)])
    gathered = _allgather_small(everything.reshape(8, -1), "ag_small_grads")
    w8 = gathered.shape[1]
    summed = _sum_devices(gathered).reshape(-1)
    dm_lat_all = gathered.reshape(N_DEV, 8 * w8)[:, n_sum:n_sum + 6 * d]
    off = n_rep
    g_conv_w_full = summed[off:off + k5 * conv_dim].reshape(k5, conv_dim)
    off += k5 * conv_dim
    g_cf_w_full = summed[off:off + k31 * d].reshape(k31, d)
    off += k31 * d
    dm_ctx_all = summed[off:off + 6 * d].reshape(1, 6 * d)
    g_conv_w = lax.dynamic_slice(g_conv_w_full, (0, chip * cw_shard.shape[1]), cw_shard.shape)
    g_cf_w = lax.dynamic_slice(g_cf_w_full, (0, chip * cfw_shard.shape[1]), cfw_shard.shape)

    dm_rows = jnp.concatenate([dm_lat_all, dm_ctx_all, jnp.zeros((7, 6 * d), F32)], axis=0)
    dm_mine = lax.dynamic_slice(dm_rows, (0, chip * ws), (16, ws))
    g_w_mod, ds_part = _ada_bwd(s_rows.T, w_mod[0], dm_mine)
    ds_all = _allgather_small(ds_part[8:16], "ag_c_ctx")
    g_c_ctx = _c_ctx_grad(ds_all, row(c_ctx))

    grads, deltas, new_ms, new_vs = {}, {}, {}, {}

    def update(name, w2, g2, m2, v2, shape):
        dl, mm, vv = _adamw(w2, g2, m2, v2, f"adamw_{name}")
        grads[name], deltas[name], new_ms[name], new_vs[name] = (t.reshape(shape) for t in (g2, dl, mm, vv))

    for name, w_, g_, m_, v_ in [
            ("w_mod", w_mod, g_w_mod, m_w_mod, v_w_mod), ("w_in", w_in, g_in, m_w_in, v_w_in),
            ("ssm_conv_w", ssm_conv_w, g_conv_w, m_ssm_conv_w, v_ssm_conv_w),
            ("cf_conv_w", cf_conv_w, g_cf_w, m_cf_conv_w, v_cf_conv_w),
            ("w_proj_a", w_proj_a, g_pa, m_w_proj_a, v_w_proj_a), ("w_proj_b", w_proj_b, g_pb, m_w_proj_b, v_w_proj_b),
            ("w_out", w_out, g_out, m_w_out, v_w_out), ("w_ffn_gate", w_ffn_gate, g_gate, m_w_ffn_gate, v_w_ffn_gate),
            ("w_ffn_up", w_ffn_up, g_up, m_w_ffn_up, v_w_ffn_up), ("w_ffn_down", w_ffn_down, g_dn, m_w_ffn_down, v_w_ffn_down)]:
        update(name, w_[0], g_, m_[0], v_[0], w_.shape)
    update("c_ctx", row(c_ctx), g_c_ctx, row(m_c_ctx), row(v_c_ctx), c_ctx.shape)

    rep_names = ["b_mod", "norm_mix", "ssm_conv_b", "dt_bias", "a_log", "d_skip", "ssm_norm", "cf_conv_b", "cf_ln_g", "cf_ln_b",
                 "norm_ffn", "norm_final"]
    as8 = lambda vs: _pack(vs, quantum).reshape(8, -1)
    g8 = summed[:n_rep].reshape(8, -1)
    d8, m8, v8 = _adamw(as8(rep_w), g8, as8(rep_m), as8(rep_v), "adamw_replicated")
    off = 0
    for name, w_ in zip(rep_names, rep_w):
        size = w_.size
        for store, packed8 in ((grads, g8), (deltas, d8), (new_ms, m8), (new_vs, v8)):
            store[name] = packed8.reshape(-1)[off:off + size].reshape(w_.shape)
        off += size

    order = ["c_ctx", "w_mod", "b_mod", "norm_mix", "w_in", "ssm_conv_w", "ssm_conv_b", "dt_bias", "a_log", "d_skip", "ssm_norm",
             "cf_conv_w", "cf_conv_b", "cf_ln_g", "cf_ln_b", "w_proj_a", "w_proj_b", "w_out", "norm_ffn", "w_ffn_gate", "w_ffn_up",
             "w_ffn_down", "norm_final"]
    total_loss = lax.psum(loss[0, 0], ("x", "y", "c"))
    return (total_loss, grad_x.reshape(x.shape), *[grads[k] for k in order], *[deltas[k] for k in order],
            *[new_ms[k] for k in order], *[new_vs[k] for k in order])
```

```python
import functools

import jax
import jax.numpy as jnp
from jax import lax
from jax.experimental import pallas as pl
from jax.experimental.pallas import tpu as pltpu

F32 = jnp.float32
MXU_DTYPE = jnp.bfloat16
WIRE_DTYPE = jnp.bfloat16
HI = lax.Precision.HIGHEST
EPS = 1e-6
SSM_GROUPS = 8
SSM_STATE = 128
CHUNK = 128
GRID_W = 64
LANES = 128
HEAD_COLS = 16
VMEM_LIMIT = 52 * 1024 * 1024
ADAM_LR, ADAM_B1, ADAM_B2, ADAM_EPS, ADAM_WD, ADAM_STEP = 0.001, 0.9, 0.999, 1e-08, 0.01, 10
MESH = pl.DeviceIdType.MESH
N_SHARD = 4
N_DEV = 8


def _cp(sem=None):
    kw = dict(vmem_limit_bytes=VMEM_LIMIT)
    if sem is not None:
        kw["dimension_semantics"] = sem
    return pltpu.CompilerParams(**kw)


def _tile(n, target, q):
    best = None
    for t in range(q, min(n, target) + 1, q):
        if n % t == 0:
            best = t
    return best if best is not None else n


def _acc(ref, val, i):
    @pl.when(i == 0)
    def _():
        ref[...] = val

    @pl.when(i > 0)
    def _():
        ref[...] += val


def _bc_spec(w):
    return pl.BlockSpec((1, w), lambda *_: (0, 0))


def _rms(x, w):
    return x * lax.rsqrt(jnp.mean(x * x, axis=-1, keepdims=True) + EPS) * w


def _silu(x):
    return x * jax.nn.sigmoid(x)


def _f_mod(x, w, sc, sh):
    return _rms(x, w) * (1.0 + sc) + sh


def _f_gate(yf, yr, xs, z, dsk, wn):
    return _rms((yf + yr + dsk * xs) * _silu(z), wn)


def _f_ln(cv, g, b):
    mu = jnp.mean(cv, axis=-1, keepdims=True)
    xc = cv - mu
    var = jnp.mean(xc * xc, axis=-1, keepdims=True)
    return _silu(xc * lax.rsqrt(var + EPS) * g + b)


def _f_merge(ya, yb, ga, gb):
    return jax.nn.sigmoid(ga) * ya + jax.nn.sigmoid(gb) * yb


def _f_res(x, mix, g1, wn, sc2, sh2):
    x1 = x + g1 * mix
    return x1, _rms(x1, wn) * (1.0 + sc2) + sh2


def _f_swiglu(gt, up):
    return _silu(gt) * up


def _f_loss(x1, dn, g2, wn, tgt):
    out = _rms(x1 + g2 * dn, wn)
    err = out - tgt
    per_tok = jnp.mean(err * err, axis=-1, keepdims=True)
    return 0.5 * jnp.sum(per_tok, axis=0, keepdims=True)


def _matmul(a, b, *, ta=False, tb=False, out_dtype=F32, tm=1024, tn=512, tk=2048, name, side=None, n_outer=False):
    m, k = (a.shape[1], a.shape[0]) if ta else a.shape
    n = b.shape[0] if tb else b.shape[1]
    assert (b.shape[1] if tb else b.shape[0]) == k, (a.shape, b.shape, ta, tb)
    tm, tn, tk = _tile(m, tm, LANES if ta else 16), _tile(n, tn, LANES), _tile(k, tk, LANES)
    grid = (n // tn, m // tm, k // tk) if n_outer else (m // tm, n // tn, k // tk)
    ij = (lambda g0, g1: (g1, g0)) if n_outer else (lambda g0, g1: (g0, g1))
    nk = grid[2]
    dims = (((0 if ta else 1,), (1 if tb else 0,)), ((), ()))
    n_in = len(side.inputs) if side else 0
    n_out = len(side.out_shapes) if side else 0

    def body(a_ref, b_ref, *rest):
        side_in, o_ref, side_out, scratch = rest[:n_in], rest[n_in], rest[n_in + 1:n_in + 1 + n_out], rest[n_in + 1 + n_out:]
        steps = [pl.program_id(axis) for axis in range(3)]
        if side:
            @pl.when((steps[0] == 0) & (steps[1] == 0) & (steps[2] == 0))
            def _():
                side.start(side_in, side_out, *scratch[-2:])

        prod = lax.dot_general(a_ref[...].astype(MXU_DTYPE), b_ref[...].astype(MXU_DTYPE), dims,
                               preferred_element_type=F32)
        if nk == 1:
            o_ref[...] = prod.astype(o_ref.dtype)
        else:
            acc = scratch[0]
            _acc(acc, prod, steps[2])

            @pl.when(steps[2] == nk - 1)
            def _():
                o_ref[...] = acc[...].astype(o_ref.dtype)

        if side:
            @pl.when((steps[0] == grid[0] - 1) & (steps[1] == grid[1] - 1) & (steps[2] == nk - 1))
            def _():
                side.finish(side_in, side_out, *scratch[-2:])

    a_spec = (pl.BlockSpec((tk, tm), lambda g0, g1, kk: (kk, ij(g0, g1)[0])) if ta
              else pl.BlockSpec((tm, tk), lambda g0, g1, kk: (ij(g0, g1)[0], kk)))
    b_spec = (pl.BlockSpec((tn, tk), lambda g0, g1, kk: (ij(g0, g1)[1], kk)) if tb
              else pl.BlockSpec((tk, tn), lambda g0, g1, kk: (kk, ij(g0, g1)[1])))
    out = pl.pallas_call(
        body, name=name, grid=grid, in_specs=[a_spec, b_spec] + [_HBM] * n_in,
        out_specs=[pl.BlockSpec((tm, tn), lambda g0, g1, kk: ij(g0, g1))] + [_HBM] * n_out,
        out_shape=[jax.ShapeDtypeStruct((m, n), out_dtype)] + (side.out_shapes if side else []),
        scratch_shapes=([] if nk == 1 else [pltpu.VMEM((tm, tn), F32)]) + (side.scratch() if side else []),
        compiler_params=_cp(("arbitrary",) * 3 if side else ("parallel", "parallel", "arbitrary")),
    )(a, b, *(side.inputs if side else []))
    return out if side else out[0]


def _mesh_pos():
    return lax.axis_index("x"), lax.axis_index("y"), lax.axis_index("c")


def _other_chips(x, y):
    return [(1 - x, y), (x, 1 - y), (1 - x, 1 - y)]


def _allgather_small(v, name):
    m_per, n = v.shape

    def body(x_ref, out_ref, send_sems, recv_sems, local_sem):
        x, y, c = _mesh_pos()
        me, sibling = (x, y, c), (x, y, 1 - c)
        chips = _other_chips(x, y)

        def rows(px, py, pc):
            return out_ref.at[pl.ds((4 * px + 2 * py + pc) * m_per, m_per), :]

        def copy(k, block, to, src=None):
            return pltpu.make_async_remote_copy(
                src_ref=rows(*block) if src is None else src, dst_ref=rows(*block),
                send_sem=send_sems.at[k], recv_sem=recv_sems.at[k], device_id=to, device_id_type=MESH)

        mine = pltpu.make_async_copy(x_ref, rows(*me), local_sem)
        mine.start()
        first = [copy(0, me, sibling, src=x_ref)]
        first += [copy(1 + j, me, (*chip, c), src=x_ref) for j, chip in enumerate(chips)]
        for cp in first:
            cp.start()
        passed = [copy(4 + j, (*chip, c), sibling) for j, chip in enumerate(chips)]
        for j, chip in enumerate(chips):
            copy(1 + j, (*chip, c), me).wait_recv()
            passed[j].start()
        copy(0, sibling, me).wait_recv()
        for j, chip in enumerate(chips):
            copy(4 + j, (*chip, 1 - c), me).wait_recv()
        for cp in first + passed:
            cp.wait_send()
        mine.wait()

    return pl.pallas_call(
        body, name=name, out_shape=jax.ShapeDtypeStruct((N_DEV * m_per, n), v.dtype),
        in_specs=[pl.BlockSpec(memory_space=pltpu.VMEM)], out_specs=pl.BlockSpec(memory_space=pltpu.VMEM),
        scratch_shapes=[pltpu.SemaphoreType.DMA((7,)), pltpu.SemaphoreType.DMA((7,)), pltpu.SemaphoreType.DMA],
        compiler_params=_cp(),
    )(v)


_HBM = pl.BlockSpec(memory_space=pltpu.HBM)


class _Side:
    def __init__(self, inputs, out_shapes, n_sems, start, finish):
        self.inputs, self.out_shapes, self.n_sems, self.start, self.finish = inputs, out_shapes, n_sems, start, finish

    def scratch(self):
        return [pltpu.SemaphoreType.DMA((self.n_sems,)), pltpu.SemaphoreType.DMA((self.n_sems,))]


def _run_side(side, name):
    n_in = len(side.inputs)

    def body(*refs):
        src, dst, sems = refs[:n_in], refs[n_in:-2], refs[-2:]
        side.start(src, dst, *sems)
        side.finish(src, dst, *sems)

    return pl.pallas_call(
        body, name=name, out_shape=side.out_shapes, in_specs=[_HBM] * n_in, out_specs=[_HBM] * len(side.out_shapes),
        scratch_shapes=side.scratch(), compiler_params=_cp(),
    )(*side.inputs)


def _gather_side(shards):
    n = len(shards)

    def plan(src, dst, send_sems, recv_sems):
        x, y, c = _mesh_pos()
        chips = _other_chips(x, y)

        def half(i, px, py, pc):
            r = src[i].shape[0]
            return dst[i].at[pl.ds(pl.multiple_of((2 * px + py) * r + pc * (r // 2), 16), r // 2), :]

        def copy(i, k, block, to, own=False):
            r = src[i].shape[0]
            mine = src[i].at[pl.ds(pl.multiple_of(c * (r // 2), 16), r // 2), :]
            return pltpu.make_async_remote_copy(
                src_ref=mine if own else half(i, *block), dst_ref=half(i, *block), send_sem=send_sems.at[6 * i + k],
                recv_sem=recv_sems.at[6 * i + k], device_id=to, device_id_type=MESH)

        first = [copy(i, j, (x, y, c), (*chip, c), own=True) for i in range(n) for j, chip in enumerate(chips)]
        return (x, y, c), chips, copy, first

    def start(src, dst, send_sems, recv_sems):
        for cp in plan(src, dst, send_sems, recv_sems)[3]:
            cp.start()

    def finish(src, dst, send_sems, recv_sems):
        (x, y, c), chips, copy, first = plan(src, dst, send_sems, recv_sems)
        passed = []
        for i in range(n):
            for j, chip in enumerate(chips):
                copy(i, j, (*chip, c), (x, y, c)).wait_recv()
                passed.append(copy(i, 3 + j, (*chip, c), (x, y, 1 - c)))
                passed[-1].start()
        for i in range(n):
            for j, chip in enumerate(chips):
                copy(i, 3 + j, (*chip, 1 - c), (x, y, c)).wait_recv()
        for cp in first + passed:
            cp.wait_send()

    shapes = [jax.ShapeDtypeStruct((N_SHARD * s.shape[0], s.shape[1]), s.dtype) for s in shards]
    return _Side(list(shards), shapes, 6 * n, start, finish)


def _fill_own_rows(gathered, shards):
    chip = 2 * lax.axis_index("x") + lax.axis_index("y")
    return [lax.dynamic_update_slice(full, s, (chip * s.shape[0], 0)) for full, s in zip(gathered, shards)]


def _swap_halves(parts, name):
    n = len(parts)

    def body(*refs):
        src, dst = refs[:n], refs[n:2 * n]
        send_sems, recv_sems = refs[2 * n:]
        x, y, c = _mesh_pos()
        copies = [pltpu.make_async_remote_copy(
            src_ref=src[i].at[s, 1 - c], dst_ref=dst[i].at[s], send_sem=send_sems.at[N_SHARD * i + s],
            recv_sem=recv_sems.at[N_SHARD * i + s], device_id=(x, y, 1 - c), device_id_type=MESH)
            for i in range(n) for s in range(N_SHARD)]
        for cp in copies:
            cp.start()
        for cp in copies:
            cp.wait()

    return pl.pallas_call(
        body, name=name,
        out_shape=[jax.ShapeDtypeStruct((N_SHARD,) + p.shape[2:], p.dtype) for p in parts],
        in_specs=[_HBM] * n, out_specs=[_HBM] * n,
        scratch_shapes=[pltpu.SemaphoreType.DMA((N_SHARD * n,)), pltpu.SemaphoreType.DMA((N_SHARD * n,))],
        compiler_params=_cp(),
    )(*parts)


def _scatter_side(parts):
    n = len(parts)

    def copies(src, dst, send_sems, recv_sems):
        x, y, c = _mesh_pos()
        return [pltpu.make_async_remote_copy(
            src_ref=src[i].at[2 * chip[0] + chip[1]], dst_ref=dst[i].at[j], send_sem=send_sems.at[3 * i + j],
            recv_sem=recv_sems.at[3 * i + j], device_id=(*chip, c), device_id_type=MESH)
            for i in range(n) for j, chip in enumerate(_other_chips(x, y))]

    def start(*refs):
        for cp in copies(*refs):
            cp.start()

    def finish(*refs):
        for cp in copies(*refs):
            cp.wait()

    shapes = [jax.ShapeDtypeStruct((3,) + p.shape[1:], p.dtype) for p in parts]
    return _Side(list(parts), shapes, 3 * n, start, finish)


def _join_halves(halves, name):
    n = len(halves)

    def body(*refs):
        src, dst = refs[:n], refs[n:2 * n]
        send_sems, recv_sems = refs[2 * n:]
        x, y, c = _mesh_pos()
        remote = [pltpu.make_async_remote_copy(
            src_ref=src[i], dst_ref=dst[i].at[c], send_sem=send_sems.at[i], recv_sem=recv_sems.at[i],
            device_id=(x, y, 1 - c), device_id_type=MESH) for i in range(n)]
        for cp in remote:
            cp.start()
        for i in range(n):
            pltpu.make_async_remote_copy(
                src_ref=src[i], dst_ref=dst[i].at[1 - c], send_sem=send_sems.at[i], recv_sem=recv_sems.at[i],
                device_id=(x, y, 1 - c), device_id_type=MESH).wait_recv()
        for cp in remote:
            cp.wait_send()

    joined = pl.pallas_call(
        body, name=name,
        out_shape=[jax.ShapeDtypeStruct((2,) + h.shape, h.dtype) for h in halves],
        in_specs=[_HBM] * n, out_specs=[_HBM] * n,
        scratch_shapes=[pltpu.SemaphoreType.DMA((n,)), pltpu.SemaphoreType.DMA((n,))],
        compiler_params=_cp(),
    )(*halves)
    c = lax.axis_index("c")
    return [lax.dynamic_update_slice(j, h[None], (c, 0, 0)) for j, h in zip(joined, halves)]


def _pair_sum(g, got, name):
    _, _, hr, d = g.shape
    t = _tile(hr, 256, 16)

    def body(g0_ref, g1_ref, got_ref, wire_ref, own_ref):
        x, y, c = _mesh_pos()
        total = jnp.where(c == 0, g0_ref[0, 0], g1_ref[0, 0]) + got_ref[0]
        wire_ref[0] = total.astype(wire_ref.dtype)

        @pl.when(pl.program_id(1) == 2 * x + y)
        def _():
            own_ref[...] = total

    return pl.pallas_call(
        body, name=name, grid=(hr // t, N_SHARD),
        in_specs=[pl.BlockSpec((1, 1, t, d), lambda i, s: (s, 0, i, 0)), pl.BlockSpec((1, 1, t, d), lambda i, s: (s, 1, i, 0)),
                  pl.BlockSpec((1, t, d), lambda i, s: (s, i, 0))],
        out_specs=[pl.BlockSpec((1, t, d), lambda i, s: (s, i, 0)), pl.BlockSpec((t, d), lambda i, s: (i, 0))],
        out_shape=[jax.ShapeDtypeStruct((N_SHARD, hr, d), WIRE_DTYPE), jax.ShapeDtypeStruct((hr, d), F32)],
        compiler_params=_cp(("parallel", "arbitrary")),
    )(g, g, got)


def _sum_partials(own, recv, name):
    hr, d = own.shape
    t = _tile(hr, 256, 16)

    def body(own_ref, recv_ref, o_ref):
        total = own_ref[...]
        for j in range(3):
            total = total + recv_ref[j].astype(F32)
        o_ref[...] = total

    blk = pl.BlockSpec((t, d), lambda i: (i, 0))
    return pl.pallas_call(
        body, name=name, grid=(hr // t,), in_specs=[blk, pl.BlockSpec((3, t, d), lambda i: (0, i, 0))], out_specs=blk,
        out_shape=jax.ShapeDtypeStruct((hr, d), F32), compiler_params=_cp(("parallel",)),
    )(own, recv)


def _reduce_scatter_begin(grads, tag):
    split = [g.reshape(N_SHARD, 2, g.shape[0] // (2 * N_SHARD), g.shape[1]) for g in grads]
    got = _swap_halves(split, f"rs_swap_halves_{tag}")
    sums = [_pair_sum(g, h, f"rs_pair_sum_{tag}{i}") for i, (g, h) in enumerate(zip(split, got))]
    return [w for w, _ in sums], [own for _, own in sums]


def _reduce_scatter_end(own, recv, tag):
    halves = [_sum_partials(o, rv, f"rs_sum_{tag}{i}") for i, (o, rv) in enumerate(zip(own, recv))]
    return [j.reshape(-1, j.shape[-1]) for j in _join_halves(halves, f"rs_join_halves_{tag}")]


def _mod_fwd(x, ctx, nw, sc, sh, csc, csh):
    l, d = x.shape
    lc = ctx.shape[0]
    t = min(256, lc)
    nl, nc = l // t, lc // t

    def body(x_ref, c_ref, nw_ref, sc_ref, sh_ref, csc_ref, csh_ref, o_ref):
        i = pl.program_id(0)

        @pl.when(i < nl)
        def _():
            o_ref[...] = _f_mod(x_ref[...], nw_ref[...], sc_ref[...], sh_ref[...]).astype(o_ref.dtype)

        @pl.when(i >= nl)
        def _():
            o_ref[...] = _f_mod(c_ref[...], nw_ref[...], csc_ref[...], csh_ref[...]).astype(o_ref.dtype)

    return pl.pallas_call(
        body, name="mod_fwd", grid=(nl + nc,),
        in_specs=[pl.BlockSpec((t, d), lambda i: (jnp.minimum(i, nl - 1), 0)),
                  pl.BlockSpec((t, d), lambda i: (jnp.maximum(i - nl, 0), 0))] + [_bc_spec(d)] * 5,
        out_specs=pl.BlockSpec((t, d), lambda i: (i, 0)),
        out_shape=jax.ShapeDtypeStruct((l + lc, d), MXU_DTYPE), compiler_params=_cp(("arbitrary",)),
    )(x, ctx, nw, sc, sh, csc, csh)


def _mod_bwd(x, ctx, nw, sc, sh, csc, csh, dhx, dx_res):
    l, d = x.shape
    lc = ctx.shape[0]
    t = min(256, lc)
    nl, nc = l // t, lc // t

    def body(x_ref, c_ref, nw_ref, sc_ref, sh_ref, csc_ref, csh_ref, dh_ref, dr_ref,
             dx_ref, dnw_ref, dsc_ref, dsh_ref, dcsc_ref, dcsh_ref):
        i = pl.program_id(0)

        @pl.when(i == 0)
        def _():
            for r in (dnw_ref, dsc_ref, dsh_ref, dcsc_ref, dcsh_ref):
                r[...] = jnp.zeros_like(r)

        @pl.when(i < nl)
        def _():
            _, vjp = jax.vjp(_f_mod, x_ref[...], nw_ref[...], sc_ref[...], sh_ref[...])
            dx, dnw, dsc, dsh = vjp(dh_ref[...])
            dx_ref[...] = dx + dr_ref[...]
            dnw_ref[...] += dnw
            dsc_ref[...] += dsc
            dsh_ref[...] += dsh

        @pl.when(i >= nl)
        def _():
            _, vjp = jax.vjp(_f_mod, c_ref[...], nw_ref[...], csc_ref[...], csh_ref[...])
            _, dnw, dsc, dsh = vjp(dh_ref[...])
            dnw_ref[...] += dnw
            dcsc_ref[...] += dsc
            dcsh_ref[...] += dsh

    lat = pl.BlockSpec((t, d), lambda i: (jnp.minimum(i, nl - 1), 0))
    vec = jax.ShapeDtypeStruct((1, d), F32)
    return pl.pallas_call(
        body, name="mod_bwd", grid=(nl + nc,),
        in_specs=[lat, pl.BlockSpec((t, d), lambda i: (jnp.maximum(i - nl, 0), 0))] + [_bc_spec(d)] * 5
        + [pl.BlockSpec((t, d), lambda i: (i, 0)), lat],
        out_specs=[lat] + [_bc_spec(d)] * 5,
        out_shape=[jax.ShapeDtypeStruct((l, d), F32)] + [vec] * 5, compiler_params=_cp(("arbitrary",)),
    )(x, ctx, nw, sc, sh, csc, csh, dhx, dx_res)


def _gate_fwd(yf, yr, xbc, proj, dsk, wn, l, di):
    t = 128

    def body(yf_ref, yr_ref, xs_ref, z_ref, dsk_ref, wn_ref, o_ref):
        o_ref[...] = _f_gate(yf_ref[...], yr_ref[...], xs_ref[...], z_ref[...], dsk_ref[...], wn_ref[...]).astype(o_ref.dtype)

    row = pl.BlockSpec((t, di), lambda i: (i, 0))
    return pl.pallas_call(
        body, name="gate_fwd", grid=(l // t,), in_specs=[row] * 4 + [_bc_spec(di)] * 2, out_specs=row,
        out_shape=jax.ShapeDtypeStruct((l, di), MXU_DTYPE), compiler_params=_cp(("parallel",)),
    )(yf, yr, xbc, proj, dsk, wn)


def _gate_bwd(yf, yr, xbc, proj, dsk, wn, dya, l, lc, di):
    t = 128
    nl, nc = l // t, lc // t

    def body(yf_ref, yr_ref, xs_ref, z_ref, dsk_ref, wn_ref, g_ref, dy_ref, dz_ref, ddsk_ref, dwn_ref):
        i = pl.program_id(0)

        @pl.when(i == 0)
        def _():
            ddsk_ref[...] = jnp.zeros_like(ddsk_ref)
            dwn_ref[...] = jnp.zeros_like(dwn_ref)

        @pl.when(i < nl)
        def _():
            _, vjp = jax.vjp(_f_gate, yf_ref[...], yr_ref[...], xs_ref[...], z_ref[...], dsk_ref[...], wn_ref[...])
            dyf, _, _, dz, ddsk, dwn = vjp(g_ref[...])
            dy_ref[...] = dyf
            dz_ref[...] = dz.astype(dz_ref.dtype)
            ddsk_ref[...] += ddsk
            dwn_ref[...] += dwn

        @pl.when(i >= nl)
        def _():
            dz_ref[...] = jnp.zeros_like(dz_ref)

    lat = pl.BlockSpec((t, di), lambda i: (jnp.minimum(i, nl - 1), 0))
    vec = jax.ShapeDtypeStruct((1, di), F32)
    return pl.pallas_call(
        body, name="gate_bwd", grid=(nl + nc,),
        in_specs=[lat] * 4 + [_bc_spec(di)] * 2 + [lat],
        out_specs=[lat, pl.BlockSpec((t, di), lambda i: (i, 0))] + [_bc_spec(di)] * 2,
        out_shape=[jax.ShapeDtypeStruct((l, di), F32), jax.ShapeDtypeStruct((l + lc, di), MXU_DTYPE)] + [vec] * 2,
        compiler_params=_cp(("arbitrary",)),
    )(yf, yr, xbc, proj, dsk, wn, dya)


def _ln_fwd(cv, g, b):
    l, d = cv.shape
    t = 256

    def body(cv_ref, g_ref, b_ref, o_ref):
        o_ref[...] = _f_ln(cv_ref[...], g_ref[...], b_ref[...]).astype(o_ref.dtype)

    row = pl.BlockSpec((t, d), lambda i: (i, 0))
    return pl.pallas_call(
        body, name="ln_fwd", grid=(l // t,), in_specs=[row] + [_bc_spec(d)] * 2, out_specs=row,
        out_shape=jax.ShapeDtypeStruct((l, d), MXU_DTYPE), compiler_params=_cp(("parallel",)),
    )(cv, g, b)


def _ln_bwd(cv, g, b, dcf):
    l, d = cv.shape
    t = 256

    def body(cv_ref, g_ref, b_ref, dcf_ref, dcv_ref, dg_ref, db_ref):
        _, vjp = jax.vjp(_f_ln, cv_ref[...], g_ref[...], b_ref[...])
        dcv, dg, db = vjp(dcf_ref[...])
        dcv_ref[...] = dcv
        i = pl.program_id(0)
        _acc(dg_ref, dg, i)
        _acc(db_ref, db, i)

    row = pl.BlockSpec((t, d), lambda i: (i, 0))
    vec = jax.ShapeDtypeStruct((1, d), F32)
    return pl.pallas_call(
        body, name="ln_bwd", grid=(l // t,), in_specs=[row] + [_bc_spec(d)] * 2 + [row],
        out_specs=[row] + [_bc_spec(d)] * 2, out_shape=[jax.ShapeDtypeStruct((l, d), F32), vec, vec],
        compiler_params=_cp(("arbitrary",)),
    )(cv, g, b, dcf)


def _merge_fwd(ya, yb, proj, ga_blk):
    l, d = ya.shape
    t = 256

    def body(ya_ref, yb_ref, ga_ref, gb_ref, o_ref):
        o_ref[...] = _f_merge(ya_ref[...], yb_ref[...], ga_ref[...], gb_ref[...]).astype(o_ref.dtype)

    row = pl.BlockSpec((t, d), lambda i: (i, 0))
    return pl.pallas_call(
        body, name="merge_fwd", grid=(l // t,),
        in_specs=[row, row, pl.BlockSpec((t, d), lambda i: (i, ga_blk)), pl.BlockSpec((t, d), lambda i: (i, ga_blk + 1))],
        out_specs=row, out_shape=jax.ShapeDtypeStruct((l, d), MXU_DTYPE), compiler_params=_cp(("parallel",)),
    )(ya, yb, proj, proj)


def _merge_bwd(ya, yb, proj, ga_blk, dmerged, lc):
    l, d = ya.shape
    t = min(256, lc)
    nl, nc = l // t, lc // t

    def body(ya_ref, yb_ref, ga_ref, gb_ref, g_ref, dya_ref, dyb_ref, dga_ref, dgb_ref):
        i = pl.program_id(0)

        @pl.when(i < nl)
        def _():
            _, vjp = jax.vjp(_f_merge, ya_ref[...], yb_ref[...], ga_ref[...], gb_ref[...])
            dya, dyb, dga, dgb = vjp(g_ref[...])
            dya_ref[...] = dya.astype(dya_ref.dtype)
            dyb_ref[...] = dyb.astype(dyb_ref.dtype)
            dga_ref[...] = dga.astype(dga_ref.dtype)
            dgb_ref[...] = dgb.astype(dgb_ref.dtype)

        @pl.when(i >= nl)
        def _():
            dga_ref[...] = jnp.zeros_like(dga_ref)
            dgb_ref[...] = jnp.zeros_like(dgb_ref)

    lat = pl.BlockSpec((t, d), lambda i: (jnp.minimum(i, nl - 1), 0))
    full = pl.BlockSpec((t, d), lambda i: (i, 0))
    return pl.pallas_call(
        body, name="merge_bwd", grid=(nl + nc,),
        in_specs=[lat, lat, pl.BlockSpec((t, d), lambda i: (jnp.minimum(i, nl - 1), ga_blk)),
                  pl.BlockSpec((t, d), lambda i: (jnp.minimum(i, nl - 1), ga_blk + 1)), lat],
        out_specs=[lat, lat, full, full],
        out_shape=[jax.ShapeDtypeStruct((l, d), MXU_DTYPE)] * 2 + [jax.ShapeDtypeStruct((l + lc, d), MXU_DTYPE)] * 2,
        compiler_params=_cp(("arbitrary",)),
    )(ya, yb, proj, proj, dmerged)


def _res_fwd(x, mix, g1, wn, sc2, sh2):
    l, d = x.shape
    t = 256

    def body(x_ref, m_ref, g1_ref, wn_ref, sc_ref, sh_ref, x1_ref, hx_ref):
        x1, hx = _f_res(x_ref[...], m_ref[...], g1_ref[...], wn_ref[...], sc_ref[...], sh_ref[...])
        x1_ref[...] = x1
        hx_ref[...] = hx.astype(hx_ref.dtype)

    row = pl.BlockSpec((t, d), lambda i: (i, 0))
    return pl.pallas_call(
        body, name="res_fwd", grid=(l // t,), in_specs=[row, row] + [_bc_spec(d)] * 4, out_specs=[row, row],
        out_shape=[jax.ShapeDtypeStruct((l, d), F32), jax.ShapeDtypeStruct((l, d), MXU_DTYPE)],
        compiler_params=_cp(("parallel",)),
    )(x, mix, g1, wn, sc2, sh2)


def _res_bwd(x, mix, g1, wn, sc2, sh2, dx1, dhx2):
    l, d = x.shape
    t = 256

    def body(x_ref, m_ref, g1_ref, wn_ref, sc_ref, sh_ref, dx1_ref, dh_ref, dx_ref, dm_ref, dg1_ref, dwn_ref, dsc_ref, dsh_ref):
        _, vjp = jax.vjp(_f_res, x_ref[...], m_ref[...], g1_ref[...], wn_ref[...], sc_ref[...], sh_ref[...])
        dx, dm, dg1, dwn, dsc, dsh = vjp((dx1_ref[...], dh_ref[...]))
        dx_ref[...] = dx
        dm_ref[...] = dm.astype(dm_ref.dtype)
        i = pl.program_id(0)
        _acc(dg1_ref, dg1, i)
        _acc(dwn_ref, dwn, i)
        _acc(dsc_ref, dsc, i)
        _acc(dsh_ref, dsh, i)

    row = pl.BlockSpec((t, d), lambda i: (i, 0))
    vec = jax.ShapeDtypeStruct((1, d), F32)
    return pl.pallas_call(
        body, name="res_bwd", grid=(l // t,), in_specs=[row, row] + [_bc_spec(d)] * 4 + [row, row],
        out_specs=[row, row] + [_bc_spec(d)] * 4,
        out_shape=[jax.ShapeDtypeStruct((l, d), F32), jax.ShapeDtypeStruct((l, d), MXU_DTYPE)] + [vec] * 4,
        compiler_params=_cp(("arbitrary",)),
    )(x, mix, g1, wn, sc2, sh2, dx1, dhx2)


def _swiglu_fwd(gu, df):
    l = gu.shape[0]
    r = df // N_SHARD
    t = 256

    def body(gu_ref, o_ref):
        o_ref[...] = _f_swiglu(gu_ref[:, :r], gu_ref[:, r:]).astype(o_ref.dtype)

    return pl.pallas_call(
        body, name="swiglu_fwd", grid=(l // t, N_SHARD),
        in_specs=[pl.BlockSpec((t, 2 * r), lambda i, s: (i, s))], out_specs=pl.BlockSpec((t, r), lambda i, s: (i, s)),
        out_shape=jax.ShapeDtypeStruct((l, df), MXU_DTYPE), compiler_params=_cp(("parallel", "parallel")),
    )(gu)


def _swiglu_bwd(gu, dact, df):
    l = gu.shape[0]
    r = df // N_SHARD
    t = 256

    def body(gu_ref, da_ref, dgu_ref):
        _, vjp = jax.vjp(_f_swiglu, gu_ref[:, :r], gu_ref[:, r:])
        dg, du = vjp(da_ref[...])
        dgu_ref[:, :r] = dg.astype(dgu_ref.dtype)
        dgu_ref[:, r:] = du.astype(dgu_ref.dtype)

    pair = pl.BlockSpec((t, 2 * r), lambda i, s: (i, s))
    return pl.pallas_call(
        body, name="swiglu_bwd", grid=(l // t, N_SHARD), in_specs=[pair, pl.BlockSpec((t, r), lambda i, s: (i, s))],
        out_specs=pair, out_shape=jax.ShapeDtypeStruct((l, 2 * df), MXU_DTYPE),
        compiler_params=_cp(("parallel", "parallel")),
    )(gu, dact)


def _loss_and_grads(x1, dn, g2, wn, tgt):
    l, d = x1.shape
    t = 256

    def body(x1_ref, dn_ref, g2_ref, wn_ref, t_ref, loss_ref, dx_ref, ddn_ref, dg2_ref, dwn_ref):
        loss, vjp = jax.vjp(lambda a, b, c, e: _f_loss(a, b, c, e, t_ref[...]), x1_ref[...], dn_ref[...], g2_ref[...], wn_ref[...])
        dx, ddn, dg2, dwn = vjp(jnp.ones((1, 1), F32))
        dx_ref[...] = dx
        ddn_ref[...] = ddn.astype(ddn_ref.dtype)
        i = pl.program_id(0)
        _acc(loss_ref, loss, i)
        _acc(dg2_ref, dg2, i)
        _acc(dwn_ref, dwn, i)

    row = pl.BlockSpec((t, d), lambda i: (i, 0))
    vec = jax.ShapeDtypeStruct((1, d), F32)
    return pl.pallas_call(
        body, name="loss_and_grads", grid=(l // t,), in_specs=[row, row] + [_bc_spec(d)] * 2 + [row],
        out_specs=[pl.BlockSpec((1, 1), lambda i: (0, 0)), row, row] + [_bc_spec(d)] * 2,
        out_shape=[jax.ShapeDtypeStruct((1, 1), F32), jax.ShapeDtypeStruct((l, d), F32),
                   jax.ShapeDtypeStruct((l, d), MXU_DTYPE), vec, vec],
        compiler_params=_cp(("arbitrary",)),
    )(x1, dn, g2, wn, tgt)


PAD = 8


def _conv5_taps(s_ref, w_ref, l, lc, width):
    half = width // 2
    lat = sum(w_ref[k:k + 1, :] * s_ref[pl.ds(PAD + k - half, l), :] for k in range(width))
    ctx = sum(w_ref[k:k + 1, :] * s_ref[pl.ds(2 * PAD + l + k - half, lc), :] for k in range(width))
    return lat, ctx


def _fill_padded(s_ref, lat, ctx, l, lc):
    zeros = jnp.zeros((PAD, s_ref.shape[1]), F32)
    s_ref[pl.ds(0, PAD), :] = zeros
    s_ref[pl.ds(PAD, l), :] = lat
    s_ref[pl.ds(PAD + l, PAD), :] = zeros
    s_ref[pl.ds(2 * PAD + l, lc), :] = ctx
    s_ref[pl.ds(2 * PAD + l + lc, PAD), :] = zeros


def _conv5_fwd(proj, w, b, l, lc, col0, ncols):
    t_all = l + lc
    cw = LANES
    blk0 = col0 // cw
    width = w.shape[0]

    def body(x_ref, w_ref, b_ref, o_ref, s_ref):
        _fill_padded(s_ref, x_ref[pl.ds(0, l), :], x_ref[pl.ds(l, lc), :], l, lc)
        lat, ctx = _conv5_taps(s_ref, w_ref, l, lc, width)
        o_ref[pl.ds(0, l), :] = _silu(lat + b_ref[...])
        o_ref[pl.ds(l, lc), :] = _silu(ctx + b_ref[...])

    return pl.pallas_call(
        body, name="conv5_fwd", grid=(ncols // cw,),
        in_specs=[pl.BlockSpec((t_all, cw), lambda j: (0, blk0 + j)), pl.BlockSpec((width, cw), lambda j: (0, j)),
                  pl.BlockSpec((1, cw), lambda j: (0, j))],
        out_specs=pl.BlockSpec((t_all, cw), lambda j: (0, j)),
        out_shape=jax.ShapeDtypeStruct((t_all, ncols), F32),
        scratch_shapes=[pltpu.VMEM((t_all + 3 * PAD, cw), F32)], compiler_params=_cp(("parallel",)),
    )(proj, w, b)


def _conv5_bwd(proj, w, b, cots, l, lc, col0, seg0, ncols):
    t_all = l + lc
    cw = LANES
    blk0, sblk0 = col0 // cw, seg0 // cw
    width = w.shape[0]
    half = width // 2
    nc = len(cots)

    def body(*refs):
        x_ref, w_ref, b_ref = refs[:3]
        cot_refs = refs[3:3 + nc]
        dx_ref, dw_ref, db_ref, s_ref = refs[3 + nc:]
        x_lat, x_ctx = x_ref[pl.ds(0, l), :], x_ref[pl.ds(l, lc), :]
        _fill_padded(s_ref, x_lat, x_ctx, l, lc)
        pre_lat, pre_ctx = _conv5_taps(s_ref, w_ref, l, lc, width)
        g = sum(c[...] for c in cot_refs)

        def through_silu(pre, cot):
            _, vjp = jax.vjp(_silu, pre + b_ref[...])
            return vjp(cot)[0]

        d_lat = through_silu(pre_lat, g[:l])
        d_ctx = through_silu(pre_ctx, g[l:])
        db_ref[...] = jnp.sum(d_lat, axis=0, keepdims=True) + jnp.sum(d_ctx, axis=0, keepdims=True)
        for k in range(width):
            dw_ref[k:k + 1, :] = (
                jnp.sum(d_lat * s_ref[pl.ds(PAD + k - half, l), :], axis=0, keepdims=True)
                + jnp.sum(d_ctx * s_ref[pl.ds(2 * PAD + l + k - half, lc), :], axis=0, keepdims=True))
        _fill_padded(s_ref, d_lat, d_ctx, l, lc)
        dx_lat = sum(w_ref[k:k + 1, :] * s_ref[pl.ds(PAD - (k - half), l), :] for k in range(width))
        dx_ctx = sum(w_ref[k:k + 1, :] * s_ref[pl.ds(2 * PAD + l - (k - half), lc), :] for k in range(width))
        dx_ref[pl.ds(0, l), :] = dx_lat.astype(dx_ref.dtype)
        dx_ref[pl.ds(l, lc), :] = dx_ctx.astype(dx_ref.dtype)

    col = pl.BlockSpec((t_all, cw), lambda j: (0, j))
    return pl.pallas_call(
        body, name=f"conv5_bwd_{seg0}", grid=(ncols // cw,),
        in_specs=[pl.BlockSpec((t_all, cw), lambda j: (0, blk0 + sblk0 + j)),
                  pl.BlockSpec((width, cw), lambda j: (0, sblk0 + j)), pl.BlockSpec((1, cw), lambda j: (0, sblk0 + j))]
        + [col] * nc,
        out_specs=[col, pl.BlockSpec((width, cw), lambda j: (0, j)), pl.BlockSpec((1, cw), lambda j: (0, j))],
        out_shape=[jax.ShapeDtypeStruct((t_all, ncols), MXU_DTYPE), jax.ShapeDtypeStruct((width, ncols), F32),
                   jax.ShapeDtypeStruct((1, ncols), F32)],
        scratch_shapes=[pltpu.VMEM((t_all + 3 * PAD, cw), F32)], compiler_params=_cp(("parallel",)),
    )(proj, w, b, *cots)


def _conv31_fwd(proj, w, b, l, d, u_blk):
    cw = LANES
    width = w.shape[0]
    reach = (width // 2) * GRID_W
    nb = d // cw

    def body(u_ref, v_ref, w_ref, b_ref, o_ref, s_ref):
        s_ref[pl.ds(0, reach), :] = jnp.zeros((reach, cw), F32)
        s_ref[pl.ds(reach, l), :] = u_ref[...] * jax.nn.sigmoid(v_ref[...])
        s_ref[pl.ds(reach + l, reach), :] = jnp.zeros((reach, cw), F32)
        o_ref[...] = sum(w_ref[k:k + 1, :] * s_ref[pl.ds(k * GRID_W, l), :] for k in range(width)) + b_ref[...]

    return pl.pallas_call(
        body, name="conv31_fwd", grid=(nb,),
        in_specs=[pl.BlockSpec((l, cw), lambda j: (0, u_blk * nb + j)), pl.BlockSpec((l, cw), lambda j: (0, (u_blk + 1) * nb + j)),
                  pl.BlockSpec((width, cw), lambda j: (0, j)), pl.BlockSpec((1, cw), lambda j: (0, j))],
        out_specs=pl.BlockSpec((l, cw), lambda j: (0, j)), out_shape=jax.ShapeDtypeStruct((l, d), F32),
        scratch_shapes=[pltpu.VMEM((l + 2 * reach, cw), F32)], compiler_params=_cp(("parallel",)),
    )(proj, proj, w, b)


def _conv31_bwd(proj, w, dcv, l, lc, d, u_blk):
    cw = LANES
    width = w.shape[0]
    reach = (width // 2) * GRID_W
    nb = d // cw
    t_all = l + lc

    def body(u_ref, v_ref, w_ref, g_ref, du_ref, dv_ref, dw_ref, db_ref, s_ref):
        zeros = jnp.zeros((reach, cw), F32)
        s_ref[pl.ds(0, reach), :] = zeros
        s_ref[pl.ds(reach + l, reach), :] = zeros
        u, v, g = u_ref[...], v_ref[...], g_ref[...]
        s_ref[pl.ds(reach, l), :] = u * jax.nn.sigmoid(v)
        db_ref[...] = jnp.sum(g, axis=0, keepdims=True)
        for k in range(width):
            dw_ref[k:k + 1, :] = jnp.sum(g * s_ref[pl.ds(k * GRID_W, l), :], axis=0, keepdims=True)
        s_ref[pl.ds(reach, l), :] = g
        dt = sum(w_ref[k:k + 1, :] * s_ref[pl.ds((width - 1 - k) * GRID_W, l), :] for k in range(width))
        _, vjp = jax.vjp(lambda a, c: a * jax.nn.sigmoid(c), u, v)
        du, dv = vjp(dt)
        du_ref[pl.ds(0, l), :] = du.astype(du_ref.dtype)
        dv_ref[pl.ds(0, l), :] = dv.astype(dv_ref.dtype)
        du_ref[pl.ds(l, lc), :] = jnp.zeros((lc, cw), du_ref.dtype)
        dv_ref[pl.ds(l, lc), :] = jnp.zeros((lc, cw), dv_ref.dtype)

    pshape = jax.ShapeDtypeStruct((t_all, d), MXU_DTYPE)
    tall = pl.BlockSpec((t_all, cw), lambda j: (0, j))
    return pl.pallas_call(
        body, name="conv31_bwd", grid=(nb,),
        in_specs=[pl.BlockSpec((l, cw), lambda j: (0, u_blk * nb + j)), pl.BlockSpec((l, cw), lambda j: (0, (u_blk + 1) * nb + j)),
                  pl.BlockSpec((width, cw), lambda j: (0, j)), pl.BlockSpec((l, cw), lambda j: (0, j))],
        out_specs=[tall, tall, pl.BlockSpec((width, cw), lambda j: (0, j)), pl.BlockSpec((1, cw), lambda j: (0, j))],
        out_shape=[pshape, pshape, jax.ShapeDtypeStruct((width, d), F32), jax.ShapeDtypeStruct((1, d), F32)],
        scratch_shapes=[pltpu.VMEM((l + 2 * reach, cw), F32)], compiler_params=_cp(("parallel",)),
    )(proj, proj, w, dcv)


def _softplus(x):
    return jnp.maximum(x, 0.0) + jnp.log(1.0 + jnp.exp(-jnp.abs(x)))


def _dt_fwd(proj, bias, a, dt_blk):
    t_all = proj.shape[0]
    hh = bias.shape[1]
    q = CHUNK

    def body(r_ref, b_ref, a_ref, dt_ref, cs_ref, tc_ref, cst_ref):
        dt = _softplus(r_ref[...] + b_ref[...])
        dt_ref[...] = dt
        da = dt * a_ref[...]
        li, si = _iota((q, q), 0), _iota((q, q), 1)
        reverse_cols = _iota((q, hh), 1) >= hh // 2
        cs = jnp.where(reverse_cols, _dot((si >= li).astype(F32), da, exact=True), _dot((si <= li).astype(F32), da, exact=True))
        cs_ref[...] = cs
        cst_ref[...] = cs.T
        total = jnp.where(_iota((1, hh), 1) >= hh // 2, cs_ref[0:1, :], cs_ref[q - 1:q, :])
        tc_ref[...] = total - cs

    row = pl.BlockSpec((q, hh), lambda i: (i, 0))
    shape = jax.ShapeDtypeStruct((t_all, hh), F32)
    return pl.pallas_call(
        body, name="dt_fwd", grid=(t_all // q,),
        in_specs=[pl.BlockSpec((q, hh), lambda i: (i, dt_blk)), _bc_spec(hh), _bc_spec(hh)],
        out_specs=[row, row, row, pl.BlockSpec((hh, q), lambda i: (0, i))],
        out_shape=[shape, shape, shape, jax.ShapeDtypeStruct((hh, t_all), F32)],
        compiler_params=_cp(("parallel",)),
    )(proj, bias, a)


def _three_way(x):
    def top(v):
        word = lax.bitcast_convert_type(v, jnp.uint32) & jnp.uint32(0xFFFF0000)
        return lax.bitcast_convert_type(word, F32)

    hi = top(x)
    rest = x - hi
    mid = top(rest)
    return hi.astype(jnp.bfloat16), mid.astype(jnp.bfloat16), (rest - mid).astype(jnp.bfloat16)


def _scan_columns(dt, cs, tc, groups2, hpg):
    t_all = dt.shape[0]
    parts = [part.reshape(t_all, groups2, 1, hpg) for arr in (dt, cs, tc) for part in _three_way(arr)]
    cols = jnp.concatenate(parts, axis=2).transpose(1, 0, 2, 3).reshape(groups2, t_all, 9 * hpg)
    return jnp.pad(cols, ((0, 0), (0, 0), (0, LANES - 9 * hpg)))


def _dt_bwd(proj, bias, dt, ddt, dda, dt_blk):
    t_all = proj.shape[0]
    hh = bias.shape[1]
    q = _tile(t_all, 1024, LANES)

    def body(r_ref, b_ref, dt_ref, ddt_ref, dda_ref, dr_ref, db_ref, da_ref):
        dr = ddt_ref[...] * jax.nn.sigmoid(r_ref[...] + b_ref[...])
        dr_ref[...] = dr.astype(dr_ref.dtype)
        i = pl.program_id(0)
        _acc(db_ref, jnp.sum(dr, axis=0, keepdims=True), i)
        _acc(da_ref, jnp.sum(dda_ref[...] * dt_ref[...], axis=0, keepdims=True), i)

    row = pl.BlockSpec((q, hh), lambda i: (i, 0))
    vec = jax.ShapeDtypeStruct((1, hh), F32)
    return pl.pallas_call(
        body, name="dt_bwd", grid=(t_all // q,),
        in_specs=[pl.BlockSpec((q, hh), lambda i: (i, dt_blk)), _bc_spec(hh), row, row, row],
        out_specs=[row, _bc_spec(hh), _bc_spec(hh)],
        out_shape=[jax.ShapeDtypeStruct((t_all, hh), MXU_DTYPE), vec, vec], compiler_params=_cp(("arbitrary",)),
    )(proj, bias, dt, ddt, dda)


_NT = (((1,), (1,)), ((), ()))
_TN = (((0,), (0,)), ((), ()))


def _dot(a, b, dims=None, exact=False):
    kw = dict(preferred_element_type=F32)
    if exact:
        kw["precision"] = HI
    if dims is None:
        return jnp.dot(a, b, **kw)
    return lax.dot_general(a, b, dims, **kw)


def _iota(shape, dim):
    return lax.broadcasted_iota(jnp.int32, shape, dim)


class _Ssd:
    def __init__(self, l, lc, di, p, reverse):
        self.q, self.n, self.g = CHUNK, SSM_STATE, SSM_GROUPS
        self.nl, self.ncx = l // CHUNK, lc // CHUNK
        self.ns = self.nl + self.ncx
        self.t_all, self.di, self.p, self.reverse = l + lc, di, p, reverse
        self.hpg = di // p // SSM_GROUPS
        self.gw = self.hpg * p
        self.ntile = self.gw // LANES
        self.hpt = LANES // p
        self.log2p = p.bit_length() - 1
        assert 1 << self.log2p == p and self.gw % LANES == 0 and self.n == LANES and self.q == LANES
        assert 9 * self.hpg <= LANES
        self.d = 1 if reverse else 0

    def chunk_at(self, step):
        if self.reverse:
            return self.ns - 1 - step
        return jnp.where(step < self.ncx, self.nl + step, step - self.ncx)

    def selectors(self):
        hpg = self.hpg
        k = jnp.arange(LANES)
        quantity, head, used = k // (3 * hpg), k % hpg, k < 9 * hpg
        lane_head = jnp.arange(LANES) // self.p
        tiles = jnp.concatenate([(used & (quantity == qo))[:, None] & (head[:, None] == tt * self.hpt + lane_head[None, :])
                                 for tt in range(self.ntile) for qo in range(3)], axis=1)
        heads = jnp.concatenate([jnp.broadcast_to((used & (quantity == 1) & (head == j))[:, None], (LANES, LANES))
                                 for j in range(hpg)], axis=1)
        return tiles.astype(jnp.bfloat16), heads.astype(jnp.bfloat16)

    def in_specs(self, chunk_of):
        g, n, hpg, q = self.g, self.n, self.hpg, self.q
        b_blk, c_blk = self.di // n, self.di // n + g
        d = self.d
        return [
            pl.BlockSpec((q, self.gw), lambda gi, i: (chunk_of(i), gi)),
            pl.BlockSpec((q, n), lambda gi, i: (chunk_of(i), b_blk + gi)),
            pl.BlockSpec((q, n), lambda gi, i: (chunk_of(i), c_blk + gi)),
            pl.BlockSpec((1, q, LANES), lambda gi, i: (d * g + gi, chunk_of(i), 0)),
            pl.BlockSpec((hpg, q), lambda gi, i: (d * g + gi, chunk_of(i))),
            pl.BlockSpec((LANES, self.ntile * 3 * LANES), lambda gi, i: (0, 0)),
            pl.BlockSpec((LANES, hpg * LANES), lambda gi, i: (0, 0)),
        ]

    def masks(self):
        li, si = _iota((self.q, self.q), 0), _iota((self.q, self.q), 1)
        if self.reverse:
            return si >= li, li >= si
        return si <= li, li <= si

    def spread(self, spread_all, tt):
        at = 3 * LANES * tt
        return tuple(spread_all[:, at + k * LANES:at + (k + 1) * LANES] for k in range(3))

    def head_lanes(self, qq):
        return lax.shift_right_logical(_iota((self.q, LANES), 1), self.log2p) == qq

    def head_sums(self, values, tt):
        sel = _iota((HEAD_COLS, LANES), 0) == tt * self.hpt + lax.shift_right_logical(_iota((HEAD_COLS, LANES), 1), self.log2p)
        parts = [part for v in values for part in _three_way(v)]
        sums = _dot(jnp.concatenate(parts, axis=0), sel.astype(jnp.bfloat16), _NT)
        out, at = [], 0
        for v in values:
            rows = v.shape[0]
            out.append(sums[at:at + rows] + sums[at + rows:at + 2 * rows] + sums[at + 2 * rows:at + 3 * rows])
            at += 3 * rows
        return out

    def state_scale(self, csr_ref):
        last = 0 if self.reverse else self.q - 1
        total = jnp.sum(jnp.where(_iota((self.hpg, self.q), 1) == last, csr_ref[...], 0.0), axis=1, keepdims=True)
        decay = jnp.broadcast_to(jnp.exp(total), (self.hpg, self.n))
        decay = jnp.concatenate([decay, jnp.zeros((HEAD_COLS - self.hpg, self.n), F32)], axis=0)
        rows = lax.shift_right_logical(_iota((self.gw, HEAD_COLS), 0), self.log2p) == _iota((self.gw, HEAD_COLS), 1)
        return _dot_parts(rows.astype(jnp.bfloat16), decay)


def _dot_parts(sel, v, dims=None):
    return sum(_dot(sel, part, dims) for part in _three_way(v))


def _ssd_fwd(xbc, cols, cs_t, l, lc, di, p, reverse):
    s = _Ssd(l, lc, di, p, reverse)
    q, n, gw = s.q, s.n, s.gw
    neg_inf = float("-inf")

    def body(xs_ref, b_ref, c_ref, cols_ref, csr_ref, et_ref, eh_ref, y_ref, hp_ref, h_scr):
        i = pl.program_id(1)

        @pl.when(i == 0)
        def _():
            h_scr[...] = jnp.zeros_like(h_scr)

        h = h_scr[...]
        hp_ref[0, 0] = h
        mask, _ = s.masks()
        cols = cols_ref[0]
        bb, cb = b_ref[...].astype(MXU_DTYPE), c_ref[...].astype(MXU_DTYPE)
        cbt = _dot(cb, bb, _NT)
        y_off = _dot(cb, h.astype(MXU_DTYPE), _NT)
        spread_all, cs_heads = _dot(cols, et_ref[...]), _dot(cols, eh_ref[...])
        w_tiles = []
        for tt in range(s.ntile):
            sl = slice(tt * LANES, (tt + 1) * LANES)
            dt_b, cs_b, tc_b = s.spread(spread_all, tt)
            x = xs_ref[:, sl] * dt_b
            ms, xhs = [], []
            for qq in range(s.hpt):
                j = tt * s.hpt + qq
                seg = cs_heads[:, j * LANES:(j + 1) * LANES] - csr_ref[j:j + 1, :]
                ms.append((cbt * jnp.exp(jnp.where(mask, seg, neg_inf))).astype(MXU_DTYPE))
                xhs.append(jnp.where(s.head_lanes(qq), x, 0.0).astype(MXU_DTYPE))
            yd = _dot(jnp.concatenate(ms, axis=1), jnp.concatenate(xhs, axis=0))
            y_ref[:, sl] = yd + y_off[:, sl] * jnp.exp(cs_b)
            w_tiles.append((x * jnp.exp(tc_b)).astype(MXU_DTYPE))
        wm = w_tiles[0] if s.ntile == 1 else jnp.concatenate(w_tiles, axis=1)
        h_scr[...] = h * s.state_scale(csr_ref) + _dot(wm, bb, _TN)

    d = "rev" if reverse else "fwd"
    e_tiles, e_heads = s.selectors()
    return pl.pallas_call(
        body, name=f"ssd_{d}", grid=(s.g, s.ns), in_specs=s.in_specs(s.chunk_at),
        out_specs=[pl.BlockSpec((q, gw), lambda gi, i: (s.chunk_at(i), gi)),
                   pl.BlockSpec((1, 1, gw, n), lambda gi, i: (i, gi, 0, 0))],
        out_shape=[jax.ShapeDtypeStruct((s.t_all, di), F32), jax.ShapeDtypeStruct((s.ns, s.g, gw, n), F32)],
        scratch_shapes=[pltpu.VMEM((gw, n), F32)],
        compiler_params=_cp(("parallel", "arbitrary")),
    )(xbc, xbc, xbc, cols, cs_t, e_tiles, e_heads)


def _ssd_bwd(xbc, cols, cs_t, a_cols, dy, hprev, l, lc, di, p, reverse, dsk=None, prev=None):
    s = _Ssd(l, lc, di, p, reverse)
    q, n, gw, hpg = s.q, s.n, s.gw, s.hpg
    neg_inf = float("-inf")
    n_extra = (dsk is not None) + (3 if prev is not None else 0)

    def chunk_of(i):
        return s.chunk_at(s.ns - 1 - i)

    def body(xs_ref, b_ref, c_ref, cols_ref, csr_ref, et_ref, eh_ref, ac_ref, dy_ref, hp_ref, *rest):
        extra, (dxs_ref, db_ref, dc_ref, ddt_ref, dda_ref, dh_scr) = rest[:n_extra], rest[n_extra:]
        dsk_ref = extra[0] if dsk is not None else None
        prev_refs = extra[-3:] if prev is not None else None
        i = pl.program_id(1)

        @pl.when(i == 0)
        def _():
            dh_scr[...] = jnp.zeros_like(dh_scr)

        latent = (chunk_of(i) < s.nl).astype(F32)
        h, dh = hp_ref[0, 0], dh_scr[...]
        hb, dhb = h.astype(MXU_DTYPE), dh.astype(MXU_DTYPE)
        mask, mask_t = s.masks()
        cols = cols_ref[0]
        bb, cb = b_ref[...].astype(MXU_DTYPE), c_ref[...].astype(MXU_DTYPE)
        cbt, bct = _dot(cb, bb, _NT), _dot(bb, cb, _NT)
        b_dh = _dot(bb, dhb, _NT)
        y_off0 = _dot(cb, hb, _NT)
        d_g, d_gt = jnp.zeros((q, q), F32), jnp.zeros((q, q), F32)
        dcs = jnp.zeros((q, HEAD_COLS), F32)
        ddt_x = jnp.zeros((q, HEAD_COLS), F32)
        r_state = jnp.zeros((16, HEAD_COLS), F32)
        spread_all, cs_heads = _dot(cols, et_ref[...]), _dot(cols, eh_ref[...])
        dye_tiles, xte_tiles = [], []
        for tt in range(s.ntile):
            sl = slice(tt * LANES, (tt + 1) * LANES)
            dt_b, cs_b, tc_b = s.spread(spread_all, tt)
            ecs_b, te_b = jnp.exp(cs_b), jnp.exp(tc_b)
            xs_t = xs_ref[:, sl]
            x = xs_t * dt_b
            d_y = dy_ref[:, sl] * latent
            dx_state = b_dh[:, sl] * te_b
            lms, lm_ts, m_ts, d_yhs, xhs = [], [], [], [], []
            for qq in range(s.hpt):
                j = tt * s.hpt + qq
                csc_b = cs_heads[:, j * LANES:(j + 1) * LANES]
                csr = csr_ref[j:j + 1, :]
                lms.append(jnp.exp(jnp.where(mask, csc_b - csr, neg_inf)))
                lm_ts.append(jnp.exp(jnp.where(mask_t, csr - csc_b, neg_inf)))
                m_ts.append(bct * lm_ts[-1])
                lanes = s.head_lanes(qq)
                d_yhs.append(jnp.where(lanes, d_y, 0.0).astype(MXU_DTYPE))
                xhs.append(jnp.where(lanes, x, 0.0).astype(MXU_DTYPE))
            d_yh_rows = jnp.concatenate(d_yhs, axis=0)
            d_m_all = _dot(d_yh_rows, x.astype(MXU_DTYPE), _NT)
            d_mt_all = _dot(jnp.concatenate(xhs, axis=0), d_y.astype(MXU_DTYPE), _NT)
            for qq in range(s.hpt):
                j = tt * s.hpt + qq
                d_m, d_mt = d_m_all[qq * q:(qq + 1) * q], d_mt_all[qq * q:(qq + 1) * q]
                r1 = jnp.sum(d_m * (cbt * lms[qq]), axis=1, keepdims=True)
                r2 = jnp.sum(d_mt * m_ts[qq], axis=1, keepdims=True)
                dcs = dcs + (r1 - r2) * (_iota((1, HEAD_COLS), 1) == j).astype(F32)
                d_g = d_g + d_m * lms[qq]
                d_gt = d_gt + d_mt * lm_ts[qq]
            d_x = _dot(jnp.concatenate([m.astype(MXU_DTYPE) for m in m_ts], axis=1), d_yh_rows) + dx_state
            d_xs = d_x * dt_b
            if dsk_ref is not None:
                d_xs = d_xs + d_y * dsk_ref[:, sl]
            if prev_refs is not None:
                d_xs = d_xs + prev_refs[0][:, sl]
            dxs_ref[:, sl] = d_xs
            fed = x * dx_state
            fed_rows = jnp.broadcast_to(jnp.sum(fed, axis=0, keepdims=True), (16, LANES))
            sums = s.head_sums([d_x * xs_t, d_y * y_off0[:, sl] * ecs_b - fed, fed_rows], tt)
            ddt_x, dcs, r_state = ddt_x + sums[0], dcs + sums[1], r_state + sums[2]
            dye_tiles.append((d_y * ecs_b).astype(MXU_DTYPE))
            xte_tiles.append((x * te_b).astype(MXU_DTYPE))
        dye = dye_tiles[0] if s.ntile == 1 else jnp.concatenate(dye_tiles, axis=1)
        xte = xte_tiles[0] if s.ntile == 1 else jnp.concatenate(xte_tiles, axis=1)
        d_c = _dot(d_g.astype(MXU_DTYPE), bb) + _dot(dye, hb)
        d_b = _dot(d_gt.astype(MXU_DTYPE), cb) + _dot(xte, dhb)
        if prev_refs is not None:
            d_b, d_c = d_b + prev_refs[1][...], d_c + prev_refs[2][...]
        dc_ref[...] = d_c
        db_ref[...] = d_b
        scale = s.state_scale(csr_ref)
        carried = dh * h * scale
        d_tot = jnp.sum(r_state, axis=0, keepdims=True) * 0.0625
        for j in range(hpg):
            part = jnp.sum(carried[j * p:(j + 1) * p, :], axis=0, keepdims=True)
            d_tot = d_tot + jnp.sum(part, axis=1, keepdims=True) * (_iota((1, HEAD_COLS), 1) == j).astype(F32)
        dda = _dot_parts(mask_t.astype(jnp.bfloat16), dcs) + d_tot
        ddt_ref[0] = ddt_x + dda * ac_ref[0]
        dda_ref[0] = dda
        dh_scr[...] = dh * scale + _dot(dye, cb, _TN)

    d = "rev" if reverse else "fwd"
    e_tiles, e_heads = s.selectors()
    col = pl.BlockSpec((1, q, HEAD_COLS), lambda gi, i: (gi, chunk_of(i), 0))
    gn = pl.BlockSpec((q, n), lambda gi, i: (chunk_of(i), gi))
    wide = pl.BlockSpec((q, gw), lambda gi, i: (chunk_of(i), gi))
    extra_specs, extra_args, aliases = [], [], {}
    if dsk is not None:
        extra_specs.append(pl.BlockSpec((1, gw), lambda gi, i: (0, gi)))
        extra_args.append(dsk)
    if prev is not None:
        first = 10 + len(extra_args)
        extra_specs += [wide, gn, gn]
        extra_args += list(prev)
        aliases = {first: 0, first + 1: 1, first + 2: 2}
    return pl.pallas_call(
        body, name=f"ssd_bwd_{d}", grid=(s.g, s.ns),
        in_specs=s.in_specs(chunk_of) + [
            pl.BlockSpec((1, 1, HEAD_COLS), lambda gi, i: (s.d * s.g + gi, 0, 0)),
            pl.BlockSpec((q, gw), lambda gi, i: (jnp.minimum(chunk_of(i), s.nl - 1), gi)),
            pl.BlockSpec((1, 1, gw, n), lambda gi, i: (s.ns - 1 - i, gi, 0, 0))] + extra_specs,
        out_specs=[wide, gn, gn, col, col],
        out_shape=[jax.ShapeDtypeStruct((s.t_all, di), F32), jax.ShapeDtypeStruct((s.t_all, s.g * n), F32),
                   jax.ShapeDtypeStruct((s.t_all, s.g * n), F32), jax.ShapeDtypeStruct((s.g, s.t_all, HEAD_COLS), F32),
                   jax.ShapeDtypeStruct((s.g, s.t_all, HEAD_COLS), F32)],
        scratch_shapes=[pltpu.VMEM((gw, n), F32)],
        input_output_aliases=aliases, compiler_params=_cp(("parallel", "arbitrary")),
    )(xbc, xbc, xbc, cols, cs_t, e_tiles, e_heads, a_cols, dy, hprev, *extra_args)


def _ada_fwd(crows, w, b):
    r, d = crows.shape
    ws = w.shape[1]
    tn = _tile(ws, 512, LANES)

    def body(c_ref, w_ref, b_ref, m_ref, s_ref):
        s = _silu(c_ref[...])
        s_ref[...] = s
        m_ref[...] = _dot(s.astype(MXU_DTYPE), w_ref[...].astype(MXU_DTYPE)) + b_ref[...]

    full = pl.BlockSpec((r, d), lambda j: (0, 0))
    return pl.pallas_call(
        body, name="ada_fwd", grid=(ws // tn,),
        in_specs=[full, pl.BlockSpec((d, tn), lambda j: (0, j)), pl.BlockSpec((1, tn), lambda j: (0, j))],
        out_specs=[pl.BlockSpec((r, tn), lambda j: (0, j)), full],
        out_shape=[jax.ShapeDtypeStruct((r, ws), F32), jax.ShapeDtypeStruct((r, d), F32)],
        compiler_params=_cp(("arbitrary",)),
    )(crows, w, b)


def _ada_bwd(s_t, w, dm):
    d, r = s_t.shape
    ws = w.shape[1]
    tn = _tile(ws, 512, LANES)

    def body(st_ref, w_ref, dm_ref, dw_ref, ds_ref):
        dmb = dm_ref[...].astype(MXU_DTYPE)
        dw_ref[...] = _dot(st_ref[...].astype(MXU_DTYPE), dmb)
        _acc(ds_ref, _dot(dmb, w_ref[...].astype(MXU_DTYPE), _NT), pl.program_id(0))

    return pl.pallas_call(
        body, name="ada_bwd", grid=(ws // tn,),
        in_specs=[pl.BlockSpec((d, r), lambda j: (0, 0)), pl.BlockSpec((d, tn), lambda j: (0, j)),
                  pl.BlockSpec((r, tn), lambda j: (0, j))],
        out_specs=[pl.BlockSpec((d, tn), lambda j: (0, j)), pl.BlockSpec((r, d), lambda j: (0, 0))],
        out_shape=[jax.ShapeDtypeStruct((d, ws), F32), jax.ShapeDtypeStruct((r, d), F32)],
        compiler_params=_cp(("arbitrary",)),
    )(s_t, w, dm)


def _adamw(w, g, m, v, name):
    r, c = w.shape
    t = _tile(r, max(8, 300_000 // c), 8)

    def body(w_ref, g_ref, m_ref, v_ref, d_ref, m2_ref, v2_ref):
        g = g_ref[...]
        m2 = ADAM_B1 * m_ref[...] + (1.0 - ADAM_B1) * g
        v2 = ADAM_B2 * v_ref[...] + (1.0 - ADAM_B2) * (g * g)
        m_hat = m2 / (1.0 - ADAM_B1 ** ADAM_STEP)
        v_hat = v2 / (1.0 - ADAM_B2 ** ADAM_STEP)
        d_ref[...] = -ADAM_LR * (m_hat / (jnp.sqrt(v_hat) + ADAM_EPS) + ADAM_WD * w_ref[...])
        m2_ref[...] = m2
        v2_ref[...] = v2

    blk = pl.BlockSpec((t, c), lambda i: (i, 0))
    shape = jax.ShapeDtypeStruct((r, c), F32)
    return pl.pallas_call(
        body, name=name, grid=(r // t,), in_specs=[blk] * 4, out_specs=[blk] * 3, out_shape=[shape] * 3,
        compiler_params=_cp(("parallel",)),
    )(w, g, m, v)


def _sum_devices(gathered):
    rows, w = gathered.shape
    per = rows // N_DEV

    def body(g_ref, o_ref):
        total = g_ref[pl.ds(0, per), :]
        for dev in range(1, N_DEV):
            total = total + g_ref[pl.ds(dev * per, per), :]
        o_ref[...] = total

    return pl.pallas_call(
        body, name="sum_devices", out_shape=jax.ShapeDtypeStruct((per, w), F32),
        in_specs=[pl.BlockSpec(memory_space=pltpu.VMEM)], out_specs=pl.BlockSpec(memory_space=pltpu.VMEM),
        compiler_params=_cp(),
    )(gathered)


def _c_ctx_grad(parts, c_ctx):
    rows, d = parts.shape
    per = rows // N_DEV

    def body(p_ref, c_ref, o_ref):
        total = p_ref[pl.ds(0, 1), :]
        for chip in range(1, N_SHARD):
            total = total + p_ref[pl.ds(2 * chip * per, 1), :]
        _, vjp = jax.vjp(_silu, c_ref[...])
        o_ref[...] = vjp(total)[0]

    return pl.pallas_call(
        body, name="c_ctx_grad", out_shape=jax.ShapeDtypeStruct((1, d), F32),
        in_specs=[pl.BlockSpec(memory_space=pltpu.VMEM)] * 2, out_specs=pl.BlockSpec(memory_space=pltpu.VMEM),
        compiler_params=_cp(),
    )(parts, c_ctx)


def _pad_rows(a, rows, width):
    return jnp.pad(a, ((0, rows - a.shape[0]), (0, width - a.shape[1])))


def _pack(vectors, quantum):
    flat = jnp.concatenate([v.reshape(-1) for v in vectors])
    return jnp.pad(flat, (0, -flat.shape[0] % quantum))


def kernel(x, c, ctx, c_ctx, w_mod, b_mod, norm_mix, w_in, ssm_conv_w, ssm_conv_b, dt_bias, a_log, d_skip, ssm_norm, cf_conv_w, cf_conv_b, cf_ln_g, cf_ln_b, w_proj_a, w_proj_b, w_out, norm_ffn, w_ffn_gate, w_ffn_up, w_ffn_down, norm_final, loss_target, m_c_ctx, m_w_mod, m_b_mod, m_norm_mix, m_w_in, m_ssm_conv_w, m_ssm_conv_b, m_dt_bias, m_a_log, m_d_skip, m_ssm_norm, m_cf_conv_w, m_cf_conv_b, m_cf_ln_g, m_cf_ln_b, m_w_proj_a, m_w_proj_b, m_w_out, m_norm_ffn, m_w_ffn_gate, m_w_ffn_up, m_w_ffn_down, m_norm_final, v_c_ctx, v_w_mod, v_b_mod, v_norm_mix, v_w_in, v_ssm_conv_w, v_ssm_conv_b, v_dt_bias, v_a_log, v_d_skip, v_ssm_norm, v_cf_conv_w, v_cf_conv_b, v_cf_ln_g, v_cf_ln_b, v_w_proj_a, v_w_proj_b, v_w_out, v_norm_ffn, v_w_ffn_gate, v_w_ffn_up, v_w_ffn_down, v_norm_final):
    l, d = x.shape[1], x.shape[2]
    lc = ctx.shape[1]
    t_all = l + lc
    di = ssm_norm.shape[-1]
    h = d_skip.shape[-1]
    p = di // h
    g, n = SSM_GROUPS, SSM_STATE
    hpg = h // g
    conv_dim = di + 2 * g * n
    df = w_ffn_down.shape[1] * N_SHARD
    assert 2 * h == LANES and d % (2 * LANES) == 0

    my_x, my_y, my_c = _mesh_pos()
    chip = 2 * my_x + my_y
    dev = 2 * chip + my_c

    x2, ctx2, tgt = x[0], ctx[0], loss_target[0]
    row = lambda a: a.reshape(1, -1)

    cw_shard, cfw_shard = ssm_conv_w[0], cf_conv_w[0]
    k5, k31 = cw_shard.shape[0], cfw_shard.shape[0]
    r5, r31 = -(-k5 // 8) * 8, -(-k31 // 8) * 8
    wp = max(d, cw_shard.shape[1], cfw_shard.shape[1])
    packed = jnp.concatenate([_pad_rows(c, 8, wp), _pad_rows(cw_shard, r5, wp), _pad_rows(cfw_shard, r31, wp)], axis=0)
    got = _allgather_small(packed, "ag_params").reshape(N_DEV, 8 + r5 + r31, wp)
    c_all = got[:, 0, :d]
    conv_w = got[0::2, 8:8 + k5, :cw_shard.shape[1]].transpose(1, 0, 2).reshape(k5, conv_dim)
    cf_w = got[0::2, 8 + r5:8 + r5 + k31, :cfw_shard.shape[1]].transpose(1, 0, 2).reshape(k31, d)

    ws = w_mod.shape[2]
    crows = jnp.concatenate([c_all, row(c_ctx), jnp.zeros((7, d), F32)], axis=0)
    b_mod_mine = lax.dynamic_slice(b_mod, (0, chip * ws), (1, ws))
    m_part, s_rows = _ada_fwd(crows, w_mod[0], b_mod_mine)
    m_full = _allgather_small(m_part, "ag_mod").reshape(N_DEV, 16, ws)[0::2].transpose(1, 0, 2).reshape(16, N_SHARD * ws)
    m_lat = lax.dynamic_slice(m_full, (dev, 0), (1, 6 * d))
    sh1, sc1, g1, sh2, sc2, g2 = [m_lat[:, i * d:(i + 1) * d] for i in range(6)]
    csh1, csc1 = m_full[8:9, 0:d], m_full[8:9, d:2 * d]

    gate_up = jnp.concatenate([w_ffn_gate[0].T, w_ffn_up[0].T], axis=0)
    shards = [w_in[0].T, gate_up, w_proj_a[0], w_proj_b[0], w_out[0], w_ffn_down[0]]
    shards = [s.astype(WIRE_DTYPE) for s in shards]
    (win_got,) = _run_side(_gather_side(shards[:1]), "ag_w_in")
    mine = (jnp.arange(N_SHARD) == chip)[:, None, None]
    win_t = jnp.where(mine, shards[0][None], win_got.reshape(N_SHARD, -1, d)).reshape(win_got.shape)
    o_xbc, o_dt, o_glu, o_gates = di, di + conv_dim, di + conv_dim + 2 * h, di + conv_dim + 2 * h + 2 * d
    win_work = jnp.concatenate([win_t[:o_xbc], win_t[o_glu:], win_t[o_xbc:o_dt], win_t[o_dt:o_glu]], axis=0)
    c_u, c_ga, c_xbc, c_dt = di, di + 2 * d, di + 4 * d, di + 4 * d + conv_dim

    nm = norm_mix
    hx = _mod_fwd(x2, ctx2, nm, sc1, sh1, csc1, csh1)
    proj, *rest = _matmul(hx, win_work.T, tm=384, tn=29 * LANES, n_outer=True, name="mm_proj", side=_gather_side(shards[1:]))
    wgu, wpa, wpb, wout, wdn = _fill_own_rows(rest, shards[1:])
    xbc = _conv5_fwd(proj, conv_w, ssm_conv_b, l, lc, c_xbc, conv_dim)
    a = -jnp.exp(a_log.reshape(1, 2 * h))
    dt, cs, tc, cs_t = _dt_fwd(proj, dt_bias.reshape(1, 2 * h), a, c_dt // LANES)
    cols = _scan_columns(dt, cs, tc, 2 * g, hpg)
    a_cols = jnp.pad(a.reshape(2 * g, 1, hpg), ((0, 0), (0, 0), (0, HEAD_COLS - hpg)))
    y_f, hp_f = _ssd_fwd(xbc, cols, cs_t, l, lc, di, p, False)
    y_r, hp_r = _ssd_fwd(xbc, cols, cs_t, l, lc, di, p, True)
    dsk = jnp.repeat(d_skip.reshape(h), p).reshape(1, di)
    ya_in = _gate_fwd(y_f, y_r, xbc, proj, dsk, ssm_norm, l, di)
    y_a = _matmul(ya_in, wpa, tk=di, name="mm_ya")
    u_blk = c_u // d
    cv = _conv31_fwd(proj, cf_w, cf_conv_b, l, d, u_blk)
    cf = _ln_fwd(cv, cf_ln_g, cf_ln_b)
    y_b = _matmul(cf, wpb, name="mm_yb")
    ga_blk = c_ga // d
    merged = _merge_fwd(y_a, y_b, proj, ga_blk)
    mix = _matmul(merged, wout, name="mm_mix")
    x1, hx2 = _res_fwd(x2, mix, g1, norm_ffn, sc2, sh2)
    gu = _matmul(hx2, wgu, tb=True, tm=1024, tn=1024, name="mm_gu")
    act = _swiglu_fwd(gu, df)
    dn = _matmul(act, wdn, tk=df, name="mm_dn")
    loss, dx1, ddn, dg2, d_norm_final = _loss_and_grads(x1, dn, g2, row(norm_final), tgt)

    dact = _matmul(ddn, wdn, tb=True, tm=1024, tn=_tile(df, 1024, LANES), name="mm_dact")
    dw_dn = _matmul(act, ddn, ta=True, tn=d, tk=1024, name="mm_dw_dn")
    dgu = _swiglu_bwd(gu, dact, df)
    dhx2 = _matmul(dgu, wgu, tm=1024, tn=1024, tk=_tile(2 * df, 2816, LANES), name="mm_dhx2")
    dw_gu = _matmul(dgu, hx2, ta=True, tm=1024, tn=d, tk=1024, name="mm_dw_gu")
    dx_res, dmix, dg1, d_norm_ffn, dsc2, dsh2 = _res_bwd(x2, mix, g1, norm_ffn, sc2, sh2, dx1, dhx2)
    dmerged = _matmul(dmix, wout, tb=True, name="mm_dmerged")
    dw_out = _matmul(merged, dmix, ta=True, tn=d, tk=1024, name="mm_dw_out")
    dya, dyb, dga, dgb = _merge_bwd(y_a, y_b, proj, ga_blk, dmerged, lc)
    dcf = _matmul(dyb, wpb, tb=True, name="mm_dcf")
    dw_pb = _matmul(cf, dyb, ta=True, tn=d, tk=1024, name="mm_dw_pb")
    dcv, d_ln_g, d_ln_b = _ln_bwd(cv, cf_ln_g, cf_ln_b, dcf)
    du, dv, d_cf_w, d_cf_b = _conv31_bwd(proj, cf_w, dcv, l, lc, d, u_blk)
    dya_in = _matmul(dya, wpa, tb=True, tn=_tile(di, 1024, LANES), name="mm_dya_in")
    dw_pa = _matmul(ya_in, dya, ta=True, tn=d, tk=1024, name="mm_dw_pa")
    dy, dz, ddsk, d_ssm_norm = _gate_bwd(y_f, y_r, xbc, proj, dsk, ssm_norm, dya_in, l, lc, di)
    dxs_f, db_f, dc_f, ddt_f, dda_f = _ssd_bwd(xbc, cols, cs_t, a_cols, dy, hp_f, l, lc, di, p, False, dsk=dsk)
    dxs, db, dc, ddt_r, dda_r = _ssd_bwd(xbc, cols, cs_t, a_cols, dy, hp_r, l, lc, di, p, True, prev=(dxs_f, db_f, dc_f))
    dxs_raw, dcw_x, dcb_x = _conv5_bwd(proj, conv_w, ssm_conv_b, [dxs], l, lc, c_xbc, 0, di)
    db_raw, dcw_b, dcb_b = _conv5_bwd(proj, conv_w, ssm_conv_b, [db], l, lc, c_xbc, di, g * n)
    dc_raw, dcw_c, dcb_c = _conv5_bwd(proj, conv_w, ssm_conv_b, [dc], l, lc, c_xbc, di + g * n, g * n)
    d_conv_w = jnp.concatenate([dcw_x, dcw_b, dcw_c], axis=1)
    d_conv_b = jnp.concatenate([dcb_x, dcb_b, dcb_c], axis=1)
    heads = lambda f, r: jnp.concatenate([t[:, :, :hpg].transpose(1, 0, 2).reshape(t_all, h) for t in (f, r)], axis=1)
    ddt_raw, d_dt_bias, dda_dt = _dt_bwd(proj, dt_bias.reshape(1, 2 * h), dt, heads(ddt_f, ddt_r), heads(dda_f, dda_r), c_dt // LANES)
    d_a_log = dda_dt * a
    dproj = jnp.concatenate([dz, du, dv, dga, dgb, dxs_raw, db_raw, dc_raw, ddt_raw], axis=1)
    wire, own = _reduce_scatter_begin([dw_gu, dw_pa, dw_pb, dw_out, dw_dn], "a")
    dhx, *recv = _matmul(dproj, win_work, tm=768, tn=1024, tk=_tile(win_work.shape[0], 4096, LANES), name="mm_dhx",
                         side=_scatter_side(wire))
    g_gu_t, g_pa, g_pb, g_out, g_dn = _reduce_scatter_end(own, recv, "a")
    g_gate_t, g_up_t = g_gu_t[:df // N_SHARD], g_gu_t[df // N_SHARD:]
    dw_in_work = _matmul(dproj, hx, ta=True, tm=640, tn=d, tk=1408, name="mm_dw_in")
    grad_x, d_norm_mix, dsc1, dsh1, dcsc1, dcsh1 = _mod_bwd(x2, ctx2, nm, sc1, sh1, csc1, csh1, dhx, dx_res)

    dw_in_t = jnp.concatenate([dw_in_work[:c_u], dw_in_work[c_xbc:], dw_in_work[c_u:c_xbc]], axis=0)
    wire, own = _reduce_scatter_begin([dw_in_t], "b")
    (g_in_t,) = _reduce_scatter_end(own, _run_side(_scatter_side(wire), "rs_scatter_b"), "b")
    g_in, g_gate, g_up = g_in_t.T, g_gate_t.T, g_up_t.T

    zeros_d = jnp.zeros((1, d), F32)
    dm_lat = jnp.concatenate([dsh1, dsc1, dg1, dsh2, dsc2, dg2], axis=1)
    dm_ctx = jnp.concatenate([dcsh1, dcsc1] + [zeros_d] * 4, axis=1)
    d_d_skip = ddsk.reshape(h, p).sum(axis=1)
    replicated = [dm_lat + dm_ctx, d_norm_mix, d_conv_b, d_dt_bias, d_a_log, d_d_skip, d_ssm_norm, d_cf_b, d_ln_g, d_ln_b,
                  d_norm_ffn, d_norm_final]
    rep_w = [b_mod, norm_mix, ssm_conv_b, dt_bias, a_log, d_skip, ssm_norm, cf_conv_b, cf_ln_g, cf_ln_b, norm_ffn, norm_final]
    rep_m = [m_b_mod, m_norm_mix, m_ssm_conv_b, m_dt_bias, m_a_log, m_d_skip, m_ssm_norm, m_cf_conv_b, m_cf_ln_g, m_cf_ln_b,
             m_norm_ffn, m_norm_final]
    rep_v = [v_b_mod, v_norm_mix, v_ssm_conv_b, v_dt_bias, v_a_log, v_d_skip, v_ssm_norm, v_cf_conv_b, v_cf_ln_g, v_cf_ln_b,
             v_norm_ffn, v_norm_final]
    quantum = 8 * LANES
    rep_flat = _pack(replicated, quantum)
    n_rep = rep_flat.shape[0]
    summed_parts = [rep_flat, _pack([d_conv_w, d_cf_w, dm_ctx], quantum)]
    n_sum = n_rep + summed_parts[1].shape[0]
    everything = jnp.concatenate(summed_parts + [_pack([dm_lat], quantum)])
    gathered = _allgather_small(everything.reshape(8, -1), "ag_small_grads")
    w8 = gathered.shape[1]
    summed = _sum_devices(gathered).reshape(-1)
    dm_lat_all = gathered.reshape(N_DEV, 8 * w8)[:, n_sum:n_sum + 6 * d]
    off = n_rep
    g_conv_w_full = summed[off:off + k5 * conv_dim].reshape(k5, conv_dim)
    off += k5 * conv_dim
    g_cf_w_full = summed[off:off + k31 * d].reshape(k31, d)
    off += k31 * d
    dm_ctx_all = summed[off:off + 6 * d].reshape(1, 6 * d)
    g_conv_w = lax.dynamic_slice(g_conv_w_full, (0, chip * cw_shard.shape[1]), cw_shard.shape)
    g_cf_w = lax.dynamic_slice(g_cf_w_full, (0, chip * cfw_shard.shape[1]), cfw_shard.shape)

    dm_rows = jnp.concatenate([dm_lat_all, dm_ctx_all, jnp.zeros((7, 6 * d), F32)], axis=0)
    dm_mine = lax.dynamic_slice(dm_rows, (0, chip * ws), (16, ws))
    g_w_mod, ds_part = _ada_bwd(s_rows.T, w_mod[0], dm_mine)
    ds_all = _allgather_small(ds_part[8:16], "ag_c_ctx")
    g_c_ctx = _c_ctx_grad(ds_all, row(c_ctx))

    grads, deltas, new_ms, new_vs = {}, {}, {}, {}

    def update(name, w2, g2, m2, v2, shape):
        dl, mm, vv = _adamw(w2, g2, m2, v2, f"adamw_{name}")
        grads[name], deltas[name], new_ms[name], new_vs[name] = (t.reshape(shape) for t in (g2, dl, mm, vv))

    for name, w_, g_, m_, v_ in [
            ("w_mod", w_mod, g_w_mod, m_w_mod, v_w_mod), ("w_in", w_in, g_in, m_w_in, v_w_in),
            ("ssm_conv_w", ssm_conv_w, g_conv_w, m_ssm_conv_w, v_ssm_conv_w),
            ("cf_conv_w", cf_conv_w, g_cf_w, m_cf_conv_w, v_cf_conv_w),
            ("w_proj_a", w_proj_a, g_pa, m_w_proj_a, v_w_proj_a), ("w_proj_b", w_proj_b, g_pb, m_w_proj_b, v_w_proj_b),
            ("w_out", w_out, g_out, m_w_out, v_w_out), ("w_ffn_gate", w_ffn_gate, g_gate, m_w_ffn_gate, v_w_ffn_gate),
            ("w_ffn_up", w_ffn_up, g_up, m_w_ffn_up, v_w_ffn_up), ("w_ffn_down", w_ffn_down, g_dn, m_w_ffn_down, v_w_ffn_down)]:
        update(name, w_[0], g_, m_[0], v_[0], w_.shape)
    update("c_ctx", row(c_ctx), g_c_ctx, row(m_c_ctx), row(v_c_ctx), c_ctx.shape)

    rep_names = ["b_mod", "norm_mix", "ssm_conv_b", "dt_bias", "a_log", "d_skip", "ssm_norm", "cf_conv_b", "cf_ln_g", "cf_ln_b",
                 "norm_ffn", "norm_final"]
    as8 = lambda vs: _pack(vs, quantum).reshape(8, -1)
    g8 = summed[:n_rep].reshape(8, -1)
    d8, m8, v8 = _adamw(as8(rep_w), g8, as8(rep_m), as8(rep_v), "adamw_replicated")
    off = 0
    for name, w_ in zip(rep_names, rep_w):
        size = w_.size
        for store, packed8 in ((grads, g8), (deltas, d8), (new_ms, m8), (new_vs, v8)):
            store[name] = packed8.reshape(-1)[off:off + size].reshape(w_.shape)
        off += size

    order = ["c_ctx", "w_mod", "b_mod", "norm_mix", "w_in", "ssm_conv_w", "ssm_conv_b", "dt_bias", "a_log", "d_skip", "ssm_norm",
             "cf_conv_w", "cf_conv_b", "cf_ln_g", "cf_ln_b", "w_proj_a", "w_proj_b", "w_out", "norm_ffn", "w_ffn_gate", "w_ffn_up",
             "w_ffn_down", "norm_final"]
    total_loss = lax.psum(loss[0, 0], ("x", "y", "c"))
    return (total_loss, grad_x.reshape(x.shape), *[grads[k] for k in order], *[deltas[k] for k in order],
            *[new_ms[k] for k in order], *[new_vs[k] for k in order])
```

```python
import functools

import jax
import jax.numpy as jnp
from jax import lax
from jax.experimental import pallas as pl
from jax.experimental.pallas import tpu as pltpu

F32 = jnp.float32
MXU_DTYPE = jnp.bfloat16
WIRE_DTYPE = jnp.bfloat16
HI = lax.Precision.HIGHEST
EPS = 1e-6
SSM_GROUPS = 8
SSM_STATE = 128
CHUNK = 128
GRID_W = 64
LANES = 128
HEAD_COLS = 16
VMEM_LIMIT = 52 * 1024 * 1024
ADAM_LR, ADAM_B1, ADAM_B2, ADAM_EPS, ADAM_WD, ADAM_STEP = 0.001, 0.9, 0.999, 1e-08, 0.01, 10
MESH = pl.DeviceIdType.MESH
N_SHARD = 4
N_DEV = 8


def _cp(sem=None):
    kw = dict(vmem_limit_bytes=VMEM_LIMIT)
    if sem is not None:
        kw["dimension_semantics"] = sem
    return pltpu.CompilerParams(**kw)


def _tile(n, target, q):
    best = None
    for t in range(q, min(n, target) + 1, q):
        if n % t == 0:
            best = t
    return best if best is not None else n


def _acc(ref, val, i):
    @pl.when(i == 0)
    def _():
        ref[...] = val

    @pl.when(i > 0)
    def _():
        ref[...] += val


def _bc_spec(w):
    return pl.BlockSpec((1, w), lambda *_: (0, 0))


def _rms(x, w):
    return x * lax.rsqrt(jnp.mean(x * x, axis=-1, keepdims=True) + EPS) * w


def _silu(x):
    return x * jax.nn.sigmoid(x)


def _f_mod(x, w, sc, sh):
    return _rms(x, w) * (1.0 + sc) + sh


def _f_gate(yf, yr, xs, z, dsk, wn):
    return _rms((yf + yr + dsk * xs) * _silu(z), wn)


def _f_ln(cv, g, b):
    mu = jnp.mean(cv, axis=-1, keepdims=True)
    xc = cv - mu
    var = jnp.mean(xc * xc, axis=-1, keepdims=True)
    return _silu(xc * lax.rsqrt(var + EPS) * g + b)


def _f_merge(ya, yb, ga, gb):
    return jax.nn.sigmoid(ga) * ya + jax.nn.sigmoid(gb) * yb


def _f_res(x, mix, g1, wn, sc2, sh2):
    x1 = x + g1 * mix
    return x1, _rms(x1, wn) * (1.0 + sc2) + sh2


def _f_swiglu(gt, up):
    return _silu(gt) * up


def _f_loss(x1, dn, g2, wn, tgt):
    out = _rms(x1 + g2 * dn, wn)
    err = out - tgt
    per_tok = jnp.mean(err * err, axis=-1, keepdims=True)
    return 0.5 * jnp.sum(per_tok, axis=0, keepdims=True)


def _matmul(a, b, *, ta=False, tb=False, out_dtype=F32, tm=1024, tn=512, tk=2048, name, side=None, n_outer=False):
    m, k = (a.shape[1], a.shape[0]) if ta else a.shape
    n = b.shape[0] if tb else b.shape[1]
    assert (b.shape[1] if tb else b.shape[0]) == k, (a.shape, b.shape, ta, tb)
    tm, tn, tk = _tile(m, tm, LANES if ta else 16), _tile(n, tn, LANES), _tile(k, tk, LANES)
    grid = (n // tn, m // tm, k // tk) if n_outer else (m // tm, n // tn, k // tk)
    ij = (lambda g0, g1: (g1, g0)) if n_outer else (lambda g0, g1: (g0, g1))
    nk = grid[2]
    dims = (((0 if ta else 1,), (1 if tb else 0,)), ((), ()))
    n_in = len(side.inputs) if side else 0
    n_out = len(side.out_shapes) if side else 0

    def body(a_ref, b_ref, *rest):
        side_in, o_ref, side_out, scratch = rest[:n_in], rest[n_in], rest[n_in + 1:n_in + 1 + n_out], rest[n_in + 1 + n_out:]
        steps = [pl.program_id(axis) for axis in range(3)]
        if side:
            @pl.when((steps[0] == 0) & (steps[1] == 0) & (steps[2] == 0))
            def _():
                side.start(side_in, side_out, *scratch[-2:])

        prod = lax.dot_general(a_ref[...].astype(MXU_DTYPE), b_ref[...].astype(MXU_DTYPE), dims,
                               preferred_element_type=F32)
        if nk == 1:
            o_ref[...] = prod.astype(o_ref.dtype)
        else:
            acc = scratch[0]
            _acc(acc, prod, steps[2])

            @pl.when(steps[2] == nk - 1)
            def _():
                o_ref[...] = acc[...].astype(o_ref.dtype)

        if side:
            @pl.when((steps[0] == grid[0] - 1) & (steps[1] == grid[1] - 1) & (steps[2] == nk - 1))
            def _():
                side.finish(side_in, side_out, *scratch[-2:])

    a_spec = (pl.BlockSpec((tk, tm), lambda g0, g1, kk: (kk, ij(g0, g1)[0])) if ta
              else pl.BlockSpec((tm, tk), lambda g0, g1, kk: (ij(g0, g1)[0], kk)))
    b_spec = (pl.BlockSpec((tn, tk), lambda g0, g1, kk: (ij(g0, g1)[1], kk)) if tb
              else pl.BlockSpec((tk, tn), lambda g0, g1, kk: (kk, ij(g0, g1)[1])))
    out = pl.pallas_call(
        body, name=name, grid=grid, in_specs=[a_spec, b_spec] + [_HBM] * n_in,
        out_specs=[pl.BlockSpec((tm, tn), lambda g0, g1, kk: ij(g0, g1))] + [_HBM] * n_out,
        out_shape=[jax.ShapeDtypeStruct((m, n), out_dtype)] + (side.out_shapes if side else []),
        scratch_shapes=([] if nk == 1 else [pltpu.VMEM((tm, tn), F32)]) + (side.scratch() if side else []),
        compiler_params=_cp(("arbitrary",) * 3 if side else ("parallel", "parallel", "arbitrary")),
    )(a, b, *(side.inputs if side else []))
    return out if side else out[0]


def _mesh_pos():
    return lax.axis_index("x"), lax.axis_index("y"), lax.axis_index("c")


def _other_chips(x, y):
    return [(1 - x, y), (x, 1 - y), (1 - x, 1 - y)]


def _allgather_small(v, name):
    m_per, n = v.shape

    def body(x_ref, out_ref, send_sems, recv_sems, local_sem):
        x, y, c = _mesh_pos()
        me, sibling = (x, y, c), (x, y, 1 - c)
        chips = _other_chips(x, y)

        def rows(px, py, pc):
            return out_ref.at[pl.ds((4 * px + 2 * py + pc) * m_per, m_per), :]

        def copy(k, block, to, src=None):
            return pltpu.make_async_remote_copy(
                src_ref=rows(*block) if src is None else src, dst_ref=rows(*block),
                send_sem=send_sems.at[k], recv_sem=recv_sems.at[k], device_id=to, device_id_type=MESH)

        mine = pltpu.make_async_copy(x_ref, rows(*me), local_sem)
        mine.start()
        first = [copy(0, me, sibling, src=x_ref)]
        first += [copy(1 + j, me, (*chip, c), src=x_ref) for j, chip in enumerate(chips)]
        for cp in first:
            cp.start()
        passed = [copy(4 + j, (*chip, c), sibling) for j, chip in enumerate(chips)]
        for j, chip in enumerate(chips):
            copy(1 + j, (*chip, c), me).wait_recv()
            passed[j].start()
        copy(0, sibling, me).wait_recv()
        for j, chip in enumerate(chips):
            copy(4 + j, (*chip, 1 - c), me).wait_recv()
        for cp in first + passed:
            cp.wait_send()
        mine.wait()

    return pl.pallas_call(
        body, name=name, out_shape=jax.ShapeDtypeStruct((N_DEV * m_per, n), v.dtype),
        in_specs=[pl.BlockSpec(memory_space=pltpu.VMEM)], out_specs=pl.BlockSpec(memory_space=pltpu.VMEM),
        scratch_shapes=[pltpu.SemaphoreType.DMA((7,)), pltpu.SemaphoreType.DMA((7,)), pltpu.SemaphoreType.DMA],
        compiler_params=_cp(),
    )(v)


_HBM = pl.BlockSpec(memory_space=pltpu.HBM)


class _Side:
    def __init__(self, inputs, out_shapes, n_sems, start, finish):
        self.inputs, self.out_shapes, self.n_sems, self.start, self.finish = inputs, out_shapes, n_sems, start, finish

    def scratch(self):
        return [pltpu.SemaphoreType.DMA((self.n_sems,)), pltpu.SemaphoreType.DMA((self.n_sems,))]


def _run_side(side, name):
    n_in = len(side.inputs)

    def body(*refs):
        src, dst, sems = refs[:n_in], refs[n_in:-2], refs[-2:]
        side.start(src, dst, *sems)
        side.finish(src, dst, *sems)

    return pl.pallas_call(
        body, name=name, out_shape=side.out_shapes, in_specs=[_HBM] * n_in, out_specs=[_HBM] * len(side.out_shapes),
        scratch_shapes=side.scratch(), compiler_params=_cp(),
    )(*side.inputs)


def _gather_side(shards):
    n = len(shards)

    def plan(src, dst, send_sems, recv_sems):
        x, y, c = _mesh_pos()
        chips = _other_chips(x, y)

        def half(i, px, py, pc):
            r = src[i].shape[0]
            return dst[i].at[pl.ds(pl.multiple_of((2 * px + py) * r + pc * (r // 2), 16), r // 2), :]

        def copy(i, k, block, to, own=False):
            r = src[i].shape[0]
            mine = src[i].at[pl.ds(pl.multiple_of(c * (r // 2), 16), r // 2), :]
            return pltpu.make_async_remote_copy(
                src_ref=mine if own else half(i, *block), dst_ref=half(i, *block), send_sem=send_sems.at[6 * i + k],
                recv_sem=recv_sems.at[6 * i + k], device_id=to, device_id_type=MESH)

        first = [copy(i, j, (x, y, c), (*chip, c), own=True) for i in range(n) for j, chip in enumerate(chips)]
        return (x, y, c), chips, copy, first

    def start(src, dst, send_sems, recv_sems):
        for cp in plan(src, dst, send_sems, recv_sems)[3]:
            cp.start()

    def finish(src, dst, send_sems, recv_sems):
        (x, y, c), chips, copy, first = plan(src, dst, send_sems, recv_sems)
        passed = []
        for i in range(n):
            for j, chip in enumerate(chips):
                copy(i, j, (*chip, c), (x, y, c)).wait_recv()
                passed.append(copy(i, 3 + j, (*chip, c), (x, y, 1 - c)))
                passed[-1].start()
        for i in range(n):
            for j, chip in enumerate(chips):
                copy(i, 3 + j, (*chip, 1 - c), (x, y, c)).wait_recv()
        for cp in first + passed:
            cp.wait_send()

    shapes = [jax.ShapeDtypeStruct((N_SHARD * s.shape[0], s.shape[1]), s.dtype) for s in shards]
    return _Side(list(shards), shapes, 6 * n, start, finish)


def _fill_own_rows(gathered, shards):
    chip = 2 * lax.axis_index("x") + lax.axis_index("y")
    return [lax.dynamic_update_slice(full, s, (chip * s.shape[0], 0)) for full, s in zip(gathered, shards)]


def _swap_halves(parts, name):
    n = len(parts)

    def body(*refs):
        src, dst = refs[:n], refs[n:2 * n]
        send_sems, recv_sems = refs[2 * n:]
        x, y, c = _mesh_pos()
        copies = [pltpu.make_async_remote_copy(
            src_ref=src[i].at[s, 1 - c], dst_ref=dst[i].at[s], send_sem=send_sems.at[N_SHARD * i + s],
            recv_sem=recv_sems.at[N_SHARD * i + s], device_id=(x, y, 1 - c), device_id_type=MESH)
            for i in range(n) for s in range(N_SHARD)]
        for cp in copies:
            cp.start()
        for cp in copies:
            cp.wait()

    return pl.pallas_call(
        body, name=name,
        out_shape=[jax.ShapeDtypeStruct((N_SHARD,) + p.shape[2:], p.dtype) for p in parts],
        in_specs=[_HBM] * n, out_specs=[_HBM] * n,
        scratch_shapes=[pltpu.SemaphoreType.DMA((N_SHARD * n,)), pltpu.SemaphoreType.DMA((N_SHARD * n,))],
        compiler_params=_cp(),
    )(*parts)


def _scatter_side(parts):
    n = len(parts)

    def copies(src, dst, send_sems, recv_sems):
        x, y, c = _mesh_pos()
        return [pltpu.make_async_remote_copy(
            src_ref=src[i].at[2 * chip[0] + chip[1]], dst_ref=dst[i].at[j], send_sem=send_sems.at[3 * i + j],
            recv_sem=recv_sems.at[3 * i + j], device_id=(*chip, c), device_id_type=MESH)
            for i in range(n) for j, chip in enumerate(_other_chips(x, y))]

    def start(*refs):
        for cp in copies(*refs):
            cp.start()

    def finish(*refs):
        for cp in copies(*refs):
            cp.wait()

    shapes = [jax.ShapeDtypeStruct((3,) + p.shape[1:], p.dtype) for p in parts]
    return _Side(list(parts), shapes, 3 * n, start, finish)


def _join_halves(halves, name):
    n = len(halves)

    def body(*refs):
        src, dst = refs[:n], refs[n:2 * n]
        send_sems, recv_sems = refs[2 * n:]
        x, y, c = _mesh_pos()
        remote = [pltpu.make_async_remote_copy(
            src_ref=src[i], dst_ref=dst[i].at[c], send_sem=send_sems.at[i], recv_sem=recv_sems.at[i],
            device_id=(x, y, 1 - c), device_id_type=MESH) for i in range(n)]
        for cp in remote:
            cp.start()
        for i in range(n):
            pltpu.make_async_remote_copy(
                src_ref=src[i], dst_ref=dst[i].at[1 - c], send_sem=send_sems.at[i], recv_sem=recv_sems.at[i],
                device_id=(x, y, 1 - c), device_id_type=MESH).wait_recv()
        for cp in remote:
            cp.wait_send()

    joined = pl.pallas_call(
        body, name=name,
        out_shape=[jax.ShapeDtypeStruct((2,) + h.shape, h.dtype) for h in halves],
        in_specs=[_HBM] * n, out_specs=[_HBM] * n,
        scratch_shapes=[pltpu.SemaphoreType.DMA((n,)), pltpu.SemaphoreType.DMA((n,))],
        compiler_params=_cp(),
    )(*halves)
    c = lax.axis_index("c")
    return [lax.dynamic_update_slice(j, h[None], (c, 0, 0)) for j, h in zip(joined, halves)]


def _pair_sum(g, got, name):
    _, _, hr, d = g.shape
    t = _tile(hr, 256, 16)

    def body(g0_ref, g1_ref, got_ref, wire_ref, own_ref):
        x, y, c = _mesh_pos()
        total = jnp.where(c == 0, g0_ref[0, 0], g1_ref[0, 0]) + got_ref[0]
        wire_ref[0] = total.astype(wire_ref.dtype)

        @pl.when(pl.program_id(1) == 2 * x + y)
        def _():
            own_ref[...] = total

    return pl.pallas_call(
        body, name=name, grid=(hr // t, N_SHARD),
        in_specs=[pl.BlockSpec((1, 1, t, d), lambda i, s: (s, 0, i, 0)), pl.BlockSpec((1, 1, t, d), lambda i, s: (s, 1, i, 0)),
                  pl.BlockSpec((1, t, d), lambda i, s: (s, i, 0))],
        out_specs=[pl.BlockSpec((1, t, d), lambda i, s: (s, i, 0)), pl.BlockSpec((t, d), lambda i, s: (i, 0))],
        out_shape=[jax.ShapeDtypeStruct((N_SHARD, hr, d), WIRE_DTYPE), jax.ShapeDtypeStruct((hr, d), F32)],
        compiler_params=_cp(("parallel", "arbitrary")),
    )(g, g, got)


def _sum_partials(own, recv, name):
    hr, d = own.shape
    t = _tile(hr, 256, 16)

    def body(own_ref, recv_ref, o_ref):
        total = own_ref[...]
        for j in range(3):
            total = total + recv_ref[j].astype(F32)
        o_ref[...] = total

    blk = pl.BlockSpec((t, d), lambda i: (i, 0))
    return pl.pallas_call(
        body, name=name, grid=(hr // t,), in_specs=[blk, pl.BlockSpec((3, t, d), lambda i: (0, i, 0))], out_specs=blk,
        out_shape=jax.ShapeDtypeStruct((hr, d), F32), compiler_params=_cp(("parallel",)),
    )(own, recv)


def _reduce_scatter_begin(grads, tag):
    split = [g.reshape(N_SHARD, 2, g.shape[0] // (2 * N_SHARD), g.shape[1]) for g in grads]
    got = _swap_halves(split, f"rs_swap_halves_{tag}")
    sums = [_pair_sum(g, h, f"rs_pair_sum_{tag}{i}") for i, (g, h) in enumerate(zip(split, got))]
    return [w for w, _ in sums], [own for _, own in sums]


def _reduce_scatter_end(own, recv, tag):
    halves = [_sum_partials(o, rv, f"rs_sum_{tag}{i}") for i, (o, rv) in enumerate(zip(own, recv))]
    return [j.reshape(-1, j.shape[-1]) for j in _join_halves(halves, f"rs_join_halves_{tag}")]


def _mod_fwd(x, ctx, nw, sc, sh, csc, csh):
    l, d = x.shape
    lc = ctx.shape[0]
    t = min(256, lc)
    nl, nc = l // t, lc // t

    def body(x_ref, c_ref, nw_ref, sc_ref, sh_ref, csc_ref, csh_ref, o_ref):
        i = pl.program_id(0)

        @pl.when(i < nl)
        def _():
            o_ref[...] = _f_mod(x_ref[...], nw_ref[...], sc_ref[...], sh_ref[...]).astype(o_ref.dtype)

        @pl.when(i >= nl)
        def _():
            o_ref[...] = _f_mod(c_ref[...], nw_ref[...], csc_ref[...], csh_ref[...]).astype(o_ref.dtype)

    return pl.pallas_call(
        body, name="mod_fwd", grid=(nl + nc,),
        in_specs=[pl.BlockSpec((t, d), lambda i: (jnp.minimum(i, nl - 1), 0)),
                  pl.BlockSpec((t, d), lambda i: (jnp.maximum(i - nl, 0), 0))] + [_bc_spec(d)] * 5,
        out_specs=pl.BlockSpec((t, d), lambda i: (i, 0)),
        out_shape=jax.ShapeDtypeStruct((l + lc, d), MXU_DTYPE), compiler_params=_cp(("arbitrary",)),
    )(x, ctx, nw, sc, sh, csc, csh)


def _mod_bwd(x, ctx, nw, sc, sh, csc, csh, dhx, dx_res):
    l, d = x.shape
    lc = ctx.shape[0]
    t = min(256, lc)
    nl, nc = l // t, lc // t

    def body(x_ref, c_ref, nw_ref, sc_ref, sh_ref, csc_ref, csh_ref, dh_ref, dr_ref,
             dx_ref, dnw_ref, dsc_ref, dsh_ref, dcsc_ref, dcsh_ref):
        i = pl.program_id(0)

        @pl.when(i == 0)
        def _():
            for r in (dnw_ref, dsc_ref, dsh_ref, dcsc_ref, dcsh_ref):
                r[...] = jnp.zeros_like(r)

        @pl.when(i < nl)
        def _():
            _, vjp = jax.vjp(_f_mod, x_ref[...], nw_ref[...], sc_ref[...], sh_ref[...])
            dx, dnw, dsc, dsh = vjp(dh_ref[...])
            dx_ref[...] = dx + dr_ref[...]
            dnw_ref[...] += dnw
            dsc_ref[...] += dsc
            dsh_ref[...] += dsh

        @pl.when(i >= nl)
        def _():
            _, vjp = jax.vjp(_f_mod, c_ref[...], nw_ref[...], csc_ref[...], csh_ref[...])
            _, dnw, dsc, dsh = vjp(dh_ref[...])
            dnw_ref[...] += dnw
            dcsc_ref[...] += dsc
            dcsh_ref[...] += dsh

    lat = pl.BlockSpec((t, d), lambda i: (jnp.minimum(i, nl - 1), 0))
    vec = jax.ShapeDtypeStruct((1, d), F32)
    return pl.pallas_call(
        body, name="mod_bwd", grid=(nl + nc,),
        in_specs=[lat, pl.BlockSpec((t, d), lambda i: (jnp.maximum(i - nl, 0), 0))] + [_bc_spec(d)] * 5
        + [pl.BlockSpec((t, d), lambda i: (i, 0)), lat],
        out_specs=[lat] + [_bc_spec(d)] * 5,
        out_shape=[jax.ShapeDtypeStruct((l, d), F32)] + [vec] * 5, compiler_params=_cp(("arbitrary",)),
    )(x, ctx, nw, sc, sh, csc, csh, dhx, dx_res)


def _gate_fwd(yf, yr, xbc, proj, dsk, wn, l, di):
    t = 128

    def body(yf_ref, yr_ref, xs_ref, z_ref, dsk_ref, wn_ref, o_ref):
        o_ref[...] = _f_gate(yf_ref[...], yr_ref[...], xs_ref[...], z_ref[...], dsk_ref[...], wn_ref[...]).astype(o_ref.dtype)

    row = pl.BlockSpec((t, di), lambda i: (i, 0))
    return pl.pallas_call(
        body, name="gate_fwd", grid=(l // t,), in_specs=[row] * 4 + [_bc_spec(di)] * 2, out_specs=row,
        out_shape=jax.ShapeDtypeStruct((l, di), MXU_DTYPE), compiler_params=_cp(("parallel",)),
    )(yf, yr, xbc, proj, dsk, wn)


def _gate_bwd(yf, yr, xbc, proj, dsk, wn, dya, l, lc, di):
    t = 128
    nl, nc = l // t, lc // t

    def body(yf_ref, yr_ref, xs_ref, z_ref, dsk_ref, wn_ref, g_ref, dy_ref, dz_ref, ddsk_ref, dwn_ref):
        i = pl.program_id(0)

        @pl.when(i == 0)
        def _():
            ddsk_ref[...] = jnp.zeros_like(ddsk_ref)
            dwn_ref[...] = jnp.zeros_like(dwn_ref)

        @pl.when(i < nl)
        def _():
            _, vjp = jax.vjp(_f_gate, yf_ref[...], yr_ref[...], xs_ref[...], z_ref[...], dsk_ref[...], wn_ref[...])
            dyf, _, _, dz, ddsk, dwn = vjp(g_ref[...])
            dy_ref[...] = dyf
            dz_ref[...] = dz.astype(dz_ref.dtype)
            ddsk_ref[...] += ddsk
            dwn_ref[...] += dwn

        @pl.when(i >= nl)
        def _():
            dz_ref[...] = jnp.zeros_like(dz_ref)

    lat = pl.BlockSpec((t, di), lambda i: (jnp.minimum(i, nl - 1), 0))
    vec = jax.ShapeDtypeStruct((1, di), F32)
    return pl.pallas_call(
        body, name="gate_bwd", grid=(nl + nc,),
        in_specs=[lat] * 4 + [_bc_spec(di)] * 2 + [lat],
        out_specs=[lat, pl.BlockSpec((t, di), lambda i: (i, 0))] + [_bc_spec(di)] * 2,
        out_shape=[jax.ShapeDtypeStruct((l, di), F32), jax.ShapeDtypeStruct((l + lc, di), MXU_DTYPE)] + [vec] * 2,
        compiler_params=_cp(("arbitrary",)),
    )(yf, yr, xbc, proj, dsk, wn, dya)


def _ln_fwd(cv, g, b):
    l, d = cv.shape
    t = 256

    def body(cv_ref, g_ref, b_ref, o_ref):
        o_ref[...] = _f_ln(cv_ref[...], g_ref[...], b_ref[...]).astype(o_ref.dtype)

    row = pl.BlockSpec((t, d), lambda i: (i, 0))
    return pl.pallas_call(
        body, name="ln_fwd", grid=(l // t,), in_specs=[row] + [_bc_spec(d)] * 2, out_specs=row,
        out_shape=jax.ShapeDtypeStruct((l, d), MXU_DTYPE), compiler_params=_cp(("parallel",)),
    )(cv, g, b)


def _ln_bwd(cv, g, b, dcf):
    l, d = cv.shape
    t = 256

    def body(cv_ref, g_ref, b_ref, dcf_ref, dcv_ref, dg_ref, db_ref):
        _, vjp = jax.vjp(_f_ln, cv_ref[...], g_ref[...], b_ref[...])
        dcv, dg, db = vjp(dcf_ref[...])
        dcv_ref[...] = dcv
        i = pl.program_id(0)
        _acc(dg_ref, dg, i)
        _acc(db_ref, db, i)

    row = pl.BlockSpec((t, d), lambda i: (i, 0))
    vec = jax.ShapeDtypeStruct((1, d), F32)
    return pl.pallas_call(
        body, name="ln_bwd", grid=(l // t,), in_specs=[row] + [_bc_spec(d)] * 2 + [row],
        out_specs=[row] + [_bc_spec(d)] * 2, out_shape=[jax.ShapeDtypeStruct((l, d), F32), vec, vec],
        compiler_params=_cp(("arbitrary",)),
    )(cv, g, b, dcf)


def _merge_fwd(ya, yb, proj, ga_blk):
    l, d = ya.shape
    t = 256

    def body(ya_ref, yb_ref, ga_ref, gb_ref, o_ref):
        o_ref[...] = _f_merge(ya_ref[...], yb_ref[...], ga_ref[...], gb_ref[...]).astype(o_ref.dtype)

    row = pl.BlockSpec((t, d), lambda i: (i, 0))
    return pl.pallas_call(
        body, name="merge_fwd", grid=(l // t,),
        in_specs=[row, row, pl.BlockSpec((t, d), lambda i: (i, ga_blk)), pl.BlockSpec((t, d), lambda i: (i, ga_blk + 1))],
        out_specs=row, out_shape=jax.ShapeDtypeStruct((l, d), MXU_DTYPE), compiler_params=_cp(("parallel",)),
    )(ya, yb, proj, proj)


def _merge_bwd(ya, yb, proj, ga_blk, dmerged, lc):
    l, d = ya.shape
    t = min(256, lc)
    nl, nc = l // t, lc // t

    def body(ya_ref, yb_ref, ga_ref, gb_ref, g_ref, dya_ref, dyb_ref, dga_ref, dgb_ref):
        i = pl.program_id(0)

        @pl.when(i < nl)
        def _():
            _, vjp = jax.vjp(_f_merge, ya_ref[...], yb_ref[...], ga_ref[...], gb_ref[...])
            dya, dyb, dga, dgb = vjp(g_ref[...])
            dya_ref[...] = dya.astype(dya_ref.dtype)
            dyb_ref[...] = dyb.astype(dyb_ref.dtype)
            dga_ref[...] = dga.astype(dga_ref.dtype)
            dgb_ref[...] = dgb.astype(dgb_ref.dtype)

        @pl.when(i >= nl)
        def _():
            dga_ref[...] = jnp.zeros_like(dga_ref)
            dgb_ref[...] = jnp.zeros_like(dgb_ref)

    lat = pl.BlockSpec((t, d), lambda i: (jnp.minimum(i, nl - 1), 0))
    full = pl.BlockSpec((t, d), lambda i: (i, 0))
    return pl.pallas_call(
        body, name="merge_bwd", grid=(nl + nc,),
        in_specs=[lat, lat, pl.BlockSpec((t, d), lambda i: (jnp.minimum(i, nl - 1), ga_blk)),
                  pl.BlockSpec((t, d), lambda i: (jnp.minimum(i, nl - 1), ga_blk + 1)), lat],
        out_specs=[lat, lat, full, full],
        out_shape=[jax.ShapeDtypeStruct((l, d), MXU_DTYPE)] * 2 + [jax.ShapeDtypeStruct((l + lc, d), MXU_DTYPE)] * 2,
        compiler_params=_cp(("arbitrary",)),
    )(ya, yb, proj, proj, dmerged)


def _res_fwd(x, mix, g1, wn, sc2, sh2):
    l, d = x.shape
    t = 256

    def body(x_ref, m_ref, g1_ref, wn_ref, sc_ref, sh_ref, x1_ref, hx_ref):
        x1, hx = _f_res(x_ref[...], m_ref[...], g1_ref[...], wn_ref[...], sc_ref[...], sh_ref[...])
        x1_ref[...] = x1
        hx_ref[...] = hx.astype(hx_ref.dtype)

    row = pl.BlockSpec((t, d), lambda i: (i, 0))
    return pl.pallas_call(
        body, name="res_fwd", grid=(l // t,), in_specs=[row, row] + [_bc_spec(d)] * 4, out_specs=[row, row],
        out_shape=[jax.ShapeDtypeStruct((l, d), F32), jax.ShapeDtypeStruct((l, d), MXU_DTYPE)],
        compiler_params=_cp(("parallel",)),
    )(x, mix, g1, wn, sc2, sh2)


def _res_bwd(x, mix, g1, wn, sc2, sh2, dx1, dhx2):
    l, d = x.shape
    t = 256

    def body(x_ref, m_ref, g1_ref, wn_ref, sc_ref, sh_ref, dx1_ref, dh_ref, dx_ref, dm_ref, dg1_ref, dwn_ref, dsc_ref, dsh_ref):
        _, vjp = jax.vjp(_f_res, x_ref[...], m_ref[...], g1_ref[...], wn_ref[...], sc_ref[...], sh_ref[...])
        dx, dm, dg1, dwn, dsc, dsh = vjp((dx1_ref[...], dh_ref[...]))
        dx_ref[...] = dx
        dm_ref[...] = dm.astype(dm_ref.dtype)
        i = pl.program_id(0)
        _acc(dg1_ref, dg1, i)
        _acc(dwn_ref, dwn, i)
        _acc(dsc_ref, dsc, i)
        _acc(dsh_ref, dsh, i)

    row = pl.BlockSpec((t, d), lambda i: (i, 0))
    vec = jax.ShapeDtypeStruct((1, d), F32)
    return pl.pallas_call(
        body, name="res_bwd", grid=(l // t,), in_specs=[row, row] + [_bc_spec(d)] * 4 + [row, row],
        out_specs=[row, row] + [_bc_spec(d)] * 4,
        out_shape=[jax.ShapeDtypeStruct((l, d), F32), jax.ShapeDtypeStruct((l, d), MXU_DTYPE)] + [vec] * 4,
        compiler_params=_cp(("arbitrary",)),
    )(x, mix, g1, wn, sc2, sh2, dx1, dhx2)


def _swiglu_fwd(gu, df):
    l = gu.shape[0]
    r = df // N_SHARD
    t = 256

    def body(gu_ref, o_ref):
        o_ref[...] = _f_swiglu(gu_ref[:, :r], gu_ref[:, r:]).astype(o_ref.dtype)

    return pl.pallas_call(
        body, name="swiglu_fwd", grid=(l // t, N_SHARD),
        in_specs=[pl.BlockSpec((t, 2 * r), lambda i, s: (i, s))], out_specs=pl.BlockSpec((t, r), lambda i, s: (i, s)),
        out_shape=jax.ShapeDtypeStruct((l, df), MXU_DTYPE), compiler_params=_cp(("parallel", "parallel")),
    )(gu)


def _swiglu_bwd(gu, dact, df):
    l = gu.shape[0]
    r = df // N_SHARD
    t = 256

    def body(gu_ref, da_ref, dgu_ref):
        _, vjp = jax.vjp(_f_swiglu, gu_ref[:, :r], gu_ref[:, r:])
        dg, du = vjp(da_ref[...])
        dgu_ref[:, :r] = dg.astype(dgu_ref.dtype)
        dgu_ref[:, r:] = du.astype(dgu_ref.dtype)

    pair = pl.BlockSpec((t, 2 * r), lambda i, s: (i, s))
    return pl.pallas_call(
        body, name="swiglu_bwd", grid=(l // t, N_SHARD), in_specs=[pair, pl.BlockSpec((t, r), lambda i, s: (i, s))],
        out_specs=pair, out_shape=jax.ShapeDtypeStruct((l, 2 * df), MXU_DTYPE),
        compiler_params=_cp(("parallel", "parallel")),
    )(gu, dact)


def _loss_and_grads(x1, dn, g2, wn, tgt):
    l, d = x1.shape
    t = 256

    def body(x1_ref, dn_ref, g2_ref, wn_ref, t_ref, loss_ref, dx_ref, ddn_ref, dg2_ref, dwn_ref):
        loss, vjp = jax.vjp(lambda a, b, c, e: _f_loss(a, b, c, e, t_ref[...]), x1_ref[...], dn_ref[...], g2_ref[...], wn_ref[...])
        dx, ddn, dg2, dwn = vjp(jnp.ones((1, 1), F32))
        dx_ref[...] = dx
        ddn_ref[...] = ddn.astype(ddn_ref.dtype)
        i = pl.program_id(0)
        _acc(loss_ref, loss, i)
        _acc(dg2_ref, dg2, i)
        _acc(dwn_ref, dwn, i)

    row = pl.BlockSpec((t, d), lambda i: (i, 0))
    vec = jax.ShapeDtypeStruct((1, d), F32)
    return pl.pallas_call(
        body, name="loss_and_grads", grid=(l // t,), in_specs=[row, row] + [_bc_spec(d)] * 2 + [row],
        out_specs=[pl.BlockSpec((1, 1), lambda i: (0, 0)), row, row] + [_bc_spec(d)] * 2,
        out_shape=[jax.ShapeDtypeStruct((1, 1), F32), jax.ShapeDtypeStruct((l, d), F32),
                   jax.ShapeDtypeStruct((l, d), MXU_DTYPE), vec, vec],
        compiler_params=_cp(("arbitrary",)),
    )(x1, dn, g2, wn, tgt)


PAD = 8


def _conv5_taps(s_ref, w_ref, l, lc, width):
    half = width // 2
    lat = sum(w_ref[k:k + 1, :] * s_ref[pl.ds(PAD + k - half, l), :] for k in range(width))
    ctx = sum(w_ref[k:k + 1, :] * s_ref[pl.ds(2 * PAD + l + k - half, lc), :] for k in range(width))
    return lat, ctx


def _fill_padded(s_ref, lat, ctx, l, lc):
    zeros = jnp.zeros((PAD, s_ref.shape[1]), F32)
    s_ref[pl.ds(0, PAD), :] = zeros
    s_ref[pl.ds(PAD, l), :] = lat
    s_ref[pl.ds(PAD + l, PAD), :] = zeros
    s_ref[pl.ds(2 * PAD + l, lc), :] = ctx
    s_ref[pl.ds(2 * PAD + l + lc, PAD), :] = zeros


def _conv5_fwd(proj, w, b, l, lc, col0, ncols):
    t_all = l + lc
    cw = LANES
    blk0 = col0 // cw
    width = w.shape[0]

    def body(x_ref, w_ref, b_ref, o_ref, s_ref):
        _fill_padded(s_ref, x_ref[pl.ds(0, l), :], x_ref[pl.ds(l, lc), :], l, lc)
        lat, ctx = _conv5_taps(s_ref, w_ref, l, lc, width)
        o_ref[pl.ds(0, l), :] = _silu(lat + b_ref[...])
        o_ref[pl.ds(l, lc), :] = _silu(ctx + b_ref[...])

    return pl.pallas_call(
        body, name="conv5_fwd", grid=(ncols // cw,),
        in_specs=[pl.BlockSpec((t_all, cw), lambda j: (0, blk0 + j)), pl.BlockSpec((width, cw), lambda j: (0, j)),
                  pl.BlockSpec((1, cw), lambda j: (0, j))],
        out_specs=pl.BlockSpec((t_all, cw), lambda j: (0, j)),
        out_shape=jax.ShapeDtypeStruct((t_all, ncols), F32),
        scratch_shapes=[pltpu.VMEM((t_all + 3 * PAD, cw), F32)], compiler_params=_cp(("parallel",)),
    )(proj, w, b)


def _conv5_bwd(proj, w, b, cots, l, lc, col0, seg0, ncols):
    t_all = l + lc
    cw = LANES
    blk0, sblk0 = col0 // cw, seg0 // cw
    width = w.shape[0]
    half = width // 2
    nc = len(cots)

    def body(*refs):
        x_ref, w_ref, b_ref = refs[:3]
        cot_refs = refs[3:3 + nc]
        dx_ref, dw_ref, db_ref, s_ref = refs[3 + nc:]
        x_lat, x_ctx = x_ref[pl.ds(0, l), :], x_ref[pl.ds(l, lc), :]
        _fill_padded(s_ref, x_lat, x_ctx, l, lc)
        pre_lat, pre_ctx = _conv5_taps(s_ref, w_ref, l, lc, width)
        g = sum(c[...] for c in cot_refs)

        def through_silu(pre, cot):
            _, vjp = jax.vjp(_silu, pre + b_ref[...])
            return vjp(cot)[0]

        d_lat = through_silu(pre_lat, g[:l])
        d_ctx = through_silu(pre_ctx, g[l:])
        db_ref[...] = jnp.sum(d_lat, axis=0, keepdims=True) + jnp.sum(d_ctx, axis=0, keepdims=True)
        for k in range(width):
            dw_ref[k:k + 1, :] = (
                jnp.sum(d_lat * s_ref[pl.ds(PAD + k - half, l), :], axis=0, keepdims=True)
                + jnp.sum(d_ctx * s_ref[pl.ds(2 * PAD + l + k - half, lc), :], axis=0, keepdims=True))
        _fill_padded(s_ref, d_lat, d_ctx, l, lc)
        dx_lat = sum(w_ref[k:k + 1, :] * s_ref[pl.ds(PAD - (k - half), l), :] for k in range(width))
        dx_ctx = sum(w_ref[k:k + 1, :] * s_ref[pl.ds(2 * PAD + l - (k - half), lc), :] for k in range(width))
        dx_ref[pl.ds(0, l), :] = dx_lat.astype(dx_ref.dtype)
        dx_ref[pl.ds(l, lc), :] = dx_ctx.astype(dx_ref.dtype)

    col = pl.BlockSpec((t_all, cw), lambda j: (0, j))
    return pl.pallas_call(
        body, name=f"conv5_bwd_{seg0}", grid=(ncols // cw,),
        in_specs=[pl.BlockSpec((t_all, cw), lambda j: (0, blk0 + sblk0 + j)),
                  pl.BlockSpec((width, cw), lambda j: (0, sblk0 + j)), pl.BlockSpec((1, cw), lambda j: (0, sblk0 + j))]
        + [col] * nc,
        out_specs=[col, pl.BlockSpec((width, cw), lambda j: (0, j)), pl.BlockSpec((1, cw), lambda j: (0, j))],
        out_shape=[jax.ShapeDtypeStruct((t_all, ncols), MXU_DTYPE), jax.ShapeDtypeStruct((width, ncols), F32),
                   jax.ShapeDtypeStruct((1, ncols), F32)],
        scratch_shapes=[pltpu.VMEM((t_all + 3 * PAD, cw), F32)], compiler_params=_cp(("parallel",)),
    )(proj, w, b, *cots)


def _conv31_fwd(proj, w, b, l, d, u_blk):
    cw = LANES
    width = w.shape[0]
    reach = (width // 2) * GRID_W
    nb = d // cw

    def body(u_ref, v_ref, w_ref, b_ref, o_ref, s_ref):
        s_ref[pl.ds(0, reach), :] = jnp.zeros((reach, cw), F32)
        s_ref[pl.ds(reach, l), :] = u_ref[...] * jax.nn.sigmoid(v_ref[...])
        s_ref[pl.ds(reach + l, reach), :] = jnp.zeros((reach, cw), F32)
        o_ref[...] = sum(w_ref[k:k + 1, :] * s_ref[pl.ds(k * GRID_W, l), :] for k in range(width)) + b_ref[...]

    return pl.pallas_call(
        body, name="conv31_fwd", grid=(nb,),
        in_specs=[pl.BlockSpec((l, cw), lambda j: (0, u_blk * nb + j)), pl.BlockSpec((l, cw), lambda j: (0, (u_blk + 1) * nb + j)),
                  pl.BlockSpec((width, cw), lambda j: (0, j)), pl.BlockSpec((1, cw), lambda j: (0, j))],
        out_specs=pl.BlockSpec((l, cw), lambda j: (0, j)), out_shape=jax.ShapeDtypeStruct((l, d), F32),
        scratch_shapes=[pltpu.VMEM((l + 2 * reach, cw), F32)], compiler_params=_cp(("parallel",)),
    )(proj, proj, w, b)


def _conv31_bwd(proj, w, dcv, l, lc, d, u_blk):
    cw = LANES
    width = w.shape[0]
    reach = (width // 2) * GRID_W
    nb = d // cw
    t_all = l + lc

    def body(u_ref, v_ref, w_ref, g_ref, du_ref, dv_ref, dw_ref, db_ref, s_ref):
        zeros = jnp.zeros((reach, cw), F32)
        s_ref[pl.ds(0, reach), :] = zeros
        s_ref[pl.ds(reach + l, reach), :] = zeros
        u, v, g = u_ref[...], v_ref[...], g_ref[...]
        s_ref[pl.ds(reach, l), :] = u * jax.nn.sigmoid(v)
        db_ref[...] = jnp.sum(g, axis=0, keepdims=True)
        for k in range(width):
            dw_ref[k:k + 1, :] = jnp.sum(g * s_ref[pl.ds(k * GRID_W, l), :], axis=0, keepdims=True)
        s_ref[pl.ds(reach, l), :] = g
        dt = sum(w_ref[k:k + 1, :] * s_ref[pl.ds((width - 1 - k) * GRID_W, l), :] for k in range(width))
        _, vjp = jax.vjp(lambda a, c: a * jax.nn.sigmoid(c), u, v)
        du, dv = vjp(dt)
        du_ref[pl.ds(0, l), :] = du.astype(du_ref.dtype)
        dv_ref[pl.ds(0, l), :] = dv.astype(dv_ref.dtype)
        du_ref[pl.ds(l, lc), :] = jnp.zeros((lc, cw), du_ref.dtype)
        dv_ref[pl.ds(l, lc), :] = jnp.zeros((lc, cw), dv_ref.dtype)

    pshape = jax.ShapeDtypeStruct((t_all, d), MXU_DTYPE)
    tall = pl.BlockSpec((t_all, cw), lambda j: (0, j))
    return pl.pallas_call(
        body, name="conv31_bwd", grid=(nb,),
        in_specs=[pl.BlockSpec((l, cw), lambda j: (0, u_blk * nb + j)), pl.BlockSpec((l, cw), lambda j: (0, (u_blk + 1) * nb + j)),
                  pl.BlockSpec((width, cw), lambda j: (0, j)), pl.BlockSpec((l, cw), lambda j: (0, j))],
        out_specs=[tall, tall, pl.BlockSpec((width, cw), lambda j: (0, j)), pl.BlockSpec((1, cw), lambda j: (0, j))],
        out_shape=[pshape, pshape, jax.ShapeDtypeStruct((width, d), F32), jax.ShapeDtypeStruct((1, d), F32)],
        scratch_shapes=[pltpu.VMEM((l + 2 * reach, cw), F32)], compiler_params=_cp(("parallel",)),
    )(proj, proj, w, dcv)


def _softplus(x):
    return jnp.maximum(x, 0.0) + jnp.log(1.0 + jnp.exp(-jnp.abs(x)))


def _dt_fwd(proj, bias, a, dt_blk):
    t_all = proj.shape[0]
    hh = bias.shape[1]
    q = CHUNK

    def body(r_ref, b_ref, a_ref, dt_ref, cs_ref, tc_ref, cst_ref):
        dt = _softplus(r_ref[...] + b_ref[...])
        dt_ref[...] = dt
        da = dt * a_ref[...]
        li, si = _iota((q, q), 0), _iota((q, q), 1)
        reverse_cols = _iota((q, hh), 1) >= hh // 2
        cs = jnp.where(reverse_cols, _dot((si >= li).astype(F32), da, exact=True), _dot((si <= li).astype(F32), da, exact=True))
        cs_ref[...] = cs
        cst_ref[...] = cs.T
        total = jnp.where(_iota((1, hh), 1) >= hh // 2, cs_ref[0:1, :], cs_ref[q - 1:q, :])
        tc_ref[...] = total - cs

    row = pl.BlockSpec((q, hh), lambda i: (i, 0))
    shape = jax.ShapeDtypeStruct((t_all, hh), F32)
    return pl.pallas_call(
        body, name="dt_fwd", grid=(t_all // q,),
        in_specs=[pl.BlockSpec((q, hh), lambda i: (i, dt_blk)), _bc_spec(hh), _bc_spec(hh)],
        out_specs=[row, row, row, pl.BlockSpec((hh, q), lambda i: (0, i))],
        out_shape=[shape, shape, shape, jax.ShapeDtypeStruct((hh, t_all), F32)],
        compiler_params=_cp(("parallel",)),
    )(proj, bias, a)


def _three_way(x):
    def top(v):
        word = lax.bitcast_convert_type(v, jnp.uint32) & jnp.uint32(0xFFFF0000)
        return lax.bitcast_convert_type(word, F32)

    hi = top(x)
    rest = x - hi
    mid = top(rest)
    return hi.astype(jnp.bfloat16), mid.astype(jnp.bfloat16), (rest - mid).astype(jnp.bfloat16)


def _scan_columns(dt, cs, tc, groups2, hpg):
    t_all = dt.shape[0]
    parts = [part.reshape(t_all, groups2, 1, hpg) for arr in (dt, cs, tc) for part in _three_way(arr)]
    cols = jnp.concatenate(parts, axis=2).transpose(1, 0, 2, 3).reshape(groups2, t_all, 9 * hpg)
    return jnp.pad(cols, ((0, 0), (0, 0), (0, LANES - 9 * hpg)))


def _dt_bwd(proj, bias, dt, ddt, dda, dt_blk):
    t_all = proj.shape[0]
    hh = bias.shape[1]
    q = _tile(t_all, 1024, LANES)

    def body(r_ref, b_ref, dt_ref, ddt_ref, dda_ref, dr_ref, db_ref, da_ref):
        dr = ddt_ref[...] * jax.nn.sigmoid(r_ref[...] + b_ref[...])
        dr_ref[...] = dr.astype(dr_ref.dtype)
        i = pl.program_id(0)
        _acc(db_ref, jnp.sum(dr, axis=0, keepdims=True), i)
        _acc(da_ref, jnp.sum(dda_ref[...] * dt_ref[...], axis=0, keepdims=True), i)

    row = pl.BlockSpec((q, hh), lambda i: (i, 0))
    vec = jax.ShapeDtypeStruct((1, hh), F32)
    return pl.pallas_call(
        body, name="dt_bwd", grid=(t_all // q,),
        in_specs=[pl.BlockSpec((q, hh), lambda i: (i, dt_blk)), _bc_spec(hh), row, row, row],
        out_specs=[row, _bc_spec(hh), _bc_spec(hh)],
        out_shape=[jax.ShapeDtypeStruct((t_all, hh), MXU_DTYPE), vec, vec], compiler_params=_cp(("arbitrary",)),
    )(proj, bias, dt, ddt, dda)


_NT = (((1,), (1,)), ((), ()))
_TN = (((0,), (0,)), ((), ()))


def _dot(a, b, dims=None, exact=False):
    kw = dict(preferred_element_type=F32)
    if exact:
        kw["precision"] = HI
    if dims is None:
        return jnp.dot(a, b, **kw)
    return lax.dot_general(a, b, dims, **kw)


def _iota(shape, dim):
    return lax.broadcasted_iota(jnp.int32, shape, dim)


class _Ssd:
    def __init__(self, l, lc, di, p, reverse):
        self.q, self.n, self.g = CHUNK, SSM_STATE, SSM_GROUPS
        self.nl, self.ncx = l // CHUNK, lc // CHUNK
        self.ns = self.nl + self.ncx
        self.t_all, self.di, self.p, self.reverse = l + lc, di, p, reverse
        self.hpg = di // p // SSM_GROUPS
        self.gw = self.hpg * p
        self.ntile = self.gw // LANES
        self.hpt = LANES // p
        self.log2p = p.bit_length() - 1
        assert 1 << self.log2p == p and self.gw % LANES == 0 and self.n == LANES and self.q == LANES
        assert 9 * self.hpg <= LANES
        self.d = 1 if reverse else 0

    def chunk_at(self, step):
        if self.reverse:
            return self.ns - 1 - step
        return jnp.where(step < self.ncx, self.nl + step, step - self.ncx)

    def selectors(self):
        hpg = self.hpg
        k = jnp.arange(LANES)
        quantity, head, used = k // (3 * hpg), k % hpg, k < 9 * hpg
        lane_head = jnp.arange(LANES) // self.p
        tiles = jnp.concatenate([(used & (quantity == qo))[:, None] & (head[:, None] == tt * self.hpt + lane_head[None, :])
                                 for tt in range(self.ntile) for qo in range(3)], axis=1)
        heads = jnp.concatenate([jnp.broadcast_to((used & (quantity == 1) & (head == j))[:, None], (LANES, LANES))
                                 for j in range(hpg)], axis=1)
        return tiles.astype(jnp.bfloat16), heads.astype(jnp.bfloat16)

    def in_specs(self, chunk_of):
        g, n, hpg, q = self.g, self.n, self.hpg, self.q
        b_blk, c_blk = self.di // n, self.di // n + g
        d = self.d
        return [
            pl.BlockSpec((q, self.gw), lambda gi, i: (chunk_of(i), gi)),
            pl.BlockSpec((q, n), lambda gi, i: (chunk_of(i), b_blk + gi)),
            pl.BlockSpec((q, n), lambda gi, i: (chunk_of(i), c_blk + gi)),
            pl.BlockSpec((1, q, LANES), lambda gi, i: (d * g + gi, chunk_of(i), 0)),
            pl.BlockSpec((hpg, q), lambda gi, i: (d * g + gi, chunk_of(i))),
            pl.BlockSpec((LANES, self.ntile * 3 * LANES), lambda gi, i: (0, 0)),
            pl.BlockSpec((LANES, hpg * LANES), lambda gi, i: (0, 0)),
        ]

    def masks(self):
        li, si = _iota((self.q, self.q), 0), _iota((self.q, self.q), 1)
        if self.reverse:
            return si >= li, li >= si
        return si <= li, li <= si

    def spread(self, spread_all, tt):
        at = 3 * LANES * tt
        return tuple(spread_all[:, at + k * LANES:at + (k + 1) * LANES] for k in range(3))

    def head_lanes(self, qq):
        return lax.shift_right_logical(_iota((self.q, LANES), 1), self.log2p) == qq

    def head_sums(self, values, tt):
        sel = _iota((HEAD_COLS, LANES), 0) == tt * self.hpt + lax.shift_right_logical(_iota((HEAD_COLS, LANES), 1), self.log2p)
        parts = [part for v in values for part in _three_way(v)]
        sums = _dot(jnp.concatenate(parts, axis=0), sel.astype(jnp.bfloat16), _NT)
        out, at = [], 0
        for v in values:
            rows = v.shape[0]
            out.append(sums[at:at + rows] + sums[at + rows:at + 2 * rows] + sums[at + 2 * rows:at + 3 * rows])
            at += 3 * rows
        return out

    def state_scale(self, csr_ref):
        last = 0 if self.reverse else self.q - 1
        total = jnp.sum(jnp.where(_iota((self.hpg, self.q), 1) == last, csr_ref[...], 0.0), axis=1, keepdims=True)
        decay = jnp.broadcast_to(jnp.exp(total), (self.hpg, self.n))
        decay = jnp.concatenate([decay, jnp.zeros((HEAD_COLS - self.hpg, self.n), F32)], axis=0)
        rows = lax.shift_right_logical(_iota((self.gw, HEAD_COLS), 0), self.log2p) == _iota((self.gw, HEAD_COLS), 1)
        return _dot_parts(rows.astype(jnp.bfloat16), decay)


def _dot_parts(sel, v, dims=None):
    return sum(_dot(sel, part, dims) for part in _three_way(v))


def _ssd_fwd(xbc, cols, cs_t, l, lc, di, p, reverse):
    s = _Ssd(l, lc, di, p, reverse)
    q, n, gw = s.q, s.n, s.gw
    neg_inf = float("-inf")

    def body(xs_ref, b_ref, c_ref, cols_ref, csr_ref, et_ref, eh_ref, y_ref, hp_ref, h_scr):
        i = pl.program_id(1)

        @pl.when(i == 0)
        def _():
            h_scr[...] = jnp.zeros_like(h_scr)

        h = h_scr[...]
        hp_ref[0, 0] = h
        mask, _ = s.masks()
        cols = cols_ref[0]
        bb, cb = b_ref[...].astype(MXU_DTYPE), c_ref[...].astype(MXU_DTYPE)
        cbt = _dot(cb, bb, _NT)
        y_off = _dot(cb, h.astype(MXU_DTYPE), _NT)
        spread_all, cs_heads = _dot(cols, et_ref[...]), _dot(cols, eh_ref[...])
        w_tiles = []
        for tt in range(s.ntile):
            sl = slice(tt * LANES, (tt + 1) * LANES)
            dt_b, cs_b, tc_b = s.spread(spread_all, tt)
            x = xs_ref[:, sl] * dt_b
            ms, xhs = [], []
            for qq in range(s.hpt):
                j = tt * s.hpt + qq
                seg = cs_heads[:, j * LANES:(j + 1) * LANES] - csr_ref[j:j + 1, :]
                ms.append((cbt * jnp.exp(jnp.where(mask, seg, neg_inf))).astype(MXU_DTYPE))
                xhs.append(jnp.where(s.head_lanes(qq), x, 0.0).astype(MXU_DTYPE))
            yd = _dot(jnp.concatenate(ms, axis=1), jnp.concatenate(xhs, axis=0))
            y_ref[:, sl] = yd + y_off[:, sl] * jnp.exp(cs_b)
            w_tiles.append((x * jnp.exp(tc_b)).astype(MXU_DTYPE))
        wm = w_tiles[0] if s.ntile == 1 else jnp.concatenate(w_tiles, axis=1)
        h_scr[...] = h * s.state_scale(csr_ref) + _dot(wm, bb, _TN)

    d = "rev" if reverse else "fwd"
    e_tiles, e_heads = s.selectors()
    return pl.pallas_call(
        body, name=f"ssd_{d}", grid=(s.g, s.ns), in_specs=s.in_specs(s.chunk_at),
        out_specs=[pl.BlockSpec((q, gw), lambda gi, i: (s.chunk_at(i), gi)),
                   pl.BlockSpec((1, 1, gw, n), lambda gi, i: (i, gi, 0, 0))],
        out_shape=[jax.ShapeDtypeStruct((s.t_all, di), F32), jax.ShapeDtypeStruct((s.ns, s.g, gw, n), F32)],
        scratch_shapes=[pltpu.VMEM((gw, n), F32)],
        compiler_params=_cp(("parallel", "arbitrary")),
    )(xbc, xbc, xbc, cols, cs_t, e_tiles, e_heads)


def _ssd_bwd(xbc, cols, cs_t, a_cols, dy, hprev, l, lc, di, p, reverse, dsk=None, prev=None):
    s = _Ssd(l, lc, di, p, reverse)
    q, n, gw, hpg = s.q, s.n, s.gw, s.hpg
    neg_inf = float("-inf")
    n_extra = (dsk is not None) + (3 if prev is not None else 0)

    def chunk_of(i):
        return s.chunk_at(s.ns - 1 - i)

    def body(xs_ref, b_ref, c_ref, cols_ref, csr_ref, et_ref, eh_ref, ac_ref, dy_ref, hp_ref, *rest):
        extra, (dxs_ref, db_ref, dc_ref, ddt_ref, dda_ref, dh_scr) = rest[:n_extra], rest[n_extra:]
        dsk_ref = extra[0] if dsk is not None else None
        prev_refs = extra[-3:] if prev is not None else None
        i = pl.program_id(1)

        @pl.when(i == 0)
        def _():
            dh_scr[...] = jnp.zeros_like(dh_scr)

        latent = (chunk_of(i) < s.nl).astype(F32)
        h, dh = hp_ref[0, 0], dh_scr[...]
        hb, dhb = h.astype(MXU_DTYPE), dh.astype(MXU_DTYPE)
        mask, mask_t = s.masks()
        cols = cols_ref[0]
        bb, cb = b_ref[...].astype(MXU_DTYPE), c_ref[...].astype(MXU_DTYPE)
        cbt, bct = _dot(cb, bb, _NT), _dot(bb, cb, _NT)
        b_dh = _dot(bb, dhb, _NT)
        y_off0 = _dot(cb, hb, _NT)
        d_g, d_gt = jnp.zeros((q, q), F32), jnp.zeros((q, q), F32)
        dcs = jnp.zeros((q, HEAD_COLS), F32)
        ddt_x = jnp.zeros((q, HEAD_COLS), F32)
        r_state = jnp.zeros((16, HEAD_COLS), F32)
        spread_all, cs_heads = _dot(cols, et_ref[...]), _dot(cols, eh_ref[...])
        dye_tiles, xte_tiles = [], []
        for tt in range(s.ntile):
            sl = slice(tt * LANES, (tt + 1) * LANES)
            dt_b, cs_b, tc_b = s.spread(spread_all, tt)
            ecs_b, te_b = jnp.exp(cs_b), jnp.exp(tc_b)
            xs_t = xs_ref[:, sl]
            x = xs_t * dt_b
            d_y = dy_ref[:, sl] * latent
            dx_state = b_dh[:, sl] * te_b
            lms, lm_ts, m_ts, d_yhs, xhs = [], [], [], [], []
            for qq in range(s.hpt):
                j = tt * s.hpt + qq
                csc_b = cs_heads[:, j * LANES:(j + 1) * LANES]
                csr = csr_ref[j:j + 1, :]
                lms.append(jnp.exp(jnp.where(mask, csc_b - csr, neg_inf)))
                lm_ts.append(jnp.exp(jnp.where(mask_t, csr - csc_b, neg_inf)))
                m_ts.append(bct * lm_ts[-1])
                lanes = s.head_lanes(qq)
                d_yhs.append(jnp.where(lanes, d_y, 0.0).astype(MXU_DTYPE))
                xhs.append(jnp.where(lanes, x, 0.0).astype(MXU_DTYPE))
            d_yh_rows = jnp.concatenate(d_yhs, axis=0)
            d_m_all = _dot(d_yh_rows, x.astype(MXU_DTYPE), _NT)
            d_mt_all = _dot(jnp.concatenate(xhs, axis=0), d_y.astype(MXU_DTYPE), _NT)
            for qq in range(s.hpt):
                j = tt * s.hpt + qq
                d_m, d_mt = d_m_all[qq * q:(qq + 1) * q], d_mt_all[qq * q:(qq + 1) * q]
                r1 = jnp.sum(d_m * (cbt * lms[qq]), axis=1, keepdims=True)
                r2 = jnp.sum(d_mt * m_ts[qq], axis=1, keepdims=True)
                dcs = dcs + (r1 - r2) * (_iota((1, HEAD_COLS), 1) == j).astype(F32)
                d_g = d_g + d_m * lms[qq]
                d_gt = d_gt + d_mt * lm_ts[qq]
            d_x = _dot(jnp.concatenate([m.astype(MXU_DTYPE) for m in m_ts], axis=1), d_yh_rows) + dx_state
            d_xs = d_x * dt_b
            if dsk_ref is not None:
                d_xs = d_xs + d_y * dsk_ref[:, sl]
            if prev_refs is not None:
                d_xs = d_xs + prev_refs[0][:, sl]
            dxs_ref[:, sl] = d_xs
            fed = x * dx_state
            fed_rows = jnp.broadcast_to(jnp.sum(fed, axis=0, keepdims=True), (16, LANES))
            sums = s.head_sums([d_x * xs_t, d_y * y_off0[:, sl] * ecs_b - fed, fed_rows], tt)
            ddt_x, dcs, r_state = ddt_x + sums[0], dcs + sums[1], r_state + sums[2]
            dye_tiles.append((d_y * ecs_b).astype(MXU_DTYPE))
            xte_tiles.append((x * te_b).astype(MXU_DTYPE))
        dye = dye_tiles[0] if s.ntile == 1 else jnp.concatenate(dye_tiles, axis=1)
        xte = xte_tiles[0] if s.ntile == 1 else jnp.concatenate(xte_tiles, axis=1)
        d_c = _dot(d_g.astype(MXU_DTYPE), bb) + _dot(dye, hb)
        d_b = _dot(d_gt.astype(MXU_DTYPE), cb) + _dot(xte, dhb)
        if prev_refs is not None:
            d_b, d_c = d_b + prev_refs[1][...], d_c + prev_refs[2][...]
        dc_ref[...] = d_c
        db_ref[...] = d_b
        scale = s.state_scale(csr_ref)
        carried = dh * h * scale
        d_tot = jnp.sum(r_state, axis=0, keepdims=True) * 0.0625
        for j in range(hpg):
            part = jnp.sum(carried[j * p:(j + 1) * p, :], axis=0, keepdims=True)
            d_tot = d_tot + jnp.sum(part, axis=1, keepdims=True) * (_iota((1, HEAD_COLS), 1) == j).astype(F32)
        dda = _dot_parts(mask_t.astype(jnp.bfloat16), dcs) + d_tot
        ddt_ref[0] = ddt_x + dda * ac_ref[0]
        dda_ref[0] = dda
        dh_scr[...] = dh * scale + _dot(dye, cb, _TN)

    d = "rev" if reverse else "fwd"
    e_tiles, e_heads = s.selectors()
    col = pl.BlockSpec((1, q, HEAD_COLS), lambda gi, i: (gi, chunk_of(i), 0))
    gn = pl.BlockSpec((q, n), lambda gi, i: (chunk_of(i), gi))
    wide = pl.BlockSpec((q, gw), lambda gi, i: (chunk_of(i), gi))
    extra_specs, extra_args, aliases = [], [], {}
    if dsk is not None:
        extra_specs.append(pl.BlockSpec((1, gw), lambda gi, i: (0, gi)))
        extra_args.append(dsk)
    if prev is not None:
        first = 10 + len(extra_args)
        extra_specs += [wide, gn, gn]
        extra_args += list(prev)
        aliases = {first: 0, first + 1: 1, first + 2: 2}
    return pl.pallas_call(
        body, name=f"ssd_bwd_{d}", grid=(s.g, s.ns),
        in_specs=s.in_specs(chunk_of) + [
            pl.BlockSpec((1, 1, HEAD_COLS), lambda gi, i: (s.d * s.g + gi, 0, 0)),
            pl.BlockSpec((q, gw), lambda gi, i: (jnp.minimum(chunk_of(i), s.nl - 1), gi)),
            pl.BlockSpec((1, 1, gw, n), lambda gi, i: (s.ns - 1 - i, gi, 0, 0))] + extra_specs,
        out_specs=[wide, gn, gn, col, col],
        out_shape=[jax.ShapeDtypeStruct((s.t_all, di), F32), jax.ShapeDtypeStruct((s.t_all, s.g * n), F32),
                   jax.ShapeDtypeStruct((s.t_all, s.g * n), F32), jax.ShapeDtypeStruct((s.g, s.t_all, HEAD_COLS), F32),
                   jax.ShapeDtypeStruct((s.g, s.t_all, HEAD_COLS), F32)],
        scratch_shapes=[pltpu.VMEM((gw, n), F32)],
        input_output_aliases=aliases, compiler_params=_cp(("parallel", "arbitrary")),
    )(xbc, xbc, xbc, cols, cs_t, e_tiles, e_heads, a_cols, dy, hprev, *extra_args)


def _ada_fwd(crows, w, b):
    r, d = crows.shape
    ws = w.shape[1]
    tn = _tile(ws, 512, LANES)

    def body(c_ref, w_ref, b_ref, m_ref, s_ref):
        s = _silu(c_ref[...])
        s_ref[...] = s
        m_ref[...] = _dot(s.astype(MXU_DTYPE), w_ref[...].astype(MXU_DTYPE)) + b_ref[...]

    full = pl.BlockSpec((r, d), lambda j: (0, 0))
    return pl.pallas_call(
        body, name="ada_fwd", grid=(ws // tn,),
        in_specs=[full, pl.BlockSpec((d, tn), lambda j: (0, j)), pl.BlockSpec((1, tn), lambda j: (0, j))],
        out_specs=[pl.BlockSpec((r, tn), lambda j: (0, j)), full],
        out_shape=[jax.ShapeDtypeStruct((r, ws), F32), jax.ShapeDtypeStruct((r, d), F32)],
        compiler_params=_cp(("arbitrary",)),
    )(crows, w, b)


def _ada_bwd(s_t, w, dm):
    d, r = s_t.shape
    ws = w.shape[1]
    tn = _tile(ws, 512, LANES)

    def body(st_ref, w_ref, dm_ref, dw_ref, ds_ref):
        dmb = dm_ref[...].astype(MXU_DTYPE)
        dw_ref[...] = _dot(st_ref[...].astype(MXU_DTYPE), dmb)
        _acc(ds_ref, _dot(dmb, w_ref[...].astype(MXU_DTYPE), _NT), pl.program_id(0))

    return pl.pallas_call(
        body, name="ada_bwd", grid=(ws // tn,),
        in_specs=[pl.BlockSpec((d, r), lambda j: (0, 0)), pl.BlockSpec((d, tn), lambda j: (0, j)),
                  pl.BlockSpec((r, tn), lambda j: (0, j))],
        out_specs=[pl.BlockSpec((d, tn), lambda j: (0, j)), pl.BlockSpec((r, d), lambda j: (0, 0))],
        out_shape=[jax.ShapeDtypeStruct((d, ws), F32), jax.ShapeDtypeStruct((r, d), F32)],
        compiler_params=_cp(("arbitrary",)),
    )(s_t, w, dm)


def _adamw(w, g, m, v, name):
    r, c = w.shape
    t = _tile(r, max(8, 300_000 // c), 8)

    def body(w_ref, g_ref, m_ref, v_ref, d_ref, m2_ref, v2_ref):
        g = g_ref[...]
        m2 = ADAM_B1 * m_ref[...] + (1.0 - ADAM_B1) * g
        v2 = ADAM_B2 * v_ref[...] + (1.0 - ADAM_B2) * (g * g)
        m_hat = m2 / (1.0 - ADAM_B1 ** ADAM_STEP)
        v_hat = v2 / (1.0 - ADAM_B2 ** ADAM_STEP)
        d_ref[...] = -ADAM_LR * (m_hat / (jnp.sqrt(v_hat) + ADAM_EPS) + ADAM_WD * w_ref[...])
        m2_ref[...] = m2
        v2_ref[...] = v2

    blk = pl.BlockSpec((t, c), lambda i: (i, 0))
    shape = jax.ShapeDtypeStruct((r, c), F32)
    return pl.pallas_call(
        body, name=name, grid=(r // t,), in_specs=[blk] * 4, out_specs=[blk] * 3, out_shape=[shape] * 3,
        compiler_params=_cp(("parallel",)),
    )(w, g, m, v)


def _sum_devices(gathered):
    rows, w = gathered.shape
    per = rows // N_DEV

    def body(g_ref, o_ref):
        total = g_ref[pl.ds(0, per), :]
        for dev in range(1, N_DEV):
            total = total + g_ref[pl.ds(dev * per, per), :]
        o_ref[...] = total

    return pl.pallas_call(
        body, name="sum_devices", out_shape=jax.ShapeDtypeStruct((per, w), F32),
        in_specs=[pl.BlockSpec(memory_space=pltpu.VMEM)], out_specs=pl.BlockSpec(memory_space=pltpu.VMEM),
        compiler_params=_cp(),
    )(gathered)


def _c_ctx_grad(parts, c_ctx):
    rows, d = parts.shape
    per = rows // N_DEV

    def body(p_ref, c_ref, o_ref):
        total = p_ref[pl.ds(0, 1), :]
        for chip in range(1, N_SHARD):
            total = total + p_ref[pl.ds(2 * chip * per, 1), :]
        _, vjp = jax.vjp(_silu, c_ref[...])
        o_ref[...] = vjp(total)[0]

    return pl.pallas_call(
        body, name="c_ctx_grad", out_shape=jax.ShapeDtypeStruct((1, d), F32),
        in_specs=[pl.BlockSpec(memory_space=pltpu.VMEM)] * 2, out_specs=pl.BlockSpec(memory_space=pltpu.VMEM),
        compiler_params=_cp(),
    )(parts, c_ctx)


def _pad_rows(a, rows, width):
    return jnp.pad(a, ((0, rows - a.shape[0]), (0, width - a.shape[1])))


def _pack(vectors, quantum):
    flat = jnp.concatenate([v.reshape(-1) for v in vectors])
    return jnp.pad(flat, (0, -flat.shape[0] % quantum))


def kernel(x, c, ctx, c_ctx, w_mod, b_mod, norm_mix, w_in, ssm_conv_w, ssm_conv_b, dt_bias, a_log, d_skip, ssm_norm, cf_conv_w, cf_conv_b, cf_ln_g, cf_ln_b, w_proj_a, w_proj_b, w_out, norm_ffn, w_ffn_gate, w_ffn_up, w_ffn_down, norm_final, loss_target, m_c_ctx, m_w_mod, m_b_mod, m_norm_mix, m_w_in, m_ssm_conv_w, m_ssm_conv_b, m_dt_bias, m_a_log, m_d_skip, m_ssm_norm, m_cf_conv_w, m_cf_conv_b, m_cf_ln_g, m_cf_ln_b, m_w_proj_a, m_w_proj_b, m_w_out, m_norm_ffn, m_w_ffn_gate, m_w_ffn_up, m_w_ffn_down, m_norm_final, v_c_ctx, v_w_mod, v_b_mod, v_norm_mix, v_w_in, v_ssm_conv_w, v_ssm_conv_b, v_dt_bias, v_a_log, v_d_skip, v_ssm_norm, v_cf_conv_w, v_cf_conv_b, v_cf_ln_g, v_cf_ln_b, v_w_proj_a, v_w_proj_b, v_w_out, v_norm_ffn, v_w_ffn_gate, v_w_ffn_up, v_w_ffn_down, v_norm_final):
    l, d = x.shape[1], x.shape[2]
    lc = ctx.shape[1]
    t_all = l + lc
    di = ssm_norm.shape[-1]
    h = d_skip.shape[-1]
    p = di // h
    g, n = SSM_GROUPS, SSM_STATE
    hpg = h // g
    conv_dim = di + 2 * g * n
    df = w_ffn_down.shape[1] * N_SHARD
    assert 2 * h == LANES and d % (2 * LANES) == 0

    my_x, my_y, my_c = _mesh_pos()
    chip = 2 * my_x + my_y
    dev = 2 * chip + my_c

    x2, ctx2, tgt = x[0], ctx[0], loss_target[0]
    row = lambda a: a.reshape(1, -1)

    cw_shard, cfw_shard = ssm_conv_w[0], cf_conv_w[0]
    k5, k31 = cw_shard.shape[0], cfw_shard.shape[0]
    r5, r31 = -(-k5 // 8) * 8, -(-k31 // 8) * 8
    wp = max(d, cw_shard.shape[1], cfw_shard.shape[1])
    packed = jnp.concatenate([_pad_rows(c, 8, wp), _pad_rows(cw_shard, r5, wp), _pad_rows(cfw_shard, r31, wp)], axis=0)
    got = _allgather_small(packed, "ag_params").reshape(N_DEV, 8 + r5 + r31, wp)
    c_all = got[:, 0, :d]
    conv_w = got[0::2, 8:8 + k5, :cw_shard.shape[1]].transpose(1, 0, 2).reshape(k5, conv_dim)
    cf_w = got[0::2, 8 + r5:8 + r5 + k31, :cfw_shard.shape[1]].transpose(1, 0, 2).reshape(k31, d)

    ws = w_mod.shape[2]
    crows = jnp.concatenate([c_all, row(c_ctx), jnp.zeros((7, d), F32)], axis=0)
    b_mod_mine = lax.dynamic_slice(b_mod, (0, chip * ws), (1, ws))
    m_part, s_rows = _ada_fwd(crows, w_mod[0], b_mod_mine)
    m_full = _allgather_small(m_part, "ag_mod").reshape(N_DEV, 16, ws)[0::2].transpose(1, 0, 2).reshape(16, N_SHARD * ws)
    m_lat = lax.dynamic_slice(m_full, (dev, 0), (1, 6 * d))
    sh1, sc1, g1, sh2, sc2, g2 = [m_lat[:, i * d:(i + 1) * d] for i in range(6)]
    csh1, csc1 = m_full[8:9, 0:d], m_full[8:9, d:2 * d]

    gate_up = jnp.concatenate([w_ffn_gate[0].T, w_ffn_up[0].T], axis=0)
    shards = [w_in[0].T, gate_up, w_proj_a[0], w_proj_b[0], w_out[0], w_ffn_down[0]]
    shards = [s.astype(WIRE_DTYPE) for s in shards]
    (win_got,) = _run_side(_gather_side(shards[:1]), "ag_w_in")
    mine = (jnp.arange(N_SHARD) == chip)[:, None, None]
    win_t = jnp.where(mine, shards[0][None], win_got.reshape(N_SHARD, -1, d)).reshape(win_got.shape)
    o_xbc, o_dt, o_glu, o_gates = di, di + conv_dim, di + conv_dim + 2 * h, di + conv_dim + 2 * h + 2 * d
    win_work = jnp.concatenate([win_t[:o_xbc], win_t[o_glu:], win_t[o_xbc:o_dt], win_t[o_dt:o_glu]], axis=0)
    c_u, c_ga, c_xbc, c_dt = di, di + 2 * d, di + 4 * d, di + 4 * d + conv_dim

    nm = norm_mix
    hx = _mod_fwd(x2, ctx2, nm, sc1, sh1, csc1, csh1)
    proj, *rest = _matmul(hx, win_work.T, tm=384, tn=29 * LANES, n_outer=True, name="mm_proj", side=_gather_side(shards[1:]))
    wgu, wpa, wpb, wout, wdn = _fill_own_rows(rest, shards[1:])
    xbc = _conv5_fwd(proj, conv_w, ssm_conv_b, l, lc, c_xbc, conv_dim)
    a = -jnp.exp(a_log.reshape(1, 2 * h))
    dt, cs, tc, cs_t = _dt_fwd(proj, dt_bias.reshape(1, 2 * h), a, c_dt // LANES)
    cols = _scan_columns(dt, cs, tc, 2 * g, hpg)
    a_cols = jnp.pad(a.reshape(2 * g, 1, hpg), ((0, 0), (0, 0), (0, HEAD_COLS - hpg)))
    y_f, hp_f = _ssd_fwd(xbc, cols, cs_t, l, lc, di, p, False)
    y_r, hp_r = _ssd_fwd(xbc, cols, cs_t, l, lc, di, p, True)
    dsk = jnp.repeat(d_skip.reshape(h), p).reshape(1, di)
    ya_in = _gate_fwd(y_f, y_r, xbc, proj, dsk, ssm_norm, l, di)
    y_a = _matmul(ya_in, wpa, tk=di, name="mm_ya")
    u_blk = c_u // d
    cv = _conv31_fwd(proj, cf_w, cf_conv_b, l, d, u_blk)
    cf = _ln_fwd(cv, cf_ln_g, cf_ln_b)
    y_b = _matmul(cf, wpb, name="mm_yb")
    ga_blk = c_ga // d
    merged = _merge_fwd(y_a, y_b, proj, ga_blk)
    mix = _matmul(merged, wout, name="mm_mix")
    x1, hx2 = _res_fwd(x2, mix, g1, norm_ffn, sc2, sh2)
    gu = _matmul(hx2, wgu, tb=True, tm=1024, tn=1024, name="mm_gu")
    act = _swiglu_fwd(gu, df)
    dn = _matmul(act, wdn, tk=df, name="mm_dn")
    loss, dx1, ddn, dg2, d_norm_final = _loss_and_grads(x1, dn, g2, row(norm_final), tgt)

    dact = _matmul(ddn, wdn, tb=True, tm=1024, tn=_tile(df, 1024, LANES), name="mm_dact")
    dw_dn = _matmul(act, ddn, ta=True, tn=d, tk=1024, name="mm_dw_dn")
    dgu = _swiglu_bwd(gu, dact, df)
    dhx2 = _matmul(dgu, wgu, tm=1024, tn=1024, tk=_tile(2 * df, 2816, LANES), name="mm_dhx2")
    dw_gu = _matmul(dgu, hx2, ta=True, tm=1024, tn=d, tk=1024, name="mm_dw_gu")
    dx_res, dmix, dg1, d_norm_ffn, dsc2, dsh2 = _res_bwd(x2, mix, g1, norm_ffn, sc2, sh2, dx1, dhx2)
    dmerged = _matmul(dmix, wout, tb=True, name="mm_dmerged")
    dw_out = _matmul(merged, dmix, ta=True, tn=d, tk=1024, name="mm_dw_out")
    dya, dyb, dga, dgb = _merge_bwd(y_a, y_b, proj, ga_blk, dmerged, lc)
    dcf = _matmul(dyb, wpb, tb=True, name="mm_dcf")
    dw_pb = _matmul(cf, dyb, ta=True, tn=d, tk=1024, name="mm_dw_pb")
    dcv, d_ln_g, d_ln_b = _ln_bwd(cv, cf_ln_g, cf_ln_b, dcf)
    du, dv, d_cf_w, d_cf_b = _conv31_bwd(proj, cf_w, dcv, l, lc, d, u_blk)
    dya_in = _matmul(dya, wpa, tb=True, tn=_tile(di, 1024, LANES), name="mm_dya_in")
    dw_pa = _matmul(ya_in, dya, ta=True, tn=d, tk=1024, name="mm_dw_pa")
    dy, dz, ddsk, d_ssm_norm = _gate_bwd(y_f, y_r, xbc, proj, dsk, ssm_norm, dya_in, l, lc, di)
    dxs_f, db_f, dc_f, ddt_f, dda_f = _ssd_bwd(xbc, cols, cs_t, a_cols, dy, hp_f, l, lc, di, p, False, dsk=dsk)
    dxs, db, dc, ddt_r, dda_r = _ssd_bwd(xbc, cols, cs_t, a_cols, dy, hp_r, l, lc, di, p, True, prev=(dxs_f, db_f, dc_f))
    dxs_raw, dcw_x, dcb_x = _conv5_bwd(proj, conv_w, ssm_conv_b, [dxs], l, lc, c_xbc, 0, di)
    db_raw, dcw_b, dcb_b = _conv5_bwd(proj, conv_w, ssm_conv_b, [db], l, lc, c_xbc, di, g * n)
    dc_raw, dcw_c, dcb_c = _conv5_bwd(proj, conv_w, ssm_conv_b, [dc], l, lc, c_xbc, di + g * n, g * n)
    d_conv_w = jnp.concatenate([dcw_x, dcw_b, dcw_c], axis=1)
    d_conv_b = jnp.concatenate([dcb_x, dcb_b, dcb_c], axis=1)
    heads = lambda f, r: jnp.concatenate([t[:, :, :hpg].transpose(1, 0, 2).reshape(t_all, h) for t in (f, r)], axis=1)
    ddt_raw, d_dt_bias, dda_dt = _dt_bwd(proj, dt_bias.reshape(1, 2 * h), dt, heads(ddt_f, ddt_r), heads(dda_f, dda_r), c_dt // LANES)
    d_a_log = dda_dt * a
    dproj = jnp.concatenate([dz, du, dv, dga, dgb, dxs_raw, db_raw, dc_raw, ddt_raw], axis=1)
    dw_in_work = _matmul(dproj, hx, ta=True, tm=640, tn=d, tk=1408, name="mm_dw_in")
    dw_in_t = jnp.concatenate([dw_in_work[:c_u], dw_in_work[c_xbc:], dw_in_work[c_u:c_xbc]], axis=0)
    wire, own = _reduce_scatter_begin([dw_in_t, dw_gu, dw_pa, dw_pb, dw_out, dw_dn], "a")
    dhx, *recv = _matmul(dproj, win_work, tm=768, tn=1024, tk=_tile(win_work.shape[0], 4096, LANES), name="mm_dhx",
                         side=_scatter_side(wire))
    g_in_t, g_gu_t, g_pa, g_pb, g_out, g_dn = _reduce_scatter_end(own, recv, "a")
    g_gate_t, g_up_t = g_gu_t[:df // N_SHARD], g_gu_t[df // N_SHARD:]
    grad_x, d_norm_mix, dsc1, dsh1, dcsc1, dcsh1 = _mod_bwd(x2, ctx2, nm, sc1, sh1, csc1, csh1, dhx, dx_res)
    g_in, g_gate, g_up = g_in_t.T, g_gate_t.T, g_up_t.T

    zeros_d = jnp.zeros((1, d), F32)
    dm_lat = jnp.concatenate([dsh1, dsc1, dg1, dsh2, dsc2, dg2], axis=1)
    dm_ctx = jnp.concatenate([dcsh1, dcsc1] + [zeros_d] * 4, axis=1)
    d_d_skip = ddsk.reshape(h, p).sum(axis=1)
    replicated = [dm_lat + dm_ctx, d_norm_mix, d_conv_b, d_dt_bias, d_a_log, d_d_skip, d_ssm_norm, d_cf_b, d_ln_g, d_ln_b,
                  d_norm_ffn, d_norm_final]
    rep_w = [b_mod, norm_mix, ssm_conv_b, dt_bias, a_log, d_skip, ssm_norm, cf_conv_b, cf_ln_g, cf_ln_b, norm_ffn, norm_final]
    rep_m = [m_b_mod, m_norm_mix, m_ssm_conv_b, m_dt_bias, m_a_log, m_d_skip, m_ssm_norm, m_cf_conv_b, m_cf_ln_g, m_cf_ln_b,
             m_norm_ffn, m_norm_final]
    rep_v = [v_b_mod, v_norm_mix, v_ssm_conv_b, v_dt_bias, v_a_log, v_d_skip, v_ssm_norm, v_cf_conv_b, v_cf_ln_g, v_cf_ln_b,
             v_norm_ffn, v_norm_final]
    quantum = 8 * LANES
    rep_flat = _pack(replicated, quantum)
    n_rep = rep_flat.shape[0]
    summed_parts = [rep_flat, _pack([d_conv_w, d_cf_w, dm_ctx], quantum)]
    n_sum = n_rep + summed_parts[1].shape[0]
    everything = jnp.concatenate(summed_parts + [_pack([dm_lat], quantum)])
    gathered = _allgather_small(everything.reshape(8, -1), "ag_small_grads")
    w8 = gathered.shape[1]
    summed = _sum_devices(gathered).reshape(-1)
    dm_lat_all = gathered.reshape(N_DEV, 8 * w8)[:, n_sum:n_sum + 6 * d]
    off = n_rep
    g_conv_w_full = summed[off:off + k5 * conv_dim].reshape(k5, conv_dim)
    off += k5 * conv_dim
    g_cf_w_full = summed[off:off + k31 * d].reshape(k31, d)
    off += k31 * d
    dm_ctx_all = summed[off:off + 6 * d].reshape(1, 6 * d)
    g_conv_w = lax.dynamic_slice(g_conv_w_full, (0, chip * cw_shard.shape[1]), cw_shard.shape)
    g_cf_w = lax.dynamic_slice(g_cf_w_full, (0, chip * cfw_shard.shape[1]), cfw_shard.shape)

    dm_rows = jnp.concatenate([dm_lat_all, dm_ctx_all, jnp.zeros((7, 6 * d), F32)], axis=0)
    dm_mine = lax.dynamic_slice(dm_rows, (0, chip * ws), (16, ws))
    g_w_mod, ds_part = _ada_bwd(s_rows.T, w_mod[0], dm_mine)
    ds_all = _allgather_small(ds_part[8:16], "ag_c_ctx")
    g_c_ctx = _c_ctx_grad(ds_all, row(c_ctx))

    grads, deltas, new_ms, new_vs = {}, {}, {}, {}

    def update(name, w2, g2, m2, v2, shape):
        dl, mm, vv = _adamw(w2, g2, m2, v2, f"adamw_{name}")
        grads[name], deltas[name], new_ms[name], new_vs[name] = (t.reshape(shape) for t in (g2, dl, mm, vv))

    for name, w_, g_, m_, v_ in [
            ("w_mod", w_mod, g_w_mod, m_w_mod, v_w_mod), ("w_in", w_in, g_in, m_w_in, v_w_in),
            ("ssm_conv_w", ssm_conv_w, g_conv_w, m_ssm_conv_w, v_ssm_conv_w),
            ("cf_conv_w", cf_conv_w, g_cf_w, m_cf_conv_w, v_cf_conv_w),
            ("w_proj_a", w_proj_a, g_pa, m_w_proj_a, v_w_proj_a), ("w_proj_b", w_proj_b, g_pb, m_w_proj_b, v_w_proj_b),
            ("w_out", w_out, g_out, m_w_out, v_w_out), ("w_ffn_gate", w_ffn_gate, g_gate, m_w_ffn_gate, v_w_ffn_gate),
            ("w_ffn_up", w_ffn_up, g_up, m_w_ffn_up, v_w_ffn_up), ("w_ffn_down", w_ffn_down, g_dn, m_w_ffn_down, v_w_ffn_down)]:
        update(name, w_[0], g_, m_[0], v_[0], w_.shape)
    update("c_ctx", row(c_ctx), g_c_ctx, row(m_c_ctx), row(v_c_ctx), c_ctx.shape)

    rep_names = ["b_mod", "norm_mix", "ssm_conv_b", "dt_bias", "a_log", "d_skip", "ssm_norm", "cf_conv_b", "cf_ln_g", "cf_ln_b",
                 "norm_ffn", "norm_final"]
    as8 = lambda vs: _pack(vs, quantum).reshape(8, -1)
    g8 = summed[:n_rep].reshape(8, -1)
    d8, m8, v8 = _adamw(as8(rep_w), g8, as8(rep_m), as8(rep_v), "adamw_replicated")
    off = 0
    for name, w_ in zip(rep_names, rep_w):
        size = w_.size
        for store, packed8 in ((grads, g8), (deltas, d8), (new_ms, m8), (new_vs, v8)):
            store[name] = packed8.reshape(-1)[off:off + size].reshape(w_.shape)
        off += size

    order = ["c_ctx", "w_mod", "b_mod", "norm_mix", "w_in", "ssm_conv_w", "ssm_conv_b", "dt_bias", "a_log", "d_skip", "ssm_norm",
             "cf_conv_w", "cf_conv_b", "cf_ln_g", "cf_ln_b", "w_proj_a", "w_proj_b", "w_out", "norm_ffn", "w_ffn_gate", "w_ffn_up",
             "w_ffn_down", "norm_final"]
    total_loss = lax.psum(loss[0, 0], ("x", "y", "c"))
    return (total_loss, grad_x.reshape(x.shape), *[grads[k] for k in order], *[deltas[k] for k in order],
            *[new_ms[k] for k in order], *[new_vs[k] for k in order])
```
